```python
import jax, jax.numpy as jnp
from jax import lax
import numpy as np

D_MODEL = 2048
BATCH = 8
SEQ = 4096
DEPTH = 1

D_MIX = D_MODEL
HG_DK = 128
HG_DV = 128
HG_WIDTH = D_MIX // 2
HG_HEADS = HG_WIDTH // HG_DV
HG_QK = HG_HEADS * HG_DK
GDN_DK = 128
GDN_DV = 128
GDN_WIDTH = D_MIX - HG_WIDTH
GDN_HEADS = GDN_WIDTH // GDN_DV
GDN_QK = GDN_HEADS * GDN_DK
CONV_K = 4
CHUNK = 64
D_FF = 4 * D_MODEL
N_MOD = 6
EPS = 1e-6

HG_COLS = 2 * HG_QK + 2 * HG_WIDTH
GDN_CONV_CH = 2 * GDN_QK + GDN_WIDTH
GDN_COLS = GDN_CONV_CH + GDN_WIDTH + 2 * GDN_HEADS
IN_COLS = HG_COLS + GDN_COLS

kernel_name = "hybrid_hgrn2_gdn_parallel_heads_adaln"


def rmsnorm(x, w):
    x32 = x.astype(jnp.float32)
    y = x32 * lax.rsqrt(jnp.mean(x32 * x32, axis=-1, keepdims=True) + EPS)
    return (y * w.astype(jnp.float32)).astype(x.dtype)


def l2norm(x):
    return x * lax.rsqrt(jnp.sum(x * x, axis=-1, keepdims=True) + EPS)


def to_chunks(x):
    B, T, H, D = x.shape
    return x.reshape(B, T // CHUNK, CHUNK, H, D).transpose(1, 0, 3, 2, 4)


def from_chunks(x):
    N, B, H, C, D = x.shape
    return x.transpose(1, 0, 3, 2, 4).reshape(B, N * C, H, D)


def causal_conv(u, w):
    ch = u.shape[-1]
    return lax.conv_general_dilated(
        u, w[:, None, :].astype(u.dtype), window_strides=(1,), padding=[(CONV_K - 1, 0)],
        dimension_numbers=('NWC', 'WIO', 'NWC'), feature_group_count=ch)


def hgrn2_chunked(q, log_f, k, v):
    B, T, H, DK = q.shape
    DV = v.shape[-1]
    causal = jnp.tril(jnp.ones((CHUNK, CHUNK), dtype=bool))

    def step(S, xs):
        qc, lfc, kc, vc = xs
        b = jnp.cumsum(lfc, axis=-2)
        diff = b[:, :, :, None, :] - b[:, :, None, :, :]
        decay = jnp.exp(jnp.where(causal[None, None, :, :, None], diff, -jnp.inf))
        attn = jnp.einsum('bhtk,bhsk,bhtsk->bhts', qc, kc, decay)
        o = (jnp.einsum('bhts,bhsv->bhtv', attn, vc)
             + jnp.einsum('bhtk,bhkv->bhtv', qc * jnp.exp(b), S))
        b_last = b[:, :, -1:, :]
        S_new = (S * jnp.exp(b_last[:, :, 0, :, None])
                 + jnp.einsum('bhsk,bhsv->bhkv', kc * jnp.exp(b_last - b), vc))
        return S_new, o

    S0 = jnp.zeros((B, H, DK, DV), jnp.float32)
    _, o = lax.scan(step, S0, (to_chunks(q), to_chunks(log_f), to_chunks(k), to_chunks(v)))
    return from_chunks(o)


def gated_delta_chunked(q, k, v, log_a, beta):
    B, T, H, DK = q.shape
    DV = v.shape[-1]
    qc, kc, vc = to_chunks(q), to_chunks(k), to_chunks(v)
    g = jnp.cumsum(to_chunks(log_a[..., None])[..., 0], axis=-1)
    bc = to_chunks(beta[..., None])[..., 0]
    incl = jnp.tril(jnp.ones((CHUNK, CHUNK), dtype=bool))
    strict = jnp.tril(jnp.ones((CHUNK, CHUNK), dtype=bool), -1)
    gamma = jnp.exp(jnp.where(incl, g[..., :, None] - g[..., None, :], -jnp.inf))
    kk = jnp.einsum('nbhtk,nbhsk->nbhts', kc, kc)
    m = jnp.where(strict, bc[..., :, None] * kk * gamma, 0.0)
    a_mat = jnp.eye(CHUNK, dtype=jnp.float32) + m
    rhs = jnp.concatenate([vc * bc[..., None], kc * (bc * jnp.exp(g))[..., None]], axis=-1)
    sol = lax.linalg.triangular_solve(a_mat, rhs, left_side=True, lower=True, unit_diagonal=True)
    u, w = sol[..., :DV], sol[..., DV:]
    qk = jnp.einsum('nbhtk,nbhsk->nbhts', qc, kc) * gamma
    q_dec = qc * jnp.exp(g)[..., None]
    k_tail = kc * jnp.exp(g[..., -1:] - g)[..., None]
    tail = jnp.exp(g[..., -1])

    def step(S, xs):
        u_c, w_c, qk_c, qd_c, kt_c, tl_c = xs
        v_new = u_c - jnp.einsum('bhck,bhkv->bhcv', w_c, S)
        o = jnp.einsum('bhck,bhkv->bhcv', qd_c, S) + jnp.einsum('bhts,bhsv->bhtv', qk_c, v_new)
        S = S * tl_c[..., None, None] + jnp.einsum('bhck,bhcv->bhkv', kt_c, v_new)
        return S, o

    S0 = jnp.zeros((B, H, DK, DV), jnp.float32)
    _, o = lax.scan(step, S0, (u, w, qk, q_dec, k_tail, tail))
    return from_chunks(o)


def hgrn2_group(p, lb, norm_w):
    B, T, _ = p.shape
    dt = p.dtype
    p32 = p.astype(jnp.float32)
    q = p32[..., :HG_QK].reshape(B, T, HG_HEADS, HG_DK)
    f_logit = p32[..., HG_QK:2 * HG_QK].reshape(B, T, HG_HEADS, HG_DK)
    i_in = p32[..., 2 * HG_QK:2 * HG_QK + HG_WIDTH].reshape(B, T, HG_HEADS, HG_DV)
    g_out = p[..., 2 * HG_QK + HG_WIDTH:].reshape(B, T, HG_HEADS, HG_DV)
    f = lb + (1.0 - lb) * jax.nn.sigmoid(f_logit)
    o = hgrn2_chunked(q, jnp.log(f), 1.0 - f, i_in).astype(dt)
    o = rmsnorm(o, norm_w) * jax.nn.silu(g_out)
    return o.reshape(B, T, HG_WIDTH)


def gdn_group(p, conv_w, a_log, dt_bias, norm_w):
    B, T, _ = p.shape
    dt = p.dtype
    qkv = jax.nn.silu(causal_conv(p[..., :GDN_CONV_CH], conv_w)).astype(jnp.float32)
    q = l2norm(qkv[..., :GDN_QK].reshape(B, T, GDN_HEADS, GDN_DK)) * (GDN_DK ** -0.5)
    k = l2norm(qkv[..., GDN_QK:2 * GDN_QK].reshape(B, T, GDN_HEADS, GDN_DK))
    v = qkv[..., 2 * GDN_QK:].reshape(B, T, GDN_HEADS, GDN_DV)
    off = GDN_CONV_CH
    g_out = p[..., off:off + GDN_WIDTH].reshape(B, T, GDN_HEADS, GDN_DV)
    a = p[..., off + GDN_WIDTH:off + GDN_WIDTH + GDN_HEADS].astype(jnp.float32)
    b = p[..., off + GDN_WIDTH + GDN_HEADS:].astype(jnp.float32)
    log_a = -jnp.exp(a_log.astype(jnp.float32)) * jax.nn.softplus(a + dt_bias.astype(jnp.float32))
    beta = jax.nn.sigmoid(b)
    o = gated_delta_chunked(q, k, v, log_a, beta).astype(dt)
    o = rmsnorm(o, norm_w) * jax.nn.silu(g_out)
    return o.reshape(B, T, GDN_WIDTH)


def _fwd_setup_inputs(seed: int = 0) -> dict:
    key = jax.random.key(seed)
    ks = jax.random.split(key, 20)
    f32 = jnp.float32
    nrm = lambda k, s, sc: jax.random.normal(k, s, f32) * sc
    gain = lambda k, s: 1.0 + 0.05 * jax.random.normal(k, s, f32)
    dtv = jnp.exp(jax.random.uniform(ks[13], (DEPTH, GDN_HEADS), f32, np.log(1e-3), np.log(1e-1)))
    return {
        "x": nrm(ks[0], (BATCH, SEQ, D_MODEL), 1.0),
        "c": nrm(ks[1], (BATCH, D_MODEL), 1.0),
        "w_ada": nrm(ks[2], (DEPTH, D_MODEL, N_MOD * D_MODEL), 0.5 * D_MODEL ** -0.5),
        "b_ada": nrm(ks[3], (DEPTH, N_MOD * D_MODEL), 0.02),
        "pre_mix_norm": gain(ks[4], (DEPTH, D_MODEL)),
        "post_mix_norm": gain(ks[5], (DEPTH, D_MODEL)),
        "pre_ffn_norm": gain(ks[6], (DEPTH, D_MODEL)),
        "post_ffn_norm": gain(ks[7], (DEPTH, D_MODEL)),
        "w_in": nrm(ks[8], (DEPTH, D_MODEL, IN_COLS), D_MODEL ** -0.5),
        "hg_lb_logits": nrm(ks[9], (DEPTH + 1, HG_HEADS, HG_DK), 0.5),
        "hg_norm": gain(ks[10], (DEPTH, HG_DV)),
        "gdn_conv_w": nrm(ks[11], (DEPTH, CONV_K, GDN_CONV_CH), CONV_K ** -0.5),
        "gdn_a_log": jnp.log(jax.random.uniform(ks[12], (DEPTH, GDN_HEADS), f32, 1.0, 16.0)),
        "gdn_dt_bias": dtv + jnp.log(-jnp.expm1(-dtv)),
        "gdn_norm": gain(ks[14], (DEPTH, GDN_DV)),
        "w_out": nrm(ks[15], (DEPTH, D_MIX, D_MODEL), D_MIX ** -0.5),
        "w_ff1": nrm(ks[16], (DEPTH, D_MODEL, D_FF), D_MODEL ** -0.5),
        "w_ff2": nrm(ks[17], (DEPTH, D_FF, D_MODEL), D_FF ** -0.5),
    }


def _fwd_reference(x, c, w_ada, b_ada, pre_mix_norm, post_mix_norm, pre_ffn_norm, post_ffn_norm,
              w_in, hg_lb_logits, hg_norm, gdn_conv_w, gdn_a_log, gdn_dt_bias, gdn_norm,
              w_out, w_ff1, w_ff2):
    lb_all = jnp.cumsum(jax.nn.softmax(hg_lb_logits.astype(jnp.float32), axis=0), axis=0)
    c_act = jax.nn.silu(c)
    for l in range(DEPTH):
        mod = c_act @ w_ada[l] + b_ada[l]
        sh_m, sc_m, gt_m, sh_f, sc_f, gt_f = jnp.split(mod[:, None, :], N_MOD, axis=-1)
        h = rmsnorm(x, pre_mix_norm[l]) * (1.0 + sc_m) + sh_m
        proj = h @ w_in[l]
        o_hg = hgrn2_group(proj[..., :HG_COLS], lb_all[l], hg_norm[l])
        o_gdn = gdn_group(proj[..., HG_COLS:], gdn_conv_w[l], gdn_a_log[l], gdn_dt_bias[l], gdn_norm[l])
        y = jnp.concatenate([o_hg, o_gdn], axis=-1) @ w_out[l]
        x = x + gt_m * rmsnorm(y, post_mix_norm[l])
        h = rmsnorm(x, pre_ffn_norm[l]) * (1.0 + sc_f) + sh_f
        y = jnp.square(jax.nn.relu(h @ w_ff1[l])) @ w_ff2[l]
        x = x + gt_f * rmsnorm(y, post_ffn_norm[l])
    return x


import jax as _jax
import jax.numpy as _jnp

TWIN_FORMAT = 'train_step'
FWD_PARAMS = ['x', 'c', 'w_ada', 'b_ada', 'pre_mix_norm', 'post_mix_norm', 'pre_ffn_norm', 'post_ffn_norm', 'w_in', 'hg_lb_logits', 'hg_norm', 'gdn_conv_w', 'gdn_a_log', 'gdn_dt_bias', 'gdn_norm', 'w_out', 'w_ff1', 'w_ff2']
TWIN_WEIGHTS = ['w_ada', 'b_ada', 'pre_mix_norm', 'post_mix_norm', 'pre_ffn_norm', 'post_ffn_norm', 'w_in', 'hg_lb_logits', 'hg_norm', 'gdn_conv_w', 'gdn_a_log', 'gdn_dt_bias', 'gdn_norm', 'w_out', 'w_ff1', 'w_ff2']
TWIN_DIFF_INPUT = 'x'
TWIN_INPUTS = ['x', 'c', 'w_ada', 'b_ada', 'pre_mix_norm', 'post_mix_norm', 'pre_ffn_norm', 'post_ffn_norm', 'w_in', 'hg_lb_logits', 'hg_norm', 'gdn_conv_w', 'gdn_a_log', 'gdn_dt_bias', 'gdn_norm', 'w_out', 'w_ff1', 'w_ff2', 'loss_target', 'm_w_ada', 'm_b_ada', 'm_pre_mix_norm', 'm_post_mix_norm', 'm_pre_ffn_norm', 'm_post_ffn_norm', 'm_w_in', 'm_hg_lb_logits', 'm_hg_norm', 'm_gdn_conv_w', 'm_gdn_a_log', 'm_gdn_dt_bias', 'm_gdn_norm', 'm_w_out', 'm_w_ff1', 'm_w_ff2', 'v_w_ada', 'v_b_ada', 'v_pre_mix_norm', 'v_post_mix_norm', 'v_pre_ffn_norm', 'v_post_ffn_norm', 'v_w_in', 'v_hg_lb_logits', 'v_hg_norm', 'v_gdn_conv_w', 'v_gdn_a_log', 'v_gdn_dt_bias', 'v_gdn_norm', 'v_w_out', 'v_w_ff1', 'v_w_ff2']
TWIN_OUTPUTS = ['loss', 'grad_x', 'grad_w_ada', 'grad_b_ada', 'grad_pre_mix_norm', 'grad_post_mix_norm', 'grad_pre_ffn_norm', 'grad_post_ffn_norm', 'grad_w_in', 'grad_hg_lb_logits', 'grad_hg_norm', 'grad_gdn_conv_w', 'grad_gdn_a_log', 'grad_gdn_dt_bias', 'grad_gdn_norm', 'grad_w_out', 'grad_w_ff1', 'grad_w_ff2', 'delta_w_ada', 'delta_b_ada', 'delta_pre_mix_norm', 'delta_post_mix_norm', 'delta_pre_ffn_norm', 'delta_post_ffn_norm', 'delta_w_in', 'delta_hg_lb_logits', 'delta_hg_norm', 'delta_gdn_conv_w', 'delta_gdn_a_log', 'delta_gdn_dt_bias', 'delta_gdn_norm', 'delta_w_out', 'delta_w_ff1', 'delta_w_ff2', 'new_m_w_ada', 'new_m_b_ada', 'new_m_pre_mix_norm', 'new_m_post_mix_norm', 'new_m_pre_ffn_norm', 'new_m_post_ffn_norm', 'new_m_w_in', 'new_m_hg_lb_logits', 'new_m_hg_norm', 'new_m_gdn_conv_w', 'new_m_gdn_a_log', 'new_m_gdn_dt_bias', 'new_m_gdn_norm', 'new_m_w_out', 'new_m_w_ff1', 'new_m_w_ff2', 'new_v_w_ada', 'new_v_b_ada', 'new_v_pre_mix_norm', 'new_v_post_mix_norm', 'new_v_pre_ffn_norm', 'new_v_post_ffn_norm', 'new_v_w_in', 'new_v_hg_lb_logits', 'new_v_hg_norm', 'new_v_gdn_conv_w', 'new_v_gdn_a_log', 'new_v_gdn_dt_bias', 'new_v_gdn_norm', 'new_v_w_out', 'new_v_w_ff1', 'new_v_w_ff2']
TWIN_LEAF_KINDS = {'loss': 'loss', 'grad_x': 'grad_x', 'grad_w_ada': 'grad_w', 'grad_b_ada': 'grad_w', 'grad_pre_mix_norm': 'grad_w', 'grad_post_mix_norm': 'grad_w', 'grad_pre_ffn_norm': 'grad_w', 'grad_post_ffn_norm': 'grad_w', 'grad_w_in': 'grad_w', 'grad_hg_lb_logits': 'grad_w', 'grad_hg_norm': 'grad_w', 'grad_gdn_conv_w': 'grad_w', 'grad_gdn_a_log': 'grad_w', 'grad_gdn_dt_bias': 'grad_w', 'grad_gdn_norm': 'grad_w', 'grad_w_out': 'grad_w', 'grad_w_ff1': 'grad_w', 'grad_w_ff2': 'grad_w', 'delta_w_ada': 'delta_w', 'delta_b_ada': 'delta_w', 'delta_pre_mix_norm': 'delta_w', 'delta_post_mix_norm': 'delta_w', 'delta_pre_ffn_norm': 'delta_w', 'delta_post_ffn_norm': 'delta_w', 'delta_w_in': 'delta_w', 'delta_hg_lb_logits': 'delta_w', 'delta_hg_norm': 'delta_w', 'delta_gdn_conv_w': 'delta_w', 'delta_gdn_a_log': 'delta_w', 'delta_gdn_dt_bias': 'delta_w', 'delta_gdn_norm': 'delta_w', 'delta_w_out': 'delta_w', 'delta_w_ff1': 'delta_w', 'delta_w_ff2': 'delta_w', 'new_m_w_ada': 'new_m', 'new_m_b_ada': 'new_m', 'new_m_pre_mix_norm': 'new_m', 'new_m_post_mix_norm': 'new_m', 'new_m_pre_ffn_norm': 'new_m', 'new_m_post_ffn_norm': 'new_m', 'new_m_w_in': 'new_m', 'new_m_hg_lb_logits': 'new_m', 'new_m_hg_norm': 'new_m', 'new_m_gdn_conv_w': 'new_m', 'new_m_gdn_a_log': 'new_m', 'new_m_gdn_dt_bias': 'new_m', 'new_m_gdn_norm': 'new_m', 'new_m_w_out': 'new_m', 'new_m_w_ff1': 'new_m', 'new_m_w_ff2': 'new_m', 'new_v_w_ada': 'new_v', 'new_v_b_ada': 'new_v', 'new_v_pre_mix_norm': 'new_v', 'new_v_post_mix_norm': 'new_v', 'new_v_pre_ffn_norm': 'new_v', 'new_v_post_ffn_norm': 'new_v', 'new_v_w_in': 'new_v', 'new_v_hg_lb_logits': 'new_v', 'new_v_hg_norm': 'new_v', 'new_v_gdn_conv_w': 'new_v', 'new_v_gdn_a_log': 'new_v', 'new_v_gdn_dt_bias': 'new_v', 'new_v_gdn_norm': 'new_v', 'new_v_w_out': 'new_v', 'new_v_w_ff1': 'new_v', 'new_v_w_ff2': 'new_v'}


def _forward(args):
    return _fwd_reference(*[args[k] for k in FWD_PARAMS])


def _output_shape():
    def fwd():
        inp = _fwd_setup_inputs(0)
        return _fwd_reference(*[inp[k] for k in FWD_PARAMS])
    out = _jax.eval_shape(fwd)
    return out.shape, out.dtype

N_MICROBATCH = 1
ADAM_LR = 0.001
ADAM_B1 = 0.9
ADAM_B2 = 0.999
ADAM_EPS = 1e-08
ADAM_WD = 0.01
ADAM_STEP = 10
PER_EXAMPLE_BATCH_AXIS = {'x': 0, 'c': 0, 'loss_target': 0}
SHARED_INPUTS = []
_WEIGHT_DTYPES = {'w_ada': _jnp.float32, 'b_ada': _jnp.float32, 'pre_mix_norm': _jnp.float32, 'post_mix_norm': _jnp.float32, 'pre_ffn_norm': _jnp.float32, 'post_ffn_norm': _jnp.float32, 'w_in': _jnp.float32, 'hg_lb_logits': _jnp.float32, 'hg_norm': _jnp.float32, 'gdn_conv_w': _jnp.float32, 'gdn_a_log': _jnp.float32, 'gdn_dt_bias': _jnp.float32, 'gdn_norm': _jnp.float32, 'w_out': _jnp.float32, 'w_ff1': _jnp.float32, 'w_ff2': _jnp.float32}
MOMENT_SCALE = {'w_ada': 6.239603e-01, 'b_ada': 1.361854e+00, 'pre_mix_norm': 7.777822e-02, 'post_mix_norm': 1.645563e+00, 'pre_ffn_norm': 5.623270e-02, 'post_ffn_norm': 1.686692e+00, 'w_in': 4.532494e-02, 'hg_lb_logits': 2.302217e-02, 'hg_norm': 1.738965e-01, 'gdn_conv_w': 4.345360e-02, 'gdn_a_log': 2.044061e-01, 'gdn_dt_bias': 1.861958e-01, 'gdn_norm': 2.643777e-01, 'w_out': 6.357285e-02, 'w_ff1': 3.647038e-02, 'w_ff2': 1.433071e-01}


def _to_microbatches(a, axis):
    t = _jnp.moveaxis(a, axis, 0)
    t = t.reshape((N_MICROBATCH, t.shape[0] // N_MICROBATCH) + t.shape[1:])
    return _jnp.moveaxis(t, 1, axis + 1)


def setup_inputs(seed: int = 0) -> dict:
    inp = _fwd_setup_inputs(seed)
    key = _jax.random.fold_in(_jax.random.key(seed), 7919)
    shape, _ = _output_shape()
    out = dict(inp)
    out["loss_target"] = _jax.random.normal(_jax.random.fold_in(key, 0), shape, _jnp.float32)
    for i, name in enumerate(TWIN_WEIGHTS):
        w = inp[name].astype(_jnp.float32)
        if MOMENT_SCALE is None:
            s = _jnp.sqrt(_jnp.mean(_jnp.square(w)) + 1e-30)
        else:
            s = MOMENT_SCALE[name]
        km, kv = _jax.random.split(_jax.random.fold_in(key, i + 1))
        out[name] = w
        out["m_" + name] = s * _jax.random.normal(km, w.shape, _jnp.float32)
        out["v_" + name] = (s * s) * _jax.random.uniform(kv, w.shape, _jnp.float32, 0.5, 1.5)
    if N_MICROBATCH > 1:
        for name, axis in PER_EXAMPLE_BATCH_AXIS.items():
            out[name] = _to_microbatches(out[name], axis)
    return {'x': out['x'], 'c': out['c'], 'w_ada': out['w_ada'], 'b_ada': out['b_ada'], 'pre_mix_norm': out['pre_mix_norm'], 'post_mix_norm': out['post_mix_norm'], 'pre_ffn_norm': out['pre_ffn_norm'], 'post_ffn_norm': out['post_ffn_norm'], 'w_in': out['w_in'], 'hg_lb_logits': out['hg_lb_logits'], 'hg_norm': out['hg_norm'], 'gdn_conv_w': out['gdn_conv_w'], 'gdn_a_log': out['gdn_a_log'], 'gdn_dt_bias': out['gdn_dt_bias'], 'gdn_norm': out['gdn_norm'], 'w_out': out['w_out'], 'w_ff1': out['w_ff1'], 'w_ff2': out['w_ff2'], 'loss_target': out['loss_target'], 'm_w_ada': out['m_w_ada'], 'm_b_ada': out['m_b_ada'], 'm_pre_mix_norm': out['m_pre_mix_norm'], 'm_post_mix_norm': out['m_post_mix_norm'], 'm_pre_ffn_norm': out['m_pre_ffn_norm'], 'm_post_ffn_norm': out['m_post_ffn_norm'], 'm_w_in': out['m_w_in'], 'm_hg_lb_logits': out['m_hg_lb_logits'], 'm_hg_norm': out['m_hg_norm'], 'm_gdn_conv_w': out['m_gdn_conv_w'], 'm_gdn_a_log': out['m_gdn_a_log'], 'm_gdn_dt_bias': out['m_gdn_dt_bias'], 'm_gdn_norm': out['m_gdn_norm'], 'm_w_out': out['m_w_out'], 'm_w_ff1': out['m_w_ff1'], 'm_w_ff2': out['m_w_ff2'], 'v_w_ada': out['v_w_ada'], 'v_b_ada': out['v_b_ada'], 'v_pre_mix_norm': out['v_pre_mix_norm'], 'v_post_mix_norm': out['v_post_mix_norm'], 'v_pre_ffn_norm': out['v_pre_ffn_norm'], 'v_post_ffn_norm': out['v_post_ffn_norm'], 'v_w_in': out['v_w_in'], 'v_hg_lb_logits': out['v_hg_lb_logits'], 'v_hg_norm': out['v_hg_norm'], 'v_gdn_conv_w': out['v_gdn_conv_w'], 'v_gdn_a_log': out['v_gdn_a_log'], 'v_gdn_dt_bias': out['v_gdn_dt_bias'], 'v_gdn_norm': out['v_gdn_norm'], 'v_w_out': out['v_w_out'], 'v_w_ff1': out['v_w_ff1'], 'v_w_ff2': out['v_w_ff2']}


def _loss(weights, diff, rest, loss_target):
    with _jax.named_scope("forward"):
        args = {**rest, TWIN_DIFF_INPUT: diff, **{k: w.astype(_WEIGHT_DTYPES[k]) for k, w in weights.items()}}
        y = _forward(args)
    with _jax.named_scope("loss_head"):
        err = _jnp.square(y.astype(_jnp.float32) - loss_target)
        return 0.5 * _jnp.sum(_jnp.mean(err, axis=-1)) if err.ndim else 0.5 * err


def _adamw(w, g, m, v):
    m = ADAM_B1 * m + (1.0 - ADAM_B1) * g
    v = ADAM_B2 * v + (1.0 - ADAM_B2) * _jnp.square(g)
    m_hat = m / (1.0 - ADAM_B1 ** ADAM_STEP)
    v_hat = v / (1.0 - ADAM_B2 ** ADAM_STEP)
    delta = -ADAM_LR * (m_hat / (_jnp.sqrt(v_hat) + ADAM_EPS) + ADAM_WD * w)
    return delta, m, v


def reference(x, c, w_ada, b_ada, pre_mix_norm, post_mix_norm, pre_ffn_norm, post_ffn_norm, w_in, hg_lb_logits, hg_norm, gdn_conv_w, gdn_a_log, gdn_dt_bias, gdn_norm, w_out, w_ff1, w_ff2, loss_target, m_w_ada, m_b_ada, m_pre_mix_norm, m_post_mix_norm, m_pre_ffn_norm, m_post_ffn_norm, m_w_in, m_hg_lb_logits, m_hg_norm, m_gdn_conv_w, m_gdn_a_log, m_gdn_dt_bias, m_gdn_norm, m_w_out, m_w_ff1, m_w_ff2, v_w_ada, v_b_ada, v_pre_mix_norm, v_post_mix_norm, v_pre_ffn_norm, v_post_ffn_norm, v_w_in, v_hg_lb_logits, v_hg_norm, v_gdn_conv_w, v_gdn_a_log, v_gdn_dt_bias, v_gdn_norm, v_w_out, v_w_ff1, v_w_ff2):
    given = dict(x=x, c=c, w_ada=w_ada, b_ada=b_ada, pre_mix_norm=pre_mix_norm, post_mix_norm=post_mix_norm, pre_ffn_norm=pre_ffn_norm, post_ffn_norm=post_ffn_norm, w_in=w_in, hg_lb_logits=hg_lb_logits, hg_norm=hg_norm, gdn_conv_w=gdn_conv_w, gdn_a_log=gdn_a_log, gdn_dt_bias=gdn_dt_bias, gdn_norm=gdn_norm, w_out=w_out, w_ff1=w_ff1, w_ff2=w_ff2, loss_target=loss_target, m_w_ada=m_w_ada, m_b_ada=m_b_ada, m_pre_mix_norm=m_pre_mix_norm, m_post_mix_norm=m_post_mix_norm, m_pre_ffn_norm=m_pre_ffn_norm, m_post_ffn_norm=m_post_ffn_norm, m_w_in=m_w_in, m_hg_lb_logits=m_hg_lb_logits, m_hg_norm=m_hg_norm, m_gdn_conv_w=m_gdn_conv_w, m_gdn_a_log=m_gdn_a_log, m_gdn_dt_bias=m_gdn_dt_bias, m_gdn_norm=m_gdn_norm, m_w_out=m_w_out, m_w_ff1=m_w_ff1, m_w_ff2=m_w_ff2, v_w_ada=v_w_ada, v_b_ada=v_b_ada, v_pre_mix_norm=v_pre_mix_norm, v_post_mix_norm=v_post_mix_norm, v_pre_ffn_norm=v_pre_ffn_norm, v_post_ffn_norm=v_post_ffn_norm, v_w_in=v_w_in, v_hg_lb_logits=v_hg_lb_logits, v_hg_norm=v_hg_norm, v_gdn_conv_w=v_gdn_conv_w, v_gdn_a_log=v_gdn_a_log, v_gdn_dt_bias=v_gdn_dt_bias, v_gdn_norm=v_gdn_norm, v_w_out=v_w_out, v_w_ff1=v_w_ff1, v_w_ff2=v_w_ff2)
    weights = {n: given[n] for n in TWIN_WEIGHTS}
    shared = {n: given[n] for n in SHARED_INPUTS}
    per_example = {n: given[n] for n in ['x', 'c']}
    grad_fn = _jax.value_and_grad(_loss, argnums=(0, 1))

    def one_microbatch(ex, loss_target):
        ex = dict(ex)
        diff = ex.pop(TWIN_DIFF_INPUT)
        return grad_fn(weights, diff, {**shared, **ex}, loss_target)

    if N_MICROBATCH == 1:
        loss, (grad_w, grad_x) = one_microbatch(per_example, given["loss_target"])
    else:
        def body(carry, xs):
            loss_sum, grad_sum = carry
            l_k, (gw_k, gx_k) = one_microbatch(xs[0], xs[1])
            with _jax.named_scope("update"):
                return (loss_sum + l_k, _jax.tree.map(_jnp.add, grad_sum, gw_k)), gx_k

        init = (_jnp.zeros((), _jnp.float32), _jax.tree.map(_jnp.zeros_like, weights))
        (loss, grad_w), grad_x = _jax.lax.scan(body, init, (per_example, given["loss_target"]))
    with _jax.named_scope("update"):
        delta_w, new_m, new_v = {}, {}, {}
        for n in TWIN_WEIGHTS:
            delta_w[n], new_m[n], new_v[n] = _adamw(weights[n], grad_w[n], given["m_" + n], given["v_" + n])
    return (loss, grad_x, *[grad_w[n] for n in TWIN_WEIGHTS], *[delta_w[n] for n in TWIN_WEIGHTS],
            *[new_m[n] for n in TWIN_WEIGHTS], *[new_v[n] for n in TWIN_WEIGHTS])
```

```python
import functools

import jax
import jax.numpy as jnp
from jax import lax
from jax.experimental import pallas as pl
from jax.experimental.pallas import tpu as pltpu

F32 = jnp.float32
BF16 = jnp.bfloat16
HI = lax.Precision.HIGHEST

EPS = 1e-6
CHUNK = 64
HD = 128
CONV_K = 4
N_DEV = 8
LANES = 128
VMEM_LIMIT = 56 * 1024 * 1024

ADAM_LR = 0.001
ADAM_B1 = 0.9
ADAM_B2 = 0.999
ADAM_EPS = 1e-08
ADAM_WD = 0.01
ADAM_STEP = 10

ANY = pl.BlockSpec(memory_space=pl.ANY)
MESH = pl.DeviceIdType.MESH


def _cp(sem=None):
    return pltpu.CompilerParams(dimension_semantics=sem, vmem_limit_bytes=VMEM_LIMIT)


def _dot(a, b, dims, precision=None):
    return lax.dot_general(a, b, (dims, ((), ())), precision=precision, preferred_element_type=F32)


def _nn(a, b, precision=None):
    return _dot(a, b, ((1,), (0,)), precision)


def _nt(a, b, precision=None):
    return _dot(a, b, ((1,), (1,)), precision)


def _tn(a, b, precision=None):
    return _dot(a, b, ((0,), (0,)), precision)


def _bf(x):
    return x.astype(BF16)


def _sigmoid(x):
    return 1.0 / (1.0 + jnp.exp(-x))


def _pick(n, pref):
    if n <= pref:
        return n
    t = pref
    while n % t:
        t -= LANES
    assert t > 0, (n, pref)
    return t


def _mm(a, b, mode, out_dtypes, name, epilogue=None, extras=(), tm=512, tn=1024, tk=512):
    if mode == "nn":
        (m, kd), (_, n) = a.shape, b.shape
    elif mode == "nt":
        (m, kd), (n, _) = a.shape, b.shape
    else:
        (kd, m), (_, n) = a.shape, b.shape
    tm, tn, tk = _pick(m, tm), _pick(n, tn), _pick(kd, tk)
    nk = kd // tk
    if mode == "nn":
        a_spec = pl.BlockSpec((tm, tk), lambda i, j, k: (i, k))
        b_spec = pl.BlockSpec((tk, tn), lambda i, j, k: (k, j))
        dims = ((1,), (0,))
    elif mode == "nt":
        a_spec = pl.BlockSpec((tm, tk), lambda i, j, k: (i, k))
        b_spec = pl.BlockSpec((tn, tk), lambda i, j, k: (j, k))
        dims = ((1,), (1,))
    else:
        a_spec = pl.BlockSpec((tk, tm), lambda i, j, k: (k, i))
        b_spec = pl.BlockSpec((tk, tn), lambda i, j, k: (k, j))
        dims = ((0,), (0,))
    o_spec = pl.BlockSpec((tm, tn), lambda i, j, k: (i, j))
    n_extra, n_out = len(extras), len(out_dtypes)

    def body(a_ref, b_ref, *rest):
        extra_refs = rest[:n_extra]
        out_refs = rest[n_extra:n_extra + n_out]
        acc = rest[-1]
        k = pl.program_id(2)

        @pl.when(k == 0)
        def _():
            acc[...] = jnp.zeros_like(acc)

        acc[...] += _dot(a_ref[...], b_ref[...], dims)

        @pl.when(k == nk - 1)
        def _():
            if epilogue is None:
                out_refs[0][...] = acc[...].astype(out_dtypes[0])
            else:
                epilogue(acc[...], extra_refs, out_refs)

    outs = pl.pallas_call(
        body, name=name,
        grid=(m // tm, n // tn, nk),
        in_specs=[a_spec, b_spec] + [o_spec] * n_extra,
        out_specs=[o_spec] * n_out,
        out_shape=[jax.ShapeDtypeStruct((m, n), dt) for dt in out_dtypes],
        scratch_shapes=[pltpu.VMEM((tm, tn), F32)],
        compiler_params=_cp(("parallel", "parallel", "arbitrary")),
    )(a, b, *extras)
    return outs[0] if n_out == 1 else outs


def _row_spec(tb, d):
    return pl.BlockSpec((tb, d), lambda i: (i, 0))


def _vec_spec(d):
    return pl.BlockSpec((1, d), lambda i: (0, 0))


def _prenorm(x, w, sc, sh, name):
    t, d = x.shape
    tb = _pick(t, 256)

    def body(x_ref, w_ref, sc_ref, sh_ref, h_ref, r_ref):
        xv = x_ref[...]
        r = lax.rsqrt(jnp.mean(xv * xv, axis=-1, keepdims=True) + EPS)
        h_ref[...] = ((xv * r * w_ref[...]) * (1.0 + sc_ref[...]) + sh_ref[...]).astype(BF16)
        r_ref[...] = r

    return pl.pallas_call(
        body, name=name, grid=(t // tb,),
        in_specs=[_row_spec(tb, d), _vec_spec(d), _vec_spec(d), _vec_spec(d)],
        out_specs=[_row_spec(tb, d), _row_spec(tb, 1)],
        out_shape=[jax.ShapeDtypeStruct((t, d), BF16), jax.ShapeDtypeStruct((t, 1), F32)],
        compiler_params=_cp(("parallel",)),
    )(x, w, sc, sh)


def _postnorm_res(x, y, w, gt, name):
    t, d = x.shape
    tb = _pick(t, 256)

    def body(x_ref, y_ref, w_ref, gt_ref, o_ref, r_ref):
        yv = y_ref[...]
        r = lax.rsqrt(jnp.mean(yv * yv, axis=-1, keepdims=True) + EPS)
        o_ref[...] = x_ref[...] + gt_ref[...] * (yv * r * w_ref[...])
        r_ref[...] = r

    return pl.pallas_call(
        body, name=name, grid=(t // tb,),
        in_specs=[_row_spec(tb, d), _row_spec(tb, d), _vec_spec(d), _vec_spec(d)],
        out_specs=[_row_spec(tb, d), _row_spec(tb, 1)],
        out_shape=[jax.ShapeDtypeStruct((t, d), F32), jax.ShapeDtypeStruct((t, 1), F32)],
        compiler_params=_cp(("parallel",)),
    )(x, y, w, gt)


def _final_loss(x, y, w, gt, tgt, name):
    t, d = x.shape
    tb = _pick(t, 256)

    def body(x_ref, y_ref, w_ref, gt_ref, tgt_ref, dout_ref, r_ref, loss_ref):
        @pl.when(pl.program_id(0) == 0)
        def _():
            loss_ref[...] = jnp.zeros_like(loss_ref)

        yv = y_ref[...]
        r = lax.rsqrt(jnp.mean(yv * yv, axis=-1, keepdims=True) + EPS)
        out = x_ref[...] + gt_ref[...] * (yv * r * w_ref[...])
        diff = out - tgt_ref[...]
        row = jnp.mean(diff * diff, axis=-1, keepdims=True)
        loss_ref[...] += 0.5 * jnp.sum(row, axis=0, keepdims=True)
        dout_ref[...] = diff * (1.0 / d)
        r_ref[...] = r

    return pl.pallas_call(
        body, name=name, grid=(t // tb,),
        in_specs=[_row_spec(tb, d), _row_spec(tb, d), _vec_spec(d), _vec_spec(d), _row_spec(tb, d)],
        out_specs=[_row_spec(tb, d), _row_spec(tb, 1), pl.BlockSpec((1, 1), lambda i: (0, 0))],
        out_shape=[jax.ShapeDtypeStruct((t, d), F32), jax.ShapeDtypeStruct((t, 1), F32),
                   jax.ShapeDtypeStruct((1, 1), F32)],
        compiler_params=_cp(("arbitrary",)),
    )(x, y, w, gt, tgt)


def _postnorm_bwd(dxn, y, r, w, gt, name):
    t, d = y.shape
    tb = _pick(t, 256)

    def body(dx_ref, y_ref, r_ref, w_ref, gt_ref, dy_ref, dgt_ref, dw_ref):
        @pl.when(pl.program_id(0) == 0)
        def _():
            dgt_ref[...] = jnp.zeros_like(dgt_ref)
            dw_ref[...] = jnp.zeros_like(dw_ref)

        dxv, rv, wv = dx_ref[...], r_ref[...], w_ref[...]
        z = y_ref[...] * rv
        dgt_ref[...] += jnp.sum(dxv * (z * wv), axis=0, keepdims=True)
        dn = dxv * gt_ref[...]
        dw_ref[...] += jnp.sum(dn * z, axis=0, keepdims=True)
        dz = dn * wv
        dy_ref[...] = (rv * (dz - z * jnp.mean(dz * z, axis=-1, keepdims=True))).astype(BF16)

    return pl.pallas_call(
        body, name=name, grid=(t // tb,),
        in_specs=[_row_spec(tb, d), _row_spec(tb, d), _row_spec(tb, 1), _vec_spec(d), _vec_spec(d)],
        out_specs=[_row_spec(tb, d), _vec_spec(d), _vec_spec(d)],
        out_shape=[jax.ShapeDtypeStruct((t, d), BF16), jax.ShapeDtypeStruct((1, d), F32),
                   jax.ShapeDtypeStruct((1, d), F32)],
        compiler_params=_cp(("arbitrary",)),
    )(dxn, y, r, w, gt)


def _prenorm_bwd(dh, x, r, w, sc, dres, name):
    t, d = x.shape
    tb = _pick(t, 256)

    def body(dh_ref, x_ref, r_ref, w_ref, sc_ref, dres_ref, dx_ref, dsh_ref, dsc_ref, dw_ref):
        @pl.when(pl.program_id(0) == 0)
        def _():
            dsh_ref[...] = jnp.zeros_like(dsh_ref)
            dsc_ref[...] = jnp.zeros_like(dsc_ref)
            dw_ref[...] = jnp.zeros_like(dw_ref)

        dhv, rv, wv = dh_ref[...], r_ref[...], w_ref[...]
        z = x_ref[...] * rv
        dsh_ref[...] += jnp.sum(dhv, axis=0, keepdims=True)
        dsc_ref[...] += jnp.sum(dhv * (z * wv), axis=0, keepdims=True)
        dzw = dhv * (1.0 + sc_ref[...])
        dw_ref[...] += jnp.sum(dzw * z, axis=0, keepdims=True)
        dz = dzw * wv
        dx_ref[...] = dres_ref[...] + rv * (dz - z * jnp.mean(dz * z, axis=-1, keepdims=True))

    return pl.pallas_call(
        body, name=name, grid=(t // tb,),
        in_specs=[_row_spec(tb, d), _row_spec(tb, d), _row_spec(tb, 1), _vec_spec(d), _vec_spec(d),
                  _row_spec(tb, d)],
        out_specs=[_row_spec(tb, d), _vec_spec(d), _vec_spec(d), _vec_spec(d)],
        out_shape=[jax.ShapeDtypeStruct((t, d), F32)] + [jax.ShapeDtypeStruct((1, d), F32)] * 3,
        compiler_params=_cp(("arbitrary",)),
    )(dh, x, r, w, sc, dres)


def _headnorm_fwd(o, proj, g_blk, nw, name):
    t, wd = o.shape
    nh = wd // HD
    tb = _pick(t, 512)
    gb = g_blk * HD // wd

    def body(o_ref, g_ref, nw_ref, out_ref):
        o3 = o_ref[...].reshape(tb, nh, HD)
        g3 = g_ref[...].reshape(tb, nh, HD)
        rh = lax.rsqrt(jnp.mean(o3 * o3, axis=-1, keepdims=True) + EPS)
        res = (o3 * rh * nw_ref[...].reshape(1, 1, HD)) * (g3 * _sigmoid(g3))
        out_ref[...] = res.reshape(tb, wd).astype(BF16)

    return pl.pallas_call(
        body, name=name, grid=(t // tb,),
        in_specs=[_row_spec(tb, wd), pl.BlockSpec((tb, wd), lambda i: (i, gb)), _vec_spec(HD)],
        out_specs=_row_spec(tb, wd),
        out_shape=jax.ShapeDtypeStruct((t, wd), BF16),
        compiler_params=_cp(("parallel",)),
    )(o, proj, nw)


def _headnorm_bwd(dom, col_blk, o, proj, g_blk, nw, name):
    t, wd = o.shape
    nh = wd // HD
    tb = _pick(t, 512)
    gb = g_blk * HD // wd

    def body(do_ref, o_ref, g_ref, nw_ref, dout_ref, dg_ref, dnw_ref):
        @pl.when(pl.program_id(0) == 0)
        def _():
            dnw_ref[...] = jnp.zeros_like(dnw_ref)

        dn = do_ref[...].reshape(tb, nh, HD)
        o3 = o_ref[...].reshape(tb, nh, HD)
        g3 = g_ref[...].reshape(tb, nh, HD)
        nw3 = nw_ref[...].reshape(1, 1, HD)
        rh = lax.rsqrt(jnp.mean(o3 * o3, axis=-1, keepdims=True) + EPS)
        z = o3 * rh
        sg = _sigmoid(g3)
        sl = g3 * sg
        dnw_ref[...] += jnp.sum(jnp.sum(dn * sl * z, axis=1), axis=0, keepdims=True)
        dg_ref[...] = (dn * (z * nw3) * (sg * (1.0 + g3 * (1.0 - sg)))).reshape(tb, wd).astype(BF16)
        dz = dn * sl * nw3
        dout_ref[...] = (rh * (dz - z * jnp.mean(dz * z, axis=-1, keepdims=True))).reshape(tb, wd)

    return pl.pallas_call(
        body, name=name, grid=(t // tb,),
        in_specs=[pl.BlockSpec((tb, wd), lambda i: (i, col_blk)), _row_spec(tb, wd),
                  pl.BlockSpec((tb, wd), lambda i: (i, gb)), _vec_spec(HD)],
        out_specs=[_row_spec(tb, wd), _row_spec(tb, wd), _vec_spec(HD)],
        out_shape=[jax.ShapeDtypeStruct((t, wd), F32), jax.ShapeDtypeStruct((t, wd), BF16),
                   jax.ShapeDtypeStruct((1, HD), F32)],
        compiler_params=_cp(("arbitrary",)),
    )(dom, o, proj, nw)


def _tri(n, kind):
    r = lax.broadcasted_iota(jnp.int32, (n, n), 0)
    c = lax.broadcasted_iota(jnp.int32, (n, n), 1)
    if kind == "lower":
        return r >= c
    if kind == "strict":
        return r > c
    return r <= c


def _hg_gate(fl, lg_ref):
    l0, l1 = lg_ref[0, 0], lg_ref[1, 0]
    mx = jnp.maximum(l0, l1)
    e0, e1 = jnp.exp(l0 - mx), jnp.exp(l1 - mx)
    lb = e0 / (e0 + e1)
    sg = _sigmoid(fl)
    f = lb + (1.0 - lb) * sg
    return lb, sg, f


def _hgrn2_fwd(proj, lb_logits, nh, name):
    t = proj.shape[0]
    nc = t // CHUNK
    C = CHUNK
    lg = lb_logits.reshape(2, nh, 1, HD)

    def body(q_ref, f_ref, i_ref, lg_ref, o_ref, a_ref, st_ref, s_sc, b_sc, k_sc):
        c, h = pl.program_id(0), pl.program_id(1)

        @pl.when(c == 0)
        def _():
            s_sc[h] = jnp.zeros((HD, HD), F32)

        q, v = q_ref[...], i_ref[...]
        _, _, f = _hg_gate(f_ref[...], lg_ref)
        k = 1.0 - f
        low = _tri(C, "lower")
        b = _nn(low.astype(F32), jnp.log(f), HI)
        b_sc[...] = b
        k_sc[...] = k
        lane = lax.broadcasted_iota(jnp.int32, (C, C), 1)

        def col(j, a):
            e = jnp.exp(jnp.minimum(b - b_sc[pl.ds(j, 1), :], 0.0))
            cj = jnp.sum(q * k_sc[pl.ds(j, 1), :] * e, axis=1, keepdims=True)
            return jnp.where(lane == j, cj, a)

        a = lax.fori_loop(0, C, col, jnp.zeros((C, C), F32), unroll=4)
        a = jnp.where(low, a, 0.0)
        st = s_sc[h]
        bl = b[C - 1:C, :]
        o_ref[...] = _nn(_bf(a), _bf(v)) + _nt(_bf(q * jnp.exp(b)), _bf(st))
        a_ref[0, 0] = a
        st_ref[0, 0] = st
        s_sc[h] = st * jnp.exp(bl) + _tn(_bf(v), _bf(k * jnp.exp(bl - b)))

    blk = lambda off: pl.BlockSpec((C, HD), lambda c, h: (c, off + h))
    return pl.pallas_call(
        body, name=name, grid=(nc, nh),
        in_specs=[blk(0), blk(nh), blk(2 * nh),
                  pl.BlockSpec((2, 1, 1, HD), lambda c, h: (0, h, 0, 0))],
        out_specs=[blk(0),
                   pl.BlockSpec((1, 1, C, C), lambda c, h: (c, h, 0, 0)),
                   pl.BlockSpec((1, 1, HD, HD), lambda c, h: (c, h, 0, 0))],
        out_shape=[jax.ShapeDtypeStruct((t, nh * HD), F32),
                   jax.ShapeDtypeStruct((nc, nh, C, C), F32),
                   jax.ShapeDtypeStruct((nc, nh, HD, HD), F32)],
        scratch_shapes=[pltpu.VMEM((nh, HD, HD), F32), pltpu.VMEM((C, HD), F32), pltpu.VMEM((C, HD), F32)],
        compiler_params=_cp(("arbitrary", "arbitrary")),
    )(proj, proj, proj, lg)


def _hgrn2_bwd(proj, lb_logits, do, a_sv, st_sv, nh, name):
    t = proj.shape[0]
    nc = t // CHUNK
    C = CHUNK
    lg = lb_logits.reshape(2, nh, 1, HD)

    def body(q_ref, f_ref, i_ref, lg_ref, do_ref, a_ref, st_ref,
             dq_ref, df_ref, di_ref, dl_ref, ds_sc, b_sc, k_sc, dk_sc):
        c, h = pl.program_id(0), pl.program_id(1)

        @pl.when(c == 0)
        def _():
            ds_sc[h] = jnp.zeros((HD, HD), F32)

        @pl.when((c == 0) & (h == 0))
        def _():
            dl_ref[...] = jnp.zeros_like(dl_ref)

        q, v, do_ = q_ref[...], i_ref[...], do_ref[...]
        lb, sg, f = _hg_gate(f_ref[...], lg_ref)
        k = 1.0 - f
        low = _tri(C, "lower")
        b = _nn(low.astype(F32), jnp.log(f), HI)
        bl = b[C - 1:C, :]
        eb, ekb = jnp.exp(b), jnp.exp(bl - b)
        qb, kb = q * eb, k * ekb
        a, st, dst = a_ref[0, 0], st_ref[0, 0], ds_sc[h]

        da = jnp.where(low, _nt(_bf(do_), _bf(v)), 0.0)
        dv = _tn(_bf(a), _bf(do_)) + _nt(_bf(kb), _bf(dst))
        dqb = _nn(_bf(do_), _bf(st))
        dkb = _nn(_bf(v), _bf(dst))

        b_sc[...] = b
        k_sc[...] = k
        lane = lax.broadcasted_iota(jnp.int32, (C, C), 1)
        row = lax.broadcasted_iota(jnp.int32, (C, HD), 0)

        def col(j, dqi):
            e = jnp.where(row >= j, jnp.exp(jnp.minimum(b - b_sc[pl.ds(j, 1), :], 0.0)), 0.0)
            dac = jnp.sum(jnp.where(lane == j, da, 0.0), axis=1, keepdims=True)
            de = dac * e
            dk_sc[pl.ds(j, 1), :] = jnp.sum(de * q, axis=0, keepdims=True)
            return dqi + de * k_sc[pl.ds(j, 1), :]

        dqi = lax.fori_loop(0, C, col, jnp.zeros((C, HD), F32), unroll=4)
        dq = dqi + dqb * eb
        dk_inter = dkb * ekb
        dk = dk_sc[...] + dk_inter
        db = q * dq - k * dk
        extra = (jnp.sum(k * dk_inter, axis=0, keepdims=True)
                 + jnp.exp(bl) * jnp.sum(dst * st, axis=0, keepdims=True))
        db = db + jnp.where(row == C - 1, extra, 0.0)
        dlf = _nn(_tri(C, "upper").astype(F32), db, HI)
        df = dlf / f - dk
        dq_ref[...] = dq.astype(BF16)
        df_ref[...] = (df * (1.0 - lb) * sg * (1.0 - sg)).astype(BF16)
        di_ref[...] = dv.astype(BF16)
        dl_ref[pl.ds(h, 1), :] += jnp.sum(df * (1.0 - sg), axis=0, keepdims=True) * (lb * (1.0 - lb))
        ds_sc[h] = dst * jnp.exp(bl) + _tn(_bf(do_), _bf(qb))

    rblk = lambda off: pl.BlockSpec((C, HD), lambda c, h: (nc - 1 - c, off + h))
    oblk = pl.BlockSpec((C, HD), lambda c, h: (nc - 1 - c, h))
    return pl.pallas_call(
        body, name=name, grid=(nc, nh),
        in_specs=[rblk(0), rblk(nh), rblk(2 * nh),
                  pl.BlockSpec((2, 1, 1, HD), lambda c, h: (0, h, 0, 0)),
                  oblk,
                  pl.BlockSpec((1, 1, C, C), lambda c, h: (nc - 1 - c, h, 0, 0)),
                  pl.BlockSpec((1, 1, HD, HD), lambda c, h: (nc - 1 - c, h, 0, 0))],
        out_specs=[oblk, oblk, oblk, pl.BlockSpec((nh, HD), lambda c, h: (0, 0))],
        out_shape=[jax.ShapeDtypeStruct((t, nh * HD), BF16)] * 3 + [jax.ShapeDtypeStruct((nh, HD), F32)],
        scratch_shapes=[pltpu.VMEM((nh, HD, HD), F32), pltpu.VMEM((C, HD), F32), pltpu.VMEM((C, HD), F32),
                        pltpu.VMEM((C, HD), F32)],
        compiler_params=_cp(("arbitrary", "arbitrary")),
    )(proj, proj, proj, lg, do, a_sv, st_sv)


def _shift_rows(u, d, row):
    t = u.shape[0]
    if d == 0:
        return u
    rolled = pltpu.roll(u, d % t, 0)
    if d > 0:
        return jnp.where(row >= d, rolled, 0.0)
    return jnp.where(row < t + d, rolled, 0.0)


def _gdn_prep(proj, conv_w, blk0, nh, name):
    t = proj.shape[0]
    scale = HD ** -0.5

    def body(u_ref, w_ref, o_ref):
        j = pl.program_id(0)
        u, w = u_ref[...], w_ref[...]
        row = lax.broadcasted_iota(jnp.int32, (t, HD), 0)
        y = w[CONV_K - 1:CONV_K, :] * u
        for d in range(1, CONV_K):
            y = y + w[CONV_K - 1 - d:CONV_K - d, :] * _shift_rows(u, d, row)
        a = y * _sigmoid(y)
        n = a * lax.rsqrt(jnp.sum(a * a, axis=-1, keepdims=True) + EPS)
        n = n * jnp.where(j < nh, scale, 1.0)
        o_ref[...] = jnp.where(j < 2 * nh, n, a)

    return pl.pallas_call(
        body, name=name, grid=(3 * nh,),
        in_specs=[pl.BlockSpec((t, HD), lambda j: (0, blk0 + j)), pl.BlockSpec((CONV_K, HD), lambda j: (0, j))],
        out_specs=pl.BlockSpec((t, HD), lambda j: (0, j)),
        out_shape=jax.ShapeDtypeStruct((t, 3 * nh * HD), F32),
        compiler_params=_cp(("parallel",)),
    )(proj, conv_w)


def _gdn_prep_bwd(proj, conv_w, dqkv, blk0, nh, name):
    t = proj.shape[0]
    scale = HD ** -0.5

    def body(u_ref, w_ref, d_ref, du_ref, dw_ref):
        j = pl.program_id(0)
        u, w, dout = u_ref[...], w_ref[...], d_ref[...]
        row = lax.broadcasted_iota(jnp.int32, (t, HD), 0)
        us = [_shift_rows(u, d, row) for d in range(CONV_K)]
        y = w[CONV_K - 1:CONV_K, :] * us[0]
        for d in range(1, CONV_K):
            y = y + w[CONV_K - 1 - d:CONV_K - d, :] * us[d]
        sg = _sigmoid(y)
        a = y * sg
        rs = lax.rsqrt(jnp.sum(a * a, axis=-1, keepdims=True) + EPS)
        n = a * rs
        dn = dout * jnp.where(j < nh, scale, 1.0)
        da_n = rs * (dn - n * jnp.sum(dn * n, axis=-1, keepdims=True))
        da = jnp.where(j < 2 * nh, da_n, dout)
        dy = da * (sg * (1.0 + y * (1.0 - sg)))
        du = w[CONV_K - 1:CONV_K, :] * dy
        for d in range(1, CONV_K):
            du = du + w[CONV_K - 1 - d:CONV_K - d, :] * _shift_rows(dy, -d, row)
        du_ref[...] = du.astype(BF16)
        for d in range(CONV_K):
            dw_ref[CONV_K - 1 - d:CONV_K - d, :] = jnp.sum(dy * us[d], axis=0, keepdims=True)

    return pl.pallas_call(
        body, name=name, grid=(3 * nh,),
        in_specs=[pl.BlockSpec((t, HD), lambda j: (0, blk0 + j)), pl.BlockSpec((CONV_K, HD), lambda j: (0, j)),
                  pl.BlockSpec((t, HD), lambda j: (0, j))],
        out_specs=[pl.BlockSpec((t, HD), lambda j: (0, j)), pl.BlockSpec((CONV_K, HD), lambda j: (0, j))],
        out_shape=[jax.ShapeDtypeStruct((t, 3 * nh * HD), BF16), jax.ShapeDtypeStruct((CONV_K, 3 * nh * HD), F32)],
        compiler_params=_cp(("parallel",)),
    )(proj, conv_w, dqkv)


def _gdn_gates(ab, alog, dtb, h, nh):
    lane = lax.broadcasted_iota(jnp.int32, ab.shape, 1)
    x = ab + dtb
    sp = jnp.maximum(x, 0.0) + jnp.log(1.0 + jnp.exp(-jnp.abs(x)))
    ea = jnp.exp(alog)
    la_all = -ea * sp
    beta_all = _sigmoid(ab)
    pick = lambda val, ln: jnp.sum(jnp.where(lane == ln, val, 0.0), axis=1, keepdims=True)
    la = pick(la_all, h)
    beta = pick(beta_all, nh + h)
    dla_da = pick(-ea * _sigmoid(x), h)
    return la, beta, dla_da


def _gdn_chunk(q, k, v, la, beta, C):
    low, strict = _tri(C, "lower"), _tri(C, "strict")
    g_b = _nn(low.astype(F32), jnp.broadcast_to(la, (C, HD)), HI)
    g_c = g_b[:, :C]
    gamma = jnp.where(low, jnp.exp(jnp.minimum(g_c - g_c.T, 0.0)), 0.0)
    eg = jnp.exp(g_b)
    gl = g_b[C - 1:C, :]
    ekt = jnp.exp(gl - g_b)
    p = _nt(k, k, HI)
    m = jnp.where(strict, beta * p * gamma, 0.0)
    x = (lax.broadcasted_iota(jnp.int32, (C, C), 0) == lax.broadcasted_iota(jnp.int32, (C, C), 1)).astype(F32)
    for s in range(C - 1):
        x = x - m[:, s:s + 1] * x[s:s + 1, :]
    r_w = k * (beta * eg)
    rhs = jnp.concatenate([v * beta, r_w], axis=1)
    uw = _nn(x, rhs, HI)
    qk_raw = _nt(_bf(q), _bf(k))
    return dict(gamma=gamma, eg=eg, gl=gl, ekt=ekt, p=p, x=x, r_w=r_w, uw=uw, qk_raw=qk_raw,
                low=low, strict=strict)


def _gdn_fwd(qkv, proj, ab_blk, alog, dtb, nh, name):
    t = qkv.shape[0]
    nc = t // CHUNK
    C = CHUNK

    def body(q_ref, k_ref, v_ref, ab_ref, al_ref, dt_ref, o_ref, x_ref, st_ref, s_sc):
        c, h = pl.program_id(0), pl.program_id(1)

        @pl.when(c == 0)
        def _():
            s_sc[h] = jnp.zeros((HD, HD), F32)

        q, k, v = q_ref[...], k_ref[...], v_ref[...]
        la, beta, _ = _gdn_gates(ab_ref[...], al_ref[...], dt_ref[...], h, nh)
        ch = _gdn_chunk(q, k, v, la, beta, C)
        st = s_sc[h]
        stb = _bf(st)
        u, w = ch["uw"][:, :HD], ch["uw"][:, HD:]
        vn = u - _nt(_bf(w), stb)
        qk = ch["qk_raw"] * ch["gamma"]
        o_ref[...] = _nt(_bf(q * ch["eg"]), stb) + _nn(_bf(qk), _bf(vn))
        x_ref[0, 0] = ch["x"]
        st_ref[0, 0] = st
        s_sc[h] = st * jnp.exp(ch["gl"]) + _tn(_bf(vn), _bf(k * ch["ekt"]))

    blk = lambda off: pl.BlockSpec((C, HD), lambda c, h: (c, off + h))
    vec = pl.BlockSpec((1, HD), lambda c, h: (0, 0))
    return pl.pallas_call(
        body, name=name, grid=(nc, nh),
        in_specs=[blk(0), blk(nh), blk(2 * nh), pl.BlockSpec((C, HD), lambda c, h: (c, ab_blk)), vec, vec],
        out_specs=[blk(0),
                   pl.BlockSpec((1, 1, C, C), lambda c, h: (c, h, 0, 0)),
                   pl.BlockSpec((1, 1, HD, HD), lambda c, h: (c, h, 0, 0))],
        out_shape=[jax.ShapeDtypeStruct((t, nh * HD), F32),
                   jax.ShapeDtypeStruct((nc, nh, C, C), F32),
                   jax.ShapeDtypeStruct((nc, nh, HD, HD), F32)],
        scratch_shapes=[pltpu.VMEM((nh, HD, HD), F32)],
        compiler_params=_cp(("arbitrary", "arbitrary")),
    )(qkv, qkv, qkv, proj, alog, dtb)


def _gdn_bwd(qkv, proj, ab_blk, alog, dtb, do, x_sv, st_sv, nh, name):
    t = qkv.shape[0]
    nc = t // CHUNK
    C = CHUNK

    def body(q_ref, k_ref, v_ref, ab_ref, al_ref, dt_ref, do_ref, x_ref, st_ref,
             dq_ref, dk_ref, dv_ref, dab_ref, dpar_ref, ds_sc):
        c, h = pl.program_id(0), pl.program_id(1)

        @pl.when(c == 0)
        def _():
            ds_sc[h] = jnp.zeros((HD, HD), F32)

        @pl.when((c == 0) & (h == 0))
        def _():
            dpar_ref[...] = jnp.zeros_like(dpar_ref)

        @pl.when(h == 0)
        def _():
            dab_ref[...] = jnp.zeros_like(dab_ref)

        q, k, v, do_ = q_ref[...], k_ref[...], v_ref[...], do_ref[...]
        la, beta, dla_da = _gdn_gates(ab_ref[...], al_ref[...], dt_ref[...], h, nh)
        low, strict = _tri(C, "lower"), _tri(C, "strict")
        g_b = _nn(low.astype(F32), jnp.broadcast_to(la, (C, HD)), HI)
        g_c = g_b[:, :C]
        gamma = jnp.where(low, jnp.exp(jnp.minimum(g_c - g_c.T, 0.0)), 0.0)
        eg = jnp.exp(g_b)
        gl = g_b[C - 1:C, :]
        ekt = jnp.exp(gl - g_b)
        egl = jnp.exp(gl)
        p = _nt(k, k, HI)
        x = x_ref[0, 0]
        r_w = k * (beta * eg)
        rhs = jnp.concatenate([v * beta, r_w], axis=1)
        uw = _nn(x, rhs, HI)
        u, w = uw[:, :HD], uw[:, HD:]
        qk_raw = _nt(_bf(q), _bf(k))
        qk = qk_raw * gamma
        st, dst = st_ref[0, 0], ds_sc[h]
        stb, dstb = _bf(st), _bf(dst)
        vn = u - _nt(_bf(w), stb)
        qd, kt = q * eg, k * ekt

        dvn = _tn(_bf(qk), _bf(do_)) + _nt(_bf(kt), dstb)
        dq2 = jnp.where(low, _nt(_bf(do_), _bf(vn)), 0.0)
        dqd = _nn(_bf(do_), stb)
        dkt = _nn(_bf(vn), dstb)
        dw = -_nn(_bf(dvn), stb)
        dxx = jnp.concatenate([dvn, dw], axis=1)
        dr = _tn(x, dxx, HI)
        dm = -jnp.where(strict, _nt(dr, uw, HI), 0.0)
        dr_u, dr_w = dr[:, :HD], dr[:, HD:]
        rsum = lambda z: jnp.sum(z, axis=1, keepdims=True)

        dv_ref[...] = dr_u * beta
        dmg = dm * gamma
        dbeta = rsum(dr_u * v) + rsum(dr_w * k) * eg[:, :1] + rsum(dmg * p)
        dp = dmg * beta
        dq2g = dq2 * gamma
        dk = (dr_w * (beta * eg) + dkt * ekt + _tn(_bf(dq2g), _bf(q))
              + _nn(_bf(dp + dp.T), _bf(k)))
        dq_ref[...] = dqd * eg + _nn(_bf(dq2g), _bf(k))
        dk_ref[...] = dk
        e = dp * p + dq2g * qk_raw
        t_kt = rsum(dkt * kt)
        dg = rsum(dqd * qd) + rsum(dr_w * r_w) - t_kt + rsum(e) - rsum(e.T)
        dgl = jnp.sum(t_kt, axis=0, keepdims=True) + jnp.sum(dst * st, keepdims=True) * egl[:, :1]
        rowc = lax.broadcasted_iota(jnp.int32, (C, 1), 0)
        dg = dg + jnp.where(rowc == C - 1, dgl, 0.0)
        dla = _nn(_tri(C, "upper").astype(F32), jnp.broadcast_to(dg, (C, HD)), HI)[:, :1]
        da = dla * dla_da
        db = dbeta * beta * (1.0 - beta)
        lane = lax.broadcasted_iota(jnp.int32, (C, HD), 1)
        dab_ref[...] += jnp.where(lane == h, da, 0.0) + jnp.where(lane == nh + h, db, 0.0)
        lane1 = lax.broadcasted_iota(jnp.int32, (1, HD), 1)
        d_alog = jnp.sum(dla * la, axis=0, keepdims=True)
        d_dtb = jnp.sum(da, axis=0, keepdims=True)
        dpar_ref[0:1, :] += jnp.where(lane1 == h, d_alog, 0.0)
        dpar_ref[1:2, :] += jnp.where(lane1 == h, d_dtb, 0.0)
        ds_sc[h] = dst * egl + _tn(_bf(do_), _bf(qd)) - _tn(_bf(dvn), _bf(w))

    rblk = lambda off: pl.BlockSpec((C, HD), lambda c, h: (nc - 1 - c, off + h))
    oblk = pl.BlockSpec((C, HD), lambda c, h: (nc - 1 - c, h))
    vec = pl.BlockSpec((1, HD), lambda c, h: (0, 0))
    return pl.pallas_call(
        body, name=name, grid=(nc, nh),
        in_specs=[rblk(0), rblk(nh), rblk(2 * nh),
                  pl.BlockSpec((C, HD), lambda c, h: (nc - 1 - c, ab_blk)), vec, vec, oblk,
                  pl.BlockSpec((1, 1, C, C), lambda c, h: (nc - 1 - c, h, 0, 0)),
                  pl.BlockSpec((1, 1, HD, HD), lambda c, h: (nc - 1 - c, h, 0, 0))],
        out_specs=[oblk, oblk, oblk,
                   pl.BlockSpec((C, HD), lambda c, h: (nc - 1 - c, 0)),
                   pl.BlockSpec((8, HD), lambda c, h: (0, 0))],
        out_shape=[jax.ShapeDtypeStruct((t, nh * HD), F32)] * 3
        + [jax.ShapeDtypeStruct((t, HD), F32), jax.ShapeDtypeStruct((8, HD), F32)],
        scratch_shapes=[pltpu.VMEM((nh, HD, HD), F32)],
        compiler_params=_cp(("arbitrary", "arbitrary")),
    )(qkv, qkv, qkv, proj, alog, dtb, do, x_sv, st_sv)


def _ada_fwd(c_all, w, b, name):
    nb, d = c_all.shape
    n = w.shape[1]
    tn = _pick(n, 512)

    def body(c_ref, w_ref, b_ref, o_ref):
        cv = c_ref[...]
        o_ref[...] = _nn(cv * _sigmoid(cv), w_ref[...], HI) + b_ref[...]

    return pl.pallas_call(
        body, name=name, grid=(n // tn,),
        in_specs=[pl.BlockSpec((nb, d), lambda j: (0, 0)), pl.BlockSpec((d, tn), lambda j: (0, j)),
                  pl.BlockSpec((1, tn), lambda j: (0, j))],
        out_specs=pl.BlockSpec((nb, tn), lambda j: (0, j)),
        out_shape=jax.ShapeDtypeStruct((nb, n), F32),
        compiler_params=_cp(("parallel",)),
    )(c_all, w, b)


def _ada_wgrad(c_all, dmod, name):
    nb, d = c_all.shape
    n = dmod.shape[1]
    tn = _pick(n, 512)

    def body(c_ref, g_ref, o_ref):
        cv = c_ref[...]
        o_ref[...] = _tn(cv * _sigmoid(cv), g_ref[...], HI)

    return pl.pallas_call(
        body, name=name, grid=(n // tn,),
        in_specs=[pl.BlockSpec((nb, d), lambda j: (0, 0)), pl.BlockSpec((nb, tn), lambda j: (0, j))],
        out_specs=pl.BlockSpec((d, tn), lambda j: (0, j)),
        out_shape=jax.ShapeDtypeStruct((d, n), F32),
        compiler_params=_cp(("parallel",)),
    )(c_all, dmod)


def _adamw(w, m, v, g, name, parts=False):
    r, cdim = w.shape
    tr = r if r <= 256 else _pick_rows(r, 256)
    bc1 = 1.0 - ADAM_B1 ** ADAM_STEP
    bc2 = 1.0 - ADAM_B2 ** ADAM_STEP

    def body(w_ref, m_ref, v_ref, g_ref, go_ref, d_ref, mo_ref, vo_ref):
        if parts:
            gv = g_ref[0].astype(F32)
            for s in range(1, N_DEV):
                gv = gv + g_ref[s].astype(F32)
        else:
            gv = g_ref[...]
        wv = w_ref[...]
        mn = ADAM_B1 * m_ref[...] + (1.0 - ADAM_B1) * gv
        vn = ADAM_B2 * v_ref[...] + (1.0 - ADAM_B2) * (gv * gv)
        m_hat = mn / bc1
        v_hat = vn / bc2
        go_ref[...] = gv
        d_ref[...] = -ADAM_LR * (m_hat / (jnp.sqrt(v_hat) + ADAM_EPS) + ADAM_WD * wv)
        mo_ref[...] = mn
        vo_ref[...] = vn

    spec = pl.BlockSpec((tr, cdim), lambda i: (i, 0))
    gspec = pl.BlockSpec((N_DEV, tr, cdim), lambda i: (0, i, 0)) if parts else spec
    return pl.pallas_call(
        body, name=name, grid=(r // tr,),
        in_specs=[spec, spec, spec, gspec],
        out_specs=[spec] * 4,
        out_shape=[jax.ShapeDtypeStruct((r, cdim), F32)] * 4,
        compiler_params=_cp(("parallel",)),
    )(w, m, v, g)


def _pick_rows(r, pref):
    t = pref
    while r % t:
        t -= 8
    assert t > 0
    return t


def _dev_index(x, y, c):
    return 4 * x + 2 * y + c


def _all_gather(arrays, name):
    n = len(arrays)

    def body(*refs):
        ins, outs = refs[:n], refs[n:2 * n]
        send_sems, recv_sems, local_sems = refs[2 * n:]
        x, y, c = lax.axis_index("x"), lax.axis_index("y"), lax.axis_index("c")
        me, sibling = (x, y, c), (x, y, 1 - c)
        chips = [(1 - x, y), (x, 1 - y), (1 - x, 1 - y)]

        def copy(a, k, block, to, src=None):
            slot = outs[a].at[_dev_index(*block)]
            return pltpu.make_async_remote_copy(
                src_ref=slot if src is None else src, dst_ref=slot,
                send_sem=send_sems.at[a, k], recv_sem=recv_sems.at[a, k],
                device_id=to, device_id_type=MESH)

        mine = [pltpu.make_async_copy(ins[a], outs[a].at[_dev_index(*me)], local_sems.at[a]) for a in range(n)]
        for cp in mine:
            cp.start()
        first = []
        for a in range(n):
            first.append(copy(a, 0, me, sibling, src=ins[a]))
            first += [copy(a, 1 + j, me, (*chip, c), src=ins[a]) for j, chip in enumerate(chips)]
        for cp in first:
            cp.start()
        passed = []
        for j, chip in enumerate(chips):
            for a in range(n):
                copy(a, 1 + j, (*chip, c), me).wait_recv()
                fwd = copy(a, 4 + j, (*chip, c), sibling)
                fwd.start()
                passed.append(fwd)
        for a in range(n):
            copy(a, 0, sibling, me).wait_recv()
            for j, chip in enumerate(chips):
                copy(a, 4 + j, (*chip, 1 - c), me).wait_recv()
        for cp in first + passed:
            cp.wait_send()
        for cp in mine:
            cp.wait()

    return pl.pallas_call(
        body, name=name,
        in_specs=[ANY] * n, out_specs=[ANY] * n,
        out_shape=[jax.ShapeDtypeStruct((N_DEV,) + a.shape, a.dtype) for a in arrays],
        scratch_shapes=[pltpu.SemaphoreType.DMA((n, 7)), pltpu.SemaphoreType.DMA((n, 7)),
                        pltpu.SemaphoreType.DMA((n,))],
    )(*arrays)


def _exchange_parts(arrays, name):
    n = len(arrays)

    def body(*refs):
        ins, outs = refs[:n], refs[n:2 * n]
        send_sems, recv_sems, local_sems = refs[2 * n:]
        x, y, c = lax.axis_index("x"), lax.axis_index("y"), lax.axis_index("c")
        my = _dev_index(x, y, c)
        mine = [pltpu.make_async_copy(ins[a].at[my], outs[a].at[my], local_sems.at[a]) for a in range(n)]
        for cp in mine:
            cp.start()
        sends = []
        for k in range(1, N_DEV):
            fx, fy, fc = (k >> 2) & 1, (k >> 1) & 1, k & 1
            px = (1 - x) if fx else x
            py = (1 - y) if fy else y
            pc = (1 - c) if fc else c
            for a in range(n):
                cp = pltpu.make_async_remote_copy(
                    src_ref=ins[a].at[_dev_index(px, py, pc)], dst_ref=outs[a].at[my],
                    send_sem=send_sems.at[a, k - 1], recv_sem=recv_sems.at[a, k - 1],
                    device_id=(px, py, pc), device_id_type=MESH)
                cp.start()
                sends.append((cp, a, k, _dev_index(px, py, pc)))
        for cp, a, k, peer in sends:
            cp.wait_send()
            pltpu.make_async_remote_copy(
                src_ref=ins[a].at[my], dst_ref=outs[a].at[peer],
                send_sem=send_sems.at[a, k - 1], recv_sem=recv_sems.at[a, k - 1],
                device_id=(x, y, c), device_id_type=MESH).wait_recv()
        for cp in mine:
            cp.wait()

    return pl.pallas_call(
        body, name=name,
        in_specs=[ANY] * n, out_specs=[ANY] * n,
        out_shape=[jax.ShapeDtypeStruct(a.shape, a.dtype) for a in arrays],
        scratch_shapes=[pltpu.SemaphoreType.DMA((n, 7)), pltpu.SemaphoreType.DMA((n, 7)),
                        pltpu.SemaphoreType.DMA((n,))],
    )(*arrays)


def _local_step(x, tgt, mod, n1, n2, n3, n4, w_in_p, lb_logits, hg_norm, conv_w, alog, dtb, gdn_norm,
                w_out, w_ff1, w_ff2):
    t, d = x.shape
    nh = d // 2 // HD
    ab_blk = 8 * nh
    sh_m, sc_m, gt_m, sh_f, sc_f, gt_f = [mod[i:i + 1] for i in range(6)]

    h1, r1 = _prenorm(x, n1, sc_m, sh_m, "prenorm_mix")
    proj = _mm(h1, w_in_p, "nn", [F32], "mm_proj", tn=640)
    o_hg, a_sv, hst_sv = _hgrn2_fwd(proj, lb_logits, nh, "hgrn2_fwd")
    qkv = _gdn_prep(proj, conv_w, 4 * nh, nh, "gdn_prep")
    o_gd, x_sv, gst_sv = _gdn_fwd(qkv, proj, ab_blk, alog, dtb, nh, "gdn_fwd")
    om_hg = _headnorm_fwd(o_hg, proj, 3 * nh, hg_norm, "headnorm_hg")
    om_gd = _headnorm_fwd(o_gd, proj, 7 * nh, gdn_norm, "headnorm_gdn")
    om = jnp.concatenate([om_hg, om_gd], axis=1)
    y1 = _mm(om, w_out, "nn", [F32], "mm_out")
    x1, r2 = _postnorm_res(x, y1, n2, gt_m, "postnorm_mix")
    h2, r3 = _prenorm(x1, n3, sc_f, sh_f, "prenorm_ffn")

    def relu2(acc, extra, outs):
        outs[0][...] = acc
        rl = jnp.maximum(acc, 0.0)
        outs[1][...] = (rl * rl).astype(BF16)

    u, act = _mm(h2, w_ff1, "nn", [F32, BF16], "mm_ff1", epilogue=relu2)
    y2 = _mm(act, w_ff2, "nn", [F32], "mm_ff2")
    dout, r4, loss = _final_loss(x1, y2, n4, gt_f, tgt, "final_loss")

    dy2, dgt_f, dn4 = _postnorm_bwd(dout, y2, r4, n4, gt_f, "postnorm_ffn_bwd")
    dw_ff2 = _mm(act, dy2, "tn", [F32], "mm_dw_ff2")

    def drelu2(acc, extra, outs):
        outs[0][...] = (acc * (2.0 * jnp.maximum(extra[0][...], 0.0))).astype(BF16)

    du = _mm(dy2, w_ff2, "nt", [BF16], "mm_da", epilogue=drelu2, extras=(u,))
    dw_ff1 = _mm(h2, du, "tn", [F32], "mm_dw_ff1")
    dh2 = _mm(du, w_ff1, "nt", [F32], "mm_dh2")
    dx1, dsh_f, dsc_f, dn3 = _prenorm_bwd(dh2, x1, r3, n3, sc_f, dout, "prenorm_ffn_bwd")

    dy1, dgt_m, dn2 = _postnorm_bwd(dx1, y1, r2, n2, gt_m, "postnorm_mix_bwd")
    dw_out = _mm(om, dy1, "tn", [F32], "mm_dw_out")
    dom = _mm(dy1, w_out, "nt", [F32], "mm_dom")
    do_hg, dg_hg, dhgn = _headnorm_bwd(dom, 0, o_hg, proj, 3 * nh, hg_norm, "headnorm_hg_bwd")
    do_gd, dg_gd, dgdn = _headnorm_bwd(dom, 1, o_gd, proj, 7 * nh, gdn_norm, "headnorm_gdn_bwd")
    dq_hg, df_hg, di_hg, dl0 = _hgrn2_bwd(proj, lb_logits, do_hg, a_sv, hst_sv, nh, "hgrn2_bwd")
    dq_g, dk_g, dv_g, dab, dpar = _gdn_bwd(qkv, proj, ab_blk, alog, dtb, do_gd, x_sv, gst_sv, nh, "gdn_bwd")
    dqkv = jnp.concatenate([dq_g, dk_g, dv_g], axis=1)
    du_conv, dconv = _gdn_prep_bwd(proj, conv_w, dqkv, 4 * nh, nh, "gdn_prep_bwd")
    dproj = jnp.concatenate([dq_hg, df_hg, di_hg, dg_hg, du_conv, dg_gd, dab.astype(BF16)], axis=1)
    dw_in = _mm(h1, dproj, "tn", [F32], "mm_dw_in", tn=640)
    dh1 = _mm(dproj, w_in_p, "nt", [F32], "mm_dh1", tk=640)
    dx, dsh_m, dsc_m, dn1 = _prenorm_bwd(dh1, x, r1, n1, sc_m, dx1, "prenorm_mix_bwd")

    dmod = jnp.concatenate([dsh_m, dsc_m, dgt_m, dsh_f, dsc_f, dgt_f], axis=0)
    grads = dict(dmod=dmod, n1=dn1, n2=dn2, n3=dn3, n4=dn4, w_in=dw_in, lb0=dl0, hg_norm=dhgn, conv=dconv,
                 alog=dpar[0:1], dtb=dpar[1:2], gdn_norm=dgdn, w_out=dw_out, w_ff1=dw_ff1, w_ff2=dw_ff2)
    return loss, dx, grads


def _pack(vals):
    rows = []
    for vv in vals:
        flat = vv.reshape(-1)
        pad = (-flat.shape[0]) % LANES
        if pad:
            flat = jnp.concatenate([flat, jnp.zeros((pad,), flat.dtype)])
        rows.append(flat.reshape(-1, LANES))
    return jnp.concatenate(rows, axis=0)


def _unpack(packed, shapes):
    out, r = [], 0
    for shp in shapes:
        size = 1
        for s in shp:
            size *= s
        nr = -(-size // LANES)
        out.append(packed[r:r + nr].reshape(-1)[:size].reshape(shp))
        r += nr
    return out


def _sum_parts(parts, name):
    _, r, cdim = parts.shape

    def body(p_ref, o_ref):
        acc = p_ref[0]
        for s in range(1, N_DEV):
            acc = acc + p_ref[s]
        o_ref[...] = acc

    return pl.pallas_call(
        body, name=name,
        out_shape=jax.ShapeDtypeStruct((r, cdim), F32),
        compiler_params=_cp(),
    )(parts)


def kernel(x, c, w_ada, b_ada, pre_mix_norm, post_mix_norm, pre_ffn_norm, post_ffn_norm, w_in, hg_lb_logits, hg_norm, gdn_conv_w, gdn_a_log, gdn_dt_bias, gdn_norm, w_out, w_ff1, w_ff2, loss_target, m_w_ada, m_b_ada, m_pre_mix_norm, m_post_mix_norm, m_pre_ffn_norm, m_post_ffn_norm, m_w_in, m_hg_lb_logits, m_hg_norm, m_gdn_conv_w, m_gdn_a_log, m_gdn_dt_bias, m_gdn_norm, m_w_out, m_w_ff1, m_w_ff2, v_w_ada, v_b_ada, v_pre_mix_norm, v_post_mix_norm, v_pre_ffn_norm, v_post_ffn_norm, v_w_in, v_hg_lb_logits, v_hg_norm, v_gdn_conv_w, v_gdn_a_log, v_gdn_dt_bias, v_gdn_norm, v_w_out, v_w_ff1, v_w_ff2):
    t, d = x.shape[1], x.shape[2]
    nh = d // 2 // HD
    in_cols = w_in.shape[2] * N_DEV
    main = in_cols - 2 * nh
    me = _dev_index(lax.axis_index("x"), lax.axis_index("y"), lax.axis_index("c"))

    c_all, conv_g = _all_gather([c, gdn_conv_w[0]], "gather_small")
    c_all = c_all.reshape(N_DEV, d)
    conv_full = conv_g.transpose(1, 0, 2).reshape(CONV_K, -1)
    w_in_g, w_out_g, w_ff1_g, w_ff2_g = _all_gather(
        [w_in[0].astype(BF16), w_out[0].astype(BF16), w_ff1[0].astype(BF16), w_ff2[0].astype(BF16)],
        "gather_weights")
    w_in_full = w_in_g.transpose(1, 0, 2).reshape(d, in_cols)
    w_in_p = jnp.concatenate([w_in_full, jnp.zeros((d, LANES - 2 * nh), BF16)], axis=1)
    w_out_full = w_out_g.reshape(d, d)
    w_ff1_full = w_ff1_g.transpose(1, 0, 2).reshape(d, -1)
    w_ff2_full = w_ff2_g.reshape(-1, d)

    n_ada = w_ada.shape[2]
    b_loc = lax.dynamic_slice(b_ada, (0, me * n_ada), (1, n_ada))
    mod_part = _ada_fwd(c_all, w_ada[0], b_loc, "ada_fwd")
    mod_all = _all_gather([mod_part], "gather_mod")[0]
    mod = lax.dynamic_slice(mod_all, (0, me, 0), (N_DEV, 1, n_ada)).reshape(6, d)

    pad_lane = lambda vv: jnp.concatenate([vv, jnp.zeros((1, LANES - vv.shape[1]), F32)], axis=1)
    loss, dx, g = _local_step(
        x[0], loss_target[0], mod, pre_mix_norm, post_mix_norm, pre_ffn_norm, post_ffn_norm, w_in_p,
        hg_lb_logits, hg_norm, conv_full, pad_lane(gdn_a_log), pad_lane(gdn_dt_bias), gdn_norm,
        w_out_full, w_ff1_full, w_ff2_full)

    rep_names = ["b_ada", "n1", "n2", "n3", "n4", "lb", "hg_norm", "alog", "dtb", "gdn_norm"]
    rep_w = [b_ada, pre_mix_norm, post_mix_norm, pre_ffn_norm, post_ffn_norm, hg_lb_logits, hg_norm,
             gdn_a_log, gdn_dt_bias, gdn_norm]
    rep_m = [m_b_ada, m_pre_mix_norm, m_post_mix_norm, m_pre_ffn_norm, m_post_ffn_norm, m_hg_lb_logits,
             m_hg_norm, m_gdn_a_log, m_gdn_dt_bias, m_gdn_norm]
    rep_v = [v_b_ada, v_pre_mix_norm, v_post_mix_norm, v_pre_ffn_norm, v_post_ffn_norm, v_hg_lb_logits,
             v_hg_norm, v_gdn_a_log, v_gdn_dt_bias, v_gdn_norm]
    rep_shapes = [a.shape for a in rep_w]
    g_lb = jnp.stack([g["lb0"], -g["lb0"]], axis=0)
    rep_g = [g["dmod"], g["n1"], g["n2"], g["n3"], g["n4"], g_lb, g["hg_norm"],
             g["alog"][:, :nh], g["dtb"][:, :nh], g["gdn_norm"]]
    small = _pack(rep_g + [g["conv"]])
    n_rep_rows = _pack(rep_g).shape[0]
    pad_rows = (-small.shape[0]) % 8
    if pad_rows:
        small = jnp.concatenate([small, jnp.zeros((pad_rows, LANES), F32)], axis=0)
    small_all = _all_gather([small], "gather_small_grads")[0]
    small_sum = _sum_parts(small_all, "sum_small_grads")
    rep_out = _adamw(_pack(rep_w), _pack(rep_m), _pack(rep_v), small_sum[:n_rep_rows], "adamw_small")
    rep_g_o, rep_d_o, rep_m_o, rep_v_o = [dict(zip(rep_names, _unpack(p, rep_shapes))) for p in rep_out]

    conv_sum = small_sum[n_rep_rows:n_rep_rows + CONV_K * conv_full.shape[1] // LANES].reshape(CONV_K, -1)
    n_conv = gdn_conv_w.shape[2]
    conv_loc = lax.dynamic_slice(conv_sum, (0, me * n_conv), (CONV_K, n_conv))
    conv_o = _adamw(gdn_conv_w[0], m_gdn_conv_w[0], v_gdn_conv_w[0], conv_loc, "adamw_conv")

    dmod_all = small_all[:, :6 * d // LANES, :].reshape(N_DEV, 6 * d)
    dmod_loc = lax.dynamic_slice(dmod_all, (0, me * n_ada), (N_DEV, n_ada))
    g_ada = _ada_wgrad(c_all, dmod_loc, "ada_wgrad")
    ada_o = _adamw(w_ada[0], m_w_ada[0], v_w_ada[0], g_ada, "adamw_ada")

    n_in = w_in.shape[2]
    n_ff = w_ff1.shape[2]
    p_in = g["w_in"][:, :in_cols].reshape(d, N_DEV, n_in).transpose(1, 0, 2).astype(BF16)
    p_out = g["w_out"].reshape(N_DEV, d // N_DEV, d).astype(BF16)
    p_ff1 = g["w_ff1"].reshape(d, N_DEV, n_ff).transpose(1, 0, 2).astype(BF16)
    p_ff2 = g["w_ff2"].reshape(N_DEV, -1, d).astype(BF16)
    r_in, r_out, r_ff1, r_ff2 = _exchange_parts([p_in, p_out, p_ff1, p_ff2], "exchange_grads")
    in_o = _adamw(w_in[0], m_w_in[0], v_w_in[0], r_in, "adamw_w_in", parts=True)
    out_o = _adamw(w_out[0], m_w_out[0], v_w_out[0], r_out, "adamw_w_out", parts=True)
    ff1_o = _adamw(w_ff1[0], m_w_ff1[0], v_w_ff1[0], r_ff1, "adamw_w_ff1", parts=True)
    ff2_o = _adamw(w_ff2[0], m_w_ff2[0], v_w_ff2[0], r_ff2, "adamw_w_ff2", parts=True)

    loss_tot = lax.psum(loss[0, 0], ("x", "y", "c"))

    def leaf(kind):
        return [ada_o[kind][None], rep_out_d[kind]["b_ada"], rep_out_d[kind]["n1"], rep_out_d[kind]["n2"],
                rep_out_d[kind]["n3"], rep_out_d[kind]["n4"], in_o[kind][None], rep_out_d[kind]["lb"],
                rep_out_d[kind]["hg_norm"], conv_o[kind][None], rep_out_d[kind]["alog"], rep_out_d[kind]["dtb"],
                rep_out_d[kind]["gdn_norm"], out_o[kind][None], ff1_o[kind][None], ff2_o[kind][None]]

    rep_out_d = [rep_g_o, rep_d_o, rep_m_o, rep_v_o]
    return (loss_tot, dx[None], *leaf(0), *leaf(1), *leaf(2), *leaf(3))
```

```python
import functools

import jax
import jax.numpy as jnp
from jax import lax
from jax.experimental import pallas as pl
from jax.experimental.pallas import tpu as pltpu

F32 = jnp.float32
BF16 = jnp.bfloat16
HI = lax.Precision.HIGHEST

EPS = 1e-6
CHUNK = 64
SB = 16
NSB = CHUNK // SB
HD = 128
CONV_K = 4
N_DEV = 8
LANES = 128
VMEM_LIMIT = 56 * 1024 * 1024

ADAM_LR = 0.001
ADAM_B1 = 0.9
ADAM_B2 = 0.999
ADAM_EPS = 1e-08
ADAM_WD = 0.01
ADAM_STEP = 10

ANY = pl.BlockSpec(memory_space=pl.ANY)
MESH = pl.DeviceIdType.MESH


def _cp(sem=None):
    return pltpu.CompilerParams(dimension_semantics=sem, vmem_limit_bytes=VMEM_LIMIT)


def _dot(a, b, dims, precision=None):
    return lax.dot_general(a, b, (dims, ((), ())), precision=precision, preferred_element_type=F32)


def _nn(a, b, precision=None):
    return _dot(a, b, ((1,), (0,)), precision)


def _nt(a, b, precision=None):
    return _dot(a, b, ((1,), (1,)), precision)


def _tn(a, b, precision=None):
    return _dot(a, b, ((0,), (0,)), precision)


def _bf(x):
    return x.astype(BF16)


def _sigmoid(x):
    return 1.0 / (1.0 + jnp.exp(-x))


def _pick(n, pref):
    if n <= pref:
        return n
    t = pref
    while n % t:
        t -= LANES
    assert t > 0, (n, pref)
    return t


def _mm(a, b, mode, out_dtypes, name, epilogue=None, extras=(), tm=1024, tn=2048, tk=512):
    if mode == "nn":
        (m, kd), (_, n) = a.shape, b.shape
    elif mode == "nt":
        (m, kd), (n, _) = a.shape, b.shape
    else:
        (kd, m), (_, n) = a.shape, b.shape
    tm, tn, tk = _pick(m, tm), _pick(n, tn), _pick(kd, tk)
    nk = kd // tk
    if mode == "nn":
        a_spec = pl.BlockSpec((tm, tk), lambda i, j, k: (i, k))
        b_spec = pl.BlockSpec((tk, tn), lambda i, j, k: (k, j))
        dims = ((1,), (0,))
    elif mode == "nt":
        a_spec = pl.BlockSpec((tm, tk), lambda i, j, k: (i, k))
        b_spec = pl.BlockSpec((tn, tk), lambda i, j, k: (j, k))
        dims = ((1,), (1,))
    else:
        a_spec = pl.BlockSpec((tk, tm), lambda i, j, k: (k, i))
        b_spec = pl.BlockSpec((tk, tn), lambda i, j, k: (k, j))
        dims = ((0,), (0,))
    o_spec = pl.BlockSpec((tm, tn), lambda i, j, k: (i, j))
    n_extra, n_out = len(extras), len(out_dtypes)

    def body(a_ref, b_ref, *rest):
        extra_refs = rest[:n_extra]
        out_refs = rest[n_extra:n_extra + n_out]
        acc = rest[-1]
        k = pl.program_id(2)

        @pl.when(k == 0)
        def _():
            acc[...] = jnp.zeros_like(acc)

        acc[...] += _dot(a_ref[...], b_ref[...], dims)

        @pl.when(k == nk - 1)
        def _():
            if epilogue is None:
                out_refs[0][...] = acc[...].astype(out_dtypes[0])
            else:
                epilogue(acc[...], extra_refs, out_refs)

    outs = pl.pallas_call(
        body, name=name,
        grid=(m // tm, n // tn, nk),
        in_specs=[a_spec, b_spec] + [o_spec] * n_extra,
        out_specs=[o_spec] * n_out,
        out_shape=[jax.ShapeDtypeStruct((m, n), dt) for dt in out_dtypes],
        scratch_shapes=[pltpu.VMEM((tm, tn), F32)],
        compiler_params=_cp(("parallel", "parallel", "arbitrary")),
    )(a, b, *extras)
    return outs[0] if n_out == 1 else outs


def _row_spec(tb, d):
    return pl.BlockSpec((tb, d), lambda i: (i, 0))


def _vec_spec(d):
    return pl.BlockSpec((1, d), lambda i: (0, 0))


def _prenorm(x, w, sc, sh, name):
    t, d = x.shape
    tb = _pick(t, 256)

    def body(x_ref, w_ref, sc_ref, sh_ref, h_ref, r_ref):
        xv = x_ref[...]
        r = lax.rsqrt(jnp.mean(xv * xv, axis=-1, keepdims=True) + EPS)
        h_ref[...] = ((xv * r * w_ref[...]) * (1.0 + sc_ref[...]) + sh_ref[...]).astype(BF16)
        r_ref[...] = r

    return pl.pallas_call(
        body, name=name, grid=(t // tb,),
        in_specs=[_row_spec(tb, d), _vec_spec(d), _vec_spec(d), _vec_spec(d)],
        out_specs=[_row_spec(tb, d), _row_spec(tb, 1)],
        out_shape=[jax.ShapeDtypeStruct((t, d), BF16), jax.ShapeDtypeStruct((t, 1), F32)],
        compiler_params=_cp(("parallel",)),
    )(x, w, sc, sh)


def _postnorm_res(x, y, w, gt, name):
    t, d = x.shape
    tb = _pick(t, 256)

    def body(x_ref, y_ref, w_ref, gt_ref, o_ref, r_ref):
        yv = y_ref[...]
        r = lax.rsqrt(jnp.mean(yv * yv, axis=-1, keepdims=True) + EPS)
        o_ref[...] = x_ref[...] + gt_ref[...] * (yv * r * w_ref[...])
        r_ref[...] = r

    return pl.pallas_call(
        body, name=name, grid=(t // tb,),
        in_specs=[_row_spec(tb, d), _row_spec(tb, d), _vec_spec(d), _vec_spec(d)],
        out_specs=[_row_spec(tb, d), _row_spec(tb, 1)],
        out_shape=[jax.ShapeDtypeStruct((t, d), F32), jax.ShapeDtypeStruct((t, 1), F32)],
        compiler_params=_cp(("parallel",)),
    )(x, y, w, gt)


def _final_loss(x, y, w, gt, tgt, name):
    t, d = x.shape
    tb = _pick(t, 256)

    def body(x_ref, y_ref, w_ref, gt_ref, tgt_ref, dout_ref, r_ref, loss_ref):
        @pl.when(pl.program_id(0) == 0)
        def _():
            loss_ref[...] = jnp.zeros_like(loss_ref)

        yv = y_ref[...]
        r = lax.rsqrt(jnp.mean(yv * yv, axis=-1, keepdims=True) + EPS)
        out = x_ref[...] + gt_ref[...] * (yv * r * w_ref[...])
        diff = out - tgt_ref[...]
        row = jnp.mean(diff * diff, axis=-1, keepdims=True)
        loss_ref[...] += 0.5 * jnp.sum(row, axis=0, keepdims=True)
        dout_ref[...] = diff * (1.0 / d)
        r_ref[...] = r

    return pl.pallas_call(
        body, name=name, grid=(t // tb,),
        in_specs=[_row_spec(tb, d), _row_spec(tb, d), _vec_spec(d), _vec_spec(d), _row_spec(tb, d)],
        out_specs=[_row_spec(tb, d), _row_spec(tb, 1), pl.BlockSpec((1, 1), lambda i: (0, 0))],
        out_shape=[jax.ShapeDtypeStruct((t, d), F32), jax.ShapeDtypeStruct((t, 1), F32),
                   jax.ShapeDtypeStruct((1, 1), F32)],
        compiler_params=_cp(("arbitrary",)),
    )(x, y, w, gt, tgt)


def _postnorm_bwd(dxn, y, r, w, gt, name):
    t, d = y.shape
    tb = _pick(t, 256)

    def body(dx_ref, y_ref, r_ref, w_ref, gt_ref, dy_ref, dgt_ref, dw_ref):
        @pl.when(pl.program_id(0) == 0)
        def _():
            dgt_ref[...] = jnp.zeros_like(dgt_ref)
            dw_ref[...] = jnp.zeros_like(dw_ref)

        dxv, rv, wv = dx_ref[...], r_ref[...], w_ref[...]
        z = y_ref[...] * rv
        dgt_ref[...] += jnp.sum(dxv * (z * wv), axis=0, keepdims=True)
        dn = dxv * gt_ref[...]
        dw_ref[...] += jnp.sum(dn * z, axis=0, keepdims=True)
        dz = dn * wv
        dy_ref[...] = (rv * (dz - z * jnp.mean(dz * z, axis=-1, keepdims=True))).astype(BF16)

    return pl.pallas_call(
        body, name=name, grid=(t // tb,),
        in_specs=[_row_spec(tb, d), _row_spec(tb, d), _row_spec(tb, 1), _vec_spec(d), _vec_spec(d)],
        out_specs=[_row_spec(tb, d), _vec_spec(d), _vec_spec(d)],
        out_shape=[jax.ShapeDtypeStruct((t, d), BF16), jax.ShapeDtypeStruct((1, d), F32),
                   jax.ShapeDtypeStruct((1, d), F32)],
        compiler_params=_cp(("arbitrary",)),
    )(dxn, y, r, w, gt)


def _prenorm_bwd(dh, x, r, w, sc, dres, name):
    t, d = x.shape
    tb = _pick(t, 256)

    def body(dh_ref, x_ref, r_ref, w_ref, sc_ref, dres_ref, dx_ref, dsh_ref, dsc_ref, dw_ref):
        @pl.when(pl.program_id(0) == 0)
        def _():
            dsh_ref[...] = jnp.zeros_like(dsh_ref)
            dsc_ref[...] = jnp.zeros_like(dsc_ref)
            dw_ref[...] = jnp.zeros_like(dw_ref)

        dhv, rv, wv = dh_ref[...], r_ref[...], w_ref[...]
        z = x_ref[...] * rv
        dsh_ref[...] += jnp.sum(dhv, axis=0, keepdims=True)
        dsc_ref[...] += jnp.sum(dhv * (z * wv), axis=0, keepdims=True)
        dzw = dhv * (1.0 + sc_ref[...])
        dw_ref[...] += jnp.sum(dzw * z, axis=0, keepdims=True)
        dz = dzw * wv
        dx_ref[...] = dres_ref[...] + rv * (dz - z * jnp.mean(dz * z, axis=-1, keepdims=True))

    return pl.pallas_call(
        body, name=name, grid=(t // tb,),
        in_specs=[_row_spec(tb, d), _row_spec(tb, d), _row_spec(tb, 1), _vec_spec(d), _vec_spec(d),
                  _row_spec(tb, d)],
        out_specs=[_row_spec(tb, d), _vec_spec(d), _vec_spec(d), _vec_spec(d)],
        out_shape=[jax.ShapeDtypeStruct((t, d), F32)] + [jax.ShapeDtypeStruct((1, d), F32)] * 3,
        compiler_params=_cp(("arbitrary",)),
    )(dh, x, r, w, sc, dres)


def _headnorm_fwd(o, proj, g_blk, nw, name):
    t, wd = o.shape
    nh = wd // HD
    tb = _pick(t, 512)
    gb = g_blk * HD // wd

    def body(o_ref, g_ref, nw_ref, out_ref):
        o3 = o_ref[...].reshape(tb, nh, HD)
        g3 = g_ref[...].reshape(tb, nh, HD)
        rh = lax.rsqrt(jnp.mean(o3 * o3, axis=-1, keepdims=True) + EPS)
        res = (o3 * rh * nw_ref[...].reshape(1, 1, HD)) * (g3 * _sigmoid(g3))
        out_ref[...] = res.reshape(tb, wd).astype(BF16)

    return pl.pallas_call(
        body, name=name, grid=(t // tb,),
        in_specs=[_row_spec(tb, wd), pl.BlockSpec((tb, wd), lambda i: (i, gb)), _vec_spec(HD)],
        out_specs=_row_spec(tb, wd),
        out_shape=jax.ShapeDtypeStruct((t, wd), BF16),
        compiler_params=_cp(("parallel",)),
    )(o, proj, nw)


def _headnorm_bwd(dom, col_blk, o, proj, g_blk, nw, name):
    t, wd = o.shape
    nh = wd // HD
    tb = _pick(t, 512)
    gb = g_blk * HD // wd

    def body(do_ref, o_ref, g_ref, nw_ref, dout_ref, dg_ref, dnw_ref):
        @pl.when(pl.program_id(0) == 0)
        def _():
            dnw_ref[...] = jnp.zeros_like(dnw_ref)

        dn = do_ref[...].reshape(tb, nh, HD)
        o3 = o_ref[...].reshape(tb, nh, HD)
        g3 = g_ref[...].reshape(tb, nh, HD)
        nw3 = nw_ref[...].reshape(1, 1, HD)
        rh = lax.rsqrt(jnp.mean(o3 * o3, axis=-1, keepdims=True) + EPS)
        z = o3 * rh
        sg = _sigmoid(g3)
        sl = g3 * sg
        dnw_ref[...] += jnp.sum(jnp.sum(dn * sl * z, axis=1), axis=0, keepdims=True)
        dg_ref[...] = (dn * (z * nw3) * (sg * (1.0 + g3 * (1.0 - sg)))).reshape(tb, wd).astype(BF16)
        dz = dn * sl * nw3
        dout_ref[...] = (rh * (dz - z * jnp.mean(dz * z, axis=-1, keepdims=True))).reshape(tb, wd)

    return pl.pallas_call(
        body, name=name, grid=(t // tb,),
        in_specs=[pl.BlockSpec((tb, wd), lambda i: (i, col_blk)), _row_spec(tb, wd),
                  pl.BlockSpec((tb, wd), lambda i: (i, gb)), _vec_spec(HD)],
        out_specs=[_row_spec(tb, wd), _row_spec(tb, wd), _vec_spec(HD)],
        out_shape=[jax.ShapeDtypeStruct((t, wd), F32), jax.ShapeDtypeStruct((t, wd), BF16),
                   jax.ShapeDtypeStruct((1, HD), F32)],
        compiler_params=_cp(("arbitrary",)),
    )(dom, o, proj, nw)


def _tri(n, kind):
    r = lax.broadcasted_iota(jnp.int32, (n, n), 0)
    c = lax.broadcasted_iota(jnp.int32, (n, n), 1)
    if kind == "lower":
        return r >= c
    if kind == "strict":
        return r > c
    return r <= c


def _hg_gate(fl, lg_ref):
    l0, l1 = lg_ref[0, 0], lg_ref[1, 0]
    mx = jnp.maximum(l0, l1)
    e0, e1 = jnp.exp(l0 - mx), jnp.exp(l1 - mx)
    lb = e0 / (e0 + e1)
    sg = _sigmoid(fl)
    f = lb + (1.0 - lb) * sg
    return lb, sg, f


def _hgrn2_fwd(proj, lb_logits, nh, name):
    t = proj.shape[0]
    nc = t // CHUNK
    C = CHUNK
    lg = lb_logits.reshape(2, nh, 1, HD)

    def body(q_ref, f_ref, i_ref, lg_ref, o_ref, a_ref, st_ref, s_sc, p_sc, r_sc):
        c, h = pl.program_id(0), pl.program_id(1)

        @pl.when(c == 0)
        def _():
            s_sc[h] = jnp.zeros((HD, HD), F32)

        q, v = q_ref[...], i_ref[...]
        _, _, f = _hg_gate(f_ref[...], lg_ref)
        k = 1.0 - f
        low = _tri(C, "lower")
        b = _nn(low.astype(F32), jnp.log(f), HI)
        lane_c = lax.broadcasted_iota(jnp.int32, (SB, C), 1)
        lane_h = lax.broadcasted_iota(jnp.int32, (SB, HD), 1)
        row_h = lax.broadcasted_iota(jnp.int32, (SB, HD), 0)
        ones = jnp.ones((HD, HD), BF16)

        for i in range(NSB):
            qi, ki, bi = q[SB * i:SB * (i + 1)], k[SB * i:SB * (i + 1)], b[SB * i:SB * (i + 1)]
            for s in range(SB):
                e = jnp.exp(jnp.minimum(bi - bi[s:s + 1], 0.0))
                p = jnp.where(row_h >= s, qi * ki[s:s + 1] * e, 0.0)
                p_sc[pl.ds((i * SB + s) * SB, SB), :] = p.astype(BF16)
        r_sc[...] = _nn(p_sc[...], ones)
        a_rows = []
        for i in range(NSB):
            acc = jnp.zeros((SB, HD), F32)
            for s in range(SB):
                acc = jnp.where(lane_h == SB * i + s, r_sc[pl.ds((i * SB + s) * SB, SB), :], acc)
            acc = acc[:, :C]
            if i > 0:
                r = b[SB * i - 1:SB * i]
                bi = b[SB * i:SB * (i + 1)]
                qf = q[SB * i:SB * (i + 1)] * jnp.exp(bi - r)
                kf = k * jnp.exp(jnp.minimum(r - b, 0.0))
                acc = acc + jnp.where(lane_c < SB * i, _nt(_bf(qf), _bf(kf)), 0.0)
            a_rows.append(acc)
        a = jnp.concatenate(a_rows, axis=0)
        st = s_sc[h]
        bl = b[C - 1:C, :]
        o_ref[...] = _nn(_bf(a), _bf(v)) + _nt(_bf(q * jnp.exp(b)), _bf(st))
        a_ref[0, 0] = a
        st_ref[0, 0] = st
        s_sc[h] = st * jnp.exp(bl) + _tn(_bf(v), _bf(k * jnp.exp(bl - b)))

    blk = lambda off: pl.BlockSpec((C, HD), lambda c, h: (c, off + h))
    return pl.pallas_call(
        body, name=name, grid=(nc, nh),
        in_specs=[blk(0), blk(nh), blk(2 * nh),
                  pl.BlockSpec((2, 1, 1, HD), lambda c, h: (0, h, 0, 0))],
        out_specs=[blk(0),
                   pl.BlockSpec((1, 1, C, C), lambda c, h: (c, h, 0, 0)),
                   pl.BlockSpec((1, 1, HD, HD), lambda c, h: (c, h, 0, 0))],
        out_shape=[jax.ShapeDtypeStruct((t, nh * HD), F32),
                   jax.ShapeDtypeStruct((nc, nh, C, C), F32),
                   jax.ShapeDtypeStruct((nc, nh, HD, HD), F32)],
        scratch_shapes=[pltpu.VMEM((nh, HD, HD), F32), pltpu.VMEM((C * SB, HD), BF16), pltpu.VMEM((C * SB, HD), F32)],
        compiler_params=_cp(("arbitrary", "arbitrary")),
    )(proj, proj, proj, lg)


def _hgrn2_bwd(proj, lb_logits, do, a_sv, st_sv, nh, name):
    t = proj.shape[0]
    nc = t // CHUNK
    C = CHUNK
    lg = lb_logits.reshape(2, nh, 1, HD)

    def body(q_ref, f_ref, i_ref, lg_ref, do_ref, a_ref, st_ref,
             dq_ref, df_ref, di_ref, dl_ref, ds_sc, p_sc, r_sc):
        c, h = pl.program_id(0), pl.program_id(1)

        @pl.when(c == 0)
        def _():
            ds_sc[h] = jnp.zeros((HD, HD), F32)

        @pl.when((c == 0) & (h == 0))
        def _():
            dl_ref[...] = jnp.zeros_like(dl_ref)

        q, v, do_ = q_ref[...], i_ref[...], do_ref[...]
        lb, sg, f = _hg_gate(f_ref[...], lg_ref)
        k = 1.0 - f
        low = _tri(C, "lower")
        b = _nn(low.astype(F32), jnp.log(f), HI)
        bl = b[C - 1:C, :]
        eb, ekb = jnp.exp(b), jnp.exp(bl - b)
        qb, kb = q * eb, k * ekb
        a, st, dst = a_ref[0, 0], st_ref[0, 0], ds_sc[h]

        da = jnp.where(low, _nt(_bf(do_), _bf(v)), 0.0)
        dv = _tn(_bf(a), _bf(do_)) + _nt(_bf(kb), _bf(dst))
        dqb = _nn(_bf(do_), _bf(st))
        dkb = _nn(_bf(v), _bf(dst))

        row = lax.broadcasted_iota(jnp.int32, (C, HD), 0)
        lane_c = lax.broadcasted_iota(jnp.int32, (SB, C), 1)
        row_h = lax.broadcasted_iota(jnp.int32, (SB, HD), 0)
        ones = jnp.ones((HD, HD), BF16)
        sel = (lax.broadcasted_iota(jnp.int32, (C, C * SB), 0)
               == jnp.right_shift(lax.broadcasted_iota(jnp.int32, (C, C * SB), 1), SB.bit_length() - 1)).astype(BF16)

        for i in range(NSB):
            doi, vi = do_[SB * i:SB * (i + 1)], v[SB * i:SB * (i + 1)]
            for s in range(SB):
                p_sc[pl.ds((i * SB + s) * SB, SB), :] = (doi * vi[s:s + 1]).astype(BF16)
        r_sc[...] = _nn(p_sc[...], ones)
        dq_rows = []
        dk_off = jnp.zeros((C, HD), F32)
        for i in range(NSB):
            qi, ki, bi = q[SB * i:SB * (i + 1)], k[SB * i:SB * (i + 1)], b[SB * i:SB * (i + 1)]
            acc = jnp.zeros((SB, HD), F32)
            for s in range(SB):
                e = jnp.exp(jnp.minimum(bi - bi[s:s + 1], 0.0))
                g = jnp.where(row_h >= s, r_sc[pl.ds((i * SB + s) * SB, SB), :] * e, 0.0)
                acc = acc + g * ki[s:s + 1]
                p_sc[pl.ds((i * SB + s) * SB, SB), :] = (g * qi).astype(BF16)
            if i > 0:
                r = b[SB * i - 1:SB * i]
                fq = jnp.exp(bi - r)
                fk = jnp.exp(jnp.minimum(r - b, 0.0))
                dai = _bf(jnp.where(lane_c < SB * i, da[SB * i:SB * (i + 1)], 0.0))
                acc = acc + _nn(dai, _bf(k * fk)) * fq
                dk_off = dk_off + _tn(dai, _bf(qi * fq)) * fk
            dq_rows.append(acc)
        dqi = jnp.concatenate(dq_rows, axis=0)
        dq = dqi + dqb * eb
        dk_inter = dkb * ekb
        dk = _nn(sel, p_sc[...]) + dk_off + dk_inter
        db = q * dq - k * dk
        extra = (jnp.sum(k * dk_inter, axis=0, keepdims=True)
                 + jnp.exp(bl) * jnp.sum(dst * st, axis=0, keepdims=True))
        db = db + jnp.where(row == C - 1, extra, 0.0)
        dlf = _nn(_tri(C, "upper").astype(F32), db, HI)
        df = dlf / f - dk
        dq_ref[...] = dq.astype(BF16)
        df_ref[...] = (df * (1.0 - lb) * sg * (1.0 - sg)).astype(BF16)
        di_ref[...] = dv.astype(BF16)
        dl_ref[pl.ds(h, 1), :] += jnp.sum(df * (1.0 - sg), axis=0, keepdims=True) * (lb * (1.0 - lb))
        ds_sc[h] = dst * jnp.exp(bl) + _tn(_bf(do_), _bf(qb))

    rblk = lambda off: pl.BlockSpec((C, HD), lambda c, h: (nc - 1 - c, off + h))
    oblk = pl.BlockSpec((C, HD), lambda c, h: (nc - 1 - c, h))
    return pl.pallas_call(
        body, name=name, grid=(nc, nh),
        in_specs=[rblk(0), rblk(nh), rblk(2 * nh),
                  pl.BlockSpec((2, 1, 1, HD), lambda c, h: (0, h, 0, 0)),
                  oblk,
                  pl.BlockSpec((1, 1, C, C), lambda c, h: (nc - 1 - c, h, 0, 0)),
                  pl.BlockSpec((1, 1, HD, HD), lambda c, h: (nc - 1 - c, h, 0, 0))],
        out_specs=[oblk, oblk, oblk, pl.BlockSpec((nh, HD), lambda c, h: (0, 0))],
        out_shape=[jax.ShapeDtypeStruct((t, nh * HD), BF16)] * 3 + [jax.ShapeDtypeStruct((nh, HD), F32)],
        scratch_shapes=[pltpu.VMEM((nh, HD, HD), F32), pltpu.VMEM((C * SB, HD), BF16),
                        pltpu.VMEM((C * SB, HD), F32)],
        compiler_params=_cp(("arbitrary", "arbitrary")),
    )(proj, proj, proj, lg, do, a_sv, st_sv)


def _shift_rows(u, d, row):
    t = u.shape[0]
    if d == 0:
        return u
    rolled = pltpu.roll(u, d % t, 0)
    if d > 0:
        return jnp.where(row >= d, rolled, 0.0)
    return jnp.where(row < t + d, rolled, 0.0)


def _gdn_prep(proj, conv_w, blk0, nh, name):
    t = proj.shape[0]
    scale = HD ** -0.5

    def body(u_ref, w_ref, o_ref):
        j = pl.program_id(0)
        u, w = u_ref[...], w_ref[...]
        row = lax.broadcasted_iota(jnp.int32, (t, HD), 0)
        y = w[CONV_K - 1:CONV_K, :] * u
        for d in range(1, CONV_K):
            y = y + w[CONV_K - 1 - d:CONV_K - d, :] * _shift_rows(u, d, row)
        a = y * _sigmoid(y)
        n = a * lax.rsqrt(jnp.sum(a * a, axis=-1, keepdims=True) + EPS)
        n = n * jnp.where(j < nh, scale, 1.0)
        o_ref[...] = jnp.where(j < 2 * nh, n, a)

    return pl.pallas_call(
        body, name=name, grid=(3 * nh,),
        in_specs=[pl.BlockSpec((t, HD), lambda j: (0, blk0 + j)), pl.BlockSpec((CONV_K, HD), lambda j: (0, j))],
        out_specs=pl.BlockSpec((t, HD), lambda j: (0, j)),
        out_shape=jax.ShapeDtypeStruct((t, 3 * nh * HD), F32),
        compiler_params=_cp(("parallel",)),
    )(proj, conv_w)


def _gdn_prep_bwd(proj, conv_w, dqkv, blk0, nh, name):
    t = proj.shape[0]
    scale = HD ** -0.5

    def body(u_ref, w_ref, d_ref, du_ref, dw_ref):
        j = pl.program_id(0)
        u, w, dout = u_ref[...], w_ref[...], d_ref[...]
        row = lax.broadcasted_iota(jnp.int32, (t, HD), 0)
        us = [_shift_rows(u, d, row) for d in range(CONV_K)]
        y = w[CONV_K - 1:CONV_K, :] * us[0]
        for d in range(1, CONV_K):
            y = y + w[CONV_K - 1 - d:CONV_K - d, :] * us[d]
        sg = _sigmoid(y)
        a = y * sg
        rs = lax.rsqrt(jnp.sum(a * a, axis=-1, keepdims=True) + EPS)
        n = a * rs
        dn = dout * jnp.where(j < nh, scale, 1.0)
        da_n = rs * (dn - n * jnp.sum(dn * n, axis=-1, keepdims=True))
        da = jnp.where(j < 2 * nh, da_n, dout)
        dy = da * (sg * (1.0 + y * (1.0 - sg)))
        du = w[CONV_K - 1:CONV_K, :] * dy
        for d in range(1, CONV_K):
            du = du + w[CONV_K - 1 - d:CONV_K - d, :] * _shift_rows(dy, -d, row)
        du_ref[...] = du.astype(BF16)
        for d in range(CONV_K):
            dw_ref[CONV_K - 1 - d:CONV_K - d, :] = jnp.sum(dy * us[d], axis=0, keepdims=True)

    return pl.pallas_call(
        body, name=name, grid=(3 * nh,),
        in_specs=[pl.BlockSpec((t, HD), lambda j: (0, blk0 + j)), pl.BlockSpec((CONV_K, HD), lambda j: (0, j)),
                  pl.BlockSpec((t, HD), lambda j: (0, j))],
        out_specs=[pl.BlockSpec((t, HD), lambda j: (0, j)), pl.BlockSpec((CONV_K, HD), lambda j: (0, j))],
        out_shape=[jax.ShapeDtypeStruct((t, 3 * nh * HD), BF16), jax.ShapeDtypeStruct((CONV_K, 3 * nh * HD), F32)],
        compiler_params=_cp(("parallel",)),
    )(proj, conv_w, dqkv)


def _gdn_gates(ab, alog, dtb, h, nh):
    lane = lax.broadcasted_iota(jnp.int32, ab.shape, 1)
    x = ab + dtb
    sp = jnp.maximum(x, 0.0) + jnp.log(1.0 + jnp.exp(-jnp.abs(x)))
    ea = jnp.exp(alog)
    la_all = -ea * sp
    beta_all = _sigmoid(ab)
    pick = lambda val, ln: jnp.sum(jnp.where(lane == ln, val, 0.0), axis=1, keepdims=True)
    la = pick(la_all, h)
    beta = pick(beta_all, nh + h)
    dla_da = pick(-ea * _sigmoid(x), h)
    return la, beta, dla_da


def _gdn_chunk(q, k, v, la, beta, C):
    low, strict = _tri(C, "lower"), _tri(C, "strict")
    g_b = _nn(low.astype(F32), jnp.broadcast_to(la, (C, HD)), HI)
    g_c = g_b[:, :C]
    gamma = jnp.where(low, jnp.exp(jnp.minimum(g_c - g_c.T, 0.0)), 0.0)
    eg = jnp.exp(g_b)
    gl = g_b[C - 1:C, :]
    ekt = jnp.exp(gl - g_b)
    p = _nt(k, k, HI)
    m = jnp.where(strict, beta * p * gamma, 0.0)
    x = (lax.broadcasted_iota(jnp.int32, (C, C), 0) == lax.broadcasted_iota(jnp.int32, (C, C), 1)).astype(F32)
    for s in range(C - 1):
        x = x - m[:, s:s + 1] * x[s:s + 1, :]
    r_w = k * (beta * eg)
    rhs = jnp.concatenate([v * beta, r_w], axis=1)
    uw = _nn(x, rhs, HI)
    qk_raw = _nt(_bf(q), _bf(k))
    return dict(gamma=gamma, eg=eg, gl=gl, ekt=ekt, p=p, x=x, r_w=r_w, uw=uw, qk_raw=qk_raw,
                low=low, strict=strict)


def _gdn_fwd(qkv, proj, ab_blk, alog, dtb, nh, name):
    t = qkv.shape[0]
    nc = t // CHUNK
    C = CHUNK

    def body(q_ref, k_ref, v_ref, ab_ref, al_ref, dt_ref, o_ref, x_ref, st_ref, s_sc):
        c, h = pl.program_id(0), pl.program_id(1)

        @pl.when(c == 0)
        def _():
            s_sc[h] = jnp.zeros((HD, HD), F32)

        q, k, v = q_ref[...], k_ref[...], v_ref[...]
        la, beta, _ = _gdn_gates(ab_ref[...], al_ref[...], dt_ref[...], h, nh)
        ch = _gdn_chunk(q, k, v, la, beta, C)
        st = s_sc[h]
        stb = _bf(st)
        u, w = ch["uw"][:, :HD], ch["uw"][:, HD:]
        vn = u - _nt(_bf(w), stb)
        qk = ch["qk_raw"] * ch["gamma"]
        o_ref[...] = _nt(_bf(q * ch["eg"]), stb) + _nn(_bf(qk), _bf(vn))
        x_ref[0, 0] = ch["x"]
        st_ref[0, 0] = st
        s_sc[h] = st * jnp.exp(ch["gl"]) + _tn(_bf(vn), _bf(k * ch["ekt"]))

    blk = lambda off: pl.BlockSpec((C, HD), lambda c, h: (c, off + h))
    vec = pl.BlockSpec((1, HD), lambda c, h: (0, 0))
    return pl.pallas_call(
        body, name=name, grid=(nc, nh),
        in_specs=[blk(0), blk(nh), blk(2 * nh), pl.BlockSpec((C, HD), lambda c, h: (c, ab_blk)), vec, vec],
        out_specs=[blk(0),
                   pl.BlockSpec((1, 1, C, C), lambda c, h: (c, h, 0, 0)),
                   pl.BlockSpec((1, 1, HD, HD), lambda c, h: (c, h, 0, 0))],
        out_shape=[jax.ShapeDtypeStruct((t, nh * HD), F32),
                   jax.ShapeDtypeStruct((nc, nh, C, C), F32),
                   jax.ShapeDtypeStruct((nc, nh, HD, HD), F32)],
        scratch_shapes=[pltpu.VMEM((nh, HD, HD), F32)],
        compiler_params=_cp(("arbitrary", "arbitrary")),
    )(qkv, qkv, qkv, proj, alog, dtb)


def _gdn_bwd(qkv, proj, ab_blk, alog, dtb, do, x_sv, st_sv, nh, name):
    t = qkv.shape[0]
    nc = t // CHUNK
    C = CHUNK

    def body(q_ref, k_ref, v_ref, ab_ref, al_ref, dt_ref, do_ref, x_ref, st_ref,
             dq_ref, dk_ref, dv_ref, dab_ref, dpar_ref, ds_sc):
        c, h = pl.program_id(0), pl.program_id(1)

        @pl.when(c == 0)
        def _():
            ds_sc[h] = jnp.zeros((HD, HD), F32)

        @pl.when((c == 0) & (h == 0))
        def _():
            dpar_ref[...] = jnp.zeros_like(dpar_ref)

        @pl.when(h == 0)
        def _():
            dab_ref[...] = jnp.zeros_like(dab_ref)

        q, k, v, do_ = q_ref[...], k_ref[...], v_ref[...], do_ref[...]
        la, beta, dla_da = _gdn_gates(ab_ref[...], al_ref[...], dt_ref[...], h, nh)
        low, strict = _tri(C, "lower"), _tri(C, "strict")
        g_b = _nn(low.astype(F32), jnp.broadcast_to(la, (C, HD)), HI)
        g_c = g_b[:, :C]
        gamma = jnp.where(low, jnp.exp(jnp.minimum(g_c - g_c.T, 0.0)), 0.0)
        eg = jnp.exp(g_b)
        gl = g_b[C - 1:C, :]
        ekt = jnp.exp(gl - g_b)
        egl = jnp.exp(gl)
        p = _nt(k, k, HI)
        x = x_ref[0, 0]
        r_w = k * (beta * eg)
        rhs = jnp.concatenate([v * beta, r_w], axis=1)
        uw = _nn(x, rhs, HI)
        u, w = uw[:, :HD], uw[:, HD:]
        qk_raw = _nt(_bf(q), _bf(k))
        qk = qk_raw * gamma
        st, dst = st_ref[0, 0], ds_sc[h]
        stb, dstb = _bf(st), _bf(dst)
        vn = u - _nt(_bf(w), stb)
        qd, kt = q * eg, k * ekt

        dvn = _tn(_bf(qk), _bf(do_)) + _nt(_bf(kt), dstb)
        dq2 = jnp.where(low, _nt(_bf(do_), _bf(vn)), 0.0)
        dqd = _nn(_bf(do_), stb)
        dkt = _nn(_bf(vn), dstb)
        dw = -_nn(_bf(dvn), stb)
        dxx = jnp.concatenate([dvn, dw], axis=1)
        dr = _tn(x, dxx, HI)
        dm = -jnp.where(strict, _nt(dr, uw, HI), 0.0)
        dr_u, dr_w = dr[:, :HD], dr[:, HD:]
        rsum = lambda z: jnp.sum(z, axis=1, keepdims=True)

        dv_ref[...] = dr_u * beta
        dmg = dm * gamma
        dbeta = rsum(dr_u * v) + rsum(dr_w * k) * eg[:, :1] + rsum(dmg * p)
        dp = dmg * beta
        dq2g = dq2 * gamma
        dk = (dr_w * (beta * eg) + dkt * ekt + _tn(_bf(dq2g), _bf(q))
              + _nn(_bf(dp + dp.T), _bf(k)))
        dq_ref[...] = dqd * eg + _nn(_bf(dq2g), _bf(k))
        dk_ref[...] = dk
        e = dp * p + dq2g * qk_raw
        t_kt = rsum(dkt * kt)
        dg = rsum(dqd * qd) + rsum(dr_w * r_w) - t_kt + rsum(e) - rsum(e.T)
        dgl = jnp.sum(t_kt, axis=0, keepdims=True) + jnp.sum(dst * st, keepdims=True) * egl[:, :1]
        rowc = lax.broadcasted_iota(jnp.int32, (C, 1), 0)
        dg = dg + jnp.where(rowc == C - 1, dgl, 0.0)
        dla = _nn(_tri(C, "upper").astype(F32), jnp.broadcast_to(dg, (C, HD)), HI)[:, :1]
        da = dla * dla_da
        db = dbeta * beta * (1.0 - beta)
        lane = lax.broadcasted_iota(jnp.int32, (C, HD), 1)
        dab_ref[...] += jnp.where(lane == h, da, 0.0) + jnp.where(lane == nh + h, db, 0.0)
        lane1 = lax.broadcasted_iota(jnp.int32, (1, HD), 1)
        d_alog = jnp.sum(dla * la, axis=0, keepdims=True)
        d_dtb = jnp.sum(da, axis=0, keepdims=True)
        dpar_ref[0:1, :] += jnp.where(lane1 == h, d_alog, 0.0)
        dpar_ref[1:2, :] += jnp.where(lane1 == h, d_dtb, 0.0)
        ds_sc[h] = dst * egl + _tn(_bf(do_), _bf(qd)) - _tn(_bf(dvn), _bf(w))

    rblk = lambda off: pl.BlockSpec((C, HD), lambda c, h: (nc - 1 - c, off + h))
    oblk = pl.BlockSpec((C, HD), lambda c, h: (nc - 1 - c, h))
    vec = pl.BlockSpec((1, HD), lambda c, h: (0, 0))
    return pl.pallas_call(
        body, name=name, grid=(nc, nh),
        in_specs=[rblk(0), rblk(nh), rblk(2 * nh),
                  pl.BlockSpec((C, HD), lambda c, h: (nc - 1 - c, ab_blk)), vec, vec, oblk,
                  pl.BlockSpec((1, 1, C, C), lambda c, h: (nc - 1 - c, h, 0, 0)),
                  pl.BlockSpec((1, 1, HD, HD), lambda c, h: (nc - 1 - c, h, 0, 0))],
        out_specs=[oblk, oblk, oblk,
                   pl.BlockSpec((C, HD), lambda c, h: (nc - 1 - c, 0)),
                   pl.BlockSpec((8, HD), lambda c, h: (0, 0))],
        out_shape=[jax.ShapeDtypeStruct((t, nh * HD), F32)] * 3
        + [jax.ShapeDtypeStruct((t, HD), F32), jax.ShapeDtypeStruct((8, HD), F32)],
        scratch_shapes=[pltpu.VMEM((nh, HD, HD), F32)],
        compiler_params=_cp(("arbitrary", "arbitrary")),
    )(qkv, qkv, qkv, proj, alog, dtb, do, x_sv, st_sv)


def _ada_fwd(c_all, w, b, name):
    nb, d = c_all.shape
    n = w.shape[1]
    tn = _pick(n, 512)

    def body(c_ref, w_ref, b_ref, o_ref):
        cv = c_ref[...]
        o_ref[...] = _nn(cv * _sigmoid(cv), w_ref[...], HI) + b_ref[...]

    return pl.pallas_call(
        body, name=name, grid=(n // tn,),
        in_specs=[pl.BlockSpec((nb, d), lambda j: (0, 0)), pl.BlockSpec((d, tn), lambda j: (0, j)),
                  pl.BlockSpec((1, tn), lambda j: (0, j))],
        out_specs=pl.BlockSpec((nb, tn), lambda j: (0, j)),
        out_shape=jax.ShapeDtypeStruct((nb, n), F32),
        compiler_params=_cp(("parallel",)),
    )(c_all, w, b)


def _ada_wgrad(c_all, dmod, name):
    nb, d = c_all.shape
    n = dmod.shape[1]
    tn = _pick(n, 512)

    def body(c_ref, g_ref, o_ref):
        cv = c_ref[...]
        o_ref[...] = _tn(cv * _sigmoid(cv), g_ref[...], HI)

    return pl.pallas_call(
        body, name=name, grid=(n // tn,),
        in_specs=[pl.BlockSpec((nb, d), lambda j: (0, 0)), pl.BlockSpec((nb, tn), lambda j: (0, j))],
        out_specs=pl.BlockSpec((d, tn), lambda j: (0, j)),
        out_shape=jax.ShapeDtypeStruct((d, n), F32),
        compiler_params=_cp(("parallel",)),
    )(c_all, dmod)


def _adamw(w, m, v, g, name, parts=False):
    r, cdim = w.shape
    tr = r if r <= 256 else _pick_rows(r, 256)
    bc1 = 1.0 - ADAM_B1 ** ADAM_STEP
    bc2 = 1.0 - ADAM_B2 ** ADAM_STEP

    def body(w_ref, m_ref, v_ref, g_ref, go_ref, d_ref, mo_ref, vo_ref):
        if parts:
            gv = g_ref[0].astype(F32)
            for s in range(1, N_DEV):
                gv = gv + g_ref[s].astype(F32)
        else:
            gv = g_ref[...]
        wv = w_ref[...]
        mn = ADAM_B1 * m_ref[...] + (1.0 - ADAM_B1) * gv
        vn = ADAM_B2 * v_ref[...] + (1.0 - ADAM_B2) * (gv * gv)
        m_hat = mn / bc1
        v_hat = vn / bc2
        go_ref[...] = gv
        d_ref[...] = -ADAM_LR * (m_hat / (jnp.sqrt(v_hat) + ADAM_EPS) + ADAM_WD * wv)
        mo_ref[...] = mn
        vo_ref[...] = vn

    spec = pl.BlockSpec((tr, cdim), lambda i: (i, 0))
    gspec = pl.BlockSpec((N_DEV, tr, cdim), lambda i: (0, i, 0)) if parts else spec
    return pl.pallas_call(
        body, name=name, grid=(r // tr,),
        in_specs=[spec, spec, spec, gspec],
        out_specs=[spec] * 4,
        out_shape=[jax.ShapeDtypeStruct((r, cdim), F32)] * 4,
        compiler_params=_cp(("parallel",)),
    )(w, m, v, g)


def _pick_rows(r, pref):
    t = pref
    while r % t:
        t -= 8
    assert t > 0
    return t


def _dev_index(x, y, c):
    return 4 * x + 2 * y + c


def _all_gather(arrays, name):
    n = len(arrays)

    def body(*refs):
        ins, outs = refs[:n], refs[n:2 * n]
        send_sems, recv_sems, local_sems = refs[2 * n:]
        x, y, c = lax.axis_index("x"), lax.axis_index("y"), lax.axis_index("c")
        me, sibling = (x, y, c), (x, y, 1 - c)
        chips = [(1 - x, y), (x, 1 - y), (1 - x, 1 - y)]

        def copy(a, k, block, to, src=None):
            slot = outs[a].at[_dev_index(*block)]
            return pltpu.make_async_remote_copy(
                src_ref=slot if src is None else src, dst_ref=slot,
                send_sem=send_sems.at[a, k], recv_sem=recv_sems.at[a, k],
                device_id=to, device_id_type=MESH)

        mine = [pltpu.make_async_copy(ins[a], outs[a].at[_dev_index(*me)], local_sems.at[a]) for a in range(n)]
        for cp in mine:
            cp.start()
        first = []
        for a in range(n):
            first.append(copy(a, 0, me, sibling, src=ins[a]))
            first += [copy(a, 1 + j, me, (*chip, c), src=ins[a]) for j, chip in enumerate(chips)]
        for cp in first:
            cp.start()
        passed = []
        for j, chip in enumerate(chips):
            for a in range(n):
                copy(a, 1 + j, (*chip, c), me).wait_recv()
                fwd = copy(a, 4 + j, (*chip, c), sibling)
                fwd.start()
                passed.append(fwd)
        for a in range(n):
            copy(a, 0, sibling, me).wait_recv()
            for j, chip in enumerate(chips):
                copy(a, 4 + j, (*chip, 1 - c), me).wait_recv()
        for cp in first + passed:
            cp.wait_send()
        for cp in mine:
            cp.wait()

    return pl.pallas_call(
        body, name=name,
        in_specs=[ANY] * n, out_specs=[ANY] * n,
        out_shape=[jax.ShapeDtypeStruct((N_DEV,) + a.shape, a.dtype) for a in arrays],
        scratch_shapes=[pltpu.SemaphoreType.DMA((n, 7)), pltpu.SemaphoreType.DMA((n, 7)),
                        pltpu.SemaphoreType.DMA((n,))],
    )(*arrays)


def _exchange_parts(arrays, name):
    n = len(arrays)

    def body(*refs):
        ins, outs = refs[:n], refs[n:2 * n]
        send_sems, recv_sems, local_sems = refs[2 * n:]
        x, y, c = lax.axis_index("x"), lax.axis_index("y"), lax.axis_index("c")
        my = _dev_index(x, y, c)
        mine = [pltpu.make_async_copy(ins[a].at[my], outs[a].at[my], local_sems.at[a]) for a in range(n)]
        for cp in mine:
            cp.start()
        sends = []
        for k in range(1, N_DEV):
            fx, fy, fc = (k >> 2) & 1, (k >> 1) & 1, k & 1
            px = (1 - x) if fx else x
            py = (1 - y) if fy else y
            pc = (1 - c) if fc else c
            for a in range(n):
                cp = pltpu.make_async_remote_copy(
                    src_ref=ins[a].at[_dev_index(px, py, pc)], dst_ref=outs[a].at[my],
                    send_sem=send_sems.at[a, k - 1], recv_sem=recv_sems.at[a, k - 1],
                    device_id=(px, py, pc), device_id_type=MESH)
                cp.start()
                sends.append((cp, a, k, _dev_index(px, py, pc)))
        for cp, a, k, peer in sends:
            cp.wait_send()
            pltpu.make_async_remote_copy(
                src_ref=ins[a].at[my], dst_ref=outs[a].at[peer],
                send_sem=send_sems.at[a, k - 1], recv_sem=recv_sems.at[a, k - 1],
                device_id=(x, y, c), device_id_type=MESH).wait_recv()
        for cp in mine:
            cp.wait()

    return pl.pallas_call(
        body, name=name,
        in_specs=[ANY] * n, out_specs=[ANY] * n,
        out_shape=[jax.ShapeDtypeStruct(a.shape, a.dtype) for a in arrays],
        scratch_shapes=[pltpu.SemaphoreType.DMA((n, 7)), pltpu.SemaphoreType.DMA((n, 7)),
                        pltpu.SemaphoreType.DMA((n,))],
    )(*arrays)


def _local_step(x, tgt, mod, n1, n2, n3, n4, w_in_p, lb_logits, hg_norm, conv_w, alog, dtb, gdn_norm,
                w_out, w_ff1, w_ff2):
    t, d = x.shape
    nh = d // 2 // HD
    ab_blk = 8 * nh
    sh_m, sc_m, gt_m, sh_f, sc_f, gt_f = [mod[i:i + 1] for i in range(6)]

    h1, r1 = _prenorm(x, n1, sc_m, sh_m, "prenorm_mix")
    proj = _mm(h1, w_in_p, "nn", [F32], "mm_proj")
    o_hg, a_sv, hst_sv = _hgrn2_fwd(proj, lb_logits, nh, "hgrn2_fwd")
    qkv = _gdn_prep(proj, conv_w, 4 * nh, nh, "gdn_prep")
    o_gd, x_sv, gst_sv = _gdn_fwd(qkv, proj, ab_blk, alog, dtb, nh, "gdn_fwd")
    om_hg = _headnorm_fwd(o_hg, proj, 3 * nh, hg_norm, "headnorm_hg")
    om_gd = _headnorm_fwd(o_gd, proj, 7 * nh, gdn_norm, "headnorm_gdn")
    om = jnp.concatenate([om_hg, om_gd], axis=1)
    y1 = _mm(om, w_out, "nn", [F32], "mm_out")
    x1, r2 = _postnorm_res(x, y1, n2, gt_m, "postnorm_mix")
    h2, r3 = _prenorm(x1, n3, sc_f, sh_f, "prenorm_ffn")

    def relu2(acc, extra, outs):
        outs[0][...] = acc
        rl = jnp.maximum(acc, 0.0)
        outs[1][...] = (rl * rl).astype(BF16)

    u, act = _mm(h2, w_ff1, "nn", [F32, BF16], "mm_ff1", epilogue=relu2)
    y2 = _mm(act, w_ff2, "nn", [F32], "mm_ff2")
    dout, r4, loss = _final_loss(x1, y2, n4, gt_f, tgt, "final_loss")

    dy2, dgt_f, dn4 = _postnorm_bwd(dout, y2, r4, n4, gt_f, "postnorm_ffn_bwd")
    dw_ff2 = _mm(act, dy2, "tn", [F32], "mm_dw_ff2")

    def drelu2(acc, extra, outs):
        outs[0][...] = (acc * (2.0 * jnp.maximum(extra[0][...], 0.0))).astype(BF16)

    du = _mm(dy2, w_ff2, "nt", [BF16], "mm_da", epilogue=drelu2, extras=(u,))
    dw_ff1 = _mm(h2, du, "tn", [F32], "mm_dw_ff1")
    dh2 = _mm(du, w_ff1, "nt", [F32], "mm_dh2")
    dx1, dsh_f, dsc_f, dn3 = _prenorm_bwd(dh2, x1, r3, n3, sc_f, dout, "prenorm_ffn_bwd")

    dy1, dgt_m, dn2 = _postnorm_bwd(dx1, y1, r2, n2, gt_m, "postnorm_mix_bwd")
    dw_out = _mm(om, dy1, "tn", [F32], "mm_dw_out")
    dom = _mm(dy1, w_out, "nt", [F32], "mm_dom")
    do_hg, dg_hg, dhgn = _headnorm_bwd(dom, 0, o_hg, proj, 3 * nh, hg_norm, "headnorm_hg_bwd")
    do_gd, dg_gd, dgdn = _headnorm_bwd(dom, 1, o_gd, proj, 7 * nh, gdn_norm, "headnorm_gdn_bwd")
    dq_hg, df_hg, di_hg, dl0 = _hgrn2_bwd(proj, lb_logits, do_hg, a_sv, hst_sv, nh, "hgrn2_bwd")
    dq_g, dk_g, dv_g, dab, dpar = _gdn_bwd(qkv, proj, ab_blk, alog, dtb, do_gd, x_sv, gst_sv, nh, "gdn_bwd")
    dqkv = jnp.concatenate([dq_g, dk_g, dv_g], axis=1)
    du_conv, dconv = _gdn_prep_bwd(proj, conv_w, dqkv, 4 * nh, nh, "gdn_prep_bwd")
    dproj = jnp.concatenate([dq_hg, df_hg, di_hg, dg_hg, du_conv, dg_gd, dab.astype(BF16)], axis=1)
    dw_in = _mm(h1, dproj, "tn", [F32], "mm_dw_in")
    dh1 = _mm(dproj, w_in_p, "nt", [F32], "mm_dh1", tk=640)
    dx, dsh_m, dsc_m, dn1 = _prenorm_bwd(dh1, x, r1, n1, sc_m, dx1, "prenorm_mix_bwd")

    dmod = jnp.concatenate([dsh_m, dsc_m, dgt_m, dsh_f, dsc_f, dgt_f], axis=0)
    grads = dict(dmod=dmod, n1=dn1, n2=dn2, n3=dn3, n4=dn4, w_in=dw_in, lb0=dl0, hg_norm=dhgn, conv=dconv,
                 alog=dpar[0:1], dtb=dpar[1:2], gdn_norm=dgdn, w_out=dw_out, w_ff1=dw_ff1, w_ff2=dw_ff2)
    return loss, dx, grads


def _pack(vals):
    rows = []
    for vv in vals:
        flat = vv.reshape(-1)
        pad = (-flat.shape[0]) % LANES
        if pad:
            flat = jnp.concatenate([flat, jnp.zeros((pad,), flat.dtype)])
        rows.append(flat.reshape(-1, LANES))
    return jnp.concatenate(rows, axis=0)


def _unpack(packed, shapes):
    out, r = [], 0
    for shp in shapes:
        size = 1
        for s in shp:
            size *= s
        nr = -(-size // LANES)
        out.append(packed[r:r + nr].reshape(-1)[:size].reshape(shp))
        r += nr
    return out


def _sum_parts(parts, name):
    _, r, cdim = parts.shape

    def body(p_ref, o_ref):
        acc = p_ref[0]
        for s in range(1, N_DEV):
            acc = acc + p_ref[s]
        o_ref[...] = acc

    return pl.pallas_call(
        body, name=name,
        out_shape=jax.ShapeDtypeStruct((r, cdim), F32),
        compiler_params=_cp(),
    )(parts)


def kernel(x, c, w_ada, b_ada, pre_mix_norm, post_mix_norm, pre_ffn_norm, post_ffn_norm, w_in, hg_lb_logits, hg_norm, gdn_conv_w, gdn_a_log, gdn_dt_bias, gdn_norm, w_out, w_ff1, w_ff2, loss_target, m_w_ada, m_b_ada, m_pre_mix_norm, m_post_mix_norm, m_pre_ffn_norm, m_post_ffn_norm, m_w_in, m_hg_lb_logits, m_hg_norm, m_gdn_conv_w, m_gdn_a_log, m_gdn_dt_bias, m_gdn_norm, m_w_out, m_w_ff1, m_w_ff2, v_w_ada, v_b_ada, v_pre_mix_norm, v_post_mix_norm, v_pre_ffn_norm, v_post_ffn_norm, v_w_in, v_hg_lb_logits, v_hg_norm, v_gdn_conv_w, v_gdn_a_log, v_gdn_dt_bias, v_gdn_norm, v_w_out, v_w_ff1, v_w_ff2):
    t, d = x.shape[1], x.shape[2]
    nh = d // 2 // HD
    in_cols = w_in.shape[2] * N_DEV
    main = in_cols - 2 * nh
    me = _dev_index(lax.axis_index("x"), lax.axis_index("y"), lax.axis_index("c"))

    c_all, conv_g = _all_gather([c, gdn_conv_w[0]], "gather_small")
    c_all = c_all.reshape(N_DEV, d)
    conv_full = conv_g.transpose(1, 0, 2).reshape(CONV_K, -1)
    w_in_g, w_out_g, w_ff1_g, w_ff2_g = _all_gather(
        [w_in[0].astype(BF16), w_out[0].astype(BF16), w_ff1[0].astype(BF16), w_ff2[0].astype(BF16)],
        "gather_weights")
    w_in_full = w_in_g.transpose(1, 0, 2).reshape(d, in_cols)
    w_in_p = jnp.concatenate([w_in_full, jnp.zeros((d, LANES - 2 * nh), BF16)], axis=1)
    w_out_full = w_out_g.reshape(d, d)
    w_ff1_full = w_ff1_g.transpose(1, 0, 2).reshape(d, -1)
    w_ff2_full = w_ff2_g.reshape(-1, d)

    n_ada = w_ada.shape[2]
    b_loc = lax.dynamic_slice(b_ada, (0, me * n_ada), (1, n_ada))
    mod_part = _ada_fwd(c_all, w_ada[0], b_loc, "ada_fwd")
    mod_all = _all_gather([mod_part], "gather_mod")[0]
    mod = lax.dynamic_slice(mod_all, (0, me, 0), (N_DEV, 1, n_ada)).reshape(6, d)

    pad_lane = lambda vv: jnp.concatenate([vv, jnp.zeros((1, LANES - vv.shape[1]), F32)], axis=1)
    loss, dx, g = _local_step(
        x[0], loss_target[0], mod, pre_mix_norm, post_mix_norm, pre_ffn_norm, post_ffn_norm, w_in_p,
        hg_lb_logits, hg_norm, conv_full, pad_lane(gdn_a_log), pad_lane(gdn_dt_bias), gdn_norm,
        w_out_full, w_ff1_full, w_ff2_full)

    rep_names = ["b_ada", "n1", "n2", "n3", "n4", "lb", "hg_norm", "alog", "dtb", "gdn_norm"]
    rep_w = [b_ada, pre_mix_norm, post_mix_norm, pre_ffn_norm, post_ffn_norm, hg_lb_logits, hg_norm,
             gdn_a_log, gdn_dt_bias, gdn_norm]
    rep_m = [m_b_ada, m_pre_mix_norm, m_post_mix_norm, m_pre_ffn_norm, m_post_ffn_norm, m_hg_lb_logits,
             m_hg_norm, m_gdn_a_log, m_gdn_dt_bias, m_gdn_norm]
    rep_v = [v_b_ada, v_pre_mix_norm, v_post_mix_norm, v_pre_ffn_norm, v_post_ffn_norm, v_hg_lb_logits,
             v_hg_norm, v_gdn_a_log, v_gdn_dt_bias, v_gdn_norm]
    rep_shapes = [a.shape for a in rep_w]
    g_lb = jnp.stack([g["lb0"], -g["lb0"]], axis=0)
    rep_g = [g["dmod"], g["n1"], g["n2"], g["n3"], g["n4"], g_lb, g["hg_norm"],
             g["alog"][:, :nh], g["dtb"][:, :nh], g["gdn_norm"]]
    small = _pack(rep_g + [g["conv"]])
    n_rep_rows = _pack(rep_g).shape[0]
    pad_rows = (-small.shape[0]) % 8
    if pad_rows:
        small = jnp.concatenate([small, jnp.zeros((pad_rows, LANES), F32)], axis=0)
    small_all = _all_gather([small], "gather_small_grads")[0]
    small_sum = _sum_parts(small_all, "sum_small_grads")
    rep_out = _adamw(_pack(rep_w), _pack(rep_m), _pack(rep_v), small_sum[:n_rep_rows], "adamw_small")
    rep_g_o, rep_d_o, rep_m_o, rep_v_o = [dict(zip(rep_names, _unpack(p, rep_shapes))) for p in rep_out]

    conv_sum = small_sum[n_rep_rows:n_rep_rows + CONV_K * conv_full.shape[1] // LANES].reshape(CONV_K, -1)
    n_conv = gdn_conv_w.shape[2]
    conv_loc = lax.dynamic_slice(conv_sum, (0, me * n_conv), (CONV_K, n_conv))
    conv_o = _adamw(gdn_conv_w[0], m_gdn_conv_w[0], v_gdn_conv_w[0], conv_loc, "adamw_conv")

    dmod_all = small_all[:, :6 * d // LANES, :].reshape(N_DEV, 6 * d)
    dmod_loc = lax.dynamic_slice(dmod_all, (0, me * n_ada), (N_DEV, n_ada))
    g_ada = _ada_wgrad(c_all, dmod_loc, "ada_wgrad")
    ada_o = _adamw(w_ada[0], m_w_ada[0], v_w_ada[0], g_ada, "adamw_ada")

    n_in = w_in.shape[2]
    n_ff = w_ff1.shape[2]
    p_in = g["w_in"][:, :in_cols].reshape(d, N_DEV, n_in).transpose(1, 0, 2).astype(BF16)
    p_out = g["w_out"].reshape(N_DEV, d // N_DEV, d).astype(BF16)
    p_ff1 = g["w_ff1"].reshape(d, N_DEV, n_ff).transpose(1, 0, 2).astype(BF16)
    p_ff2 = g["w_ff2"].reshape(N_DEV, -1, d).astype(BF16)
    r_in, r_out, r_ff1, r_ff2 = _exchange_parts([p_in, p_out, p_ff1, p_ff2], "exchange_grads")
    in_o = _adamw(w_in[0], m_w_in[0], v_w_in[0], r_in, "adamw_w_in", parts=True)
    out_o = _adamw(w_out[0], m_w_out[0], v_w_out[0], r_out, "adamw_w_out", parts=True)
    ff1_o = _adamw(w_ff1[0], m_w_ff1[0], v_w_ff1[0], r_ff1, "adamw_w_ff1", parts=True)
    ff2_o = _adamw(w_ff2[0], m_w_ff2[0], v_w_ff2[0], r_ff2, "adamw_w_ff2", parts=True)

    loss_tot = lax.psum(loss[0, 0], ("x", "y", "c"))

    def leaf(kind):
        return [ada_o[kind][None], rep_out_d[kind]["b_ada"], rep_out_d[kind]["n1"], rep_out_d[kind]["n2"],
                rep_out_d[kind]["n3"], rep_out_d[kind]["n4"], in_o[kind][None], rep_out_d[kind]["lb"],
                rep_out_d[kind]["hg_norm"], conv_o[kind][None], rep_out_d[kind]["alog"], rep_out_d[kind]["dtb"],
                rep_out_d[kind]["gdn_norm"], out_o[kind][None], ff1_o[kind][None], ff2_o[kind][None]]

    rep_out_d = [rep_g_o, rep_d_o, rep_m_o, rep_v_o]
    return (loss_tot, dx[None], *leaf(0), *leaf(1), *leaf(2), *leaf(3))
```

```python
import functools

import jax
import jax.numpy as jnp
from jax import lax
from jax.experimental import pallas as pl
from jax.experimental.pallas import tpu as pltpu

F32 = jnp.float32
BF16 = jnp.bfloat16
HI = lax.Precision.HIGHEST
HIGH = lax.Precision.HIGH

EPS = 1e-6
CHUNK = 64
SB = 16
NSB = CHUNK // SB
HP = 2
HD = 128
CONV_K = 4
N_DEV = 8
LANES = 128
VMEM_LIMIT = 56 * 1024 * 1024

ADAM_LR = 0.001
ADAM_B1 = 0.9
ADAM_B2 = 0.999
ADAM_EPS = 1e-08
ADAM_WD = 0.01
ADAM_STEP = 10

ANY = pl.BlockSpec(memory_space=pl.ANY)
MESH = pl.DeviceIdType.MESH


def _cp(sem=None):
    return pltpu.CompilerParams(dimension_semantics=sem, vmem_limit_bytes=VMEM_LIMIT)


def _dot(a, b, dims, precision=None):
    return lax.dot_general(a, b, (dims, ((), ())), precision=precision, preferred_element_type=F32)


def _nn(a, b, precision=None):
    return _dot(a, b, ((1,), (0,)), precision)


def _nt(a, b, precision=None):
    return _dot(a, b, ((1,), (1,)), precision)


def _tn(a, b, precision=None):
    return _dot(a, b, ((0,), (0,)), precision)


def _bf(x):
    return x.astype(BF16)


def _sigmoid(x):
    return 1.0 / (1.0 + jnp.exp(-x))


def _pick(n, pref):
    if n <= pref:
        return n
    t = pref
    while n % t:
        t -= LANES
    assert t > 0, (n, pref)
    return t


def _mm(a, b, mode, out_dtypes, name, epilogue=None, extras=(), tm=1024, tn=2048, tk=512, comm=None):
    if mode == "nn":
        (m, kd), (_, n) = a.shape, b.shape
    elif mode == "nt":
        (m, kd), (n, _) = a.shape, b.shape
    else:
        (kd, m), (_, n) = a.shape, b.shape
    tm, tn, tk = _pick(m, tm), _pick(n, tn), _pick(kd, tk)
    nk = kd // tk
    if mode == "nn":
        a_spec = pl.BlockSpec((tm, tk), lambda i, j, k: (i, k))
        b_spec = pl.BlockSpec((tk, tn), lambda i, j, k: (k, j))
        dims = ((1,), (0,))
    elif mode == "nt":
        a_spec = pl.BlockSpec((tm, tk), lambda i, j, k: (i, k))
        b_spec = pl.BlockSpec((tn, tk), lambda i, j, k: (j, k))
        dims = ((1,), (1,))
    else:
        a_spec = pl.BlockSpec((tk, tm), lambda i, j, k: (k, i))
        b_spec = pl.BlockSpec((tk, tn), lambda i, j, k: (k, j))
        dims = ((0,), (0,))
    o_spec = pl.BlockSpec((tm, tn), lambda i, j, k: (i, j))
    n_extra, n_out = len(extras), len(out_dtypes)

    gm, gn = m // tm, n // tn
    cn = comm.n if comm is not None else 0

    def body(*refs):
        i, j, k = pl.program_id(0), pl.program_id(1), pl.program_id(2)
        at0 = (j == 0) & (k == 0)
        ins, out_refs, scratch, comm_begin, comm_end = _comm_hooks(
            comm, refs, 2 + n_extra, n_out, (i == 0) & at0, (i == gm // 2) & at0,
            (i == gm - 1) & (j == gn - 1) & (k == nk - 1))
        a_ref, b_ref, extra_refs = ins[0], ins[1], ins[2:]
        acc, = scratch
        comm_begin()

        @pl.when(k == 0)
        def _():
            acc[...] = jnp.zeros_like(acc)

        acc[...] += _dot(a_ref[...], b_ref[...], dims)

        @pl.when(k == nk - 1)
        def _():
            if epilogue is None:
                out_refs[0][...] = acc[...].astype(out_dtypes[0])
            else:
                epilogue(acc[...], extra_refs, out_refs)

        comm_end()

    sem = ("arbitrary",) * 3 if cn else ("parallel", "parallel", "arbitrary")
    outs = pl.pallas_call(
        body, name=name,
        grid=(gm, gn, nk),
        in_specs=[a_spec, b_spec] + [o_spec] * n_extra + [ANY] * cn,
        out_specs=[o_spec] * n_out + [ANY] * cn,
        out_shape=[jax.ShapeDtypeStruct((m, n), dt) for dt in out_dtypes] + (comm.out_shapes() if cn else []),
        scratch_shapes=[pltpu.VMEM((tm, tn), F32)] + (comm.scratch() if cn else []),
        compiler_params=_cp(sem),
    )(a, b, *extras, *(comm.arrays if cn else []))
    return outs[0] if n_out + cn == 1 else outs


def _row_spec(tb, d):
    return pl.BlockSpec((tb, d), lambda i: (i, 0))


def _vec_spec(d):
    return pl.BlockSpec((1, d), lambda i: (0, 0))


def _prenorm(x, w, sc, sh, name):
    t, d = x.shape
    tb = _pick(t, 256)

    def body(x_ref, w_ref, sc_ref, sh_ref, h_ref, r_ref):
        xv = x_ref[...]
        r = lax.rsqrt(jnp.mean(xv * xv, axis=-1, keepdims=True) + EPS)
        h_ref[...] = ((xv * r * w_ref[...]) * (1.0 + sc_ref[...]) + sh_ref[...]).astype(BF16)
        r_ref[...] = r

    return pl.pallas_call(
        body, name=name, grid=(t // tb,),
        in_specs=[_row_spec(tb, d), _vec_spec(d), _vec_spec(d), _vec_spec(d)],
        out_specs=[_row_spec(tb, d), _row_spec(tb, 1)],
        out_shape=[jax.ShapeDtypeStruct((t, d), BF16), jax.ShapeDtypeStruct((t, 1), F32)],
        compiler_params=_cp(("parallel",)),
    )(x, w, sc, sh)


def _postnorm_res(x, y, w, gt, name):
    t, d = x.shape
    tb = _pick(t, 256)

    def body(x_ref, y_ref, w_ref, gt_ref, o_ref, r_ref):
        yv = y_ref[...]
        r = lax.rsqrt(jnp.mean(yv * yv, axis=-1, keepdims=True) + EPS)
        o_ref[...] = x_ref[...] + gt_ref[...] * (yv * r * w_ref[...])
        r_ref[...] = r

    return pl.pallas_call(
        body, name=name, grid=(t // tb,),
        in_specs=[_row_spec(tb, d), _row_spec(tb, d), _vec_spec(d), _vec_spec(d)],
        out_specs=[_row_spec(tb, d), _row_spec(tb, 1)],
        out_shape=[jax.ShapeDtypeStruct((t, d), F32), jax.ShapeDtypeStruct((t, 1), F32)],
        compiler_params=_cp(("parallel",)),
    )(x, y, w, gt)


def _final_loss(x, y, w, gt, tgt, name):
    t, d = x.shape
    tb = _pick(t, 256)

    def body(x_ref, y_ref, w_ref, gt_ref, tgt_ref, dout_ref, r_ref, loss_ref):
        @pl.when(pl.program_id(0) == 0)
        def _():
            loss_ref[...] = jnp.zeros_like(loss_ref)

        yv = y_ref[...]
        r = lax.rsqrt(jnp.mean(yv * yv, axis=-1, keepdims=True) + EPS)
        out = x_ref[...] + gt_ref[...] * (yv * r * w_ref[...])
        diff = out - tgt_ref[...]
        row = jnp.mean(diff * diff, axis=-1, keepdims=True)
        loss_ref[...] += 0.5 * jnp.sum(row, axis=0, keepdims=True)
        dout_ref[...] = diff * (1.0 / d)
        r_ref[...] = r

    return pl.pallas_call(
        body, name=name, grid=(t // tb,),
        in_specs=[_row_spec(tb, d), _row_spec(tb, d), _vec_spec(d), _vec_spec(d), _row_spec(tb, d)],
        out_specs=[_row_spec(tb, d), _row_spec(tb, 1), pl.BlockSpec((1, 1), lambda i: (0, 0))],
        out_shape=[jax.ShapeDtypeStruct((t, d), F32), jax.ShapeDtypeStruct((t, 1), F32),
                   jax.ShapeDtypeStruct((1, 1), F32)],
        compiler_params=_cp(("arbitrary",)),
    )(x, y, w, gt, tgt)


def _postnorm_bwd(dxn, y, r, w, gt, name):
    t, d = y.shape
    tb = _pick(t, 256)

    def body(dx_ref, y_ref, r_ref, w_ref, gt_ref, dy_ref, dgt_ref, dw_ref):
        @pl.when(pl.program_id(0) == 0)
        def _():
            dgt_ref[...] = jnp.zeros_like(dgt_ref)
            dw_ref[...] = jnp.zeros_like(dw_ref)

        dxv, rv, wv = dx_ref[...], r_ref[...], w_ref[...]
        z = y_ref[...] * rv
        dgt_ref[...] += jnp.sum(dxv * (z * wv), axis=0, keepdims=True)
        dn = dxv * gt_ref[...]
        dw_ref[...] += jnp.sum(dn * z, axis=0, keepdims=True)
        dz = dn * wv
        dy_ref[...] = (rv * (dz - z * jnp.mean(dz * z, axis=-1, keepdims=True))).astype(BF16)

    return pl.pallas_call(
        body, name=name, grid=(t // tb,),
        in_specs=[_row_spec(tb, d), _row_spec(tb, d), _row_spec(tb, 1), _vec_spec(d), _vec_spec(d)],
        out_specs=[_row_spec(tb, d), _vec_spec(d), _vec_spec(d)],
        out_shape=[jax.ShapeDtypeStruct((t, d), BF16), jax.ShapeDtypeStruct((1, d), F32),
                   jax.ShapeDtypeStruct((1, d), F32)],
        compiler_params=_cp(("arbitrary",)),
    )(dxn, y, r, w, gt)


def _prenorm_bwd(dh, x, r, w, sc, dres, name):
    t, d = x.shape
    tb = _pick(t, 256)

    def body(dh_ref, x_ref, r_ref, w_ref, sc_ref, dres_ref, dx_ref, dsh_ref, dsc_ref, dw_ref):
        @pl.when(pl.program_id(0) == 0)
        def _():
            dsh_ref[...] = jnp.zeros_like(dsh_ref)
            dsc_ref[...] = jnp.zeros_like(dsc_ref)
            dw_ref[...] = jnp.zeros_like(dw_ref)

        dhv, rv, wv = dh_ref[...], r_ref[...], w_ref[...]
        z = x_ref[...] * rv
        dsh_ref[...] += jnp.sum(dhv, axis=0, keepdims=True)
        dsc_ref[...] += jnp.sum(dhv * (z * wv), axis=0, keepdims=True)
        dzw = dhv * (1.0 + sc_ref[...])
        dw_ref[...] += jnp.sum(dzw * z, axis=0, keepdims=True)
        dz = dzw * wv
        dx_ref[...] = dres_ref[...] + rv * (dz - z * jnp.mean(dz * z, axis=-1, keepdims=True))

    return pl.pallas_call(
        body, name=name, grid=(t // tb,),
        in_specs=[_row_spec(tb, d), _row_spec(tb, d), _row_spec(tb, 1), _vec_spec(d), _vec_spec(d),
                  _row_spec(tb, d)],
        out_specs=[_row_spec(tb, d), _vec_spec(d), _vec_spec(d), _vec_spec(d)],
        out_shape=[jax.ShapeDtypeStruct((t, d), F32)] + [jax.ShapeDtypeStruct((1, d), F32)] * 3,
        compiler_params=_cp(("arbitrary",)),
    )(dh, x, r, w, sc, dres)


def _headnorm_fwd(o, proj, g_blk, nw, name):
    t, wd = o.shape
    nh = wd // HD
    tb = _pick(t, 512)
    gb = g_blk * HD // wd

    def body(o_ref, g_ref, nw_ref, out_ref):
        o3 = o_ref[...].reshape(tb, nh, HD)
        g3 = g_ref[...].reshape(tb, nh, HD)
        rh = lax.rsqrt(jnp.mean(o3 * o3, axis=-1, keepdims=True) + EPS)
        res = (o3 * rh * nw_ref[...].reshape(1, 1, HD)) * (g3 * _sigmoid(g3))
        out_ref[...] = res.reshape(tb, wd).astype(BF16)

    return pl.pallas_call(
        body, name=name, grid=(t // tb,),
        in_specs=[_row_spec(tb, wd), pl.BlockSpec((tb, wd), lambda i: (i, gb)), _vec_spec(HD)],
        out_specs=_row_spec(tb, wd),
        out_shape=jax.ShapeDtypeStruct((t, wd), BF16),
        compiler_params=_cp(("parallel",)),
    )(o, proj, nw)


def _headnorm_bwd(dom, col_blk, o, proj, g_blk, nw, name):
    t, wd = o.shape
    nh = wd // HD
    tb = _pick(t, 512)
    gb = g_blk * HD // wd

    def body(do_ref, o_ref, g_ref, nw_ref, dout_ref, dg_ref, dnw_ref):
        @pl.when(pl.program_id(0) == 0)
        def _():
            dnw_ref[...] = jnp.zeros_like(dnw_ref)

        dn = do_ref[...].reshape(tb, nh, HD)
        o3 = o_ref[...].reshape(tb, nh, HD)
        g3 = g_ref[...].reshape(tb, nh, HD)
        nw3 = nw_ref[...].reshape(1, 1, HD)
        rh = lax.rsqrt(jnp.mean(o3 * o3, axis=-1, keepdims=True) + EPS)
        z = o3 * rh
        sg = _sigmoid(g3)
        sl = g3 * sg
        dnw_ref[...] += jnp.sum(jnp.sum(dn * sl * z, axis=1), axis=0, keepdims=True)
        dg_ref[...] = (dn * (z * nw3) * (sg * (1.0 + g3 * (1.0 - sg)))).reshape(tb, wd).astype(BF16)
        dz = dn * sl * nw3
        dout_ref[...] = (rh * (dz - z * jnp.mean(dz * z, axis=-1, keepdims=True))).reshape(tb, wd)

    return pl.pallas_call(
        body, name=name, grid=(t // tb,),
        in_specs=[pl.BlockSpec((tb, wd), lambda i: (i, col_blk)), _row_spec(tb, wd),
                  pl.BlockSpec((tb, wd), lambda i: (i, gb)), _vec_spec(HD)],
        out_specs=[_row_spec(tb, wd), _row_spec(tb, wd), _vec_spec(HD)],
        out_shape=[jax.ShapeDtypeStruct((t, wd), F32), jax.ShapeDtypeStruct((t, wd), BF16),
                   jax.ShapeDtypeStruct((1, HD), F32)],
        compiler_params=_cp(("arbitrary",)),
    )(dom, o, proj, nw)


def _tri(n, kind):
    r = lax.broadcasted_iota(jnp.int32, (n, n), 0)
    c = lax.broadcasted_iota(jnp.int32, (n, n), 1)
    if kind == "lower":
        return r >= c
    if kind == "strict":
        return r > c
    return r <= c


def _hg_gate(fl, lg_ref):
    l0, l1 = lg_ref[0, 0], lg_ref[1, 0]
    mx = jnp.maximum(l0, l1)
    e0, e1 = jnp.exp(l0 - mx), jnp.exp(l1 - mx)
    lb = e0 / (e0 + e1)
    sg = _sigmoid(fl)
    f = lb + (1.0 - lb) * sg
    return lb, sg, f


def _hgrn2_fwd(proj, lb_logits, nh, name):
    t = proj.shape[0]
    nc = t // CHUNK
    C = CHUNK
    lg = lb_logits.reshape(2, nh, 1, HD)

    def body(q_ref, f_ref, i_ref, lg_ref, o_ref, a_ref, st_ref, s_sc, p_sc, r_sc):
        c, h = pl.program_id(0), pl.program_id(1)

        @pl.when(c == 0)
        def _():
            s_sc[h] = jnp.zeros((HD, HD), F32)

        q, v = q_ref[...], i_ref[...]
        _, _, f = _hg_gate(f_ref[...], lg_ref)
        k = 1.0 - f
        low = _tri(C, "lower")
        b = _nn(low.astype(F32), jnp.log(f), HI)
        lane_c = lax.broadcasted_iota(jnp.int32, (SB, C), 1)
        lane_h = lax.broadcasted_iota(jnp.int32, (SB, HD), 1)
        row_h = lax.broadcasted_iota(jnp.int32, (SB, HD), 0)
        ones = jnp.ones((HD, HD), F32)

        for i in range(NSB):
            qi, ki, bi = q[SB * i:SB * (i + 1)], k[SB * i:SB * (i + 1)], b[SB * i:SB * (i + 1)]
            for s in range(SB):
                e = jnp.exp(jnp.minimum(bi - bi[s:s + 1], 0.0))
                p = jnp.where(row_h >= s, qi * ki[s:s + 1] * e, 0.0)
                p_sc[pl.ds((i * SB + s) * SB, SB), :] = p
        r_sc[...] = _nn(p_sc[...], ones, HIGH)
        a_rows = []
        for i in range(NSB):
            acc = jnp.zeros((SB, HD), F32)
            for s in range(SB):
                acc = jnp.where(lane_h == SB * i + s, r_sc[pl.ds((i * SB + s) * SB, SB), :], acc)
            acc = acc[:, :C]
            if i > 0:
                r = b[SB * i - 1:SB * i]
                bi = b[SB * i:SB * (i + 1)]
                qf = q[SB * i:SB * (i + 1)] * jnp.exp(bi - r)
                kf = k * jnp.exp(jnp.minimum(r - b, 0.0))
                acc = acc + jnp.where(lane_c < SB * i, _nt(qf, kf, HIGH), 0.0)
            a_rows.append(acc)
        a = jnp.concatenate(a_rows, axis=0)
        st = s_sc[h]
        bl = b[C - 1:C, :]
        o_ref[...] = _nn(_bf(a), _bf(v)) + _nt(_bf(q * jnp.exp(b)), _bf(st))
        a_ref[0, 0] = a
        st_ref[0, 0] = st
        s_sc[h] = st * jnp.exp(bl) + _tn(_bf(v), _bf(k * jnp.exp(bl - b)))

    blk = lambda off: pl.BlockSpec((C, HD), lambda c, h: (c, off + h))
    return pl.pallas_call(
        body, name=name, grid=(nc, nh),
        in_specs=[blk(0), blk(nh), blk(2 * nh),
                  pl.BlockSpec((2, 1, 1, HD), lambda c, h: (0, h, 0, 0))],
        out_specs=[blk(0),
                   pl.BlockSpec((1, 1, C, C), lambda c, h: (c, h, 0, 0)),
                   pl.BlockSpec((1, 1, HD, HD), lambda c, h: (c, h, 0, 0))],
        out_shape=[jax.ShapeDtypeStruct((t, nh * HD), F32),
                   jax.ShapeDtypeStruct((nc, nh, C, C), F32),
                   jax.ShapeDtypeStruct((nc, nh, HD, HD), F32)],
        scratch_shapes=[pltpu.VMEM((nh, HD, HD), F32), pltpu.VMEM((C * SB, HD), F32), pltpu.VMEM((C * SB, HD), F32)],
        compiler_params=_cp(("arbitrary", "arbitrary")),
    )(proj, proj, proj, lg)


def _hgrn2_bwd(proj, lb_logits, do, a_sv, st_sv, nh, name):
    t = proj.shape[0]
    nc = t // CHUNK
    C = CHUNK
    lg = lb_logits.reshape(2, nh, 1, HD)

    def body(q_ref, f_ref, i_ref, lg_ref, do_ref, a_ref, st_ref,
             dq_ref, df_ref, di_ref, dl_ref, ds_sc, p_sc, r_sc):
        c, h = pl.program_id(0), pl.program_id(1)

        @pl.when(c == 0)
        def _():
            ds_sc[h] = jnp.zeros((HD, HD), F32)

        @pl.when((c == 0) & (h == 0))
        def _():
            dl_ref[...] = jnp.zeros_like(dl_ref)

        q, v, do_ = q_ref[...], i_ref[...], do_ref[...]
        lb, sg, f = _hg_gate(f_ref[...], lg_ref)
        k = 1.0 - f
        low = _tri(C, "lower")
        b = _nn(low.astype(F32), jnp.log(f), HI)
        bl = b[C - 1:C, :]
        eb, ekb = jnp.exp(b), jnp.exp(bl - b)
        qb, kb = q * eb, k * ekb
        a, st, dst = a_ref[0, 0], st_ref[0, 0], ds_sc[h]

        da = jnp.where(low, _nt(_bf(do_), _bf(v)), 0.0)
        dv = _tn(_bf(a), _bf(do_)) + _nt(_bf(kb), _bf(dst))
        dqb = _nn(_bf(do_), _bf(st))
        dkb = _nn(_bf(v), _bf(dst))

        row = lax.broadcasted_iota(jnp.int32, (C, HD), 0)
        lane_c = lax.broadcasted_iota(jnp.int32, (SB, C), 1)
        row_h = lax.broadcasted_iota(jnp.int32, (SB, HD), 0)
        ones = jnp.ones((HD, HD), F32)
        sel = (lax.broadcasted_iota(jnp.int32, (C, C * SB), 0)
               == jnp.right_shift(lax.broadcasted_iota(jnp.int32, (C, C * SB), 1), SB.bit_length() - 1)).astype(F32)

        for i in range(NSB):
            doi, vi = do_[SB * i:SB * (i + 1)], v[SB * i:SB * (i + 1)]
            for s in range(SB):
                p_sc[pl.ds((i * SB + s) * SB, SB), :] = doi * vi[s:s + 1]
        r_sc[...] = _nn(p_sc[...], ones, HIGH)
        dq_rows = []
        dk_off = jnp.zeros((C, HD), F32)
        for i in range(NSB):
            qi, ki, bi = q[SB * i:SB * (i + 1)], k[SB * i:SB * (i + 1)], b[SB * i:SB * (i + 1)]
            acc = jnp.zeros((SB, HD), F32)
            for s in range(SB):
                e = jnp.exp(jnp.minimum(bi - bi[s:s + 1], 0.0))
                g = jnp.where(row_h >= s, r_sc[pl.ds((i * SB + s) * SB, SB), :] * e, 0.0)
                acc = acc + g * ki[s:s + 1]
                p_sc[pl.ds((i * SB + s) * SB, SB), :] = g * qi
            if i > 0:
                r = b[SB * i - 1:SB * i]
                fq = jnp.exp(bi - r)
                fk = jnp.exp(jnp.minimum(r - b, 0.0))
                dai = jnp.where(lane_c < SB * i, da[SB * i:SB * (i + 1)], 0.0)
                acc = acc + _nn(dai, k * fk, HIGH) * fq
                dk_off = dk_off + _tn(dai, qi * fq, HIGH) * fk
            dq_rows.append(acc)
        dqi = jnp.concatenate(dq_rows, axis=0)
        dq = dqi + dqb * eb
        dk_inter = dkb * ekb
        dk = _nn(sel, p_sc[...], HIGH) + dk_off + dk_inter
        db = q * dq - k * dk
        extra = (jnp.sum(k * dk_inter, axis=0, keepdims=True)
                 + jnp.exp(bl) * jnp.sum(dst * st, axis=0, keepdims=True))
        db = db + jnp.where(row == C - 1, extra, 0.0)
        dlf = _nn(_tri(C, "upper").astype(F32), db, HI)
        df = dlf / f - dk
        dq_ref[...] = dq.astype(BF16)
        df_ref[...] = (df * (1.0 - lb) * sg * (1.0 - sg)).astype(BF16)
        di_ref[...] = dv.astype(BF16)
        dl_ref[pl.ds(h, 1), :] += jnp.sum(df * (1.0 - sg), axis=0, keepdims=True) * (lb * (1.0 - lb))
        ds_sc[h] = dst * jnp.exp(bl) + _tn(_bf(do_), _bf(qb))

    rblk = lambda off: pl.BlockSpec((C, HD), lambda c, h: (nc - 1 - c, off + h))
    oblk = pl.BlockSpec((C, HD), lambda c, h: (nc - 1 - c, h))
    return pl.pallas_call(
        body, name=name, grid=(nc, nh),
        in_specs=[rblk(0), rblk(nh), rblk(2 * nh),
                  pl.BlockSpec((2, 1, 1, HD), lambda c, h: (0, h, 0, 0)),
                  oblk,
                  pl.BlockSpec((1, 1, C, C), lambda c, h: (nc - 1 - c, h, 0, 0)),
                  pl.BlockSpec((1, 1, HD, HD), lambda c, h: (nc - 1 - c, h, 0, 0))],
        out_specs=[oblk, oblk, oblk, pl.BlockSpec((nh, HD), lambda c, h: (0, 0))],
        out_shape=[jax.ShapeDtypeStruct((t, nh * HD), BF16)] * 3 + [jax.ShapeDtypeStruct((nh, HD), F32)],
        scratch_shapes=[pltpu.VMEM((nh, HD, HD), F32), pltpu.VMEM((C * SB, HD), F32),
                        pltpu.VMEM((C * SB, HD), F32)],
        compiler_params=_cp(("arbitrary", "arbitrary")),
    )(proj, proj, proj, lg, do, a_sv, st_sv)


def _shift_rows(u, d, row):
    t = u.shape[0]
    if d == 0:
        return u
    rolled = pltpu.roll(u, d % t, 0)
    if d > 0:
        return jnp.where(row >= d, rolled, 0.0)
    return jnp.where(row < t + d, rolled, 0.0)


def _gdn_prep(proj, conv_w, blk0, nh, name):
    t = proj.shape[0]
    scale = HD ** -0.5

    def body(u_ref, w_ref, o_ref):
        j = pl.program_id(0)
        u, w = u_ref[...], w_ref[...]
        row = lax.broadcasted_iota(jnp.int32, (t, HD), 0)
        y = w[CONV_K - 1:CONV_K, :] * u
        for d in range(1, CONV_K):
            y = y + w[CONV_K - 1 - d:CONV_K - d, :] * _shift_rows(u, d, row)
        a = y * _sigmoid(y)
        n = a * lax.rsqrt(jnp.sum(a * a, axis=-1, keepdims=True) + EPS)
        n = n * jnp.where(j < nh, scale, 1.0)
        o_ref[...] = jnp.where(j < 2 * nh, n, a)

    return pl.pallas_call(
        body, name=name, grid=(3 * nh,),
        in_specs=[pl.BlockSpec((t, HD), lambda j: (0, blk0 + j)), pl.BlockSpec((CONV_K, HD), lambda j: (0, j))],
        out_specs=pl.BlockSpec((t, HD), lambda j: (0, j)),
        out_shape=jax.ShapeDtypeStruct((t, 3 * nh * HD), F32),
        compiler_params=_cp(("parallel",)),
    )(proj, conv_w)


def _gdn_prep_bwd(proj, conv_w, dqkv, blk0, nh, name):
    t = proj.shape[0]
    scale = HD ** -0.5

    def body(u_ref, w_ref, d_ref, du_ref, dw_ref):
        j = pl.program_id(0)
        u, w, dout = u_ref[...], w_ref[...], d_ref[...]
        row = lax.broadcasted_iota(jnp.int32, (t, HD), 0)
        us = [_shift_rows(u, d, row) for d in range(CONV_K)]
        y = w[CONV_K - 1:CONV_K, :] * us[0]
        for d in range(1, CONV_K):
            y = y + w[CONV_K - 1 - d:CONV_K - d, :] * us[d]
        sg = _sigmoid(y)
        a = y * sg
        rs = lax.rsqrt(jnp.sum(a * a, axis=-1, keepdims=True) + EPS)
        n = a * rs
        dn = dout * jnp.where(j < nh, scale, 1.0)
        da_n = rs * (dn - n * jnp.sum(dn * n, axis=-1, keepdims=True))
        da = jnp.where(j < 2 * nh, da_n, dout)
        dy = da * (sg * (1.0 + y * (1.0 - sg)))
        du = w[CONV_K - 1:CONV_K, :] * dy
        for d in range(1, CONV_K):
            du = du + w[CONV_K - 1 - d:CONV_K - d, :] * _shift_rows(dy, -d, row)
        du_ref[...] = du.astype(BF16)
        for d in range(CONV_K):
            dw_ref[CONV_K - 1 - d:CONV_K - d, :] = jnp.sum(dy * us[d], axis=0, keepdims=True)

    return pl.pallas_call(
        body, name=name, grid=(3 * nh,),
        in_specs=[pl.BlockSpec((t, HD), lambda j: (0, blk0 + j)), pl.BlockSpec((CONV_K, HD), lambda j: (0, j)),
                  pl.BlockSpec((t, HD), lambda j: (0, j))],
        out_specs=[pl.BlockSpec((t, HD), lambda j: (0, j)), pl.BlockSpec((CONV_K, HD), lambda j: (0, j))],
        out_shape=[jax.ShapeDtypeStruct((t, 3 * nh * HD), BF16), jax.ShapeDtypeStruct((CONV_K, 3 * nh * HD), F32)],
        compiler_params=_cp(("parallel",)),
    )(proj, conv_w, dqkv)


def _gdn_gates(ab, alog, dtb, h, nh):
    lane = lax.broadcasted_iota(jnp.int32, ab.shape, 1)
    x = ab + dtb
    sp = jnp.maximum(x, 0.0) + jnp.log(1.0 + jnp.exp(-jnp.abs(x)))
    ea = jnp.exp(alog)
    la_all = -ea * sp
    beta_all = _sigmoid(ab)
    pick = lambda val, ln: jnp.sum(jnp.where(lane == ln, val, 0.0), axis=1, keepdims=True)
    la = pick(la_all, h)
    beta = pick(beta_all, nh + h)
    dla_da = pick(-ea * _sigmoid(x), h)
    return la, beta, dla_da


def _gdn_chunk(q, k, v, la, beta, C):
    low, strict = _tri(C, "lower"), _tri(C, "strict")
    g_b = _nn(low.astype(F32), jnp.broadcast_to(la, (C, HD)), HI)
    g_c = g_b[:, :C]
    gamma = jnp.where(low, jnp.exp(jnp.minimum(g_c - g_c.T, 0.0)), 0.0)
    eg = jnp.exp(g_b)
    gl = g_b[C - 1:C, :]
    ekt = jnp.exp(gl - g_b)
    p = _nt(k, k, HI)
    m = jnp.where(strict, beta * p * gamma, 0.0)
    x = (lax.broadcasted_iota(jnp.int32, (C, C), 0) == lax.broadcasted_iota(jnp.int32, (C, C), 1)).astype(F32)
    for s in range(C - 1):
        x = x - m[:, s:s + 1] * x[s:s + 1, :]
    r_w = k * (beta * eg)
    rhs = jnp.concatenate([v * beta, r_w], axis=1)
    uw = _nn(x, rhs, HI)
    qk_raw = _nt(_bf(q), _bf(k))
    return dict(gamma=gamma, eg=eg, gl=gl, ekt=ekt, p=p, x=x, r_w=r_w, uw=uw, qk_raw=qk_raw,
                low=low, strict=strict)


def _gdn_fwd(qkv, proj, ab_blk, alog, dtb, nh, name, comm=None):
    t = qkv.shape[0]
    nc = t // CHUNK
    C = CHUNK
    ng = nh // HP

    def body(*refs):
        c, hg = pl.program_id(0), pl.program_id(1)
        step = c * ng + hg
        ins, outs, scratch, comm_begin, comm_end = _comm_hooks(
            comm, refs, 6, 3, step == 0, step == (nc * ng) // 2, step == nc * ng - 1)
        q_ref, k_ref, v_ref, ab_ref, al_ref, dt_ref = ins
        o_ref, x_ref, st_ref = outs
        s_sc, = scratch
        comm_begin()

        @pl.when(c == 0)
        def _():
            for hh in range(HP):
                s_sc[hg * HP + hh] = jnp.zeros((HD, HD), F32)

        for hh in range(HP):
            h = hg * HP + hh
            sl = slice(hh * HD, (hh + 1) * HD)
            q, k, v = q_ref[:, sl], k_ref[:, sl], v_ref[:, sl]
            la, beta, _ = _gdn_gates(ab_ref[...], al_ref[...], dt_ref[...], h, nh)
            ch = _gdn_chunk(q, k, v, la, beta, C)
            st = s_sc[h]
            stb = _bf(st)
            u, w = ch["uw"][:, :HD], ch["uw"][:, HD:]
            vn = u - _nt(_bf(w), stb)
            qk = ch["qk_raw"] * ch["gamma"]
            o_ref[:, sl] = _nt(_bf(q * ch["eg"]), stb) + _nn(_bf(qk), _bf(vn))
            x_ref[0, hh] = ch["x"]
            st_ref[0, hh] = st
            s_sc[h] = st * jnp.exp(ch["gl"]) + _tn(_bf(vn), _bf(k * ch["ekt"]))
        comm_end()

    blk = lambda off: pl.BlockSpec((C, HP * HD), lambda c, g: (c, off // HP + g))
    vec = pl.BlockSpec((1, HD), lambda c, g: (0, 0))
    cn = comm.n if comm is not None else 0
    return pl.pallas_call(
        body, name=name, grid=(nc, ng),
        in_specs=[blk(0), blk(nh), blk(2 * nh), pl.BlockSpec((C, HD), lambda c, g: (c, ab_blk)), vec, vec]
        + [ANY] * cn,
        out_specs=[blk(0),
                   pl.BlockSpec((1, HP, C, C), lambda c, g: (c, g, 0, 0)),
                   pl.BlockSpec((1, HP, HD, HD), lambda c, g: (c, g, 0, 0))] + [ANY] * cn,
        out_shape=[jax.ShapeDtypeStruct((t, nh * HD), F32),
                   jax.ShapeDtypeStruct((nc, nh, C, C), F32),
                   jax.ShapeDtypeStruct((nc, nh, HD, HD), F32)] + (comm.out_shapes() if cn else []),
        scratch_shapes=[pltpu.VMEM((nh, HD, HD), F32)] + (comm.scratch() if cn else []),
        compiler_params=_cp(("arbitrary", "arbitrary")),
    )(qkv, qkv, qkv, proj, alog, dtb, *(comm.arrays if cn else []))


def _gdn_bwd(qkv, proj, ab_blk, alog, dtb, do, x_sv, st_sv, nh, name, comm=None):
    t = qkv.shape[0]
    nc = t // CHUNK
    C = CHUNK
    ng = nh // HP

    def one_head(h, hh, q_ref, k_ref, v_ref, ab_ref, al_ref, dt_ref, do_ref, x_ref, st_ref,
                 dq_ref, dk_ref, dv_ref, ds_sc):
        sl = slice(hh * HD, (hh + 1) * HD)
        q, k, v, do_ = q_ref[:, sl], k_ref[:, sl], v_ref[:, sl], do_ref[:, sl]
        la, beta, dla_da = _gdn_gates(ab_ref[...], al_ref[...], dt_ref[...], h, nh)
        low, strict = _tri(C, "lower"), _tri(C, "strict")
        g_b = _nn(low.astype(F32), jnp.broadcast_to(la, (C, HD)), HI)
        g_c = g_b[:, :C]
        gamma = jnp.where(low, jnp.exp(jnp.minimum(g_c - g_c.T, 0.0)), 0.0)
        eg = jnp.exp(g_b)
        gl = g_b[C - 1:C, :]
        ekt = jnp.exp(gl - g_b)
        egl = jnp.exp(gl)
        p = _nt(k, k, HI)
        x = x_ref[0, hh]
        r_w = k * (beta * eg)
        rhs = jnp.concatenate([v * beta, r_w], axis=1)
        uw = _nn(x, rhs, HI)
        u, w = uw[:, :HD], uw[:, HD:]
        qk_raw = _nt(_bf(q), _bf(k))
        qk = qk_raw * gamma
        st, dst = st_ref[0, hh], ds_sc[h]
        stb, dstb = _bf(st), _bf(dst)
        vn = u - _nt(_bf(w), stb)
        qd, kt = q * eg, k * ekt

        dvn = _tn(_bf(qk), _bf(do_)) + _nt(_bf(kt), dstb)
        dq2 = jnp.where(low, _nt(_bf(do_), _bf(vn)), 0.0)
        dqd = _nn(_bf(do_), stb)
        dkt = _nn(_bf(vn), dstb)
        dw = -_nn(_bf(dvn), stb)
        dxx = jnp.concatenate([dvn, dw], axis=1)
        dr = _tn(x, dxx, HI)
        dm = -jnp.where(strict, _nt(dr, uw, HI), 0.0)
        dr_u, dr_w = dr[:, :HD], dr[:, HD:]
        rsum = lambda z: jnp.sum(z, axis=1, keepdims=True)

        dv_ref[:, sl] = dr_u * beta
        dmg = dm * gamma
        dbeta = rsum(dr_u * v) + rsum(dr_w * k) * eg[:, :1] + rsum(dmg * p)
        dp = dmg * beta
        dq2g = dq2 * gamma
        dk = (dr_w * (beta * eg) + dkt * ekt + _tn(_bf(dq2g), _bf(q))
              + _nn(_bf(dp + dp.T), _bf(k)))
        dq_ref[:, sl] = dqd * eg + _nn(_bf(dq2g), _bf(k))
        dk_ref[:, sl] = dk
        e = dp * p + dq2g * qk_raw
        t_kt = rsum(dkt * kt)
        dg = rsum(dqd * qd) + rsum(dr_w * r_w) - t_kt + rsum(e) - rsum(e.T)
        dgl = jnp.sum(t_kt, axis=0, keepdims=True) + jnp.sum(dst * st, keepdims=True) * egl[:, :1]
        rowc = lax.broadcasted_iota(jnp.int32, (C, 1), 0)
        dg = dg + jnp.where(rowc == C - 1, dgl, 0.0)
        dla = _nn(_tri(C, "upper").astype(F32), jnp.broadcast_to(dg, (C, HD)), HI)[:, :1]
        da = dla * dla_da
        db = dbeta * beta * (1.0 - beta)
        lane = lax.broadcasted_iota(jnp.int32, (C, HD), 1)
        dab = jnp.where(lane == h, da, 0.0) + jnp.where(lane == nh + h, db, 0.0)
        lane1 = lax.broadcasted_iota(jnp.int32, (1, HD), 1)
        d_alog = jnp.where(lane1 == h, jnp.sum(dla * la, axis=0, keepdims=True), 0.0)
        d_dtb = jnp.where(lane1 == h, jnp.sum(da, axis=0, keepdims=True), 0.0)
        ds_sc[h] = dst * egl + _tn(_bf(do_), _bf(qd)) - _tn(_bf(dvn), _bf(w))
        return dab, d_alog, d_dtb

    def body(*refs):
        c, hg = pl.program_id(0), pl.program_id(1)
        step = c * ng + hg
        ins, outs, scratch, comm_begin, comm_end = _comm_hooks(
            comm, refs, 9, 5, step == 0, step == (nc * ng) // 2, step == nc * ng - 1)
        dq_ref, dk_ref, dv_ref, dab_ref, dpar_ref = outs
        ds_sc, = scratch
        comm_begin()

        @pl.when(c == 0)
        def _():
            for hh in range(HP):
                ds_sc[hg * HP + hh] = jnp.zeros((HD, HD), F32)

        @pl.when(step == 0)
        def _():
            dpar_ref[...] = jnp.zeros_like(dpar_ref)

        @pl.when(hg == 0)
        def _():
            dab_ref[...] = jnp.zeros_like(dab_ref)

        res = [one_head(hg * HP + hh, hh, *ins, dq_ref, dk_ref, dv_ref, ds_sc) for hh in range(HP)]
        dab_ref[...] += sum(r[0] for r in res[1:]) + res[0][0]
        dpar_ref[0:1, :] += sum(r[1] for r in res[1:]) + res[0][1]
        dpar_ref[1:2, :] += sum(r[2] for r in res[1:]) + res[0][2]
        comm_end()

    rblk = lambda off: pl.BlockSpec((C, HP * HD), lambda c, g: (nc - 1 - c, off // HP + g))
    oblk = pl.BlockSpec((C, HP * HD), lambda c, g: (nc - 1 - c, g))
    vec = pl.BlockSpec((1, HD), lambda c, g: (0, 0))
    cn = comm.n if comm is not None else 0
    return pl.pallas_call(
        body, name=name, grid=(nc, ng),
        in_specs=[rblk(0), rblk(nh), rblk(2 * nh),
                  pl.BlockSpec((C, HD), lambda c, g: (nc - 1 - c, ab_blk)), vec, vec, oblk,
                  pl.BlockSpec((1, HP, C, C), lambda c, g: (nc - 1 - c, g, 0, 0)),
                  pl.BlockSpec((1, HP, HD, HD), lambda c, g: (nc - 1 - c, g, 0, 0))] + [ANY] * cn,
        out_specs=[oblk, oblk, oblk,
                   pl.BlockSpec((C, HD), lambda c, g: (nc - 1 - c, 0)),
                   pl.BlockSpec((8, HD), lambda c, g: (0, 0))] + [ANY] * cn,
        out_shape=[jax.ShapeDtypeStruct((t, nh * HD), F32)] * 3
        + [jax.ShapeDtypeStruct((t, HD), F32), jax.ShapeDtypeStruct((8, HD), F32)]
        + (comm.out_shapes() if cn else []),
        scratch_shapes=[pltpu.VMEM((nh, HD, HD), F32)] + (comm.scratch() if cn else []),
        compiler_params=_cp(("arbitrary", "arbitrary")),
    )(qkv, qkv, qkv, proj, alog, dtb, do, x_sv, st_sv, *(comm.arrays if cn else []))


def _ada_fwd(c_all, w, b, name):
    nb, d = c_all.shape
    n = w.shape[1]
    tn = _pick(n, 512)

    def body(c_ref, w_ref, b_ref, o_ref):
        cv = c_ref[...]
        o_ref[...] = _nn(cv * _sigmoid(cv), w_ref[...], HI) + b_ref[...]

    return pl.pallas_call(
        body, name=name, grid=(n // tn,),
        in_specs=[pl.BlockSpec((nb, d), lambda j: (0, 0)), pl.BlockSpec((d, tn), lambda j: (0, j)),
                  pl.BlockSpec((1, tn), lambda j: (0, j))],
        out_specs=pl.BlockSpec((nb, tn), lambda j: (0, j)),
        out_shape=jax.ShapeDtypeStruct((nb, n), F32),
        compiler_params=_cp(("parallel",)),
    )(c_all, w, b)


def _ada_wgrad(c_all, dmod, name):
    nb, d = c_all.shape
    n = dmod.shape[1]
    tn = _pick(n, 512)

    def body(c_ref, g_ref, o_ref):
        cv = c_ref[...]
        o_ref[...] = _tn(cv * _sigmoid(cv), g_ref[...], HI)

    return pl.pallas_call(
        body, name=name, grid=(n // tn,),
        in_specs=[pl.BlockSpec((nb, d), lambda j: (0, 0)), pl.BlockSpec((nb, tn), lambda j: (0, j))],
        out_specs=pl.BlockSpec((d, tn), lambda j: (0, j)),
        out_shape=jax.ShapeDtypeStruct((d, n), F32),
        compiler_params=_cp(("parallel",)),
    )(c_all, dmod)


def _adamw(w, m, v, g, name, parts=False):
    r, cdim = w.shape
    tr = r if r <= 256 else _pick_rows(r, 256)
    bc1 = 1.0 - ADAM_B1 ** ADAM_STEP
    bc2 = 1.0 - ADAM_B2 ** ADAM_STEP

    def body(w_ref, m_ref, v_ref, g_ref, go_ref, d_ref, mo_ref, vo_ref):
        if parts:
            gv = g_ref[0].astype(F32)
            for s in range(1, N_DEV):
                gv = gv + g_ref[s].astype(F32)
        else:
            gv = g_ref[...]
        wv = w_ref[...]
        mn = ADAM_B1 * m_ref[...] + (1.0 - ADAM_B1) * gv
        vn = ADAM_B2 * v_ref[...] + (1.0 - ADAM_B2) * (gv * gv)
        m_hat = mn / bc1
        v_hat = vn / bc2
        go_ref[...] = gv
        d_ref[...] = -ADAM_LR * (m_hat / (jnp.sqrt(v_hat) + ADAM_EPS) + ADAM_WD * wv)
        mo_ref[...] = mn
        vo_ref[...] = vn

    spec = pl.BlockSpec((tr, cdim), lambda i: (i, 0))
    gspec = pl.BlockSpec((N_DEV, tr, cdim), lambda i: (0, i, 0)) if parts else spec
    return pl.pallas_call(
        body, name=name, grid=(r // tr,),
        in_specs=[spec, spec, spec, gspec],
        out_specs=[spec] * 4,
        out_shape=[jax.ShapeDtypeStruct((r, cdim), F32)] * 4,
        compiler_params=_cp(("parallel",)),
    )(w, m, v, g)


def _pick_rows(r, pref):
    t = pref
    while r % t:
        t -= 8
    assert t > 0
    return t


def _dev_index(x, y, c):
    return 4 * x + 2 * y + c


class _Comm:
    def __init__(self, kind, arrays):
        self.kind, self.arrays, self.n = kind, list(arrays), len(arrays)

    def out_shapes(self):
        if self.kind == "gather":
            return [jax.ShapeDtypeStruct((N_DEV,) + a.shape, a.dtype) for a in self.arrays]
        return [jax.ShapeDtypeStruct(a.shape, a.dtype) for a in self.arrays]

    def scratch(self):
        return [pltpu.SemaphoreType.DMA((self.n, 7)), pltpu.SemaphoreType.DMA((self.n, 7)),
                pltpu.SemaphoreType.DMA((self.n,))]

    def _gather_parts(self, ins, outs, sems):
        send_sems, recv_sems, local_sems = sems
        x, y, c = lax.axis_index("x"), lax.axis_index("y"), lax.axis_index("c")
        me, sibling = (x, y, c), (x, y, 1 - c)
        chips = [(1 - x, y), (x, 1 - y), (1 - x, 1 - y)]

        def copy(a, k, block, to, src=None):
            slot = outs[a].at[_dev_index(*block)]
            return pltpu.make_async_remote_copy(
                src_ref=slot if src is None else src, dst_ref=slot,
                send_sem=send_sems.at[a, k], recv_sem=recv_sems.at[a, k],
                device_id=to, device_id_type=MESH)

        n = self.n
        mine = [pltpu.make_async_copy(ins[a], outs[a].at[_dev_index(*me)], local_sems.at[a]) for a in range(n)]
        first = []
        for a in range(n):
            first.append(copy(a, 0, me, sibling, src=ins[a]))
            first += [copy(a, 1 + j, me, (*chip, c), src=ins[a]) for j, chip in enumerate(chips)]
        landed = [copy(a, 1 + j, (*chip, c), me) for j, chip in enumerate(chips) for a in range(n)]
        passed = [copy(a, 4 + j, (*chip, c), sibling) for j, chip in enumerate(chips) for a in range(n)]
        late = []
        for a in range(n):
            late.append(copy(a, 0, sibling, me))
            late += [copy(a, 4 + j, (*chip, 1 - c), me) for j, chip in enumerate(chips)]
        return mine, first, landed, passed, late

    def _exchange_parts(self, ins, outs, sems):
        send_sems, recv_sems, local_sems = sems
        x, y, c = lax.axis_index("x"), lax.axis_index("y"), lax.axis_index("c")
        my = _dev_index(x, y, c)
        n = self.n
        mine = [pltpu.make_async_copy(ins[a].at[my], outs[a].at[my], local_sems.at[a]) for a in range(n)]
        sends, recvs = [], []
        for k in range(1, N_DEV):
            px = (1 - x) if (k >> 2) & 1 else x
            py = (1 - y) if (k >> 1) & 1 else y
            pc = (1 - c) if k & 1 else c
            peer = _dev_index(px, py, pc)
            for a in range(n):
                sends.append(pltpu.make_async_remote_copy(
                    src_ref=ins[a].at[peer], dst_ref=outs[a].at[my],
                    send_sem=send_sems.at[a, k - 1], recv_sem=recv_sems.at[a, k - 1],
                    device_id=(px, py, pc), device_id_type=MESH))
                recvs.append(pltpu.make_async_remote_copy(
                    src_ref=ins[a].at[my], dst_ref=outs[a].at[peer],
                    send_sem=send_sems.at[a, k - 1], recv_sem=recv_sems.at[a, k - 1],
                    device_id=(x, y, c), device_id_type=MESH))
        return mine, sends, recvs

    def start(self, ins, outs, sems):
        if self.kind == "gather":
            mine, first, _, _, _ = self._gather_parts(ins, outs, sems)
        else:
            mine, first, _ = self._exchange_parts(ins, outs, sems)
        for cp in mine + first:
            cp.start()

    def mid(self, ins, outs, sems):
        if self.kind == "gather":
            _, _, landed, passed, _ = self._gather_parts(ins, outs, sems)
            for got, fwd in zip(landed, passed):
                got.wait_recv()
                fwd.start()

    def finish(self, ins, outs, sems):
        if self.kind == "gather":
            mine, first, _, passed, late = self._gather_parts(ins, outs, sems)
            for cp in late:
                cp.wait_recv()
            for cp in first + passed:
                cp.wait_send()
        else:
            mine, sends, recvs = self._exchange_parts(ins, outs, sems)
            for cp in sends:
                cp.wait_send()
            for cp in recvs:
                cp.wait_recv()
        for cp in mine:
            cp.wait()

    def run(self, name):
        n = self.n

        def body(*refs):
            ins, outs, sems = refs[:n], refs[n:2 * n], refs[2 * n:]
            self.start(ins, outs, sems)
            self.mid(ins, outs, sems)
            self.finish(ins, outs, sems)

        return pl.pallas_call(
            body, name=name, in_specs=[ANY] * n, out_specs=[ANY] * n,
            out_shape=self.out_shapes(), scratch_shapes=self.scratch(),
        )(*self.arrays)


def _all_gather(arrays, name):
    return _Comm("gather", arrays).run(name)


def _comm_hooks(comm, refs, n_in, n_out, first, middle, last):
    cn = comm.n if comm is not None else 0
    ins, cins = refs[:n_in], refs[n_in:n_in + cn]
    outs, couts = refs[n_in + cn:n_in + cn + n_out], refs[n_in + cn + n_out:n_in + 2 * cn + n_out]
    rest = refs[n_in + 2 * cn + n_out:]
    scratch, csems = (rest[:len(rest) - 3], rest[len(rest) - 3:]) if cn else (rest, ())

    def begin():
        if cn:
            pl.when(first)(lambda: comm.start(cins, couts, csems))
            pl.when(middle)(lambda: comm.mid(cins, couts, csems))

    def end():
        if cn:
            pl.when(last)(lambda: comm.finish(cins, couts, csems))

    return ins, outs, scratch, begin, end


def _local_step(x, tgt, mod, n1, n2, n3, n4, w_in_p, lb_logits, hg_norm, conv_w, alog, dtb, gdn_norm,
                late_w, late_gather=None, parts_a=None, parts_b=None):
    t, d = x.shape
    nh = d // 2 // HD
    ab_blk = 8 * nh
    sh_m, sc_m, gt_m, sh_f, sc_f, gt_f = [mod[i:i + 1] for i in range(6)]

    h1, r1 = _prenorm(x, n1, sc_m, sh_m, "prenorm_mix")
    proj = _mm(h1, w_in_p, "nn", [F32], "mm_proj")
    o_hg, a_sv, hst_sv = _hgrn2_fwd(proj, lb_logits, nh, "hgrn2_fwd")
    qkv = _gdn_prep(proj, conv_w, 4 * nh, nh, "gdn_prep")
    if late_gather is None:
        o_gd, x_sv, gst_sv = _gdn_fwd(qkv, proj, ab_blk, alog, dtb, nh, "gdn_fwd")
        w_out, w_ff1, w_ff2 = late_w
    else:
        o_gd, x_sv, gst_sv, *gathered = _gdn_fwd(qkv, proj, ab_blk, alog, dtb, nh, "gdn_fwd",
                                                 comm=_Comm("gather", late_w))
        w_out, w_ff1, w_ff2 = late_gather(gathered)
    om_hg = _headnorm_fwd(o_hg, proj, 3 * nh, hg_norm, "headnorm_hg")
    om_gd = _headnorm_fwd(o_gd, proj, 7 * nh, gdn_norm, "headnorm_gdn")
    om = jnp.concatenate([om_hg, om_gd], axis=1)
    y1 = _mm(om, w_out, "nn", [F32], "mm_out")
    x1, r2 = _postnorm_res(x, y1, n2, gt_m, "postnorm_mix")
    h2, r3 = _prenorm(x1, n3, sc_f, sh_f, "prenorm_ffn")

    def relu2(acc, extra, outs):
        outs[0][...] = acc
        rl = jnp.maximum(acc, 0.0)
        outs[1][...] = (rl * rl).astype(BF16)

    u, act = _mm(h2, w_ff1, "nn", [F32, BF16], "mm_ff1", epilogue=relu2)
    y2 = _mm(act, w_ff2, "nn", [F32], "mm_ff2")
    dout, r4, loss = _final_loss(x1, y2, n4, gt_f, tgt, "final_loss")

    dy2, dgt_f, dn4 = _postnorm_bwd(dout, y2, r4, n4, gt_f, "postnorm_ffn_bwd")
    dw_ff2 = _mm(act, dy2, "tn", [F32], "mm_dw_ff2")

    def drelu2(acc, extra, outs):
        outs[0][...] = (acc * (2.0 * jnp.maximum(extra[0][...], 0.0))).astype(BF16)

    du = _mm(dy2, w_ff2, "nt", [BF16], "mm_da", epilogue=drelu2, extras=(u,))
    dw_ff1 = _mm(h2, du, "tn", [F32], "mm_dw_ff1")
    dh2 = _mm(du, w_ff1, "nt", [F32], "mm_dh2")
    dx1, dsh_f, dsc_f, dn3 = _prenorm_bwd(dh2, x1, r3, n3, sc_f, dout, "prenorm_ffn_bwd")

    dy1, dgt_m, dn2 = _postnorm_bwd(dx1, y1, r2, n2, gt_m, "postnorm_mix_bwd")
    dw_out = _mm(om, dy1, "tn", [F32], "mm_dw_out")
    dom = _mm(dy1, w_out, "nt", [F32], "mm_dom")
    do_hg, dg_hg, dhgn = _headnorm_bwd(dom, 0, o_hg, proj, 3 * nh, hg_norm, "headnorm_hg_bwd")
    do_gd, dg_gd, dgdn = _headnorm_bwd(dom, 1, o_gd, proj, 7 * nh, gdn_norm, "headnorm_gdn_bwd")
    dq_hg, df_hg, di_hg, dl0 = _hgrn2_bwd(proj, lb_logits, do_hg, a_sv, hst_sv, nh, "hgrn2_bwd")
    comm_a = _Comm("exchange", parts_a(dw_ff2, dw_ff1, dw_out)) if parts_a is not None else None
    dq_g, dk_g, dv_g, dab, dpar, *recv_a = _gdn_bwd(qkv, proj, ab_blk, alog, dtb, do_gd, x_sv, gst_sv, nh,
                                                    "gdn_bwd", comm=comm_a)
    dqkv = jnp.concatenate([dq_g, dk_g, dv_g], axis=1)
    du_conv, dconv = _gdn_prep_bwd(proj, conv_w, dqkv, 4 * nh, nh, "gdn_prep_bwd")
    dproj = jnp.concatenate([dq_hg, df_hg, di_hg, dg_hg, du_conv, dg_gd, dab.astype(BF16)], axis=1)
    dw_in = _mm(h1, dproj, "tn", [F32], "mm_dw_in")
    comm_b = _Comm("exchange", parts_b(dw_in)) if parts_b is not None else None
    res = _mm(dproj, w_in_p, "nt", [F32], "mm_dh1", tk=640, comm=comm_b)
    dh1, recv_b = (res[0], list(res[1:])) if comm_b is not None else (res, [])
    dx, dsh_m, dsc_m, dn1 = _prenorm_bwd(dh1, x, r1, n1, sc_m, dx1, "prenorm_mix_bwd")

    dmod = jnp.concatenate([dsh_m, dsc_m, dgt_m, dsh_f, dsc_f, dgt_f], axis=0)
    grads = dict(dmod=dmod, n1=dn1, n2=dn2, n3=dn3, n4=dn4, w_in=dw_in, lb0=dl0, hg_norm=dhgn, conv=dconv,
                 alog=dpar[0:1], dtb=dpar[1:2], gdn_norm=dgdn, w_out=dw_out, w_ff1=dw_ff1, w_ff2=dw_ff2,
                 recv_a=recv_a, recv_b=recv_b)
    return loss, dx, grads


def _pack(vals):
    rows = []
    for vv in vals:
        flat = vv.reshape(-1)
        pad = (-flat.shape[0]) % LANES
        if pad:
            flat = jnp.concatenate([flat, jnp.zeros((pad,), flat.dtype)])
        rows.append(flat.reshape(-1, LANES))
    return jnp.concatenate(rows, axis=0)


def _unpack(packed, shapes):
    out, r = [], 0
    for shp in shapes:
        size = 1
        for s in shp:
            size *= s
        nr = -(-size // LANES)
        out.append(packed[r:r + nr].reshape(-1)[:size].reshape(shp))
        r += nr
    return out


def _sum_parts(parts, name):
    _, r, cdim = parts.shape

    def body(p_ref, o_ref):
        acc = p_ref[0]
        for s in range(1, N_DEV):
            acc = acc + p_ref[s]
        o_ref[...] = acc

    return pl.pallas_call(
        body, name=name,
        out_shape=jax.ShapeDtypeStruct((r, cdim), F32),
        compiler_params=_cp(),
    )(parts)


def kernel(x, c, w_ada, b_ada, pre_mix_norm, post_mix_norm, pre_ffn_norm, post_ffn_norm, w_in, hg_lb_logits, hg_norm, gdn_conv_w, gdn_a_log, gdn_dt_bias, gdn_norm, w_out, w_ff1, w_ff2, loss_target, m_w_ada, m_b_ada, m_pre_mix_norm, m_post_mix_norm, m_pre_ffn_norm, m_post_ffn_norm, m_w_in, m_hg_lb_logits, m_hg_norm, m_gdn_conv_w, m_gdn_a_log, m_gdn_dt_bias, m_gdn_norm, m_w_out, m_w_ff1, m_w_ff2, v_w_ada, v_b_ada, v_pre_mix_norm, v_post_mix_norm, v_pre_ffn_norm, v_post_ffn_norm, v_w_in, v_hg_lb_logits, v_hg_norm, v_gdn_conv_w, v_gdn_a_log, v_gdn_dt_bias, v_gdn_norm, v_w_out, v_w_ff1, v_w_ff2):
    t, d = x.shape[1], x.shape[2]
    nh = d // 2 // HD
    in_cols = w_in.shape[2] * N_DEV
    main = in_cols - 2 * nh
    me = _dev_index(lax.axis_index("x"), lax.axis_index("y"), lax.axis_index("c"))

    c_all, conv_g = _all_gather([c, gdn_conv_w[0]], "gather_small")
    c_all = c_all.reshape(N_DEV, d)
    conv_full = conv_g.transpose(1, 0, 2).reshape(CONV_K, -1)
    w_in_g = _all_gather([w_in[0].astype(BF16)], "gather_w_in")[0]
    w_in_full = w_in_g.transpose(1, 0, 2).reshape(d, in_cols)
    w_in_p = jnp.concatenate([w_in_full, jnp.zeros((d, LANES - 2 * nh), BF16)], axis=1)
    late_w = [w_out[0].astype(BF16), w_ff1[0].astype(BF16), w_ff2[0].astype(BF16)]

    def late_gather(gathered):
        w_out_g, w_ff1_g, w_ff2_g = gathered
        return w_out_g.reshape(d, d), w_ff1_g.transpose(1, 0, 2).reshape(d, -1), w_ff2_g.reshape(-1, d)

    n_in = w_in.shape[2]
    n_ff = w_ff1.shape[2]

    def parts_a(dw_ff2, dw_ff1, dw_out):
        return [dw_ff2.reshape(N_DEV, -1, d).astype(BF16),
                dw_ff1.reshape(d, N_DEV, n_ff).transpose(1, 0, 2).astype(BF16),
                dw_out.reshape(N_DEV, d // N_DEV, d).astype(BF16)]

    def parts_b(dw_in):
        return [dw_in[:, :in_cols].reshape(d, N_DEV, n_in).transpose(1, 0, 2).astype(BF16)]

    n_ada = w_ada.shape[2]
    b_loc = lax.dynamic_slice(b_ada, (0, me * n_ada), (1, n_ada))
    mod_part = _ada_fwd(c_all, w_ada[0], b_loc, "ada_fwd")
    mod_all = _all_gather([mod_part], "gather_mod")[0]
    mod = lax.dynamic_slice(mod_all, (0, me, 0), (N_DEV, 1, n_ada)).reshape(6, d)

    pad_lane = lambda vv: jnp.concatenate([vv, jnp.zeros((1, LANES - vv.shape[1]), F32)], axis=1)
    loss, dx, g = _local_step(
        x[0], loss_target[0], mod, pre_mix_norm, post_mix_norm, pre_ffn_norm, post_ffn_norm, w_in_p,
        hg_lb_logits, hg_norm, conv_full, pad_lane(gdn_a_log), pad_lane(gdn_dt_bias), gdn_norm,
        late_w, late_gather, parts_a, parts_b)

    rep_names = ["b_ada", "n1", "n2", "n3", "n4", "lb", "hg_norm", "alog", "dtb", "gdn_norm"]
    rep_w = [b_ada, pre_mix_norm, post_mix_norm, pre_ffn_norm, post_ffn_norm, hg_lb_logits, hg_norm,
             gdn_a_log, gdn_dt_bias, gdn_norm]
    rep_m = [m_b_ada, m_pre_mix_norm, m_post_mix_norm, m_pre_ffn_norm, m_post_ffn_norm, m_hg_lb_logits,
             m_hg_norm, m_gdn_a_log, m_gdn_dt_bias, m_gdn_norm]
    rep_v = [v_b_ada, v_pre_mix_norm, v_post_mix_norm, v_pre_ffn_norm, v_post_ffn_norm, v_hg_lb_logits,
             v_hg_norm, v_gdn_a_log, v_gdn_dt_bias, v_gdn_norm]
    rep_shapes = [a.shape for a in rep_w]
    g_lb = jnp.stack([g["lb0"], -g["lb0"]], axis=0)
    rep_g = [g["dmod"], g["n1"], g["n2"], g["n3"], g["n4"], g_lb, g["hg_norm"],
             g["alog"][:, :nh], g["dtb"][:, :nh], g["gdn_norm"]]
    small = _pack(rep_g + [g["conv"]])
    n_rep_rows = _pack(rep_g).shape[0]
    pad_rows = (-small.shape[0]) % 8
    if pad_rows:
        small = jnp.concatenate([small, jnp.zeros((pad_rows, LANES), F32)], axis=0)
    small_all = _all_gather([small], "gather_small_grads")[0]
    small_sum = _sum_parts(small_all, "sum_small_grads")
    rep_out = _adamw(_pack(rep_w), _pack(rep_m), _pack(rep_v), small_sum[:n_rep_rows], "adamw_small")
    rep_g_o, rep_d_o, rep_m_o, rep_v_o = [dict(zip(rep_names, _unpack(p, rep_shapes))) for p in rep_out]

    conv_sum = small_sum[n_rep_rows:n_rep_rows + CONV_K * conv_full.shape[1] // LANES].reshape(CONV_K, -1)
    n_conv = gdn_conv_w.shape[2]
    conv_loc = lax.dynamic_slice(conv_sum, (0, me * n_conv), (CONV_K, n_conv))
    conv_o = _adamw(gdn_conv_w[0], m_gdn_conv_w[0], v_gdn_conv_w[0], conv_loc, "adamw_conv")

    dmod_all = small_all[:, :6 * d // LANES, :].reshape(N_DEV, 6 * d)
    dmod_loc = lax.dynamic_slice(dmod_all, (0, me * n_ada), (N_DEV, n_ada))
    g_ada = _ada_wgrad(c_all, dmod_loc, "ada_wgrad")
    ada_o = _adamw(w_ada[0], m_w_ada[0], v_w_ada[0], g_ada, "adamw_ada")

    r_ff2, r_ff1, r_out = g["recv_a"]
    r_in, = g["recv_b"]
    in_o = _adamw(w_in[0], m_w_in[0], v_w_in[0], r_in, "adamw_w_in", parts=True)
    out_o = _adamw(w_out[0], m_w_out[0], v_w_out[0], r_out, "adamw_w_out", parts=True)
    ff1_o = _adamw(w_ff1[0], m_w_ff1[0], v_w_ff1[0], r_ff1, "adamw_w_ff1", parts=True)
    ff2_o = _adamw(w_ff2[0], m_w_ff2[0], v_w_ff2[0], r_ff2, "adamw_w_ff2", parts=True)

    loss_tot = lax.psum(loss[0, 0], ("x", "y", "c"))

    def leaf(kind):
        return [ada_o[kind][None], rep_out_d[kind]["b_ada"], rep_out_d[kind]["n1"], rep_out_d[kind]["n2"],
                rep_out_d[kind]["n3"], rep_out_d[kind]["n4"], in_o[kind][None], rep_out_d[kind]["lb"],
                rep_out_d[kind]["hg_norm"], conv_o[kind][None], rep_out_d[kind]["alog"], rep_out_d[kind]["dtb"],
                rep_out_d[kind]["gdn_norm"], out_o[kind][None], ff1_o[kind][None], ff2_o[kind][None]]

    rep_out_d = [rep_g_o, rep_d_o, rep_m_o, rep_v_o]
    return (loss_tot, dx[None], *leaf(0), *leaf(1), *leaf(2), *leaf(3))
```

```python
import functools

import jax
import jax.numpy as jnp
from jax import lax
from jax.experimental import pallas as pl
from jax.experimental.pallas import tpu as pltpu

F32 = jnp.float32
BF16 = jnp.bfloat16
HI = lax.Precision.HIGHEST
HIGH = lax.Precision.HIGH

EPS = 1e-6
CHUNK = 64
SB = 16
NSB = CHUNK // SB
HP = 8
HD = 128
CONV_K = 4
N_DEV = 8
LANES = 128
VMEM_LIMIT = 56 * 1024 * 1024

ADAM_LR = 0.001
ADAM_B1 = 0.9
ADAM_B2 = 0.999
ADAM_EPS = 1e-08
ADAM_WD = 0.01
ADAM_STEP = 10

ANY = pl.BlockSpec(memory_space=pl.ANY)
MESH = pl.DeviceIdType.MESH


def _cp(sem=None):
    return pltpu.CompilerParams(dimension_semantics=sem, vmem_limit_bytes=VMEM_LIMIT)


def _dot(a, b, dims, precision=None):
    return lax.dot_general(a, b, (dims, ((), ())), precision=precision, preferred_element_type=F32)


def _nn(a, b, precision=None):
    return _dot(a, b, ((1,), (0,)), precision)


def _nt(a, b, precision=None):
    return _dot(a, b, ((1,), (1,)), precision)


def _tn(a, b, precision=None):
    return _dot(a, b, ((0,), (0,)), precision)


def _bf(x):
    return x.astype(BF16)


def _sigmoid(x):
    return 1.0 / (1.0 + jnp.exp(-x))


def _interleave(gens):
    results = [None] * len(gens)
    live = list(range(len(gens)))
    while live:
        for i in list(live):
            try:
                next(gens[i])
            except StopIteration as stop:
                results[i] = stop.value
                live.remove(i)
    return results


def _pick(n, pref):
    if n <= pref:
        return n
    t = pref
    while n % t:
        t -= LANES
    assert t > 0, (n, pref)
    return t


def _mm(a, b, mode, out_dtypes, name, epilogue=None, extras=(), tm=1024, tn=2048, tk=512, comm=None):
    if mode == "nn":
        (m, kd), (_, n) = a.shape, b.shape
    elif mode == "nt":
        (m, kd), (n, _) = a.shape, b.shape
    else:
        (kd, m), (_, n) = a.shape, b.shape
    tm, tn, tk = _pick(m, tm), _pick(n, tn), _pick(kd, tk)
    nk = kd // tk
    if mode == "nn":
        a_spec = pl.BlockSpec((tm, tk), lambda i, j, k: (i, k))
        b_spec = pl.BlockSpec((tk, tn), lambda i, j, k: (k, j))
        dims = ((1,), (0,))
    elif mode == "nt":
        a_spec = pl.BlockSpec((tm, tk), lambda i, j, k: (i, k))
        b_spec = pl.BlockSpec((tn, tk), lambda i, j, k: (j, k))
        dims = ((1,), (1,))
    else:
        a_spec = pl.BlockSpec((tk, tm), lambda i, j, k: (k, i))
        b_spec = pl.BlockSpec((tk, tn), lambda i, j, k: (k, j))
        dims = ((0,), (0,))
    o_spec = pl.BlockSpec((tm, tn), lambda i, j, k: (i, j))
    n_extra, n_out = len(extras), len(out_dtypes)

    gm, gn = m // tm, n // tn
    cn = comm.n if comm is not None else 0

    def body(*refs):
        i, j, k = pl.program_id(0), pl.program_id(1), pl.program_id(2)
        at0 = (j == 0) & (k == 0)
        ins, out_refs, scratch, comm_begin, comm_end = _comm_hooks(
            comm, refs, 2 + n_extra, n_out, (i == 0) & at0, (i == gm // 2) & at0,
            (i == gm - 1) & (j == gn - 1) & (k == nk - 1))
        a_ref, b_ref, extra_refs = ins[0], ins[1], ins[2:]
        acc, = scratch
        comm_begin()

        @pl.when(k == 0)
        def _():
            acc[...] = jnp.zeros_like(acc)

        acc[...] += _dot(a_ref[...], b_ref[...], dims)

        @pl.when(k == nk - 1)
        def _():
            if epilogue is None:
                out_refs[0][...] = acc[...].astype(out_dtypes[0])
            else:
                epilogue(acc[...], extra_refs, out_refs)

        comm_end()

    sem = ("arbitrary",) * 3 if cn else ("parallel", "parallel", "arbitrary")
    outs = pl.pallas_call(
        body, name=name,
        grid=(gm, gn, nk),
        in_specs=[a_spec, b_spec] + [o_spec] * n_extra + [ANY] * cn,
        out_specs=[o_spec] * n_out + [ANY] * cn,
        out_shape=[jax.ShapeDtypeStruct((m, n), dt) for dt in out_dtypes] + (comm.out_shapes() if cn else []),
        scratch_shapes=[pltpu.VMEM((tm, tn), F32)] + (comm.scratch() if cn else []),
        compiler_params=_cp(sem),
    )(a, b, *extras, *(comm.arrays if cn else []))
    return outs[0] if n_out + cn == 1 else outs


def _row_spec(tb, d):
    return pl.BlockSpec((tb, d), lambda i: (i, 0))


def _vec_spec(d):
    return pl.BlockSpec((1, d), lambda i: (0, 0))


def _prenorm(x, w, sc, sh, name):
    t, d = x.shape
    tb = _pick(t, 256)

    def body(x_ref, w_ref, sc_ref, sh_ref, h_ref, r_ref):
        xv = x_ref[...]
        r = lax.rsqrt(jnp.mean(xv * xv, axis=-1, keepdims=True) + EPS)
        h_ref[...] = ((xv * r * w_ref[...]) * (1.0 + sc_ref[...]) + sh_ref[...]).astype(BF16)
        r_ref[...] = r

    return pl.pallas_call(
        body, name=name, grid=(t // tb,),
        in_specs=[_row_spec(tb, d), _vec_spec(d), _vec_spec(d), _vec_spec(d)],
        out_specs=[_row_spec(tb, d), _row_spec(tb, 1)],
        out_shape=[jax.ShapeDtypeStruct((t, d), BF16), jax.ShapeDtypeStruct((t, 1), F32)],
        compiler_params=_cp(("parallel",)),
    )(x, w, sc, sh)


def _postnorm_res(x, y, w, gt, name):
    t, d = x.shape
    tb = _pick(t, 256)

    def body(x_ref, y_ref, w_ref, gt_ref, o_ref, r_ref):
        yv = y_ref[...]
        r = lax.rsqrt(jnp.mean(yv * yv, axis=-1, keepdims=True) + EPS)
        o_ref[...] = x_ref[...] + gt_ref[...] * (yv * r * w_ref[...])
        r_ref[...] = r

    return pl.pallas_call(
        body, name=name, grid=(t // tb,),
        in_specs=[_row_spec(tb, d), _row_spec(tb, d), _vec_spec(d), _vec_spec(d)],
        out_specs=[_row_spec(tb, d), _row_spec(tb, 1)],
        out_shape=[jax.ShapeDtypeStruct((t, d), F32), jax.ShapeDtypeStruct((t, 1), F32)],
        compiler_params=_cp(("parallel",)),
    )(x, y, w, gt)


def _final_loss(x, y, w, gt, tgt, name):
    t, d = x.shape
    tb = _pick(t, 256)

    def body(x_ref, y_ref, w_ref, gt_ref, tgt_ref, dout_ref, r_ref, loss_ref):
        @pl.when(pl.program_id(0) == 0)
        def _():
            loss_ref[...] = jnp.zeros_like(loss_ref)

        yv = y_ref[...]
        r = lax.rsqrt(jnp.mean(yv * yv, axis=-1, keepdims=True) + EPS)
        out = x_ref[...] + gt_ref[...] * (yv * r * w_ref[...])
        diff = out - tgt_ref[...]
        row = jnp.mean(diff * diff, axis=-1, keepdims=True)
        loss_ref[...] += 0.5 * jnp.sum(row, axis=0, keepdims=True)
        dout_ref[...] = diff * (1.0 / d)
        r_ref[...] = r

    return pl.pallas_call(
        body, name=name, grid=(t // tb,),
        in_specs=[_row_spec(tb, d), _row_spec(tb, d), _vec_spec(d), _vec_spec(d), _row_spec(tb, d)],
        out_specs=[_row_spec(tb, d), _row_spec(tb, 1), pl.BlockSpec((1, 1), lambda i: (0, 0))],
        out_shape=[jax.ShapeDtypeStruct((t, d), F32), jax.ShapeDtypeStruct((t, 1), F32),
                   jax.ShapeDtypeStruct((1, 1), F32)],
        compiler_params=_cp(("arbitrary",)),
    )(x, y, w, gt, tgt)


def _postnorm_bwd(dxn, y, r, w, gt, name):
    t, d = y.shape
    tb = _pick(t, 256)

    def body(dx_ref, y_ref, r_ref, w_ref, gt_ref, dy_ref, dgt_ref, dw_ref):
        @pl.when(pl.program_id(0) == 0)
        def _():
            dgt_ref[...] = jnp.zeros_like(dgt_ref)
            dw_ref[...] = jnp.zeros_like(dw_ref)

        dxv, rv, wv = dx_ref[...], r_ref[...], w_ref[...]
        z = y_ref[...] * rv
        dgt_ref[...] += jnp.sum(dxv * (z * wv), axis=0, keepdims=True)
        dn = dxv * gt_ref[...]
        dw_ref[...] += jnp.sum(dn * z, axis=0, keepdims=True)
        dz = dn * wv
        dy_ref[...] = (rv * (dz - z * jnp.mean(dz * z, axis=-1, keepdims=True))).astype(BF16)

    return pl.pallas_call(
        body, name=name, grid=(t // tb,),
        in_specs=[_row_spec(tb, d), _row_spec(tb, d), _row_spec(tb, 1), _vec_spec(d), _vec_spec(d)],
        out_specs=[_row_spec(tb, d), _vec_spec(d), _vec_spec(d)],
        out_shape=[jax.ShapeDtypeStruct((t, d), BF16), jax.ShapeDtypeStruct((1, d), F32),
                   jax.ShapeDtypeStruct((1, d), F32)],
        compiler_params=_cp(("arbitrary",)),
    )(dxn, y, r, w, gt)


def _prenorm_bwd(dh, x, r, w, sc, dres, name):
    t, d = x.shape
    tb = _pick(t, 256)

    def body(dh_ref, x_ref, r_ref, w_ref, sc_ref, dres_ref, dx_ref, dsh_ref, dsc_ref, dw_ref):
        @pl.when(pl.program_id(0) == 0)
        def _():
            dsh_ref[...] = jnp.zeros_like(dsh_ref)
            dsc_ref[...] = jnp.zeros_like(dsc_ref)
            dw_ref[...] = jnp.zeros_like(dw_ref)

        dhv, rv, wv = dh_ref[...], r_ref[...], w_ref[...]
        z = x_ref[...] * rv
        dsh_ref[...] += jnp.sum(dhv, axis=0, keepdims=True)
        dsc_ref[...] += jnp.sum(dhv * (z * wv), axis=0, keepdims=True)
        dzw = dhv * (1.0 + sc_ref[...])
        dw_ref[...] += jnp.sum(dzw * z, axis=0, keepdims=True)
        dz = dzw * wv
        dx_ref[...] = dres_ref[...] + rv * (dz - z * jnp.mean(dz * z, axis=-1, keepdims=True))

    return pl.pallas_call(
        body, name=name, grid=(t // tb,),
        in_specs=[_row_spec(tb, d), _row_spec(tb, d), _row_spec(tb, 1), _vec_spec(d), _vec_spec(d),
                  _row_spec(tb, d)],
        out_specs=[_row_spec(tb, d), _vec_spec(d), _vec_spec(d), _vec_spec(d)],
        out_shape=[jax.ShapeDtypeStruct((t, d), F32)] + [jax.ShapeDtypeStruct((1, d), F32)] * 3,
        compiler_params=_cp(("arbitrary",)),
    )(dh, x, r, w, sc, dres)


def _headnorm_fwd(o, proj, g_blk, nw, name):
    t, wd = o.shape
    nh = wd // HD
    tb = _pick(t, 512)
    gb = g_blk * HD // wd

    def body(o_ref, g_ref, nw_ref, out_ref):
        o3 = o_ref[...].reshape(tb, nh, HD)
        g3 = g_ref[...].reshape(tb, nh, HD)
        rh = lax.rsqrt(jnp.mean(o3 * o3, axis=-1, keepdims=True) + EPS)
        res = (o3 * rh * nw_ref[...].reshape(1, 1, HD)) * (g3 * _sigmoid(g3))
        out_ref[...] = res.reshape(tb, wd).astype(BF16)

    return pl.pallas_call(
        body, name=name, grid=(t // tb,),
        in_specs=[_row_spec(tb, wd), pl.BlockSpec((tb, wd), lambda i: (i, gb)), _vec_spec(HD)],
        out_specs=_row_spec(tb, wd),
        out_shape=jax.ShapeDtypeStruct((t, wd), BF16),
        compiler_params=_cp(("parallel",)),
    )(o, proj, nw)


def _headnorm_bwd(dom, col_blk, o, proj, g_blk, nw, name):
    t, wd = o.shape
    nh = wd // HD
    tb = _pick(t, 512)
    gb = g_blk * HD // wd

    def body(do_ref, o_ref, g_ref, nw_ref, dout_ref, dg_ref, dnw_ref):
        @pl.when(pl.program_id(0) == 0)
        def _():
            dnw_ref[...] = jnp.zeros_like(dnw_ref)

        dn = do_ref[...].reshape(tb, nh, HD)
        o3 = o_ref[...].reshape(tb, nh, HD)
        g3 = g_ref[...].reshape(tb, nh, HD)
        nw3 = nw_ref[...].reshape(1, 1, HD)
        rh = lax.rsqrt(jnp.mean(o3 * o3, axis=-1, keepdims=True) + EPS)
        z = o3 * rh
        sg = _sigmoid(g3)
        sl = g3 * sg
        dnw_ref[...] += jnp.sum(jnp.sum(dn * sl * z, axis=1), axis=0, keepdims=True)
        dg_ref[...] = (dn * (z * nw3) * (sg * (1.0 + g3 * (1.0 - sg)))).reshape(tb, wd).astype(BF16)
        dz = dn * sl * nw3
        dout_ref[...] = (rh * (dz - z * jnp.mean(dz * z, axis=-1, keepdims=True))).reshape(tb, wd)

    return pl.pallas_call(
        body, name=name, grid=(t // tb,),
        in_specs=[pl.BlockSpec((tb, wd), lambda i: (i, col_blk)), _row_spec(tb, wd),
                  pl.BlockSpec((tb, wd), lambda i: (i, gb)), _vec_spec(HD)],
        out_specs=[_row_spec(tb, wd), _row_spec(tb, wd), _vec_spec(HD)],
        out_shape=[jax.ShapeDtypeStruct((t, wd), F32), jax.ShapeDtypeStruct((t, wd), BF16),
                   jax.ShapeDtypeStruct((1, HD), F32)],
        compiler_params=_cp(("arbitrary",)),
    )(dom, o, proj, nw)


def _tri(n, kind):
    r = lax.broadcasted_iota(jnp.int32, (n, n), 0)
    c = lax.broadcasted_iota(jnp.int32, (n, n), 1)
    if kind == "lower":
        return r >= c
    if kind == "strict":
        return r > c
    return r <= c


def _hg_gate(fl, l0, l1):
    mx = jnp.maximum(l0, l1)
    e0, e1 = jnp.exp(l0 - mx), jnp.exp(l1 - mx)
    lb = e0 / (e0 + e1)
    sg = _sigmoid(fl)
    f = lb + (1.0 - lb) * sg
    return lb, sg, f


def _hgrn2_fwd(proj, lb_logits, nh, name):
    t = proj.shape[0]
    nc = t // CHUNK
    C = CHUNK
    lg = lb_logits.reshape(2, nh, 1, HD)

    hp = min(HP, nh)
    ng = nh // hp

    def one_head(hh, st, q_ref, f_ref, i_ref, lg_ref, p_sc, r_sc):
        sl = slice(hh * HD, (hh + 1) * HD)
        q, v = q_ref[:, sl], i_ref[:, sl]
        _, _, f = _hg_gate(f_ref[:, sl], lg_ref[0, hh], lg_ref[1, hh])
        k = 1.0 - f
        low = _tri(C, "lower")
        b = _nn(low.astype(F32), jnp.log(f), HI)
        yield
        lane_c = lax.broadcasted_iota(jnp.int32, (SB, C), 1)
        lane_h = lax.broadcasted_iota(jnp.int32, (SB, HD), 1)
        row_h = lax.broadcasted_iota(jnp.int32, (SB, HD), 0)
        ones = jnp.ones((HD, HD), F32)

        for i in range(NSB):
            qi, ki, bi = q[SB * i:SB * (i + 1)], k[SB * i:SB * (i + 1)], b[SB * i:SB * (i + 1)]
            for s in range(SB):
                e = jnp.exp(jnp.minimum(bi - bi[s:s + 1], 0.0))
                p = jnp.where(row_h >= s, qi * ki[s:s + 1] * e, 0.0)
                p_sc[hh, pl.ds((i * SB + s) * SB, SB), :] = p
            yield
        r_sc[hh] = _nn(p_sc[hh], ones, HIGH)
        yield
        a_rows = []
        for i in range(NSB):
            acc = jnp.zeros((SB, HD), F32)
            for s in range(SB):
                acc = jnp.where(lane_h == SB * i + s, r_sc[hh, pl.ds((i * SB + s) * SB, SB), :], acc)
            acc = acc[:, :C]
            if i > 0:
                r = b[SB * i - 1:SB * i]
                bi = b[SB * i:SB * (i + 1)]
                qf = q[SB * i:SB * (i + 1)] * jnp.exp(bi - r)
                kf = k * jnp.exp(jnp.minimum(r - b, 0.0))
                acc = acc + jnp.where(lane_c < SB * i, _nt(qf, kf, HIGH), 0.0)
            a_rows.append(acc)
            yield
        a = jnp.concatenate(a_rows, axis=0)
        bl = b[C - 1:C, :]
        o = _nn(_bf(a), _bf(v)) + _nt(_bf(q * jnp.exp(b)), _bf(st))
        yield
        new_st = st * jnp.exp(bl) + _tn(_bf(v), _bf(k * jnp.exp(bl - b)))
        return o, a, new_st

    def body(q_ref, f_ref, i_ref, lg_ref, o_ref, a_ref, st_ref, s_sc, p_sc, r_sc):
        c, hg = pl.program_id(0), pl.program_id(1)

        @pl.when(c == 0)
        def _():
            for hh in range(hp):
                s_sc[hg * hp + hh] = jnp.zeros((HD, HD), F32)

        sts = [s_sc[hg * hp + hh] for hh in range(hp)]
        res = _interleave([one_head(hh, sts[hh], q_ref, f_ref, i_ref, lg_ref, p_sc, r_sc) for hh in range(hp)])
        for hh in range(hp):
            o_ref[:, hh * HD:(hh + 1) * HD] = res[hh][0]
            a_ref[0, hh] = res[hh][1]
            st_ref[0, hh] = sts[hh]
            s_sc[hg * hp + hh] = res[hh][2]

    blk = lambda off: pl.BlockSpec((C, hp * HD), lambda c, g: (c, off // hp + g))
    return pl.pallas_call(
        body, name=name, grid=(nc, ng),
        in_specs=[blk(0), blk(nh), blk(2 * nh),
                  pl.BlockSpec((2, hp, 1, HD), lambda c, g: (0, g, 0, 0))],
        out_specs=[blk(0),
                   pl.BlockSpec((1, hp, C, C), lambda c, g: (c, g, 0, 0)),
                   pl.BlockSpec((1, hp, HD, HD), lambda c, g: (c, g, 0, 0))],
        out_shape=[jax.ShapeDtypeStruct((t, nh * HD), F32),
                   jax.ShapeDtypeStruct((nc, nh, C, C), F32),
                   jax.ShapeDtypeStruct((nc, nh, HD, HD), F32)],
        scratch_shapes=[pltpu.VMEM((nh, HD, HD), F32), pltpu.VMEM((hp, C * SB, HD), F32),
                        pltpu.VMEM((hp, C * SB, HD), F32)],
        compiler_params=_cp(("arbitrary", "arbitrary")),
    )(proj, proj, proj, lg)


def _hgrn2_bwd(proj, lb_logits, do, a_sv, st_sv, nh, name, comm=None):
    t = proj.shape[0]
    nc = t // CHUNK
    C = CHUNK
    lg = lb_logits.reshape(2, nh, 1, HD)
    hp = min(HP, nh)
    ng = nh // hp

    def one_head(hh, dst, q_ref, f_ref, i_ref, lg_ref, do_ref, a_ref, st_ref, p_sc, r_sc):
        sl = slice(hh * HD, (hh + 1) * HD)
        q, v, do_ = q_ref[:, sl], i_ref[:, sl], do_ref[:, sl]
        lb, sg, f = _hg_gate(f_ref[:, sl], lg_ref[0, hh], lg_ref[1, hh])
        k = 1.0 - f
        low = _tri(C, "lower")
        b = _nn(low.astype(F32), jnp.log(f), HI)
        yield
        bl = b[C - 1:C, :]
        eb, ekb = jnp.exp(b), jnp.exp(bl - b)
        qb, kb = q * eb, k * ekb
        a, st = a_ref[0, hh], st_ref[0, hh]

        da = jnp.where(low, _nt(_bf(do_), _bf(v)), 0.0)
        yield
        dv = _tn(_bf(a), _bf(do_)) + _nt(_bf(kb), _bf(dst))
        yield
        dqb = _nn(_bf(do_), _bf(st))
        dkb = _nn(_bf(v), _bf(dst))
        yield

        row = lax.broadcasted_iota(jnp.int32, (C, HD), 0)
        lane_c = lax.broadcasted_iota(jnp.int32, (SB, C), 1)
        row_h = lax.broadcasted_iota(jnp.int32, (SB, HD), 0)
        ones = jnp.ones((HD, HD), F32)
        sel = (lax.broadcasted_iota(jnp.int32, (C, C * SB), 0)
               == jnp.right_shift(lax.broadcasted_iota(jnp.int32, (C, C * SB), 1), SB.bit_length() - 1)).astype(F32)

        for i in range(NSB):
            doi, vi = do_[SB * i:SB * (i + 1)], v[SB * i:SB * (i + 1)]
            for s in range(SB):
                p_sc[hh, pl.ds((i * SB + s) * SB, SB), :] = doi * vi[s:s + 1]
            yield
        r_sc[hh] = _nn(p_sc[hh], ones, HIGH)
        yield
        dq_rows = []
        dk_off = jnp.zeros((C, HD), F32)
        for i in range(NSB):
            qi, ki, bi = q[SB * i:SB * (i + 1)], k[SB * i:SB * (i + 1)], b[SB * i:SB * (i + 1)]
            acc = jnp.zeros((SB, HD), F32)
            for s in range(SB):
                e = jnp.exp(jnp.minimum(bi - bi[s:s + 1], 0.0))
                g = jnp.where(row_h >= s, r_sc[hh, pl.ds((i * SB + s) * SB, SB), :] * e, 0.0)
                acc = acc + g * ki[s:s + 1]
                p_sc[hh, pl.ds((i * SB + s) * SB, SB), :] = g * qi
            yield
            if i > 0:
                r = b[SB * i - 1:SB * i]
                fq = jnp.exp(bi - r)
                fk = jnp.exp(jnp.minimum(r - b, 0.0))
                dai = jnp.where(lane_c < SB * i, da[SB * i:SB * (i + 1)], 0.0)
                acc = acc + _nn(dai, k * fk, HIGH) * fq
                dk_off = dk_off + _tn(dai, qi * fq, HIGH) * fk
                yield
            dq_rows.append(acc)
        dqi = jnp.concatenate(dq_rows, axis=0)
        dq = dqi + dqb * eb
        dk_inter = dkb * ekb
        dk = _nn(sel, p_sc[hh], HIGH) + dk_off + dk_inter
        yield
        db = q * dq - k * dk
        extra = (jnp.sum(k * dk_inter, axis=0, keepdims=True)
                 + jnp.exp(bl) * jnp.sum(dst * st, axis=0, keepdims=True))
        db = db + jnp.where(row == C - 1, extra, 0.0)
        dlf = _nn(_tri(C, "upper").astype(F32), db, HI)
        yield
        df = dlf / f - dk
        dfl = (df * (1.0 - lb) * sg * (1.0 - sg)).astype(BF16)
        dl = jnp.sum(df * (1.0 - sg), axis=0, keepdims=True) * (lb * (1.0 - lb))
        new_dst = dst * jnp.exp(bl) + _tn(_bf(do_), _bf(qb))
        return dq.astype(BF16), dfl, dv.astype(BF16), dl, new_dst

    def body(*refs):
        c, hg = pl.program_id(0), pl.program_id(1)
        step = c * ng + hg
        ins, outs, scratch, comm_begin, comm_end = _comm_hooks(
            comm, refs, 7, 4, step == 0, step == (nc * ng) // 2, step == nc * ng - 1)
        dq_ref, df_ref, di_ref, dl_ref = outs
        ds_sc, p_sc, r_sc = scratch
        comm_begin()

        @pl.when(c == 0)
        def _():
            for hh in range(hp):
                ds_sc[hg * hp + hh] = jnp.zeros((HD, HD), F32)

        @pl.when(step == 0)
        def _():
            dl_ref[...] = jnp.zeros_like(dl_ref)

        dsts = [ds_sc[hg * hp + hh] for hh in range(hp)]
        res = _interleave([one_head(hh, dsts[hh], *ins, p_sc, r_sc) for hh in range(hp)])
        for hh in range(hp):
            sl = slice(hh * HD, (hh + 1) * HD)
            dq_ref[:, sl], df_ref[:, sl], di_ref[:, sl] = res[hh][0], res[hh][1], res[hh][2]
            dl_ref[pl.ds(hg * hp + hh, 1), :] += res[hh][3]
            ds_sc[hg * hp + hh] = res[hh][4]
        comm_end()

    rblk = lambda off: pl.BlockSpec((C, hp * HD), lambda c, g: (nc - 1 - c, off // hp + g))
    oblk = pl.BlockSpec((C, hp * HD), lambda c, g: (nc - 1 - c, g))
    cn = comm.n if comm is not None else 0
    return pl.pallas_call(
        body, name=name, grid=(nc, ng),
        in_specs=[rblk(0), rblk(nh), rblk(2 * nh),
                  pl.BlockSpec((2, hp, 1, HD), lambda c, g: (0, g, 0, 0)),
                  oblk,
                  pl.BlockSpec((1, hp, C, C), lambda c, g: (nc - 1 - c, g, 0, 0)),
                  pl.BlockSpec((1, hp, HD, HD), lambda c, g: (nc - 1 - c, g, 0, 0))] + [ANY] * cn,
        out_specs=[oblk, oblk, oblk, pl.BlockSpec((nh, HD), lambda c, g: (0, 0))] + [ANY] * cn,
        out_shape=[jax.ShapeDtypeStruct((t, nh * HD), BF16)] * 3 + [jax.ShapeDtypeStruct((nh, HD), F32)]
        + (comm.out_shapes() if cn else []),
        scratch_shapes=[pltpu.VMEM((nh, HD, HD), F32), pltpu.VMEM((hp, C * SB, HD), F32),
                        pltpu.VMEM((hp, C * SB, HD), F32)] + (comm.scratch() if cn else []),
        compiler_params=_cp(("arbitrary", "arbitrary")),
    )(proj, proj, proj, lg, do, a_sv, st_sv, *(comm.arrays if cn else []))


def _shift_rows(u, d, row):
    t = u.shape[0]
    if d == 0:
        return u
    rolled = pltpu.roll(u, d % t, 0)
    if d > 0:
        return jnp.where(row >= d, rolled, 0.0)
    return jnp.where(row < t + d, rolled, 0.0)


def _gdn_prep(proj, conv_w, blk0, nh, name):
    t = proj.shape[0]
    scale = HD ** -0.5

    def body(u_ref, w_ref, o_ref):
        j = pl.program_id(0)
        u, w = u_ref[...], w_ref[...]
        row = lax.broadcasted_iota(jnp.int32, (t, HD), 0)
        y = w[CONV_K - 1:CONV_K, :] * u
        for d in range(1, CONV_K):
            y = y + w[CONV_K - 1 - d:CONV_K - d, :] * _shift_rows(u, d, row)
        a = y * _sigmoid(y)
        n = a * lax.rsqrt(jnp.sum(a * a, axis=-1, keepdims=True) + EPS)
        n = n * jnp.where(j < nh, scale, 1.0)
        o_ref[...] = jnp.where(j < 2 * nh, n, a)

    return pl.pallas_call(
        body, name=name, grid=(3 * nh,),
        in_specs=[pl.BlockSpec((t, HD), lambda j: (0, blk0 + j)), pl.BlockSpec((CONV_K, HD), lambda j: (0, j))],
        out_specs=pl.BlockSpec((t, HD), lambda j: (0, j)),
        out_shape=jax.ShapeDtypeStruct((t, 3 * nh * HD), F32),
        compiler_params=_cp(("parallel",)),
    )(proj, conv_w)


def _gdn_prep_bwd(proj, conv_w, dqkv, blk0, nh, name):
    t = proj.shape[0]
    scale = HD ** -0.5

    def body(u_ref, w_ref, d_ref, du_ref, dw_ref):
        j = pl.program_id(0)
        u, w, dout = u_ref[...], w_ref[...], d_ref[...]
        row = lax.broadcasted_iota(jnp.int32, (t, HD), 0)
        us = [_shift_rows(u, d, row) for d in range(CONV_K)]
        y = w[CONV_K - 1:CONV_K, :] * us[0]
        for d in range(1, CONV_K):
            y = y + w[CONV_K - 1 - d:CONV_K - d, :] * us[d]
        sg = _sigmoid(y)
        a = y * sg
        rs = lax.rsqrt(jnp.sum(a * a, axis=-1, keepdims=True) + EPS)
        n = a * rs
        dn = dout * jnp.where(j < nh, scale, 1.0)
        da_n = rs * (dn - n * jnp.sum(dn * n, axis=-1, keepdims=True))
        da = jnp.where(j < 2 * nh, da_n, dout)
        dy = da * (sg * (1.0 + y * (1.0 - sg)))
        du = w[CONV_K - 1:CONV_K, :] * dy
        for d in range(1, CONV_K):
            du = du + w[CONV_K - 1 - d:CONV_K - d, :] * _shift_rows(dy, -d, row)
        du_ref[...] = du.astype(BF16)
        for d in range(CONV_K):
            dw_ref[CONV_K - 1 - d:CONV_K - d, :] = jnp.sum(dy * us[d], axis=0, keepdims=True)

    return pl.pallas_call(
        body, name=name, grid=(3 * nh,),
        in_specs=[pl.BlockSpec((t, HD), lambda j: (0, blk0 + j)), pl.BlockSpec((CONV_K, HD), lambda j: (0, j)),
                  pl.BlockSpec((t, HD), lambda j: (0, j))],
        out_specs=[pl.BlockSpec((t, HD), lambda j: (0, j)), pl.BlockSpec((CONV_K, HD), lambda j: (0, j))],
        out_shape=[jax.ShapeDtypeStruct((t, 3 * nh * HD), BF16), jax.ShapeDtypeStruct((CONV_K, 3 * nh * HD), F32)],
        compiler_params=_cp(("parallel",)),
    )(proj, conv_w, dqkv)


def _gdn_gates(ab, alog, dtb, h, nh):
    lane = lax.broadcasted_iota(jnp.int32, ab.shape, 1)
    x = ab + dtb
    sp = jnp.maximum(x, 0.0) + jnp.log(1.0 + jnp.exp(-jnp.abs(x)))
    ea = jnp.exp(alog)
    la_all = -ea * sp
    beta_all = _sigmoid(ab)
    pick = lambda val, ln: jnp.sum(jnp.where(lane == ln, val, 0.0), axis=1, keepdims=True)
    la = pick(la_all, h)
    beta = pick(beta_all, nh + h)
    dla_da = pick(-ea * _sigmoid(x), h)
    return la, beta, dla_da


def _gdn_chunks(qs, ks, vs, las, betas, C):
    low, strict = _tri(C, "lower"), _tri(C, "strict")
    eye = (lax.broadcasted_iota(jnp.int32, (C, C), 0) == lax.broadcasted_iota(jnp.int32, (C, C), 1)).astype(F32)
    g_bs = [_nn(low.astype(F32), jnp.broadcast_to(la, (C, HD)), HI) for la in las]
    ps = [_nt(k, k, HI) for k in ks]
    qks = [_nt(_bf(q), _bf(k)) for q, k in zip(qs, ks)]
    chs = []
    for g_b, p, qk_raw, beta in zip(g_bs, ps, qks, betas):
        g_c = g_b[:, :C]
        gamma = jnp.where(low, jnp.exp(jnp.minimum(g_c - g_c.T, 0.0)), 0.0)
        gl = g_b[C - 1:C, :]
        chs.append(dict(gamma=gamma, eg=jnp.exp(g_b), gl=gl, ekt=jnp.exp(gl - g_b), p=p,
                        m=jnp.where(strict, beta * p * gamma, 0.0), qk_raw=qk_raw))
    xs = [eye for _ in chs]
    for s in range(C - 1):
        xs = [x - ch["m"][:, s:s + 1] * x[s:s + 1, :] for x, ch in zip(xs, chs)]
    r_ws = [k * (beta * ch["eg"]) for ch, k, beta in zip(chs, ks, betas)]
    uws = [_nn(x, jnp.concatenate([v * beta, r_w], axis=1), HI) for x, v, beta, r_w in zip(xs, vs, betas, r_ws)]
    for ch, x, r_w, uw in zip(chs, xs, r_ws, uws):
        ch.update(x=x, r_w=r_w, uw=uw)
    return chs


def _gdn_fwd(qkv, proj, ab_blk, alog, dtb, nh, name, comm=None):
    t = qkv.shape[0]
    nc = t // CHUNK
    C = CHUNK
    hp = min(HP, nh)
    ng = nh // hp

    def body(*refs):
        c, hg = pl.program_id(0), pl.program_id(1)
        step = c * ng + hg
        ins, outs, scratch, comm_begin, comm_end = _comm_hooks(
            comm, refs, 6, 3, step == 0, step == (nc * ng) // 2, step == nc * ng - 1)
        q_ref, k_ref, v_ref, ab_ref, al_ref, dt_ref = ins
        o_ref, x_ref, st_ref = outs
        s_sc, = scratch
        comm_begin()

        @pl.when(c == 0)
        def _():
            for hh in range(hp):
                s_sc[hg * hp + hh] = jnp.zeros((HD, HD), F32)

        sls = [slice(hh * HD, (hh + 1) * HD) for hh in range(hp)]
        qs, ks, vs = [q_ref[:, sl] for sl in sls], [k_ref[:, sl] for sl in sls], [v_ref[:, sl] for sl in sls]
        sts = [s_sc[hg * hp + hh] for hh in range(hp)]
        gates = [_gdn_gates(ab_ref[...], al_ref[...], dt_ref[...], hg * hp + hh, nh) for hh in range(hp)]
        chs = _gdn_chunks(qs, ks, vs, [g[0] for g in gates], [g[1] for g in gates], C)
        stbs = [_bf(st) for st in sts]
        vns = [ch["uw"][:, :HD] - _nt(_bf(ch["uw"][:, HD:]), stb) for ch, stb in zip(chs, stbs)]
        o_st = [_nt(_bf(q * ch["eg"]), stb) for q, ch, stb in zip(qs, chs, stbs)]
        outs_ = [o + _nn(_bf(ch["qk_raw"] * ch["gamma"]), _bf(vn)) for o, ch, vn in zip(o_st, chs, vns)]
        new_sts = [st * jnp.exp(ch["gl"]) + _tn(_bf(vn), _bf(k * ch["ekt"]))
                   for st, ch, vn, k in zip(sts, chs, vns, ks)]
        for hh in range(hp):
            o_ref[:, sls[hh]] = outs_[hh]
            x_ref[0, hh] = chs[hh]["x"]
            st_ref[0, hh] = sts[hh]
            s_sc[hg * hp + hh] = new_sts[hh]
        comm_end()

    blk = lambda off: pl.BlockSpec((C, hp * HD), lambda c, g: (c, off // hp + g))
    vec = pl.BlockSpec((1, HD), lambda c, g: (0, 0))
    cn = comm.n if comm is not None else 0
    return pl.pallas_call(
        body, name=name, grid=(nc, ng),
        in_specs=[blk(0), blk(nh), blk(2 * nh), pl.BlockSpec((C, HD), lambda c, g: (c, ab_blk)), vec, vec]
        + [ANY] * cn,
        out_specs=[blk(0),
                   pl.BlockSpec((1, hp,C, C), lambda c, g: (c, g, 0, 0)),
                   pl.BlockSpec((1, hp,HD, HD), lambda c, g: (c, g, 0, 0))] + [ANY] * cn,
        out_shape=[jax.ShapeDtypeStruct((t, nh * HD), F32),
                   jax.ShapeDtypeStruct((nc, nh, C, C), F32),
                   jax.ShapeDtypeStruct((nc, nh, HD, HD), F32)] + (comm.out_shapes() if cn else []),
        scratch_shapes=[pltpu.VMEM((nh, HD, HD), F32)] + (comm.scratch() if cn else []),
        compiler_params=_cp(("arbitrary", "arbitrary")),
    )(qkv, qkv, qkv, proj, alog, dtb, *(comm.arrays if cn else []))


def _gdn_bwd(qkv, proj, ab_blk, alog, dtb, do, x_sv, st_sv, nh, name, comm=None):
    t = qkv.shape[0]
    nc = t // CHUNK
    C = CHUNK
    hp = min(HP, nh)
    ng = nh // hp

    def one_head(h, hh, dst, q_ref, k_ref, v_ref, ab_ref, al_ref, dt_ref, do_ref, x_ref, st_ref):
        sl = slice(hh * HD, (hh + 1) * HD)
        q, k, v, do_ = q_ref[:, sl], k_ref[:, sl], v_ref[:, sl], do_ref[:, sl]
        la, beta, dla_da = _gdn_gates(ab_ref[...], al_ref[...], dt_ref[...], h, nh)
        low, strict = _tri(C, "lower"), _tri(C, "strict")
        g_b = _nn(low.astype(F32), jnp.broadcast_to(la, (C, HD)), HI)
        yield
        g_c = g_b[:, :C]
        gamma = jnp.where(low, jnp.exp(jnp.minimum(g_c - g_c.T, 0.0)), 0.0)
        eg = jnp.exp(g_b)
        gl = g_b[C - 1:C, :]
        ekt = jnp.exp(gl - g_b)
        egl = jnp.exp(gl)
        p = _nt(k, k, HI)
        yield
        x = x_ref[0, hh]
        r_w = k * (beta * eg)
        rhs = jnp.concatenate([v * beta, r_w], axis=1)
        uw = _nn(x, rhs, HI)
        yield
        u, w = uw[:, :HD], uw[:, HD:]
        qk_raw = _nt(_bf(q), _bf(k))
        yield
        qk = qk_raw * gamma
        st = st_ref[0, hh]
        stb, dstb = _bf(st), _bf(dst)
        vn = u - _nt(_bf(w), stb)
        yield
        qd, kt = q * eg, k * ekt

        dvn = _tn(_bf(qk), _bf(do_)) + _nt(_bf(kt), dstb)
        yield
        dq2 = jnp.where(low, _nt(_bf(do_), _bf(vn)), 0.0)
        yield
        dqd = _nn(_bf(do_), stb)
        yield
        dkt = _nn(_bf(vn), dstb)
        yield
        dw = -_nn(_bf(dvn), stb)
        yield
        dxx = jnp.concatenate([dvn, dw], axis=1)
        dr = _tn(x, dxx, HI)
        yield
        dm = -jnp.where(strict, _nt(dr, uw, HI), 0.0)
        yield
        dr_u, dr_w = dr[:, :HD], dr[:, HD:]
        rsum = lambda z: jnp.sum(z, axis=1, keepdims=True)

        dv = dr_u * beta
        dmg = dm * gamma
        dbeta = rsum(dr_u * v) + rsum(dr_w * k) * eg[:, :1] + rsum(dmg * p)
        yield
        dp = dmg * beta
        dq2g = dq2 * gamma
        dk = (dr_w * (beta * eg) + dkt * ekt + _tn(_bf(dq2g), _bf(q))
              + _nn(_bf(dp + dp.T), _bf(k)))
        yield
        dq = dqd * eg + _nn(_bf(dq2g), _bf(k))
        yield
        e = dp * p + dq2g * qk_raw
        t_kt = rsum(dkt * kt)
        dg = rsum(dqd * qd) + rsum(dr_w * r_w) - t_kt + rsum(e) - rsum(e.T)
        yield
        dgl = jnp.sum(t_kt, axis=0, keepdims=True) + jnp.sum(dst * st, keepdims=True) * egl[:, :1]
        rowc = lax.broadcasted_iota(jnp.int32, (C, 1), 0)
        dg = dg + jnp.where(rowc == C - 1, dgl, 0.0)
        dla = _nn(_tri(C, "upper").astype(F32), jnp.broadcast_to(dg, (C, HD)), HI)[:, :1]
        yield
        da = dla * dla_da
        db = dbeta * beta * (1.0 - beta)
        lane = lax.broadcasted_iota(jnp.int32, (C, HD), 1)
        dab = jnp.where(lane == h, da, 0.0) + jnp.where(lane == nh + h, db, 0.0)
        lane1 = lax.broadcasted_iota(jnp.int32, (1, HD), 1)
        d_alog = jnp.where(lane1 == h, jnp.sum(dla * la, axis=0, keepdims=True), 0.0)
        d_dtb = jnp.where(lane1 == h, jnp.sum(da, axis=0, keepdims=True), 0.0)
        new_dst = dst * egl + _tn(_bf(do_), _bf(qd)) - _tn(_bf(dvn), _bf(w))
        return dab, d_alog, d_dtb, new_dst, dq, dk, dv

    def body(*refs):
        c, hg = pl.program_id(0), pl.program_id(1)
        step = c * ng + hg
        ins, outs, scratch, comm_begin, comm_end = _comm_hooks(
            comm, refs, 9, 5, step == 0, step == (nc * ng) // 2, step == nc * ng - 1)
        dq_ref, dk_ref, dv_ref, dab_ref, dpar_ref = outs
        ds_sc, = scratch
        comm_begin()

        @pl.when(c == 0)
        def _():
            for hh in range(hp):
                ds_sc[hg * hp + hh] = jnp.zeros((HD, HD), F32)

        @pl.when(step == 0)
        def _():
            dpar_ref[...] = jnp.zeros_like(dpar_ref)

        @pl.when(hg == 0)
        def _():
            dab_ref[...] = jnp.zeros_like(dab_ref)

        dsts = [ds_sc[hg * hp + hh] for hh in range(hp)]
        res = _interleave([one_head(hg * hp + hh, hh, dsts[hh], *ins) for hh in range(hp)])
        for hh in range(hp):
            sl = slice(hh * HD, (hh + 1) * HD)
            ds_sc[hg * hp + hh] = res[hh][3]
            dq_ref[:, sl], dk_ref[:, sl], dv_ref[:, sl] = res[hh][4], res[hh][5], res[hh][6]
        dab_ref[...] += sum(r[0] for r in res[1:]) + res[0][0]
        dpar_ref[0:1, :] += sum(r[1] for r in res[1:]) + res[0][1]
        dpar_ref[1:2, :] += sum(r[2] for r in res[1:]) + res[0][2]
        comm_end()

    rblk = lambda off: pl.BlockSpec((C, hp * HD), lambda c, g: (nc - 1 - c, off // hp + g))
    oblk = pl.BlockSpec((C, hp * HD), lambda c, g: (nc - 1 - c, g))
    vec = pl.BlockSpec((1, HD), lambda c, g: (0, 0))
    cn = comm.n if comm is not None else 0
    return pl.pallas_call(
        body, name=name, grid=(nc, ng),
        in_specs=[rblk(0), rblk(nh), rblk(2 * nh),
                  pl.BlockSpec((C, HD), lambda c, g: (nc - 1 - c, ab_blk)), vec, vec, oblk,
                  pl.BlockSpec((1, hp,C, C), lambda c, g: (nc - 1 - c, g, 0, 0)),
                  pl.BlockSpec((1, hp,HD, HD), lambda c, g: (nc - 1 - c, g, 0, 0))] + [ANY] * cn,
        out_specs=[oblk, oblk, oblk,
                   pl.BlockSpec((C, HD), lambda c, g: (nc - 1 - c, 0)),
                   pl.BlockSpec((8, HD), lambda c, g: (0, 0))] + [ANY] * cn,
        out_shape=[jax.ShapeDtypeStruct((t, nh * HD), F32)] * 3
        + [jax.ShapeDtypeStruct((t, HD), F32), jax.ShapeDtypeStruct((8, HD), F32)]
        + (comm.out_shapes() if cn else []),
        scratch_shapes=[pltpu.VMEM((nh, HD, HD), F32)] + (comm.scratch() if cn else []),
        compiler_params=_cp(("arbitrary", "arbitrary")),
    )(qkv, qkv, qkv, proj, alog, dtb, do, x_sv, st_sv, *(comm.arrays if cn else []))


def _ada_fwd(c_all, w, b, name):
    nb, d = c_all.shape
    n = w.shape[1]
    tn = _pick(n, 512)

    def body(c_ref, w_ref, b_ref, o_ref):
        cv = c_ref[...]
        o_ref[...] = _nn(cv * _sigmoid(cv), w_ref[...], HI) + b_ref[...]

    return pl.pallas_call(
        body, name=name, grid=(n // tn,),
        in_specs=[pl.BlockSpec((nb, d), lambda j: (0, 0)), pl.BlockSpec((d, tn), lambda j: (0, j)),
                  pl.BlockSpec((1, tn), lambda j: (0, j))],
        out_specs=pl.BlockSpec((nb, tn), lambda j: (0, j)),
        out_shape=jax.ShapeDtypeStruct((nb, n), F32),
        compiler_params=_cp(("parallel",)),
    )(c_all, w, b)


def _ada_wgrad(c_all, dmod, name):
    nb, d = c_all.shape
    n = dmod.shape[1]
    tn = _pick(n, 512)

    def body(c_ref, g_ref, o_ref):
        cv = c_ref[...]
        o_ref[...] = _tn(cv * _sigmoid(cv), g_ref[...], HI)

    return pl.pallas_call(
        body, name=name, grid=(n // tn,),
        in_specs=[pl.BlockSpec((nb, d), lambda j: (0, 0)), pl.BlockSpec((nb, tn), lambda j: (0, j))],
        out_specs=pl.BlockSpec((d, tn), lambda j: (0, j)),
        out_shape=jax.ShapeDtypeStruct((d, n), F32),
        compiler_params=_cp(("parallel",)),
    )(c_all, dmod)


def _adamw(w, m, v, g, name, parts=False):
    r, cdim = w.shape
    tr = r if r <= 256 else _pick_rows(r, 256)
    bc1 = 1.0 - ADAM_B1 ** ADAM_STEP
    bc2 = 1.0 - ADAM_B2 ** ADAM_STEP

    def body(w_ref, m_ref, v_ref, g_ref, go_ref, d_ref, mo_ref, vo_ref):
        if parts:
            gv = g_ref[0].astype(F32)
            for s in range(1, N_DEV):
                gv = gv + g_ref[s].astype(F32)
        else:
            gv = g_ref[...]
        wv = w_ref[...]
        mn = ADAM_B1 * m_ref[...] + (1.0 - ADAM_B1) * gv
        vn = ADAM_B2 * v_ref[...] + (1.0 - ADAM_B2) * (gv * gv)
        m_hat = mn / bc1
        v_hat = vn / bc2
        go_ref[...] = gv
        d_ref[...] = -ADAM_LR * (m_hat / (jnp.sqrt(v_hat) + ADAM_EPS) + ADAM_WD * wv)
        mo_ref[...] = mn
        vo_ref[...] = vn

    spec = pl.BlockSpec((tr, cdim), lambda i: (i, 0))
    gspec = pl.BlockSpec((N_DEV, tr, cdim), lambda i: (0, i, 0)) if parts else spec
    return pl.pallas_call(
        body, name=name, grid=(r // tr,),
        in_specs=[spec, spec, spec, gspec],
        out_specs=[spec] * 4,
        out_shape=[jax.ShapeDtypeStruct((r, cdim), F32)] * 4,
        compiler_params=_cp(("parallel",)),
    )(w, m, v, g)


def _pick_rows(r, pref):
    t = pref
    while r % t:
        t -= 8
    assert t > 0
    return t


def _dev_index(x, y, c):
    return 4 * x + 2 * y + c


class _Comm:
    def __init__(self, kind, arrays):
        self.kind, self.arrays, self.n = kind, list(arrays), len(arrays)

    def out_shapes(self):
        if self.kind == "gather":
            return [jax.ShapeDtypeStruct((N_DEV,) + a.shape, a.dtype) for a in self.arrays]
        return [jax.ShapeDtypeStruct(a.shape, a.dtype) for a in self.arrays]

    def scratch(self):
        return [pltpu.SemaphoreType.DMA((self.n, 7)), pltpu.SemaphoreType.DMA((self.n, 7)),
                pltpu.SemaphoreType.DMA((self.n,))]

    def _gather_parts(self, ins, outs, sems):
        send_sems, recv_sems, local_sems = sems
        x, y, c = lax.axis_index("x"), lax.axis_index("y"), lax.axis_index("c")
        me, sibling = (x, y, c), (x, y, 1 - c)
        chips = [(1 - x, y), (x, 1 - y), (1 - x, 1 - y)]

        def copy(a, k, block, to, src=None):
            slot = outs[a].at[_dev_index(*block)]
            return pltpu.make_async_remote_copy(
                src_ref=slot if src is None else src, dst_ref=slot,
                send_sem=send_sems.at[a, k], recv_sem=recv_sems.at[a, k],
                device_id=to, device_id_type=MESH)

        n = self.n
        mine = [pltpu.make_async_copy(ins[a], outs[a].at[_dev_index(*me)], local_sems.at[a]) for a in range(n)]
        first = []
        for a in range(n):
            first.append(copy(a, 0, me, sibling, src=ins[a]))
            first += [copy(a, 1 + j, me, (*chip, c), src=ins[a]) for j, chip in enumerate(chips)]
        landed = [copy(a, 1 + j, (*chip, c), me) for j, chip in enumerate(chips) for a in range(n)]
        passed = [copy(a, 4 + j, (*chip, c), sibling) for j, chip in enumerate(chips) for a in range(n)]
        late = []
        for a in range(n):
            late.append(copy(a, 0, sibling, me))
            late += [copy(a, 4 + j, (*chip, 1 - c), me) for j, chip in enumerate(chips)]
        return mine, first, landed, passed, late

    def _exchange_parts(self, ins, outs, sems):
        send_sems, recv_sems, local_sems = sems
        x, y, c = lax.axis_index("x"), lax.axis_index("y"), lax.axis_index("c")
        my = _dev_index(x, y, c)
        n = self.n
        mine = [pltpu.make_async_copy(ins[a].at[my], outs[a].at[my], local_sems.at[a]) for a in range(n)]
        sends, recvs = [], []
        for k in range(1, N_DEV):
            px = (1 - x) if (k >> 2) & 1 else x
            py = (1 - y) if (k >> 1) & 1 else y
            pc = (1 - c) if k & 1 else c
            peer = _dev_index(px, py, pc)
            for a in range(n):
                sends.append(pltpu.make_async_remote_copy(
                    src_ref=ins[a].at[peer], dst_ref=outs[a].at[my],
                    send_sem=send_sems.at[a, k - 1], recv_sem=recv_sems.at[a, k - 1],
                    device_id=(px, py, pc), device_id_type=MESH))
                recvs.append(pltpu.make_async_remote_copy(
                    src_ref=ins[a].at[my], dst_ref=outs[a].at[peer],
                    send_sem=send_sems.at[a, k - 1], recv_sem=recv_sems.at[a, k - 1],
                    device_id=(x, y, c), device_id_type=MESH))
        return mine, sends, recvs

    def start(self, ins, outs, sems):
        if self.kind == "gather":
            mine, first, _, _, _ = self._gather_parts(ins, outs, sems)
        else:
            mine, first, _ = self._exchange_parts(ins, outs, sems)
        for cp in mine + first:
            cp.start()

    def mid(self, ins, outs, sems):
        if self.kind == "gather":
            _, _, landed, passed, _ = self._gather_parts(ins, outs, sems)
            for got, fwd in zip(landed, passed):
                got.wait_recv()
                fwd.start()

    def finish(self, ins, outs, sems):
        if self.kind == "gather":
            mine, first, _, passed, late = self._gather_parts(ins, outs, sems)
            for cp in late:
                cp.wait_recv()
            for cp in first + passed:
                cp.wait_send()
        else:
            mine, sends, recvs = self._exchange_parts(ins, outs, sems)
            for cp in sends:
                cp.wait_send()
            for cp in recvs:
                cp.wait_recv()
        for cp in mine:
            cp.wait()

    def run(self, name):
        n = self.n

        def body(*refs):
            ins, outs, sems = refs[:n], refs[n:2 * n], refs[2 * n:]
            self.start(ins, outs, sems)
            self.mid(ins, outs, sems)
            self.finish(ins, outs, sems)

        return pl.pallas_call(
            body, name=name, in_specs=[ANY] * n, out_specs=[ANY] * n,
            out_shape=self.out_shapes(), scratch_shapes=self.scratch(),
        )(*self.arrays)


def _all_gather(arrays, name):
    return _Comm("gather", arrays).run(name)


def _comm_hooks(comm, refs, n_in, n_out, first, middle, last):
    cn = comm.n if comm is not None else 0
    ins, cins = refs[:n_in], refs[n_in:n_in + cn]
    outs, couts = refs[n_in + cn:n_in + cn + n_out], refs[n_in + cn + n_out:n_in + 2 * cn + n_out]
    rest = refs[n_in + 2 * cn + n_out:]
    scratch, csems = (rest[:len(rest) - 3], rest[len(rest) - 3:]) if cn else (rest, ())

    def begin():
        if cn:
            pl.when(first)(lambda: comm.start(cins, couts, csems))
            pl.when(middle)(lambda: comm.mid(cins, couts, csems))

    def end():
        if cn:
            pl.when(last)(lambda: comm.finish(cins, couts, csems))

    return ins, outs, scratch, begin, end


def _local_step(x, tgt, mod, n1, n2, n3, n4, w_in_p, lb_logits, hg_norm, conv_w, alog, dtb, gdn_norm,
                late_w, late_gather=None, parts_a=None, parts_b=None):
    t, d = x.shape
    nh = d // 2 // HD
    ab_blk = 8 * nh
    sh_m, sc_m, gt_m, sh_f, sc_f, gt_f = [mod[i:i + 1] for i in range(6)]

    h1, r1 = _prenorm(x, n1, sc_m, sh_m, "prenorm_mix")
    proj = _mm(h1, w_in_p, "nn", [F32], "mm_proj")
    o_hg, a_sv, hst_sv = _hgrn2_fwd(proj, lb_logits, nh, "hgrn2_fwd")
    qkv = _gdn_prep(proj, conv_w, 4 * nh, nh, "gdn_prep")
    if late_gather is None:
        o_gd, x_sv, gst_sv = _gdn_fwd(qkv, proj, ab_blk, alog, dtb, nh, "gdn_fwd")
        w_out, w_ff1, w_ff2 = late_w
    else:
        o_gd, x_sv, gst_sv, *gathered = _gdn_fwd(qkv, proj, ab_blk, alog, dtb, nh, "gdn_fwd",
                                                 comm=_Comm("gather", late_w))
        w_out, w_ff1, w_ff2 = late_gather(gathered)
    om_hg = _headnorm_fwd(o_hg, proj, 3 * nh, hg_norm, "headnorm_hg")
    om_gd = _headnorm_fwd(o_gd, proj, 7 * nh, gdn_norm, "headnorm_gdn")
    om = jnp.concatenate([om_hg, om_gd], axis=1)
    y1 = _mm(om, w_out, "nn", [F32], "mm_out")
    x1, r2 = _postnorm_res(x, y1, n2, gt_m, "postnorm_mix")
    h2, r3 = _prenorm(x1, n3, sc_f, sh_f, "prenorm_ffn")

    def relu2(acc, extra, outs):
        outs[0][...] = acc
        rl = jnp.maximum(acc, 0.0)
        outs[1][...] = (rl * rl).astype(BF16)

    u, act = _mm(h2, w_ff1, "nn", [F32, BF16], "mm_ff1", epilogue=relu2)
    y2 = _mm(act, w_ff2, "nn", [F32], "mm_ff2")
    dout, r4, loss = _final_loss(x1, y2, n4, gt_f, tgt, "final_loss")

    dy2, dgt_f, dn4 = _postnorm_bwd(dout, y2, r4, n4, gt_f, "postnorm_ffn_bwd")
    dw_ff2 = _mm(act, dy2, "tn", [F32], "mm_dw_ff2")

    def drelu2(acc, extra, outs):
        outs[0][...] = (acc * (2.0 * jnp.maximum(extra[0][...], 0.0))).astype(BF16)

    du = _mm(dy2, w_ff2, "nt", [BF16], "mm_da", epilogue=drelu2, extras=(u,))
    dw_ff1 = _mm(h2, du, "tn", [F32], "mm_dw_ff1")
    dh2 = _mm(du, w_ff1, "nt", [F32], "mm_dh2")
    dx1, dsh_f, dsc_f, dn3 = _prenorm_bwd(dh2, x1, r3, n3, sc_f, dout, "prenorm_ffn_bwd")

    dy1, dgt_m, dn2 = _postnorm_bwd(dx1, y1, r2, n2, gt_m, "postnorm_mix_bwd")
    dw_out = _mm(om, dy1, "tn", [F32], "mm_dw_out")
    dom = _mm(dy1, w_out, "nt", [F32], "mm_dom")
    do_hg, dg_hg, dhgn = _headnorm_bwd(dom, 0, o_hg, proj, 3 * nh, hg_norm, "headnorm_hg_bwd")
    do_gd, dg_gd, dgdn = _headnorm_bwd(dom, 1, o_gd, proj, 7 * nh, gdn_norm, "headnorm_gdn_bwd")
    pa = parts_a(dw_ff2, dw_ff1, dw_out) if parts_a is not None else None
    comm_a1 = _Comm("exchange", pa[:1]) if pa is not None else None
    comm_a2 = _Comm("exchange", pa[1:]) if pa is not None else None
    dq_hg, df_hg, di_hg, dl0, *recv_a1 = _hgrn2_bwd(proj, lb_logits, do_hg, a_sv, hst_sv, nh, "hgrn2_bwd",
                                                    comm=comm_a1)
    dq_g, dk_g, dv_g, dab, dpar, *recv_a2 = _gdn_bwd(qkv, proj, ab_blk, alog, dtb, do_gd, x_sv, gst_sv, nh,
                                                     "gdn_bwd", comm=comm_a2)
    recv_a = recv_a1 + recv_a2
    dqkv = jnp.concatenate([dq_g, dk_g, dv_g], axis=1)
    du_conv, dconv = _gdn_prep_bwd(proj, conv_w, dqkv, 4 * nh, nh, "gdn_prep_bwd")
    dproj = jnp.concatenate([dq_hg, df_hg, di_hg, dg_hg, du_conv, dg_gd, dab.astype(BF16)], axis=1)
    dw_in = _mm(h1, dproj, "tn", [F32], "mm_dw_in")
    comm_b = _Comm("exchange", parts_b(dw_in)) if parts_b is not None else None
    res = _mm(dproj, w_in_p, "nt", [F32], "mm_dh1", tk=640, comm=comm_b)
    dh1, recv_b = (res[0], list(res[1:])) if comm_b is not None else (res, [])
    dx, dsh_m, dsc_m, dn1 = _prenorm_bwd(dh1, x, r1, n1, sc_m, dx1, "prenorm_mix_bwd")

    dmod = jnp.concatenate([dsh_m, dsc_m, dgt_m, dsh_f, dsc_f, dgt_f], axis=0)
    grads = dict(dmod=dmod, n1=dn1, n2=dn2, n3=dn3, n4=dn4, w_in=dw_in, lb0=dl0, hg_norm=dhgn, conv=dconv,
                 alog=dpar[0:1], dtb=dpar[1:2], gdn_norm=dgdn, w_out=dw_out, w_ff1=dw_ff1, w_ff2=dw_ff2,
                 recv_a=recv_a, recv_b=recv_b)
    return loss, dx, grads


def _pack(vals):
    rows = []
    for vv in vals:
        flat = vv.reshape(-1)
        pad = (-flat.shape[0]) % LANES
        if pad:
            flat = jnp.concatenate([flat, jnp.zeros((pad,), flat.dtype)])
        rows.append(flat.reshape(-1, LANES))
    return jnp.concatenate(rows, axis=0)


def _unpack(packed, shapes):
    out, r = [], 0
    for shp in shapes:
        size = 1
        for s in shp:
            size *= s
        nr = -(-size // LANES)
        out.append(packed[r:r + nr].reshape(-1)[:size].reshape(shp))
        r += nr
    return out


def _sum_parts(parts, name):
    _, r, cdim = parts.shape

    def body(p_ref, o_ref):
        acc = p_ref[0]
        for s in range(1, N_DEV):
            acc = acc + p_ref[s]
        o_ref[...] = acc

    return pl.pallas_call(
        body, name=name,
        out_shape=jax.ShapeDtypeStruct((r, cdim), F32),
        compiler_params=_cp(),
    )(parts)


def kernel(x, c, w_ada, b_ada, pre_mix_norm, post_mix_norm, pre_ffn_norm, post_ffn_norm, w_in, hg_lb_logits, hg_norm, gdn_conv_w, gdn_a_log, gdn_dt_bias, gdn_norm, w_out, w_ff1, w_ff2, loss_target, m_w_ada, m_b_ada, m_pre_mix_norm, m_post_mix_norm, m_pre_ffn_norm, m_post_ffn_norm, m_w_in, m_hg_lb_logits, m_hg_norm, m_gdn_conv_w, m_gdn_a_log, m_gdn_dt_bias, m_gdn_norm, m_w_out, m_w_ff1, m_w_ff2, v_w_ada, v_b_ada, v_pre_mix_norm, v_post_mix_norm, v_pre_ffn_norm, v_post_ffn_norm, v_w_in, v_hg_lb_logits, v_hg_norm, v_gdn_conv_w, v_gdn_a_log, v_gdn_dt_bias, v_gdn_norm, v_w_out, v_w_ff1, v_w_ff2):
    t, d = x.shape[1], x.shape[2]
    nh = d // 2 // HD
    in_cols = w_in.shape[2] * N_DEV
    main = in_cols - 2 * nh
    me = _dev_index(lax.axis_index("x"), lax.axis_index("y"), lax.axis_index("c"))

    c_all, conv_g = _all_gather([c, gdn_conv_w[0]], "gather_small")
    c_all = c_all.reshape(N_DEV, d)
    conv_full = conv_g.transpose(1, 0, 2).reshape(CONV_K, -1)
    w_in_g = _all_gather([w_in[0].astype(BF16)], "gather_w_in")[0]
    w_in_full = w_in_g.transpose(1, 0, 2).reshape(d, in_cols)
    w_in_p = jnp.concatenate([w_in_full, jnp.zeros((d, LANES - 2 * nh), BF16)], axis=1)
    late_w = [w_out[0].astype(BF16), w_ff1[0].astype(BF16), w_ff2[0].astype(BF16)]

    def late_gather(gathered):
        w_out_g, w_ff1_g, w_ff2_g = gathered
        return w_out_g.reshape(d, d), w_ff1_g.transpose(1, 0, 2).reshape(d, -1), w_ff2_g.reshape(-1, d)

    n_in = w_in.shape[2]
    n_ff = w_ff1.shape[2]

    def parts_a(dw_ff2, dw_ff1, dw_out):
        return [dw_ff2.reshape(N_DEV, -1, d).astype(BF16),
                dw_ff1.reshape(d, N_DEV, n_ff).transpose(1, 0, 2).astype(BF16),
                dw_out.reshape(N_DEV, d // N_DEV, d).astype(BF16)]

    def parts_b(dw_in):
        return [dw_in[:, :in_cols].reshape(d, N_DEV, n_in).transpose(1, 0, 2).astype(BF16)]

    n_ada = w_ada.shape[2]
    b_loc = lax.dynamic_slice(b_ada, (0, me * n_ada), (1, n_ada))
    mod_part = _ada_fwd(c_all, w_ada[0], b_loc, "ada_fwd")
    mod_all = _all_gather([mod_part], "gather_mod")[0]
    mod = lax.dynamic_slice(mod_all, (0, me, 0), (N_DEV, 1, n_ada)).reshape(6, d)

    pad_lane = lambda vv: jnp.concatenate([vv, jnp.zeros((1, LANES - vv.shape[1]), F32)], axis=1)
    loss, dx, g = _local_step(
        x[0], loss_target[0], mod, pre_mix_norm, post_mix_norm, pre_ffn_norm, post_ffn_norm, w_in_p,
        hg_lb_logits, hg_norm, conv_full, pad_lane(gdn_a_log), pad_lane(gdn_dt_bias), gdn_norm,
        late_w, late_gather, parts_a, parts_b)

    rep_names = ["b_ada", "n1", "n2", "n3", "n4", "lb", "hg_norm", "alog", "dtb", "gdn_norm"]
    rep_w = [b_ada, pre_mix_norm, post_mix_norm, pre_ffn_norm, post_ffn_norm, hg_lb_logits, hg_norm,
             gdn_a_log, gdn_dt_bias, gdn_norm]
    rep_m = [m_b_ada, m_pre_mix_norm, m_post_mix_norm, m_pre_ffn_norm, m_post_ffn_norm, m_hg_lb_logits,
             m_hg_norm, m_gdn_a_log, m_gdn_dt_bias, m_gdn_norm]
    rep_v = [v_b_ada, v_pre_mix_norm, v_post_mix_norm, v_pre_ffn_norm, v_post_ffn_norm, v_hg_lb_logits,
             v_hg_norm, v_gdn_a_log, v_gdn_dt_bias, v_gdn_norm]
    rep_shapes = [a.shape for a in rep_w]
    g_lb = jnp.stack([g["lb0"], -g["lb0"]], axis=0)
    rep_g = [g["dmod"], g["n1"], g["n2"], g["n3"], g["n4"], g_lb, g["hg_norm"],
             g["alog"][:, :nh], g["dtb"][:, :nh], g["gdn_norm"]]
    small = _pack(rep_g + [g["conv"]])
    n_rep_rows = _pack(rep_g).shape[0]
    pad_rows = (-small.shape[0]) % 8
    if pad_rows:
        small = jnp.concatenate([small, jnp.zeros((pad_rows, LANES), F32)], axis=0)
    small_all = _all_gather([small], "gather_small_grads")[0]
    small_sum = _sum_parts(small_all, "sum_small_grads")
    rep_out = _adamw(_pack(rep_w), _pack(rep_m), _pack(rep_v), small_sum[:n_rep_rows], "adamw_small")
    rep_g_o, rep_d_o, rep_m_o, rep_v_o = [dict(zip(rep_names, _unpack(p, rep_shapes))) for p in rep_out]

    conv_sum = small_sum[n_rep_rows:n_rep_rows + CONV_K * conv_full.shape[1] // LANES].reshape(CONV_K, -1)
    n_conv = gdn_conv_w.shape[2]
    conv_loc = lax.dynamic_slice(conv_sum, (0, me * n_conv), (CONV_K, n_conv))
    conv_o = _adamw(gdn_conv_w[0], m_gdn_conv_w[0], v_gdn_conv_w[0], conv_loc, "adamw_conv")

    dmod_all = small_all[:, :6 * d // LANES, :].reshape(N_DEV, 6 * d)
    dmod_loc = lax.dynamic_slice(dmod_all, (0, me * n_ada), (N_DEV, n_ada))
    g_ada = _ada_wgrad(c_all, dmod_loc, "ada_wgrad")
    ada_o = _adamw(w_ada[0], m_w_ada[0], v_w_ada[0], g_ada, "adamw_ada")

    r_ff2, r_ff1, r_out = g["recv_a"]
    r_in, = g["recv_b"]
    in_o = _adamw(w_in[0], m_w_in[0], v_w_in[0], r_in, "adamw_w_in", parts=True)
    out_o = _adamw(w_out[0], m_w_out[0], v_w_out[0], r_out, "adamw_w_out", parts=True)
    ff1_o = _adamw(w_ff1[0], m_w_ff1[0], v_w_ff1[0], r_ff1, "adamw_w_ff1", parts=True)
    ff2_o = _adamw(w_ff2[0], m_w_ff2[0], v_w_ff2[0], r_ff2, "adamw_w_ff2", parts=True)

    loss_tot = lax.psum(loss[0, 0], ("x", "y", "c"))

    def leaf(kind):
        return [ada_o[kind][None], rep_out_d[kind]["b_ada"], rep_out_d[kind]["n1"], rep_out_d[kind]["n2"],
                rep_out_d[kind]["n3"], rep_out_d[kind]["n4"], in_o[kind][None], rep_out_d[kind]["lb"],
                rep_out_d[kind]["hg_norm"], conv_o[kind][None], rep_out_d[kind]["alog"], rep_out_d[kind]["dtb"],
                rep_out_d[kind]["gdn_norm"], out_o[kind][None], ff1_o[kind][None], ff2_o[kind][None]]

    rep_out_d = [rep_g_o, rep_d_o, rep_m_o, rep_v_o]
    return (loss_tot, dx[None], *leaf(0), *leaf(1), *leaf(2), *leaf(3))
```

```python
import functools

import jax
import jax.numpy as jnp
from jax import lax
from jax.experimental import pallas as pl
from jax.experimental.pallas import tpu as pltpu

F32 = jnp.float32
BF16 = jnp.bfloat16
HI = lax.Precision.HIGHEST
HIGH = lax.Precision.HIGH

EPS = 1e-6
CHUNK = 64
SB = 16
NSB = CHUNK // SB
HP = 8
HD = 128
CONV_K = 4
N_DEV = 8
LANES = 128
SUBLANES = 8
VMEM_LIMIT = 56 * 1024 * 1024

ADAM_LR = 0.001
ADAM_B1 = 0.9
ADAM_B2 = 0.999
ADAM_EPS = 1e-08
ADAM_WD = 0.01
ADAM_STEP = 10

ANY = pl.BlockSpec(memory_space=pl.ANY)
MESH = pl.DeviceIdType.MESH


def _cp(sem=None):
    return pltpu.CompilerParams(dimension_semantics=sem, vmem_limit_bytes=VMEM_LIMIT)


def _dot(a, b, dims, precision=None):
    return lax.dot_general(a, b, (dims, ((), ())), precision=precision, preferred_element_type=F32)


def _nn(a, b, precision=None):
    return _dot(a, b, ((1,), (0,)), precision)


def _nt(a, b, precision=None):
    return _dot(a, b, ((1,), (1,)), precision)


def _tn(a, b, precision=None):
    return _dot(a, b, ((0,), (0,)), precision)


def _bf(x):
    return x.astype(BF16)


def _sigmoid(x):
    return 1.0 / (1.0 + jnp.exp(-x))


def _interleave(gens):
    results = [None] * len(gens)
    live = list(range(len(gens)))
    while live:
        for i in list(live):
            try:
                next(gens[i])
            except StopIteration as stop:
                results[i] = stop.value
                live.remove(i)
    return results


def _listed(res):
    return list(res) if isinstance(res, (list, tuple)) else [res]


def _pick(n, pref):
    if n <= pref:
        return n
    t = pref
    while n % t:
        t -= LANES
    assert t > 0, (n, pref)
    return t


def _mm(a, b, mode, out_dtypes, name, epilogue=None, extras=(), tm=1024, tn=2048, tk=512, comm=None):
    if mode == "nn":
        (m, kd), (_, n) = a.shape, b.shape
    elif mode == "nt":
        (m, kd), (n, _) = a.shape, b.shape
    else:
        (kd, m), (_, n) = a.shape, b.shape
    tm, tn, tk = _pick(m, tm), _pick(n, tn), _pick(kd, tk)
    nk = kd // tk
    if mode == "nn":
        a_spec = pl.BlockSpec((tm, tk), lambda i, j, k: (i, k))
        b_spec = pl.BlockSpec((tk, tn), lambda i, j, k: (k, j))
        dims = ((1,), (0,))
    elif mode == "nt":
        a_spec = pl.BlockSpec((tm, tk), lambda i, j, k: (i, k))
        b_spec = pl.BlockSpec((tn, tk), lambda i, j, k: (j, k))
        dims = ((1,), (1,))
    else:
        a_spec = pl.BlockSpec((tk, tm), lambda i, j, k: (k, i))
        b_spec = pl.BlockSpec((tk, tn), lambda i, j, k: (k, j))
        dims = ((0,), (0,))
    o_spec = pl.BlockSpec((tm, tn), lambda i, j, k: (i, j))
    n_extra, n_out = len(extras), len(out_dtypes)

    gm, gn = m // tm, n // tn
    cn = comm.n if comm is not None else 0

    def body(*refs):
        i, j, k = pl.program_id(0), pl.program_id(1), pl.program_id(2)
        at0 = (j == 0) & (k == 0)
        ins, out_refs, scratch, comm_begin, comm_end = _comm_hooks(
            comm, refs, 2 + n_extra, n_out, (i == 0) & at0, (i == gm // 2) & at0,
            (i == gm - 1) & (j == gn - 1) & (k == nk - 1))
        a_ref, b_ref, extra_refs = ins[0], ins[1], ins[2:]
        acc, = scratch
        comm_begin()

        @pl.when(k == 0)
        def _():
            acc[...] = jnp.zeros_like(acc)

        acc[...] += _dot(a_ref[...], b_ref[...], dims)

        @pl.when(k == nk - 1)
        def _():
            if epilogue is None:
                out_refs[0][...] = acc[...].astype(out_dtypes[0])
            else:
                epilogue(acc[...], extra_refs, out_refs)

        comm_end()

    sem = ("arbitrary",) * 3 if cn else ("parallel", "parallel", "arbitrary")
    outs = pl.pallas_call(
        body, name=name,
        grid=(gm, gn, nk),
        in_specs=[a_spec, b_spec] + [o_spec] * n_extra + [ANY] * cn,
        out_specs=[o_spec] * n_out + [ANY] * cn,
        out_shape=[jax.ShapeDtypeStruct((m, n), dt) for dt in out_dtypes] + (comm.out_shapes() if cn else []),
        scratch_shapes=[pltpu.VMEM((tm, tn), F32)] + (comm.scratch() if cn else []),
        compiler_params=_cp(sem),
    )(a, b, *extras, *(comm.arrays if cn else []))
    return outs[0] if n_out + cn == 1 else outs


def _row_spec(tb, d):
    return pl.BlockSpec((tb, d), lambda i: (i, 0))


def _vec_spec(d):
    return pl.BlockSpec((1, d), lambda i: (0, 0))


def _prenorm(x, w, sc, sh, name):
    t, d = x.shape
    tb = _pick(t, 256)

    def body(x_ref, w_ref, sc_ref, sh_ref, h_ref, r_ref):
        xv = x_ref[...]
        r = lax.rsqrt(jnp.mean(xv * xv, axis=-1, keepdims=True) + EPS)
        h_ref[...] = ((xv * r * w_ref[...]) * (1.0 + sc_ref[...]) + sh_ref[...]).astype(BF16)
        r_ref[...] = r

    return pl.pallas_call(
        body, name=name, grid=(t // tb,),
        in_specs=[_row_spec(tb, d), _vec_spec(d), _vec_spec(d), _vec_spec(d)],
        out_specs=[_row_spec(tb, d), _row_spec(tb, 1)],
        out_shape=[jax.ShapeDtypeStruct((t, d), BF16), jax.ShapeDtypeStruct((t, 1), F32)],
        compiler_params=_cp(("parallel",)),
    )(x, w, sc, sh)


def _postnorm_res(x, y, w, gt, name):
    t, d = x.shape
    tb = _pick(t, 256)

    def body(x_ref, y_ref, w_ref, gt_ref, o_ref, r_ref):
        yv = y_ref[...]
        r = lax.rsqrt(jnp.mean(yv * yv, axis=-1, keepdims=True) + EPS)
        o_ref[...] = x_ref[...] + gt_ref[...] * (yv * r * w_ref[...])
        r_ref[...] = r

    return pl.pallas_call(
        body, name=name, grid=(t // tb,),
        in_specs=[_row_spec(tb, d), _row_spec(tb, d), _vec_spec(d), _vec_spec(d)],
        out_specs=[_row_spec(tb, d), _row_spec(tb, 1)],
        out_shape=[jax.ShapeDtypeStruct((t, d), F32), jax.ShapeDtypeStruct((t, 1), F32)],
        compiler_params=_cp(("parallel",)),
    )(x, y, w, gt)


def _final_loss(x, y, w, gt, tgt, name):
    t, d = x.shape
    tb = _pick(t, 256)

    def body(x_ref, y_ref, w_ref, gt_ref, tgt_ref, dout_ref, r_ref, loss_ref):
        @pl.when(pl.program_id(0) == 0)
        def _():
            loss_ref[...] = jnp.zeros_like(loss_ref)

        yv = y_ref[...]
        r = lax.rsqrt(jnp.mean(yv * yv, axis=-1, keepdims=True) + EPS)
        out = x_ref[...] + gt_ref[...] * (yv * r * w_ref[...])
        diff = out - tgt_ref[...]
        row = jnp.mean(diff * diff, axis=-1, keepdims=True)
        loss_ref[...] += 0.5 * jnp.sum(row, axis=0, keepdims=True)
        dout_ref[...] = diff * (1.0 / d)
        r_ref[...] = r

    return pl.pallas_call(
        body, name=name, grid=(t // tb,),
        in_specs=[_row_spec(tb, d), _row_spec(tb, d), _vec_spec(d), _vec_spec(d), _row_spec(tb, d)],
        out_specs=[_row_spec(tb, d), _row_spec(tb, 1), pl.BlockSpec((1, 1), lambda i: (0, 0))],
        out_shape=[jax.ShapeDtypeStruct((t, d), F32), jax.ShapeDtypeStruct((t, 1), F32),
                   jax.ShapeDtypeStruct((1, 1), F32)],
        compiler_params=_cp(("arbitrary",)),
    )(x, y, w, gt, tgt)


def _postnorm_bwd(dxn, y, r, w, gt, name):
    t, d = y.shape
    tb = _pick(t, 256)

    def body(dx_ref, y_ref, r_ref, w_ref, gt_ref, dy_ref, dgt_ref, dw_ref):
        @pl.when(pl.program_id(0) == 0)
        def _():
            dgt_ref[...] = jnp.zeros_like(dgt_ref)
            dw_ref[...] = jnp.zeros_like(dw_ref)

        dxv, rv, wv = dx_ref[...], r_ref[...], w_ref[...]
        z = y_ref[...] * rv
        dgt_ref[...] += jnp.sum(dxv * (z * wv), axis=0, keepdims=True)
        dn = dxv * gt_ref[...]
        dw_ref[...] += jnp.sum(dn * z, axis=0, keepdims=True)
        dz = dn * wv
        dy_ref[...] = (rv * (dz - z * jnp.mean(dz * z, axis=-1, keepdims=True))).astype(BF16)

    return pl.pallas_call(
        body, name=name, grid=(t // tb,),
        in_specs=[_row_spec(tb, d), _row_spec(tb, d), _row_spec(tb, 1), _vec_spec(d), _vec_spec(d)],
        out_specs=[_row_spec(tb, d), _vec_spec(d), _vec_spec(d)],
        out_shape=[jax.ShapeDtypeStruct((t, d), BF16), jax.ShapeDtypeStruct((1, d), F32),
                   jax.ShapeDtypeStruct((1, d), F32)],
        compiler_params=_cp(("arbitrary",)),
    )(dxn, y, r, w, gt)


def _prenorm_bwd(dh, x, r, w, sc, dres, name):
    t, d = x.shape
    tb = _pick(t, 256)

    def body(dh_ref, x_ref, r_ref, w_ref, sc_ref, dres_ref, dx_ref, dsh_ref, dsc_ref, dw_ref):
        @pl.when(pl.program_id(0) == 0)
        def _():
            dsh_ref[...] = jnp.zeros_like(dsh_ref)
            dsc_ref[...] = jnp.zeros_like(dsc_ref)
            dw_ref[...] = jnp.zeros_like(dw_ref)

        dhv, rv, wv = dh_ref[...], r_ref[...], w_ref[...]
        z = x_ref[...] * rv
        dsh_ref[...] += jnp.sum(dhv, axis=0, keepdims=True)
        dsc_ref[...] += jnp.sum(dhv * (z * wv), axis=0, keepdims=True)
        dzw = dhv * (1.0 + sc_ref[...])
        dw_ref[...] += jnp.sum(dzw * z, axis=0, keepdims=True)
        dz = dzw * wv
        dx_ref[...] = dres_ref[...] + rv * (dz - z * jnp.mean(dz * z, axis=-1, keepdims=True))

    return pl.pallas_call(
        body, name=name, grid=(t // tb,),
        in_specs=[_row_spec(tb, d), _row_spec(tb, d), _row_spec(tb, 1), _vec_spec(d), _vec_spec(d),
                  _row_spec(tb, d)],
        out_specs=[_row_spec(tb, d), _vec_spec(d), _vec_spec(d), _vec_spec(d)],
        out_shape=[jax.ShapeDtypeStruct((t, d), F32)] + [jax.ShapeDtypeStruct((1, d), F32)] * 3,
        compiler_params=_cp(("arbitrary",)),
    )(dh, x, r, w, sc, dres)


def _headnorm_fwd(o, proj, g_blk, nw, name):
    t, wd = o.shape
    nh = wd // HD
    tb = _pick(t, 512)
    gb = g_blk * HD // wd

    def body(o_ref, g_ref, nw_ref, out_ref):
        o3 = o_ref[...].reshape(tb, nh, HD)
        g3 = g_ref[...].reshape(tb, nh, HD)
        rh = lax.rsqrt(jnp.mean(o3 * o3, axis=-1, keepdims=True) + EPS)
        res = (o3 * rh * nw_ref[...].reshape(1, 1, HD)) * (g3 * _sigmoid(g3))
        out_ref[...] = res.reshape(tb, wd).astype(BF16)

    return pl.pallas_call(
        body, name=name, grid=(t // tb,),
        in_specs=[_row_spec(tb, wd), pl.BlockSpec((tb, wd), lambda i: (i, gb)), _vec_spec(HD)],
        out_specs=_row_spec(tb, wd),
        out_shape=jax.ShapeDtypeStruct((t, wd), BF16),
        compiler_params=_cp(("parallel",)),
    )(o, proj, nw)


def _headnorm_bwd(dom, col_blk, o, proj, g_blk, nw, name):
    t, wd = o.shape
    nh = wd // HD
    tb = _pick(t, 512)
    gb = g_blk * HD // wd

    def body(do_ref, o_ref, g_ref, nw_ref, dout_ref, dg_ref, dnw_ref):
        @pl.when(pl.program_id(0) == 0)
        def _():
            dnw_ref[...] = jnp.zeros_like(dnw_ref)

        dn = do_ref[...].reshape(tb, nh, HD)
        o3 = o_ref[...].reshape(tb, nh, HD)
        g3 = g_ref[...].reshape(tb, nh, HD)
        nw3 = nw_ref[...].reshape(1, 1, HD)
        rh = lax.rsqrt(jnp.mean(o3 * o3, axis=-1, keepdims=True) + EPS)
        z = o3 * rh
        sg = _sigmoid(g3)
        sl = g3 * sg
        dnw_ref[...] += jnp.sum(jnp.sum(dn * sl * z, axis=1), axis=0, keepdims=True)
        dg_ref[...] = (dn * (z * nw3) * (sg * (1.0 + g3 * (1.0 - sg)))).reshape(tb, wd).astype(BF16)
        dz = dn * sl * nw3
        dout_ref[...] = (rh * (dz - z * jnp.mean(dz * z, axis=-1, keepdims=True))).reshape(tb, wd)

    return pl.pallas_call(
        body, name=name, grid=(t // tb,),
        in_specs=[pl.BlockSpec((tb, wd), lambda i: (i, col_blk)), _row_spec(tb, wd),
                  pl.BlockSpec((tb, wd), lambda i: (i, gb)), _vec_spec(HD)],
        out_specs=[_row_spec(tb, wd), _row_spec(tb, wd), _vec_spec(HD)],
        out_shape=[jax.ShapeDtypeStruct((t, wd), F32), jax.ShapeDtypeStruct((t, wd), BF16),
                   jax.ShapeDtypeStruct((1, HD), F32)],
        compiler_params=_cp(("arbitrary",)),
    )(dom, o, proj, nw)


def _tri(n, kind):
    r = lax.broadcasted_iota(jnp.int32, (n, n), 0)
    c = lax.broadcasted_iota(jnp.int32, (n, n), 1)
    if kind == "lower":
        return r >= c
    if kind == "strict":
        return r > c
    return r <= c


def _hg_gate(fl, l0, l1):
    mx = jnp.maximum(l0, l1)
    e0, e1 = jnp.exp(l0 - mx), jnp.exp(l1 - mx)
    lb = e0 / (e0 + e1)
    sg = _sigmoid(fl)
    f = lb + (1.0 - lb) * sg
    return lb, sg, f


def _hgrn2_fwd(proj, lb_logits, nh, name, comm=None):
    t = proj.shape[0]
    nc = t // CHUNK
    C = CHUNK
    lg = lb_logits.reshape(2, nh, 1, HD)

    hp = min(HP, nh)
    ng = nh // hp

    def one_head(hh, st, q_ref, f_ref, i_ref, lg_ref, p_sc, r_sc):
        sl = slice(hh * HD, (hh + 1) * HD)
        q, v = q_ref[:, sl], i_ref[:, sl]
        _, _, f = _hg_gate(f_ref[:, sl], lg_ref[0, hh], lg_ref[1, hh])
        k = 1.0 - f
        low = _tri(C, "lower")
        b = _nn(low.astype(F32), jnp.log(f), HI)
        yield
        lane_c = lax.broadcasted_iota(jnp.int32, (SB, C), 1)
        lane_h = lax.broadcasted_iota(jnp.int32, (SB, HD), 1)
        row_h = lax.broadcasted_iota(jnp.int32, (SB, HD), 0)
        ones = jnp.ones((HD, HD), F32)

        for i in range(NSB):
            qi, ki, bi = q[SB * i:SB * (i + 1)], k[SB * i:SB * (i + 1)], b[SB * i:SB * (i + 1)]
            for s in range(SB):
                e = jnp.exp(jnp.minimum(bi - bi[s:s + 1], 0.0))
                p = jnp.where(row_h >= s, qi * ki[s:s + 1] * e, 0.0)
                p_sc[hh, pl.ds((i * SB + s) * SB, SB), :] = p
            yield
        r_sc[hh] = _nn(p_sc[hh], ones, HIGH)
        yield
        a_rows = []
        for i in range(NSB):
            acc = jnp.zeros((SB, HD), F32)
            for s in range(SB):
                acc = jnp.where(lane_h == SB * i + s, r_sc[hh, pl.ds((i * SB + s) * SB, SB), :], acc)
            acc = acc[:, :C]
            if i > 0:
                r = b[SB * i - 1:SB * i]
                bi = b[SB * i:SB * (i + 1)]
                qf = q[SB * i:SB * (i + 1)] * jnp.exp(bi - r)
                kf = k * jnp.exp(jnp.minimum(r - b, 0.0))
                acc = acc + jnp.where(lane_c < SB * i, _nt(qf, kf, HIGH), 0.0)
            a_rows.append(acc)
            yield
        a = jnp.concatenate(a_rows, axis=0)
        bl = b[C - 1:C, :]
        o = _nn(_bf(a), _bf(v)) + _nt(_bf(q * jnp.exp(b)), _bf(st))
        yield
        new_st = st * jnp.exp(bl) + _tn(_bf(v), _bf(k * jnp.exp(bl - b)))
        return o, a, new_st

    def body(*refs):
        c, hg = pl.program_id(0), pl.program_id(1)
        step = c * ng + hg
        ins, outs, scratch, comm_begin, comm_end = _comm_hooks(
            comm, refs, 4, 3, step == 0, step == (nc * ng) // 2, step == nc * ng - 1)
        o_ref, a_ref, st_ref = outs
        s_sc, p_sc, r_sc = scratch
        comm_begin()

        @pl.when(c == 0)
        def _():
            for hh in range(hp):
                s_sc[hg * hp + hh] = jnp.zeros((HD, HD), F32)

        sts = [s_sc[hg * hp + hh] for hh in range(hp)]
        res = _interleave([one_head(hh, sts[hh], *ins, p_sc, r_sc) for hh in range(hp)])
        for hh in range(hp):
            o_ref[:, hh * HD:(hh + 1) * HD] = res[hh][0]
            a_ref[0, hh] = res[hh][1]
            st_ref[0, hh] = sts[hh]
            s_sc[hg * hp + hh] = res[hh][2]
        comm_end()

    blk = lambda off: pl.BlockSpec((C, hp * HD), lambda c, g: (c, off // hp + g))
    cn = comm.n if comm is not None else 0
    return pl.pallas_call(
        body, name=name, grid=(nc, ng),
        in_specs=[blk(0), blk(nh), blk(2 * nh),
                  pl.BlockSpec((2, hp, 1, HD), lambda c, g: (0, g, 0, 0))] + [ANY] * cn,
        out_specs=[blk(0),
                   pl.BlockSpec((1, hp, C, C), lambda c, g: (c, g, 0, 0)),
                   pl.BlockSpec((1, hp, HD, HD), lambda c, g: (c, g, 0, 0))] + [ANY] * cn,
        out_shape=[jax.ShapeDtypeStruct((t, nh * HD), F32),
                   jax.ShapeDtypeStruct((nc, nh, C, C), F32),
                   jax.ShapeDtypeStruct((nc, nh, HD, HD), F32)] + (comm.out_shapes() if cn else []),
        scratch_shapes=[pltpu.VMEM((nh, HD, HD), F32), pltpu.VMEM((hp, C * SB, HD), F32),
                        pltpu.VMEM((hp, C * SB, HD), F32)] + (comm.scratch() if cn else []),
        compiler_params=_cp(("arbitrary", "arbitrary")),
    )(proj, proj, proj, lg, *(comm.arrays if cn else []))


def _hgrn2_bwd(proj, lb_logits, do, a_sv, st_sv, nh, name, comm=None):
    t = proj.shape[0]
    nc = t // CHUNK
    C = CHUNK
    lg = lb_logits.reshape(2, nh, 1, HD)
    hp = min(HP, nh)
    ng = nh // hp

    def one_head(hh, dst, q_ref, f_ref, i_ref, lg_ref, do_ref, a_ref, st_ref, p_sc, r_sc):
        sl = slice(hh * HD, (hh + 1) * HD)
        q, v, do_ = q_ref[:, sl], i_ref[:, sl], do_ref[:, sl]
        lb, sg, f = _hg_gate(f_ref[:, sl], lg_ref[0, hh], lg_ref[1, hh])
        k = 1.0 - f
        low = _tri(C, "lower")
        b = _nn(low.astype(F32), jnp.log(f), HI)
        yield
        bl = b[C - 1:C, :]
        eb, ekb = jnp.exp(b), jnp.exp(bl - b)
        qb, kb = q * eb, k * ekb
        a, st = a_ref[0, hh], st_ref[0, hh]

        da = jnp.where(low, _nt(_bf(do_), _bf(v)), 0.0)
        yield
        dv = _tn(_bf(a), _bf(do_)) + _nt(_bf(kb), _bf(dst))
        yield
        dqb = _nn(_bf(do_), _bf(st))
        dkb = _nn(_bf(v), _bf(dst))
        yield

        row = lax.broadcasted_iota(jnp.int32, (C, HD), 0)
        lane_c = lax.broadcasted_iota(jnp.int32, (SB, C), 1)
        row_h = lax.broadcasted_iota(jnp.int32, (SB, HD), 0)
        ones = jnp.ones((HD, HD), F32)
        sel = (lax.broadcasted_iota(jnp.int32, (C, C * SB), 0)
               == jnp.right_shift(lax.broadcasted_iota(jnp.int32, (C, C * SB), 1), SB.bit_length() - 1)).astype(F32)

        for i in range(NSB):
            doi, vi = do_[SB * i:SB * (i + 1)], v[SB * i:SB * (i + 1)]
            for s in range(SB):
                p_sc[hh, pl.ds((i * SB + s) * SB, SB), :] = doi * vi[s:s + 1]
            yield
        r_sc[hh] = _nn(p_sc[hh], ones, HIGH)
        yield
        dq_rows = []
        dk_off = jnp.zeros((C, HD), F32)
        for i in range(NSB):
            qi, ki, bi = q[SB * i:SB * (i + 1)], k[SB * i:SB * (i + 1)], b[SB * i:SB * (i + 1)]
            acc = jnp.zeros((SB, HD), F32)
            for s in range(SB):
                e = jnp.exp(jnp.minimum(bi - bi[s:s + 1], 0.0))
                g = jnp.where(row_h >= s, r_sc[hh, pl.ds((i * SB + s) * SB, SB), :] * e, 0.0)
                acc = acc + g * ki[s:s + 1]
                p_sc[hh, pl.ds((i * SB + s) * SB, SB), :] = g * qi
            yield
            if i > 0:
                r = b[SB * i - 1:SB * i]
                fq = jnp.exp(bi - r)
                fk = jnp.exp(jnp.minimum(r - b, 0.0))
                dai = jnp.where(lane_c < SB * i, da[SB * i:SB * (i + 1)], 0.0)
                acc = acc + _nn(dai, k * fk, HIGH) * fq
                dk_off = dk_off + _tn(dai, qi * fq, HIGH) * fk
                yield
            dq_rows.append(acc)
        dqi = jnp.concatenate(dq_rows, axis=0)
        dq = dqi + dqb * eb
        dk_inter = dkb * ekb
        dk = _nn(sel, p_sc[hh], HIGH) + dk_off + dk_inter
        yield
        db = q * dq - k * dk
        extra = (jnp.sum(k * dk_inter, axis=0, keepdims=True)
                 + jnp.exp(bl) * jnp.sum(dst * st, axis=0, keepdims=True))
        db = db + jnp.where(row == C - 1, extra, 0.0)
        dlf = _nn(_tri(C, "upper").astype(F32), db, HI)
        yield
        df = dlf / f - dk
        dfl = (df * (1.0 - lb) * sg * (1.0 - sg)).astype(BF16)
        dl = jnp.sum(df * (1.0 - sg), axis=0, keepdims=True) * (lb * (1.0 - lb))
        new_dst = dst * jnp.exp(bl) + _tn(_bf(do_), _bf(qb))
        return dq.astype(BF16), dfl, dv.astype(BF16), dl, new_dst

    def body(*refs):
        c, hg = pl.program_id(0), pl.program_id(1)
        step = c * ng + hg
        ins, outs, scratch, comm_begin, comm_end = _comm_hooks(
            comm, refs, 7, 4, step == 0, step == (nc * ng) // 2, step == nc * ng - 1)
        dq_ref, df_ref, di_ref, dl_ref = outs
        ds_sc, p_sc, r_sc = scratch
        comm_begin()

        @pl.when(c == 0)
        def _():
            for hh in range(hp):
                ds_sc[hg * hp + hh] = jnp.zeros((HD, HD), F32)

        @pl.when(step == 0)
        def _():
            dl_ref[...] = jnp.zeros_like(dl_ref)

        dsts = [ds_sc[hg * hp + hh] for hh in range(hp)]
        res = _interleave([one_head(hh, dsts[hh], *ins, p_sc, r_sc) for hh in range(hp)])
        for hh in range(hp):
            sl = slice(hh * HD, (hh + 1) * HD)
            dq_ref[:, sl], df_ref[:, sl], di_ref[:, sl] = res[hh][0], res[hh][1], res[hh][2]
            dl_ref[pl.ds(hg * hp + hh, 1), :] += res[hh][3]
            ds_sc[hg * hp + hh] = res[hh][4]
        comm_end()

    rblk = lambda off: pl.BlockSpec((C, hp * HD), lambda c, g: (nc - 1 - c, off // hp + g))
    oblk = pl.BlockSpec((C, hp * HD), lambda c, g: (nc - 1 - c, g))
    cn = comm.n if comm is not None else 0
    return pl.pallas_call(
        body, name=name, grid=(nc, ng),
        in_specs=[rblk(0), rblk(nh), rblk(2 * nh),
                  pl.BlockSpec((2, hp, 1, HD), lambda c, g: (0, g, 0, 0)),
                  oblk,
                  pl.BlockSpec((1, hp, C, C), lambda c, g: (nc - 1 - c, g, 0, 0)),
                  pl.BlockSpec((1, hp, HD, HD), lambda c, g: (nc - 1 - c, g, 0, 0))] + [ANY] * cn,
        out_specs=[oblk, oblk, oblk, pl.BlockSpec((nh, HD), lambda c, g: (0, 0))] + [ANY] * cn,
        out_shape=[jax.ShapeDtypeStruct((t, nh * HD), BF16)] * 3 + [jax.ShapeDtypeStruct((nh, HD), F32)]
        + (comm.out_shapes() if cn else []),
        scratch_shapes=[pltpu.VMEM((nh, HD, HD), F32), pltpu.VMEM((hp, C * SB, HD), F32),
                        pltpu.VMEM((hp, C * SB, HD), F32)] + (comm.scratch() if cn else []),
        compiler_params=_cp(("arbitrary", "arbitrary")),
    )(proj, proj, proj, lg, do, a_sv, st_sv, *(comm.arrays if cn else []))


def _shift_rows(u, d, row):
    t = u.shape[0]
    if d == 0:
        return u
    rolled = pltpu.roll(u, d % t, 0)
    if d > 0:
        return jnp.where(row >= d, rolled, 0.0)
    return jnp.where(row < t + d, rolled, 0.0)


def _gdn_prep(proj, conv_w, blk0, nh, name):
    t = proj.shape[0]
    scale = HD ** -0.5

    def body(u_ref, w_ref, o_ref):
        j = pl.program_id(0)
        u, w = u_ref[...], w_ref[...]
        row = lax.broadcasted_iota(jnp.int32, (t, HD), 0)
        y = w[CONV_K - 1:CONV_K, :] * u
        for d in range(1, CONV_K):
            y = y + w[CONV_K - 1 - d:CONV_K - d, :] * _shift_rows(u, d, row)
        a = y * _sigmoid(y)
        n = a * lax.rsqrt(jnp.sum(a * a, axis=-1, keepdims=True) + EPS)
        n = n * jnp.where(j < nh, scale, 1.0)
        o_ref[...] = jnp.where(j < 2 * nh, n, a)

    return pl.pallas_call(
        body, name=name, grid=(3 * nh,),
        in_specs=[pl.BlockSpec((t, HD), lambda j: (0, blk0 + j)), pl.BlockSpec((CONV_K, HD), lambda j: (0, j))],
        out_specs=pl.BlockSpec((t, HD), lambda j: (0, j)),
        out_shape=jax.ShapeDtypeStruct((t, 3 * nh * HD), F32),
        compiler_params=_cp(("parallel",)),
    )(proj, conv_w)


def _gdn_prep_bwd(proj, conv_w, dqkv, blk0, nh, name):
    t = proj.shape[0]
    scale = HD ** -0.5

    def body(u_ref, w_ref, d_ref, du_ref, dw_ref):
        j = pl.program_id(0)
        u, w, dout = u_ref[...], w_ref[...], d_ref[...]
        row = lax.broadcasted_iota(jnp.int32, (t, HD), 0)
        us = [_shift_rows(u, d, row) for d in range(CONV_K)]
        y = w[CONV_K - 1:CONV_K, :] * us[0]
        for d in range(1, CONV_K):
            y = y + w[CONV_K - 1 - d:CONV_K - d, :] * us[d]
        sg = _sigmoid(y)
        a = y * sg
        rs = lax.rsqrt(jnp.sum(a * a, axis=-1, keepdims=True) + EPS)
        n = a * rs
        dn = dout * jnp.where(j < nh, scale, 1.0)
        da_n = rs * (dn - n * jnp.sum(dn * n, axis=-1, keepdims=True))
        da = jnp.where(j < 2 * nh, da_n, dout)
        dy = da * (sg * (1.0 + y * (1.0 - sg)))
        du = w[CONV_K - 1:CONV_K, :] * dy
        for d in range(1, CONV_K):
            du = du + w[CONV_K - 1 - d:CONV_K - d, :] * _shift_rows(dy, -d, row)
        du_ref[...] = du.astype(BF16)
        for d in range(CONV_K):
            dw_ref[CONV_K - 1 - d:CONV_K - d, :] = jnp.sum(dy * us[d], axis=0, keepdims=True)

    return pl.pallas_call(
        body, name=name, grid=(3 * nh,),
        in_specs=[pl.BlockSpec((t, HD), lambda j: (0, blk0 + j)), pl.BlockSpec((CONV_K, HD), lambda j: (0, j)),
                  pl.BlockSpec((t, HD), lambda j: (0, j))],
        out_specs=[pl.BlockSpec((t, HD), lambda j: (0, j)), pl.BlockSpec((CONV_K, HD), lambda j: (0, j))],
        out_shape=[jax.ShapeDtypeStruct((t, 3 * nh * HD), BF16), jax.ShapeDtypeStruct((CONV_K, 3 * nh * HD), F32)],
        compiler_params=_cp(("parallel",)),
    )(proj, conv_w, dqkv)


def _gdn_gates(ab, alog, dtb, h, nh):
    lane = lax.broadcasted_iota(jnp.int32, ab.shape, 1)
    x = ab + dtb
    sp = jnp.maximum(x, 0.0) + jnp.log(1.0 + jnp.exp(-jnp.abs(x)))
    ea = jnp.exp(alog)
    la_all = -ea * sp
    beta_all = _sigmoid(ab)
    pick = lambda val, ln: jnp.sum(jnp.where(lane == ln, val, 0.0), axis=1, keepdims=True)
    la = pick(la_all, h)
    beta = pick(beta_all, nh + h)
    dla_da = pick(-ea * _sigmoid(x), h)
    return la, beta, dla_da


def _gdn_chunks(qs, ks, vs, las, betas, C):
    low, strict = _tri(C, "lower"), _tri(C, "strict")
    eye = (lax.broadcasted_iota(jnp.int32, (C, C), 0) == lax.broadcasted_iota(jnp.int32, (C, C), 1)).astype(F32)
    g_bs = [_nn(low.astype(F32), jnp.broadcast_to(la, (C, HD)), HI) for la in las]
    ps = [_nt(k, k, HI) for k in ks]
    qks = [_nt(_bf(q), _bf(k)) for q, k in zip(qs, ks)]
    chs = []
    for g_b, p, qk_raw, beta in zip(g_bs, ps, qks, betas):
        g_c = g_b[:, :C]
        gamma = jnp.where(low, jnp.exp(jnp.minimum(g_c - g_c.T, 0.0)), 0.0)
        gl = g_b[C - 1:C, :]
        chs.append(dict(gamma=gamma, eg=jnp.exp(g_b), gl=gl, ekt=jnp.exp(gl - g_b), p=p,
                        m=jnp.where(strict, beta * p * gamma, 0.0), qk_raw=qk_raw))
    xs = [eye for _ in chs]
    for s in range(C - 1):
        xs = [x - ch["m"][:, s:s + 1] * x[s:s + 1, :] for x, ch in zip(xs, chs)]
    r_ws = [k * (beta * ch["eg"]) for ch, k, beta in zip(chs, ks, betas)]
    uws = [_nn(x, jnp.concatenate([v * beta, r_w], axis=1), HI) for x, v, beta, r_w in zip(xs, vs, betas, r_ws)]
    for ch, x, r_w, uw in zip(chs, xs, r_ws, uws):
        ch.update(x=x, r_w=r_w, uw=uw)
    return chs


def _gdn_fwd(qkv, proj, ab_blk, alog, dtb, nh, name, comm=None):
    t = qkv.shape[0]
    nc = t // CHUNK
    C = CHUNK
    hp = min(HP, nh)
    ng = nh // hp

    def body(*refs):
        c, hg = pl.program_id(0), pl.program_id(1)
        step = c * ng + hg
        ins, outs, scratch, comm_begin, comm_end = _comm_hooks(
            comm, refs, 6, 3, step == 0, step == (nc * ng) // 2, step == nc * ng - 1)
        q_ref, k_ref, v_ref, ab_ref, al_ref, dt_ref = ins
        o_ref, x_ref, st_ref = outs
        s_sc, = scratch
        comm_begin()

        @pl.when(c == 0)
        def _():
            for hh in range(hp):
                s_sc[hg * hp + hh] = jnp.zeros((HD, HD), F32)

        sls = [slice(hh * HD, (hh + 1) * HD) for hh in range(hp)]
        qs, ks, vs = [q_ref[:, sl] for sl in sls], [k_ref[:, sl] for sl in sls], [v_ref[:, sl] for sl in sls]
        sts = [s_sc[hg * hp + hh] for hh in range(hp)]
        gates = [_gdn_gates(ab_ref[...], al_ref[...], dt_ref[...], hg * hp + hh, nh) for hh in range(hp)]
        chs = _gdn_chunks(qs, ks, vs, [g[0] for g in gates], [g[1] for g in gates], C)
        stbs = [_bf(st) for st in sts]
        vns = [ch["uw"][:, :HD] - _nt(_bf(ch["uw"][:, HD:]), stb) for ch, stb in zip(chs, stbs)]
        o_st = [_nt(_bf(q * ch["eg"]), stb) for q, ch, stb in zip(qs, chs, stbs)]
        outs_ = [o + _nn(_bf(ch["qk_raw"] * ch["gamma"]), _bf(vn)) for o, ch, vn in zip(o_st, chs, vns)]
        new_sts = [st * jnp.exp(ch["gl"]) + _tn(_bf(vn), _bf(k * ch["ekt"]))
                   for st, ch, vn, k in zip(sts, chs, vns, ks)]
        for hh in range(hp):
            o_ref[:, sls[hh]] = outs_[hh]
            x_ref[0, hh] = chs[hh]["x"]
            st_ref[0, hh] = sts[hh]
            s_sc[hg * hp + hh] = new_sts[hh]
        comm_end()

    blk = lambda off: pl.BlockSpec((C, hp * HD), lambda c, g: (c, off // hp + g))
    vec = pl.BlockSpec((1, HD), lambda c, g: (0, 0))
    cn = comm.n if comm is not None else 0
    return pl.pallas_call(
        body, name=name, grid=(nc, ng),
        in_specs=[blk(0), blk(nh), blk(2 * nh), pl.BlockSpec((C, HD), lambda c, g: (c, ab_blk)), vec, vec]
        + [ANY] * cn,
        out_specs=[blk(0),
                   pl.BlockSpec((1, hp,C, C), lambda c, g: (c, g, 0, 0)),
                   pl.BlockSpec((1, hp,HD, HD), lambda c, g: (c, g, 0, 0))] + [ANY] * cn,
        out_shape=[jax.ShapeDtypeStruct((t, nh * HD), F32),
                   jax.ShapeDtypeStruct((nc, nh, C, C), F32),
                   jax.ShapeDtypeStruct((nc, nh, HD, HD), F32)] + (comm.out_shapes() if cn else []),
        scratch_shapes=[pltpu.VMEM((nh, HD, HD), F32)] + (comm.scratch() if cn else []),
        compiler_params=_cp(("arbitrary", "arbitrary")),
    )(qkv, qkv, qkv, proj, alog, dtb, *(comm.arrays if cn else []))


def _gdn_bwd(qkv, proj, ab_blk, alog, dtb, do, x_sv, st_sv, nh, name, comm=None):
    t = qkv.shape[0]
    nc = t // CHUNK
    C = CHUNK
    hp = min(HP, nh)
    ng = nh // hp

    def one_head(h, hh, dst, q_ref, k_ref, v_ref, ab_ref, al_ref, dt_ref, do_ref, x_ref, st_ref):
        sl = slice(hh * HD, (hh + 1) * HD)
        q, k, v, do_ = q_ref[:, sl], k_ref[:, sl], v_ref[:, sl], do_ref[:, sl]
        la, beta, dla_da = _gdn_gates(ab_ref[...], al_ref[...], dt_ref[...], h, nh)
        low, strict = _tri(C, "lower"), _tri(C, "strict")
        g_b = _nn(low.astype(F32), jnp.broadcast_to(la, (C, HD)), HI)
        yield
        g_c = g_b[:, :C]
        gamma = jnp.where(low, jnp.exp(jnp.minimum(g_c - g_c.T, 0.0)), 0.0)
        eg = jnp.exp(g_b)
        gl = g_b[C - 1:C, :]
        ekt = jnp.exp(gl - g_b)
        egl = jnp.exp(gl)
        p = _nt(k, k, HI)
        yield
        x = x_ref[0, hh]
        r_w = k * (beta * eg)
        rhs = jnp.concatenate([v * beta, r_w], axis=1)
        uw = _nn(x, rhs, HI)
        yield
        u, w = uw[:, :HD], uw[:, HD:]
        qk_raw = _nt(_bf(q), _bf(k))
        yield
        qk = qk_raw * gamma
        st = st_ref[0, hh]
        stb, dstb = _bf(st), _bf(dst)
        vn = u - _nt(_bf(w), stb)
        yield
        qd, kt = q * eg, k * ekt

        dvn = _tn(_bf(qk), _bf(do_)) + _nt(_bf(kt), dstb)
        yield
        dq2 = jnp.where(low, _nt(_bf(do_), _bf(vn)), 0.0)
        yield
        dqd = _nn(_bf(do_), stb)
        yield
        dkt = _nn(_bf(vn), dstb)
        yield
        dw = -_nn(_bf(dvn), stb)
        yield
        dxx = jnp.concatenate([dvn, dw], axis=1)
        dr = _tn(x, dxx, HI)
        yield
        dm = -jnp.where(strict, _nt(dr, uw, HI), 0.0)
        yield
        dr_u, dr_w = dr[:, :HD], dr[:, HD:]
        rsum = lambda z: jnp.sum(z, axis=1, keepdims=True)

        dv = dr_u * beta
        dmg = dm * gamma
        dbeta = rsum(dr_u * v) + rsum(dr_w * k) * eg[:, :1] + rsum(dmg * p)
        yield
        dp = dmg * beta
        dq2g = dq2 * gamma
        dk = (dr_w * (beta * eg) + dkt * ekt + _tn(_bf(dq2g), _bf(q))
              + _nn(_bf(dp + dp.T), _bf(k)))
        yield
        dq = dqd * eg + _nn(_bf(dq2g), _bf(k))
        yield
        e = dp * p + dq2g * qk_raw
        t_kt = rsum(dkt * kt)
        dg = rsum(dqd * qd) + rsum(dr_w * r_w) - t_kt + rsum(e) - rsum(e.T)
        yield
        dgl = jnp.sum(t_kt, axis=0, keepdims=True) + jnp.sum(dst * st, keepdims=True) * egl[:, :1]
        rowc = lax.broadcasted_iota(jnp.int32, (C, 1), 0)
        dg = dg + jnp.where(rowc == C - 1, dgl, 0.0)
        dla = _nn(_tri(C, "upper").astype(F32), jnp.broadcast_to(dg, (C, HD)), HI)[:, :1]
        yield
        da = dla * dla_da
        db = dbeta * beta * (1.0 - beta)
        lane = lax.broadcasted_iota(jnp.int32, (C, HD), 1)
        dab = jnp.where(lane == h, da, 0.0) + jnp.where(lane == nh + h, db, 0.0)
        lane1 = lax.broadcasted_iota(jnp.int32, (1, HD), 1)
        d_alog = jnp.where(lane1 == h, jnp.sum(dla * la, axis=0, keepdims=True), 0.0)
        d_dtb = jnp.where(lane1 == h, jnp.sum(da, axis=0, keepdims=True), 0.0)
        new_dst = dst * egl + _tn(_bf(do_), _bf(qd)) - _tn(_bf(dvn), _bf(w))
        return dab, d_alog, d_dtb, new_dst, dq, dk, dv

    def body(*refs):
        c, hg = pl.program_id(0), pl.program_id(1)
        step = c * ng + hg
        ins, outs, scratch, comm_begin, comm_end = _comm_hooks(
            comm, refs, 9, 5, step == 0, step == (nc * ng) // 2, step == nc * ng - 1)
        dq_ref, dk_ref, dv_ref, dab_ref, dpar_ref = outs
        ds_sc, = scratch
        comm_begin()

        @pl.when(c == 0)
        def _():
            for hh in range(hp):
                ds_sc[hg * hp + hh] = jnp.zeros((HD, HD), F32)

        @pl.when(step == 0)
        def _():
            dpar_ref[...] = jnp.zeros_like(dpar_ref)

        @pl.when(hg == 0)
        def _():
            dab_ref[...] = jnp.zeros_like(dab_ref)

        dsts = [ds_sc[hg * hp + hh] for hh in range(hp)]
        res = _interleave([one_head(hg * hp + hh, hh, dsts[hh], *ins) for hh in range(hp)])
        for hh in range(hp):
            sl = slice(hh * HD, (hh + 1) * HD)
            ds_sc[hg * hp + hh] = res[hh][3]
            dq_ref[:, sl], dk_ref[:, sl], dv_ref[:, sl] = res[hh][4], res[hh][5], res[hh][6]
        dab_ref[...] += sum(r[0] for r in res[1:]) + res[0][0]
        dpar_ref[0:1, :] += sum(r[1] for r in res[1:]) + res[0][1]
        dpar_ref[1:2, :] += sum(r[2] for r in res[1:]) + res[0][2]
        comm_end()

    rblk = lambda off: pl.BlockSpec((C, hp * HD), lambda c, g: (nc - 1 - c, off // hp + g))
    oblk = pl.BlockSpec((C, hp * HD), lambda c, g: (nc - 1 - c, g))
    vec = pl.BlockSpec((1, HD), lambda c, g: (0, 0))
    cn = comm.n if comm is not None else 0
    return pl.pallas_call(
        body, name=name, grid=(nc, ng),
        in_specs=[rblk(0), rblk(nh), rblk(2 * nh),
                  pl.BlockSpec((C, HD), lambda c, g: (nc - 1 - c, ab_blk)), vec, vec, oblk,
                  pl.BlockSpec((1, hp,C, C), lambda c, g: (nc - 1 - c, g, 0, 0)),
                  pl.BlockSpec((1, hp,HD, HD), lambda c, g: (nc - 1 - c, g, 0, 0))] + [ANY] * cn,
        out_specs=[oblk, oblk, oblk,
                   pl.BlockSpec((C, HD), lambda c, g: (nc - 1 - c, 0)),
                   pl.BlockSpec((8, HD), lambda c, g: (0, 0))] + [ANY] * cn,
        out_shape=[jax.ShapeDtypeStruct((t, nh * HD), F32)] * 3
        + [jax.ShapeDtypeStruct((t, HD), F32), jax.ShapeDtypeStruct((8, HD), F32)]
        + (comm.out_shapes() if cn else []),
        scratch_shapes=[pltpu.VMEM((nh, HD, HD), F32)] + (comm.scratch() if cn else []),
        compiler_params=_cp(("arbitrary", "arbitrary")),
    )(qkv, qkv, qkv, proj, alog, dtb, do, x_sv, st_sv, *(comm.arrays if cn else []))


def _ada_fwd(c_all, w, b, name):
    nb, d = c_all.shape
    n = w.shape[1]
    tn = _pick(n, 512)

    def body(c_ref, w_ref, b_ref, o_ref):
        cv = c_ref[...]
        o_ref[...] = _nn(cv * _sigmoid(cv), w_ref[...], HI) + b_ref[...]

    return pl.pallas_call(
        body, name=name, grid=(n // tn,),
        in_specs=[pl.BlockSpec((nb, d), lambda j: (0, 0)), pl.BlockSpec((d, tn), lambda j: (0, j)),
                  pl.BlockSpec((1, tn), lambda j: (0, j))],
        out_specs=pl.BlockSpec((nb, tn), lambda j: (0, j)),
        out_shape=jax.ShapeDtypeStruct((nb, n), F32),
        compiler_params=_cp(("parallel",)),
    )(c_all, w, b)


def _ada_wgrad(c_all, dmod, name):
    nb, d = c_all.shape
    n = dmod.shape[1]
    tn = _pick(n, 512)

    def body(c_ref, g_ref, o_ref):
        cv = c_ref[...]
        o_ref[...] = _tn(cv * _sigmoid(cv), g_ref[...], HI)

    return pl.pallas_call(
        body, name=name, grid=(n // tn,),
        in_specs=[pl.BlockSpec((nb, d), lambda j: (0, 0)), pl.BlockSpec((nb, tn), lambda j: (0, j))],
        out_specs=pl.BlockSpec((d, tn), lambda j: (0, j)),
        out_shape=jax.ShapeDtypeStruct((d, n), F32),
        compiler_params=_cp(("parallel",)),
    )(c_all, dmod)


def _adamw(w, m, v, g, name, parts=False):
    lead = w.ndim == 3
    r, cdim = w.shape[-2:]
    tr = r if r <= 256 else _pick_rows(r, 256)
    bc1 = 1.0 - ADAM_B1 ** ADAM_STEP
    bc2 = 1.0 - ADAM_B2 ** ADAM_STEP

    def body(w_ref, m_ref, v_ref, g_ref, go_ref, d_ref, mo_ref, vo_ref):
        if parts:
            gv = g_ref[0].astype(F32)
            for s in range(1, N_DEV):
                gv = gv + g_ref[s].astype(F32)
        else:
            gv = g_ref[...]
        wv = w_ref[...]
        mn = ADAM_B1 * m_ref[...] + (1.0 - ADAM_B1) * gv
        vn = ADAM_B2 * v_ref[...] + (1.0 - ADAM_B2) * (gv * gv)
        m_hat = mn / bc1
        v_hat = vn / bc2
        go_ref[...] = gv
        d_ref[...] = -ADAM_LR * (m_hat / (jnp.sqrt(v_hat) + ADAM_EPS) + ADAM_WD * wv)
        mo_ref[...] = mn
        vo_ref[...] = vn

    flat = pl.BlockSpec((tr, cdim), lambda i: (i, 0))
    spec = pl.BlockSpec((None, tr, cdim), lambda i: (0, i, 0)) if lead else flat
    gspec = pl.BlockSpec((N_DEV, tr, cdim), lambda i: (0, i, 0)) if parts else flat
    return pl.pallas_call(
        body, name=name, grid=(r // tr,),
        in_specs=[spec, spec, spec, gspec],
        out_specs=[spec] * 4,
        out_shape=[jax.ShapeDtypeStruct(w.shape, F32)] * 4,
        compiler_params=_cp(("parallel",)),
    )(w, m, v, g)


def _pick_rows(r, pref):
    t = pref
    while r % t:
        t -= 8
    assert t > 0
    return t


def _dev_index(x, y, c):
    return 4 * x + 2 * y + c


class _Comm:
    def __init__(self, kind, arrays):
        self.kind, self.arrays, self.n = kind, list(arrays), len(arrays)

    def out_shapes(self):
        if self.kind == "gather":
            return [jax.ShapeDtypeStruct((N_DEV,) + a.shape, a.dtype) for a in self.arrays]
        return [jax.ShapeDtypeStruct(a.shape, a.dtype) for a in self.arrays]

    def scratch(self):
        return [pltpu.SemaphoreType.DMA((self.n, 7)), pltpu.SemaphoreType.DMA((self.n, 7)),
                pltpu.SemaphoreType.DMA((self.n,))]

    def _gather_parts(self, ins, outs, sems):
        send_sems, recv_sems, local_sems = sems
        x, y, c = lax.axis_index("x"), lax.axis_index("y"), lax.axis_index("c")
        me, sibling = (x, y, c), (x, y, 1 - c)
        chips = [(1 - x, y), (x, 1 - y), (1 - x, 1 - y)]

        def copy(a, k, block, to, src=None):
            slot = outs[a].at[_dev_index(*block)]
            return pltpu.make_async_remote_copy(
                src_ref=slot if src is None else src, dst_ref=slot,
                send_sem=send_sems.at[a, k], recv_sem=recv_sems.at[a, k],
                device_id=to, device_id_type=MESH)

        n = self.n
        mine = [pltpu.make_async_copy(ins[a], outs[a].at[_dev_index(*me)], local_sems.at[a]) for a in range(n)]
        first = []
        for a in range(n):
            first.append(copy(a, 0, me, sibling, src=ins[a]))
            first += [copy(a, 1 + j, me, (*chip, c), src=ins[a]) for j, chip in enumerate(chips)]
        landed = [copy(a, 1 + j, (*chip, c), me) for j, chip in enumerate(chips) for a in range(n)]
        passed = [copy(a, 4 + j, (*chip, c), sibling) for j, chip in enumerate(chips) for a in range(n)]
        late = []
        for a in range(n):
            late.append(copy(a, 0, sibling, me))
            late += [copy(a, 4 + j, (*chip, 1 - c), me) for j, chip in enumerate(chips)]
        return mine, first, landed, passed, late

    def _exchange_parts(self, ins, outs, sems):
        send_sems, recv_sems, local_sems = sems
        x, y, c = lax.axis_index("x"), lax.axis_index("y"), lax.axis_index("c")
        my = _dev_index(x, y, c)
        n = self.n
        mine = [pltpu.make_async_copy(ins[a].at[my], outs[a].at[my], local_sems.at[a]) for a in range(n)]
        sends, recvs = [], []
        for k in range(1, N_DEV):
            px = (1 - x) if (k >> 2) & 1 else x
            py = (1 - y) if (k >> 1) & 1 else y
            pc = (1 - c) if k & 1 else c
            peer = _dev_index(px, py, pc)
            for a in range(n):
                sends.append(pltpu.make_async_remote_copy(
                    src_ref=ins[a].at[peer], dst_ref=outs[a].at[my],
                    send_sem=send_sems.at[a, k - 1], recv_sem=recv_sems.at[a, k - 1],
                    device_id=(px, py, pc), device_id_type=MESH))
                recvs.append(pltpu.make_async_remote_copy(
                    src_ref=ins[a].at[my], dst_ref=outs[a].at[peer],
                    send_sem=send_sems.at[a, k - 1], recv_sem=recv_sems.at[a, k - 1],
                    device_id=(x, y, c), device_id_type=MESH))
        return mine, sends, recvs

    def start(self, ins, outs, sems):
        if self.kind == "gather":
            mine, first, _, _, _ = self._gather_parts(ins, outs, sems)
        else:
            mine, first, _ = self._exchange_parts(ins, outs, sems)
        for cp in mine + first:
            cp.start()

    def mid(self, ins, outs, sems):
        if self.kind == "gather":
            _, _, landed, passed, _ = self._gather_parts(ins, outs, sems)
            for got, fwd in zip(landed, passed):
                got.wait_recv()
                fwd.start()

    def finish(self, ins, outs, sems):
        if self.kind == "gather":
            mine, first, _, passed, late = self._gather_parts(ins, outs, sems)
            for cp in late:
                cp.wait_recv()
            for cp in first + passed:
                cp.wait_send()
        else:
            mine, sends, recvs = self._exchange_parts(ins, outs, sems)
            for cp in sends:
                cp.wait_send()
            for cp in recvs:
                cp.wait_recv()
        for cp in mine:
            cp.wait()

    def run(self, name):
        n = self.n

        def body(*refs):
            ins, outs, sems = refs[:n], refs[n:2 * n], refs[2 * n:]
            self.start(ins, outs, sems)
            self.mid(ins, outs, sems)
            self.finish(ins, outs, sems)

        return pl.pallas_call(
            body, name=name, in_specs=[ANY] * n, out_specs=[ANY] * n,
            out_shape=self.out_shapes(), scratch_shapes=self.scratch(),
        )(*self.arrays)


def _all_gather(arrays, name):
    return _Comm("gather", arrays).run(name)


def _comm_hooks(comm, refs, n_in, n_out, first, middle, last):
    cn = comm.n if comm is not None else 0
    ins, cins = refs[:n_in], refs[n_in:n_in + cn]
    outs, couts = refs[n_in + cn:n_in + cn + n_out], refs[n_in + cn + n_out:n_in + 2 * cn + n_out]
    rest = refs[n_in + 2 * cn + n_out:]
    scratch, csems = (rest[:len(rest) - 3], rest[len(rest) - 3:]) if cn else (rest, ())

    def begin():
        if cn:
            pl.when(first)(lambda: comm.start(cins, couts, csems))
            pl.when(middle)(lambda: comm.mid(cins, couts, csems))

    def end():
        if cn:
            pl.when(last)(lambda: comm.finish(cins, couts, csems))

    return ins, outs, scratch, begin, end


def _local_step(x, tgt, mod, n1, n2, n3, n4, w_in_p, lb_logits, hg_norm, conv_w, alog, dtb, gdn_norm,
                late_w, dist=None):
    t, d = x.shape
    nh = d // 2 // HD
    ab_blk = 8 * nh
    sh_m, sc_m, gt_m, sh_f, sc_f, gt_f = [mod[i:i + 1] for i in range(6)]

    h1, r1 = _prenorm(x, n1, sc_m, sh_m, "prenorm_mix")
    proj = _mm(h1, w_in_p, "nn", [F32], "mm_proj")
    if dist is None:
        o_hg, a_sv, hst_sv = _hgrn2_fwd(proj, lb_logits, nh, "hgrn2_fwd")
        qkv = _gdn_prep(proj, conv_w, 4 * nh, nh, "gdn_prep")
        o_gd, x_sv, gst_sv = _gdn_fwd(qkv, proj, ab_blk, alog, dtb, nh, "gdn_fwd")
        w_out, w_ff1, w_ff2 = late_w
        exch = lambda arrays: None
    else:
        o_hg, a_sv, hst_sv, g_ff2 = _hgrn2_fwd(proj, lb_logits, nh, "hgrn2_fwd",
                                               comm=_Comm("gather", late_w[2:]))
        qkv = _gdn_prep(proj, conv_w, 4 * nh, nh, "gdn_prep")
        o_gd, x_sv, gst_sv, g_out, g_ff1 = _gdn_fwd(qkv, proj, ab_blk, alog, dtb, nh, "gdn_fwd",
                                                    comm=_Comm("gather", late_w[:2]))
        w_out, w_ff1, w_ff2 = dist["assemble"](g_out, g_ff1, g_ff2)
        exch = lambda arrays: _Comm("exchange", arrays)
    om_hg = _headnorm_fwd(o_hg, proj, 3 * nh, hg_norm, "headnorm_hg")
    om_gd = _headnorm_fwd(o_gd, proj, 7 * nh, gdn_norm, "headnorm_gdn")
    om = jnp.concatenate([om_hg, om_gd], axis=1)
    y1 = _mm(om, w_out, "nn", [F32], "mm_out")
    x1, r2 = _postnorm_res(x, y1, n2, gt_m, "postnorm_mix")
    h2, r3 = _prenorm(x1, n3, sc_f, sh_f, "prenorm_ffn")

    def relu2(acc, extra, outs):
        outs[0][...] = acc
        rl = jnp.maximum(acc, 0.0)
        outs[1][...] = (rl * rl).astype(BF16)

    u, act = _mm(h2, w_ff1, "nn", [F32, BF16], "mm_ff1", epilogue=relu2)
    y2 = _mm(act, w_ff2, "nn", [F32], "mm_ff2")
    dout, r4, loss = _final_loss(x1, y2, n4, gt_f, tgt, "final_loss")

    dy2, dgt_f, dn4 = _postnorm_bwd(dout, y2, r4, n4, gt_f, "postnorm_ffn_bwd")
    dw_ff2 = _mm(act, dy2, "tn", [F32], "mm_dw_ff2")

    def drelu2(acc, extra, outs):
        outs[0][...] = (acc * (2.0 * jnp.maximum(extra[0][...], 0.0))).astype(BF16)

    recv = {}
    ff2a, ff2b = dist["parts_ff2"](dw_ff2) if dist else (None, None)
    du, *recv["ff2a"] = _listed(_mm(dy2, w_ff2, "nt", [BF16], "mm_da", epilogue=drelu2, extras=(u,),
                                    comm=exch([ff2a])))
    dw_ff1, *recv["ff2b"] = _listed(_mm(h2, du, "tn", [F32], "mm_dw_ff1", comm=exch([ff2b])))
    ff1a, ff1b = dist["parts_ff1"](dw_ff1) if dist else (None, None)
    dh2, *recv["ff1a"] = _listed(_mm(du, w_ff1, "nt", [F32], "mm_dh2", comm=exch([ff1a])))
    dx1, dsh_f, dsc_f, dn3 = _prenorm_bwd(dh2, x1, r3, n3, sc_f, dout, "prenorm_ffn_bwd")

    dy1, dgt_m, dn2 = _postnorm_bwd(dx1, y1, r2, n2, gt_m, "postnorm_mix_bwd")
    dw_out = _mm(om, dy1, "tn", [F32], "mm_dw_out")
    dom = _mm(dy1, w_out, "nt", [F32], "mm_dom")
    do_hg, dg_hg, dhgn = _headnorm_bwd(dom, 0, o_hg, proj, 3 * nh, hg_norm, "headnorm_hg_bwd")
    do_gd, dg_gd, dgdn = _headnorm_bwd(dom, 1, o_gd, proj, 7 * nh, gdn_norm, "headnorm_gdn_bwd")
    p_out = dist["parts_out"](dw_out) if dist else None
    dq_hg, df_hg, di_hg, dl0, *recv["ff1b_out"] = _hgrn2_bwd(proj, lb_logits, do_hg, a_sv, hst_sv, nh,
                                                             "hgrn2_bwd", comm=exch([ff1b, p_out]))
    dq_g, dk_g, dv_g, dab, dpar = _gdn_bwd(qkv, proj, ab_blk, alog, dtb, do_gd, x_sv, gst_sv, nh, "gdn_bwd")
    dqkv = jnp.concatenate([dq_g, dk_g, dv_g], axis=1)
    du_conv, dconv = _gdn_prep_bwd(proj, conv_w, dqkv, 4 * nh, nh, "gdn_prep_bwd")
    dproj = jnp.concatenate([dq_hg, df_hg, di_hg, dg_hg, du_conv, dg_gd, dab.astype(BF16)], axis=1)
    dw_in = _mm(h1, dproj, "tn", [F32], "mm_dw_in")
    p_in = dist["parts_in"](dw_in) if dist else None
    dh1, *recv["in"] = _listed(_mm(dproj, w_in_p, "nt", [F32], "mm_dh1", tk=640, comm=exch([p_in])))
    dx, dsh_m, dsc_m, dn1 = _prenorm_bwd(dh1, x, r1, n1, sc_m, dx1, "prenorm_mix_bwd")

    dmod = jnp.concatenate([dsh_m, dsc_m, dgt_m, dsh_f, dsc_f, dgt_f], axis=0)
    grads = dict(dmod=dmod, n1=dn1, n2=dn2, n3=dn3, n4=dn4, w_in=dw_in, lb0=dl0, hg_norm=dhgn, conv=dconv,
                 alog=dpar[0:1], dtb=dpar[1:2], gdn_norm=dgdn, w_out=dw_out, w_ff1=dw_ff1, w_ff2=dw_ff2,
                 recv=recv)
    return loss, dx, grads


def _pack(vals):
    rows = []
    for vv in vals:
        flat = vv.reshape(-1)
        flat = jnp.pad(flat, (0, (-flat.shape[0]) % (SUBLANES * LANES)))
        rows.append(flat.reshape(-1, LANES))
    return jnp.concatenate(rows, axis=0)


def _unpack(packed, shapes):
    out, r = [], 0
    for shp in shapes:
        size = 1
        for s in shp:
            size *= s
        nr = -(-size // (SUBLANES * LANES)) * SUBLANES
        out.append(packed[r:r + nr].reshape(-1)[:size].reshape(shp))
        r += nr
    return out


def _sum_parts(parts, name):
    _, r, cdim = parts.shape

    def body(p_ref, o_ref):
        acc = p_ref[0]
        for s in range(1, N_DEV):
            acc = acc + p_ref[s]
        o_ref[...] = acc

    return pl.pallas_call(
        body, name=name,
        out_shape=jax.ShapeDtypeStruct((r, cdim), F32),
        compiler_params=_cp(),
    )(parts)


def kernel(x, c, w_ada, b_ada, pre_mix_norm, post_mix_norm, pre_ffn_norm, post_ffn_norm, w_in, hg_lb_logits, hg_norm, gdn_conv_w, gdn_a_log, gdn_dt_bias, gdn_norm, w_out, w_ff1, w_ff2, loss_target, m_w_ada, m_b_ada, m_pre_mix_norm, m_post_mix_norm, m_pre_ffn_norm, m_post_ffn_norm, m_w_in, m_hg_lb_logits, m_hg_norm, m_gdn_conv_w, m_gdn_a_log, m_gdn_dt_bias, m_gdn_norm, m_w_out, m_w_ff1, m_w_ff2, v_w_ada, v_b_ada, v_pre_mix_norm, v_post_mix_norm, v_pre_ffn_norm, v_post_ffn_norm, v_w_in, v_hg_lb_logits, v_hg_norm, v_gdn_conv_w, v_gdn_a_log, v_gdn_dt_bias, v_gdn_norm, v_w_out, v_w_ff1, v_w_ff2):
    t, d = x.shape[1], x.shape[2]
    nh = d // 2 // HD
    in_cols = w_in.shape[2] * N_DEV
    main = in_cols - 2 * nh
    me = _dev_index(lax.axis_index("x"), lax.axis_index("y"), lax.axis_index("c"))

    c_all, conv_g = _all_gather([c, gdn_conv_w[0]], "gather_small")
    c_all = c_all.reshape(N_DEV, d)
    conv_full = conv_g.transpose(1, 0, 2).reshape(CONV_K, -1)
    w_in_g = _all_gather([w_in[0].astype(BF16)], "gather_w_in")[0]
    w_in_full = w_in_g.transpose(1, 0, 2).reshape(d, in_cols)
    w_in_p = jnp.concatenate([w_in_full, jnp.zeros((d, LANES - 2 * nh), BF16)], axis=1)
    late_w = [w_out[0].astype(BF16), w_ff1[0].astype(BF16), w_ff2[0].astype(BF16)]

    n_in = w_in.shape[2]
    n_ff = w_ff1.shape[2]

    def halves(p):
        r = p.shape[1] // 2
        return p[:, :r], p[:, r:]

    dist = dict(
        assemble=lambda g_out, g_ff1, g_ff2: (g_out.reshape(d, d), g_ff1.transpose(1, 0, 2).reshape(d, -1),
                                              g_ff2.reshape(-1, d)),
        parts_ff2=lambda dw: halves(dw.reshape(N_DEV, -1, d).astype(BF16)),
        parts_ff1=lambda dw: halves(dw.reshape(d, N_DEV, n_ff).transpose(1, 0, 2).astype(BF16)),
        parts_out=lambda dw: dw.reshape(N_DEV, d // N_DEV, d).astype(BF16),
        parts_in=lambda dw: dw[:, :in_cols].reshape(d, N_DEV, n_in).transpose(1, 0, 2).astype(BF16),
    )

    n_ada = w_ada.shape[2]
    b_loc = lax.dynamic_slice(b_ada, (0, me * n_ada), (1, n_ada))
    mod_part = _ada_fwd(c_all, w_ada[0], b_loc, "ada_fwd")
    mod_all = _all_gather([mod_part], "gather_mod")[0]
    mod = lax.dynamic_slice(mod_all, (0, me, 0), (N_DEV, 1, n_ada)).reshape(6, d)

    pad_lane = lambda vv: jnp.concatenate([vv, jnp.zeros((1, LANES - vv.shape[1]), F32)], axis=1)
    loss, dx, g = _local_step(
        x[0], loss_target[0], mod, pre_mix_norm, post_mix_norm, pre_ffn_norm, post_ffn_norm, w_in_p,
        hg_lb_logits, hg_norm, conv_full, pad_lane(gdn_a_log), pad_lane(gdn_dt_bias), gdn_norm,
        late_w, dist)

    rep_names = ["b_ada", "n1", "n2", "n3", "n4", "lb", "hg_norm", "alog", "dtb", "gdn_norm"]
    rep_w = [b_ada, pre_mix_norm, post_mix_norm, pre_ffn_norm, post_ffn_norm, hg_lb_logits, hg_norm,
             gdn_a_log, gdn_dt_bias, gdn_norm]
    rep_m = [m_b_ada, m_pre_mix_norm, m_post_mix_norm, m_pre_ffn_norm, m_post_ffn_norm, m_hg_lb_logits,
             m_hg_norm, m_gdn_a_log, m_gdn_dt_bias, m_gdn_norm]
    rep_v = [v_b_ada, v_pre_mix_norm, v_post_mix_norm, v_pre_ffn_norm, v_post_ffn_norm, v_hg_lb_logits,
             v_hg_norm, v_gdn_a_log, v_gdn_dt_bias, v_gdn_norm]
    rep_shapes = [a.shape for a in rep_w]
    g_lb = jnp.stack([g["lb0"], -g["lb0"]], axis=0)
    rep_g = [g["dmod"], g["n1"], g["n2"], g["n3"], g["n4"], g_lb, g["hg_norm"],
             g["alog"][:, :nh], g["dtb"][:, :nh], g["gdn_norm"]]
    small = _pack(rep_g + [g["conv"]])
    n_rep_rows = _pack(rep_g).shape[0]
    pad_rows = (-small.shape[0]) % 8
    if pad_rows:
        small = jnp.concatenate([small, jnp.zeros((pad_rows, LANES), F32)], axis=0)
    small_all = _all_gather([small], "gather_small_grads")[0]
    small_sum = _sum_parts(small_all, "sum_small_grads")
    rep_out = _adamw(_pack(rep_w), _pack(rep_m), _pack(rep_v), small_sum[:n_rep_rows], "adamw_small")
    rep_g_o, rep_d_o, rep_m_o, rep_v_o = [dict(zip(rep_names, _unpack(p, rep_shapes))) for p in rep_out]

    conv_sum = small_sum[n_rep_rows:n_rep_rows + CONV_K * conv_full.shape[1] // LANES].reshape(CONV_K, -1)
    n_conv = gdn_conv_w.shape[2]
    conv_loc = lax.dynamic_slice(conv_sum, (0, me * n_conv), (CONV_K, n_conv))
    conv_o = _adamw(gdn_conv_w, m_gdn_conv_w, v_gdn_conv_w, conv_loc, "adamw_conv")

    dmod_all = small_all[:, :6 * d // LANES, :].reshape(N_DEV, 6 * d)
    dmod_loc = lax.dynamic_slice(dmod_all, (0, me * n_ada), (N_DEV, n_ada))
    g_ada = _ada_wgrad(c_all, dmod_loc, "ada_wgrad")
    ada_o = _adamw(w_ada, m_w_ada, v_w_ada, g_ada, "adamw_ada")

    rc = g["recv"]
    r_ff2 = jnp.concatenate([rc["ff2a"][0], rc["ff2b"][0]], axis=1)
    r_ff1 = jnp.concatenate([rc["ff1a"][0], rc["ff1b_out"][0]], axis=1)
    r_out, r_in = rc["ff1b_out"][1], rc["in"][0]
    in_o = _adamw(w_in, m_w_in, v_w_in, r_in, "adamw_w_in", parts=True)
    out_o = _adamw(w_out, m_w_out, v_w_out, r_out, "adamw_w_out", parts=True)
    ff1_o = _adamw(w_ff1, m_w_ff1, v_w_ff1, r_ff1, "adamw_w_ff1", parts=True)
    ff2_o = _adamw(w_ff2, m_w_ff2, v_w_ff2, r_ff2, "adamw_w_ff2", parts=True)

    loss_tot = lax.psum(loss[0, 0], ("x", "y", "c"))

    def leaf(kind):
        return [ada_o[kind], rep_out_d[kind]["b_ada"], rep_out_d[kind]["n1"], rep_out_d[kind]["n2"],
                rep_out_d[kind]["n3"], rep_out_d[kind]["n4"], in_o[kind], rep_out_d[kind]["lb"],
                rep_out_d[kind]["hg_norm"], conv_o[kind], rep_out_d[kind]["alog"], rep_out_d[kind]["dtb"],
                rep_out_d[kind]["gdn_norm"], out_o[kind], ff1_o[kind], ff2_o[kind]]

    rep_out_d = [rep_g_o, rep_d_o, rep_m_o, rep_v_o]
    return (loss_tot, dx[None], *leaf(0), *leaf(1), *leaf(2), *leaf(3))
```

```python
import functools

import jax
import jax.numpy as jnp
from jax import lax
from jax.experimental import pallas as pl
from jax.experimental.pallas import tpu as pltpu

F32 = jnp.float32
BF16 = jnp.bfloat16
HI = lax.Precision.HIGHEST
HIGH = lax.Precision.HIGH

EPS = 1e-6
CHUNK = 64
SB = 16
NSB = CHUNK // SB
HP = 8
HD = 128
CONV_K = 4
N_DEV = 8
LANES = 128
SUBLANES = 8
VMEM_LIMIT = 56 * 1024 * 1024

ADAM_BLOCK_ELEMS = 256 * 1024
ADAM_LR = 0.001
ADAM_B1 = 0.9
ADAM_B2 = 0.999
ADAM_EPS = 1e-08
ADAM_WD = 0.01
ADAM_STEP = 10

ANY = pl.BlockSpec(memory_space=pl.ANY)
MESH = pl.DeviceIdType.MESH


def _cp(sem=None):
    return pltpu.CompilerParams(dimension_semantics=sem, vmem_limit_bytes=VMEM_LIMIT)


def _dot(a, b, dims, precision=None):
    return lax.dot_general(a, b, (dims, ((), ())), precision=precision, preferred_element_type=F32)


def _nn(a, b, precision=None):
    return _dot(a, b, ((1,), (0,)), precision)


def _nt(a, b, precision=None):
    return _dot(a, b, ((1,), (1,)), precision)


def _tn(a, b, precision=None):
    return _dot(a, b, ((0,), (0,)), precision)


def _bf(x):
    return x.astype(BF16)


def _sigmoid(x):
    return 1.0 / (1.0 + jnp.exp(-x))


def _interleave(gens):
    results = [None] * len(gens)
    live = list(range(len(gens)))
    while live:
        for i in list(live):
            try:
                next(gens[i])
            except StopIteration as stop:
                results[i] = stop.value
                live.remove(i)
    return results


def _listed(res):
    return list(res) if isinstance(res, (list, tuple)) else [res]


def _pick(n, pref):
    if n <= pref:
        return n
    t = pref
    while n % t:
        t -= LANES
    assert t > 0, (n, pref)
    return t


def _mm(a, b, mode, out_dtypes, name, epilogue=None, extras=(), tm=1024, tn=2048, tk=512, comm=None,
        by_cols=False):
    if mode == "nn":
        (m, kd), (_, n) = a.shape, b.shape
    elif mode == "nt":
        (m, kd), (n, _) = a.shape, b.shape
    else:
        (kd, m), (_, n) = a.shape, b.shape
    tm, tn, tk = _pick(m, tm), _pick(n, tn), _pick(kd, tk)
    nk = kd // tk
    if mode == "nn":
        a_spec = pl.BlockSpec((tm, tk), lambda i, j, k: (i, k))
        b_spec = pl.BlockSpec((tk, tn), lambda i, j, k: (k, j))
        dims = ((1,), (0,))
    elif mode == "nt":
        a_spec = pl.BlockSpec((tm, tk), lambda i, j, k: (i, k))
        b_spec = pl.BlockSpec((tn, tk), lambda i, j, k: (j, k))
        dims = ((1,), (1,))
    else:
        a_spec = pl.BlockSpec((tk, tm), lambda i, j, k: (k, i))
        b_spec = pl.BlockSpec((tk, tn), lambda i, j, k: (k, j))
        dims = ((0,), (0,))
    o_spec = pl.BlockSpec((tm, tn), lambda i, j, k: (i, j))
    if by_cols:
        assert epilogue is None and not extras
        res_spec = pl.BlockSpec((None, tm, tn), lambda i, j, k: (j, i, 0))
        res_shape = (n // tn, m, tn)
    else:
        res_spec, res_shape = o_spec, (m, n)
    n_extra, n_out = len(extras), len(out_dtypes)

    gm, gn = m // tm, n // tn
    cn = comm.n if comm is not None else 0

    def body(*refs):
        i, j, k = pl.program_id(0), pl.program_id(1), pl.program_id(2)
        at0 = (j == 0) & (k == 0)
        ins, out_refs, scratch, comm_begin, comm_end = _comm_hooks(
            comm, refs, 2 + n_extra, n_out, (i == 0) & at0, (i == gm // 2) & at0,
            (i == gm - 1) & (j == gn - 1) & (k == nk - 1))
        a_ref, b_ref, extra_refs = ins[0], ins[1], ins[2:]
        acc, = scratch
        comm_begin()

        @pl.when(k == 0)
        def _():
            acc[...] = jnp.zeros_like(acc)

        acc[...] += _dot(a_ref[...], b_ref[...], dims)

        @pl.when(k == nk - 1)
        def _():
            if epilogue is None:
                out_refs[0][...] = acc[...].astype(out_dtypes[0])
            else:
                epilogue(acc[...], extra_refs, out_refs)

        comm_end()

    sem = ("arbitrary",) * 3 if cn else ("parallel", "parallel", "arbitrary")
    outs = pl.pallas_call(
        body, name=name,
        grid=(gm, gn, nk),
        in_specs=[a_spec, b_spec] + [o_spec] * n_extra + [ANY] * cn,
        out_specs=[res_spec] * n_out + [ANY] * cn,
        out_shape=[jax.ShapeDtypeStruct(res_shape, dt) for dt in out_dtypes] + (comm.out_shapes() if cn else []),
        scratch_shapes=[pltpu.VMEM((tm, tn), F32)] + (comm.scratch() if cn else []),
        compiler_params=_cp(sem),
    )(a, b, *extras, *(comm.arrays if cn else []))
    return outs[0] if n_out + cn == 1 else outs


def _row_spec(tb, d):
    return pl.BlockSpec((tb, d), lambda i: (i, 0))


def _vec_spec(d):
    return pl.BlockSpec((1, d), lambda i: (0, 0))


def _prenorm(x, w, sc, sh, name):
    t, d = x.shape
    tb = _pick(t, 256)

    def body(x_ref, w_ref, sc_ref, sh_ref, h_ref, r_ref):
        xv = x_ref[...]
        r = lax.rsqrt(jnp.mean(xv * xv, axis=-1, keepdims=True) + EPS)
        h_ref[...] = ((xv * r * w_ref[...]) * (1.0 + sc_ref[...]) + sh_ref[...]).astype(BF16)
        r_ref[...] = r

    return pl.pallas_call(
        body, name=name, grid=(t // tb,),
        in_specs=[_row_spec(tb, d), _vec_spec(d), _vec_spec(d), _vec_spec(d)],
        out_specs=[_row_spec(tb, d), _row_spec(tb, 1)],
        out_shape=[jax.ShapeDtypeStruct((t, d), BF16), jax.ShapeDtypeStruct((t, 1), F32)],
        compiler_params=_cp(("parallel",)),
    )(x, w, sc, sh)


def _postnorm_res(x, y, w, gt, name):
    t, d = x.shape
    tb = _pick(t, 256)

    def body(x_ref, y_ref, w_ref, gt_ref, o_ref, r_ref):
        yv = y_ref[...]
        r = lax.rsqrt(jnp.mean(yv * yv, axis=-1, keepdims=True) + EPS)
        o_ref[...] = x_ref[...] + gt_ref[...] * (yv * r * w_ref[...])
        r_ref[...] = r

    return pl.pallas_call(
        body, name=name, grid=(t // tb,),
        in_specs=[_row_spec(tb, d), _row_spec(tb, d), _vec_spec(d), _vec_spec(d)],
        out_specs=[_row_spec(tb, d), _row_spec(tb, 1)],
        out_shape=[jax.ShapeDtypeStruct((t, d), F32), jax.ShapeDtypeStruct((t, 1), F32)],
        compiler_params=_cp(("parallel",)),
    )(x, y, w, gt)


def _final_loss(x, y, w, gt, tgt, name):
    t, d = x.shape
    tb = _pick(t, 256)

    def body(x_ref, y_ref, w_ref, gt_ref, tgt_ref, dout_ref, r_ref, loss_ref):
        @pl.when(pl.program_id(0) == 0)
        def _():
            loss_ref[...] = jnp.zeros_like(loss_ref)

        yv = y_ref[...]
        r = lax.rsqrt(jnp.mean(yv * yv, axis=-1, keepdims=True) + EPS)
        out = x_ref[...] + gt_ref[...] * (yv * r * w_ref[...])
        diff = out - tgt_ref[...]
        row = jnp.mean(diff * diff, axis=-1, keepdims=True)
        loss_ref[...] += 0.5 * jnp.sum(row, axis=0, keepdims=True)
        dout_ref[...] = diff * (1.0 / d)
        r_ref[...] = r

    return pl.pallas_call(
        body, name=name, grid=(t // tb,),
        in_specs=[_row_spec(tb, d), _row_spec(tb, d), _vec_spec(d), _vec_spec(d), _row_spec(tb, d)],
        out_specs=[_row_spec(tb, d), _row_spec(tb, 1), pl.BlockSpec((1, 1), lambda i: (0, 0))],
        out_shape=[jax.ShapeDtypeStruct((t, d), F32), jax.ShapeDtypeStruct((t, 1), F32),
                   jax.ShapeDtypeStruct((1, 1), F32)],
        compiler_params=_cp(("arbitrary",)),
    )(x, y, w, gt, tgt)


def _postnorm_bwd(dxn, y, r, w, gt, name):
    t, d = y.shape
    tb = _pick(t, 256)

    def body(dx_ref, y_ref, r_ref, w_ref, gt_ref, dy_ref, dgt_ref, dw_ref):
        @pl.when(pl.program_id(0) == 0)
        def _():
            dgt_ref[...] = jnp.zeros_like(dgt_ref)
            dw_ref[...] = jnp.zeros_like(dw_ref)

        dxv, rv, wv = dx_ref[...], r_ref[...], w_ref[...]
        z = y_ref[...] * rv
        dgt_ref[...] += jnp.sum(dxv * (z * wv), axis=0, keepdims=True)
        dn = dxv * gt_ref[...]
        dw_ref[...] += jnp.sum(dn * z, axis=0, keepdims=True)
        dz = dn * wv
        dy_ref[...] = (rv * (dz - z * jnp.mean(dz * z, axis=-1, keepdims=True))).astype(BF16)

    return pl.pallas_call(
        body, name=name, grid=(t // tb,),
        in_specs=[_row_spec(tb, d), _row_spec(tb, d), _row_spec(tb, 1), _vec_spec(d), _vec_spec(d)],
        out_specs=[_row_spec(tb, d), _vec_spec(d), _vec_spec(d)],
        out_shape=[jax.ShapeDtypeStruct((t, d), BF16), jax.ShapeDtypeStruct((1, d), F32),
                   jax.ShapeDtypeStruct((1, d), F32)],
        compiler_params=_cp(("arbitrary",)),
    )(dxn, y, r, w, gt)


def _prenorm_bwd(dh, x, r, w, sc, dres, name):
    t, d = x.shape
    tb = _pick(t, 256)

    def body(dh_ref, x_ref, r_ref, w_ref, sc_ref, dres_ref, dx_ref, dsh_ref, dsc_ref, dw_ref):
        @pl.when(pl.program_id(0) == 0)
        def _():
            dsh_ref[...] = jnp.zeros_like(dsh_ref)
            dsc_ref[...] = jnp.zeros_like(dsc_ref)
            dw_ref[...] = jnp.zeros_like(dw_ref)

        dhv, rv, wv = dh_ref[...], r_ref[...], w_ref[...]
        z = x_ref[...] * rv
        dsh_ref[...] += jnp.sum(dhv, axis=0, keepdims=True)
        dsc_ref[...] += jnp.sum(dhv * (z * wv), axis=0, keepdims=True)
        dzw = dhv * (1.0 + sc_ref[...])
        dw_ref[...] += jnp.sum(dzw * z, axis=0, keepdims=True)
        dz = dzw * wv
        dx_ref[...] = dres_ref[...] + rv * (dz - z * jnp.mean(dz * z, axis=-1, keepdims=True))

    return pl.pallas_call(
        body, name=name, grid=(t // tb,),
        in_specs=[_row_spec(tb, d), _row_spec(tb, d), _row_spec(tb, 1), _vec_spec(d), _vec_spec(d),
                  _row_spec(tb, d)],
        out_specs=[_row_spec(tb, d), _vec_spec(d), _vec_spec(d), _vec_spec(d)],
        out_shape=[jax.ShapeDtypeStruct((t, d), F32)] + [jax.ShapeDtypeStruct((1, d), F32)] * 3,
        compiler_params=_cp(("arbitrary",)),
    )(dh, x, r, w, sc, dres)


def _headnorm_fwd(o, proj, g_blk, nw, name):
    t, wd = o.shape
    nh = wd // HD
    tb = _pick(t, 512)
    gb = g_blk * HD // wd

    def body(o_ref, g_ref, nw_ref, out_ref):
        o3 = o_ref[...].reshape(tb, nh, HD)
        g3 = g_ref[...].reshape(tb, nh, HD)
        rh = lax.rsqrt(jnp.mean(o3 * o3, axis=-1, keepdims=True) + EPS)
        res = (o3 * rh * nw_ref[...].reshape(1, 1, HD)) * (g3 * _sigmoid(g3))
        out_ref[...] = res.reshape(tb, wd).astype(BF16)

    return pl.pallas_call(
        body, name=name, grid=(t // tb,),
        in_specs=[_row_spec(tb, wd), pl.BlockSpec((tb, wd), lambda i: (i, gb)), _vec_spec(HD)],
        out_specs=_row_spec(tb, wd),
        out_shape=jax.ShapeDtypeStruct((t, wd), BF16),
        compiler_params=_cp(("parallel",)),
    )(o, proj, nw)


def _headnorm_bwd(dom, col_blk, o, proj, g_blk, nw, name):
    t, wd = o.shape
    nh = wd // HD
    tb = _pick(t, 512)
    gb = g_blk * HD // wd

    def body(do_ref, o_ref, g_ref, nw_ref, dout_ref, dg_ref, dnw_ref):
        @pl.when(pl.program_id(0) == 0)
        def _():
            dnw_ref[...] = jnp.zeros_like(dnw_ref)

        dn = do_ref[...].reshape(tb, nh, HD)
        o3 = o_ref[...].reshape(tb, nh, HD)
        g3 = g_ref[...].reshape(tb, nh, HD)
        nw3 = nw_ref[...].reshape(1, 1, HD)
        rh = lax.rsqrt(jnp.mean(o3 * o3, axis=-1, keepdims=True) + EPS)
        z = o3 * rh
        sg = _sigmoid(g3)
        sl = g3 * sg
        dnw_ref[...] += jnp.sum(jnp.sum(dn * sl * z, axis=1), axis=0, keepdims=True)
        dg_ref[...] = (dn * (z * nw3) * (sg * (1.0 + g3 * (1.0 - sg)))).reshape(tb, wd).astype(BF16)
        dz = dn * sl * nw3
        dout_ref[...] = (rh * (dz - z * jnp.mean(dz * z, axis=-1, keepdims=True))).reshape(tb, wd)

    return pl.pallas_call(
        body, name=name, grid=(t // tb,),
        in_specs=[pl.BlockSpec((tb, wd), lambda i: (i, col_blk)), _row_spec(tb, wd),
                  pl.BlockSpec((tb, wd), lambda i: (i, gb)), _vec_spec(HD)],
        out_specs=[_row_spec(tb, wd), _row_spec(tb, wd), _vec_spec(HD)],
        out_shape=[jax.ShapeDtypeStruct((t, wd), F32), jax.ShapeDtypeStruct((t, wd), BF16),
                   jax.ShapeDtypeStruct((1, HD), F32)],
        compiler_params=_cp(("arbitrary",)),
    )(dom, o, proj, nw)


def _tri(n, kind):
    r = lax.broadcasted_iota(jnp.int32, (n, n), 0)
    c = lax.broadcasted_iota(jnp.int32, (n, n), 1)
    if kind == "lower":
        return r >= c
    if kind == "strict":
        return r > c
    return r <= c


def _hg_gate(fl, l0, l1):
    mx = jnp.maximum(l0, l1)
    e0, e1 = jnp.exp(l0 - mx), jnp.exp(l1 - mx)
    lb = e0 / (e0 + e1)
    sg = _sigmoid(fl)
    f = lb + (1.0 - lb) * sg
    return lb, sg, f


def _hgrn2_fwd(proj, lb_logits, nh, name, comm=None):
    t = proj.shape[0]
    nc = t // CHUNK
    C = CHUNK
    lg = lb_logits.reshape(2, nh, 1, HD)

    hp = min(HP, nh)
    ng = nh // hp

    def one_head(hh, st, q_ref, f_ref, i_ref, lg_ref, p_sc, r_sc):
        sl = slice(hh * HD, (hh + 1) * HD)
        q, v = q_ref[:, sl], i_ref[:, sl]
        _, _, f = _hg_gate(f_ref[:, sl], lg_ref[0, hh], lg_ref[1, hh])
        k = 1.0 - f
        low = _tri(C, "lower")
        b = _nn(low.astype(F32), jnp.log(f), HI)
        yield
        lane_c = lax.broadcasted_iota(jnp.int32, (SB, C), 1)
        lane_h = lax.broadcasted_iota(jnp.int32, (SB, HD), 1)
        row_h = lax.broadcasted_iota(jnp.int32, (SB, HD), 0)
        ones = jnp.ones((HD, HD), F32)

        for i in range(NSB):
            qi, ki, bi = q[SB * i:SB * (i + 1)], k[SB * i:SB * (i + 1)], b[SB * i:SB * (i + 1)]
            for s in range(SB):
                e = jnp.exp(jnp.minimum(bi - bi[s:s + 1], 0.0))
                p = jnp.where(row_h >= s, qi * ki[s:s + 1] * e, 0.0)
                p_sc[hh, pl.ds((i * SB + s) * SB, SB), :] = p
            yield
        r_sc[hh] = _nn(p_sc[hh], ones, HIGH)
        yield
        a_rows = []
        for i in range(NSB):
            acc = jnp.zeros((SB, HD), F32)
            for s in range(SB):
                acc = jnp.where(lane_h == SB * i + s, r_sc[hh, pl.ds((i * SB + s) * SB, SB), :], acc)
            acc = acc[:, :C]
            if i > 0:
                r = b[SB * i - 1:SB * i]
                bi = b[SB * i:SB * (i + 1)]
                qf = q[SB * i:SB * (i + 1)] * jnp.exp(bi - r)
                kf = k * jnp.exp(jnp.minimum(r - b, 0.0))
                acc = acc + jnp.where(lane_c < SB * i, _nt(qf, kf, HIGH), 0.0)
            a_rows.append(acc)
            yield
        a = jnp.concatenate(a_rows, axis=0)
        bl = b[C - 1:C, :]
        o = _nn(_bf(a), _bf(v)) + _nt(_bf(q * jnp.exp(b)), _bf(st))
        yield
        new_st = st * jnp.exp(bl) + _tn(_bf(v), _bf(k * jnp.exp(bl - b)))
        return o, a, new_st

    def body(*refs):
        c, hg = pl.program_id(0), pl.program_id(1)
        step = c * ng + hg
        ins, outs, scratch, comm_begin, comm_end = _comm_hooks(
            comm, refs, 4, 3, step == 0, step == (nc * ng) // 2, step == nc * ng - 1)
        o_ref, a_ref, st_ref = outs
        s_sc, p_sc, r_sc = scratch
        comm_begin()

        @pl.when(c == 0)
        def _():
            for hh in range(hp):
                s_sc[hg * hp + hh] = jnp.zeros((HD, HD), F32)

        sts = [s_sc[hg * hp + hh] for hh in range(hp)]
        res = _interleave([one_head(hh, sts[hh], *ins, p_sc, r_sc) for hh in range(hp)])
        for hh in range(hp):
            o_ref[:, hh * HD:(hh + 1) * HD] = res[hh][0]
            a_ref[0, hh] = res[hh][1]
            st_ref[0, hh] = sts[hh]
            s_sc[hg * hp + hh] = res[hh][2]
        comm_end()

    blk = lambda off: pl.BlockSpec((C, hp * HD), lambda c, g: (c, off // hp + g))
    cn = comm.n if comm is not None else 0
    return pl.pallas_call(
        body, name=name, grid=(nc, ng),
        in_specs=[blk(0), blk(nh), blk(2 * nh),
                  pl.BlockSpec((2, hp, 1, HD), lambda c, g: (0, g, 0, 0))] + [ANY] * cn,
        out_specs=[blk(0),
                   pl.BlockSpec((1, hp, C, C), lambda c, g: (c, g, 0, 0)),
                   pl.BlockSpec((1, hp, HD, HD), lambda c, g: (c, g, 0, 0))] + [ANY] * cn,
        out_shape=[jax.ShapeDtypeStruct((t, nh * HD), F32),
                   jax.ShapeDtypeStruct((nc, nh, C, C), F32),
                   jax.ShapeDtypeStruct((nc, nh, HD, HD), F32)] + (comm.out_shapes() if cn else []),
        scratch_shapes=[pltpu.VMEM((nh, HD, HD), F32), pltpu.VMEM((hp, C * SB, HD), F32),
                        pltpu.VMEM((hp, C * SB, HD), F32)] + (comm.scratch() if cn else []),
        compiler_params=_cp(("arbitrary", "arbitrary")),
    )(proj, proj, proj, lg, *(comm.arrays if cn else []))


def _hgrn2_bwd(proj, lb_logits, do, a_sv, st_sv, nh, name, comm=None):
    t = proj.shape[0]
    nc = t // CHUNK
    C = CHUNK
    lg = lb_logits.reshape(2, nh, 1, HD)
    hp = min(HP, nh)
    ng = nh // hp

    def one_head(hh, dst, q_ref, f_ref, i_ref, lg_ref, do_ref, a_ref, st_ref, p_sc, r_sc):
        sl = slice(hh * HD, (hh + 1) * HD)
        q, v, do_ = q_ref[:, sl], i_ref[:, sl], do_ref[:, sl]
        lb, sg, f = _hg_gate(f_ref[:, sl], lg_ref[0, hh], lg_ref[1, hh])
        k = 1.0 - f
        low = _tri(C, "lower")
        b = _nn(low.astype(F32), jnp.log(f), HI)
        yield
        bl = b[C - 1:C, :]
        eb, ekb = jnp.exp(b), jnp.exp(bl - b)
        qb, kb = q * eb, k * ekb
        a, st = a_ref[0, hh], st_ref[0, hh]

        da = jnp.where(low, _nt(_bf(do_), _bf(v)), 0.0)
        yield
        dv = _tn(_bf(a), _bf(do_)) + _nt(_bf(kb), _bf(dst))
        yield
        dqb = _nn(_bf(do_), _bf(st))
        dkb = _nn(_bf(v), _bf(dst))
        yield

        row = lax.broadcasted_iota(jnp.int32, (C, HD), 0)
        lane_c = lax.broadcasted_iota(jnp.int32, (SB, C), 1)
        row_h = lax.broadcasted_iota(jnp.int32, (SB, HD), 0)
        ones = jnp.ones((HD, HD), F32)
        sel = (lax.broadcasted_iota(jnp.int32, (C, C * SB), 0)
               == jnp.right_shift(lax.broadcasted_iota(jnp.int32, (C, C * SB), 1), SB.bit_length() - 1)).astype(F32)

        for i in range(NSB):
            doi, vi = do_[SB * i:SB * (i + 1)], v[SB * i:SB * (i + 1)]
            for s in range(SB):
                p_sc[hh, pl.ds((i * SB + s) * SB, SB), :] = doi * vi[s:s + 1]
            yield
        r_sc[hh] = _nn(p_sc[hh], ones, HIGH)
        yield
        dq_rows = []
        dk_off = jnp.zeros((C, HD), F32)
        for i in range(NSB):
            qi, ki, bi = q[SB * i:SB * (i + 1)], k[SB * i:SB * (i + 1)], b[SB * i:SB * (i + 1)]
            acc = jnp.zeros((SB, HD), F32)
            for s in range(SB):
                e = jnp.exp(jnp.minimum(bi - bi[s:s + 1], 0.0))
                g = jnp.where(row_h >= s, r_sc[hh, pl.ds((i * SB + s) * SB, SB), :] * e, 0.0)
                acc = acc + g * ki[s:s + 1]
                p_sc[hh, pl.ds((i * SB + s) * SB, SB), :] = g * qi
            yield
            if i > 0:
                r = b[SB * i - 1:SB * i]
                fq = jnp.exp(bi - r)
                fk = jnp.exp(jnp.minimum(r - b, 0.0))
                dai = jnp.where(lane_c < SB * i, da[SB * i:SB * (i + 1)], 0.0)
                acc = acc + _nn(dai, k * fk, HIGH) * fq
                dk_off = dk_off + _tn(dai, qi * fq, HIGH) * fk
                yield
            dq_rows.append(acc)
        dqi = jnp.concatenate(dq_rows, axis=0)
        dq = dqi + dqb * eb
        dk_inter = dkb * ekb
        dk = _nn(sel, p_sc[hh], HIGH) + dk_off + dk_inter
        yield
        db = q * dq - k * dk
        extra = (jnp.sum(k * dk_inter, axis=0, keepdims=True)
                 + jnp.exp(bl) * jnp.sum(dst * st, axis=0, keepdims=True))
        db = db + jnp.where(row == C - 1, extra, 0.0)
        dlf = _nn(_tri(C, "upper").astype(F32), db, HI)
        yield
        df = dlf / f - dk
        dfl = (df * (1.0 - lb) * sg * (1.0 - sg)).astype(BF16)
        dl = jnp.sum(df * (1.0 - sg), axis=0, keepdims=True) * (lb * (1.0 - lb))
        new_dst = dst * jnp.exp(bl) + _tn(_bf(do_), _bf(qb))
        return dq.astype(BF16), dfl, dv.astype(BF16), dl, new_dst

    def body(*refs):
        c, hg = pl.program_id(0), pl.program_id(1)
        step = c * ng + hg
        ins, outs, scratch, comm_begin, comm_end = _comm_hooks(
            comm, refs, 7, 4, step == 0, step == (nc * ng) // 2, step == nc * ng - 1)
        dq_ref, df_ref, di_ref, dl_ref = outs
        ds_sc, p_sc, r_sc = scratch
        comm_begin()

        @pl.when(c == 0)
        def _():
            for hh in range(hp):
                ds_sc[hg * hp + hh] = jnp.zeros((HD, HD), F32)

        @pl.when(step == 0)
        def _():
            dl_ref[...] = jnp.zeros_like(dl_ref)

        dsts = [ds_sc[hg * hp + hh] for hh in range(hp)]
        res = _interleave([one_head(hh, dsts[hh], *ins, p_sc, r_sc) for hh in range(hp)])
        for hh in range(hp):
            sl = slice(hh * HD, (hh + 1) * HD)
            dq_ref[:, sl], df_ref[:, sl], di_ref[:, sl] = res[hh][0], res[hh][1], res[hh][2]
            dl_ref[pl.ds(hg * hp + hh, 1), :] += res[hh][3]
            ds_sc[hg * hp + hh] = res[hh][4]
        comm_end()

    rblk = lambda off: pl.BlockSpec((C, hp * HD), lambda c, g: (nc - 1 - c, off // hp + g))
    oblk = pl.BlockSpec((C, hp * HD), lambda c, g: (nc - 1 - c, g))
    cn = comm.n if comm is not None else 0
    return pl.pallas_call(
        body, name=name, grid=(nc, ng),
        in_specs=[rblk(0), rblk(nh), rblk(2 * nh),
                  pl.BlockSpec((2, hp, 1, HD), lambda c, g: (0, g, 0, 0)),
                  oblk,
                  pl.BlockSpec((1, hp, C, C), lambda c, g: (nc - 1 - c, g, 0, 0)),
                  pl.BlockSpec((1, hp, HD, HD), lambda c, g: (nc - 1 - c, g, 0, 0))] + [ANY] * cn,
        out_specs=[oblk, oblk, oblk, pl.BlockSpec((nh, HD), lambda c, g: (0, 0))] + [ANY] * cn,
        out_shape=[jax.ShapeDtypeStruct((t, nh * HD), BF16)] * 3 + [jax.ShapeDtypeStruct((nh, HD), F32)]
        + (comm.out_shapes() if cn else []),
        scratch_shapes=[pltpu.VMEM((nh, HD, HD), F32), pltpu.VMEM((hp, C * SB, HD), F32),
                        pltpu.VMEM((hp, C * SB, HD), F32)] + (comm.scratch() if cn else []),
        compiler_params=_cp(("arbitrary", "arbitrary")),
    )(proj, proj, proj, lg, do, a_sv, st_sv, *(comm.arrays if cn else []))


def _shift_rows(u, d, row):
    t = u.shape[0]
    if d == 0:
        return u
    rolled = pltpu.roll(u, d % t, 0)
    if d > 0:
        return jnp.where(row >= d, rolled, 0.0)
    return jnp.where(row < t + d, rolled, 0.0)


def _gdn_prep(proj, conv_w, blk0, nh, name):
    t = proj.shape[0]
    scale = HD ** -0.5

    def body(u_ref, w_ref, o_ref):
        j = pl.program_id(0)
        u, w = u_ref[...], w_ref[...]
        row = lax.broadcasted_iota(jnp.int32, (t, HD), 0)
        y = w[CONV_K - 1:CONV_K, :] * u
        for d in range(1, CONV_K):
            y = y + w[CONV_K - 1 - d:CONV_K - d, :] * _shift_rows(u, d, row)
        a = y * _sigmoid(y)
        n = a * lax.rsqrt(jnp.sum(a * a, axis=-1, keepdims=True) + EPS)
        n = n * jnp.where(j < nh, scale, 1.0)
        o_ref[...] = jnp.where(j < 2 * nh, n, a)

    return pl.pallas_call(
        body, name=name, grid=(3 * nh,),
        in_specs=[pl.BlockSpec((t, HD), lambda j: (0, blk0 + j)), pl.BlockSpec((CONV_K, HD), lambda j: (0, j))],
        out_specs=pl.BlockSpec((t, HD), lambda j: (0, j)),
        out_shape=jax.ShapeDtypeStruct((t, 3 * nh * HD), F32),
        compiler_params=_cp(("parallel",)),
    )(proj, conv_w)


def _gdn_prep_bwd(proj, conv_w, dqkv, blk0, nh, name):
    t = proj.shape[0]
    scale = HD ** -0.5

    def body(u_ref, w_ref, d_ref, du_ref, dw_ref):
        j = pl.program_id(0)
        u, w, dout = u_ref[...], w_ref[...], d_ref[...]
        row = lax.broadcasted_iota(jnp.int32, (t, HD), 0)
        us = [_shift_rows(u, d, row) for d in range(CONV_K)]
        y = w[CONV_K - 1:CONV_K, :] * us[0]
        for d in range(1, CONV_K):
            y = y + w[CONV_K - 1 - d:CONV_K - d, :] * us[d]
        sg = _sigmoid(y)
        a = y * sg
        rs = lax.rsqrt(jnp.sum(a * a, axis=-1, keepdims=True) + EPS)
        n = a * rs
        dn = dout * jnp.where(j < nh, scale, 1.0)
        da_n = rs * (dn - n * jnp.sum(dn * n, axis=-1, keepdims=True))
        da = jnp.where(j < 2 * nh, da_n, dout)
        dy = da * (sg * (1.0 + y * (1.0 - sg)))
        du = w[CONV_K - 1:CONV_K, :] * dy
        for d in range(1, CONV_K):
            du = du + w[CONV_K - 1 - d:CONV_K - d, :] * _shift_rows(dy, -d, row)
        du_ref[...] = du.astype(BF16)
        for d in range(CONV_K):
            dw_ref[CONV_K - 1 - d:CONV_K - d, :] = jnp.sum(dy * us[d], axis=0, keepdims=True)

    return pl.pallas_call(
        body, name=name, grid=(3 * nh,),
        in_specs=[pl.BlockSpec((t, HD), lambda j: (0, blk0 + j)), pl.BlockSpec((CONV_K, HD), lambda j: (0, j)),
                  pl.BlockSpec((t, HD), lambda j: (0, j))],
        out_specs=[pl.BlockSpec((t, HD), lambda j: (0, j)), pl.BlockSpec((CONV_K, HD), lambda j: (0, j))],
        out_shape=[jax.ShapeDtypeStruct((t, 3 * nh * HD), BF16), jax.ShapeDtypeStruct((CONV_K, 3 * nh * HD), F32)],
        compiler_params=_cp(("parallel",)),
    )(proj, conv_w, dqkv)


def _gdn_gates(ab, alog, dtb, h, nh):
    lane = lax.broadcasted_iota(jnp.int32, ab.shape, 1)
    x = ab + dtb
    sp = jnp.maximum(x, 0.0) + jnp.log(1.0 + jnp.exp(-jnp.abs(x)))
    ea = jnp.exp(alog)
    la_all = -ea * sp
    beta_all = _sigmoid(ab)
    pick = lambda val, ln: jnp.sum(jnp.where(lane == ln, val, 0.0), axis=1, keepdims=True)
    la = pick(la_all, h)
    beta = pick(beta_all, nh + h)
    dla_da = pick(-ea * _sigmoid(x), h)
    return la, beta, dla_da


def _gdn_chunks(qs, ks, vs, las, betas, C):
    low, strict = _tri(C, "lower"), _tri(C, "strict")
    eye = (lax.broadcasted_iota(jnp.int32, (C, C), 0) == lax.broadcasted_iota(jnp.int32, (C, C), 1)).astype(F32)
    g_bs = [_nn(low.astype(F32), jnp.broadcast_to(la, (C, HD)), HI) for la in las]
    ps = [_nt(k, k, HI) for k in ks]
    qks = [_nt(_bf(q), _bf(k)) for q, k in zip(qs, ks)]
    chs = []
    for g_b, p, qk_raw, beta in zip(g_bs, ps, qks, betas):
        g_c = g_b[:, :C]
        gamma = jnp.where(low, jnp.exp(jnp.minimum(g_c - g_c.T, 0.0)), 0.0)
        gl = g_b[C - 1:C, :]
        chs.append(dict(gamma=gamma, eg=jnp.exp(g_b), gl=gl, ekt=jnp.exp(gl - g_b), p=p,
                        m=jnp.where(strict, beta * p * gamma, 0.0), qk_raw=qk_raw))
    xs = [eye for _ in chs]
    for s in range(C - 1):
        xs = [x - ch["m"][:, s:s + 1] * x[s:s + 1, :] for x, ch in zip(xs, chs)]
    r_ws = [k * (beta * ch["eg"]) for ch, k, beta in zip(chs, ks, betas)]
    uws = [_nn(x, jnp.concatenate([v * beta, r_w], axis=1), HI) for x, v, beta, r_w in zip(xs, vs, betas, r_ws)]
    for ch, x, r_w, uw in zip(chs, xs, r_ws, uws):
        ch.update(x=x, r_w=r_w, uw=uw)
    return chs


def _gdn_fwd(qkv, proj, ab_blk, alog, dtb, nh, name, comm=None):
    t = qkv.shape[0]
    nc = t // CHUNK
    C = CHUNK
    hp = min(HP, nh)
    ng = nh // hp

    def body(*refs):
        c, hg = pl.program_id(0), pl.program_id(1)
        step = c * ng + hg
        ins, outs, scratch, comm_begin, comm_end = _comm_hooks(
            comm, refs, 6, 3, step == 0, step == (nc * ng) // 2, step == nc * ng - 1)
        q_ref, k_ref, v_ref, ab_ref, al_ref, dt_ref = ins
        o_ref, x_ref, st_ref = outs
        s_sc, = scratch
        comm_begin()

        @pl.when(c == 0)
        def _():
            for hh in range(hp):
                s_sc[hg * hp + hh] = jnp.zeros((HD, HD), F32)

        sls = [slice(hh * HD, (hh + 1) * HD) for hh in range(hp)]
        qs, ks, vs = [q_ref[:, sl] for sl in sls], [k_ref[:, sl] for sl in sls], [v_ref[:, sl] for sl in sls]
        sts = [s_sc[hg * hp + hh] for hh in range(hp)]
        gates = [_gdn_gates(ab_ref[...], al_ref[...], dt_ref[...], hg * hp + hh, nh) for hh in range(hp)]
        chs = _gdn_chunks(qs, ks, vs, [g[0] for g in gates], [g[1] for g in gates], C)
        stbs = [_bf(st) for st in sts]
        vns = [ch["uw"][:, :HD] - _nt(_bf(ch["uw"][:, HD:]), stb) for ch, stb in zip(chs, stbs)]
        o_st = [_nt(_bf(q * ch["eg"]), stb) for q, ch, stb in zip(qs, chs, stbs)]
        outs_ = [o + _nn(_bf(ch["qk_raw"] * ch["gamma"]), _bf(vn)) for o, ch, vn in zip(o_st, chs, vns)]
        new_sts = [st * jnp.exp(ch["gl"]) + _tn(_bf(vn), _bf(k * ch["ekt"]))
                   for st, ch, vn, k in zip(sts, chs, vns, ks)]
        for hh in range(hp):
            o_ref[:, sls[hh]] = outs_[hh]
            x_ref[0, hh] = chs[hh]["x"]
            st_ref[0, hh] = sts[hh]
            s_sc[hg * hp + hh] = new_sts[hh]
        comm_end()

    blk = lambda off: pl.BlockSpec((C, hp * HD), lambda c, g: (c, off // hp + g))
    vec = pl.BlockSpec((1, HD), lambda c, g: (0, 0))
    cn = comm.n if comm is not None else 0
    return pl.pallas_call(
        body, name=name, grid=(nc, ng),
        in_specs=[blk(0), blk(nh), blk(2 * nh), pl.BlockSpec((C, HD), lambda c, g: (c, ab_blk)), vec, vec]
        + [ANY] * cn,
        out_specs=[blk(0),
                   pl.BlockSpec((1, hp,C, C), lambda c, g: (c, g, 0, 0)),
                   pl.BlockSpec((1, hp,HD, HD), lambda c, g: (c, g, 0, 0))] + [ANY] * cn,
        out_shape=[jax.ShapeDtypeStruct((t, nh * HD), F32),
                   jax.ShapeDtypeStruct((nc, nh, C, C), F32),
                   jax.ShapeDtypeStruct((nc, nh, HD, HD), F32)] + (comm.out_shapes() if cn else []),
        scratch_shapes=[pltpu.VMEM((nh, HD, HD), F32)] + (comm.scratch() if cn else []),
        compiler_params=_cp(("arbitrary", "arbitrary")),
    )(qkv, qkv, qkv, proj, alog, dtb, *(comm.arrays if cn else []))


def _gdn_bwd(qkv, proj, ab_blk, alog, dtb, do, x_sv, st_sv, nh, name, comm=None):
    t = qkv.shape[0]
    nc = t // CHUNK
    C = CHUNK
    hp = min(HP, nh)
    ng = nh // hp

    def one_head(h, hh, dst, q_ref, k_ref, v_ref, ab_ref, al_ref, dt_ref, do_ref, x_ref, st_ref):
        sl = slice(hh * HD, (hh + 1) * HD)
        q, k, v, do_ = q_ref[:, sl], k_ref[:, sl], v_ref[:, sl], do_ref[:, sl]
        la, beta, dla_da = _gdn_gates(ab_ref[...], al_ref[...], dt_ref[...], h, nh)
        low, strict = _tri(C, "lower"), _tri(C, "strict")
        g_b = _nn(low.astype(F32), jnp.broadcast_to(la, (C, HD)), HI)
        yield
        g_c = g_b[:, :C]
        gamma = jnp.where(low, jnp.exp(jnp.minimum(g_c - g_c.T, 0.0)), 0.0)
        eg = jnp.exp(g_b)
        gl = g_b[C - 1:C, :]
        ekt = jnp.exp(gl - g_b)
        egl = jnp.exp(gl)
        p = _nt(k, k, HI)
        yield
        x = x_ref[0, hh]
        r_w = k * (beta * eg)
        rhs = jnp.concatenate([v * beta, r_w], axis=1)
        uw = _nn(x, rhs, HI)
        yield
        u, w = uw[:, :HD], uw[:, HD:]
        qk_raw = _nt(_bf(q), _bf(k))
        yield
        qk = qk_raw * gamma
        st = st_ref[0, hh]
        stb, dstb = _bf(st), _bf(dst)
        vn = u - _nt(_bf(w), stb)
        yield
        qd, kt = q * eg, k * ekt

        dvn = _tn(_bf(qk), _bf(do_)) + _nt(_bf(kt), dstb)
        yield
        dq2 = jnp.where(low, _nt(_bf(do_), _bf(vn)), 0.0)
        yield
        dqd = _nn(_bf(do_), stb)
        yield
        dkt = _nn(_bf(vn), dstb)
        yield
        dw = -_nn(_bf(dvn), stb)
        yield
        dxx = jnp.concatenate([dvn, dw], axis=1)
        dr = _tn(x, dxx, HI)
        yield
        dm = -jnp.where(strict, _nt(dr, uw, HI), 0.0)
        yield
        dr_u, dr_w = dr[:, :HD], dr[:, HD:]
        rsum = lambda z: jnp.sum(z, axis=1, keepdims=True)

        dv = dr_u * beta
        dmg = dm * gamma
        dbeta = rsum(dr_u * v) + rsum(dr_w * k) * eg[:, :1] + rsum(dmg * p)
        yield
        dp = dmg * beta
        dq2g = dq2 * gamma
        dk = (dr_w * (beta * eg) + dkt * ekt + _tn(_bf(dq2g), _bf(q))
              + _nn(_bf(dp + dp.T), _bf(k)))
        yield
        dq = dqd * eg + _nn(_bf(dq2g), _bf(k))
        yield
        e = dp * p + dq2g * qk_raw
        t_kt = rsum(dkt * kt)
        dg = rsum(dqd * qd) + rsum(dr_w * r_w) - t_kt + rsum(e) - rsum(e.T)
        yield
        dgl = jnp.sum(t_kt, axis=0, keepdims=True) + jnp.sum(dst * st, keepdims=True) * egl[:, :1]
        rowc = lax.broadcasted_iota(jnp.int32, (C, 1), 0)
        dg = dg + jnp.where(rowc == C - 1, dgl, 0.0)
        dla = _nn(_tri(C, "upper").astype(F32), jnp.broadcast_to(dg, (C, HD)), HI)[:, :1]
        yield
        da = dla * dla_da
        db = dbeta * beta * (1.0 - beta)
        lane = lax.broadcasted_iota(jnp.int32, (C, HD), 1)
        dab = jnp.where(lane == h, da, 0.0) + jnp.where(lane == nh + h, db, 0.0)
        lane1 = lax.broadcasted_iota(jnp.int32, (1, HD), 1)
        d_alog = jnp.where(lane1 == h, jnp.sum(dla * la, axis=0, keepdims=True), 0.0)
        d_dtb = jnp.where(lane1 == h, jnp.sum(da, axis=0, keepdims=True), 0.0)
        new_dst = dst * egl + _tn(_bf(do_), _bf(qd)) - _tn(_bf(dvn), _bf(w))
        return dab, d_alog, d_dtb, new_dst, dq, dk, dv

    def body(*refs):
        c, hg = pl.program_id(0), pl.program_id(1)
        step = c * ng + hg
        ins, outs, scratch, comm_begin, comm_end = _comm_hooks(
            comm, refs, 9, 5, step == 0, step == (nc * ng) // 2, step == nc * ng - 1)
        dq_ref, dk_ref, dv_ref, dab_ref, dpar_ref = outs
        ds_sc, = scratch
        comm_begin()

        @pl.when(c == 0)
        def _():
            for hh in range(hp):
                ds_sc[hg * hp + hh] = jnp.zeros((HD, HD), F32)

        @pl.when(step == 0)
        def _():
            dpar_ref[...] = jnp.zeros_like(dpar_ref)

        @pl.when(hg == 0)
        def _():
            dab_ref[...] = jnp.zeros_like(dab_ref)

        dsts = [ds_sc[hg * hp + hh] for hh in range(hp)]
        res = _interleave([one_head(hg * hp + hh, hh, dsts[hh], *ins) for hh in range(hp)])
        for hh in range(hp):
            sl = slice(hh * HD, (hh + 1) * HD)
            ds_sc[hg * hp + hh] = res[hh][3]
            dq_ref[:, sl], dk_ref[:, sl], dv_ref[:, sl] = res[hh][4], res[hh][5], res[hh][6]
        dab_ref[...] += sum(r[0] for r in res[1:]) + res[0][0]
        dpar_ref[0:1, :] += sum(r[1] for r in res[1:]) + res[0][1]
        dpar_ref[1:2, :] += sum(r[2] for r in res[1:]) + res[0][2]
        comm_end()

    rblk = lambda off: pl.BlockSpec((C, hp * HD), lambda c, g: (nc - 1 - c, off // hp + g))
    oblk = pl.BlockSpec((C, hp * HD), lambda c, g: (nc - 1 - c, g))
    vec = pl.BlockSpec((1, HD), lambda c, g: (0, 0))
    cn = comm.n if comm is not None else 0
    return pl.pallas_call(
        body, name=name, grid=(nc, ng),
        in_specs=[rblk(0), rblk(nh), rblk(2 * nh),
                  pl.BlockSpec((C, HD), lambda c, g: (nc - 1 - c, ab_blk)), vec, vec, oblk,
                  pl.BlockSpec((1, hp,C, C), lambda c, g: (nc - 1 - c, g, 0, 0)),
                  pl.BlockSpec((1, hp,HD, HD), lambda c, g: (nc - 1 - c, g, 0, 0))] + [ANY] * cn,
        out_specs=[oblk, oblk, oblk,
                   pl.BlockSpec((C, HD), lambda c, g: (nc - 1 - c, 0)),
                   pl.BlockSpec((8, HD), lambda c, g: (0, 0))] + [ANY] * cn,
        out_shape=[jax.ShapeDtypeStruct((t, nh * HD), F32)] * 3
        + [jax.ShapeDtypeStruct((t, HD), F32), jax.ShapeDtypeStruct((8, HD), F32)]
        + (comm.out_shapes() if cn else []),
        scratch_shapes=[pltpu.VMEM((nh, HD, HD), F32)] + (comm.scratch() if cn else []),
        compiler_params=_cp(("arbitrary", "arbitrary")),
    )(qkv, qkv, qkv, proj, alog, dtb, do, x_sv, st_sv, *(comm.arrays if cn else []))


def _ada_fwd(c_all, w, b, name):
    nb, d = c_all.shape
    n = w.shape[1]
    tn = _pick(n, 512)

    def body(c_ref, w_ref, b_ref, o_ref):
        cv = c_ref[...]
        o_ref[...] = _nn(cv * _sigmoid(cv), w_ref[...], HI) + b_ref[...]

    return pl.pallas_call(
        body, name=name, grid=(n // tn,),
        in_specs=[pl.BlockSpec((nb, d), lambda j: (0, 0)), pl.BlockSpec((d, tn), lambda j: (0, j)),
                  pl.BlockSpec((1, tn), lambda j: (0, j))],
        out_specs=pl.BlockSpec((nb, tn), lambda j: (0, j)),
        out_shape=jax.ShapeDtypeStruct((nb, n), F32),
        compiler_params=_cp(("parallel",)),
    )(c_all, w, b)


def _ada_wgrad(c_all, dmod, name):
    nb, d = c_all.shape
    n = dmod.shape[1]
    tn = _pick(n, 512)

    def body(c_ref, g_ref, o_ref):
        cv = c_ref[...]
        o_ref[...] = _tn(cv * _sigmoid(cv), g_ref[...], HI)

    return pl.pallas_call(
        body, name=name, grid=(n // tn,),
        in_specs=[pl.BlockSpec((nb, d), lambda j: (0, 0)), pl.BlockSpec((nb, tn), lambda j: (0, j))],
        out_specs=pl.BlockSpec((d, tn), lambda j: (0, j)),
        out_shape=jax.ShapeDtypeStruct((d, n), F32),
        compiler_params=_cp(("parallel",)),
    )(c_all, dmod)


def _adamw(w, m, v, g, name, parts=False):
    lead = w.ndim == 3
    r, cdim = w.shape[-2:]
    cap = max(SUBLANES, ADAM_BLOCK_ELEMS // cdim // SUBLANES * SUBLANES)
    tr = r if r <= cap else _pick_rows(r, cap)
    bc1 = 1.0 - ADAM_B1 ** ADAM_STEP
    bc2 = 1.0 - ADAM_B2 ** ADAM_STEP

    glist = list(g) if isinstance(g, (list, tuple)) else [g]
    bounds = [0]
    for ga in glist:
        bounds.append(bounds[-1] + ga.shape[-2] // tr)

    def body(w_ref, m_ref, v_ref, *rest):
        g_refs, (go_ref, d_ref, mo_ref, vo_ref) = rest[:len(glist)], rest[len(glist):]
        if parts:
            sums = []
            for g_ref in g_refs:
                gv = g_ref[0].astype(F32)
                for s in range(1, N_DEV):
                    gv = gv + g_ref[s].astype(F32)
                sums.append(gv)
            gv = sums[-1]
            for p in range(len(sums) - 2, -1, -1):
                gv = jnp.where(pl.program_id(0) < bounds[p + 1], sums[p], gv)
        else:
            gv = g_refs[0][...]
        wv = w_ref[...]
        mn = ADAM_B1 * m_ref[...] + (1.0 - ADAM_B1) * gv
        vn = ADAM_B2 * v_ref[...] + (1.0 - ADAM_B2) * (gv * gv)
        m_hat = mn / bc1
        v_hat = vn / bc2
        go_ref[...] = gv
        d_ref[...] = -ADAM_LR * (m_hat / (jnp.sqrt(v_hat) + ADAM_EPS) + ADAM_WD * wv)
        mo_ref[...] = mn
        vo_ref[...] = vn

    flat = pl.BlockSpec((tr, cdim), lambda i: (i, 0))
    spec = pl.BlockSpec((None, tr, cdim), lambda i: (0, i, 0)) if lead else flat
    def piece_spec(p):
        lo, n = bounds[p], bounds[p + 1] - bounds[p]
        return pl.BlockSpec((N_DEV, tr, cdim), lambda i: (0, jnp.clip(i - lo, 0, n - 1), 0))

    gspecs = [piece_spec(p) for p in range(len(glist))] if parts else [flat]
    return pl.pallas_call(
        body, name=name, grid=(r // tr,),
        in_specs=[spec, spec, spec] + gspecs,
        out_specs=[spec] * 4,
        out_shape=[jax.ShapeDtypeStruct(w.shape, F32)] * 4,
        compiler_params=_cp(("arbitrary",)),
    )(w, m, v, *glist)


def _pick_rows(r, pref):
    t = pref
    while r % t:
        t -= 8
    assert t > 0
    return t


def _dev_index(x, y, c):
    return 4 * x + 2 * y + c


class _Comm:
    def __init__(self, kind, arrays):
        self.kind, self.arrays, self.n = kind, list(arrays), len(arrays)

    def out_shapes(self):
        if self.kind == "gather":
            return [jax.ShapeDtypeStruct((N_DEV,) + a.shape, a.dtype) for a in self.arrays]
        return [jax.ShapeDtypeStruct(a.shape, a.dtype) for a in self.arrays]

    def scratch(self):
        return [pltpu.SemaphoreType.DMA((self.n, 7)), pltpu.SemaphoreType.DMA((self.n, 7)),
                pltpu.SemaphoreType.DMA((self.n,))]

    def _gather_parts(self, ins, outs, sems):
        send_sems, recv_sems, local_sems = sems
        x, y, c = lax.axis_index("x"), lax.axis_index("y"), lax.axis_index("c")
        me, sibling = (x, y, c), (x, y, 1 - c)
        chips = [(1 - x, y), (x, 1 - y), (1 - x, 1 - y)]

        def copy(a, k, block, to, src=None):
            slot = outs[a].at[_dev_index(*block)]
            return pltpu.make_async_remote_copy(
                src_ref=slot if src is None else src, dst_ref=slot,
                send_sem=send_sems.at[a, k], recv_sem=recv_sems.at[a, k],
                device_id=to, device_id_type=MESH)

        n = self.n
        mine = [pltpu.make_async_copy(ins[a], outs[a].at[_dev_index(*me)], local_sems.at[a]) for a in range(n)]
        first = []
        for a in range(n):
            first.append(copy(a, 0, me, sibling, src=ins[a]))
            first += [copy(a, 1 + j, me, (*chip, c), src=ins[a]) for j, chip in enumerate(chips)]
        landed = [copy(a, 1 + j, (*chip, c), me) for j, chip in enumerate(chips) for a in range(n)]
        passed = [copy(a, 4 + j, (*chip, c), sibling) for j, chip in enumerate(chips) for a in range(n)]
        late = []
        for a in range(n):
            late.append(copy(a, 0, sibling, me))
            late += [copy(a, 4 + j, (*chip, 1 - c), me) for j, chip in enumerate(chips)]
        return mine, first, landed, passed, late

    def _exchange_parts(self, ins, outs, sems):
        send_sems, recv_sems, local_sems = sems
        x, y, c = lax.axis_index("x"), lax.axis_index("y"), lax.axis_index("c")
        my = _dev_index(x, y, c)
        n = self.n
        mine = [pltpu.make_async_copy(ins[a].at[my], outs[a].at[my], local_sems.at[a]) for a in range(n)]
        sends, recvs = [], []
        for k in range(1, N_DEV):
            px = (1 - x) if (k >> 2) & 1 else x
            py = (1 - y) if (k >> 1) & 1 else y
            pc = (1 - c) if k & 1 else c
            peer = _dev_index(px, py, pc)
            for a in range(n):
                sends.append(pltpu.make_async_remote_copy(
                    src_ref=ins[a].at[peer], dst_ref=outs[a].at[my],
                    send_sem=send_sems.at[a, k - 1], recv_sem=recv_sems.at[a, k - 1],
                    device_id=(px, py, pc), device_id_type=MESH))
                recvs.append(pltpu.make_async_remote_copy(
                    src_ref=ins[a].at[my], dst_ref=outs[a].at[peer],
                    send_sem=send_sems.at[a, k - 1], recv_sem=recv_sems.at[a, k - 1],
                    device_id=(x, y, c), device_id_type=MESH))
        return mine, sends, recvs

    def start(self, ins, outs, sems):
        if self.kind == "gather":
            mine, first, _, _, _ = self._gather_parts(ins, outs, sems)
        else:
            mine, first, _ = self._exchange_parts(ins, outs, sems)
        for cp in mine + first:
            cp.start()

    def mid(self, ins, outs, sems):
        if self.kind == "gather":
            _, _, landed, passed, _ = self._gather_parts(ins, outs, sems)
            for got, fwd in zip(landed, passed):
                got.wait_recv()
                fwd.start()

    def finish(self, ins, outs, sems):
        if self.kind == "gather":
            mine, first, _, passed, late = self._gather_parts(ins, outs, sems)
            for cp in late:
                cp.wait_recv()
            for cp in first + passed:
                cp.wait_send()
        else:
            mine, sends, recvs = self._exchange_parts(ins, outs, sems)
            for cp in sends:
                cp.wait_send()
            for cp in recvs:
                cp.wait_recv()
        for cp in mine:
            cp.wait()

    def run(self, name):
        n = self.n

        def body(*refs):
            ins, outs, sems = refs[:n], refs[n:2 * n], refs[2 * n:]
            self.start(ins, outs, sems)
            self.mid(ins, outs, sems)
            self.finish(ins, outs, sems)

        return pl.pallas_call(
            body, name=name, in_specs=[ANY] * n, out_specs=[ANY] * n,
            out_shape=self.out_shapes(), scratch_shapes=self.scratch(),
        )(*self.arrays)


def _all_gather(arrays, name):
    return _Comm("gather", arrays).run(name)


def _comm_hooks(comm, refs, n_in, n_out, first, middle, last):
    cn = comm.n if comm is not None else 0
    ins, cins = refs[:n_in], refs[n_in:n_in + cn]
    outs, couts = refs[n_in + cn:n_in + cn + n_out], refs[n_in + cn + n_out:n_in + 2 * cn + n_out]
    rest = refs[n_in + 2 * cn + n_out:]
    scratch, csems = (rest[:len(rest) - 3], rest[len(rest) - 3:]) if cn else (rest, ())

    def begin():
        if cn:
            pl.when(first)(lambda: comm.start(cins, couts, csems))
            pl.when(middle)(lambda: comm.mid(cins, couts, csems))

    def end():
        if cn:
            pl.when(last)(lambda: comm.finish(cins, couts, csems))

    return ins, outs, scratch, begin, end


def _local_step(x, tgt, mod, n1, n2, n3, n4, w_in_p, lb_logits, hg_norm, conv_w, alog, dtb, gdn_norm,
                late_w, dist=None):
    t, d = x.shape
    nh = d // 2 // HD
    ab_blk = 8 * nh
    sh_m, sc_m, gt_m, sh_f, sc_f, gt_f = [mod[i:i + 1] for i in range(6)]

    h1, r1 = _prenorm(x, n1, sc_m, sh_m, "prenorm_mix")
    if dist is None:
        proj = _mm(h1, w_in_p, "nn", [F32], "mm_proj")
        o_hg, a_sv, hst_sv = _hgrn2_fwd(proj, lb_logits, nh, "hgrn2_fwd")
        qkv = _gdn_prep(proj, conv_w, 4 * nh, nh, "gdn_prep")
        o_gd, x_sv, gst_sv = _gdn_fwd(qkv, proj, ab_blk, alog, dtb, nh, "gdn_fwd")
        w_out, w_ff1, w_ff2 = late_w
        exch = lambda arrays: None
    else:
        proj, g_ff2 = _mm(h1, w_in_p, "nn", [F32], "mm_proj", comm=_Comm("gather", late_w[2:]))
        o_hg, a_sv, hst_sv, g_out = _hgrn2_fwd(proj, lb_logits, nh, "hgrn2_fwd",
                                               comm=_Comm("gather", late_w[:1]))
        qkv = _gdn_prep(proj, conv_w, 4 * nh, nh, "gdn_prep")
        o_gd, x_sv, gst_sv, g_ff1 = _gdn_fwd(qkv, proj, ab_blk, alog, dtb, nh, "gdn_fwd",
                                             comm=_Comm("gather", late_w[1:2]))
        w_out, w_ff1, w_ff2 = dist["assemble"](g_out, g_ff1, g_ff2)
        exch = lambda arrays: _Comm("exchange", arrays)
    om_hg = _headnorm_fwd(o_hg, proj, 3 * nh, hg_norm, "headnorm_hg")
    om_gd = _headnorm_fwd(o_gd, proj, 7 * nh, gdn_norm, "headnorm_gdn")
    om = jnp.concatenate([om_hg, om_gd], axis=1)
    y1 = _mm(om, w_out, "nn", [F32], "mm_out")
    x1, r2 = _postnorm_res(x, y1, n2, gt_m, "postnorm_mix")
    h2, r3 = _prenorm(x1, n3, sc_f, sh_f, "prenorm_ffn")

    def relu2(acc, extra, outs):
        outs[0][...] = acc
        rl = jnp.maximum(acc, 0.0)
        outs[1][...] = (rl * rl).astype(BF16)

    u, act = _mm(h2, w_ff1, "nn", [F32, BF16], "mm_ff1", epilogue=relu2)
    y2 = _mm(act, w_ff2, "nn", [F32], "mm_ff2")
    dout, r4, loss = _final_loss(x1, y2, n4, gt_f, tgt, "final_loss")

    dy2, dgt_f, dn4 = _postnorm_bwd(dout, y2, r4, n4, gt_f, "postnorm_ffn_bwd")
    dw_ff2 = _mm(act, dy2, "tn", [BF16], "mm_dw_ff2")

    def drelu2(acc, extra, outs):
        outs[0][...] = (acc * (2.0 * jnp.maximum(extra[0][...], 0.0))).astype(BF16)

    recv = {}
    ff2a, ff2b = dist["parts_ff2"](dw_ff2) if dist else (None, None)
    du, *recv["ff2a"] = _listed(_mm(dy2, w_ff2, "nt", [BF16], "mm_da", epilogue=drelu2, extras=(u,),
                                    comm=exch([ff2a])))
    ff1_cols = dict(by_cols=True, tn=dist["n_ff"]) if dist else {}
    dw_ff1, *recv["ff2b"] = _listed(_mm(h2, du, "tn", [BF16], "mm_dw_ff1", comm=exch([ff2b]), **ff1_cols))
    ff1a, ff1b = dist["parts_ff1"](dw_ff1) if dist else (None, None)
    dh2, *recv["ff1a"] = _listed(_mm(du, w_ff1, "nt", [F32], "mm_dh2", comm=exch([ff1a])))
    dx1, dsh_f, dsc_f, dn3 = _prenorm_bwd(dh2, x1, r3, n3, sc_f, dout, "prenorm_ffn_bwd")

    dy1, dgt_m, dn2 = _postnorm_bwd(dx1, y1, r2, n2, gt_m, "postnorm_mix_bwd")
    dw_out = _mm(om, dy1, "tn", [BF16], "mm_dw_out")
    dom = _mm(dy1, w_out, "nt", [F32], "mm_dom")
    do_hg, dg_hg, dhgn = _headnorm_bwd(dom, 0, o_hg, proj, 3 * nh, hg_norm, "headnorm_hg_bwd")
    do_gd, dg_gd, dgdn = _headnorm_bwd(dom, 1, o_gd, proj, 7 * nh, gdn_norm, "headnorm_gdn_bwd")
    p_out = dist["parts_out"](dw_out) if dist else None
    dq_hg, df_hg, di_hg, dl0, *recv["ff1b_out"] = _hgrn2_bwd(proj, lb_logits, do_hg, a_sv, hst_sv, nh,
                                                             "hgrn2_bwd", comm=exch([ff1b, p_out]))
    dq_g, dk_g, dv_g, dab, dpar = _gdn_bwd(qkv, proj, ab_blk, alog, dtb, do_gd, x_sv, gst_sv, nh, "gdn_bwd")
    dqkv = jnp.concatenate([dq_g, dk_g, dv_g], axis=1)
    du_conv, dconv = _gdn_prep_bwd(proj, conv_w, dqkv, 4 * nh, nh, "gdn_prep_bwd")
    dproj = jnp.concatenate([dq_hg, df_hg, di_hg, dg_hg, du_conv, dg_gd, dab.astype(BF16)], axis=1)
    dw_in = _mm(h1, dproj, "tn", [BF16], "mm_dw_in")
    p_in = dist["parts_in"](dw_in) if dist else None
    dh1, *recv["in"] = _listed(_mm(dproj, w_in_p, "nt", [F32], "mm_dh1", tk=640, comm=exch([p_in])))
    dx, dsh_m, dsc_m, dn1 = _prenorm_bwd(dh1, x, r1, n1, sc_m, dx1, "prenorm_mix_bwd")

    dmod = jnp.concatenate([dsh_m, dsc_m, dgt_m, dsh_f, dsc_f, dgt_f], axis=0)
    grads = dict(dmod=dmod, n1=dn1, n2=dn2, n3=dn3, n4=dn4, w_in=dw_in, lb0=dl0, hg_norm=dhgn, conv=dconv,
                 alog=dpar[0:1], dtb=dpar[1:2], gdn_norm=dgdn, w_out=dw_out, w_ff1=dw_ff1, w_ff2=dw_ff2,
                 recv=recv)
    return loss, dx, grads


def _pack(vals):
    rows = []
    for vv in vals:
        flat = vv.reshape(-1)
        flat = jnp.pad(flat, (0, (-flat.shape[0]) % (SUBLANES * LANES)))
        rows.append(flat.reshape(-1, LANES))
    return jnp.concatenate(rows, axis=0)


def _unpack(packed, shapes):
    out, r = [], 0
    for shp in shapes:
        size = 1
        for s in shp:
            size *= s
        nr = -(-size // (SUBLANES * LANES)) * SUBLANES
        out.append(packed[r:r + nr].reshape(-1)[:size].reshape(shp))
        r += nr
    return out


def _sum_parts(parts, name):
    _, r, cdim = parts.shape

    def body(p_ref, o_ref):
        acc = p_ref[0]
        for s in range(1, N_DEV):
            acc = acc + p_ref[s]
        o_ref[...] = acc

    return pl.pallas_call(
        body, name=name,
        out_shape=jax.ShapeDtypeStruct((r, cdim), F32),
        compiler_params=_cp(),
    )(parts)


def kernel(x, c, w_ada, b_ada, pre_mix_norm, post_mix_norm, pre_ffn_norm, post_ffn_norm, w_in, hg_lb_logits, hg_norm, gdn_conv_w, gdn_a_log, gdn_dt_bias, gdn_norm, w_out, w_ff1, w_ff2, loss_target, m_w_ada, m_b_ada, m_pre_mix_norm, m_post_mix_norm, m_pre_ffn_norm, m_post_ffn_norm, m_w_in, m_hg_lb_logits, m_hg_norm, m_gdn_conv_w, m_gdn_a_log, m_gdn_dt_bias, m_gdn_norm, m_w_out, m_w_ff1, m_w_ff2, v_w_ada, v_b_ada, v_pre_mix_norm, v_post_mix_norm, v_pre_ffn_norm, v_post_ffn_norm, v_w_in, v_hg_lb_logits, v_hg_norm, v_gdn_conv_w, v_gdn_a_log, v_gdn_dt_bias, v_gdn_norm, v_w_out, v_w_ff1, v_w_ff2):
    t, d = x.shape[1], x.shape[2]
    nh = d // 2 // HD
    in_cols = w_in.shape[2] * N_DEV
    main = in_cols - 2 * nh
    me = _dev_index(lax.axis_index("x"), lax.axis_index("y"), lax.axis_index("c"))

    c_all, conv_g = _all_gather([c, gdn_conv_w[0]], "gather_small")
    c_all = c_all.reshape(N_DEV, d)
    conv_full = conv_g.transpose(1, 0, 2).reshape(CONV_K, -1)
    w_in_g = _all_gather([w_in[0].astype(BF16)], "gather_w_in")[0]
    w_in_full = w_in_g.transpose(1, 0, 2).reshape(d, in_cols)
    w_in_p = jnp.concatenate([w_in_full, jnp.zeros((d, LANES - 2 * nh), BF16)], axis=1)
    late_w = [w_out[0].astype(BF16), w_ff1[0].astype(BF16), w_ff2[0].astype(BF16)]

    n_in = w_in.shape[2]
    n_ff = w_ff1.shape[2]

    def halves(p):
        r = p.shape[1] // 2
        return p[:, :r], p[:, r:]

    dist = dict(
        assemble=lambda g_out, g_ff1, g_ff2: (g_out.reshape(d, d), g_ff1.transpose(1, 0, 2).reshape(d, -1),
                                              g_ff2.reshape(-1, d)),
        n_ff=n_ff,
        parts_ff2=lambda dw: halves(dw.reshape(N_DEV, -1, d)),
        parts_ff1=halves,
        parts_out=lambda dw: dw.reshape(N_DEV, d // N_DEV, d),
        parts_in=lambda dw: dw[:, :in_cols].reshape(d, N_DEV, n_in).transpose(1, 0, 2),
    )

    n_ada = w_ada.shape[2]
    b_loc = lax.dynamic_slice(b_ada, (0, me * n_ada), (1, n_ada))
    mod_part = _ada_fwd(c_all, w_ada[0], b_loc, "ada_fwd")
    mod_all = _all_gather([mod_part], "gather_mod")[0]
    mod = lax.dynamic_slice(mod_all, (0, me, 0), (N_DEV, 1, n_ada)).reshape(6, d)

    pad_lane = lambda vv: jnp.concatenate([vv, jnp.zeros((1, LANES - vv.shape[1]), F32)], axis=1)
    loss, dx, g = _local_step(
        x[0], loss_target[0], mod, pre_mix_norm, post_mix_norm, pre_ffn_norm, post_ffn_norm, w_in_p,
        hg_lb_logits, hg_norm, conv_full, pad_lane(gdn_a_log), pad_lane(gdn_dt_bias), gdn_norm,
        late_w, dist)

    rep_names = ["b_ada", "n1", "n2", "n3", "n4", "lb", "hg_norm", "alog", "dtb", "gdn_norm"]
    rep_w = [b_ada, pre_mix_norm, post_mix_norm, pre_ffn_norm, post_ffn_norm, hg_lb_logits, hg_norm,
             gdn_a_log, gdn_dt_bias, gdn_norm]
    rep_m = [m_b_ada, m_pre_mix_norm, m_post_mix_norm, m_pre_ffn_norm, m_post_ffn_norm, m_hg_lb_logits,
             m_hg_norm, m_gdn_a_log, m_gdn_dt_bias, m_gdn_norm]
    rep_v = [v_b_ada, v_pre_mix_norm, v_post_mix_norm, v_pre_ffn_norm, v_post_ffn_norm, v_hg_lb_logits,
             v_hg_norm, v_gdn_a_log, v_gdn_dt_bias, v_gdn_norm]
    rep_shapes = [a.shape for a in rep_w]
    g_lb = jnp.stack([g["lb0"], -g["lb0"]], axis=0)
    rep_g = [g["dmod"], g["n1"], g["n2"], g["n3"], g["n4"], g_lb, g["hg_norm"],
             g["alog"][:, :nh], g["dtb"][:, :nh], g["gdn_norm"]]
    small = _pack(rep_g + [g["conv"]])
    n_rep_rows = _pack(rep_g).shape[0]
    pad_rows = (-small.shape[0]) % 8
    if pad_rows:
        small = jnp.concatenate([small, jnp.zeros((pad_rows, LANES), F32)], axis=0)
    small_all = _all_gather([small], "gather_small_grads")[0]
    small_sum = _sum_parts(small_all, "sum_small_grads")
    rep_out = _adamw(_pack(rep_w), _pack(rep_m), _pack(rep_v), small_sum[:n_rep_rows], "adamw_small")
    rep_g_o, rep_d_o, rep_m_o, rep_v_o = [dict(zip(rep_names, _unpack(p, rep_shapes))) for p in rep_out]

    conv_sum = small_sum[n_rep_rows:n_rep_rows + CONV_K * conv_full.shape[1] // LANES].reshape(CONV_K, -1)
    n_conv = gdn_conv_w.shape[2]
    conv_loc = lax.dynamic_slice(conv_sum, (0, me * n_conv), (CONV_K, n_conv))
    conv_o = _adamw(gdn_conv_w, m_gdn_conv_w, v_gdn_conv_w, conv_loc, "adamw_conv")

    dmod_all = small_all[:, :6 * d // LANES, :].reshape(N_DEV, 6 * d)
    dmod_loc = lax.dynamic_slice(dmod_all, (0, me * n_ada), (N_DEV, n_ada))
    g_ada = _ada_wgrad(c_all, dmod_loc, "ada_wgrad")
    ada_o = _adamw(w_ada, m_w_ada, v_w_ada, g_ada, "adamw_ada")

    rc = g["recv"]
    r_ff2 = [rc["ff2a"][0], rc["ff2b"][0]]
    r_ff1 = [rc["ff1a"][0], rc["ff1b_out"][0]]
    r_out, r_in = rc["ff1b_out"][1], rc["in"][0]
    in_o = _adamw(w_in, m_w_in, v_w_in, r_in, "adamw_w_in", parts=True)
    out_o = _adamw(w_out, m_w_out, v_w_out, r_out, "adamw_w_out", parts=True)
    ff1_o = _adamw(w_ff1, m_w_ff1, v_w_ff1, r_ff1, "adamw_w_ff1", parts=True)
    ff2_o = _adamw(w_ff2, m_w_ff2, v_w_ff2, r_ff2, "adamw_w_ff2", parts=True)

    loss_tot = lax.psum(loss[0, 0], ("x", "y", "c"))

    def leaf(kind):
        return [ada_o[kind], rep_out_d[kind]["b_ada"], rep_out_d[kind]["n1"], rep_out_d[kind]["n2"],
                rep_out_d[kind]["n3"], rep_out_d[kind]["n4"], in_o[kind], rep_out_d[kind]["lb"],
                rep_out_d[kind]["hg_norm"], conv_o[kind], rep_out_d[kind]["alog"], rep_out_d[kind]["dtb"],
                rep_out_d[kind]["gdn_norm"], out_o[kind], ff1_o[kind], ff2_o[kind]]

    rep_out_d = [rep_g_o, rep_d_o, rep_m_o, rep_v_o]
    return (loss_tot, dx[None], *leaf(0), *leaf(1), *leaf(2), *leaf(3))
```

```python
import functools

import jax
import jax.numpy as jnp
from jax import lax
from jax.experimental import pallas as pl
from jax.experimental.pallas import tpu as pltpu

F32 = jnp.float32
BF16 = jnp.bfloat16
HI = lax.Precision.HIGHEST
HIGH = lax.Precision.HIGH

EPS = 1e-6
CHUNK = 64
SB = 16
NSB = CHUNK // SB
HP = 8
HD = 128
CONV_K = 4
N_DEV = 8
LANES = 128
SUBLANES = 8
VMEM_LIMIT = 56 * 1024 * 1024

ADAM_BLOCK_ELEMS = 256 * 1024
ADAM_LR = 0.001
ADAM_B1 = 0.9
ADAM_B2 = 0.999
ADAM_EPS = 1e-08
ADAM_WD = 0.01
ADAM_STEP = 10

ANY = pl.BlockSpec(memory_space=pl.ANY)
MESH = pl.DeviceIdType.MESH


def _cp(sem=None):
    return pltpu.CompilerParams(dimension_semantics=sem, vmem_limit_bytes=VMEM_LIMIT)


def _dot(a, b, dims, precision=None):
    return lax.dot_general(a, b, (dims, ((), ())), precision=precision, preferred_element_type=F32)


def _nn(a, b, precision=None):
    return _dot(a, b, ((1,), (0,)), precision)


def _nt(a, b, precision=None):
    return _dot(a, b, ((1,), (1,)), precision)


def _tn(a, b, precision=None):
    return _dot(a, b, ((0,), (0,)), precision)


def _bf(x):
    return x.astype(BF16)


def _sigmoid(x):
    return 1.0 / (1.0 + jnp.exp(-x))


def _interleave(gens):
    results = [None] * len(gens)
    live = list(range(len(gens)))
    while live:
        for i in list(live):
            try:
                next(gens[i])
            except StopIteration as stop:
                results[i] = stop.value
                live.remove(i)
    return results


def _listed(res):
    return list(res) if isinstance(res, (list, tuple)) else [res]


def _pick(n, pref):
    if n <= pref:
        return n
    t = pref
    while n % t:
        t -= LANES
    assert t > 0, (n, pref)
    return t


def _mm(a, b, mode, out_dtypes, name, epilogue=None, extras=(), tm=1024, tn=2048, tk=1024, comm=None,
        by_cols=False):
    if mode == "nn":
        (m, kd), (_, n) = a.shape, b.shape
    elif mode == "nt":
        (m, kd), (n, _) = a.shape, b.shape
    else:
        (kd, m), (_, n) = a.shape, b.shape
    tm, tn, tk = _pick(m, tm), _pick(n, tn), _pick(kd, tk)
    nk = kd // tk
    if mode == "nn":
        a_spec = pl.BlockSpec((tm, tk), lambda i, j, k: (i, k))
        b_spec = pl.BlockSpec((tk, tn), lambda i, j, k: (k, j))
        dims = ((1,), (0,))
    elif mode == "nt":
        a_spec = pl.BlockSpec((tm, tk), lambda i, j, k: (i, k))
        b_spec = pl.BlockSpec((tn, tk), lambda i, j, k: (j, k))
        dims = ((1,), (1,))
    else:
        a_spec = pl.BlockSpec((tk, tm), lambda i, j, k: (k, i))
        b_spec = pl.BlockSpec((tk, tn), lambda i, j, k: (k, j))
        dims = ((0,), (0,))
    o_spec = pl.BlockSpec((tm, tn), lambda i, j, k: (i, j))
    if by_cols:
        assert epilogue is None and not extras
        res_spec = pl.BlockSpec((None, tm, tn), lambda i, j, k: (j, i, 0))
        res_shape = (n // tn, m, tn)
    else:
        res_spec, res_shape = o_spec, (m, n)
    n_extra, n_out = len(extras), len(out_dtypes)

    gm, gn = m // tm, n // tn
    cn = comm.n if comm is not None else 0

    def body(*refs):
        i, j, k = pl.program_id(0), pl.program_id(1), pl.program_id(2)
        at0 = (j == 0) & (k == 0)
        ins, out_refs, scratch, comm_begin, comm_end = _comm_hooks(
            comm, refs, 2 + n_extra, n_out, (i == 0) & at0, (i == gm - 1) & at0,
            (i == gm - 1) & (j == gn - 1) & (k == nk - 1))
        a_ref, b_ref, extra_refs = ins[0], ins[1], ins[2:]
        acc, = scratch
        comm_begin()

        @pl.when(k == 0)
        def _():
            acc[...] = jnp.zeros_like(acc)

        acc[...] += _dot(a_ref[...], b_ref[...], dims)

        @pl.when(k == nk - 1)
        def _():
            if epilogue is None:
                out_refs[0][...] = acc[...].astype(out_dtypes[0])
            else:
                epilogue(acc[...], extra_refs, out_refs)

        comm_end()

    sem = ("arbitrary",) * 3 if cn else ("parallel", "parallel", "arbitrary")
    outs = pl.pallas_call(
        body, name=name,
        grid=(gm, gn, nk),
        in_specs=[a_spec, b_spec] + [o_spec] * n_extra + [ANY] * cn,
        out_specs=[res_spec] * n_out + [ANY] * cn,
        out_shape=[jax.ShapeDtypeStruct(res_shape, dt) for dt in out_dtypes] + (comm.out_shapes() if cn else []),
        scratch_shapes=[pltpu.VMEM((tm, tn), F32)] + (comm.scratch() if cn else []),
        compiler_params=_cp(sem),
    )(a, b, *extras, *(comm.arrays if cn else []))
    return outs[0] if n_out + cn == 1 else outs


def _row_spec(tb, d):
    return pl.BlockSpec((tb, d), lambda i: (i, 0))


def _vec_spec(d):
    return pl.BlockSpec((1, d), lambda i: (0, 0))


def _prenorm(x, w, sc, sh, name):
    t, d = x.shape
    tb = _pick(t, 256)

    def body(x_ref, w_ref, sc_ref, sh_ref, h_ref, r_ref):
        xv = x_ref[...]
        r = lax.rsqrt(jnp.mean(xv * xv, axis=-1, keepdims=True) + EPS)
        h_ref[...] = ((xv * r * w_ref[...]) * (1.0 + sc_ref[...]) + sh_ref[...]).astype(BF16)
        r_ref[...] = r

    return pl.pallas_call(
        body, name=name, grid=(t // tb,),
        in_specs=[_row_spec(tb, d), _vec_spec(d), _vec_spec(d), _vec_spec(d)],
        out_specs=[_row_spec(tb, d), _row_spec(tb, 1)],
        out_shape=[jax.ShapeDtypeStruct((t, d), BF16), jax.ShapeDtypeStruct((t, 1), F32)],
        compiler_params=_cp(("parallel",)),
    )(x, w, sc, sh)


def _postnorm_res(x, y, w, gt, name):
    t, d = x.shape
    tb = _pick(t, 256)

    def body(x_ref, y_ref, w_ref, gt_ref, o_ref, r_ref):
        yv = y_ref[...]
        r = lax.rsqrt(jnp.mean(yv * yv, axis=-1, keepdims=True) + EPS)
        o_ref[...] = x_ref[...] + gt_ref[...] * (yv * r * w_ref[...])
        r_ref[...] = r

    return pl.pallas_call(
        body, name=name, grid=(t // tb,),
        in_specs=[_row_spec(tb, d), _row_spec(tb, d), _vec_spec(d), _vec_spec(d)],
        out_specs=[_row_spec(tb, d), _row_spec(tb, 1)],
        out_shape=[jax.ShapeDtypeStruct((t, d), F32), jax.ShapeDtypeStruct((t, 1), F32)],
        compiler_params=_cp(("parallel",)),
    )(x, y, w, gt)


def _final_loss(x, y, w, gt, tgt, name):
    t, d = x.shape
    tb = _pick(t, 256)

    def body(x_ref, y_ref, w_ref, gt_ref, tgt_ref, dout_ref, r_ref, loss_ref):
        @pl.when(pl.program_id(0) == 0)
        def _():
            loss_ref[...] = jnp.zeros_like(loss_ref)

        yv = y_ref[...]
        r = lax.rsqrt(jnp.mean(yv * yv, axis=-1, keepdims=True) + EPS)
        out = x_ref[...] + gt_ref[...] * (yv * r * w_ref[...])
        diff = out - tgt_ref[...]
        row = jnp.mean(diff * diff, axis=-1, keepdims=True)
        loss_ref[...] += 0.5 * jnp.sum(row, axis=0, keepdims=True)
        dout_ref[...] = diff * (1.0 / d)
        r_ref[...] = r

    return pl.pallas_call(
        body, name=name, grid=(t // tb,),
        in_specs=[_row_spec(tb, d), _row_spec(tb, d), _vec_spec(d), _vec_spec(d), _row_spec(tb, d)],
        out_specs=[_row_spec(tb, d), _row_spec(tb, 1), pl.BlockSpec((1, 1), lambda i: (0, 0))],
        out_shape=[jax.ShapeDtypeStruct((t, d), F32), jax.ShapeDtypeStruct((t, 1), F32),
                   jax.ShapeDtypeStruct((1, 1), F32)],
        compiler_params=_cp(("arbitrary",)),
    )(x, y, w, gt, tgt)


def _postnorm_bwd(dxn, y, r, w, gt, name):
    t, d = y.shape
    tb = _pick(t, 256)

    def body(dx_ref, y_ref, r_ref, w_ref, gt_ref, dy_ref, dgt_ref, dw_ref):
        @pl.when(pl.program_id(0) == 0)
        def _():
            dgt_ref[...] = jnp.zeros_like(dgt_ref)
            dw_ref[...] = jnp.zeros_like(dw_ref)

        dxv, rv, wv = dx_ref[...], r_ref[...], w_ref[...]
        z = y_ref[...] * rv
        dgt_ref[...] += jnp.sum(dxv * (z * wv), axis=0, keepdims=True)
        dn = dxv * gt_ref[...]
        dw_ref[...] += jnp.sum(dn * z, axis=0, keepdims=True)
        dz = dn * wv
        dy_ref[...] = (rv * (dz - z * jnp.mean(dz * z, axis=-1, keepdims=True))).astype(BF16)

    return pl.pallas_call(
        body, name=name, grid=(t // tb,),
        in_specs=[_row_spec(tb, d), _row_spec(tb, d), _row_spec(tb, 1), _vec_spec(d), _vec_spec(d)],
        out_specs=[_row_spec(tb, d), _vec_spec(d), _vec_spec(d)],
        out_shape=[jax.ShapeDtypeStruct((t, d), BF16), jax.ShapeDtypeStruct((1, d), F32),
                   jax.ShapeDtypeStruct((1, d), F32)],
        compiler_params=_cp(("arbitrary",)),
    )(dxn, y, r, w, gt)


def _prenorm_bwd(dh, x, r, w, sc, dres, name):
    t, d = x.shape
    tb = _pick(t, 256)

    def body(dh_ref, x_ref, r_ref, w_ref, sc_ref, dres_ref, dx_ref, dsh_ref, dsc_ref, dw_ref):
        @pl.when(pl.program_id(0) == 0)
        def _():
            dsh_ref[...] = jnp.zeros_like(dsh_ref)
            dsc_ref[...] = jnp.zeros_like(dsc_ref)
            dw_ref[...] = jnp.zeros_like(dw_ref)

        dhv, rv, wv = dh_ref[...], r_ref[...], w_ref[...]
        z = x_ref[...] * rv
        dsh_ref[...] += jnp.sum(dhv, axis=0, keepdims=True)
        dsc_ref[...] += jnp.sum(dhv * (z * wv), axis=0, keepdims=True)
        dzw = dhv * (1.0 + sc_ref[...])
        dw_ref[...] += jnp.sum(dzw * z, axis=0, keepdims=True)
        dz = dzw * wv
        dx_ref[...] = dres_ref[...] + rv * (dz - z * jnp.mean(dz * z, axis=-1, keepdims=True))

    return pl.pallas_call(
        body, name=name, grid=(t // tb,),
        in_specs=[_row_spec(tb, d), _row_spec(tb, d), _row_spec(tb, 1), _vec_spec(d), _vec_spec(d),
                  _row_spec(tb, d)],
        out_specs=[_row_spec(tb, d), _vec_spec(d), _vec_spec(d), _vec_spec(d)],
        out_shape=[jax.ShapeDtypeStruct((t, d), F32)] + [jax.ShapeDtypeStruct((1, d), F32)] * 3,
        compiler_params=_cp(("arbitrary",)),
    )(dh, x, r, w, sc, dres)


def _headnorm_fwd(o, proj, g_blk, nw, name):
    t, wd = o.shape
    nh = wd // HD
    tb = _pick(t, 512)
    gb = g_blk * HD // wd

    def body(o_ref, g_ref, nw_ref, out_ref):
        o3 = o_ref[...].reshape(tb, nh, HD)
        g3 = g_ref[...].reshape(tb, nh, HD)
        rh = lax.rsqrt(jnp.mean(o3 * o3, axis=-1, keepdims=True) + EPS)
        res = (o3 * rh * nw_ref[...].reshape(1, 1, HD)) * (g3 * _sigmoid(g3))
        out_ref[...] = res.reshape(tb, wd).astype(BF16)

    return pl.pallas_call(
        body, name=name, grid=(t // tb,),
        in_specs=[_row_spec(tb, wd), pl.BlockSpec((tb, wd), lambda i: (i, gb)), _vec_spec(HD)],
        out_specs=_row_spec(tb, wd),
        out_shape=jax.ShapeDtypeStruct((t, wd), BF16),
        compiler_params=_cp(("parallel",)),
    )(o, proj, nw)


def _headnorm_bwd(dom, col_blk, o, proj, g_blk, nw, name):
    t, wd = o.shape
    nh = wd // HD
    tb = _pick(t, 512)
    gb = g_blk * HD // wd

    def body(do_ref, o_ref, g_ref, nw_ref, dout_ref, dg_ref, dnw_ref):
        @pl.when(pl.program_id(0) == 0)
        def _():
            dnw_ref[...] = jnp.zeros_like(dnw_ref)

        dn = do_ref[...].reshape(tb, nh, HD)
        o3 = o_ref[...].reshape(tb, nh, HD)
        g3 = g_ref[...].reshape(tb, nh, HD)
        nw3 = nw_ref[...].reshape(1, 1, HD)
        rh = lax.rsqrt(jnp.mean(o3 * o3, axis=-1, keepdims=True) + EPS)
        z = o3 * rh
        sg = _sigmoid(g3)
        sl = g3 * sg
        dnw_ref[...] += jnp.sum(jnp.sum(dn * sl * z, axis=1), axis=0, keepdims=True)
        dg_ref[...] = (dn * (z * nw3) * (sg * (1.0 + g3 * (1.0 - sg)))).reshape(tb, wd).astype(BF16)
        dz = dn * sl * nw3
        dout_ref[...] = (rh * (dz - z * jnp.mean(dz * z, axis=-1, keepdims=True))).reshape(tb, wd)

    return pl.pallas_call(
        body, name=name, grid=(t // tb,),
        in_specs=[pl.BlockSpec((tb, wd), lambda i: (i, col_blk)), _row_spec(tb, wd),
                  pl.BlockSpec((tb, wd), lambda i: (i, gb)), _vec_spec(HD)],
        out_specs=[_row_spec(tb, wd), _row_spec(tb, wd), _vec_spec(HD)],
        out_shape=[jax.ShapeDtypeStruct((t, wd), F32), jax.ShapeDtypeStruct((t, wd), BF16),
                   jax.ShapeDtypeStruct((1, HD), F32)],
        compiler_params=_cp(("arbitrary",)),
    )(dom, o, proj, nw)


def _tri(n, kind):
    r = lax.broadcasted_iota(jnp.int32, (n, n), 0)
    c = lax.broadcasted_iota(jnp.int32, (n, n), 1)
    if kind == "lower":
        return r >= c
    if kind == "strict":
        return r > c
    return r <= c


def _hg_gate(fl, l0, l1):
    mx = jnp.maximum(l0, l1)
    e0, e1 = jnp.exp(l0 - mx), jnp.exp(l1 - mx)
    lb = e0 / (e0 + e1)
    sg = _sigmoid(fl)
    f = lb + (1.0 - lb) * sg
    return lb, sg, f


def _hgrn2_fwd(proj, lb_logits, nh, name, comm=None):
    t = proj.shape[0]
    nc = t // CHUNK
    C = CHUNK
    lg = lb_logits.reshape(2, nh, 1, HD)

    hp = min(HP, nh)
    ng = nh // hp

    def one_head(hh, st, q_ref, f_ref, i_ref, lg_ref, p_sc, r_sc):
        sl = slice(hh * HD, (hh + 1) * HD)
        q, v = q_ref[:, sl], i_ref[:, sl]
        _, _, f = _hg_gate(f_ref[:, sl], lg_ref[0, hh], lg_ref[1, hh])
        k = 1.0 - f
        low = _tri(C, "lower")
        b = _nn(low.astype(F32), jnp.log(f), HI)
        yield
        lane_c = lax.broadcasted_iota(jnp.int32, (SB, C), 1)
        lane_h = lax.broadcasted_iota(jnp.int32, (SB, HD), 1)
        row_h = lax.broadcasted_iota(jnp.int32, (SB, HD), 0)
        ones = jnp.ones((HD, HD), F32)

        for i in range(NSB):
            qi, ki, bi = q[SB * i:SB * (i + 1)], k[SB * i:SB * (i + 1)], b[SB * i:SB * (i + 1)]
            for s in range(SB):
                e = jnp.exp(jnp.minimum(bi - bi[s:s + 1], 0.0))
                p = jnp.where(row_h >= s, qi * ki[s:s + 1] * e, 0.0)
                p_sc[hh, pl.ds((i * SB + s) * SB, SB), :] = p
            yield
        r_sc[hh] = _nn(p_sc[hh], ones, HIGH)
        yield
        a_rows = []
        for i in range(NSB):
            acc = jnp.zeros((SB, HD), F32)
            for s in range(SB):
                acc = jnp.where(lane_h == SB * i + s, r_sc[hh, pl.ds((i * SB + s) * SB, SB), :], acc)
            acc = acc[:, :C]
            if i > 0:
                r = b[SB * i - 1:SB * i]
                bi = b[SB * i:SB * (i + 1)]
                qf = q[SB * i:SB * (i + 1)] * jnp.exp(bi - r)
                kf = k * jnp.exp(jnp.minimum(r - b, 0.0))
                acc = acc + jnp.where(lane_c < SB * i, _nt(qf, kf, HIGH), 0.0)
            a_rows.append(acc)
            yield
        a = jnp.concatenate(a_rows, axis=0)
        bl = b[C - 1:C, :]
        o = _nn(_bf(a), _bf(v)) + _nt(_bf(q * jnp.exp(b)), _bf(st))
        yield
        new_st = st * jnp.exp(bl) + _tn(_bf(v), _bf(k * jnp.exp(bl - b)))
        return o, a, new_st

    def body(*refs):
        c, hg = pl.program_id(0), pl.program_id(1)
        step = c * ng + hg
        ins, outs, scratch, comm_begin, comm_end = _comm_hooks(
            comm, refs, 4, 3, step == 0, step == (3 * nc * ng) // 4, step == nc * ng - 1)
        o_ref, a_ref, st_ref = outs
        s_sc, p_sc, r_sc = scratch
        comm_begin()

        @pl.when(c == 0)
        def _():
            for hh in range(hp):
                s_sc[hg * hp + hh] = jnp.zeros((HD, HD), F32)

        sts = [s_sc[hg * hp + hh] for hh in range(hp)]
        res = _interleave([one_head(hh, sts[hh], *ins, p_sc, r_sc) for hh in range(hp)])
        for hh in range(hp):
            o_ref[:, hh * HD:(hh + 1) * HD] = res[hh][0]
            a_ref[0, hh] = res[hh][1]
            st_ref[0, hh] = sts[hh]
            s_sc[hg * hp + hh] = res[hh][2]
        comm_end()

    blk = lambda off: pl.BlockSpec((C, hp * HD), lambda c, g: (c, off // hp + g))
    cn = comm.n if comm is not None else 0
    return pl.pallas_call(
        body, name=name, grid=(nc, ng),
        in_specs=[blk(0), blk(nh), blk(2 * nh),
                  pl.BlockSpec((2, hp, 1, HD), lambda c, g: (0, g, 0, 0))] + [ANY] * cn,
        out_specs=[blk(0),
                   pl.BlockSpec((1, hp, C, C), lambda c, g: (c, g, 0, 0)),
                   pl.BlockSpec((1, hp, HD, HD), lambda c, g: (c, g, 0, 0))] + [ANY] * cn,
        out_shape=[jax.ShapeDtypeStruct((t, nh * HD), F32),
                   jax.ShapeDtypeStruct((nc, nh, C, C), F32),
                   jax.ShapeDtypeStruct((nc, nh, HD, HD), F32)] + (comm.out_shapes() if cn else []),
        scratch_shapes=[pltpu.VMEM((nh, HD, HD), F32), pltpu.VMEM((hp, C * SB, HD), F32),
                        pltpu.VMEM((hp, C * SB, HD), F32)] + (comm.scratch() if cn else []),
        compiler_params=_cp(("arbitrary", "arbitrary")),
    )(proj, proj, proj, lg, *(comm.arrays if cn else []))


def _hgrn2_bwd(proj, lb_logits, do, a_sv, st_sv, nh, name, comm=None):
    t = proj.shape[0]
    nc = t // CHUNK
    C = CHUNK
    lg = lb_logits.reshape(2, nh, 1, HD)
    hp = min(HP, nh)
    ng = nh // hp

    def one_head(hh, dst, q_ref, f_ref, i_ref, lg_ref, do_ref, a_ref, st_ref, p_sc, r_sc):
        sl = slice(hh * HD, (hh + 1) * HD)
        q, v, do_ = q_ref[:, sl], i_ref[:, sl], do_ref[:, sl]
        lb, sg, f = _hg_gate(f_ref[:, sl], lg_ref[0, hh], lg_ref[1, hh])
        k = 1.0 - f
        low = _tri(C, "lower")
        b = _nn(low.astype(F32), jnp.log(f), HI)
        yield
        bl = b[C - 1:C, :]
        eb, ekb = jnp.exp(b), jnp.exp(bl - b)
        qb, kb = q * eb, k * ekb
        a, st = a_ref[0, hh], st_ref[0, hh]

        da = jnp.where(low, _nt(_bf(do_), _bf(v)), 0.0)
        yield
        dv = _tn(_bf(a), _bf(do_)) + _nt(_bf(kb), _bf(dst))
        yield
        dqb = _nn(_bf(do_), _bf(st))
        dkb = _nn(_bf(v), _bf(dst))
        yield

        row = lax.broadcasted_iota(jnp.int32, (C, HD), 0)
        lane_c = lax.broadcasted_iota(jnp.int32, (SB, C), 1)
        row_h = lax.broadcasted_iota(jnp.int32, (SB, HD), 0)
        ones = jnp.ones((HD, HD), F32)
        sel = (lax.broadcasted_iota(jnp.int32, (C, C * SB), 0)
               == jnp.right_shift(lax.broadcasted_iota(jnp.int32, (C, C * SB), 1), SB.bit_length() - 1)).astype(F32)

        for i in range(NSB):
            doi, vi = do_[SB * i:SB * (i + 1)], v[SB * i:SB * (i + 1)]
            for s in range(SB):
                p_sc[hh, pl.ds((i * SB + s) * SB, SB), :] = doi * vi[s:s + 1]
            yield
        r_sc[hh] = _nn(p_sc[hh], ones, HIGH)
        yield
        dq_rows = []
        dk_off = jnp.zeros((C, HD), F32)
        for i in range(NSB):
            qi, ki, bi = q[SB * i:SB * (i + 1)], k[SB * i:SB * (i + 1)], b[SB * i:SB * (i + 1)]
            acc = jnp.zeros((SB, HD), F32)
            for s in range(SB):
                e = jnp.exp(jnp.minimum(bi - bi[s:s + 1], 0.0))
                g = jnp.where(row_h >= s, r_sc[hh, pl.ds((i * SB + s) * SB, SB), :] * e, 0.0)
                acc = acc + g * ki[s:s + 1]
                p_sc[hh, pl.ds((i * SB + s) * SB, SB), :] = g * qi
            yield
            if i > 0:
                r = b[SB * i - 1:SB * i]
                fq = jnp.exp(bi - r)
                fk = jnp.exp(jnp.minimum(r - b, 0.0))
                dai = jnp.where(lane_c < SB * i, da[SB * i:SB * (i + 1)], 0.0)
                acc = acc + _nn(dai, k * fk, HIGH) * fq
                dk_off = dk_off + _tn(dai, qi * fq, HIGH) * fk
                yield
            dq_rows.append(acc)
        dqi = jnp.concatenate(dq_rows, axis=0)
        dq = dqi + dqb * eb
        dk_inter = dkb * ekb
        dk = _nn(sel, p_sc[hh], HIGH) + dk_off + dk_inter
        yield
        db = q * dq - k * dk
        extra = (jnp.sum(k * dk_inter, axis=0, keepdims=True)
                 + jnp.exp(bl) * jnp.sum(dst * st, axis=0, keepdims=True))
        db = db + jnp.where(row == C - 1, extra, 0.0)
        dlf = _nn(_tri(C, "upper").astype(F32), db, HI)
        yield
        df = dlf / f - dk
        dfl = (df * (1.0 - lb) * sg * (1.0 - sg)).astype(BF16)
        dl = jnp.sum(df * (1.0 - sg), axis=0, keepdims=True) * (lb * (1.0 - lb))
        new_dst = dst * jnp.exp(bl) + _tn(_bf(do_), _bf(qb))
        return dq.astype(BF16), dfl, dv.astype(BF16), dl, new_dst

    def body(*refs):
        c, hg = pl.program_id(0), pl.program_id(1)
        step = c * ng + hg
        ins, outs, scratch, comm_begin, comm_end = _comm_hooks(
            comm, refs, 7, 4, step == 0, step == (3 * nc * ng) // 4, step == nc * ng - 1)
        dq_ref, df_ref, di_ref, dl_ref = outs
        ds_sc, p_sc, r_sc = scratch
        comm_begin()

        @pl.when(c == 0)
        def _():
            for hh in range(hp):
                ds_sc[hg * hp + hh] = jnp.zeros((HD, HD), F32)

        @pl.when(step == 0)
        def _():
            dl_ref[...] = jnp.zeros_like(dl_ref)

        dsts = [ds_sc[hg * hp + hh] for hh in range(hp)]
        res = _interleave([one_head(hh, dsts[hh], *ins, p_sc, r_sc) for hh in range(hp)])
        for hh in range(hp):
            sl = slice(hh * HD, (hh + 1) * HD)
            dq_ref[:, sl], df_ref[:, sl], di_ref[:, sl] = res[hh][0], res[hh][1], res[hh][2]
            dl_ref[pl.ds(hg * hp + hh, 1), :] += res[hh][3]
            ds_sc[hg * hp + hh] = res[hh][4]
        comm_end()

    rblk = lambda off: pl.BlockSpec((C, hp * HD), lambda c, g: (nc - 1 - c, off // hp + g))
    oblk = pl.BlockSpec((C, hp * HD), lambda c, g: (nc - 1 - c, g))
    cn = comm.n if comm is not None else 0
    return pl.pallas_call(
        body, name=name, grid=(nc, ng),
        in_specs=[rblk(0), rblk(nh), rblk(2 * nh),
                  pl.BlockSpec((2, hp, 1, HD), lambda c, g: (0, g, 0, 0)),
                  oblk,
                  pl.BlockSpec((1, hp, C, C), lambda c, g: (nc - 1 - c, g, 0, 0)),
                  pl.BlockSpec((1, hp, HD, HD), lambda c, g: (nc - 1 - c, g, 0, 0))] + [ANY] * cn,
        out_specs=[oblk, oblk, oblk, pl.BlockSpec((nh, HD), lambda c, g: (0, 0))] + [ANY] * cn,
        out_shape=[jax.ShapeDtypeStruct((t, nh * HD), BF16)] * 3 + [jax.ShapeDtypeStruct((nh, HD), F32)]
        + (comm.out_shapes() if cn else []),
        scratch_shapes=[pltpu.VMEM((nh, HD, HD), F32), pltpu.VMEM((hp, C * SB, HD), F32),
                        pltpu.VMEM((hp, C * SB, HD), F32)] + (comm.scratch() if cn else []),
        compiler_params=_cp(("arbitrary", "arbitrary")),
    )(proj, proj, proj, lg, do, a_sv, st_sv, *(comm.arrays if cn else []))


def _shift_rows(u, d, row):
    t = u.shape[0]
    if d == 0:
        return u
    rolled = pltpu.roll(u, d % t, 0)
    if d > 0:
        return jnp.where(row >= d, rolled, 0.0)
    return jnp.where(row < t + d, rolled, 0.0)


def _gdn_prep(proj, conv_w, blk0, nh, name):
    t = proj.shape[0]
    scale = HD ** -0.5

    def body(u_ref, w_ref, o_ref):
        j = pl.program_id(0)
        u, w = u_ref[...], w_ref[...]
        row = lax.broadcasted_iota(jnp.int32, (t, HD), 0)
        y = w[CONV_K - 1:CONV_K, :] * u
        for d in range(1, CONV_K):
            y = y + w[CONV_K - 1 - d:CONV_K - d, :] * _shift_rows(u, d, row)
        a = y * _sigmoid(y)
        n = a * lax.rsqrt(jnp.sum(a * a, axis=-1, keepdims=True) + EPS)
        n = n * jnp.where(j < nh, scale, 1.0)
        o_ref[...] = jnp.where(j < 2 * nh, n, a)

    return pl.pallas_call(
        body, name=name, grid=(3 * nh,),
        in_specs=[pl.BlockSpec((t, HD), lambda j: (0, blk0 + j)), pl.BlockSpec((CONV_K, HD), lambda j: (0, j))],
        out_specs=pl.BlockSpec((t, HD), lambda j: (0, j)),
        out_shape=jax.ShapeDtypeStruct((t, 3 * nh * HD), F32),
        compiler_params=_cp(("parallel",)),
    )(proj, conv_w)


def _gdn_prep_bwd(proj, conv_w, dqkv, blk0, nh, name):
    t = proj.shape[0]
    scale = HD ** -0.5

    def body(u_ref, w_ref, d_ref, du_ref, dw_ref):
        j = pl.program_id(0)
        u, w, dout = u_ref[...], w_ref[...], d_ref[...]
        row = lax.broadcasted_iota(jnp.int32, (t, HD), 0)
        us = [_shift_rows(u, d, row) for d in range(CONV_K)]
        y = w[CONV_K - 1:CONV_K, :] * us[0]
        for d in range(1, CONV_K):
            y = y + w[CONV_K - 1 - d:CONV_K - d, :] * us[d]
        sg = _sigmoid(y)
        a = y * sg
        rs = lax.rsqrt(jnp.sum(a * a, axis=-1, keepdims=True) + EPS)
        n = a * rs
        dn = dout * jnp.where(j < nh, scale, 1.0)
        da_n = rs * (dn - n * jnp.sum(dn * n, axis=-1, keepdims=True))
        da = jnp.where(j < 2 * nh, da_n, dout)
        dy = da * (sg * (1.0 + y * (1.0 - sg)))
        du = w[CONV_K - 1:CONV_K, :] * dy
        for d in range(1, CONV_K):
            du = du + w[CONV_K - 1 - d:CONV_K - d, :] * _shift_rows(dy, -d, row)
        du_ref[...] = du.astype(BF16)
        for d in range(CONV_K):
            dw_ref[CONV_K - 1 - d:CONV_K - d, :] = jnp.sum(dy * us[d], axis=0, keepdims=True)

    return pl.pallas_call(
        body, name=name, grid=(3 * nh,),
        in_specs=[pl.BlockSpec((t, HD), lambda j: (0, blk0 + j)), pl.BlockSpec((CONV_K, HD), lambda j: (0, j)),
                  pl.BlockSpec((t, HD), lambda j: (0, j))],
        out_specs=[pl.BlockSpec((t, HD), lambda j: (0, j)), pl.BlockSpec((CONV_K, HD), lambda j: (0, j))],
        out_shape=[jax.ShapeDtypeStruct((t, 3 * nh * HD), BF16), jax.ShapeDtypeStruct((CONV_K, 3 * nh * HD), F32)],
        compiler_params=_cp(("parallel",)),
    )(proj, conv_w, dqkv)


def _gdn_gates(ab, alog, dtb, h, nh):
    lane = lax.broadcasted_iota(jnp.int32, ab.shape, 1)
    x = ab + dtb
    sp = jnp.maximum(x, 0.0) + jnp.log(1.0 + jnp.exp(-jnp.abs(x)))
    ea = jnp.exp(alog)
    la_all = -ea * sp
    beta_all = _sigmoid(ab)
    pick = lambda val, ln: jnp.sum(jnp.where(lane == ln, val, 0.0), axis=1, keepdims=True)
    la = pick(la_all, h)
    beta = pick(beta_all, nh + h)
    dla_da = pick(-ea * _sigmoid(x), h)
    return la, beta, dla_da


def _gdn_chunks(qs, ks, vs, las, betas, C):
    low, strict = _tri(C, "lower"), _tri(C, "strict")
    eye = (lax.broadcasted_iota(jnp.int32, (C, C), 0) == lax.broadcasted_iota(jnp.int32, (C, C), 1)).astype(F32)
    g_bs = [_nn(low.astype(F32), jnp.broadcast_to(la, (C, HD)), HI) for la in las]
    ps = [_nt(k, k, HI) for k in ks]
    qks = [_nt(_bf(q), _bf(k)) for q, k in zip(qs, ks)]
    chs = []
    for g_b, p, qk_raw, beta in zip(g_bs, ps, qks, betas):
        g_c = g_b[:, :C]
        gamma = jnp.where(low, jnp.exp(jnp.minimum(g_c - g_c.T, 0.0)), 0.0)
        gl = g_b[C - 1:C, :]
        chs.append(dict(gamma=gamma, eg=jnp.exp(g_b), gl=gl, ekt=jnp.exp(gl - g_b), p=p,
                        m=jnp.where(strict, beta * p * gamma, 0.0), qk_raw=qk_raw))
    xs = [eye for _ in chs]
    for s in range(C - 1):
        xs = [x - ch["m"][:, s:s + 1] * x[s:s + 1, :] for x, ch in zip(xs, chs)]
    r_ws = [k * (beta * ch["eg"]) for ch, k, beta in zip(chs, ks, betas)]
    uws = [_nn(x, jnp.concatenate([v * beta, r_w], axis=1), HI) for x, v, beta, r_w in zip(xs, vs, betas, r_ws)]
    for ch, x, r_w, uw in zip(chs, xs, r_ws, uws):
        ch.update(x=x, r_w=r_w, uw=uw)
    return chs


def _gdn_fwd(qkv, proj, ab_blk, alog, dtb, nh, name, comm=None):
    t = qkv.shape[0]
    nc = t // CHUNK
    C = CHUNK
    hp = min(HP, nh)
    ng = nh // hp

    def body(*refs):
        c, hg = pl.program_id(0), pl.program_id(1)
        step = c * ng + hg
        ins, outs, scratch, comm_begin, comm_end = _comm_hooks(
            comm, refs, 6, 3, step == 0, step == (3 * nc * ng) // 4, step == nc * ng - 1)
        q_ref, k_ref, v_ref, ab_ref, al_ref, dt_ref = ins
        o_ref, x_ref, st_ref = outs
        s_sc, = scratch
        comm_begin()

        @pl.when(c == 0)
        def _():
            for hh in range(hp):
                s_sc[hg * hp + hh] = jnp.zeros((HD, HD), F32)

        sls = [slice(hh * HD, (hh + 1) * HD) for hh in range(hp)]
        qs, ks, vs = [q_ref[:, sl] for sl in sls], [k_ref[:, sl] for sl in sls], [v_ref[:, sl] for sl in sls]
        sts = [s_sc[hg * hp + hh] for hh in range(hp)]
        gates = [_gdn_gates(ab_ref[...], al_ref[...], dt_ref[...], hg * hp + hh, nh) for hh in range(hp)]
        chs = _gdn_chunks(qs, ks, vs, [g[0] for g in gates], [g[1] for g in gates], C)
        stbs = [_bf(st) for st in sts]
        vns = [ch["uw"][:, :HD] - _nt(_bf(ch["uw"][:, HD:]), stb) for ch, stb in zip(chs, stbs)]
        o_st = [_nt(_bf(q * ch["eg"]), stb) for q, ch, stb in zip(qs, chs, stbs)]
        outs_ = [o + _nn(_bf(ch["qk_raw"] * ch["gamma"]), _bf(vn)) for o, ch, vn in zip(o_st, chs, vns)]
        new_sts = [st * jnp.exp(ch["gl"]) + _tn(_bf(vn), _bf(k * ch["ekt"]))
                   for st, ch, vn, k in zip(sts, chs, vns, ks)]
        for hh in range(hp):
            o_ref[:, sls[hh]] = outs_[hh]
            x_ref[0, hh] = chs[hh]["x"]
            st_ref[0, hh] = sts[hh]
            s_sc[hg * hp + hh] = new_sts[hh]
        comm_end()

    blk = lambda off: pl.BlockSpec((C, hp * HD), lambda c, g: (c, off // hp + g))
    vec = pl.BlockSpec((1, HD), lambda c, g: (0, 0))
    cn = comm.n if comm is not None else 0
    return pl.pallas_call(
        body, name=name, grid=(nc, ng),
        in_specs=[blk(0), blk(nh), blk(2 * nh), pl.BlockSpec((C, HD), lambda c, g: (c, ab_blk)), vec, vec]
        + [ANY] * cn,
        out_specs=[blk(0),
                   pl.BlockSpec((1, hp,C, C), lambda c, g: (c, g, 0, 0)),
                   pl.BlockSpec((1, hp,HD, HD), lambda c, g: (c, g, 0, 0))] + [ANY] * cn,
        out_shape=[jax.ShapeDtypeStruct((t, nh * HD), F32),
                   jax.ShapeDtypeStruct((nc, nh, C, C), F32),
                   jax.ShapeDtypeStruct((nc, nh, HD, HD), F32)] + (comm.out_shapes() if cn else []),
        scratch_shapes=[pltpu.VMEM((nh, HD, HD), F32)] + (comm.scratch() if cn else []),
        compiler_params=_cp(("arbitrary", "arbitrary")),
    )(qkv, qkv, qkv, proj, alog, dtb, *(comm.arrays if cn else []))


def _gdn_bwd(qkv, proj, ab_blk, alog, dtb, do, x_sv, st_sv, nh, name, comm=None):
    t = qkv.shape[0]
    nc = t // CHUNK
    C = CHUNK
    hp = min(HP, nh)
    ng = nh // hp

    def one_head(h, hh, dst, q_ref, k_ref, v_ref, ab_ref, al_ref, dt_ref, do_ref, x_ref, st_ref):
        sl = slice(hh * HD, (hh + 1) * HD)
        q, k, v, do_ = q_ref[:, sl], k_ref[:, sl], v_ref[:, sl], do_ref[:, sl]
        la, beta, dla_da = _gdn_gates(ab_ref[...], al_ref[...], dt_ref[...], h, nh)
        low, strict = _tri(C, "lower"), _tri(C, "strict")
        g_b = _nn(low.astype(F32), jnp.broadcast_to(la, (C, HD)), HI)
        yield
        g_c = g_b[:, :C]
        gamma = jnp.where(low, jnp.exp(jnp.minimum(g_c - g_c.T, 0.0)), 0.0)
        eg = jnp.exp(g_b)
        gl = g_b[C - 1:C, :]
        ekt = jnp.exp(gl - g_b)
        egl = jnp.exp(gl)
        p = _nt(k, k, HI)
        yield
        x = x_ref[0, hh]
        r_w = k * (beta * eg)
        rhs = jnp.concatenate([v * beta, r_w], axis=1)
        uw = _nn(x, rhs, HI)
        yield
        u, w = uw[:, :HD], uw[:, HD:]
        qk_raw = _nt(_bf(q), _bf(k))
        yield
        qk = qk_raw * gamma
        st = st_ref[0, hh]
        stb, dstb = _bf(st), _bf(dst)
        vn = u - _nt(_bf(w), stb)
        yield
        qd, kt = q * eg, k * ekt

        dvn = _tn(_bf(qk), _bf(do_)) + _nt(_bf(kt), dstb)
        yield
        dq2 = jnp.where(low, _nt(_bf(do_), _bf(vn)), 0.0)
        yield
        dqd = _nn(_bf(do_), stb)
        yield
        dkt = _nn(_bf(vn), dstb)
        yield
        dw = -_nn(_bf(dvn), stb)
        yield
        dxx = jnp.concatenate([dvn, dw], axis=1)
        dr = _tn(x, dxx, HI)
        yield
        dm = -jnp.where(strict, _nt(dr, uw, HI), 0.0)
        yield
        dr_u, dr_w = dr[:, :HD], dr[:, HD:]
        rsum = lambda z: jnp.sum(z, axis=1, keepdims=True)

        dv = dr_u * beta
        dmg = dm * gamma
        dbeta = rsum(dr_u * v) + rsum(dr_w * k) * eg[:, :1] + rsum(dmg * p)
        yield
        dp = dmg * beta
        dq2g = dq2 * gamma
        dk = (dr_w * (beta * eg) + dkt * ekt + _tn(_bf(dq2g), _bf(q))
              + _nn(_bf(dp + dp.T), _bf(k)))
        yield
        dq = dqd * eg + _nn(_bf(dq2g), _bf(k))
        yield
        e = dp * p + dq2g * qk_raw
        t_kt = rsum(dkt * kt)
        dg = rsum(dqd * qd) + rsum(dr_w * r_w) - t_kt + rsum(e) - rsum(e.T)
        yield
        dgl = jnp.sum(t_kt, axis=0, keepdims=True) + jnp.sum(dst * st, keepdims=True) * egl[:, :1]
        rowc = lax.broadcasted_iota(jnp.int32, (C, 1), 0)
        dg = dg + jnp.where(rowc == C - 1, dgl, 0.0)
        dla = _nn(_tri(C, "upper").astype(F32), jnp.broadcast_to(dg, (C, HD)), HI)[:, :1]
        yield
        da = dla * dla_da
        db = dbeta * beta * (1.0 - beta)
        lane = lax.broadcasted_iota(jnp.int32, (C, HD), 1)
        dab = jnp.where(lane == h, da, 0.0) + jnp.where(lane == nh + h, db, 0.0)
        lane1 = lax.broadcasted_iota(jnp.int32, (1, HD), 1)
        d_alog = jnp.where(lane1 == h, jnp.sum(dla * la, axis=0, keepdims=True), 0.0)
        d_dtb = jnp.where(lane1 == h, jnp.sum(da, axis=0, keepdims=True), 0.0)
        new_dst = dst * egl + _tn(_bf(do_), _bf(qd)) - _tn(_bf(dvn), _bf(w))
        return dab, d_alog, d_dtb, new_dst, dq, dk, dv

    def body(*refs):
        c, hg = pl.program_id(0), pl.program_id(1)
        step = c * ng + hg
        ins, outs, scratch, comm_begin, comm_end = _comm_hooks(
            comm, refs, 9, 5, step == 0, step == (3 * nc * ng) // 4, step == nc * ng - 1)
        dq_ref, dk_ref, dv_ref, dab_ref, dpar_ref = outs
        ds_sc, = scratch
        comm_begin()

        @pl.when(c == 0)
        def _():
            for hh in range(hp):
                ds_sc[hg * hp + hh] = jnp.zeros((HD, HD), F32)

        @pl.when(step == 0)
        def _():
            dpar_ref[...] = jnp.zeros_like(dpar_ref)

        @pl.when(hg == 0)
        def _():
            dab_ref[...] = jnp.zeros_like(dab_ref)

        dsts = [ds_sc[hg * hp + hh] for hh in range(hp)]
        res = _interleave([one_head(hg * hp + hh, hh, dsts[hh], *ins) for hh in range(hp)])
        for hh in range(hp):
            sl = slice(hh * HD, (hh + 1) * HD)
            ds_sc[hg * hp + hh] = res[hh][3]
            dq_ref[:, sl], dk_ref[:, sl], dv_ref[:, sl] = res[hh][4], res[hh][5], res[hh][6]
        dab_ref[...] += sum(r[0] for r in res[1:]) + res[0][0]
        dpar_ref[0:1, :] += sum(r[1] for r in res[1:]) + res[0][1]
        dpar_ref[1:2, :] += sum(r[2] for r in res[1:]) + res[0][2]
        comm_end()

    rblk = lambda off: pl.BlockSpec((C, hp * HD), lambda c, g: (nc - 1 - c, off // hp + g))
    oblk = pl.BlockSpec((C, hp * HD), lambda c, g: (nc - 1 - c, g))
    vec = pl.BlockSpec((1, HD), lambda c, g: (0, 0))
    cn = comm.n if comm is not None else 0
    return pl.pallas_call(
        body, name=name, grid=(nc, ng),
        in_specs=[rblk(0), rblk(nh), rblk(2 * nh),
                  pl.BlockSpec((C, HD), lambda c, g: (nc - 1 - c, ab_blk)), vec, vec, oblk,
                  pl.BlockSpec((1, hp,C, C), lambda c, g: (nc - 1 - c, g, 0, 0)),
                  pl.BlockSpec((1, hp,HD, HD), lambda c, g: (nc - 1 - c, g, 0, 0))] + [ANY] * cn,
        out_specs=[oblk, oblk, oblk,
                   pl.BlockSpec((C, HD), lambda c, g: (nc - 1 - c, 0)),
                   pl.BlockSpec((8, HD), lambda c, g: (0, 0))] + [ANY] * cn,
        out_shape=[jax.ShapeDtypeStruct((t, nh * HD), F32)] * 3
        + [jax.ShapeDtypeStruct((t, HD), F32), jax.ShapeDtypeStruct((8, HD), F32)]
        + (comm.out_shapes() if cn else []),
        scratch_shapes=[pltpu.VMEM((nh, HD, HD), F32)] + (comm.scratch() if cn else []),
        compiler_params=_cp(("arbitrary", "arbitrary")),
    )(qkv, qkv, qkv, proj, alog, dtb, do, x_sv, st_sv, *(comm.arrays if cn else []))


def _ada_fwd(c_all, w, b, name):
    nb, d = c_all.shape
    n = w.shape[1]
    tn = _pick(n, 512)

    def body(c_ref, w_ref, b_ref, o_ref):
        cv = c_ref[...]
        o_ref[...] = _nn(cv * _sigmoid(cv), w_ref[...], HI) + b_ref[...]

    return pl.pallas_call(
        body, name=name, grid=(n // tn,),
        in_specs=[pl.BlockSpec((nb, d), lambda j: (0, 0)), pl.BlockSpec((d, tn), lambda j: (0, j)),
                  pl.BlockSpec((1, tn), lambda j: (0, j))],
        out_specs=pl.BlockSpec((nb, tn), lambda j: (0, j)),
        out_shape=jax.ShapeDtypeStruct((nb, n), F32),
        compiler_params=_cp(("parallel",)),
    )(c_all, w, b)


def _ada_wgrad(c_all, dmod, name):
    nb, d = c_all.shape
    n = dmod.shape[1]
    tn = _pick(n, 512)

    def body(c_ref, g_ref, o_ref):
        cv = c_ref[...]
        o_ref[...] = _tn(cv * _sigmoid(cv), g_ref[...], HI)

    return pl.pallas_call(
        body, name=name, grid=(n // tn,),
        in_specs=[pl.BlockSpec((nb, d), lambda j: (0, 0)), pl.BlockSpec((nb, tn), lambda j: (0, j))],
        out_specs=pl.BlockSpec((d, tn), lambda j: (0, j)),
        out_shape=jax.ShapeDtypeStruct((d, n), F32),
        compiler_params=_cp(("parallel",)),
    )(c_all, dmod)


def _adamw(w, m, v, g, name, parts=False):
    lead = w.ndim == 3
    r, cdim = w.shape[-2:]
    cap = max(SUBLANES, ADAM_BLOCK_ELEMS // cdim // SUBLANES * SUBLANES)
    tr = r if r <= cap else _pick_rows(r, cap)
    bc1 = 1.0 - ADAM_B1 ** ADAM_STEP
    bc2 = 1.0 - ADAM_B2 ** ADAM_STEP

    glist = list(g) if isinstance(g, (list, tuple)) else [g]
    bounds = [0]
    for ga in glist:
        bounds.append(bounds[-1] + ga.shape[-2] // tr)

    def body(w_ref, m_ref, v_ref, *rest):
        g_refs, (go_ref, d_ref, mo_ref, vo_ref) = rest[:len(glist)], rest[len(glist):]
        if parts:
            sums = []
            for g_ref in g_refs:
                gv = g_ref[0].astype(F32)
                for s in range(1, N_DEV):
                    gv = gv + g_ref[s].astype(F32)
                sums.append(gv)
            gv = sums[-1]
            for p in range(len(sums) - 2, -1, -1):
                gv = jnp.where(pl.program_id(0) < bounds[p + 1], sums[p], gv)
        else:
            gv = g_refs[0][...]
        wv = w_ref[...]
        mn = ADAM_B1 * m_ref[...] + (1.0 - ADAM_B1) * gv
        vn = ADAM_B2 * v_ref[...] + (1.0 - ADAM_B2) * (gv * gv)
        m_hat = mn / bc1
        v_hat = vn / bc2
        go_ref[...] = gv
        d_ref[...] = -ADAM_LR * (m_hat / (jnp.sqrt(v_hat) + ADAM_EPS) + ADAM_WD * wv)
        mo_ref[...] = mn
        vo_ref[...] = vn

    flat = pl.BlockSpec((tr, cdim), lambda i: (i, 0))
    spec = pl.BlockSpec((None, tr, cdim), lambda i: (0, i, 0)) if lead else flat
    def piece_spec(p):
        lo, n = bounds[p], bounds[p + 1] - bounds[p]
        return pl.BlockSpec((N_DEV, tr, cdim), lambda i: (0, jnp.clip(i - lo, 0, n - 1), 0))

    gspecs = [piece_spec(p) for p in range(len(glist))] if parts else [flat]
    return pl.pallas_call(
        body, name=name, grid=(r // tr,),
        in_specs=[spec, spec, spec] + gspecs,
        out_specs=[spec] * 4,
        out_shape=[jax.ShapeDtypeStruct(w.shape, F32)] * 4,
        compiler_params=_cp(("arbitrary",)),
    )(w, m, v, *glist)


def _pick_rows(r, pref):
    t = pref
    while r % t:
        t -= 8
    assert t > 0
    return t


def _dev_index(x, y, c):
    return 4 * x + 2 * y + c


class _Comm:
    def __init__(self, kind, arrays):
        self.kind, self.arrays, self.n = kind, list(arrays), len(arrays)

    def out_shapes(self):
        if self.kind == "gather":
            return [jax.ShapeDtypeStruct((N_DEV,) + a.shape, a.dtype) for a in self.arrays]
        return [jax.ShapeDtypeStruct(a.shape, a.dtype) for a in self.arrays]

    def scratch(self):
        return [pltpu.SemaphoreType.DMA((self.n, 7)), pltpu.SemaphoreType.DMA((self.n, 7)),
                pltpu.SemaphoreType.DMA((self.n,))]

    def _gather_parts(self, ins, outs, sems):
        send_sems, recv_sems, local_sems = sems
        x, y, c = lax.axis_index("x"), lax.axis_index("y"), lax.axis_index("c")
        me, sibling = (x, y, c), (x, y, 1 - c)
        chips = [(1 - x, y), (x, 1 - y), (1 - x, 1 - y)]

        def copy(a, k, block, to, src=None):
            slot = outs[a].at[_dev_index(*block)]
            return pltpu.make_async_remote_copy(
                src_ref=slot if src is None else src, dst_ref=slot,
                send_sem=send_sems.at[a, k], recv_sem=recv_sems.at[a, k],
                device_id=to, device_id_type=MESH)

        n = self.n
        mine = [pltpu.make_async_copy(ins[a], outs[a].at[_dev_index(*me)], local_sems.at[a]) for a in range(n)]
        first = []
        for a in range(n):
            first.append(copy(a, 0, me, sibling, src=ins[a]))
            first += [copy(a, 1 + j, me, (*chip, c), src=ins[a]) for j, chip in enumerate(chips)]
        landed = [copy(a, 1 + j, (*chip, c), me) for j, chip in enumerate(chips) for a in range(n)]
        passed = [copy(a, 4 + j, (*chip, c), sibling) for j, chip in enumerate(chips) for a in range(n)]
        late = []
        for a in range(n):
            late.append(copy(a, 0, sibling, me))
            late += [copy(a, 4 + j, (*chip, 1 - c), me) for j, chip in enumerate(chips)]
        return mine, first, landed, passed, late

    def _exchange_parts(self, ins, outs, sems):
        send_sems, recv_sems, local_sems = sems
        x, y, c = lax.axis_index("x"), lax.axis_index("y"), lax.axis_index("c")
        my = _dev_index(x, y, c)
        n = self.n
        mine = [pltpu.make_async_copy(ins[a].at[my], outs[a].at[my], local_sems.at[a]) for a in range(n)]
        sends, recvs = [], []
        for k in range(1, N_DEV):
            px = (1 - x) if (k >> 2) & 1 else x
            py = (1 - y) if (k >> 1) & 1 else y
            pc = (1 - c) if k & 1 else c
            peer = _dev_index(px, py, pc)
            for a in range(n):
                sends.append(pltpu.make_async_remote_copy(
                    src_ref=ins[a].at[peer], dst_ref=outs[a].at[my],
                    send_sem=send_sems.at[a, k - 1], recv_sem=recv_sems.at[a, k - 1],
                    device_id=(px, py, pc), device_id_type=MESH))
                recvs.append(pltpu.make_async_remote_copy(
                    src_ref=ins[a].at[my], dst_ref=outs[a].at[peer],
                    send_sem=send_sems.at[a, k - 1], recv_sem=recv_sems.at[a, k - 1],
                    device_id=(x, y, c), device_id_type=MESH))
        return mine, sends, recvs

    def start(self, ins, outs, sems):
        if self.kind == "gather":
            mine, first, _, _, _ = self._gather_parts(ins, outs, sems)
        else:
            mine, first, _ = self._exchange_parts(ins, outs, sems)
        for cp in mine + first:
            cp.start()

    def mid(self, ins, outs, sems):
        if self.kind == "gather":
            _, _, landed, passed, _ = self._gather_parts(ins, outs, sems)
            for got, fwd in zip(landed, passed):
                got.wait_recv()
                fwd.start()

    def finish(self, ins, outs, sems):
        if self.kind == "gather":
            mine, first, _, passed, late = self._gather_parts(ins, outs, sems)
            for cp in late:
                cp.wait_recv()
            for cp in first + passed:
                cp.wait_send()
        else:
            mine, sends, recvs = self._exchange_parts(ins, outs, sems)
            for cp in sends:
                cp.wait_send()
            for cp in recvs:
                cp.wait_recv()
        for cp in mine:
            cp.wait()

    def run(self, name):
        n = self.n

        def body(*refs):
            ins, outs, sems = refs[:n], refs[n:2 * n], refs[2 * n:]
            self.start(ins, outs, sems)
            self.mid(ins, outs, sems)
            self.finish(ins, outs, sems)

        return pl.pallas_call(
            body, name=name, in_specs=[ANY] * n, out_specs=[ANY] * n,
            out_shape=self.out_shapes(), scratch_shapes=self.scratch(),
        )(*self.arrays)


def _all_gather(arrays, name):
    return _Comm("gather", arrays).run(name)


def _comm_hooks(comm, refs, n_in, n_out, first, middle, last):
    cn = comm.n if comm is not None else 0
    ins, cins = refs[:n_in], refs[n_in:n_in + cn]
    outs, couts = refs[n_in + cn:n_in + cn + n_out], refs[n_in + cn + n_out:n_in + 2 * cn + n_out]
    rest = refs[n_in + 2 * cn + n_out:]
    scratch, csems = (rest[:len(rest) - 3], rest[len(rest) - 3:]) if cn else (rest, ())

    def begin():
        if cn:
            pl.when(first)(lambda: comm.start(cins, couts, csems))
            pl.when(middle)(lambda: comm.mid(cins, couts, csems))

    def end():
        if cn:
            pl.when(last)(lambda: comm.finish(cins, couts, csems))

    return ins, outs, scratch, begin, end


def _local_step(x, tgt, mod, n1, n2, n3, n4, w_in_p, lb_logits, hg_norm, conv_w, alog, dtb, gdn_norm,
                late_w, dist=None):
    t, d = x.shape
    nh = d // 2 // HD
    ab_blk = 8 * nh
    sh_m, sc_m, gt_m, sh_f, sc_f, gt_f = [mod[i:i + 1] for i in range(6)]

    h1, r1 = _prenorm(x, n1, sc_m, sh_m, "prenorm_mix")
    if dist is None:
        proj = _mm(h1, w_in_p, "nn", [F32], "mm_proj")
        o_hg, a_sv, hst_sv = _hgrn2_fwd(proj, lb_logits, nh, "hgrn2_fwd")
        qkv = _gdn_prep(proj, conv_w, 4 * nh, nh, "gdn_prep")
        o_gd, x_sv, gst_sv = _gdn_fwd(qkv, proj, ab_blk, alog, dtb, nh, "gdn_fwd")
        w_out, w_ff1, w_ff2 = late_w
        exch = lambda arrays: None
    else:
        proj, g_ff2 = _mm(h1, w_in_p, "nn", [F32], "mm_proj", comm=_Comm("gather", late_w[2:]))
        o_hg, a_sv, hst_sv, g_out = _hgrn2_fwd(proj, lb_logits, nh, "hgrn2_fwd",
                                               comm=_Comm("gather", late_w[:1]))
        qkv = _gdn_prep(proj, conv_w, 4 * nh, nh, "gdn_prep")
        o_gd, x_sv, gst_sv, g_ff1 = _gdn_fwd(qkv, proj, ab_blk, alog, dtb, nh, "gdn_fwd",
                                             comm=_Comm("gather", late_w[1:2]))
        w_out, w_ff1, w_ff2 = dist["assemble"](g_out, g_ff1, g_ff2)
        exch = lambda arrays: _Comm("exchange", arrays)
    om_hg = _headnorm_fwd(o_hg, proj, 3 * nh, hg_norm, "headnorm_hg")
    om_gd = _headnorm_fwd(o_gd, proj, 7 * nh, gdn_norm, "headnorm_gdn")
    om = jnp.concatenate([om_hg, om_gd], axis=1)
    y1 = _mm(om, w_out, "nn", [F32], "mm_out")
    x1, r2 = _postnorm_res(x, y1, n2, gt_m, "postnorm_mix")
    h2, r3 = _prenorm(x1, n3, sc_f, sh_f, "prenorm_ffn")

    def relu2(acc, extra, outs):
        outs[0][...] = acc
        rl = jnp.maximum(acc, 0.0)
        outs[1][...] = (rl * rl).astype(BF16)

    u, act = _mm(h2, w_ff1, "nn", [F32, BF16], "mm_ff1", epilogue=relu2)
    y2 = _mm(act, w_ff2, "nn", [F32], "mm_ff2")
    dout, r4, loss = _final_loss(x1, y2, n4, gt_f, tgt, "final_loss")

    dy2, dgt_f, dn4 = _postnorm_bwd(dout, y2, r4, n4, gt_f, "postnorm_ffn_bwd")
    dw_ff2 = _mm(act, dy2, "tn", [BF16], "mm_dw_ff2")

    def drelu2(acc, extra, outs):
        outs[0][...] = (acc * (2.0 * jnp.maximum(extra[0][...], 0.0))).astype(BF16)

    recv = {}
    ff2a, ff2b = dist["parts_ff2"](dw_ff2) if dist else (None, None)
    du, *recv["ff2a"] = _listed(_mm(dy2, w_ff2, "nt", [BF16], "mm_da", epilogue=drelu2, extras=(u,),
                                    comm=exch([ff2a])))
    ff1_cols = dict(by_cols=True, tn=dist["n_ff"]) if dist else {}
    dw_ff1, *recv["ff2b"] = _listed(_mm(h2, du, "tn", [BF16], "mm_dw_ff1", comm=exch([ff2b]), **ff1_cols))
    ff1a, ff1b = dist["parts_ff1"](dw_ff1) if dist else (None, None)
    dh2, *recv["ff1a"] = _listed(_mm(du, w_ff1, "nt", [F32], "mm_dh2", comm=exch([ff1a])))
    dx1, dsh_f, dsc_f, dn3 = _prenorm_bwd(dh2, x1, r3, n3, sc_f, dout, "prenorm_ffn_bwd")

    dy1, dgt_m, dn2 = _postnorm_bwd(dx1, y1, r2, n2, gt_m, "postnorm_mix_bwd")
    dw_out = _mm(om, dy1, "tn", [BF16], "mm_dw_out")
    dom = _mm(dy1, w_out, "nt", [F32], "mm_dom")
    do_hg, dg_hg, dhgn = _headnorm_bwd(dom, 0, o_hg, proj, 3 * nh, hg_norm, "headnorm_hg_bwd")
    do_gd, dg_gd, dgdn = _headnorm_bwd(dom, 1, o_gd, proj, 7 * nh, gdn_norm, "headnorm_gdn_bwd")
    p_out = dist["parts_out"](dw_out) if dist else None
    dq_hg, df_hg, di_hg, dl0, *recv["ff1b_out"] = _hgrn2_bwd(proj, lb_logits, do_hg, a_sv, hst_sv, nh,
                                                             "hgrn2_bwd", comm=exch([ff1b, p_out]))
    dq_g, dk_g, dv_g, dab, dpar = _gdn_bwd(qkv, proj, ab_blk, alog, dtb, do_gd, x_sv, gst_sv, nh, "gdn_bwd")
    dqkv = jnp.concatenate([dq_g, dk_g, dv_g], axis=1)
    du_conv, dconv = _gdn_prep_bwd(proj, conv_w, dqkv, 4 * nh, nh, "gdn_prep_bwd")
    dproj = jnp.concatenate([dq_hg, df_hg, di_hg, dg_hg, du_conv, dg_gd, dab.astype(BF16)], axis=1)
    if dist is None:
        dw_in = _mm(h1, dproj, "tn", [BF16], "mm_dw_in")
        dh1 = _mm(dproj, w_in_p, "nt", [F32], "mm_dh1", tk=1664)
    else:
        dw_in_a = _mm(h1[:, :d // 2], dproj, "tn", [BF16], "mm_dw_in_a")
        dw_in_b, in_a = _mm(h1[:, d // 2:], dproj, "tn", [BF16], "mm_dw_in_b",
                            comm=exch([dist["parts_in"](dw_in_a)]))
        dh1, in_b = _mm(dproj, w_in_p, "nt", [F32], "mm_dh1", tk=1664, comm=exch([dist["parts_in"](dw_in_b)]))
        recv["in"] = [in_a, in_b]
        dw_in = None
    dx, dsh_m, dsc_m, dn1 = _prenorm_bwd(dh1, x, r1, n1, sc_m, dx1, "prenorm_mix_bwd")

    dmod = jnp.concatenate([dsh_m, dsc_m, dgt_m, dsh_f, dsc_f, dgt_f], axis=0)
    grads = dict(dmod=dmod, n1=dn1, n2=dn2, n3=dn3, n4=dn4, w_in=dw_in, lb0=dl0, hg_norm=dhgn, conv=dconv,
                 alog=dpar[0:1], dtb=dpar[1:2], gdn_norm=dgdn, w_out=dw_out, w_ff1=dw_ff1, w_ff2=dw_ff2,
                 recv=recv)
    return loss, dx, grads


def _pack(vals):
    rows = []
    for vv in vals:
        flat = vv.reshape(-1)
        flat = jnp.pad(flat, (0, (-flat.shape[0]) % (SUBLANES * LANES)))
        rows.append(flat.reshape(-1, LANES))
    return jnp.concatenate(rows, axis=0)


def _unpack(packed, shapes):
    out, r = [], 0
    for shp in shapes:
        size = 1
        for s in shp:
            size *= s
        nr = -(-size // (SUBLANES * LANES)) * SUBLANES
        out.append(packed[r:r + nr].reshape(-1)[:size].reshape(shp))
        r += nr
    return out


def _sum_parts(parts, name):
    _, r, cdim = parts.shape

    def body(p_ref, o_ref):
        acc = p_ref[0]
        for s in range(1, N_DEV):
            acc = acc + p_ref[s]
        o_ref[...] = acc

    return pl.pallas_call(
        body, name=name,
        out_shape=jax.ShapeDtypeStruct((r, cdim), F32),
        compiler_params=_cp(),
    )(parts)


def kernel(x, c, w_ada, b_ada, pre_mix_norm, post_mix_norm, pre_ffn_norm, post_ffn_norm, w_in, hg_lb_logits, hg_norm, gdn_conv_w, gdn_a_log, gdn_dt_bias, gdn_norm, w_out, w_ff1, w_ff2, loss_target, m_w_ada, m_b_ada, m_pre_mix_norm, m_post_mix_norm, m_pre_ffn_norm, m_post_ffn_norm, m_w_in, m_hg_lb_logits, m_hg_norm, m_gdn_conv_w, m_gdn_a_log, m_gdn_dt_bias, m_gdn_norm, m_w_out, m_w_ff1, m_w_ff2, v_w_ada, v_b_ada, v_pre_mix_norm, v_post_mix_norm, v_pre_ffn_norm, v_post_ffn_norm, v_w_in, v_hg_lb_logits, v_hg_norm, v_gdn_conv_w, v_gdn_a_log, v_gdn_dt_bias, v_gdn_norm, v_w_out, v_w_ff1, v_w_ff2):
    t, d = x.shape[1], x.shape[2]
    nh = d // 2 // HD
    in_cols = w_in.shape[2] * N_DEV
    main = in_cols - 2 * nh
    me = _dev_index(lax.axis_index("x"), lax.axis_index("y"), lax.axis_index("c"))

    c_all, conv_g = _all_gather([c, gdn_conv_w[0]], "gather_small")
    c_all = c_all.reshape(N_DEV, d)
    conv_full = conv_g.transpose(1, 0, 2).reshape(CONV_K, -1)
    w_in_g = _all_gather([w_in[0].astype(BF16)], "gather_w_in")[0]
    w_in_full = w_in_g.transpose(1, 0, 2).reshape(d, in_cols)
    w_in_p = jnp.concatenate([w_in_full, jnp.zeros((d, LANES - 2 * nh), BF16)], axis=1)
    late_w = [w_out[0].astype(BF16), w_ff1[0].astype(BF16), w_ff2[0].astype(BF16)]

    n_in = w_in.shape[2]
    n_ff = w_ff1.shape[2]

    def halves(p):
        r = p.shape[1] // 2
        return p[:, :r], p[:, r:]

    dist = dict(
        assemble=lambda g_out, g_ff1, g_ff2: (g_out.reshape(d, d), g_ff1.transpose(1, 0, 2).reshape(d, -1),
                                              g_ff2.reshape(-1, d)),
        n_ff=n_ff,
        parts_ff2=lambda dw: halves(dw.reshape(N_DEV, -1, d)),
        parts_ff1=halves,
        parts_out=lambda dw: dw.reshape(N_DEV, d // N_DEV, d),
        parts_in=lambda dw: dw[:, :in_cols].reshape(dw.shape[0], N_DEV, n_in).transpose(1, 0, 2),
    )

    n_ada = w_ada.shape[2]
    b_loc = lax.dynamic_slice(b_ada, (0, me * n_ada), (1, n_ada))
    mod_part = _ada_fwd(c_all, w_ada[0], b_loc, "ada_fwd")
    mod_all = _all_gather([mod_part], "gather_mod")[0]
    mod = lax.dynamic_slice(mod_all, (0, me, 0), (N_DEV, 1, n_ada)).reshape(6, d)

    pad_lane = lambda vv: jnp.concatenate([vv, jnp.zeros((1, LANES - vv.shape[1]), F32)], axis=1)
    loss, dx, g = _local_step(
        x[0], loss_target[0], mod, pre_mix_norm, post_mix_norm, pre_ffn_norm, post_ffn_norm, w_in_p,
        hg_lb_logits, hg_norm, conv_full, pad_lane(gdn_a_log), pad_lane(gdn_dt_bias), gdn_norm,
        late_w, dist)

    rep_names = ["b_ada", "n1", "n2", "n3", "n4", "lb", "hg_norm", "alog", "dtb", "gdn_norm"]
    rep_w = [b_ada, pre_mix_norm, post_mix_norm, pre_ffn_norm, post_ffn_norm, hg_lb_logits, hg_norm,
             gdn_a_log, gdn_dt_bias, gdn_norm]
    rep_m = [m_b_ada, m_pre_mix_norm, m_post_mix_norm, m_pre_ffn_norm, m_post_ffn_norm, m_hg_lb_logits,
             m_hg_norm, m_gdn_a_log, m_gdn_dt_bias, m_gdn_norm]
    rep_v = [v_b_ada, v_pre_mix_norm, v_post_mix_norm, v_pre_ffn_norm, v_post_ffn_norm, v_hg_lb_logits,
             v_hg_norm, v_gdn_a_log, v_gdn_dt_bias, v_gdn_norm]
    rep_shapes = [a.shape for a in rep_w]
    g_lb = jnp.stack([g["lb0"], -g["lb0"]], axis=0)
    rep_g = [g["dmod"], g["n1"], g["n2"], g["n3"], g["n4"], g_lb, g["hg_norm"],
             g["alog"][:, :nh], g["dtb"][:, :nh], g["gdn_norm"]]
    small = _pack(rep_g + [g["conv"]])
    n_rep_rows = _pack(rep_g).shape[0]
    pad_rows = (-small.shape[0]) % 8
    if pad_rows:
        small = jnp.concatenate([small, jnp.zeros((pad_rows, LANES), F32)], axis=0)
    small_all = _all_gather([small], "gather_small_grads")[0]
    small_sum = _sum_parts(small_all, "sum_small_grads")
    rep_out = _adamw(_pack(rep_w), _pack(rep_m), _pack(rep_v), small_sum[:n_rep_rows], "adamw_small")
    rep_g_o, rep_d_o, rep_m_o, rep_v_o = [dict(zip(rep_names, _unpack(p, rep_shapes))) for p in rep_out]

    conv_sum = small_sum[n_rep_rows:n_rep_rows + CONV_K * conv_full.shape[1] // LANES].reshape(CONV_K, -1)
    n_conv = gdn_conv_w.shape[2]
    conv_loc = lax.dynamic_slice(conv_sum, (0, me * n_conv), (CONV_K, n_conv))
    conv_o = _adamw(gdn_conv_w, m_gdn_conv_w, v_gdn_conv_w, conv_loc, "adamw_conv")

    dmod_all = small_all[:, :6 * d // LANES, :].reshape(N_DEV, 6 * d)
    dmod_loc = lax.dynamic_slice(dmod_all, (0, me * n_ada), (N_DEV, n_ada))
    g_ada = _ada_wgrad(c_all, dmod_loc, "ada_wgrad")
    ada_o = _adamw(w_ada, m_w_ada, v_w_ada, g_ada, "adamw_ada")

    rc = g["recv"]
    r_ff2 = [rc["ff2a"][0], rc["ff2b"][0]]
    r_ff1 = [rc["ff1a"][0], rc["ff1b_out"][0]]
    r_out, r_in = rc["ff1b_out"][1], rc["in"]
    in_o = _adamw(w_in, m_w_in, v_w_in, r_in, "adamw_w_in", parts=True)
    out_o = _adamw(w_out, m_w_out, v_w_out, r_out, "adamw_w_out", parts=True)
    ff1_o = _adamw(w_ff1, m_w_ff1, v_w_ff1, r_ff1, "adamw_w_ff1", parts=True)
    ff2_o = _adamw(w_ff2, m_w_ff2, v_w_ff2, r_ff2, "adamw_w_ff2", parts=True)

    loss_tot = lax.psum(loss[0, 0], ("x", "y", "c"))

    def leaf(kind):
        return [ada_o[kind], rep_out_d[kind]["b_ada"], rep_out_d[kind]["n1"], rep_out_d[kind]["n2"],
                rep_out_d[kind]["n3"], rep_out_d[kind]["n4"], in_o[kind], rep_out_d[kind]["lb"],
                rep_out_d[kind]["hg_norm"], conv_o[kind], rep_out_d[kind]["alog"], rep_out_d[kind]["dtb"],
                rep_out_d[kind]["gdn_norm"], out_o[kind], ff1_o[kind], ff2_o[kind]]

    rep_out_d = [rep_g_o, rep_d_o, rep_m_o, rep_v_o]
    return (loss_tot, dx[None], *leaf(0), *leaf(1), *leaf(2), *leaf(3))
```

```python
import functools

import jax
import jax.numpy as jnp
from jax import lax
from jax.experimental import pallas as pl
from jax.experimental.pallas import tpu as pltpu

F32 = jnp.float32
BF16 = jnp.bfloat16
HI = lax.Precision.HIGHEST
HIGH = lax.Precision.HIGH

EPS = 1e-6
CHUNK = 64
SB = 16
NSB = CHUNK // SB
HP = 8
HD = 128
CONV_K = 4
N_DEV = 8
LANES = 128
SUBLANES = 8
VMEM_LIMIT = 56 * 1024 * 1024

ADAM_BLOCK_ELEMS = 256 * 1024
ADAM_LR = 0.001
ADAM_B1 = 0.9
ADAM_B2 = 0.999
ADAM_EPS = 1e-08
ADAM_WD = 0.01
ADAM_STEP = 10

ANY = pl.BlockSpec(memory_space=pl.ANY)
MESH = pl.DeviceIdType.MESH


def _cp(sem=None):
    return pltpu.CompilerParams(dimension_semantics=sem, vmem_limit_bytes=VMEM_LIMIT)


def _dot(a, b, dims, precision=None):
    return lax.dot_general(a, b, (dims, ((), ())), precision=precision, preferred_element_type=F32)


def _nn(a, b, precision=None):
    return _dot(a, b, ((1,), (0,)), precision)


def _nt(a, b, precision=None):
    return _dot(a, b, ((1,), (1,)), precision)


def _tn(a, b, precision=None):
    return _dot(a, b, ((0,), (0,)), precision)


def _bf(x):
    return x.astype(BF16)


def _sigmoid(x):
    return 1.0 / (1.0 + jnp.exp(-x))


def _interleave(gens):
    results = [None] * len(gens)
    live = list(range(len(gens)))
    while live:
        for i in list(live):
            try:
                next(gens[i])
            except StopIteration as stop:
                results[i] = stop.value
                live.remove(i)
    return results


def _listed(res):
    return list(res) if isinstance(res, (list, tuple)) else [res]


def _pick(n, pref):
    if n <= pref:
        return n
    t = pref
    while n % t:
        t -= LANES
    assert t > 0, (n, pref)
    return t


def _mm(a, b, mode, out_dtypes, name, epilogue=None, extras=(), tm=1024, tn=2048, tk=1024, comm=None,
        by_cols=False):
    if mode == "nn":
        (m, kd), (_, n) = a.shape, b.shape
    elif mode == "nt":
        (m, kd), (n, _) = a.shape, b.shape
    else:
        (kd, m), (_, n) = a.shape, b.shape
    tm, tn, tk = _pick(m, tm), _pick(n, tn), _pick(kd, tk)
    nk = kd // tk
    if mode == "nn":
        a_spec = pl.BlockSpec((tm, tk), lambda i, j, k: (i, k))
        b_spec = pl.BlockSpec((tk, tn), lambda i, j, k: (k, j))
        dims = ((1,), (0,))
    elif mode == "nt":
        a_spec = pl.BlockSpec((tm, tk), lambda i, j, k: (i, k))
        b_spec = pl.BlockSpec((tn, tk), lambda i, j, k: (j, k))
        dims = ((1,), (1,))
    else:
        a_spec = pl.BlockSpec((tk, tm), lambda i, j, k: (k, i))
        b_spec = pl.BlockSpec((tk, tn), lambda i, j, k: (k, j))
        dims = ((0,), (0,))
    o_spec = pl.BlockSpec((tm, tn), lambda i, j, k: (i, j))
    if by_cols:
        assert epilogue is None and not extras
        res_spec = pl.BlockSpec((None, tm, tn), lambda i, j, k: (j, i, 0))
        res_shape = (n // tn, m, tn)
    else:
        res_spec, res_shape = o_spec, (m, n)
    n_extra, n_out = len(extras), len(out_dtypes)

    gm, gn = m // tm, n // tn
    cn = comm.n if comm is not None else 0

    def body(*refs):
        i, j, k = pl.program_id(0), pl.program_id(1), pl.program_id(2)
        at0 = (j == 0) & (k == 0)
        ins, out_refs, scratch, comm_begin, comm_end = _comm_hooks(
            comm, refs, 2 + n_extra, n_out, (i == 0) & at0, (i == gm - 1) & at0,
            (i == gm - 1) & (j == gn - 1) & (k == nk - 1))
        a_ref, b_ref, extra_refs = ins[0], ins[1], ins[2:]
        acc, = scratch
        comm_begin()

        @pl.when(k == 0)
        def _():
            acc[...] = jnp.zeros_like(acc)

        acc[...] += _dot(a_ref[...], b_ref[...], dims)

        @pl.when(k == nk - 1)
        def _():
            if epilogue is None:
                out_refs[0][...] = acc[...].astype(out_dtypes[0])
            else:
                epilogue(acc[...], extra_refs, out_refs)

        comm_end()

    sem = ("arbitrary",) * 3 if cn else ("parallel", "parallel", "arbitrary")
    outs = pl.pallas_call(
        body, name=name,
        grid=(gm, gn, nk),
        in_specs=[a_spec, b_spec] + [o_spec] * n_extra + [ANY] * cn,
        out_specs=[res_spec] * n_out + [ANY] * cn,
        out_shape=[jax.ShapeDtypeStruct(res_shape, dt) for dt in out_dtypes] + (comm.out_shapes() if cn else []),
        scratch_shapes=[pltpu.VMEM((tm, tn), F32)] + (comm.scratch() if cn else []),
        compiler_params=_cp(sem),
    )(a, b, *extras, *(comm.arrays if cn else []))
    return outs[0] if n_out + cn == 1 else outs


def _row_spec(tb, d):
    return pl.BlockSpec((tb, d), lambda i: (i, 0))


def _vec_spec(d):
    return pl.BlockSpec((1, d), lambda i: (0, 0))


def _prenorm(x, w, sc, sh, name):
    t, d = x.shape
    tb = _pick(t, 256)

    def body(x_ref, w_ref, sc_ref, sh_ref, h_ref, r_ref):
        xv = x_ref[...]
        r = lax.rsqrt(jnp.mean(xv * xv, axis=-1, keepdims=True) + EPS)
        h_ref[...] = ((xv * r * w_ref[...]) * (1.0 + sc_ref[...]) + sh_ref[...]).astype(BF16)
        r_ref[...] = r

    return pl.pallas_call(
        body, name=name, grid=(t // tb,),
        in_specs=[_row_spec(tb, d), _vec_spec(d), _vec_spec(d), _vec_spec(d)],
        out_specs=[_row_spec(tb, d), _row_spec(tb, 1)],
        out_shape=[jax.ShapeDtypeStruct((t, d), BF16), jax.ShapeDtypeStruct((t, 1), F32)],
        compiler_params=_cp(("parallel",)),
    )(x, w, sc, sh)


def _postnorm_res(x, y, w, gt, name):
    t, d = x.shape
    tb = _pick(t, 256)

    def body(x_ref, y_ref, w_ref, gt_ref, o_ref, r_ref):
        yv = y_ref[...]
        r = lax.rsqrt(jnp.mean(yv * yv, axis=-1, keepdims=True) + EPS)
        o_ref[...] = x_ref[...] + gt_ref[...] * (yv * r * w_ref[...])
        r_ref[...] = r

    return pl.pallas_call(
        body, name=name, grid=(t // tb,),
        in_specs=[_row_spec(tb, d), _row_spec(tb, d), _vec_spec(d), _vec_spec(d)],
        out_specs=[_row_spec(tb, d), _row_spec(tb, 1)],
        out_shape=[jax.ShapeDtypeStruct((t, d), F32), jax.ShapeDtypeStruct((t, 1), F32)],
        compiler_params=_cp(("parallel",)),
    )(x, y, w, gt)


def _final_loss(x, y, w, gt, tgt, name):
    t, d = x.shape
    tb = _pick(t, 256)

    def body(x_ref, y_ref, w_ref, gt_ref, tgt_ref, dout_ref, r_ref, loss_ref):
        @pl.when(pl.program_id(0) == 0)
        def _():
            loss_ref[...] = jnp.zeros_like(loss_ref)

        yv = y_ref[...]
        r = lax.rsqrt(jnp.mean(yv * yv, axis=-1, keepdims=True) + EPS)
        out = x_ref[...] + gt_ref[...] * (yv * r * w_ref[...])
        diff = out - tgt_ref[...]
        row = jnp.mean(diff * diff, axis=-1, keepdims=True)
        loss_ref[...] += 0.5 * jnp.sum(row, axis=0, keepdims=True)
        dout_ref[...] = diff * (1.0 / d)
        r_ref[...] = r

    return pl.pallas_call(
        body, name=name, grid=(t // tb,),
        in_specs=[_row_spec(tb, d), _row_spec(tb, d), _vec_spec(d), _vec_spec(d), _row_spec(tb, d)],
        out_specs=[_row_spec(tb, d), _row_spec(tb, 1), pl.BlockSpec((1, 1), lambda i: (0, 0))],
        out_shape=[jax.ShapeDtypeStruct((t, d), F32), jax.ShapeDtypeStruct((t, 1), F32),
                   jax.ShapeDtypeStruct((1, 1), F32)],
        compiler_params=_cp(("arbitrary",)),
    )(x, y, w, gt, tgt)


def _postnorm_bwd(dxn, y, r, w, gt, name):
    t, d = y.shape
    tb = _pick(t, 256)

    def body(dx_ref, y_ref, r_ref, w_ref, gt_ref, dy_ref, dgt_ref, dw_ref):
        @pl.when(pl.program_id(0) == 0)
        def _():
            dgt_ref[...] = jnp.zeros_like(dgt_ref)
            dw_ref[...] = jnp.zeros_like(dw_ref)

        dxv, rv, wv = dx_ref[...], r_ref[...], w_ref[...]
        z = y_ref[...] * rv
        dgt_ref[...] += jnp.sum(dxv * (z * wv), axis=0, keepdims=True)
        dn = dxv * gt_ref[...]
        dw_ref[...] += jnp.sum(dn * z, axis=0, keepdims=True)
        dz = dn * wv
        dy_ref[...] = (rv * (dz - z * jnp.mean(dz * z, axis=-1, keepdims=True))).astype(BF16)

    return pl.pallas_call(
        body, name=name, grid=(t // tb,),
        in_specs=[_row_spec(tb, d), _row_spec(tb, d), _row_spec(tb, 1), _vec_spec(d), _vec_spec(d)],
        out_specs=[_row_spec(tb, d), _vec_spec(d), _vec_spec(d)],
        out_shape=[jax.ShapeDtypeStruct((t, d), BF16), jax.ShapeDtypeStruct((1, d), F32),
                   jax.ShapeDtypeStruct((1, d), F32)],
        compiler_params=_cp(("arbitrary",)),
    )(dxn, y, r, w, gt)


def _prenorm_bwd(dh, x, r, w, sc, dres, name):
    t, d = x.shape
    tb = _pick(t, 256)

    def body(dh_ref, x_ref, r_ref, w_ref, sc_ref, dres_ref, dx_ref, dsh_ref, dsc_ref, dw_ref):
        @pl.when(pl.program_id(0) == 0)
        def _():
            dsh_ref[...] = jnp.zeros_like(dsh_ref)
            dsc_ref[...] = jnp.zeros_like(dsc_ref)
            dw_ref[...] = jnp.zeros_like(dw_ref)

        dhv, rv, wv = dh_ref[...], r_ref[...], w_ref[...]
        z = x_ref[...] * rv
        dsh_ref[...] += jnp.sum(dhv, axis=0, keepdims=True)
        dsc_ref[...] += jnp.sum(dhv * (z * wv), axis=0, keepdims=True)
        dzw = dhv * (1.0 + sc_ref[...])
        dw_ref[...] += jnp.sum(dzw * z, axis=0, keepdims=True)
        dz = dzw * wv
        dx_ref[...] = dres_ref[...] + rv * (dz - z * jnp.mean(dz * z, axis=-1, keepdims=True))

    return pl.pallas_call(
        body, name=name, grid=(t // tb,),
        in_specs=[_row_spec(tb, d), _row_spec(tb, d), _row_spec(tb, 1), _vec_spec(d), _vec_spec(d),
                  _row_spec(tb, d)],
        out_specs=[_row_spec(tb, d), _vec_spec(d), _vec_spec(d), _vec_spec(d)],
        out_shape=[jax.ShapeDtypeStruct((t, d), F32)] + [jax.ShapeDtypeStruct((1, d), F32)] * 3,
        compiler_params=_cp(("arbitrary",)),
    )(dh, x, r, w, sc, dres)


def _headnorm_fwd(o, proj, g_blk, nw, name):
    t, wd = o.shape
    nh = wd // HD
    tb = _pick(t, 512)
    gb = g_blk * HD // wd

    def body(o_ref, g_ref, nw_ref, out_ref):
        o3 = o_ref[...].reshape(tb, nh, HD)
        g3 = g_ref[...].reshape(tb, nh, HD)
        rh = lax.rsqrt(jnp.mean(o3 * o3, axis=-1, keepdims=True) + EPS)
        res = (o3 * rh * nw_ref[...].reshape(1, 1, HD)) * (g3 * _sigmoid(g3))
        out_ref[...] = res.reshape(tb, wd).astype(BF16)

    return pl.pallas_call(
        body, name=name, grid=(t // tb,),
        in_specs=[_row_spec(tb, wd), pl.BlockSpec((tb, wd), lambda i: (i, gb)), _vec_spec(HD)],
        out_specs=_row_spec(tb, wd),
        out_shape=jax.ShapeDtypeStruct((t, wd), BF16),
        compiler_params=_cp(("parallel",)),
    )(o, proj, nw)


def _headnorm_bwd(dom, col_blk, o, proj, g_blk, nw, name):
    t, wd = o.shape
    nh = wd // HD
    tb = _pick(t, 512)
    gb = g_blk * HD // wd

    def body(do_ref, o_ref, g_ref, nw_ref, dout_ref, dg_ref, dnw_ref):
        @pl.when(pl.program_id(0) == 0)
        def _():
            dnw_ref[...] = jnp.zeros_like(dnw_ref)

        dn = do_ref[...].reshape(tb, nh, HD)
        o3 = o_ref[...].reshape(tb, nh, HD)
        g3 = g_ref[...].reshape(tb, nh, HD)
        nw3 = nw_ref[...].reshape(1, 1, HD)
        rh = lax.rsqrt(jnp.mean(o3 * o3, axis=-1, keepdims=True) + EPS)
        z = o3 * rh
        sg = _sigmoid(g3)
        sl = g3 * sg
        dnw_ref[...] += jnp.sum(jnp.sum(dn * sl * z, axis=1), axis=0, keepdims=True)
        dg_ref[...] = (dn * (z * nw3) * (sg * (1.0 + g3 * (1.0 - sg)))).reshape(tb, wd).astype(BF16)
        dz = dn * sl * nw3
        dout_ref[...] = (rh * (dz - z * jnp.mean(dz * z, axis=-1, keepdims=True))).reshape(tb, wd)

    return pl.pallas_call(
        body, name=name, grid=(t // tb,),
        in_specs=[pl.BlockSpec((tb, wd), lambda i: (i, col_blk)), _row_spec(tb, wd),
                  pl.BlockSpec((tb, wd), lambda i: (i, gb)), _vec_spec(HD)],
        out_specs=[_row_spec(tb, wd), _row_spec(tb, wd), _vec_spec(HD)],
        out_shape=[jax.ShapeDtypeStruct((t, wd), F32), jax.ShapeDtypeStruct((t, wd), BF16),
                   jax.ShapeDtypeStruct((1, HD), F32)],
        compiler_params=_cp(("arbitrary",)),
    )(dom, o, proj, nw)


def _tri(n, kind):
    r = lax.broadcasted_iota(jnp.int32, (n, n), 0)
    c = lax.broadcasted_iota(jnp.int32, (n, n), 1)
    if kind == "lower":
        return r >= c
    if kind == "strict":
        return r > c
    return r <= c


def _hg_gate(fl, l0, l1):
    mx = jnp.maximum(l0, l1)
    e0, e1 = jnp.exp(l0 - mx), jnp.exp(l1 - mx)
    lb = e0 / (e0 + e1)
    sg = _sigmoid(fl)
    f = lb + (1.0 - lb) * sg
    return lb, sg, f


def _hgrn2_fwd(proj, lb_logits, nh, name, comm=None):
    t = proj.shape[0]
    nc = t // CHUNK
    C = CHUNK
    lg = lb_logits.reshape(2, nh, 1, HD)

    hp = min(HP, nh)
    ng = nh // hp

    def one_head(hh, st, q_ref, f_ref, i_ref, lg_ref, p_sc, r_sc):
        sl = slice(hh * HD, (hh + 1) * HD)
        q, v = q_ref[:, sl], i_ref[:, sl]
        _, _, f = _hg_gate(f_ref[:, sl], lg_ref[0, hh], lg_ref[1, hh])
        k = 1.0 - f
        low = _tri(C, "lower")
        b = _nn(low.astype(F32), jnp.log(f), HI)
        yield
        lane_c = lax.broadcasted_iota(jnp.int32, (SB, C), 1)
        lane_h = lax.broadcasted_iota(jnp.int32, (SB, HD), 1)
        row_h = lax.broadcasted_iota(jnp.int32, (SB, HD), 0)
        ones = jnp.ones((HD, HD), F32)

        for i in range(NSB):
            qi, ki, bi = q[SB * i:SB * (i + 1)], k[SB * i:SB * (i + 1)], b[SB * i:SB * (i + 1)]
            for s in range(SB):
                e = jnp.exp(jnp.minimum(bi - bi[s:s + 1], 0.0))
                p = jnp.where(row_h >= s, qi * ki[s:s + 1] * e, 0.0)
                p_sc[hh, pl.ds((i * SB + s) * SB, SB), :] = p
            yield
        r_sc[hh] = _nn(p_sc[hh], ones, HIGH)
        yield
        a_rows = []
        for i in range(NSB):
            acc = jnp.zeros((SB, HD), F32)
            for s in range(SB):
                acc = jnp.where(lane_h == SB * i + s, r_sc[hh, pl.ds((i * SB + s) * SB, SB), :], acc)
            acc = acc[:, :C]
            if i > 0:
                r = b[SB * i - 1:SB * i]
                bi = b[SB * i:SB * (i + 1)]
                qf = q[SB * i:SB * (i + 1)] * jnp.exp(bi - r)
                kf = k * jnp.exp(jnp.minimum(r - b, 0.0))
                acc = acc + jnp.where(lane_c < SB * i, _nt(qf, kf, HIGH), 0.0)
            a_rows.append(acc)
            yield
        a = jnp.concatenate(a_rows, axis=0)
        bl = b[C - 1:C, :]
        o = _nn(_bf(a), _bf(v)) + _nt(_bf(q * jnp.exp(b)), _bf(st))
        yield
        new_st = st * jnp.exp(bl) + _tn(_bf(v), _bf(k * jnp.exp(bl - b)))
        return o, a, new_st

    def body(*refs):
        c, hg = pl.program_id(0), pl.program_id(1)
        step = c * ng + hg
        ins, outs, scratch, comm_begin, comm_end = _comm_hooks(
            comm, refs, 4, 3, step == 0, step == (3 * nc * ng) // 4, step == nc * ng - 1)
        o_ref, a_ref, st_ref = outs
        s_sc, p_sc, r_sc = scratch
        comm_begin()

        @pl.when(c == 0)
        def _():
            for hh in range(hp):
                s_sc[hg * hp + hh] = jnp.zeros((HD, HD), F32)

        sts = [s_sc[hg * hp + hh] for hh in range(hp)]
        res = _interleave([one_head(hh, sts[hh], *ins, p_sc, r_sc) for hh in range(hp)])
        for hh in range(hp):
            o_ref[:, hh * HD:(hh + 1) * HD] = res[hh][0]
            a_ref[0, hh] = res[hh][1]
            st_ref[0, hh] = sts[hh]
            s_sc[hg * hp + hh] = res[hh][2]
        comm_end()

    blk = lambda off: pl.BlockSpec((C, hp * HD), lambda c, g: (c, off // hp + g))
    cn = comm.n if comm is not None else 0
    return pl.pallas_call(
        body, name=name, grid=(nc, ng),
        in_specs=[blk(0), blk(nh), blk(2 * nh),
                  pl.BlockSpec((2, hp, 1, HD), lambda c, g: (0, g, 0, 0))] + [ANY] * cn,
        out_specs=[blk(0),
                   pl.BlockSpec((1, hp, C, C), lambda c, g: (c, g, 0, 0)),
                   pl.BlockSpec((1, hp, HD, HD), lambda c, g: (c, g, 0, 0))] + [ANY] * cn,
        out_shape=[jax.ShapeDtypeStruct((t, nh * HD), F32),
                   jax.ShapeDtypeStruct((nc, nh, C, C), F32),
                   jax.ShapeDtypeStruct((nc, nh, HD, HD), F32)] + (comm.out_shapes() if cn else []),
        scratch_shapes=[pltpu.VMEM((nh, HD, HD), F32), pltpu.VMEM((hp, C * SB, HD), F32),
                        pltpu.VMEM((hp, C * SB, HD), F32)] + (comm.scratch() if cn else []),
        compiler_params=_cp(("arbitrary", "arbitrary")),
    )(proj, proj, proj, lg, *(comm.arrays if cn else []))


def _hgrn2_bwd(proj, lb_logits, do, a_sv, st_sv, nh, name, comm=None):
    t = proj.shape[0]
    nc = t // CHUNK
    C = CHUNK
    lg = lb_logits.reshape(2, nh, 1, HD)
    hp = min(HP, nh)
    ng = nh // hp

    def one_head(hh, dst, q_ref, f_ref, i_ref, lg_ref, do_ref, a_ref, st_ref, p_sc, r_sc):
        sl = slice(hh * HD, (hh + 1) * HD)
        q, v, do_ = q_ref[:, sl], i_ref[:, sl], do_ref[:, sl]
        lb, sg, f = _hg_gate(f_ref[:, sl], lg_ref[0, hh], lg_ref[1, hh])
        k = 1.0 - f
        low = _tri(C, "lower")
        b = _nn(low.astype(F32), jnp.log(f), HI)
        yield
        bl = b[C - 1:C, :]
        eb, ekb = jnp.exp(b), jnp.exp(bl - b)
        qb, kb = q * eb, k * ekb
        a, st = a_ref[0, hh], st_ref[0, hh]

        da = jnp.where(low, _nt(_bf(do_), _bf(v)), 0.0)
        yield
        dv = _tn(_bf(a), _bf(do_)) + _nt(_bf(kb), _bf(dst))
        yield
        dqb = _nn(_bf(do_), _bf(st))
        dkb = _nn(_bf(v), _bf(dst))
        yield

        row = lax.broadcasted_iota(jnp.int32, (C, HD), 0)
        lane_c = lax.broadcasted_iota(jnp.int32, (SB, C), 1)
        row_h = lax.broadcasted_iota(jnp.int32, (SB, HD), 0)
        ones = jnp.ones((HD, HD), F32)
        sel = (lax.broadcasted_iota(jnp.int32, (C, C * SB), 0)
               == jnp.right_shift(lax.broadcasted_iota(jnp.int32, (C, C * SB), 1), SB.bit_length() - 1)).astype(F32)

        for i in range(NSB):
            doi, vi = do_[SB * i:SB * (i + 1)], v[SB * i:SB * (i + 1)]
            for s in range(SB):
                p_sc[hh, pl.ds((i * SB + s) * SB, SB), :] = doi * vi[s:s + 1]
            yield
        r_sc[hh] = _nn(p_sc[hh], ones, HIGH)
        yield
        dq_rows = []
        dk_off = jnp.zeros((C, HD), F32)
        for i in range(NSB):
            qi, ki, bi = q[SB * i:SB * (i + 1)], k[SB * i:SB * (i + 1)], b[SB * i:SB * (i + 1)]
            acc = jnp.zeros((SB, HD), F32)
            for s in range(SB):
                e = jnp.exp(jnp.minimum(bi - bi[s:s + 1], 0.0))
                g = jnp.where(row_h >= s, r_sc[hh, pl.ds((i * SB + s) * SB, SB), :] * e, 0.0)
                acc = acc + g * ki[s:s + 1]
                p_sc[hh, pl.ds((i * SB + s) * SB, SB), :] = g * qi
            yield
            if i > 0:
                r = b[SB * i - 1:SB * i]
                fq = jnp.exp(bi - r)
                fk = jnp.exp(jnp.minimum(r - b, 0.0))
                dai = jnp.where(lane_c < SB * i, da[SB * i:SB * (i + 1)], 0.0)
                acc = acc + _nn(dai, k * fk, HIGH) * fq
                dk_off = dk_off + _tn(dai, qi * fq, HIGH) * fk
                yield
            dq_rows.append(acc)
        dqi = jnp.concatenate(dq_rows, axis=0)
        dq = dqi + dqb * eb
        dk_inter = dkb * ekb
        dk = _nn(sel, p_sc[hh], HIGH) + dk_off + dk_inter
        yield
        db = q * dq - k * dk
        extra = (jnp.sum(k * dk_inter, axis=0, keepdims=True)
                 + jnp.exp(bl) * jnp.sum(dst * st, axis=0, keepdims=True))
        db = db + jnp.where(row == C - 1, extra, 0.0)
        dlf = _nn(_tri(C, "upper").astype(F32), db, HI)
        yield
        df = dlf / f - dk
        dfl = (df * (1.0 - lb) * sg * (1.0 - sg)).astype(BF16)
        dl = jnp.sum(df * (1.0 - sg), axis=0, keepdims=True) * (lb * (1.0 - lb))
        new_dst = dst * jnp.exp(bl) + _tn(_bf(do_), _bf(qb))
        return dq.astype(BF16), dfl, dv.astype(BF16), dl, new_dst

    def body(*refs):
        c, hg = pl.program_id(0), pl.program_id(1)
        step = c * ng + hg
        ins, outs, scratch, comm_begin, comm_end = _comm_hooks(
            comm, refs, 7, 4, step == 0, step == (3 * nc * ng) // 4, step == nc * ng - 1)
        dq_ref, df_ref, di_ref, dl_ref = outs
        ds_sc, p_sc, r_sc = scratch
        comm_begin()

        @pl.when(c == 0)
        def _():
            for hh in range(hp):
                ds_sc[hg * hp + hh] = jnp.zeros((HD, HD), F32)

        @pl.when(step == 0)
        def _():
            dl_ref[...] = jnp.zeros_like(dl_ref)

        dsts = [ds_sc[hg * hp + hh] for hh in range(hp)]
        res = _interleave([one_head(hh, dsts[hh], *ins, p_sc, r_sc) for hh in range(hp)])
        for hh in range(hp):
            sl = slice(hh * HD, (hh + 1) * HD)
            dq_ref[:, sl], df_ref[:, sl], di_ref[:, sl] = res[hh][0], res[hh][1], res[hh][2]
            dl_ref[pl.ds(hg * hp + hh, 1), :] += res[hh][3]
            ds_sc[hg * hp + hh] = res[hh][4]
        comm_end()

    rblk = lambda off: pl.BlockSpec((C, hp * HD), lambda c, g: (nc - 1 - c, off // hp + g))
    oblk = pl.BlockSpec((C, hp * HD), lambda c, g: (nc - 1 - c, g))
    cn = comm.n if comm is not None else 0
    return pl.pallas_call(
        body, name=name, grid=(nc, ng),
        in_specs=[rblk(0), rblk(nh), rblk(2 * nh),
                  pl.BlockSpec((2, hp, 1, HD), lambda c, g: (0, g, 0, 0)),
                  oblk,
                  pl.BlockSpec((1, hp, C, C), lambda c, g: (nc - 1 - c, g, 0, 0)),
                  pl.BlockSpec((1, hp, HD, HD), lambda c, g: (nc - 1 - c, g, 0, 0))] + [ANY] * cn,
        out_specs=[oblk, oblk, oblk, pl.BlockSpec((nh, HD), lambda c, g: (0, 0))] + [ANY] * cn,
        out_shape=[jax.ShapeDtypeStruct((t, nh * HD), BF16)] * 3 + [jax.ShapeDtypeStruct((nh, HD), F32)]
        + (comm.out_shapes() if cn else []),
        scratch_shapes=[pltpu.VMEM((nh, HD, HD), F32), pltpu.VMEM((hp, C * SB, HD), F32),
                        pltpu.VMEM((hp, C * SB, HD), F32)] + (comm.scratch() if cn else []),
        compiler_params=_cp(("arbitrary", "arbitrary")),
    )(proj, proj, proj, lg, do, a_sv, st_sv, *(comm.arrays if cn else []))


def _shift_rows(u, d, row):
    t = u.shape[0]
    if d == 0:
        return u
    rolled = pltpu.roll(u, d % t, 0)
    if d > 0:
        return jnp.where(row >= d, rolled, 0.0)
    return jnp.where(row < t + d, rolled, 0.0)


def _gdn_prep(proj, conv_w, blk0, nh, name):
    t = proj.shape[0]
    scale = HD ** -0.5

    def body(u_ref, w_ref, o_ref):
        j = pl.program_id(0)
        u, w = u_ref[...], w_ref[...]
        row = lax.broadcasted_iota(jnp.int32, (t, HD), 0)
        y = w[CONV_K - 1:CONV_K, :] * u
        for d in range(1, CONV_K):
            y = y + w[CONV_K - 1 - d:CONV_K - d, :] * _shift_rows(u, d, row)
        a = y * _sigmoid(y)
        n = a * lax.rsqrt(jnp.sum(a * a, axis=-1, keepdims=True) + EPS)
        n = n * jnp.where(j < nh, scale, 1.0)
        o_ref[...] = jnp.where(j < 2 * nh, n, a)

    return pl.pallas_call(
        body, name=name, grid=(3 * nh,),
        in_specs=[pl.BlockSpec((t, HD), lambda j: (0, blk0 + j)), pl.BlockSpec((CONV_K, HD), lambda j: (0, j))],
        out_specs=pl.BlockSpec((t, HD), lambda j: (0, j)),
        out_shape=jax.ShapeDtypeStruct((t, 3 * nh * HD), F32),
        compiler_params=_cp(("parallel",)),
    )(proj, conv_w)


def _gdn_prep_bwd(proj, conv_w, dq, dk, dv, blk0, nh, name):
    t = proj.shape[0]
    scale = HD ** -0.5

    def body(u_ref, w_ref, dq_ref, dk_ref, dv_ref, du_ref, dw_ref):
        j = pl.program_id(0)
        u, w = u_ref[...], w_ref[...]
        dout = jnp.where(j < nh, dq_ref[...], jnp.where(j < 2 * nh, dk_ref[...], dv_ref[...]))
        row = lax.broadcasted_iota(jnp.int32, (t, HD), 0)
        us = [_shift_rows(u, d, row) for d in range(CONV_K)]
        y = w[CONV_K - 1:CONV_K, :] * us[0]
        for d in range(1, CONV_K):
            y = y + w[CONV_K - 1 - d:CONV_K - d, :] * us[d]
        sg = _sigmoid(y)
        a = y * sg
        rs = lax.rsqrt(jnp.sum(a * a, axis=-1, keepdims=True) + EPS)
        n = a * rs
        dn = dout * jnp.where(j < nh, scale, 1.0)
        da_n = rs * (dn - n * jnp.sum(dn * n, axis=-1, keepdims=True))
        da = jnp.where(j < 2 * nh, da_n, dout)
        dy = da * (sg * (1.0 + y * (1.0 - sg)))
        du = w[CONV_K - 1:CONV_K, :] * dy
        for d in range(1, CONV_K):
            du = du + w[CONV_K - 1 - d:CONV_K - d, :] * _shift_rows(dy, -d, row)
        du_ref[...] = du.astype(BF16)
        for d in range(CONV_K):
            dw_ref[CONV_K - 1 - d:CONV_K - d, :] = jnp.sum(dy * us[d], axis=0, keepdims=True)

    return pl.pallas_call(
        body, name=name, grid=(3 * nh,),
        in_specs=[pl.BlockSpec((t, HD), lambda j: (0, blk0 + j)), pl.BlockSpec((CONV_K, HD), lambda j: (0, j))]
        + [pl.BlockSpec((t, HD), functools.partial(lambda p, j: (0, jnp.clip(j - p * nh, 0, nh - 1)), p))
           for p in range(3)],
        out_specs=[pl.BlockSpec((t, HD), lambda j: (0, j)), pl.BlockSpec((CONV_K, HD), lambda j: (0, j))],
        out_shape=[jax.ShapeDtypeStruct((t, 3 * nh * HD), BF16), jax.ShapeDtypeStruct((CONV_K, 3 * nh * HD), F32)],
        compiler_params=_cp(("arbitrary",)),
    )(proj, conv_w, dq, dk, dv)


def _gdn_gates(ab, alog, dtb, h, nh):
    lane = lax.broadcasted_iota(jnp.int32, ab.shape, 1)
    x = ab + dtb
    sp = jnp.maximum(x, 0.0) + jnp.log(1.0 + jnp.exp(-jnp.abs(x)))
    ea = jnp.exp(alog)
    la_all = -ea * sp
    beta_all = _sigmoid(ab)
    pick = lambda val, ln: jnp.sum(jnp.where(lane == ln, val, 0.0), axis=1, keepdims=True)
    la = pick(la_all, h)
    beta = pick(beta_all, nh + h)
    dla_da = pick(-ea * _sigmoid(x), h)
    return la, beta, dla_da


def _unit_lower_inverses(ms, C):
    nb = C // SB
    sh = SB.bit_length() - 1
    rowb = jnp.right_shift(lax.broadcasted_iota(jnp.int32, (C, C), 0), sh)
    colb = jnp.right_shift(lax.broadcasted_iota(jnp.int32, (C, C), 1), sh)
    eye = (lax.broadcasted_iota(jnp.int32, (SB, SB), 0) == lax.broadcasted_iota(jnp.int32, (SB, SB), 1)).astype(F32)
    spread = (jnp.bitwise_and(lax.broadcasted_iota(jnp.int32, (SB, C), 1), SB - 1)
              == lax.broadcasted_iota(jnp.int32, (SB, C), 0)).astype(F32)
    blocks = [[m[SB * i:SB * (i + 1), SB * i:SB * (i + 1)] for i in range(nb)] for m in ms]
    xs = [[eye] * nb for _ in ms]
    for s in range(SB - 1):
        xs = [[x - b[:, s:s + 1] * x[s:s + 1, :] for x, b in zip(xh, bh)] for xh, bh in zip(xs, blocks)]
    ts = [jnp.where(rowb == colb, _nn(jnp.concatenate(xh, axis=0), spread, HIGH), 0.0) for xh in xs]
    lvl = 1
    while (1 << lvl) <= nb:
        off = ((jnp.right_shift(rowb, lvl) == jnp.right_shift(colb, lvl))
               & (jnp.right_shift(rowb, lvl - 1) != jnp.right_shift(colb, lvl - 1)))
        ts = [t - _nn(t, _nn(jnp.where(off, m, 0.0), t, HIGH), HIGH) for t, m in zip(ts, ms)]
        lvl += 1
    return ts


def _gdn_chunks(qs, ks, vs, las, betas, C):
    low, strict = _tri(C, "lower"), _tri(C, "strict")
    eye = (lax.broadcasted_iota(jnp.int32, (C, C), 0) == lax.broadcasted_iota(jnp.int32, (C, C), 1)).astype(F32)
    g_bs = [_nn(low.astype(F32), jnp.broadcast_to(la, (C, HD)), HI) for la in las]
    ps = [_nt(k, k, HIGH) for k in ks]
    qks = [_nt(_bf(q), _bf(k)) for q, k in zip(qs, ks)]
    chs = []
    for g_b, p, qk_raw, beta in zip(g_bs, ps, qks, betas):
        g_c = g_b[:, :C]
        gamma = jnp.where(low, jnp.exp(jnp.minimum(g_c - g_c.T, 0.0)), 0.0)
        gl = g_b[C - 1:C, :]
        chs.append(dict(gamma=gamma, eg=jnp.exp(g_b), gl=gl, ekt=jnp.exp(gl - g_b), p=p,
                        m=jnp.where(strict, beta * p * gamma, 0.0), qk_raw=qk_raw))
    xs = _unit_lower_inverses([ch["m"] for ch in chs], C)
    r_ws = [k * (beta * ch["eg"]) for ch, k, beta in zip(chs, ks, betas)]
    uws = [_nn(x, jnp.concatenate([v * beta, r_w], axis=1), HIGH) for x, v, beta, r_w in zip(xs, vs, betas, r_ws)]
    for ch, x, r_w, uw in zip(chs, xs, r_ws, uws):
        ch.update(x=x, r_w=r_w, uw=uw)
    return chs


def _gdn_fwd(qkv, proj, ab_blk, alog, dtb, nh, name, comm=None):
    t = qkv.shape[0]
    nc = t // CHUNK
    C = CHUNK
    hp = min(HP, nh)
    ng = nh // hp

    def body(*refs):
        c, hg = pl.program_id(0), pl.program_id(1)
        step = c * ng + hg
        ins, outs, scratch, comm_begin, comm_end = _comm_hooks(
            comm, refs, 6, 3, step == 0, step == (3 * nc * ng) // 4, step == nc * ng - 1)
        q_ref, k_ref, v_ref, ab_ref, al_ref, dt_ref = ins
        o_ref, x_ref, st_ref = outs
        s_sc, = scratch
        comm_begin()

        @pl.when(c == 0)
        def _():
            for hh in range(hp):
                s_sc[hg * hp + hh] = jnp.zeros((HD, HD), F32)

        sls = [slice(hh * HD, (hh + 1) * HD) for hh in range(hp)]
        qs, ks, vs = [q_ref[:, sl] for sl in sls], [k_ref[:, sl] for sl in sls], [v_ref[:, sl] for sl in sls]
        sts = [s_sc[hg * hp + hh] for hh in range(hp)]
        gates = [_gdn_gates(ab_ref[...], al_ref[...], dt_ref[...], hg * hp + hh, nh) for hh in range(hp)]
        chs = _gdn_chunks(qs, ks, vs, [g[0] for g in gates], [g[1] for g in gates], C)
        stbs = [_bf(st) for st in sts]
        vns = [ch["uw"][:, :HD] - _nt(_bf(ch["uw"][:, HD:]), stb) for ch, stb in zip(chs, stbs)]
        o_st = [_nt(_bf(q * ch["eg"]), stb) for q, ch, stb in zip(qs, chs, stbs)]
        outs_ = [o + _nn(_bf(ch["qk_raw"] * ch["gamma"]), _bf(vn)) for o, ch, vn in zip(o_st, chs, vns)]
        new_sts = [st * jnp.exp(ch["gl"]) + _tn(_bf(vn), _bf(k * ch["ekt"]))
                   for st, ch, vn, k in zip(sts, chs, vns, ks)]
        for hh in range(hp):
            o_ref[:, sls[hh]] = outs_[hh]
            x_ref[0, hh] = chs[hh]["x"]
            st_ref[0, hh] = sts[hh]
            s_sc[hg * hp + hh] = new_sts[hh]
        comm_end()

    blk = lambda off: pl.BlockSpec((C, hp * HD), lambda c, g: (c, off // hp + g))
    vec = pl.BlockSpec((1, HD), lambda c, g: (0, 0))
    cn = comm.n if comm is not None else 0
    return pl.pallas_call(
        body, name=name, grid=(nc, ng),
        in_specs=[blk(0), blk(nh), blk(2 * nh), pl.BlockSpec((C, HD), lambda c, g: (c, ab_blk)), vec, vec]
        + [ANY] * cn,
        out_specs=[blk(0),
                   pl.BlockSpec((1, hp,C, C), lambda c, g: (c, g, 0, 0)),
                   pl.BlockSpec((1, hp,HD, HD), lambda c, g: (c, g, 0, 0))] + [ANY] * cn,
        out_shape=[jax.ShapeDtypeStruct((t, nh * HD), F32),
                   jax.ShapeDtypeStruct((nc, nh, C, C), F32),
                   jax.ShapeDtypeStruct((nc, nh, HD, HD), F32)] + (comm.out_shapes() if cn else []),
        scratch_shapes=[pltpu.VMEM((nh, HD, HD), F32)] + (comm.scratch() if cn else []),
        compiler_params=_cp(("arbitrary", "arbitrary")),
    )(qkv, qkv, qkv, proj, alog, dtb, *(comm.arrays if cn else []))


def _gdn_bwd(qkv, proj, ab_blk, alog, dtb, do, x_sv, st_sv, nh, name, comm=None):
    t = qkv.shape[0]
    nc = t // CHUNK
    C = CHUNK
    hp = min(HP, nh)
    ng = nh // hp

    def one_head(h, hh, dst, q_ref, k_ref, v_ref, ab_ref, al_ref, dt_ref, do_ref, x_ref, st_ref):
        sl = slice(hh * HD, (hh + 1) * HD)
        q, k, v, do_ = q_ref[:, sl], k_ref[:, sl], v_ref[:, sl], do_ref[:, sl]
        la, beta, dla_da = _gdn_gates(ab_ref[...], al_ref[...], dt_ref[...], h, nh)
        low, strict = _tri(C, "lower"), _tri(C, "strict")
        g_b = _nn(low.astype(F32), jnp.broadcast_to(la, (C, HD)), HI)
        yield
        g_c = g_b[:, :C]
        gamma = jnp.where(low, jnp.exp(jnp.minimum(g_c - g_c.T, 0.0)), 0.0)
        eg = jnp.exp(g_b)
        gl = g_b[C - 1:C, :]
        ekt = jnp.exp(gl - g_b)
        egl = jnp.exp(gl)
        p = _nt(k, k, HIGH)
        yield
        x = x_ref[0, hh]
        r_w = k * (beta * eg)
        rhs = jnp.concatenate([v * beta, r_w], axis=1)
        uw = _nn(x, rhs, HIGH)
        yield
        u, w = uw[:, :HD], uw[:, HD:]
        qk_raw = _nt(_bf(q), _bf(k))
        yield
        qk = qk_raw * gamma
        st = st_ref[0, hh]
        stb, dstb = _bf(st), _bf(dst)
        vn = u - _nt(_bf(w), stb)
        yield
        qd, kt = q * eg, k * ekt

        dvn = _tn(_bf(qk), _bf(do_)) + _nt(_bf(kt), dstb)
        yield
        dq2 = jnp.where(low, _nt(_bf(do_), _bf(vn)), 0.0)
        yield
        dqd = _nn(_bf(do_), stb)
        yield
        dkt = _nn(_bf(vn), dstb)
        yield
        dw = -_nn(_bf(dvn), stb)
        yield
        dxx = jnp.concatenate([dvn, dw], axis=1)
        dr = _tn(x, dxx, HIGH)
        yield
        dm = -jnp.where(strict, _nt(dr, uw, HIGH), 0.0)
        yield
        dr_u, dr_w = dr[:, :HD], dr[:, HD:]
        rsum = lambda z: jnp.sum(z, axis=1, keepdims=True)

        dv = dr_u * beta
        dmg = dm * gamma
        dbeta = rsum(dr_u * v) + rsum(dr_w * k) * eg[:, :1] + rsum(dmg * p)
        yield
        dp = dmg * beta
        dq2g = dq2 * gamma
        dk = (dr_w * (beta * eg) + dkt * ekt + _tn(_bf(dq2g), _bf(q))
              + _nn(_bf(dp + dp.T), _bf(k)))
        yield
        dq = dqd * eg + _nn(_bf(dq2g), _bf(k))
        yield
        e = dp * p + dq2g * qk_raw
        t_kt = rsum(dkt * kt)
        dg = rsum(dqd * qd) + rsum(dr_w * r_w) - t_kt + rsum(e) - rsum(e.T)
        yield
        dgl = jnp.sum(t_kt, axis=0, keepdims=True) + jnp.sum(dst * st, keepdims=True) * egl[:, :1]
        rowc = lax.broadcasted_iota(jnp.int32, (C, 1), 0)
        dg = dg + jnp.where(rowc == C - 1, dgl, 0.0)
        dla = _nn(_tri(C, "upper").astype(F32), jnp.broadcast_to(dg, (C, HD)), HI)[:, :1]
        yield
        da = dla * dla_da
        db = dbeta * beta * (1.0 - beta)
        lane = lax.broadcasted_iota(jnp.int32, (C, HD), 1)
        dab = jnp.where(lane == h, da, 0.0) + jnp.where(lane == nh + h, db, 0.0)
        lane1 = lax.broadcasted_iota(jnp.int32, (1, HD), 1)
        d_alog = jnp.where(lane1 == h, jnp.sum(dla * la, axis=0, keepdims=True), 0.0)
        d_dtb = jnp.where(lane1 == h, jnp.sum(da, axis=0, keepdims=True), 0.0)
        new_dst = dst * egl + _tn(_bf(do_), _bf(qd)) - _tn(_bf(dvn), _bf(w))
        return dab, d_alog, d_dtb, new_dst, dq, dk, dv

    def body(*refs):
        c, hg = pl.program_id(0), pl.program_id(1)
        step = c * ng + hg
        ins, outs, scratch, comm_begin, comm_end = _comm_hooks(
            comm, refs, 9, 5, step == 0, step == (3 * nc * ng) // 4, step == nc * ng - 1)
        dq_ref, dk_ref, dv_ref, dab_ref, dpar_ref = outs
        ds_sc, = scratch
        comm_begin()

        @pl.when(c == 0)
        def _():
            for hh in range(hp):
                ds_sc[hg * hp + hh] = jnp.zeros((HD, HD), F32)

        @pl.when(step == 0)
        def _():
            dpar_ref[...] = jnp.zeros_like(dpar_ref)

        @pl.when(hg == 0)
        def _():
            dab_ref[...] = jnp.zeros_like(dab_ref)

        dsts = [ds_sc[hg * hp + hh] for hh in range(hp)]
        res = _interleave([one_head(hg * hp + hh, hh, dsts[hh], *ins) for hh in range(hp)])
        for hh in range(hp):
            sl = slice(hh * HD, (hh + 1) * HD)
            ds_sc[hg * hp + hh] = res[hh][3]
            dq_ref[:, sl], dk_ref[:, sl], dv_ref[:, sl] = res[hh][4], res[hh][5], res[hh][6]
        dab_ref[...] += sum(r[0] for r in res[1:]) + res[0][0]
        dpar_ref[0:1, :] += sum(r[1] for r in res[1:]) + res[0][1]
        dpar_ref[1:2, :] += sum(r[2] for r in res[1:]) + res[0][2]
        comm_end()

    rblk = lambda off: pl.BlockSpec((C, hp * HD), lambda c, g: (nc - 1 - c, off // hp + g))
    oblk = pl.BlockSpec((C, hp * HD), lambda c, g: (nc - 1 - c, g))
    vec = pl.BlockSpec((1, HD), lambda c, g: (0, 0))
    cn = comm.n if comm is not None else 0
    return pl.pallas_call(
        body, name=name, grid=(nc, ng),
        in_specs=[rblk(0), rblk(nh), rblk(2 * nh),
                  pl.BlockSpec((C, HD), lambda c, g: (nc - 1 - c, ab_blk)), vec, vec, oblk,
                  pl.BlockSpec((1, hp,C, C), lambda c, g: (nc - 1 - c, g, 0, 0)),
                  pl.BlockSpec((1, hp,HD, HD), lambda c, g: (nc - 1 - c, g, 0, 0))] + [ANY] * cn,
        out_specs=[oblk, oblk, oblk,
                   pl.BlockSpec((C, HD), lambda c, g: (nc - 1 - c, 0)),
                   pl.BlockSpec((8, HD), lambda c, g: (0, 0))] + [ANY] * cn,
        out_shape=[jax.ShapeDtypeStruct((t, nh * HD), F32)] * 3
        + [jax.ShapeDtypeStruct((t, HD), F32), jax.ShapeDtypeStruct((8, HD), F32)]
        + (comm.out_shapes() if cn else []),
        scratch_shapes=[pltpu.VMEM((nh, HD, HD), F32)] + (comm.scratch() if cn else []),
        compiler_params=_cp(("arbitrary", "arbitrary")),
    )(qkv, qkv, qkv, proj, alog, dtb, do, x_sv, st_sv, *(comm.arrays if cn else []))


def _ada_fwd(c_all, w, b, name):
    nb, d = c_all.shape
    n = w.shape[1]
    tn = _pick(n, 512)

    def body(c_ref, w_ref, b_ref, o_ref):
        cv = c_ref[...]
        o_ref[...] = _nn(cv * _sigmoid(cv), w_ref[...], HI) + b_ref[...]

    return pl.pallas_call(
        body, name=name, grid=(n // tn,),
        in_specs=[pl.BlockSpec((nb, d), lambda j: (0, 0)), pl.BlockSpec((d, tn), lambda j: (0, j)),
                  pl.BlockSpec((1, tn), lambda j: (0, j))],
        out_specs=pl.BlockSpec((nb, tn), lambda j: (0, j)),
        out_shape=jax.ShapeDtypeStruct((nb, n), F32),
        compiler_params=_cp(("parallel",)),
    )(c_all, w, b)


def _ada_wgrad(c_all, dmod, name):
    nb, d = c_all.shape
    n = dmod.shape[1]
    tn = _pick(n, 512)

    def body(c_ref, g_ref, o_ref):
        cv = c_ref[...]
        o_ref[...] = _tn(cv * _sigmoid(cv), g_ref[...], HI)

    return pl.pallas_call(
        body, name=name, grid=(n // tn,),
        in_specs=[pl.BlockSpec((nb, d), lambda j: (0, 0)), pl.BlockSpec((nb, tn), lambda j: (0, j))],
        out_specs=pl.BlockSpec((d, tn), lambda j: (0, j)),
        out_shape=jax.ShapeDtypeStruct((d, n), F32),
        compiler_params=_cp(("parallel",)),
    )(c_all, dmod)


def _adamw(w, m, v, g, name, parts=False):
    lead = w.ndim == 3
    r, cdim = w.shape[-2:]
    cap = max(SUBLANES, ADAM_BLOCK_ELEMS // cdim // SUBLANES * SUBLANES)
    tr = r if r <= cap else _pick_rows(r, cap)
    bc1 = 1.0 - ADAM_B1 ** ADAM_STEP
    bc2 = 1.0 - ADAM_B2 ** ADAM_STEP

    glist = list(g) if isinstance(g, (list, tuple)) else [g]
    bounds = [0]
    for ga in glist:
        bounds.append(bounds[-1] + ga.shape[-2] // tr)

    def body(w_ref, m_ref, v_ref, *rest):
        g_refs, (go_ref, d_ref, mo_ref, vo_ref) = rest[:len(glist)], rest[len(glist):]
        if parts:
            sums = []
            for g_ref in g_refs:
                gv = g_ref[0].astype(F32)
                for s in range(1, N_DEV):
                    gv = gv + g_ref[s].astype(F32)
                sums.append(gv)
            gv = sums[-1]
            for p in range(len(sums) - 2, -1, -1):
                gv = jnp.where(pl.program_id(0) < bounds[p + 1], sums[p], gv)
        else:
            gv = g_refs[0][...]
        wv = w_ref[...]
        mn = ADAM_B1 * m_ref[...] + (1.0 - ADAM_B1) * gv
        vn = ADAM_B2 * v_ref[...] + (1.0 - ADAM_B2) * (gv * gv)
        m_hat = mn / bc1
        v_hat = vn / bc2
        go_ref[...] = gv
        d_ref[...] = -ADAM_LR * (m_hat / (jnp.sqrt(v_hat) + ADAM_EPS) + ADAM_WD * wv)
        mo_ref[...] = mn
        vo_ref[...] = vn

    flat = pl.BlockSpec((tr, cdim), lambda i: (i, 0))
    spec = pl.BlockSpec((None, tr, cdim), lambda i: (0, i, 0)) if lead else flat
    def piece_spec(p):
        lo, n = bounds[p], bounds[p + 1] - bounds[p]
        return pl.BlockSpec((N_DEV, tr, cdim), lambda i: (0, jnp.clip(i - lo, 0, n - 1), 0))

    gspecs = [piece_spec(p) for p in range(len(glist))] if parts else [flat]
    return pl.pallas_call(
        body, name=name, grid=(r // tr,),
        in_specs=[spec, spec, spec] + gspecs,
        out_specs=[spec] * 4,
        out_shape=[jax.ShapeDtypeStruct(w.shape, F32)] * 4,
        compiler_params=_cp(("arbitrary",)),
    )(w, m, v, *glist)


def _pick_rows(r, pref):
    t = pref
    while r % t:
        t -= 8
    assert t > 0
    return t


def _dev_index(x, y, c):
    return 4 * x + 2 * y + c


class _Comm:
    def __init__(self, kind, arrays):
        self.kind, self.n = kind, len(arrays)
        self.arrays = [a[0] if isinstance(a, tuple) else a for a in arrays]
        self.rows = [(a[1], a[2]) if isinstance(a, tuple) else None for a in arrays]

    def out_shapes(self):
        if self.kind == "gather":
            return [jax.ShapeDtypeStruct((N_DEV,) + a.shape, a.dtype) for a in self.arrays]
        return [jax.ShapeDtypeStruct(a.shape if r is None else (N_DEV, r[1]) + a.shape[2:], a.dtype)
                for a, r in zip(self.arrays, self.rows)]

    def scratch(self):
        return [pltpu.SemaphoreType.DMA((self.n, 7)), pltpu.SemaphoreType.DMA((self.n, 7)),
                pltpu.SemaphoreType.DMA((self.n,))]

    def _gather_parts(self, ins, outs, sems):
        send_sems, recv_sems, local_sems = sems
        x, y, c = lax.axis_index("x"), lax.axis_index("y"), lax.axis_index("c")
        me, sibling = (x, y, c), (x, y, 1 - c)
        chips = [(1 - x, y), (x, 1 - y), (1 - x, 1 - y)]

        def copy(a, k, block, to, src=None):
            slot = outs[a].at[_dev_index(*block)]
            return pltpu.make_async_remote_copy(
                src_ref=slot if src is None else src, dst_ref=slot,
                send_sem=send_sems.at[a, k], recv_sem=recv_sems.at[a, k],
                device_id=to, device_id_type=MESH)

        n = self.n
        mine = [pltpu.make_async_copy(ins[a], outs[a].at[_dev_index(*me)], local_sems.at[a]) for a in range(n)]
        first = []
        for a in range(n):
            first.append(copy(a, 0, me, sibling, src=ins[a]))
            first += [copy(a, 1 + j, me, (*chip, c), src=ins[a]) for j, chip in enumerate(chips)]
        landed = [copy(a, 1 + j, (*chip, c), me) for j, chip in enumerate(chips) for a in range(n)]
        passed = [copy(a, 4 + j, (*chip, c), sibling) for j, chip in enumerate(chips) for a in range(n)]
        late = []
        for a in range(n):
            late.append(copy(a, 0, sibling, me))
            late += [copy(a, 4 + j, (*chip, 1 - c), me) for j, chip in enumerate(chips)]
        return mine, first, landed, passed, late

    def _exchange_parts(self, ins, outs, sems):
        send_sems, recv_sems, local_sems = sems
        x, y, c = lax.axis_index("x"), lax.axis_index("y"), lax.axis_index("c")
        my = _dev_index(x, y, c)
        n = self.n

        def block(a, j):
            r = self.rows[a]
            return ins[a].at[j] if r is None else ins[a].at[j, pl.ds(r[0], r[1])]

        mine = [pltpu.make_async_copy(block(a, my), outs[a].at[my], local_sems.at[a]) for a in range(n)]
        sends, recvs = [], []
        for k in range(1, N_DEV):
            px = (1 - x) if (k >> 2) & 1 else x
            py = (1 - y) if (k >> 1) & 1 else y
            pc = (1 - c) if k & 1 else c
            peer = _dev_index(px, py, pc)
            for a in range(n):
                sends.append(pltpu.make_async_remote_copy(
                    src_ref=block(a, peer), dst_ref=outs[a].at[my],
                    send_sem=send_sems.at[a, k - 1], recv_sem=recv_sems.at[a, k - 1],
                    device_id=(px, py, pc), device_id_type=MESH))
                recvs.append(pltpu.make_async_remote_copy(
                    src_ref=block(a, my), dst_ref=outs[a].at[peer],
                    send_sem=send_sems.at[a, k - 1], recv_sem=recv_sems.at[a, k - 1],
                    device_id=(x, y, c), device_id_type=MESH))
        return mine, sends, recvs

    def start(self, ins, outs, sems):
        if self.kind == "gather":
            mine, first, _, _, _ = self._gather_parts(ins, outs, sems)
        else:
            mine, first, _ = self._exchange_parts(ins, outs, sems)
        for cp in mine + first:
            cp.start()

    def mid(self, ins, outs, sems):
        if self.kind == "gather":
            _, _, landed, passed, _ = self._gather_parts(ins, outs, sems)
            for got, fwd in zip(landed, passed):
                got.wait_recv()
                fwd.start()

    def finish(self, ins, outs, sems):
        if self.kind == "gather":
            mine, first, _, passed, late = self._gather_parts(ins, outs, sems)
            for cp in late:
                cp.wait_recv()
            for cp in first + passed:
                cp.wait_send()
        else:
            mine, sends, recvs = self._exchange_parts(ins, outs, sems)
            for cp in sends:
                cp.wait_send()
            for cp in recvs:
                cp.wait_recv()
        for cp in mine:
            cp.wait()

    def run(self, name):
        n = self.n

        def body(*refs):
            ins, outs, sems = refs[:n], refs[n:2 * n], refs[2 * n:]
            self.start(ins, outs, sems)
            self.mid(ins, outs, sems)
            self.finish(ins, outs, sems)

        return pl.pallas_call(
            body, name=name, in_specs=[ANY] * n, out_specs=[ANY] * n,
            out_shape=self.out_shapes(), scratch_shapes=self.scratch(),
        )(*self.arrays)


def _all_gather(arrays, name):
    return _Comm("gather", arrays).run(name)


def _comm_hooks(comm, refs, n_in, n_out, first, middle, last):
    cn = comm.n if comm is not None else 0
    ins, cins = refs[:n_in], refs[n_in:n_in + cn]
    outs, couts = refs[n_in + cn:n_in + cn + n_out], refs[n_in + cn + n_out:n_in + 2 * cn + n_out]
    rest = refs[n_in + 2 * cn + n_out:]
    scratch, csems = (rest[:len(rest) - 3], rest[len(rest) - 3:]) if cn else (rest, ())

    def begin():
        if cn:
            pl.when(first)(lambda: comm.start(cins, couts, csems))
            pl.when(middle)(lambda: comm.mid(cins, couts, csems))

    def end():
        if cn:
            pl.when(last)(lambda: comm.finish(cins, couts, csems))

    return ins, outs, scratch, begin, end


def _local_step(x, tgt, mod, n1, n2, n3, n4, w_in_p, lb_logits, hg_norm, conv_w, alog, dtb, gdn_norm,
                late_w, dist=None):
    t, d = x.shape
    nh = d // 2 // HD
    ab_blk = 8 * nh
    sh_m, sc_m, gt_m, sh_f, sc_f, gt_f = [mod[i:i + 1] for i in range(6)]

    h1, r1 = _prenorm(x, n1, sc_m, sh_m, "prenorm_mix")
    if dist is None:
        proj = _mm(h1, w_in_p, "nn", [F32], "mm_proj")
        o_hg, a_sv, hst_sv = _hgrn2_fwd(proj, lb_logits, nh, "hgrn2_fwd")
        qkv = _gdn_prep(proj, conv_w, 4 * nh, nh, "gdn_prep")
        o_gd, x_sv, gst_sv = _gdn_fwd(qkv, proj, ab_blk, alog, dtb, nh, "gdn_fwd")
        w_out, w_ff1, w_ff2 = late_w
        exch = lambda arrays: None
    else:
        proj, g_ff2 = _mm(h1, w_in_p, "nn", [F32], "mm_proj", comm=_Comm("gather", late_w[2:]))
        o_hg, a_sv, hst_sv, g_out = _hgrn2_fwd(proj, lb_logits, nh, "hgrn2_fwd",
                                               comm=_Comm("gather", late_w[:1]))
        qkv = _gdn_prep(proj, conv_w, 4 * nh, nh, "gdn_prep")
        o_gd, x_sv, gst_sv, g_ff1 = _gdn_fwd(qkv, proj, ab_blk, alog, dtb, nh, "gdn_fwd",
                                             comm=_Comm("gather", late_w[1:2]))
        w_out, w_ff1, w_ff2 = dist["assemble"](g_out, g_ff1, g_ff2)
        exch = lambda arrays: _Comm("exchange", arrays)
    om_hg = _headnorm_fwd(o_hg, proj, 3 * nh, hg_norm, "headnorm_hg")
    om_gd = _headnorm_fwd(o_gd, proj, 7 * nh, gdn_norm, "headnorm_gdn")
    om = jnp.concatenate([om_hg, om_gd], axis=1)
    y1 = _mm(om, w_out, "nn", [F32], "mm_out")
    x1, r2 = _postnorm_res(x, y1, n2, gt_m, "postnorm_mix")
    h2, r3 = _prenorm(x1, n3, sc_f, sh_f, "prenorm_ffn")

    def relu2(acc, extra, outs):
        outs[0][...] = acc
        rl = jnp.maximum(acc, 0.0)
        outs[1][...] = (rl * rl).astype(BF16)

    u, act = _mm(h2, w_ff1, "nn", [F32, BF16], "mm_ff1", epilogue=relu2)
    y2 = _mm(act, w_ff2, "nn", [F32], "mm_ff2")
    dout, r4, loss = _final_loss(x1, y2, n4, gt_f, tgt, "final_loss")

    dy2, dgt_f, dn4 = _postnorm_bwd(dout, y2, r4, n4, gt_f, "postnorm_ffn_bwd")
    dw_ff2 = _mm(act, dy2, "tn", [BF16], "mm_dw_ff2")

    def drelu2(acc, extra, outs):
        outs[0][...] = (acc * (2.0 * jnp.maximum(extra[0][...], 0.0))).astype(BF16)

    recv = {}
    ff2a, ff2b = dist["parts_ff2"](dw_ff2) if dist else (None, None)
    du, *recv["ff2a"] = _listed(_mm(dy2, w_ff2, "nt", [BF16], "mm_da", epilogue=drelu2, extras=(u,),
                                    comm=exch([ff2a])))
    ff1_cols = dict(by_cols=True, tn=dist["n_ff"]) if dist else {}
    dw_ff1, *recv["ff2b"] = _listed(_mm(h2, du, "tn", [BF16], "mm_dw_ff1", comm=exch([ff2b]), **ff1_cols))
    ff1a, ff1b = dist["parts_ff1"](dw_ff1) if dist else (None, None)
    dh2, *recv["ff1a"] = _listed(_mm(du, w_ff1, "nt", [F32], "mm_dh2", comm=exch([ff1a])))
    dx1, dsh_f, dsc_f, dn3 = _prenorm_bwd(dh2, x1, r3, n3, sc_f, dout, "prenorm_ffn_bwd")

    dy1, dgt_m, dn2 = _postnorm_bwd(dx1, y1, r2, n2, gt_m, "postnorm_mix_bwd")
    dw_out = _mm(om, dy1, "tn", [BF16], "mm_dw_out")
    dom = _mm(dy1, w_out, "nt", [F32], "mm_dom")
    do_hg, dg_hg, dhgn = _headnorm_bwd(dom, 0, o_hg, proj, 3 * nh, hg_norm, "headnorm_hg_bwd")
    do_gd, dg_gd, dgdn = _headnorm_bwd(dom, 1, o_gd, proj, 7 * nh, gdn_norm, "headnorm_gdn_bwd")
    p_out = dist["parts_out"](dw_out) if dist else None
    dq_hg, df_hg, di_hg, dl0, *recv["ff1b_out"] = _hgrn2_bwd(proj, lb_logits, do_hg, a_sv, hst_sv, nh,
                                                             "hgrn2_bwd", comm=exch([ff1b, p_out]))
    dq_g, dk_g, dv_g, dab, dpar = _gdn_bwd(qkv, proj, ab_blk, alog, dtb, do_gd, x_sv, gst_sv, nh, "gdn_bwd")
    du_conv, dconv = _gdn_prep_bwd(proj, conv_w, dq_g, dk_g, dv_g, 4 * nh, nh, "gdn_prep_bwd")
    dproj = jnp.concatenate([dq_hg, df_hg, di_hg, dg_hg, du_conv, dg_gd, dab.astype(BF16)], axis=1)
    if dist is None:
        dw_in = _mm(h1, dproj, "tn", [BF16], "mm_dw_in")
        dh1 = _mm(dproj, w_in_p, "nt", [F32], "mm_dh1", tk=1664)
    else:
        q4 = d // 4
        dw_in_a = _mm(h1[:, :q4], dproj, "tn", [BF16], "mm_dw_in_a")
        dw_in_b, in_a = _mm(h1[:, q4:2 * q4], dproj, "tn", [BF16], "mm_dw_in_b",
                            comm=exch([dist["parts_in"](dw_in_a)]))
        dw_in_c, in_b = _mm(h1[:, 2 * q4:], dproj, "tn", [BF16], "mm_dw_in_c",
                            comm=exch([dist["parts_in"](dw_in_b)]))
        dh1, in_c = _mm(dproj, w_in_p, "nt", [F32], "mm_dh1", tk=1664, comm=exch([dist["parts_in"](dw_in_c)]))
        recv["in"] = [in_a, in_b, in_c]
        dw_in = None
    dx, dsh_m, dsc_m, dn1 = _prenorm_bwd(dh1, x, r1, n1, sc_m, dx1, "prenorm_mix_bwd")

    dmod = jnp.concatenate([dsh_m, dsc_m, dgt_m, dsh_f, dsc_f, dgt_f], axis=0)
    grads = dict(dmod=dmod, n1=dn1, n2=dn2, n3=dn3, n4=dn4, w_in=dw_in, lb0=dl0, hg_norm=dhgn, conv=dconv,
                 alog=dpar[0:1], dtb=dpar[1:2], gdn_norm=dgdn, w_out=dw_out, w_ff1=dw_ff1, w_ff2=dw_ff2,
                 recv=recv)
    return loss, dx, grads


def _pack(vals):
    rows = []
    for vv in vals:
        flat = vv.reshape(-1)
        flat = jnp.pad(flat, (0, (-flat.shape[0]) % (SUBLANES * LANES)))
        rows.append(flat.reshape(-1, LANES))
    return jnp.concatenate(rows, axis=0)


def _unpack(packed, shapes):
    out, r = [], 0
    for shp in shapes:
        size = 1
        for s in shp:
            size *= s
        nr = -(-size // (SUBLANES * LANES)) * SUBLANES
        out.append(packed[r:r + nr].reshape(-1)[:size].reshape(shp))
        r += nr
    return out


def _sum_parts(parts, name):
    _, r, cdim = parts.shape

    def body(p_ref, o_ref):
        acc = p_ref[0]
        for s in range(1, N_DEV):
            acc = acc + p_ref[s]
        o_ref[...] = acc

    return pl.pallas_call(
        body, name=name,
        out_shape=jax.ShapeDtypeStruct((r, cdim), F32),
        compiler_params=_cp(),
    )(parts)


def kernel(x, c, w_ada, b_ada, pre_mix_norm, post_mix_norm, pre_ffn_norm, post_ffn_norm, w_in, hg_lb_logits, hg_norm, gdn_conv_w, gdn_a_log, gdn_dt_bias, gdn_norm, w_out, w_ff1, w_ff2, loss_target, m_w_ada, m_b_ada, m_pre_mix_norm, m_post_mix_norm, m_pre_ffn_norm, m_post_ffn_norm, m_w_in, m_hg_lb_logits, m_hg_norm, m_gdn_conv_w, m_gdn_a_log, m_gdn_dt_bias, m_gdn_norm, m_w_out, m_w_ff1, m_w_ff2, v_w_ada, v_b_ada, v_pre_mix_norm, v_post_mix_norm, v_pre_ffn_norm, v_post_ffn_norm, v_w_in, v_hg_lb_logits, v_hg_norm, v_gdn_conv_w, v_gdn_a_log, v_gdn_dt_bias, v_gdn_norm, v_w_out, v_w_ff1, v_w_ff2):
    t, d = x.shape[1], x.shape[2]
    nh = d // 2 // HD
    in_cols = w_in.shape[2] * N_DEV
    main = in_cols - 2 * nh
    me = _dev_index(lax.axis_index("x"), lax.axis_index("y"), lax.axis_index("c"))

    c_all, conv_g = _all_gather([c, gdn_conv_w[0]], "gather_small")
    c_all = c_all.reshape(N_DEV, d)
    conv_full = conv_g.transpose(1, 0, 2).reshape(CONV_K, -1)
    w_in_g = _all_gather([w_in[0].astype(BF16)], "gather_w_in")[0]
    w_in_full = w_in_g.transpose(1, 0, 2).reshape(d, in_cols)
    w_in_p = jnp.concatenate([w_in_full, jnp.zeros((d, LANES - 2 * nh), BF16)], axis=1)
    late_w = [w_out[0].astype(BF16), w_ff1[0].astype(BF16), w_ff2[0].astype(BF16)]

    n_in = w_in.shape[2]
    n_ff = w_ff1.shape[2]

    def halves(p):
        r = p.shape[1] // 2
        return (p, 0, r), (p, r, r)

    dist = dict(
        assemble=lambda g_out, g_ff1, g_ff2: (g_out.reshape(d, d), g_ff1.transpose(1, 0, 2).reshape(d, -1),
                                              g_ff2.reshape(-1, d)),
        n_ff=n_ff,
        parts_ff2=lambda dw: halves(dw.reshape(N_DEV, -1, d)),
        parts_ff1=halves,
        parts_out=lambda dw: dw.reshape(N_DEV, d // N_DEV, d),
        parts_in=lambda dw: dw[:, :in_cols].reshape(dw.shape[0], N_DEV, n_in).transpose(1, 0, 2),
    )

    n_ada = w_ada.shape[2]
    b_loc = lax.dynamic_slice(b_ada, (0, me * n_ada), (1, n_ada))
    mod_part = _ada_fwd(c_all, w_ada[0], b_loc, "ada_fwd")
    mod_all = _all_gather([mod_part], "gather_mod")[0]
    mod = lax.dynamic_slice(mod_all, (0, me, 0), (N_DEV, 1, n_ada)).reshape(6, d)

    pad_lane = lambda vv: jnp.concatenate([vv, jnp.zeros((1, LANES - vv.shape[1]), F32)], axis=1)
    loss, dx, g = _local_step(
        x[0], loss_target[0], mod, pre_mix_norm, post_mix_norm, pre_ffn_norm, post_ffn_norm, w_in_p,
        hg_lb_logits, hg_norm, conv_full, pad_lane(gdn_a_log), pad_lane(gdn_dt_bias), gdn_norm,
        late_w, dist)

    rep_names = ["b_ada", "n1", "n2", "n3", "n4", "lb", "hg_norm", "alog", "dtb", "gdn_norm"]
    rep_w = [b_ada, pre_mix_norm, post_mix_norm, pre_ffn_norm, post_ffn_norm, hg_lb_logits, hg_norm,
             gdn_a_log, gdn_dt_bias, gdn_norm]
    rep_m = [m_b_ada, m_pre_mix_norm, m_post_mix_norm, m_pre_ffn_norm, m_post_ffn_norm, m_hg_lb_logits,
             m_hg_norm, m_gdn_a_log, m_gdn_dt_bias, m_gdn_norm]
    rep_v = [v_b_ada, v_pre_mix_norm, v_post_mix_norm, v_pre_ffn_norm, v_post_ffn_norm, v_hg_lb_logits,
             v_hg_norm, v_gdn_a_log, v_gdn_dt_bias, v_gdn_norm]
    rep_shapes = [a.shape for a in rep_w]
    g_lb = jnp.stack([g["lb0"], -g["lb0"]], axis=0)
    rep_g = [g["dmod"], g["n1"], g["n2"], g["n3"], g["n4"], g_lb, g["hg_norm"],
             g["alog"][:, :nh], g["dtb"][:, :nh], g["gdn_norm"]]
    small = _pack(rep_g + [g["conv"]])
    n_rep_rows = _pack(rep_g).shape[0]
    pad_rows = (-small.shape[0]) % 8
    if pad_rows:
        small = jnp.concatenate([small, jnp.zeros((pad_rows, LANES), F32)], axis=0)
    small_all = _all_gather([small], "gather_small_grads")[0]
    small_sum = _sum_parts(small_all, "sum_small_grads")
    rep_out = _adamw(_pack(rep_w), _pack(rep_m), _pack(rep_v), small_sum[:n_rep_rows], "adamw_small")
    rep_g_o, rep_d_o, rep_m_o, rep_v_o = [dict(zip(rep_names, _unpack(p, rep_shapes))) for p in rep_out]

    conv_sum = small_sum[n_rep_rows:n_rep_rows + CONV_K * conv_full.shape[1] // LANES].reshape(CONV_K, -1)
    n_conv = gdn_conv_w.shape[2]
    conv_loc = lax.dynamic_slice(conv_sum, (0, me * n_conv), (CONV_K, n_conv))
    conv_o = _adamw(gdn_conv_w, m_gdn_conv_w, v_gdn_conv_w, conv_loc, "adamw_conv")

    dmod_all = small_all[:, :6 * d // LANES, :].reshape(N_DEV, 6 * d)
    dmod_loc = lax.dynamic_slice(dmod_all, (0, me * n_ada), (N_DEV, n_ada))
    g_ada = _ada_wgrad(c_all, dmod_loc, "ada_wgrad")
    ada_o = _adamw(w_ada, m_w_ada, v_w_ada, g_ada, "adamw_ada")

    rc = g["recv"]
    r_ff2 = [rc["ff2a"][0], rc["ff2b"][0]]
    r_ff1 = [rc["ff1a"][0], rc["ff1b_out"][0]]
    r_out, r_in = rc["ff1b_out"][1], rc["in"]
    in_o = _adamw(w_in, m_w_in, v_w_in, r_in, "adamw_w_in", parts=True)
    out_o = _adamw(w_out, m_w_out, v_w_out, r_out, "adamw_w_out", parts=True)
    ff1_o = _adamw(w_ff1, m_w_ff1, v_w_ff1, r_ff1, "adamw_w_ff1", parts=True)
    ff2_o = _adamw(w_ff2, m_w_ff2, v_w_ff2, r_ff2, "adamw_w_ff2", parts=True)

    loss_tot = lax.psum(loss[0, 0], ("x", "y", "c"))

    def leaf(kind):
        return [ada_o[kind], rep_out_d[kind]["b_ada"], rep_out_d[kind]["n1"], rep_out_d[kind]["n2"],
                rep_out_d[kind]["n3"], rep_out_d[kind]["n4"], in_o[kind], rep_out_d[kind]["lb"],
                rep_out_d[kind]["hg_norm"], conv_o[kind], rep_out_d[kind]["alog"], rep_out_d[kind]["dtb"],
                rep_out_d[kind]["gdn_norm"], out_o[kind], ff1_o[kind], ff2_o[kind]]

    rep_out_d = [rep_g_o, rep_d_o, rep_m_o, rep_v_o]
    return (loss_tot, dx[None], *leaf(0), *leaf(1), *leaf(2), *leaf(3))
```

```python
import functools

import jax
import jax.numpy as jnp
from jax import lax
from jax.experimental import pallas as pl
from jax.experimental.pallas import tpu as pltpu

F32 = jnp.float32
BF16 = jnp.bfloat16
HI = lax.Precision.HIGHEST
HIGH = lax.Precision.HIGH

EPS = 1e-6
CHUNK = 64
SB = 16
NSB = CHUNK // SB
HP = 8
HD = 128
CONV_K = 4
N_DEV = 8
LANES = 128
SUBLANES = 8
VMEM_LIMIT = 56 * 1024 * 1024

ADAM_BLOCK_ELEMS = 256 * 1024
ADAM_LR = 0.001
ADAM_B1 = 0.9
ADAM_B2 = 0.999
ADAM_EPS = 1e-08
ADAM_WD = 0.01
ADAM_STEP = 10

ANY = pl.BlockSpec(memory_space=pl.ANY)
MESH = pl.DeviceIdType.MESH


def _cp(sem=None):
    return pltpu.CompilerParams(dimension_semantics=sem, vmem_limit_bytes=VMEM_LIMIT)


def _dot(a, b, dims, precision=None):
    return lax.dot_general(a, b, (dims, ((), ())), precision=precision, preferred_element_type=F32)


def _nn(a, b, precision=None):
    return _dot(a, b, ((1,), (0,)), precision)


def _nt(a, b, precision=None):
    return _dot(a, b, ((1,), (1,)), precision)


def _tn(a, b, precision=None):
    return _dot(a, b, ((0,), (0,)), precision)


def _bf(x):
    return x.astype(BF16)


def _sigmoid(x):
    return 1.0 / (1.0 + jnp.exp(-x))


def _interleave(gens):
    results = [None] * len(gens)
    live = list(range(len(gens)))
    while live:
        for i in list(live):
            try:
                next(gens[i])
            except StopIteration as stop:
                results[i] = stop.value
                live.remove(i)
    return results


def _listed(res):
    return list(res) if isinstance(res, (list, tuple)) else [res]


def _pick(n, pref):
    if n <= pref:
        return n
    t = pref
    while n % t:
        t -= LANES
    assert t > 0, (n, pref)
    return t


def _mm(a, b, mode, out_dtypes, name, epilogue=None, extras=(), tm=1024, tn=2048, tk=1024, comm=None,
        by_cols=False):
    if mode == "nn":
        (m, kd), (_, n) = a.shape, b.shape
    elif mode == "nt":
        (m, kd), (n, _) = a.shape, b.shape
    else:
        (kd, m), (_, n) = a.shape, b.shape
    tm, tn, tk = _pick(m, tm), _pick(n, tn), _pick(kd, tk)
    nk = kd // tk
    if mode == "nn":
        a_spec = pl.BlockSpec((tm, tk), lambda i, j, k: (i, k))
        b_spec = pl.BlockSpec((tk, tn), lambda i, j, k: (k, j))
        dims = ((1,), (0,))
    elif mode == "nt":
        a_spec = pl.BlockSpec((tm, tk), lambda i, j, k: (i, k))
        b_spec = pl.BlockSpec((tn, tk), lambda i, j, k: (j, k))
        dims = ((1,), (1,))
    else:
        a_spec = pl.BlockSpec((tk, tm), lambda i, j, k: (k, i))
        b_spec = pl.BlockSpec((tk, tn), lambda i, j, k: (k, j))
        dims = ((0,), (0,))
    o_spec = pl.BlockSpec((tm, tn), lambda i, j, k: (i, j))
    if by_cols:
        assert epilogue is None and not extras
        res_spec = pl.BlockSpec((None, tm, tn), lambda i, j, k: (j, i, 0))
        res_shape = (n // tn, m, tn)
    else:
        res_spec, res_shape = o_spec, (m, n)
    n_extra, n_out = len(extras), len(out_dtypes)

    gm, gn = m // tm, n // tn
    cn = comm.n if comm is not None else 0

    def body(*refs):
        i, j, k = pl.program_id(0), pl.program_id(1), pl.program_id(2)
        at0 = (j == 0) & (k == 0)
        ins, out_refs, scratch, comm_begin, comm_end = _comm_hooks(
            comm, refs, 2 + n_extra, n_out, (i == 0) & at0, (i == gm - 1) & at0,
            (i == gm - 1) & (j == gn - 1) & (k == nk - 1))
        a_ref, b_ref, extra_refs = ins[0], ins[1], ins[2:]
        acc, = scratch
        comm_begin()

        @pl.when(k == 0)
        def _():
            acc[...] = jnp.zeros_like(acc)

        acc[...] += _dot(a_ref[...], b_ref[...], dims)

        @pl.when(k == nk - 1)
        def _():
            if epilogue is None:
                out_refs[0][...] = acc[...].astype(out_dtypes[0])
            else:
                epilogue(acc[...], extra_refs, out_refs)

        comm_end()

    sem = ("arbitrary",) * 3 if cn else ("parallel", "parallel", "arbitrary")
    outs = pl.pallas_call(
        body, name=name,
        grid=(gm, gn, nk),
        in_specs=[a_spec, b_spec] + [o_spec] * n_extra + [ANY] * cn,
        out_specs=[res_spec] * n_out + [ANY] * cn,
        out_shape=[jax.ShapeDtypeStruct(res_shape, dt) for dt in out_dtypes] + (comm.out_shapes() if cn else []),
        scratch_shapes=[pltpu.VMEM((tm, tn), F32)] + (comm.scratch() if cn else []),
        compiler_params=_cp(sem),
    )(a, b, *extras, *(comm.arrays if cn else []))
    return outs[0] if n_out + cn == 1 else outs


def _row_spec(tb, d):
    return pl.BlockSpec((tb, d), lambda i: (i, 0))


def _vec_spec(d):
    return pl.BlockSpec((1, d), lambda i: (0, 0))


def _prenorm(x, w, sc, sh, name):
    t, d = x.shape
    tb = _pick(t, 256)

    def body(x_ref, w_ref, sc_ref, sh_ref, h_ref, r_ref):
        xv = x_ref[...]
        r = lax.rsqrt(jnp.mean(xv * xv, axis=-1, keepdims=True) + EPS)
        h_ref[...] = ((xv * r * w_ref[...]) * (1.0 + sc_ref[...]) + sh_ref[...]).astype(BF16)
        r_ref[...] = r

    return pl.pallas_call(
        body, name=name, grid=(t // tb,),
        in_specs=[_row_spec(tb, d), _vec_spec(d), _vec_spec(d), _vec_spec(d)],
        out_specs=[_row_spec(tb, d), _row_spec(tb, 1)],
        out_shape=[jax.ShapeDtypeStruct((t, d), BF16), jax.ShapeDtypeStruct((t, 1), F32)],
        compiler_params=_cp(("parallel",)),
    )(x, w, sc, sh)


def _postnorm_res(x, y, w, gt, name):
    t, d = x.shape
    tb = _pick(t, 256)

    def body(x_ref, y_ref, w_ref, gt_ref, o_ref, r_ref):
        yv = y_ref[...]
        r = lax.rsqrt(jnp.mean(yv * yv, axis=-1, keepdims=True) + EPS)
        o_ref[...] = x_ref[...] + gt_ref[...] * (yv * r * w_ref[...])
        r_ref[...] = r

    return pl.pallas_call(
        body, name=name, grid=(t // tb,),
        in_specs=[_row_spec(tb, d), _row_spec(tb, d), _vec_spec(d), _vec_spec(d)],
        out_specs=[_row_spec(tb, d), _row_spec(tb, 1)],
        out_shape=[jax.ShapeDtypeStruct((t, d), F32), jax.ShapeDtypeStruct((t, 1), F32)],
        compiler_params=_cp(("parallel",)),
    )(x, y, w, gt)


def _final_loss(x, y, w, gt, tgt, name):
    t, d = x.shape
    tb = _pick(t, 256)

    def body(x_ref, y_ref, w_ref, gt_ref, tgt_ref, dout_ref, r_ref, loss_ref):
        @pl.when(pl.program_id(0) == 0)
        def _():
            loss_ref[...] = jnp.zeros_like(loss_ref)

        yv = y_ref[...]
        r = lax.rsqrt(jnp.mean(yv * yv, axis=-1, keepdims=True) + EPS)
        out = x_ref[...] + gt_ref[...] * (yv * r * w_ref[...])
        diff = out - tgt_ref[...]
        row = jnp.mean(diff * diff, axis=-1, keepdims=True)
        loss_ref[...] += 0.5 * jnp.sum(row, axis=0, keepdims=True)
        dout_ref[...] = diff * (1.0 / d)
        r_ref[...] = r

    return pl.pallas_call(
        body, name=name, grid=(t // tb,),
        in_specs=[_row_spec(tb, d), _row_spec(tb, d), _vec_spec(d), _vec_spec(d), _row_spec(tb, d)],
        out_specs=[_row_spec(tb, d), _row_spec(tb, 1), pl.BlockSpec((1, 1), lambda i: (0, 0))],
        out_shape=[jax.ShapeDtypeStruct((t, d), F32), jax.ShapeDtypeStruct((t, 1), F32),
                   jax.ShapeDtypeStruct((1, 1), F32)],
        compiler_params=_cp(("arbitrary",)),
    )(x, y, w, gt, tgt)


def _final_loss_bwd(x, y, w, gt, tgt, name):
    t, d = x.shape
    tb = _pick(t, 256)

    def body(x_ref, y_ref, w_ref, gt_ref, tgt_ref, dout_ref, dy_ref, loss_ref, dgt_ref, dw_ref):
        @pl.when(pl.program_id(0) == 0)
        def _():
            loss_ref[...] = jnp.zeros_like(loss_ref)
            dgt_ref[...] = jnp.zeros_like(dgt_ref)
            dw_ref[...] = jnp.zeros_like(dw_ref)

        yv, wv, gtv = y_ref[...], w_ref[...], gt_ref[...]
        r = lax.rsqrt(jnp.mean(yv * yv, axis=-1, keepdims=True) + EPS)
        z = yv * r
        nz = z * wv
        diff = (x_ref[...] + gtv * nz) - tgt_ref[...]
        loss_ref[...] += 0.5 * jnp.sum(jnp.mean(diff * diff, axis=-1, keepdims=True), axis=0, keepdims=True)
        dxv = diff * (1.0 / d)
        dout_ref[...] = dxv
        dgt_ref[...] += jnp.sum(dxv * nz, axis=0, keepdims=True)
        dn = dxv * gtv
        dw_ref[...] += jnp.sum(dn * z, axis=0, keepdims=True)
        dz = dn * wv
        dy_ref[...] = (r * (dz - z * jnp.mean(dz * z, axis=-1, keepdims=True))).astype(BF16)

    return pl.pallas_call(
        body, name=name, grid=(t // tb,),
        in_specs=[_row_spec(tb, d), _row_spec(tb, d), _vec_spec(d), _vec_spec(d), _row_spec(tb, d)],
        out_specs=[_row_spec(tb, d), _row_spec(tb, d), pl.BlockSpec((1, 1), lambda i: (0, 0)),
                   _vec_spec(d), _vec_spec(d)],
        out_shape=[jax.ShapeDtypeStruct((t, d), F32), jax.ShapeDtypeStruct((t, d), BF16),
                   jax.ShapeDtypeStruct((1, 1), F32), jax.ShapeDtypeStruct((1, d), F32),
                   jax.ShapeDtypeStruct((1, d), F32)],
        compiler_params=_cp(("arbitrary",)),
    )(x, y, w, gt, tgt)


def _postnorm_prenorm(x, y, w_post, gt, w_pre, sc, sh, name):
    t, d = x.shape
    tb = _pick(t, 256)

    def body(x_ref, y_ref, wp_ref, gt_ref, wn_ref, sc_ref, sh_ref, x1_ref, r_ref, h_ref, r1_ref):
        yv = y_ref[...]
        r = lax.rsqrt(jnp.mean(yv * yv, axis=-1, keepdims=True) + EPS)
        x1 = x_ref[...] + gt_ref[...] * (yv * r * wp_ref[...])
        r1 = lax.rsqrt(jnp.mean(x1 * x1, axis=-1, keepdims=True) + EPS)
        x1_ref[...] = x1
        r_ref[...] = r
        h_ref[...] = ((x1 * r1 * wn_ref[...]) * (1.0 + sc_ref[...]) + sh_ref[...]).astype(BF16)
        r1_ref[...] = r1

    return pl.pallas_call(
        body, name=name, grid=(t // tb,),
        in_specs=[_row_spec(tb, d), _row_spec(tb, d)] + [_vec_spec(d)] * 5,
        out_specs=[_row_spec(tb, d), _row_spec(tb, 1), _row_spec(tb, d), _row_spec(tb, 1)],
        out_shape=[jax.ShapeDtypeStruct((t, d), F32), jax.ShapeDtypeStruct((t, 1), F32),
                   jax.ShapeDtypeStruct((t, d), BF16), jax.ShapeDtypeStruct((t, 1), F32)],
        compiler_params=_cp(("parallel",)),
    )(x, y, w_post, gt, w_pre, sc, sh)


def _prenorm_postnorm_bwd(dh, x, r_pre, w_pre, sc, dres, y, r_post, w_post, gt, name):
    t, d = x.shape
    tb = _pick(t, 256)

    def body(dh_ref, x_ref, rp_ref, wp_ref, sc_ref, dres_ref, y_ref, rq_ref, wq_ref, gt_ref,
             dx_ref, dy_ref, dsh_ref, dsc_ref, dwp_ref, dgt_ref, dwq_ref):
        @pl.when(pl.program_id(0) == 0)
        def _():
            for ref in (dsh_ref, dsc_ref, dwp_ref, dgt_ref, dwq_ref):
                ref[...] = jnp.zeros_like(ref)

        dhv, rv, wv = dh_ref[...], rp_ref[...], wp_ref[...]
        z = x_ref[...] * rv
        dsh_ref[...] += jnp.sum(dhv, axis=0, keepdims=True)
        dsc_ref[...] += jnp.sum(dhv * (z * wv), axis=0, keepdims=True)
        dzw = dhv * (1.0 + sc_ref[...])
        dwp_ref[...] += jnp.sum(dzw * z, axis=0, keepdims=True)
        dz = dzw * wv
        dxv = dres_ref[...] + rv * (dz - z * jnp.mean(dz * z, axis=-1, keepdims=True))
        dx_ref[...] = dxv

        rq, wq = rq_ref[...], wq_ref[...]
        zq = y_ref[...] * rq
        dgt_ref[...] += jnp.sum(dxv * (zq * wq), axis=0, keepdims=True)
        dn = dxv * gt_ref[...]
        dwq_ref[...] += jnp.sum(dn * zq, axis=0, keepdims=True)
        dzq = dn * wq
        dy_ref[...] = (rq * (dzq - zq * jnp.mean(dzq * zq, axis=-1, keepdims=True))).astype(BF16)

    rs, r1, vs = _row_spec(tb, d), _row_spec(tb, 1), _vec_spec(d)
    return pl.pallas_call(
        body, name=name, grid=(t // tb,),
        in_specs=[rs, rs, r1, vs, vs, rs, rs, r1, vs, vs],
        out_specs=[rs, rs] + [vs] * 5,
        out_shape=[jax.ShapeDtypeStruct((t, d), F32), jax.ShapeDtypeStruct((t, d), BF16)]
        + [jax.ShapeDtypeStruct((1, d), F32)] * 5,
        compiler_params=_cp(("arbitrary",)),
    )(dh, x, r_pre, w_pre, sc, dres, y, r_post, w_post, gt)


def _postnorm_bwd(dxn, y, r, w, gt, name):
    t, d = y.shape
    tb = _pick(t, 256)

    def body(dx_ref, y_ref, r_ref, w_ref, gt_ref, dy_ref, dgt_ref, dw_ref):
        @pl.when(pl.program_id(0) == 0)
        def _():
            dgt_ref[...] = jnp.zeros_like(dgt_ref)
            dw_ref[...] = jnp.zeros_like(dw_ref)

        dxv, rv, wv = dx_ref[...], r_ref[...], w_ref[...]
        z = y_ref[...] * rv
        dgt_ref[...] += jnp.sum(dxv * (z * wv), axis=0, keepdims=True)
        dn = dxv * gt_ref[...]
        dw_ref[...] += jnp.sum(dn * z, axis=0, keepdims=True)
        dz = dn * wv
        dy_ref[...] = (rv * (dz - z * jnp.mean(dz * z, axis=-1, keepdims=True))).astype(BF16)

    return pl.pallas_call(
        body, name=name, grid=(t // tb,),
        in_specs=[_row_spec(tb, d), _row_spec(tb, d), _row_spec(tb, 1), _vec_spec(d), _vec_spec(d)],
        out_specs=[_row_spec(tb, d), _vec_spec(d), _vec_spec(d)],
        out_shape=[jax.ShapeDtypeStruct((t, d), BF16), jax.ShapeDtypeStruct((1, d), F32),
                   jax.ShapeDtypeStruct((1, d), F32)],
        compiler_params=_cp(("arbitrary",)),
    )(dxn, y, r, w, gt)


def _prenorm_bwd(dh, x, r, w, sc, dres, name):
    t, d = x.shape
    tb = _pick(t, 256)

    def body(dh_ref, x_ref, r_ref, w_ref, sc_ref, dres_ref, dx_ref, dsh_ref, dsc_ref, dw_ref):
        @pl.when(pl.program_id(0) == 0)
        def _():
            dsh_ref[...] = jnp.zeros_like(dsh_ref)
            dsc_ref[...] = jnp.zeros_like(dsc_ref)
            dw_ref[...] = jnp.zeros_like(dw_ref)

        dhv, rv, wv = dh_ref[...], r_ref[...], w_ref[...]
        z = x_ref[...] * rv
        dsh_ref[...] += jnp.sum(dhv, axis=0, keepdims=True)
        dsc_ref[...] += jnp.sum(dhv * (z * wv), axis=0, keepdims=True)
        dzw = dhv * (1.0 + sc_ref[...])
        dw_ref[...] += jnp.sum(dzw * z, axis=0, keepdims=True)
        dz = dzw * wv
        dx_ref[...] = dres_ref[...] + rv * (dz - z * jnp.mean(dz * z, axis=-1, keepdims=True))

    return pl.pallas_call(
        body, name=name, grid=(t // tb,),
        in_specs=[_row_spec(tb, d), _row_spec(tb, d), _row_spec(tb, 1), _vec_spec(d), _vec_spec(d),
                  _row_spec(tb, d)],
        out_specs=[_row_spec(tb, d), _vec_spec(d), _vec_spec(d), _vec_spec(d)],
        out_shape=[jax.ShapeDtypeStruct((t, d), F32)] + [jax.ShapeDtypeStruct((1, d), F32)] * 3,
        compiler_params=_cp(("arbitrary",)),
    )(dh, x, r, w, sc, dres)


def _headnorm_fwd(o, proj, g_blk, nw, name):
    t, wd = o.shape
    nh = wd // HD
    tb = _pick(t, 512)
    gb = g_blk * HD // wd

    def body(o_ref, g_ref, nw_ref, out_ref):
        o3 = o_ref[...].reshape(tb, nh, HD)
        g3 = g_ref[...].reshape(tb, nh, HD)
        rh = lax.rsqrt(jnp.mean(o3 * o3, axis=-1, keepdims=True) + EPS)
        res = (o3 * rh * nw_ref[...].reshape(1, 1, HD)) * (g3 * _sigmoid(g3))
        out_ref[...] = res.reshape(tb, wd).astype(BF16)

    return pl.pallas_call(
        body, name=name, grid=(t // tb,),
        in_specs=[_row_spec(tb, wd), pl.BlockSpec((tb, wd), lambda i: (i, gb)), _vec_spec(HD)],
        out_specs=_row_spec(tb, wd),
        out_shape=jax.ShapeDtypeStruct((t, wd), BF16),
        compiler_params=_cp(("parallel",)),
    )(o, proj, nw)


def _headnorm_bwd(dom, col_blk, o, proj, g_blk, nw, name):
    t, wd = o.shape
    nh = wd // HD
    tb = _pick(t, 512)
    gb = g_blk * HD // wd

    def body(do_ref, o_ref, g_ref, nw_ref, dout_ref, dg_ref, dnw_ref):
        @pl.when(pl.program_id(0) == 0)
        def _():
            dnw_ref[...] = jnp.zeros_like(dnw_ref)

        dn = do_ref[...].reshape(tb, nh, HD)
        o3 = o_ref[...].reshape(tb, nh, HD)
        g3 = g_ref[...].reshape(tb, nh, HD)
        nw3 = nw_ref[...].reshape(1, 1, HD)
        rh = lax.rsqrt(jnp.mean(o3 * o3, axis=-1, keepdims=True) + EPS)
        z = o3 * rh
        sg = _sigmoid(g3)
        sl = g3 * sg
        dnw_ref[...] += jnp.sum(jnp.sum(dn * sl * z, axis=1), axis=0, keepdims=True)
        dg_ref[...] = (dn * (z * nw3) * (sg * (1.0 + g3 * (1.0 - sg)))).reshape(tb, wd).astype(BF16)
        dz = dn * sl * nw3
        dout_ref[...] = (rh * (dz - z * jnp.mean(dz * z, axis=-1, keepdims=True))).reshape(tb, wd)

    return pl.pallas_call(
        body, name=name, grid=(t // tb,),
        in_specs=[pl.BlockSpec((tb, wd), lambda i: (i, col_blk)), _row_spec(tb, wd),
                  pl.BlockSpec((tb, wd), lambda i: (i, gb)), _vec_spec(HD)],
        out_specs=[_row_spec(tb, wd), _row_spec(tb, wd), _vec_spec(HD)],
        out_shape=[jax.ShapeDtypeStruct((t, wd), F32), jax.ShapeDtypeStruct((t, wd), BF16),
                   jax.ShapeDtypeStruct((1, HD), F32)],
        compiler_params=_cp(("arbitrary",)),
    )(dom, o, proj, nw)


def _tri(n, kind):
    r = lax.broadcasted_iota(jnp.int32, (n, n), 0)
    c = lax.broadcasted_iota(jnp.int32, (n, n), 1)
    if kind == "lower":
        return r >= c
    if kind == "strict":
        return r > c
    return r <= c


def _hg_gate(fl, l0, l1):
    mx = jnp.maximum(l0, l1)
    e0, e1 = jnp.exp(l0 - mx), jnp.exp(l1 - mx)
    lb = e0 / (e0 + e1)
    sg = _sigmoid(fl)
    f = lb + (1.0 - lb) * sg
    return lb, sg, f


def _hgrn2_fwd(proj, lb_logits, nh, name, comm=None):
    t = proj.shape[0]
    nc = t // CHUNK
    C = CHUNK
    lg = lb_logits.reshape(2, nh, 1, HD)

    hp = min(HP, nh)
    ng = nh // hp

    def one_head(hh, st, q_ref, f_ref, i_ref, lg_ref, p_sc, r_sc):
        sl = slice(hh * HD, (hh + 1) * HD)
        q, v = q_ref[:, sl], i_ref[:, sl]
        _, _, f = _hg_gate(f_ref[:, sl], lg_ref[0, hh], lg_ref[1, hh])
        k = 1.0 - f
        low = _tri(C, "lower")
        b = _nn(low.astype(F32), jnp.log(f), HI)
        yield
        lane_c = lax.broadcasted_iota(jnp.int32, (SB, C), 1)
        lane_h = lax.broadcasted_iota(jnp.int32, (SB, HD), 1)
        row_h = lax.broadcasted_iota(jnp.int32, (SB, HD), 0)
        ones = jnp.ones((HD, HD), F32)

        for i in range(NSB):
            qi, ki, bi = q[SB * i:SB * (i + 1)], k[SB * i:SB * (i + 1)], b[SB * i:SB * (i + 1)]
            for s in range(SB):
                e = jnp.exp(jnp.minimum(bi - bi[s:s + 1], 0.0))
                p = jnp.where(row_h >= s, qi * ki[s:s + 1] * e, 0.0)
                p_sc[hh, pl.ds((i * SB + s) * SB, SB), :] = p
            yield
        r_sc[hh] = _nn(p_sc[hh], ones, HIGH)
        yield
        a_rows = []
        for i in range(NSB):
            acc = jnp.zeros((SB, HD), F32)
            for s in range(SB):
                acc = jnp.where(lane_h == SB * i + s, r_sc[hh, pl.ds((i * SB + s) * SB, SB), :], acc)
            acc = acc[:, :C]
            if i > 0:
                r = b[SB * i - 1:SB * i]
                bi = b[SB * i:SB * (i + 1)]
                qf = q[SB * i:SB * (i + 1)] * jnp.exp(bi - r)
                kf = k * jnp.exp(jnp.minimum(r - b, 0.0))
                acc = acc + jnp.where(lane_c < SB * i, _nt(qf, kf, HIGH), 0.0)
            a_rows.append(acc)
            yield
        a = jnp.concatenate(a_rows, axis=0)
        bl = b[C - 1:C, :]
        o = _nn(_bf(a), _bf(v)) + _nt(_bf(q * jnp.exp(b)), _bf(st))
        yield
        new_st = st * jnp.exp(bl) + _tn(_bf(v), _bf(k * jnp.exp(bl - b)))
        return o, a, new_st

    def body(*refs):
        c, hg = pl.program_id(0), pl.program_id(1)
        step = c * ng + hg
        ins, outs, scratch, comm_begin, comm_end = _comm_hooks(
            comm, refs, 4, 3, step == 0, step == (3 * nc * ng) // 4, step == nc * ng - 1)
        o_ref, a_ref, st_ref = outs
        s_sc, p_sc, r_sc = scratch
        comm_begin()

        @pl.when(c == 0)
        def _():
            for hh in range(hp):
                s_sc[hg * hp + hh] = jnp.zeros((HD, HD), F32)

        sts = [s_sc[hg * hp + hh] for hh in range(hp)]
        res = _interleave([one_head(hh, sts[hh], *ins, p_sc, r_sc) for hh in range(hp)])
        for hh in range(hp):
            o_ref[:, hh * HD:(hh + 1) * HD] = res[hh][0]
            a_ref[0, hh] = res[hh][1]
            st_ref[0, hh] = sts[hh]
            s_sc[hg * hp + hh] = res[hh][2]
        comm_end()

    blk = lambda off: pl.BlockSpec((C, hp * HD), lambda c, g: (c, off // hp + g))
    cn = comm.n if comm is not None else 0
    return pl.pallas_call(
        body, name=name, grid=(nc, ng),
        in_specs=[blk(0), blk(nh), blk(2 * nh),
                  pl.BlockSpec((2, hp, 1, HD), lambda c, g: (0, g, 0, 0))] + [ANY] * cn,
        out_specs=[blk(0),
                   pl.BlockSpec((1, hp, C, C), lambda c, g: (c, g, 0, 0)),
                   pl.BlockSpec((1, hp, HD, HD), lambda c, g: (c, g, 0, 0))] + [ANY] * cn,
        out_shape=[jax.ShapeDtypeStruct((t, nh * HD), F32),
                   jax.ShapeDtypeStruct((nc, nh, C, C), F32),
                   jax.ShapeDtypeStruct((nc, nh, HD, HD), F32)] + (comm.out_shapes() if cn else []),
        scratch_shapes=[pltpu.VMEM((nh, HD, HD), F32), pltpu.VMEM((hp, C * SB, HD), F32),
                        pltpu.VMEM((hp, C * SB, HD), F32)] + (comm.scratch() if cn else []),
        compiler_params=_cp(("arbitrary", "arbitrary")),
    )(proj, proj, proj, lg, *(comm.arrays if cn else []))


def _hgrn2_bwd(proj, lb_logits, do, a_sv, st_sv, nh, name, comm=None):
    t = proj.shape[0]
    nc = t // CHUNK
    C = CHUNK
    lg = lb_logits.reshape(2, nh, 1, HD)
    hp = min(HP, nh)
    ng = nh // hp

    def one_head(hh, dst, q_ref, f_ref, i_ref, lg_ref, do_ref, a_ref, st_ref, p_sc, r_sc):
        sl = slice(hh * HD, (hh + 1) * HD)
        q, v, do_ = q_ref[:, sl], i_ref[:, sl], do_ref[:, sl]
        lb, sg, f = _hg_gate(f_ref[:, sl], lg_ref[0, hh], lg_ref[1, hh])
        k = 1.0 - f
        low = _tri(C, "lower")
        b = _nn(low.astype(F32), jnp.log(f), HI)
        yield
        bl = b[C - 1:C, :]
        eb, ekb = jnp.exp(b), jnp.exp(bl - b)
        qb, kb = q * eb, k * ekb
        a, st = a_ref[0, hh], st_ref[0, hh]

        da = jnp.where(low, _nt(_bf(do_), _bf(v)), 0.0)
        yield
        dv = _tn(_bf(a), _bf(do_)) + _nt(_bf(kb), _bf(dst))
        yield
        dqb = _nn(_bf(do_), _bf(st))
        dkb = _nn(_bf(v), _bf(dst))
        yield

        row = lax.broadcasted_iota(jnp.int32, (C, HD), 0)
        lane_c = lax.broadcasted_iota(jnp.int32, (SB, C), 1)
        row_h = lax.broadcasted_iota(jnp.int32, (SB, HD), 0)
        ones = jnp.ones((HD, HD), F32)
        sel = (lax.broadcasted_iota(jnp.int32, (C, C * SB), 0)
               == jnp.right_shift(lax.broadcasted_iota(jnp.int32, (C, C * SB), 1), SB.bit_length() - 1)).astype(F32)

        for i in range(NSB):
            doi, vi = do_[SB * i:SB * (i + 1)], v[SB * i:SB * (i + 1)]
            for s in range(SB):
                p_sc[hh, pl.ds((i * SB + s) * SB, SB), :] = doi * vi[s:s + 1]
            yield
        r_sc[hh] = _nn(p_sc[hh], ones, HIGH)
        yield
        dq_rows = []
        dk_off = jnp.zeros((C, HD), F32)
        for i in range(NSB):
            qi, ki, bi = q[SB * i:SB * (i + 1)], k[SB * i:SB * (i + 1)], b[SB * i:SB * (i + 1)]
            acc = jnp.zeros((SB, HD), F32)
            for s in range(SB):
                e = jnp.exp(jnp.minimum(bi - bi[s:s + 1], 0.0))
                g = jnp.where(row_h >= s, r_sc[hh, pl.ds((i * SB + s) * SB, SB), :] * e, 0.0)
                acc = acc + g * ki[s:s + 1]
                p_sc[hh, pl.ds((i * SB + s) * SB, SB), :] = g * qi
            yield
            if i > 0:
                r = b[SB * i - 1:SB * i]
                fq = jnp.exp(bi - r)
                fk = jnp.exp(jnp.minimum(r - b, 0.0))
                dai = jnp.where(lane_c < SB * i, da[SB * i:SB * (i + 1)], 0.0)
                acc = acc + _nn(dai, k * fk, HIGH) * fq
                dk_off = dk_off + _tn(dai, qi * fq, HIGH) * fk
                yield
            dq_rows.append(acc)
        dqi = jnp.concatenate(dq_rows, axis=0)
        dq = dqi + dqb * eb
        dk_inter = dkb * ekb
        dk = _nn(sel, p_sc[hh], HIGH) + dk_off + dk_inter
        yield
        db = q * dq - k * dk
        extra = (jnp.sum(k * dk_inter, axis=0, keepdims=True)
                 + jnp.exp(bl) * jnp.sum(dst * st, axis=0, keepdims=True))
        db = db + jnp.where(row == C - 1, extra, 0.0)
        dlf = _nn(_tri(C, "upper").astype(F32), db, HI)
        yield
        df = dlf / f - dk
        dfl = (df * (1.0 - lb) * sg * (1.0 - sg)).astype(BF16)
        dl = jnp.sum(df * (1.0 - sg), axis=0, keepdims=True) * (lb * (1.0 - lb))
        new_dst = dst * jnp.exp(bl) + _tn(_bf(do_), _bf(qb))
        return dq.astype(BF16), dfl, dv.astype(BF16), dl, new_dst

    def body(*refs):
        c, hg = pl.program_id(0), pl.program_id(1)
        step = c * ng + hg
        ins, outs, scratch, comm_begin, comm_end = _comm_hooks(
            comm, refs, 7, 4, step == 0, step == (3 * nc * ng) // 4, step == nc * ng - 1)
        dq_ref, df_ref, di_ref, dl_ref = outs
        ds_sc, p_sc, r_sc = scratch
        comm_begin()

        @pl.when(c == 0)
        def _():
            for hh in range(hp):
                ds_sc[hg * hp + hh] = jnp.zeros((HD, HD), F32)

        @pl.when(step == 0)
        def _():
            dl_ref[...] = jnp.zeros_like(dl_ref)

        dsts = [ds_sc[hg * hp + hh] for hh in range(hp)]
        res = _interleave([one_head(hh, dsts[hh], *ins, p_sc, r_sc) for hh in range(hp)])
        for hh in range(hp):
            sl = slice(hh * HD, (hh + 1) * HD)
            dq_ref[:, sl], df_ref[:, sl], di_ref[:, sl] = res[hh][0], res[hh][1], res[hh][2]
            dl_ref[pl.ds(hg * hp + hh, 1), :] += res[hh][3]
            ds_sc[hg * hp + hh] = res[hh][4]
        comm_end()

    rblk = lambda off: pl.BlockSpec((C, hp * HD), lambda c, g: (nc - 1 - c, off // hp + g))
    oblk = pl.BlockSpec((C, hp * HD), lambda c, g: (nc - 1 - c, g))
    cn = comm.n if comm is not None else 0
    return pl.pallas_call(
        body, name=name, grid=(nc, ng),
        in_specs=[rblk(0), rblk(nh), rblk(2 * nh),
                  pl.BlockSpec((2, hp, 1, HD), lambda c, g: (0, g, 0, 0)),
                  oblk,
                  pl.BlockSpec((1, hp, C, C), lambda c, g: (nc - 1 - c, g, 0, 0)),
                  pl.BlockSpec((1, hp, HD, HD), lambda c, g: (nc - 1 - c, g, 0, 0))] + [ANY] * cn,
        out_specs=[oblk, oblk, oblk, pl.BlockSpec((nh, HD), lambda c, g: (0, 0))] + [ANY] * cn,
        out_shape=[jax.ShapeDtypeStruct((t, nh * HD), BF16)] * 3 + [jax.ShapeDtypeStruct((nh, HD), F32)]
        + (comm.out_shapes() if cn else []),
        scratch_shapes=[pltpu.VMEM((nh, HD, HD), F32), pltpu.VMEM((hp, C * SB, HD), F32),
                        pltpu.VMEM((hp, C * SB, HD), F32)] + (comm.scratch() if cn else []),
        compiler_params=_cp(("arbitrary", "arbitrary")),
    )(proj, proj, proj, lg, do, a_sv, st_sv, *(comm.arrays if cn else []))


def _shift_rows(u, d, row):
    t = u.shape[0]
    if d == 0:
        return u
    rolled = pltpu.roll(u, d % t, 0)
    if d > 0:
        return jnp.where(row >= d, rolled, 0.0)
    return jnp.where(row < t + d, rolled, 0.0)


def _gdn_prep(proj, conv_w, blk0, nh, name):
    t = proj.shape[0]
    scale = HD ** -0.5

    def body(u_ref, w_ref, o_ref):
        j = pl.program_id(0)
        u, w = u_ref[...], w_ref[...]
        row = lax.broadcasted_iota(jnp.int32, (t, HD), 0)
        y = w[CONV_K - 1:CONV_K, :] * u
        for d in range(1, CONV_K):
            y = y + w[CONV_K - 1 - d:CONV_K - d, :] * _shift_rows(u, d, row)
        a = y * _sigmoid(y)
        n = a * lax.rsqrt(jnp.sum(a * a, axis=-1, keepdims=True) + EPS)
        n = n * jnp.where(j < nh, scale, 1.0)
        o_ref[...] = jnp.where(j < 2 * nh, n, a)

    return pl.pallas_call(
        body, name=name, grid=(3 * nh,),
        in_specs=[pl.BlockSpec((t, HD), lambda j: (0, blk0 + j)), pl.BlockSpec((CONV_K, HD), lambda j: (0, j))],
        out_specs=pl.BlockSpec((t, HD), lambda j: (0, j)),
        out_shape=jax.ShapeDtypeStruct((t, 3 * nh * HD), F32),
        compiler_params=_cp(("parallel",)),
    )(proj, conv_w)


def _gdn_prep_bwd(proj, conv_w, dq, dk, dv, blk0, nh, name):
    t = proj.shape[0]
    scale = HD ** -0.5

    def body(u_ref, w_ref, dq_ref, dk_ref, dv_ref, du_ref, dw_ref):
        j = pl.program_id(0)
        u, w = u_ref[...], w_ref[...]
        dout = jnp.where(j < nh, dq_ref[...], jnp.where(j < 2 * nh, dk_ref[...], dv_ref[...]))
        row = lax.broadcasted_iota(jnp.int32, (t, HD), 0)
        us = [_shift_rows(u, d, row) for d in range(CONV_K)]
        y = w[CONV_K - 1:CONV_K, :] * us[0]
        for d in range(1, CONV_K):
            y = y + w[CONV_K - 1 - d:CONV_K - d, :] * us[d]
        sg = _sigmoid(y)
        a = y * sg
        rs = lax.rsqrt(jnp.sum(a * a, axis=-1, keepdims=True) + EPS)
        n = a * rs
        dn = dout * jnp.where(j < nh, scale, 1.0)
        da_n = rs * (dn - n * jnp.sum(dn * n, axis=-1, keepdims=True))
        da = jnp.where(j < 2 * nh, da_n, dout)
        dy = da * (sg * (1.0 + y * (1.0 - sg)))
        du = w[CONV_K - 1:CONV_K, :] * dy
        for d in range(1, CONV_K):
            du = du + w[CONV_K - 1 - d:CONV_K - d, :] * _shift_rows(dy, -d, row)
        du_ref[...] = du.astype(BF16)
        for d in range(CONV_K):
            dw_ref[CONV_K - 1 - d:CONV_K - d, :] = jnp.sum(dy * us[d], axis=0, keepdims=True)

    return pl.pallas_call(
        body, name=name, grid=(3 * nh,),
        in_specs=[pl.BlockSpec((t, HD), lambda j: (0, blk0 + j)), pl.BlockSpec((CONV_K, HD), lambda j: (0, j))]
        + [pl.BlockSpec((t, HD), functools.partial(lambda p, j: (0, jnp.clip(j - p * nh, 0, nh - 1)), p))
           for p in range(3)],
        out_specs=[pl.BlockSpec((t, HD), lambda j: (0, j)), pl.BlockSpec((CONV_K, HD), lambda j: (0, j))],
        out_shape=[jax.ShapeDtypeStruct((t, 3 * nh * HD), BF16), jax.ShapeDtypeStruct((CONV_K, 3 * nh * HD), F32)],
        compiler_params=_cp(("arbitrary",)),
    )(proj, conv_w, dq, dk, dv)


def _gdn_gates(ab, alog, dtb, h, nh):
    lane = lax.broadcasted_iota(jnp.int32, ab.shape, 1)
    x = ab + dtb
    sp = jnp.maximum(x, 0.0) + jnp.log(1.0 + jnp.exp(-jnp.abs(x)))
    ea = jnp.exp(alog)
    la_all = -ea * sp
    beta_all = _sigmoid(ab)
    pick = lambda val, ln: jnp.sum(jnp.where(lane == ln, val, 0.0), axis=1, keepdims=True)
    la = pick(la_all, h)
    beta = pick(beta_all, nh + h)
    dla_da = pick(-ea * _sigmoid(x), h)
    return la, beta, dla_da


def _unit_lower_inverses(ms, C):
    nb = C // SB
    sh = SB.bit_length() - 1
    rowb = jnp.right_shift(lax.broadcasted_iota(jnp.int32, (C, C), 0), sh)
    colb = jnp.right_shift(lax.broadcasted_iota(jnp.int32, (C, C), 1), sh)
    eye = (lax.broadcasted_iota(jnp.int32, (SB, SB), 0) == lax.broadcasted_iota(jnp.int32, (SB, SB), 1)).astype(F32)
    spread = (jnp.bitwise_and(lax.broadcasted_iota(jnp.int32, (SB, C), 1), SB - 1)
              == lax.broadcasted_iota(jnp.int32, (SB, C), 0)).astype(F32)
    blocks = [[m[SB * i:SB * (i + 1), SB * i:SB * (i + 1)] for i in range(nb)] for m in ms]
    xs = [[eye] * nb for _ in ms]
    for s in range(SB - 1):
        xs = [[x - b[:, s:s + 1] * x[s:s + 1, :] for x, b in zip(xh, bh)] for xh, bh in zip(xs, blocks)]
    ts = [jnp.where(rowb == colb, _nn(jnp.concatenate(xh, axis=0), spread, HIGH), 0.0) for xh in xs]
    lvl = 1
    while (1 << lvl) <= nb:
        off = ((jnp.right_shift(rowb, lvl) == jnp.right_shift(colb, lvl))
               & (jnp.right_shift(rowb, lvl - 1) != jnp.right_shift(colb, lvl - 1)))
        ts = [t - _nn(t, _nn(jnp.where(off, m, 0.0), t, HIGH), HIGH) for t, m in zip(ts, ms)]
        lvl += 1
    return ts


def _gdn_chunks(qs, ks, vs, las, betas, C):
    low, strict = _tri(C, "lower"), _tri(C, "strict")
    eye = (lax.broadcasted_iota(jnp.int32, (C, C), 0) == lax.broadcasted_iota(jnp.int32, (C, C), 1)).astype(F32)
    g_bs = [_nn(low.astype(F32), jnp.broadcast_to(la, (C, HD)), HI) for la in las]
    ps = [_nt(k, k, HIGH) for k in ks]
    qks = [_nt(_bf(q), _bf(k)) for q, k in zip(qs, ks)]
    chs = []
    for g_b, p, qk_raw, beta in zip(g_bs, ps, qks, betas):
        g_c = g_b[:, :C]
        gamma = jnp.where(low, jnp.exp(jnp.minimum(g_c - g_c.T, 0.0)), 0.0)
        gl = g_b[C - 1:C, :]
        chs.append(dict(gamma=gamma, eg=jnp.exp(g_b), gl=gl, ekt=jnp.exp(gl - g_b), p=p,
                        m=jnp.where(strict, beta * p * gamma, 0.0), qk_raw=qk_raw))
    xs = _unit_lower_inverses([ch["m"] for ch in chs], C)
    r_ws = [k * (beta * ch["eg"]) for ch, k, beta in zip(chs, ks, betas)]
    uws = [_nn(x, jnp.concatenate([v * beta, r_w], axis=1), HIGH) for x, v, beta, r_w in zip(xs, vs, betas, r_ws)]
    for ch, x, r_w, uw in zip(chs, xs, r_ws, uws):
        ch.update(x=x, r_w=r_w, uw=uw)
    return chs


def _gdn_fwd(qkv, proj, ab_blk, alog, dtb, nh, name, comm=None):
    t = qkv.shape[0]
    nc = t // CHUNK
    C = CHUNK
    hp = min(HP, nh)
    ng = nh // hp

    def body(*refs):
        c, hg = pl.program_id(0), pl.program_id(1)
        step = c * ng + hg
        ins, outs, scratch, comm_begin, comm_end = _comm_hooks(
            comm, refs, 6, 3, step == 0, step == (3 * nc * ng) // 4, step == nc * ng - 1)
        q_ref, k_ref, v_ref, ab_ref, al_ref, dt_ref = ins
        o_ref, x_ref, st_ref = outs
        s_sc, = scratch
        comm_begin()

        @pl.when(c == 0)
        def _():
            for hh in range(hp):
                s_sc[hg * hp + hh] = jnp.zeros((HD, HD), F32)

        sls = [slice(hh * HD, (hh + 1) * HD) for hh in range(hp)]
        qs, ks, vs = [q_ref[:, sl] for sl in sls], [k_ref[:, sl] for sl in sls], [v_ref[:, sl] for sl in sls]
        sts = [s_sc[hg * hp + hh] for hh in range(hp)]
        gates = [_gdn_gates(ab_ref[...], al_ref[...], dt_ref[...], hg * hp + hh, nh) for hh in range(hp)]
        chs = _gdn_chunks(qs, ks, vs, [g[0] for g in gates], [g[1] for g in gates], C)
        stbs = [_bf(st) for st in sts]
        vns = [ch["uw"][:, :HD] - _nt(_bf(ch["uw"][:, HD:]), stb) for ch, stb in zip(chs, stbs)]
        o_st = [_nt(_bf(q * ch["eg"]), stb) for q, ch, stb in zip(qs, chs, stbs)]
        outs_ = [o + _nn(_bf(ch["qk_raw"] * ch["gamma"]), _bf(vn)) for o, ch, vn in zip(o_st, chs, vns)]
        new_sts = [st * jnp.exp(ch["gl"]) + _tn(_bf(vn), _bf(k * ch["ekt"]))
                   for st, ch, vn, k in zip(sts, chs, vns, ks)]
        for hh in range(hp):
            o_ref[:, sls[hh]] = outs_[hh]
            x_ref[0, hh] = chs[hh]["x"]
            st_ref[0, hh] = sts[hh]
            s_sc[hg * hp + hh] = new_sts[hh]
        comm_end()

    blk = lambda off: pl.BlockSpec((C, hp * HD), lambda c, g: (c, off // hp + g))
    vec = pl.BlockSpec((1, HD), lambda c, g: (0, 0))
    cn = comm.n if comm is not None else 0
    return pl.pallas_call(
        body, name=name, grid=(nc, ng),
        in_specs=[blk(0), blk(nh), blk(2 * nh), pl.BlockSpec((C, HD), lambda c, g: (c, ab_blk)), vec, vec]
        + [ANY] * cn,
        out_specs=[blk(0),
                   pl.BlockSpec((1, hp,C, C), lambda c, g: (c, g, 0, 0)),
                   pl.BlockSpec((1, hp,HD, HD), lambda c, g: (c, g, 0, 0))] + [ANY] * cn,
        out_shape=[jax.ShapeDtypeStruct((t, nh * HD), F32),
                   jax.ShapeDtypeStruct((nc, nh, C, C), F32),
                   jax.ShapeDtypeStruct((nc, nh, HD, HD), F32)] + (comm.out_shapes() if cn else []),
        scratch_shapes=[pltpu.VMEM((nh, HD, HD), F32)] + (comm.scratch() if cn else []),
        compiler_params=_cp(("arbitrary", "arbitrary")),
    )(qkv, qkv, qkv, proj, alog, dtb, *(comm.arrays if cn else []))


def _gdn_bwd(qkv, proj, ab_blk, alog, dtb, do, x_sv, st_sv, nh, name, comm=None):
    t = qkv.shape[0]
    nc = t // CHUNK
    C = CHUNK
    hp = min(HP, nh)
    ng = nh // hp

    def one_head(h, hh, dst, q_ref, k_ref, v_ref, ab_ref, al_ref, dt_ref, do_ref, x_ref, st_ref):
        sl = slice(hh * HD, (hh + 1) * HD)
        q, k, v, do_ = q_ref[:, sl], k_ref[:, sl], v_ref[:, sl], do_ref[:, sl]
        la, beta, dla_da = _gdn_gates(ab_ref[...], al_ref[...], dt_ref[...], h, nh)
        low, strict = _tri(C, "lower"), _tri(C, "strict")
        g_b = _nn(low.astype(F32), jnp.broadcast_to(la, (C, HD)), HI)
        yield
        g_c = g_b[:, :C]
        gamma = jnp.where(low, jnp.exp(jnp.minimum(g_c - g_c.T, 0.0)), 0.0)
        eg = jnp.exp(g_b)
        gl = g_b[C - 1:C, :]
        ekt = jnp.exp(gl - g_b)
        egl = jnp.exp(gl)
        p = _nt(k, k, HIGH)
        yield
        x = x_ref[0, hh]
        r_w = k * (beta * eg)
        rhs = jnp.concatenate([v * beta, r_w], axis=1)
        uw = _nn(x, rhs, HIGH)
        yield
        u, w = uw[:, :HD], uw[:, HD:]
        qk_raw = _nt(_bf(q), _bf(k))
        yield
        qk = qk_raw * gamma
        st = st_ref[0, hh]
        stb, dstb = _bf(st), _bf(dst)
        vn = u - _nt(_bf(w), stb)
        yield
        qd, kt = q * eg, k * ekt

        dvn = _tn(_bf(qk), _bf(do_)) + _nt(_bf(kt), dstb)
        yield
        dq2 = jnp.where(low, _nt(_bf(do_), _bf(vn)), 0.0)
        yield
        dqd = _nn(_bf(do_), stb)
        yield
        dkt = _nn(_bf(vn), dstb)
        yield
        dw = -_nn(_bf(dvn), stb)
        yield
        dxx = jnp.concatenate([dvn, dw], axis=1)
        dr = _tn(x, dxx, HIGH)
        yield
        dm = -jnp.where(strict, _nt(dr, uw, HIGH), 0.0)
        yield
        dr_u, dr_w = dr[:, :HD], dr[:, HD:]
        rsum = lambda z: jnp.sum(z, axis=1, keepdims=True)

        dv = dr_u * beta
        dmg = dm * gamma
        dbeta = rsum(dr_u * v) + rsum(dr_w * k) * eg[:, :1] + rsum(dmg * p)
        yield
        dp = dmg * beta
        dq2g = dq2 * gamma
        dk = (dr_w * (beta * eg) + dkt * ekt + _tn(_bf(dq2g), _bf(q))
              + _nn(_bf(dp + dp.T), _bf(k)))
        yield
        dq = dqd * eg + _nn(_bf(dq2g), _bf(k))
        yield
        e = dp * p + dq2g * qk_raw
        t_kt = rsum(dkt * kt)
        dg = rsum(dqd * qd) + rsum(dr_w * r_w) - t_kt + rsum(e) - rsum(e.T)
        yield
        dgl = jnp.sum(t_kt, axis=0, keepdims=True) + jnp.sum(dst * st, keepdims=True) * egl[:, :1]
        rowc = lax.broadcasted_iota(jnp.int32, (C, 1), 0)
        dg = dg + jnp.where(rowc == C - 1, dgl, 0.0)
        dla = _nn(_tri(C, "upper").astype(F32), jnp.broadcast_to(dg, (C, HD)), HI)[:, :1]
        yield
        da = dla * dla_da
        db = dbeta * beta * (1.0 - beta)
        lane = lax.broadcasted_iota(jnp.int32, (C, HD), 1)
        dab = jnp.where(lane == h, da, 0.0) + jnp.where(lane == nh + h, db, 0.0)
        lane1 = lax.broadcasted_iota(jnp.int32, (1, HD), 1)
        d_alog = jnp.where(lane1 == h, jnp.sum(dla * la, axis=0, keepdims=True), 0.0)
        d_dtb = jnp.where(lane1 == h, jnp.sum(da, axis=0, keepdims=True), 0.0)
        new_dst = dst * egl + _tn(_bf(do_), _bf(qd)) - _tn(_bf(dvn), _bf(w))
        return dab, d_alog, d_dtb, new_dst, dq, dk, dv

    def body(*refs):
        c, hg = pl.program_id(0), pl.program_id(1)
        step = c * ng + hg
        ins, outs, scratch, comm_begin, comm_end = _comm_hooks(
            comm, refs, 9, 5, step == 0, step == (3 * nc * ng) // 4, step == nc * ng - 1)
        dq_ref, dk_ref, dv_ref, dab_ref, dpar_ref = outs
        ds_sc, = scratch
        comm_begin()

        @pl.when(c == 0)
        def _():
            for hh in range(hp):
                ds_sc[hg * hp + hh] = jnp.zeros((HD, HD), F32)

        @pl.when(step == 0)
        def _():
            dpar_ref[...] = jnp.zeros_like(dpar_ref)

        @pl.when(hg == 0)
        def _():
            dab_ref[...] = jnp.zeros_like(dab_ref)

        dsts = [ds_sc[hg * hp + hh] for hh in range(hp)]
        res = _interleave([one_head(hg * hp + hh, hh, dsts[hh], *ins) for hh in range(hp)])
        for hh in range(hp):
            sl = slice(hh * HD, (hh + 1) * HD)
            ds_sc[hg * hp + hh] = res[hh][3]
            dq_ref[:, sl], dk_ref[:, sl], dv_ref[:, sl] = res[hh][4], res[hh][5], res[hh][6]
        dab_ref[...] += sum(r[0] for r in res[1:]) + res[0][0]
        dpar_ref[0:1, :] += sum(r[1] for r in res[1:]) + res[0][1]
        dpar_ref[1:2, :] += sum(r[2] for r in res[1:]) + res[0][2]
        comm_end()

    rblk = lambda off: pl.BlockSpec((C, hp * HD), lambda c, g: (nc - 1 - c, off // hp + g))
    oblk = pl.BlockSpec((C, hp * HD), lambda c, g: (nc - 1 - c, g))
    vec = pl.BlockSpec((1, HD), lambda c, g: (0, 0))
    cn = comm.n if comm is not None else 0
    return pl.pallas_call(
        body, name=name, grid=(nc, ng),
        in_specs=[rblk(0), rblk(nh), rblk(2 * nh),
                  pl.BlockSpec((C, HD), lambda c, g: (nc - 1 - c, ab_blk)), vec, vec, oblk,
                  pl.BlockSpec((1, hp,C, C), lambda c, g: (nc - 1 - c, g, 0, 0)),
                  pl.BlockSpec((1, hp,HD, HD), lambda c, g: (nc - 1 - c, g, 0, 0))] + [ANY] * cn,
        out_specs=[oblk, oblk, oblk,
                   pl.BlockSpec((C, HD), lambda c, g: (nc - 1 - c, 0)),
                   pl.BlockSpec((8, HD), lambda c, g: (0, 0))] + [ANY] * cn,
        out_shape=[jax.ShapeDtypeStruct((t, nh * HD), F32)] * 3
        + [jax.ShapeDtypeStruct((t, HD), F32), jax.ShapeDtypeStruct((8, HD), F32)]
        + (comm.out_shapes() if cn else []),
        scratch_shapes=[pltpu.VMEM((nh, HD, HD), F32)] + (comm.scratch() if cn else []),
        compiler_params=_cp(("arbitrary", "arbitrary")),
    )(qkv, qkv, qkv, proj, alog, dtb, do, x_sv, st_sv, *(comm.arrays if cn else []))


def _ada_fwd(c_all, w, b, name):
    nb, d = c_all.shape
    n = w.shape[1]
    tn = _pick(n, 512)

    def body(c_ref, w_ref, b_ref, o_ref):
        cv = c_ref[...]
        o_ref[...] = _nn(cv * _sigmoid(cv), w_ref[...], HI) + b_ref[...]

    return pl.pallas_call(
        body, name=name, grid=(n // tn,),
        in_specs=[pl.BlockSpec((nb, d), lambda j: (0, 0)), pl.BlockSpec((d, tn), lambda j: (0, j)),
                  pl.BlockSpec((1, tn), lambda j: (0, j))],
        out_specs=pl.BlockSpec((nb, tn), lambda j: (0, j)),
        out_shape=jax.ShapeDtypeStruct((nb, n), F32),
        compiler_params=_cp(("parallel",)),
    )(c_all, w, b)


def _ada_wgrad(c_all, dmod, name):
    nb, d = c_all.shape
    n = dmod.shape[1]
    tn = _pick(n, 512)

    def body(c_ref, g_ref, o_ref):
        cv = c_ref[...]
        o_ref[...] = _tn(cv * _sigmoid(cv), g_ref[...], HI)

    return pl.pallas_call(
        body, name=name, grid=(n // tn,),
        in_specs=[pl.BlockSpec((nb, d), lambda j: (0, 0)), pl.BlockSpec((nb, tn), lambda j: (0, j))],
        out_specs=pl.BlockSpec((d, tn), lambda j: (0, j)),
        out_shape=jax.ShapeDtypeStruct((d, n), F32),
        compiler_params=_cp(("parallel",)),
    )(c_all, dmod)


def _adamw(w, m, v, g, name, parts=False):
    lead = w.ndim == 3
    r, cdim = w.shape[-2:]
    cap = max(SUBLANES, ADAM_BLOCK_ELEMS // cdim // SUBLANES * SUBLANES)
    tr = r if r <= cap else _pick_rows(r, cap)
    bc1 = 1.0 - ADAM_B1 ** ADAM_STEP
    bc2 = 1.0 - ADAM_B2 ** ADAM_STEP

    glist = list(g) if isinstance(g, (list, tuple)) else [g]
    bounds = [0]
    for ga in glist:
        bounds.append(bounds[-1] + ga.shape[-2] // tr)

    def body(w_ref, m_ref, v_ref, *rest):
        g_refs, (go_ref, d_ref, mo_ref, vo_ref) = rest[:len(glist)], rest[len(glist):]
        if parts:
            sums = []
            for g_ref in g_refs:
                gv = g_ref[0].astype(F32)
                for s in range(1, N_DEV):
                    gv = gv + g_ref[s].astype(F32)
                sums.append(gv)
            gv = sums[-1]
            for p in range(len(sums) - 2, -1, -1):
                gv = jnp.where(pl.program_id(0) < bounds[p + 1], sums[p], gv)
        else:
            gv = g_refs[0][...]
        wv = w_ref[...]
        mn = ADAM_B1 * m_ref[...] + (1.0 - ADAM_B1) * gv
        vn = ADAM_B2 * v_ref[...] + (1.0 - ADAM_B2) * (gv * gv)
        m_hat = mn / bc1
        v_hat = vn / bc2
        go_ref[...] = gv
        d_ref[...] = -ADAM_LR * (m_hat / (jnp.sqrt(v_hat) + ADAM_EPS) + ADAM_WD * wv)
        mo_ref[...] = mn
        vo_ref[...] = vn

    flat = pl.BlockSpec((tr, cdim), lambda i: (i, 0))
    spec = pl.BlockSpec((None, tr, cdim), lambda i: (0, i, 0)) if lead else flat
    def piece_spec(p):
        lo, n = bounds[p], bounds[p + 1] - bounds[p]
        return pl.BlockSpec((N_DEV, tr, cdim), lambda i: (0, jnp.clip(i - lo, 0, n - 1), 0))

    gspecs = [piece_spec(p) for p in range(len(glist))] if parts else [flat]
    return pl.pallas_call(
        body, name=name, grid=(r // tr,),
        in_specs=[spec, spec, spec] + gspecs,
        out_specs=[spec] * 4,
        out_shape=[jax.ShapeDtypeStruct(w.shape, F32)] * 4,
        compiler_params=_cp(("arbitrary",)),
    )(w, m, v, *glist)


def _pick_rows(r, pref):
    t = pref
    while r % t:
        t -= 8
    assert t > 0
    return t


def _dev_index(x, y, c):
    return 4 * x + 2 * y + c


class _Comm:
    def __init__(self, kind, arrays):
        self.kind, self.n = kind, len(arrays)
        self.arrays = [a[0] if isinstance(a, tuple) else a for a in arrays]
        self.rows = [(a[1], a[2]) if isinstance(a, tuple) else None for a in arrays]

    def out_shapes(self):
        if self.kind == "gather":
            return [jax.ShapeDtypeStruct((N_DEV,) + a.shape, a.dtype) for a in self.arrays]
        return [jax.ShapeDtypeStruct(a.shape if r is None else (N_DEV, r[1]) + a.shape[2:], a.dtype)
                for a, r in zip(self.arrays, self.rows)]

    def scratch(self):
        return [pltpu.SemaphoreType.DMA((self.n, 7)), pltpu.SemaphoreType.DMA((self.n, 7)),
                pltpu.SemaphoreType.DMA((self.n,))]

    def _gather_parts(self, ins, outs, sems):
        send_sems, recv_sems, local_sems = sems
        x, y, c = lax.axis_index("x"), lax.axis_index("y"), lax.axis_index("c")
        me, sibling = (x, y, c), (x, y, 1 - c)
        chips = [(1 - x, y), (x, 1 - y), (1 - x, 1 - y)]

        def copy(a, k, block, to, src=None):
            slot = outs[a].at[_dev_index(*block)]
            return pltpu.make_async_remote_copy(
                src_ref=slot if src is None else src, dst_ref=slot,
                send_sem=send_sems.at[a, k], recv_sem=recv_sems.at[a, k],
                device_id=to, device_id_type=MESH)

        n = self.n
        mine = [pltpu.make_async_copy(ins[a], outs[a].at[_dev_index(*me)], local_sems.at[a]) for a in range(n)]
        first = []
        for a in range(n):
            first.append(copy(a, 0, me, sibling, src=ins[a]))
            first += [copy(a, 1 + j, me, (*chip, c), src=ins[a]) for j, chip in enumerate(chips)]
        landed = [copy(a, 1 + j, (*chip, c), me) for j, chip in enumerate(chips) for a in range(n)]
        passed = [copy(a, 4 + j, (*chip, c), sibling) for j, chip in enumerate(chips) for a in range(n)]
        late = []
        for a in range(n):
            late.append(copy(a, 0, sibling, me))
            late += [copy(a, 4 + j, (*chip, 1 - c), me) for j, chip in enumerate(chips)]
        return mine, first, landed, passed, late

    def _exchange_parts(self, ins, outs, sems):
        send_sems, recv_sems, local_sems = sems
        x, y, c = lax.axis_index("x"), lax.axis_index("y"), lax.axis_index("c")
        my = _dev_index(x, y, c)
        n = self.n

        def block(a, j):
            r = self.rows[a]
            return ins[a].at[j] if r is None else ins[a].at[j, pl.ds(r[0], r[1])]

        mine = [pltpu.make_async_copy(block(a, my), outs[a].at[my], local_sems.at[a]) for a in range(n)]
        sends, recvs = [], []
        for k in range(1, N_DEV):
            px = (1 - x) if (k >> 2) & 1 else x
            py = (1 - y) if (k >> 1) & 1 else y
            pc = (1 - c) if k & 1 else c
            peer = _dev_index(px, py, pc)
            for a in range(n):
                sends.append(pltpu.make_async_remote_copy(
                    src_ref=block(a, peer), dst_ref=outs[a].at[my],
                    send_sem=send_sems.at[a, k - 1], recv_sem=recv_sems.at[a, k - 1],
                    device_id=(px, py, pc), device_id_type=MESH))
                recvs.append(pltpu.make_async_remote_copy(
                    src_ref=block(a, my), dst_ref=outs[a].at[peer],
                    send_sem=send_sems.at[a, k - 1], recv_sem=recv_sems.at[a, k - 1],
                    device_id=(x, y, c), device_id_type=MESH))
        return mine, sends, recvs

    def start(self, ins, outs, sems):
        if self.kind == "gather":
            mine, first, _, _, _ = self._gather_parts(ins, outs, sems)
        else:
            mine, first, _ = self._exchange_parts(ins, outs, sems)
        for cp in mine + first:
            cp.start()

    def mid(self, ins, outs, sems):
        if self.kind == "gather":
            _, _, landed, passed, _ = self._gather_parts(ins, outs, sems)
            for got, fwd in zip(landed, passed):
                got.wait_recv()
                fwd.start()

    def finish(self, ins, outs, sems):
        if self.kind == "gather":
            mine, first, _, passed, late = self._gather_parts(ins, outs, sems)
            for cp in late:
                cp.wait_recv()
            for cp in first + passed:
                cp.wait_send()
        else:
            mine, sends, recvs = self._exchange_parts(ins, outs, sems)
            for cp in sends:
                cp.wait_send()
            for cp in recvs:
                cp.wait_recv()
        for cp in mine:
            cp.wait()

    def run(self, name):
        n = self.n

        def body(*refs):
            ins, outs, sems = refs[:n], refs[n:2 * n], refs[2 * n:]
            self.start(ins, outs, sems)
            self.mid(ins, outs, sems)
            self.finish(ins, outs, sems)

        return pl.pallas_call(
            body, name=name, in_specs=[ANY] * n, out_specs=[ANY] * n,
            out_shape=self.out_shapes(), scratch_shapes=self.scratch(),
        )(*self.arrays)


def _all_gather(arrays, name):
    return _Comm("gather", arrays).run(name)


def _comm_hooks(comm, refs, n_in, n_out, first, middle, last):
    cn = comm.n if comm is not None else 0
    ins, cins = refs[:n_in], refs[n_in:n_in + cn]
    outs, couts = refs[n_in + cn:n_in + cn + n_out], refs[n_in + cn + n_out:n_in + 2 * cn + n_out]
    rest = refs[n_in + 2 * cn + n_out:]
    scratch, csems = (rest[:len(rest) - 3], rest[len(rest) - 3:]) if cn else (rest, ())

    def begin():
        if cn:
            pl.when(first)(lambda: comm.start(cins, couts, csems))
            pl.when(middle)(lambda: comm.mid(cins, couts, csems))

    def end():
        if cn:
            pl.when(last)(lambda: comm.finish(cins, couts, csems))

    return ins, outs, scratch, begin, end


def _local_step(x, tgt, mod, n1, n2, n3, n4, w_in_p, lb_logits, hg_norm, conv_w, alog, dtb, gdn_norm,
                late_w, dist=None):
    t, d = x.shape
    nh = d // 2 // HD
    ab_blk = 8 * nh
    sh_m, sc_m, gt_m, sh_f, sc_f, gt_f = [mod[i:i + 1] for i in range(6)]

    h1, r1 = _prenorm(x, n1, sc_m, sh_m, "prenorm_mix")
    if dist is None:
        proj = _mm(h1, w_in_p, "nn", [F32], "mm_proj")
        o_hg, a_sv, hst_sv = _hgrn2_fwd(proj, lb_logits, nh, "hgrn2_fwd")
        qkv = _gdn_prep(proj, conv_w, 4 * nh, nh, "gdn_prep")
        o_gd, x_sv, gst_sv = _gdn_fwd(qkv, proj, ab_blk, alog, dtb, nh, "gdn_fwd")
        w_out, w_ff1, w_ff2 = late_w
        exch = lambda arrays: None
    else:
        proj, g_ff2 = _mm(h1, w_in_p, "nn", [F32], "mm_proj", comm=_Comm("gather", late_w[2:]))
        o_hg, a_sv, hst_sv, g_out = _hgrn2_fwd(proj, lb_logits, nh, "hgrn2_fwd",
                                               comm=_Comm("gather", late_w[:1]))
        qkv = _gdn_prep(proj, conv_w, 4 * nh, nh, "gdn_prep")
        o_gd, x_sv, gst_sv, g_ff1 = _gdn_fwd(qkv, proj, ab_blk, alog, dtb, nh, "gdn_fwd",
                                             comm=_Comm("gather", late_w[1:2]))
        w_out, w_ff1, w_ff2 = dist["assemble"](g_out, g_ff1, g_ff2)
        exch = lambda arrays: _Comm("exchange", arrays)
    om_hg = _headnorm_fwd(o_hg, proj, 3 * nh, hg_norm, "headnorm_hg")
    om_gd = _headnorm_fwd(o_gd, proj, 7 * nh, gdn_norm, "headnorm_gdn")
    om = jnp.concatenate([om_hg, om_gd], axis=1)
    y1 = _mm(om, w_out, "nn", [F32], "mm_out")
    x1, r2, h2, r3 = _postnorm_prenorm(x, y1, n2, gt_m, n3, sc_f, sh_f, "postnorm_mix_prenorm_ffn")

    def relu2(acc, extra, outs):
        rl = jnp.maximum(acc, 0.0)
        outs[0][...] = (rl * rl).astype(BF16)

    act = _mm(h2, w_ff1, "nn", [BF16], "mm_ff1", epilogue=relu2)
    y2 = _mm(act, w_ff2, "nn", [F32], "mm_ff2")
    dout, dy2, loss, dgt_f, dn4 = _final_loss_bwd(x1, y2, n4, gt_f, tgt, "final_loss_bwd")
    dw_ff2 = _mm(act, dy2, "tn", [BF16], "mm_dw_ff2")

    def drelu2(acc, extra, outs):
        outs[0][...] = (acc * (2.0 * jnp.sqrt(extra[0][...].astype(F32)))).astype(BF16)

    recv = {}
    ff2a, ff2b = dist["parts_ff2"](dw_ff2) if dist else (None, None)
    du, *recv["ff2a"] = _listed(_mm(dy2, w_ff2, "nt", [BF16], "mm_da", epilogue=drelu2, extras=(act,),
                                    comm=exch([ff2a])))
    ff1_cols = dict(by_cols=True, tn=dist["n_ff"]) if dist else {}
    dw_ff1, *recv["ff2b"] = _listed(_mm(h2, du, "tn", [BF16], "mm_dw_ff1", comm=exch([ff2b]), **ff1_cols))
    ff1a, ff1b = dist["parts_ff1"](dw_ff1) if dist else (None, None)
    dh2, *recv["ff1a"] = _listed(_mm(du, w_ff1, "nt", [F32], "mm_dh2", comm=exch([ff1a])))
    dx1, dy1, dsh_f, dsc_f, dn3, dgt_m, dn2 = _prenorm_postnorm_bwd(
        dh2, x1, r3, n3, sc_f, dout, y1, r2, n2, gt_m, "prenorm_ffn_postnorm_mix_bwd")

    dw_out = _mm(om, dy1, "tn", [BF16], "mm_dw_out")
    dom = _mm(dy1, w_out, "nt", [F32], "mm_dom")
    do_hg, dg_hg, dhgn = _headnorm_bwd(dom, 0, o_hg, proj, 3 * nh, hg_norm, "headnorm_hg_bwd")
    do_gd, dg_gd, dgdn = _headnorm_bwd(dom, 1, o_gd, proj, 7 * nh, gdn_norm, "headnorm_gdn_bwd")
    p_out = dist["parts_out"](dw_out) if dist else None
    dq_hg, df_hg, di_hg, dl0, *recv["ff1b_out"] = _hgrn2_bwd(proj, lb_logits, do_hg, a_sv, hst_sv, nh,
                                                             "hgrn2_bwd", comm=exch([ff1b, p_out]))
    dq_g, dk_g, dv_g, dab, dpar = _gdn_bwd(qkv, proj, ab_blk, alog, dtb, do_gd, x_sv, gst_sv, nh, "gdn_bwd")
    du_conv, dconv = _gdn_prep_bwd(proj, conv_w, dq_g, dk_g, dv_g, 4 * nh, nh, "gdn_prep_bwd")
    dproj = jnp.concatenate([dq_hg, df_hg, di_hg, dg_hg, du_conv, dg_gd, dab.astype(BF16)], axis=1)
    if dist is None:
        dw_in = _mm(h1, dproj, "tn", [BF16], "mm_dw_in")
        dh1 = _mm(dproj, w_in_p, "nt", [F32], "mm_dh1", tk=1664)
    else:
        q4 = d // 4
        dw_in_a = _mm(h1[:, :q4], dproj, "tn", [BF16], "mm_dw_in_a")
        dw_in_b, in_a = _mm(h1[:, q4:2 * q4], dproj, "tn", [BF16], "mm_dw_in_b",
                            comm=exch([dist["parts_in"](dw_in_a)]))
        dw_in_c, in_b = _mm(h1[:, 2 * q4:], dproj, "tn", [BF16], "mm_dw_in_c",
                            comm=exch([dist["parts_in"](dw_in_b)]))
        dh1, in_c = _mm(dproj, w_in_p, "nt", [F32], "mm_dh1", tk=1664, comm=exch([dist["parts_in"](dw_in_c)]))
        recv["in"] = [in_a, in_b, in_c]
        dw_in = None
    dx, dsh_m, dsc_m, dn1 = _prenorm_bwd(dh1, x, r1, n1, sc_m, dx1, "prenorm_mix_bwd")

    dmod = jnp.concatenate([dsh_m, dsc_m, dgt_m, dsh_f, dsc_f, dgt_f], axis=0)
    grads = dict(dmod=dmod, n1=dn1, n2=dn2, n3=dn3, n4=dn4, w_in=dw_in, lb0=dl0, hg_norm=dhgn, conv=dconv,
                 alog=dpar[0:1], dtb=dpar[1:2], gdn_norm=dgdn, w_out=dw_out, w_ff1=dw_ff1, w_ff2=dw_ff2,
                 recv=recv)
    return loss, dx, grads


def _pack(vals):
    rows = []
    for vv in vals:
        flat = vv.reshape(-1)
        flat = jnp.pad(flat, (0, (-flat.shape[0]) % (SUBLANES * LANES)))
        rows.append(flat.reshape(-1, LANES))
    return jnp.concatenate(rows, axis=0)


def _unpack(packed, shapes):
    out, r = [], 0
    for shp in shapes:
        size = 1
        for s in shp:
            size *= s
        nr = -(-size // (SUBLANES * LANES)) * SUBLANES
        out.append(packed[r:r + nr].reshape(-1)[:size].reshape(shp))
        r += nr
    return out


def _sum_parts(parts, name):
    _, r, cdim = parts.shape

    def body(p_ref, o_ref):
        acc = p_ref[0]
        for s in range(1, N_DEV):
            acc = acc + p_ref[s]
        o_ref[...] = acc

    return pl.pallas_call(
        body, name=name,
        out_shape=jax.ShapeDtypeStruct((r, cdim), F32),
        compiler_params=_cp(),
    )(parts)


def kernel(x, c, w_ada, b_ada, pre_mix_norm, post_mix_norm, pre_ffn_norm, post_ffn_norm, w_in, hg_lb_logits, hg_norm, gdn_conv_w, gdn_a_log, gdn_dt_bias, gdn_norm, w_out, w_ff1, w_ff2, loss_target, m_w_ada, m_b_ada, m_pre_mix_norm, m_post_mix_norm, m_pre_ffn_norm, m_post_ffn_norm, m_w_in, m_hg_lb_logits, m_hg_norm, m_gdn_conv_w, m_gdn_a_log, m_gdn_dt_bias, m_gdn_norm, m_w_out, m_w_ff1, m_w_ff2, v_w_ada, v_b_ada, v_pre_mix_norm, v_post_mix_norm, v_pre_ffn_norm, v_post_ffn_norm, v_w_in, v_hg_lb_logits, v_hg_norm, v_gdn_conv_w, v_gdn_a_log, v_gdn_dt_bias, v_gdn_norm, v_w_out, v_w_ff1, v_w_ff2):
    t, d = x.shape[1], x.shape[2]
    nh = d // 2 // HD
    in_cols = w_in.shape[2] * N_DEV
    main = in_cols - 2 * nh
    me = _dev_index(lax.axis_index("x"), lax.axis_index("y"), lax.axis_index("c"))

    c_all, conv_g = _all_gather([c, gdn_conv_w[0]], "gather_small")
    c_all = c_all.reshape(N_DEV, d)
    conv_full = conv_g.transpose(1, 0, 2).reshape(CONV_K, -1)
    w_in_g = _all_gather([w_in[0].astype(BF16)], "gather_w_in")[0]
    w_in_full = w_in_g.transpose(1, 0, 2).reshape(d, in_cols)
    w_in_p = jnp.concatenate([w_in_full, jnp.zeros((d, LANES - 2 * nh), BF16)], axis=1)
    late_w = [w_out[0].astype(BF16), w_ff1[0].astype(BF16), w_ff2[0].astype(BF16)]

    n_in = w_in.shape[2]
    n_ff = w_ff1.shape[2]

    def halves(p):
        r = p.shape[1] // 2
        return (p, 0, r), (p, r, r)

    dist = dict(
        assemble=lambda g_out, g_ff1, g_ff2: (g_out.reshape(d, d), g_ff1.transpose(1, 0, 2).reshape(d, -1),
                                              g_ff2.reshape(-1, d)),
        n_ff=n_ff,
        parts_ff2=lambda dw: halves(dw.reshape(N_DEV, -1, d)),
        parts_ff1=halves,
        parts_out=lambda dw: dw.reshape(N_DEV, d // N_DEV, d),
        parts_in=lambda dw: dw[:, :in_cols].reshape(dw.shape[0], N_DEV, n_in).transpose(1, 0, 2),
    )

    n_ada = w_ada.shape[2]
    b_loc = lax.dynamic_slice(b_ada, (0, me * n_ada), (1, n_ada))
    mod_part = _ada_fwd(c_all, w_ada[0], b_loc, "ada_fwd")
    mod_all = _all_gather([mod_part], "gather_mod")[0]
    mod = lax.dynamic_slice(mod_all, (0, me, 0), (N_DEV, 1, n_ada)).reshape(6, d)

    pad_lane = lambda vv: jnp.concatenate([vv, jnp.zeros((1, LANES - vv.shape[1]), F32)], axis=1)
    loss, dx, g = _local_step(
        x[0], loss_target[0], mod, pre_mix_norm, post_mix_norm, pre_ffn_norm, post_ffn_norm, w_in_p,
        hg_lb_logits, hg_norm, conv_full, pad_lane(gdn_a_log), pad_lane(gdn_dt_bias), gdn_norm,
        late_w, dist)

    rep_names = ["b_ada", "n1", "n2", "n3", "n4", "lb", "hg_norm", "alog", "dtb", "gdn_norm"]
    rep_w = [b_ada, pre_mix_norm, post_mix_norm, pre_ffn_norm, post_ffn_norm, hg_lb_logits, hg_norm,
             gdn_a_log, gdn_dt_bias, gdn_norm]
    rep_m = [m_b_ada, m_pre_mix_norm, m_post_mix_norm, m_pre_ffn_norm, m_post_ffn_norm, m_hg_lb_logits,
             m_hg_norm, m_gdn_a_log, m_gdn_dt_bias, m_gdn_norm]
    rep_v = [v_b_ada, v_pre_mix_norm, v_post_mix_norm, v_pre_ffn_norm, v_post_ffn_norm, v_hg_lb_logits,
             v_hg_norm, v_gdn_a_log, v_gdn_dt_bias, v_gdn_norm]
    rep_shapes = [a.shape for a in rep_w]
    g_lb = jnp.stack([g["lb0"], -g["lb0"]], axis=0)
    rep_g = [g["dmod"], g["n1"], g["n2"], g["n3"], g["n4"], g_lb, g["hg_norm"],
             g["alog"][:, :nh], g["dtb"][:, :nh], g["gdn_norm"]]
    small = _pack(rep_g + [g["conv"]])
    n_rep_rows = _pack(rep_g).shape[0]
    pad_rows = (-small.shape[0]) % 8
    if pad_rows:
        small = jnp.concatenate([small, jnp.zeros((pad_rows, LANES), F32)], axis=0)
    small_all = _all_gather([small], "gather_small_grads")[0]
    small_sum = _sum_parts(small_all, "sum_small_grads")
    rep_out = _adamw(_pack(rep_w), _pack(rep_m), _pack(rep_v), small_sum[:n_rep_rows], "adamw_small")
    rep_g_o, rep_d_o, rep_m_o, rep_v_o = [dict(zip(rep_names, _unpack(p, rep_shapes))) for p in rep_out]

    conv_sum = small_sum[n_rep_rows:n_rep_rows + CONV_K * conv_full.shape[1] // LANES].reshape(CONV_K, -1)
    n_conv = gdn_conv_w.shape[2]
    conv_loc = lax.dynamic_slice(conv_sum, (0, me * n_conv), (CONV_K, n_conv))
    conv_o = _adamw(gdn_conv_w, m_gdn_conv_w, v_gdn_conv_w, conv_loc, "adamw_conv")

    dmod_all = small_all[:, :6 * d // LANES, :].reshape(N_DEV, 6 * d)
    dmod_loc = lax.dynamic_slice(dmod_all, (0, me * n_ada), (N_DEV, n_ada))
    g_ada = _ada_wgrad(c_all, dmod_loc, "ada_wgrad")
    ada_o = _adamw(w_ada, m_w_ada, v_w_ada, g_ada, "adamw_ada")

    rc = g["recv"]
    r_ff2 = [rc["ff2a"][0], rc["ff2b"][0]]
    r_ff1 = [rc["ff1a"][0], rc["ff1b_out"][0]]
    r_out, r_in = rc["ff1b_out"][1], rc["in"]
    in_o = _adamw(w_in, m_w_in, v_w_in, r_in, "adamw_w_in", parts=True)
    out_o = _adamw(w_out, m_w_out, v_w_out, r_out, "adamw_w_out", parts=True)
    ff1_o = _adamw(w_ff1, m_w_ff1, v_w_ff1, r_ff1, "adamw_w_ff1", parts=True)
    ff2_o = _adamw(w_ff2, m_w_ff2, v_w_ff2, r_ff2, "adamw_w_ff2", parts=True)

    loss_tot = lax.psum(loss[0, 0], ("x", "y", "c"))

    def leaf(kind):
        return [ada_o[kind], rep_out_d[kind]["b_ada"], rep_out_d[kind]["n1"], rep_out_d[kind]["n2"],
                rep_out_d[kind]["n3"], rep_out_d[kind]["n4"], in_o[kind], rep_out_d[kind]["lb"],
                rep_out_d[kind]["hg_norm"], conv_o[kind], rep_out_d[kind]["alog"], rep_out_d[kind]["dtb"],
                rep_out_d[kind]["gdn_norm"], out_o[kind], ff1_o[kind], ff2_o[kind]]

    rep_out_d = [rep_g_o, rep_d_o, rep_m_o, rep_v_o]
    return (loss_tot, dx[None], *leaf(0), *leaf(1), *leaf(2), *leaf(3))
```

```python
import functools

import jax
import jax.numpy as jnp
from jax import lax
from jax.experimental import pallas as pl
from jax.experimental.pallas import tpu as pltpu

F32 = jnp.float32
BF16 = jnp.bfloat16
HI = lax.Precision.HIGHEST
HIGH = lax.Precision.HIGH

EPS = 1e-6
CHUNK = 64
SB = 16
NSB = CHUNK // SB
HP = 8
HD = 128
CONV_K = 4
N_DEV = 8
LANES = 128
SUBLANES = 8
VMEM_LIMIT = 56 * 1024 * 1024

ADAM_BLOCK_ELEMS = 256 * 1024
ADAM_LR = 0.001
ADAM_B1 = 0.9
ADAM_B2 = 0.999
ADAM_EPS = 1e-08
ADAM_WD = 0.01
ADAM_STEP = 10

ANY = pl.BlockSpec(memory_space=pl.ANY)
MESH = pl.DeviceIdType.MESH


def _cp(sem=None):
    return pltpu.CompilerParams(dimension_semantics=sem, vmem_limit_bytes=VMEM_LIMIT)


def _dot(a, b, dims, precision=None):
    return lax.dot_general(a, b, (dims, ((), ())), precision=precision, preferred_element_type=F32)


def _nn(a, b, precision=None):
    return _dot(a, b, ((1,), (0,)), precision)


def _nt(a, b, precision=None):
    return _dot(a, b, ((1,), (1,)), precision)


def _tn(a, b, precision=None):
    return _dot(a, b, ((0,), (0,)), precision)


def _bf(x):
    return x.astype(BF16)


def _sigmoid(x):
    return 1.0 / (1.0 + jnp.exp(-x))


def _interleave(gens):
    results = [None] * len(gens)
    live = list(range(len(gens)))
    while live:
        for i in list(live):
            try:
                next(gens[i])
            except StopIteration as stop:
                results[i] = stop.value
                live.remove(i)
    return results


def _listed(res):
    return list(res) if isinstance(res, (list, tuple)) else [res]


def _pick(n, pref):
    if n <= pref:
        return n
    t = pref
    while n % t:
        t -= LANES
    assert t > 0, (n, pref)
    return t


def _mm(a, b, mode, out_dtypes, name, epilogue=None, extras=(), tm=1024, tn=2048, tk=1024, comm=None,
        by_cols=False, a_col0=0):
    if mode == "nn":
        m, (kd, n) = a.shape[0], b.shape
    elif mode == "nt":
        (m, kd), (n, _) = a.shape, b.shape
    else:
        (kd, m), (_, n) = a.shape, b.shape
    tm, tn, tk = _pick(m, tm), _pick(n, tn), _pick(kd, tk)
    nk = kd // tk
    assert a_col0 % tk == 0
    k0 = a_col0 // tk
    if mode == "nn":
        a_spec = pl.BlockSpec((tm, tk), lambda i, j, k: (i, k + k0))
        b_spec = pl.BlockSpec((tk, tn), lambda i, j, k: (k, j))
        dims = ((1,), (0,))
    elif mode == "nt":
        a_spec = pl.BlockSpec((tm, tk), lambda i, j, k: (i, k))
        b_spec = pl.BlockSpec((tn, tk), lambda i, j, k: (j, k))
        dims = ((1,), (1,))
    else:
        a_spec = pl.BlockSpec((tk, tm), lambda i, j, k: (k, i))
        b_spec = pl.BlockSpec((tk, tn), lambda i, j, k: (k, j))
        dims = ((0,), (0,))
    o_spec = pl.BlockSpec((tm, tn), lambda i, j, k: (i, j))
    if by_cols:
        assert epilogue is None and not extras
        res_spec = pl.BlockSpec((None, tm, tn), lambda i, j, k: (j, i, 0))
        res_shape = (n // tn, m, tn)
    else:
        res_spec, res_shape = o_spec, (m, n)
    n_extra, n_out = len(extras), len(out_dtypes)

    gm, gn = m // tm, n // tn
    cn = comm.n if comm is not None else 0

    def body(*refs):
        i, j, k = pl.program_id(0), pl.program_id(1), pl.program_id(2)
        at0 = (j == 0) & (k == 0)
        ins, out_refs, scratch, comm_begin, comm_end = _comm_hooks(
            comm, refs, 2 + n_extra, n_out, (i == 0) & at0, (i == gm - 1) & at0,
            (i == gm - 1) & (j == gn - 1) & (k == nk - 1))
        a_ref, b_ref, extra_refs = ins[0], ins[1], ins[2:]
        acc, = scratch
        comm_begin()

        @pl.when(k == 0)
        def _():
            acc[...] = jnp.zeros_like(acc)

        acc[...] += _dot(a_ref[...], b_ref[...], dims)

        @pl.when(k == nk - 1)
        def _():
            if epilogue is None:
                out_refs[0][...] = acc[...].astype(out_dtypes[0])
            else:
                epilogue(acc[...], extra_refs, out_refs)

        comm_end()

    sem = ("arbitrary",) * 3 if cn else ("parallel", "parallel", "arbitrary")
    outs = pl.pallas_call(
        body, name=name,
        grid=(gm, gn, nk),
        in_specs=[a_spec, b_spec] + [o_spec] * n_extra + [ANY] * cn,
        out_specs=[res_spec] * n_out + [ANY] * cn,
        out_shape=[jax.ShapeDtypeStruct(res_shape, dt) for dt in out_dtypes] + (comm.out_shapes() if cn else []),
        scratch_shapes=[pltpu.VMEM((tm, tn), F32)] + (comm.scratch() if cn else []),
        compiler_params=_cp(sem),
    )(a, b, *extras, *(comm.arrays if cn else []))
    return outs[0] if n_out + cn == 1 else outs


def _row_spec(tb, d):
    return pl.BlockSpec((tb, d), lambda i: (i, 0))


def _vec_spec(d):
    return pl.BlockSpec((1, d), lambda i: (0, 0))


def _prenorm(x, w, sc, sh, name):
    t, d = x.shape
    tb = _pick(t, 256)

    def body(x_ref, w_ref, sc_ref, sh_ref, h_ref, r_ref):
        xv = x_ref[...]
        r = lax.rsqrt(jnp.mean(xv * xv, axis=-1, keepdims=True) + EPS)
        h_ref[...] = ((xv * r * w_ref[...]) * (1.0 + sc_ref[...]) + sh_ref[...]).astype(BF16)
        r_ref[...] = r

    return pl.pallas_call(
        body, name=name, grid=(t // tb,),
        in_specs=[_row_spec(tb, d), _vec_spec(d), _vec_spec(d), _vec_spec(d)],
        out_specs=[_row_spec(tb, d), _row_spec(tb, 1)],
        out_shape=[jax.ShapeDtypeStruct((t, d), BF16), jax.ShapeDtypeStruct((t, 1), F32)],
        compiler_params=_cp(("parallel",)),
    )(x, w, sc, sh)


def _postnorm_res(x, y, w, gt, name):
    t, d = x.shape
    tb = _pick(t, 256)

    def body(x_ref, y_ref, w_ref, gt_ref, o_ref, r_ref):
        yv = y_ref[...]
        r = lax.rsqrt(jnp.mean(yv * yv, axis=-1, keepdims=True) + EPS)
        o_ref[...] = x_ref[...] + gt_ref[...] * (yv * r * w_ref[...])
        r_ref[...] = r

    return pl.pallas_call(
        body, name=name, grid=(t // tb,),
        in_specs=[_row_spec(tb, d), _row_spec(tb, d), _vec_spec(d), _vec_spec(d)],
        out_specs=[_row_spec(tb, d), _row_spec(tb, 1)],
        out_shape=[jax.ShapeDtypeStruct((t, d), F32), jax.ShapeDtypeStruct((t, 1), F32)],
        compiler_params=_cp(("parallel",)),
    )(x, y, w, gt)


def _final_loss(x, y, w, gt, tgt, name):
    t, d = x.shape
    tb = _pick(t, 256)

    def body(x_ref, y_ref, w_ref, gt_ref, tgt_ref, dout_ref, r_ref, loss_ref):
        @pl.when(pl.program_id(0) == 0)
        def _():
            loss_ref[...] = jnp.zeros_like(loss_ref)

        yv = y_ref[...]
        r = lax.rsqrt(jnp.mean(yv * yv, axis=-1, keepdims=True) + EPS)
        out = x_ref[...] + gt_ref[...] * (yv * r * w_ref[...])
        diff = out - tgt_ref[...]
        row = jnp.mean(diff * diff, axis=-1, keepdims=True)
        loss_ref[...] += 0.5 * jnp.sum(row, axis=0, keepdims=True)
        dout_ref[...] = diff * (1.0 / d)
        r_ref[...] = r

    return pl.pallas_call(
        body, name=name, grid=(t // tb,),
        in_specs=[_row_spec(tb, d), _row_spec(tb, d), _vec_spec(d), _vec_spec(d), _row_spec(tb, d)],
        out_specs=[_row_spec(tb, d), _row_spec(tb, 1), pl.BlockSpec((1, 1), lambda i: (0, 0))],
        out_shape=[jax.ShapeDtypeStruct((t, d), F32), jax.ShapeDtypeStruct((t, 1), F32),
                   jax.ShapeDtypeStruct((1, 1), F32)],
        compiler_params=_cp(("arbitrary",)),
    )(x, y, w, gt, tgt)


def _final_loss_bwd(x, y, w, gt, tgt, name):
    t, d = x.shape
    tb = _pick(t, 256)

    def body(x_ref, y_ref, w_ref, gt_ref, tgt_ref, dout_ref, dy_ref, loss_ref, dgt_ref, dw_ref):
        @pl.when(pl.program_id(0) == 0)
        def _():
            loss_ref[...] = jnp.zeros_like(loss_ref)
            dgt_ref[...] = jnp.zeros_like(dgt_ref)
            dw_ref[...] = jnp.zeros_like(dw_ref)

        yv, wv, gtv = y_ref[...], w_ref[...], gt_ref[...]
        r = lax.rsqrt(jnp.mean(yv * yv, axis=-1, keepdims=True) + EPS)
        z = yv * r
        nz = z * wv
        diff = (x_ref[...] + gtv * nz) - tgt_ref[...]
        loss_ref[...] += 0.5 * jnp.sum(jnp.mean(diff * diff, axis=-1, keepdims=True), axis=0, keepdims=True)
        dxv = diff * (1.0 / d)
        dout_ref[...] = dxv
        dgt_ref[...] += jnp.sum(dxv * nz, axis=0, keepdims=True)
        dn = dxv * gtv
        dw_ref[...] += jnp.sum(dn * z, axis=0, keepdims=True)
        dz = dn * wv
        dy_ref[...] = (r * (dz - z * jnp.mean(dz * z, axis=-1, keepdims=True))).astype(BF16)

    return pl.pallas_call(
        body, name=name, grid=(t // tb,),
        in_specs=[_row_spec(tb, d), _row_spec(tb, d), _vec_spec(d), _vec_spec(d), _row_spec(tb, d)],
        out_specs=[_row_spec(tb, d), _row_spec(tb, d), pl.BlockSpec((1, 1), lambda i: (0, 0)),
                   _vec_spec(d), _vec_spec(d)],
        out_shape=[jax.ShapeDtypeStruct((t, d), F32), jax.ShapeDtypeStruct((t, d), BF16),
                   jax.ShapeDtypeStruct((1, 1), F32), jax.ShapeDtypeStruct((1, d), F32),
                   jax.ShapeDtypeStruct((1, d), F32)],
        compiler_params=_cp(("arbitrary",)),
    )(x, y, w, gt, tgt)


def _postnorm_prenorm(x, y, w_post, gt, w_pre, sc, sh, name):
    t, d = x.shape
    tb = _pick(t, 256)

    def body(x_ref, y_ref, wp_ref, gt_ref, wn_ref, sc_ref, sh_ref, x1_ref, r_ref, h_ref, r1_ref):
        yv = y_ref[...]
        r = lax.rsqrt(jnp.mean(yv * yv, axis=-1, keepdims=True) + EPS)
        x1 = x_ref[...] + gt_ref[...] * (yv * r * wp_ref[...])
        r1 = lax.rsqrt(jnp.mean(x1 * x1, axis=-1, keepdims=True) + EPS)
        x1_ref[...] = x1
        r_ref[...] = r
        h_ref[...] = ((x1 * r1 * wn_ref[...]) * (1.0 + sc_ref[...]) + sh_ref[...]).astype(BF16)
        r1_ref[...] = r1

    return pl.pallas_call(
        body, name=name, grid=(t // tb,),
        in_specs=[_row_spec(tb, d), _row_spec(tb, d)] + [_vec_spec(d)] * 5,
        out_specs=[_row_spec(tb, d), _row_spec(tb, 1), _row_spec(tb, d), _row_spec(tb, 1)],
        out_shape=[jax.ShapeDtypeStruct((t, d), F32), jax.ShapeDtypeStruct((t, 1), F32),
                   jax.ShapeDtypeStruct((t, d), BF16), jax.ShapeDtypeStruct((t, 1), F32)],
        compiler_params=_cp(("parallel",)),
    )(x, y, w_post, gt, w_pre, sc, sh)


def _prenorm_postnorm_bwd(dh, x, r_pre, w_pre, sc, dres, y, r_post, w_post, gt, name):
    t, d = x.shape
    tb = _pick(t, 256)

    def body(dh_ref, x_ref, rp_ref, wp_ref, sc_ref, dres_ref, y_ref, rq_ref, wq_ref, gt_ref,
             dx_ref, dy_ref, dsh_ref, dsc_ref, dwp_ref, dgt_ref, dwq_ref):
        @pl.when(pl.program_id(0) == 0)
        def _():
            for ref in (dsh_ref, dsc_ref, dwp_ref, dgt_ref, dwq_ref):
                ref[...] = jnp.zeros_like(ref)

        dhv, rv, wv = dh_ref[...], rp_ref[...], wp_ref[...]
        z = x_ref[...] * rv
        dsh_ref[...] += jnp.sum(dhv, axis=0, keepdims=True)
        dsc_ref[...] += jnp.sum(dhv * (z * wv), axis=0, keepdims=True)
        dzw = dhv * (1.0 + sc_ref[...])
        dwp_ref[...] += jnp.sum(dzw * z, axis=0, keepdims=True)
        dz = dzw * wv
        dxv = dres_ref[...] + rv * (dz - z * jnp.mean(dz * z, axis=-1, keepdims=True))
        dx_ref[...] = dxv

        rq, wq = rq_ref[...], wq_ref[...]
        zq = y_ref[...] * rq
        dgt_ref[...] += jnp.sum(dxv * (zq * wq), axis=0, keepdims=True)
        dn = dxv * gt_ref[...]
        dwq_ref[...] += jnp.sum(dn * zq, axis=0, keepdims=True)
        dzq = dn * wq
        dy_ref[...] = (rq * (dzq - zq * jnp.mean(dzq * zq, axis=-1, keepdims=True))).astype(BF16)

    rs, r1, vs = _row_spec(tb, d), _row_spec(tb, 1), _vec_spec(d)
    return pl.pallas_call(
        body, name=name, grid=(t // tb,),
        in_specs=[rs, rs, r1, vs, vs, rs, rs, r1, vs, vs],
        out_specs=[rs, rs] + [vs] * 5,
        out_shape=[jax.ShapeDtypeStruct((t, d), F32), jax.ShapeDtypeStruct((t, d), BF16)]
        + [jax.ShapeDtypeStruct((1, d), F32)] * 5,
        compiler_params=_cp(("arbitrary",)),
    )(dh, x, r_pre, w_pre, sc, dres, y, r_post, w_post, gt)


def _postnorm_bwd(dxn, y, r, w, gt, name):
    t, d = y.shape
    tb = _pick(t, 256)

    def body(dx_ref, y_ref, r_ref, w_ref, gt_ref, dy_ref, dgt_ref, dw_ref):
        @pl.when(pl.program_id(0) == 0)
        def _():
            dgt_ref[...] = jnp.zeros_like(dgt_ref)
            dw_ref[...] = jnp.zeros_like(dw_ref)

        dxv, rv, wv = dx_ref[...], r_ref[...], w_ref[...]
        z = y_ref[...] * rv
        dgt_ref[...] += jnp.sum(dxv * (z * wv), axis=0, keepdims=True)
        dn = dxv * gt_ref[...]
        dw_ref[...] += jnp.sum(dn * z, axis=0, keepdims=True)
        dz = dn * wv
        dy_ref[...] = (rv * (dz - z * jnp.mean(dz * z, axis=-1, keepdims=True))).astype(BF16)

    return pl.pallas_call(
        body, name=name, grid=(t // tb,),
        in_specs=[_row_spec(tb, d), _row_spec(tb, d), _row_spec(tb, 1), _vec_spec(d), _vec_spec(d)],
        out_specs=[_row_spec(tb, d), _vec_spec(d), _vec_spec(d)],
        out_shape=[jax.ShapeDtypeStruct((t, d), BF16), jax.ShapeDtypeStruct((1, d), F32),
                   jax.ShapeDtypeStruct((1, d), F32)],
        compiler_params=_cp(("arbitrary",)),
    )(dxn, y, r, w, gt)


def _prenorm_bwd(dh, x, r, w, sc, dres, name):
    t, d = x.shape
    tb = _pick(t, 256)

    def body(dh_ref, x_ref, r_ref, w_ref, sc_ref, dres_ref, dx_ref, dsh_ref, dsc_ref, dw_ref):
        @pl.when(pl.program_id(0) == 0)
        def _():
            dsh_ref[...] = jnp.zeros_like(dsh_ref)
            dsc_ref[...] = jnp.zeros_like(dsc_ref)
            dw_ref[...] = jnp.zeros_like(dw_ref)

        dhv, rv, wv = dh_ref[...], r_ref[...], w_ref[...]
        z = x_ref[...] * rv
        dsh_ref[...] += jnp.sum(dhv, axis=0, keepdims=True)
        dsc_ref[...] += jnp.sum(dhv * (z * wv), axis=0, keepdims=True)
        dzw = dhv * (1.0 + sc_ref[...])
        dw_ref[...] += jnp.sum(dzw * z, axis=0, keepdims=True)
        dz = dzw * wv
        dx_ref[...] = dres_ref[...] + rv * (dz - z * jnp.mean(dz * z, axis=-1, keepdims=True))

    return pl.pallas_call(
        body, name=name, grid=(t // tb,),
        in_specs=[_row_spec(tb, d), _row_spec(tb, d), _row_spec(tb, 1), _vec_spec(d), _vec_spec(d),
                  _row_spec(tb, d)],
        out_specs=[_row_spec(tb, d), _vec_spec(d), _vec_spec(d), _vec_spec(d)],
        out_shape=[jax.ShapeDtypeStruct((t, d), F32)] + [jax.ShapeDtypeStruct((1, d), F32)] * 3,
        compiler_params=_cp(("arbitrary",)),
    )(dh, x, r, w, sc, dres)


def _headnorm_fwd(o, proj, g_blk, nw, name):
    t, wd = o.shape
    nh = wd // HD
    tb = _pick(t, 512)
    gb = g_blk * HD // wd

    def body(o_ref, g_ref, nw_ref, out_ref):
        o3 = o_ref[...].reshape(tb, nh, HD)
        g3 = g_ref[...].reshape(tb, nh, HD)
        rh = lax.rsqrt(jnp.mean(o3 * o3, axis=-1, keepdims=True) + EPS)
        res = (o3 * rh * nw_ref[...].reshape(1, 1, HD)) * (g3 * _sigmoid(g3))
        out_ref[...] = res.reshape(tb, wd).astype(BF16)

    return pl.pallas_call(
        body, name=name, grid=(t // tb,),
        in_specs=[_row_spec(tb, wd), pl.BlockSpec((tb, wd), lambda i: (i, gb)), _vec_spec(HD)],
        out_specs=_row_spec(tb, wd),
        out_shape=jax.ShapeDtypeStruct((t, wd), BF16),
        compiler_params=_cp(("parallel",)),
    )(o, proj, nw)


def _headnorm_bwd(dom, col_blk, o, proj, g_blk, nw, name):
    t, wd = o.shape
    nh = wd // HD
    tb = _pick(t, 512)
    gb = g_blk * HD // wd

    def body(do_ref, o_ref, g_ref, nw_ref, dout_ref, dg_ref, dnw_ref):
        @pl.when(pl.program_id(0) == 0)
        def _():
            dnw_ref[...] = jnp.zeros_like(dnw_ref)

        dn = do_ref[...].reshape(tb, nh, HD)
        o3 = o_ref[...].reshape(tb, nh, HD)
        g3 = g_ref[...].reshape(tb, nh, HD)
        nw3 = nw_ref[...].reshape(1, 1, HD)
        rh = lax.rsqrt(jnp.mean(o3 * o3, axis=-1, keepdims=True) + EPS)
        z = o3 * rh
        sg = _sigmoid(g3)
        sl = g3 * sg
        dnw_ref[...] += jnp.sum(jnp.sum(dn * sl * z, axis=1), axis=0, keepdims=True)
        dg_ref[...] = (dn * (z * nw3) * (sg * (1.0 + g3 * (1.0 - sg)))).reshape(tb, wd).astype(BF16)
        dz = dn * sl * nw3
        dout_ref[...] = (rh * (dz - z * jnp.mean(dz * z, axis=-1, keepdims=True))).reshape(tb, wd)

    return pl.pallas_call(
        body, name=name, grid=(t // tb,),
        in_specs=[pl.BlockSpec((tb, wd), lambda i: (i, col_blk)), _row_spec(tb, wd),
                  pl.BlockSpec((tb, wd), lambda i: (i, gb)), _vec_spec(HD)],
        out_specs=[_row_spec(tb, wd), _row_spec(tb, wd), _vec_spec(HD)],
        out_shape=[jax.ShapeDtypeStruct((t, wd), F32), jax.ShapeDtypeStruct((t, wd), BF16),
                   jax.ShapeDtypeStruct((1, HD), F32)],
        compiler_params=_cp(("arbitrary",)),
    )(dom, o, proj, nw)


def _tri(n, kind):
    r = lax.broadcasted_iota(jnp.int32, (n, n), 0)
    c = lax.broadcasted_iota(jnp.int32, (n, n), 1)
    if kind == "lower":
        return r >= c
    if kind == "strict":
        return r > c
    return r <= c


def _hg_gate(fl, l0, l1):
    mx = jnp.maximum(l0, l1)
    e0, e1 = jnp.exp(l0 - mx), jnp.exp(l1 - mx)
    lb = e0 / (e0 + e1)
    sg = _sigmoid(fl)
    f = lb + (1.0 - lb) * sg
    return lb, sg, f


def _store_split(ref, hh, row0, p):
    hi = p.astype(BF16)
    ref[hh, pl.ds(row0, SB), 0:HD] = hi
    ref[hh, pl.ds(row0, SB), HD:2 * HD] = (p - hi.astype(F32)).astype(BF16)


def _hgrn2_fwd(proj, lb_logits, nh, name, comm=None):
    t = proj.shape[0]
    nc = t // CHUNK
    C = CHUNK
    lg = lb_logits.reshape(2, nh, 1, HD)

    hp = min(HP, nh)
    ng = nh // hp

    def one_head(hh, st, q_ref, f_ref, i_ref, lg_ref, p_sc, r_sc):
        sl = slice(hh * HD, (hh + 1) * HD)
        q, v = q_ref[:, sl], i_ref[:, sl]
        _, _, f = _hg_gate(f_ref[:, sl], lg_ref[0, hh], lg_ref[1, hh])
        k = 1.0 - f
        low = _tri(C, "lower")
        b = _nn(low.astype(F32), jnp.log(f), HI)
        yield
        lane_c = lax.broadcasted_iota(jnp.int32, (SB, C), 1)
        lane_h = lax.broadcasted_iota(jnp.int32, (SB, HD), 1)
        row_h = lax.broadcasted_iota(jnp.int32, (SB, HD), 0)
        ones = jnp.ones((2 * HD, HD), BF16)

        for i in range(NSB):
            qi, ki, bi = q[SB * i:SB * (i + 1)], k[SB * i:SB * (i + 1)], b[SB * i:SB * (i + 1)]
            for s in range(SB):
                e = jnp.exp(jnp.minimum(bi - bi[s:s + 1], 0.0))
                p = jnp.where(row_h >= s, qi * ki[s:s + 1] * e, 0.0)
                _store_split(p_sc, hh, (i * SB + s) * SB, p)
            yield
        r_sc[hh] = _nn(p_sc[hh], ones)
        yield
        a_rows = []
        for i in range(NSB):
            acc = jnp.zeros((SB, HD), F32)
            for s in range(SB):
                acc = jnp.where(lane_h == SB * i + s, r_sc[hh, pl.ds((i * SB + s) * SB, SB), :], acc)
            acc = acc[:, :C]
            if i > 0:
                r = b[SB * i - 1:SB * i]
                bi = b[SB * i:SB * (i + 1)]
                qf = q[SB * i:SB * (i + 1)] * jnp.exp(bi - r)
                kf = k * jnp.exp(jnp.minimum(r - b, 0.0))
                acc = acc + jnp.where(lane_c < SB * i, _nt(qf, kf, HIGH), 0.0)
            a_rows.append(acc)
            yield
        a = jnp.concatenate(a_rows, axis=0)
        bl = b[C - 1:C, :]
        o = _nn(_bf(a), _bf(v)) + _nt(_bf(q * jnp.exp(b)), _bf(st))
        yield
        new_st = st * jnp.exp(bl) + _tn(_bf(v), _bf(k * jnp.exp(bl - b)))
        return o, a, new_st

    def body(*refs):
        c, hg = pl.program_id(0), pl.program_id(1)
        step = c * ng + hg
        ins, outs, scratch, comm_begin, comm_end = _comm_hooks(
            comm, refs, 4, 3, step == 0, step == (3 * nc * ng) // 4, step == nc * ng - 1)
        o_ref, a_ref, st_ref = outs
        s_sc, p_sc, r_sc = scratch
        comm_begin()

        @pl.when(c == 0)
        def _():
            for hh in range(hp):
                s_sc[hg * hp + hh] = jnp.zeros((HD, HD), F32)

        sts = [s_sc[hg * hp + hh] for hh in range(hp)]
        res = _interleave([one_head(hh, sts[hh], *ins, p_sc, r_sc) for hh in range(hp)])
        for hh in range(hp):
            o_ref[:, hh * HD:(hh + 1) * HD] = res[hh][0]
            a_ref[0, hh] = res[hh][1]
            st_ref[0, hh] = sts[hh]
            s_sc[hg * hp + hh] = res[hh][2]
        comm_end()

    blk = lambda off: pl.BlockSpec((C, hp * HD), lambda c, g: (c, off // hp + g))
    cn = comm.n if comm is not None else 0
    return pl.pallas_call(
        body, name=name, grid=(nc, ng),
        in_specs=[blk(0), blk(nh), blk(2 * nh),
                  pl.BlockSpec((2, hp, 1, HD), lambda c, g: (0, g, 0, 0))] + [ANY] * cn,
        out_specs=[blk(0),
                   pl.BlockSpec((1, hp, C, C), lambda c, g: (c, g, 0, 0)),
                   pl.BlockSpec((1, hp, HD, HD), lambda c, g: (c, g, 0, 0))] + [ANY] * cn,
        out_shape=[jax.ShapeDtypeStruct((t, nh * HD), F32),
                   jax.ShapeDtypeStruct((nc, nh, C, C), F32),
                   jax.ShapeDtypeStruct((nc, nh, HD, HD), F32)] + (comm.out_shapes() if cn else []),
        scratch_shapes=[pltpu.VMEM((nh, HD, HD), F32), pltpu.VMEM((hp, C * SB, 2 * HD), BF16),
                        pltpu.VMEM((hp, C * SB, HD), F32)] + (comm.scratch() if cn else []),
        compiler_params=_cp(("arbitrary", "arbitrary")),
    )(proj, proj, proj, lg, *(comm.arrays if cn else []))


def _hgrn2_bwd(proj, lb_logits, do, a_sv, st_sv, nh, name, comm=None):
    t = proj.shape[0]
    nc = t // CHUNK
    C = CHUNK
    lg = lb_logits.reshape(2, nh, 1, HD)
    hp = min(HP, nh)
    ng = nh // hp

    def one_head(hh, dst, q_ref, f_ref, i_ref, lg_ref, do_ref, a_ref, st_ref, p_sc, r_sc):
        sl = slice(hh * HD, (hh + 1) * HD)
        q, v, do_ = q_ref[:, sl], i_ref[:, sl], do_ref[:, sl]
        lb, sg, f = _hg_gate(f_ref[:, sl], lg_ref[0, hh], lg_ref[1, hh])
        k = 1.0 - f
        low = _tri(C, "lower")
        b = _nn(low.astype(F32), jnp.log(f), HI)
        yield
        bl = b[C - 1:C, :]
        eb, ekb = jnp.exp(b), jnp.exp(bl - b)
        qb, kb = q * eb, k * ekb
        a, st = a_ref[0, hh], st_ref[0, hh]

        da = jnp.where(low, _nt(_bf(do_), _bf(v)), 0.0)
        yield
        dv = _tn(_bf(a), _bf(do_)) + _nt(_bf(kb), _bf(dst))
        yield
        dqb = _nn(_bf(do_), _bf(st))
        dkb = _nn(_bf(v), _bf(dst))
        yield

        row = lax.broadcasted_iota(jnp.int32, (C, HD), 0)
        lane_c = lax.broadcasted_iota(jnp.int32, (SB, C), 1)
        row_h = lax.broadcasted_iota(jnp.int32, (SB, HD), 0)
        ones = jnp.ones((2 * HD, HD), BF16)
        sel = (lax.broadcasted_iota(jnp.int32, (C, C * SB), 0)
               == jnp.right_shift(lax.broadcasted_iota(jnp.int32, (C, C * SB), 1), SB.bit_length() - 1)).astype(BF16)

        for i in range(NSB):
            doi, vi = do_[SB * i:SB * (i + 1)], v[SB * i:SB * (i + 1)]
            for s in range(SB):
                _store_split(p_sc, hh, (i * SB + s) * SB, doi * vi[s:s + 1])
            yield
        r_sc[hh] = _nn(p_sc[hh], ones)
        yield
        dq_rows = []
        dk_off = jnp.zeros((C, HD), F32)
        for i in range(NSB):
            qi, ki, bi = q[SB * i:SB * (i + 1)], k[SB * i:SB * (i + 1)], b[SB * i:SB * (i + 1)]
            acc = jnp.zeros((SB, HD), F32)
            for s in range(SB):
                e = jnp.exp(jnp.minimum(bi - bi[s:s + 1], 0.0))
                g = jnp.where(row_h >= s, r_sc[hh, pl.ds((i * SB + s) * SB, SB), :] * e, 0.0)
                acc = acc + g * ki[s:s + 1]
                _store_split(p_sc, hh, (i * SB + s) * SB, g * qi)
            yield
            if i > 0:
                r = b[SB * i - 1:SB * i]
                fq = jnp.exp(bi - r)
                fk = jnp.exp(jnp.minimum(r - b, 0.0))
                dai = jnp.where(lane_c < SB * i, da[SB * i:SB * (i + 1)], 0.0)
                acc = acc + _nn(dai, k * fk, HIGH) * fq
                dk_off = dk_off + _tn(dai, qi * fq, HIGH) * fk
                yield
            dq_rows.append(acc)
        dqi = jnp.concatenate(dq_rows, axis=0)
        dq = dqi + dqb * eb
        dk_inter = dkb * ekb
        dk_split = _nn(sel, p_sc[hh])
        dk = dk_split[:, :HD] + dk_split[:, HD:] + dk_off + dk_inter
        yield
        db = q * dq - k * dk
        extra = (jnp.sum(k * dk_inter, axis=0, keepdims=True)
                 + jnp.exp(bl) * jnp.sum(dst * st, axis=0, keepdims=True))
        db = db + jnp.where(row == C - 1, extra, 0.0)
        dlf = _nn(_tri(C, "upper").astype(F32), db, HI)
        yield
        df = dlf / f - dk
        dfl = (df * (1.0 - lb) * sg * (1.0 - sg)).astype(BF16)
        dl = jnp.sum(df * (1.0 - sg), axis=0, keepdims=True) * (lb * (1.0 - lb))
        new_dst = dst * jnp.exp(bl) + _tn(_bf(do_), _bf(qb))
        return dq.astype(BF16), dfl, dv.astype(BF16), dl, new_dst

    def body(*refs):
        c, hg = pl.program_id(0), pl.program_id(1)
        step = c * ng + hg
        ins, outs, scratch, comm_begin, comm_end = _comm_hooks(
            comm, refs, 7, 4, step == 0, step == (3 * nc * ng) // 4, step == nc * ng - 1)
        dq_ref, df_ref, di_ref, dl_ref = outs
        ds_sc, p_sc, r_sc = scratch
        comm_begin()

        @pl.when(c == 0)
        def _():
            for hh in range(hp):
                ds_sc[hg * hp + hh] = jnp.zeros((HD, HD), F32)

        @pl.when(step == 0)
        def _():
            dl_ref[...] = jnp.zeros_like(dl_ref)

        dsts = [ds_sc[hg * hp + hh] for hh in range(hp)]
        res = _interleave([one_head(hh, dsts[hh], *ins, p_sc, r_sc) for hh in range(hp)])
        for hh in range(hp):
            sl = slice(hh * HD, (hh + 1) * HD)
            dq_ref[:, sl], df_ref[:, sl], di_ref[:, sl] = res[hh][0], res[hh][1], res[hh][2]
            dl_ref[pl.ds(hg * hp + hh, 1), :] += res[hh][3]
            ds_sc[hg * hp + hh] = res[hh][4]
        comm_end()

    rblk = lambda off: pl.BlockSpec((C, hp * HD), lambda c, g: (nc - 1 - c, off // hp + g))
    oblk = pl.BlockSpec((C, hp * HD), lambda c, g: (nc - 1 - c, g))
    cn = comm.n if comm is not None else 0
    return pl.pallas_call(
        body, name=name, grid=(nc, ng),
        in_specs=[rblk(0), rblk(nh), rblk(2 * nh),
                  pl.BlockSpec((2, hp, 1, HD), lambda c, g: (0, g, 0, 0)),
                  oblk,
                  pl.BlockSpec((1, hp, C, C), lambda c, g: (nc - 1 - c, g, 0, 0)),
                  pl.BlockSpec((1, hp, HD, HD), lambda c, g: (nc - 1 - c, g, 0, 0))] + [ANY] * cn,
        out_specs=[oblk, oblk, oblk, pl.BlockSpec((nh, HD), lambda c, g: (0, 0))] + [ANY] * cn,
        out_shape=[jax.ShapeDtypeStruct((t, nh * HD), BF16)] * 3 + [jax.ShapeDtypeStruct((nh, HD), F32)]
        + (comm.out_shapes() if cn else []),
        scratch_shapes=[pltpu.VMEM((nh, HD, HD), F32), pltpu.VMEM((hp, C * SB, 2 * HD), BF16),
                        pltpu.VMEM((hp, C * SB, HD), F32)] + (comm.scratch() if cn else []),
        compiler_params=_cp(("arbitrary", "arbitrary")),
    )(proj, proj, proj, lg, do, a_sv, st_sv, *(comm.arrays if cn else []))


def _shift_rows(u, d, row):
    t = u.shape[0]
    if d == 0:
        return u
    rolled = pltpu.roll(u, d % t, 0)
    if d > 0:
        return jnp.where(row >= d, rolled, 0.0)
    return jnp.where(row < t + d, rolled, 0.0)


def _gdn_prep(proj, conv_w, blk0, nh, name):
    t = proj.shape[0]
    scale = HD ** -0.5

    def body(u_ref, w_ref, o_ref):
        j = pl.program_id(0)
        u, w = u_ref[...], w_ref[...]
        row = lax.broadcasted_iota(jnp.int32, (t, HD), 0)
        y = w[CONV_K - 1:CONV_K, :] * u
        for d in range(1, CONV_K):
            y = y + w[CONV_K - 1 - d:CONV_K - d, :] * _shift_rows(u, d, row)
        a = y * _sigmoid(y)
        n = a * lax.rsqrt(jnp.sum(a * a, axis=-1, keepdims=True) + EPS)
        n = n * jnp.where(j < nh, scale, 1.0)
        o_ref[...] = jnp.where(j < 2 * nh, n, a)

    return pl.pallas_call(
        body, name=name, grid=(3 * nh,),
        in_specs=[pl.BlockSpec((t, HD), lambda j: (0, blk0 + j)), pl.BlockSpec((CONV_K, HD), lambda j: (0, j))],
        out_specs=pl.BlockSpec((t, HD), lambda j: (0, j)),
        out_shape=jax.ShapeDtypeStruct((t, 3 * nh * HD), F32),
        compiler_params=_cp(("parallel",)),
    )(proj, conv_w)


def _gdn_prep_bwd(proj, conv_w, dq, dk, dv, blk0, nh, name):
    t = proj.shape[0]
    scale = HD ** -0.5

    def body(u_ref, w_ref, dq_ref, dk_ref, dv_ref, du_ref, dw_ref):
        j = pl.program_id(0)
        u, w = u_ref[...], w_ref[...]
        dout = jnp.where(j < nh, dq_ref[...], jnp.where(j < 2 * nh, dk_ref[...], dv_ref[...]))
        row = lax.broadcasted_iota(jnp.int32, (t, HD), 0)
        us = [_shift_rows(u, d, row) for d in range(CONV_K)]
        y = w[CONV_K - 1:CONV_K, :] * us[0]
        for d in range(1, CONV_K):
            y = y + w[CONV_K - 1 - d:CONV_K - d, :] * us[d]
        sg = _sigmoid(y)
        a = y * sg
        rs = lax.rsqrt(jnp.sum(a * a, axis=-1, keepdims=True) + EPS)
        n = a * rs
        dn = dout * jnp.where(j < nh, scale, 1.0)
        da_n = rs * (dn - n * jnp.sum(dn * n, axis=-1, keepdims=True))
        da = jnp.where(j < 2 * nh, da_n, dout)
        dy = da * (sg * (1.0 + y * (1.0 - sg)))
        du = w[CONV_K - 1:CONV_K, :] * dy
        for d in range(1, CONV_K):
            du = du + w[CONV_K - 1 - d:CONV_K - d, :] * _shift_rows(dy, -d, row)
        du_ref[...] = du.astype(BF16)
        for d in range(CONV_K):
            dw_ref[CONV_K - 1 - d:CONV_K - d, :] = jnp.sum(dy * us[d], axis=0, keepdims=True)

    return pl.pallas_call(
        body, name=name, grid=(3 * nh,),
        in_specs=[pl.BlockSpec((t, HD), lambda j: (0, blk0 + j)), pl.BlockSpec((CONV_K, HD), lambda j: (0, j))]
        + [pl.BlockSpec((t, HD), functools.partial(lambda p, j: (0, jnp.clip(j - p * nh, 0, nh - 1)), p))
           for p in range(3)],
        out_specs=[pl.BlockSpec((t, HD), lambda j: (0, j)), pl.BlockSpec((CONV_K, HD), lambda j: (0, j))],
        out_shape=[jax.ShapeDtypeStruct((t, 3 * nh * HD), BF16), jax.ShapeDtypeStruct((CONV_K, 3 * nh * HD), F32)],
        compiler_params=_cp(("arbitrary",)),
    )(proj, conv_w, dq, dk, dv)


def _gdn_gates(ab, alog, dtb, h, nh):
    lane = lax.broadcasted_iota(jnp.int32, ab.shape, 1)
    x = ab + dtb
    sp = jnp.maximum(x, 0.0) + jnp.log(1.0 + jnp.exp(-jnp.abs(x)))
    ea = jnp.exp(alog)
    la_all = -ea * sp
    beta_all = _sigmoid(ab)
    pick = lambda val, ln: jnp.sum(jnp.where(lane == ln, val, 0.0), axis=1, keepdims=True)
    la = pick(la_all, h)
    beta = pick(beta_all, nh + h)
    dla_da = pick(-ea * _sigmoid(x), h)
    return la, beta, dla_da


def _unit_lower_inverses(ms, C):
    nb = C // SB
    sh = SB.bit_length() - 1
    rowb = jnp.right_shift(lax.broadcasted_iota(jnp.int32, (C, C), 0), sh)
    colb = jnp.right_shift(lax.broadcasted_iota(jnp.int32, (C, C), 1), sh)
    eye = (lax.broadcasted_iota(jnp.int32, (SB, SB), 0) == lax.broadcasted_iota(jnp.int32, (SB, SB), 1)).astype(F32)
    spread = (jnp.bitwise_and(lax.broadcasted_iota(jnp.int32, (SB, C), 1), SB - 1)
              == lax.broadcasted_iota(jnp.int32, (SB, C), 0)).astype(F32)
    blocks = [[m[SB * i:SB * (i + 1), SB * i:SB * (i + 1)] for i in range(nb)] for m in ms]
    xs = [[eye] * nb for _ in ms]
    for s in range(SB - 1):
        xs = [[x - b[:, s:s + 1] * x[s:s + 1, :] for x, b in zip(xh, bh)] for xh, bh in zip(xs, blocks)]
    ts = [jnp.where(rowb == colb, _nn(jnp.concatenate(xh, axis=0), spread, HIGH), 0.0) for xh in xs]
    lvl = 1
    while (1 << lvl) <= nb:
        off = ((jnp.right_shift(rowb, lvl) == jnp.right_shift(colb, lvl))
               & (jnp.right_shift(rowb, lvl - 1) != jnp.right_shift(colb, lvl - 1)))
        ts = [t - _nn(t, _nn(jnp.where(off, m, 0.0), t, HIGH), HIGH) for t, m in zip(ts, ms)]
        lvl += 1
    return ts


def _gdn_chunks(qs, ks, vs, las, betas, C):
    low, strict = _tri(C, "lower"), _tri(C, "strict")
    eye = (lax.broadcasted_iota(jnp.int32, (C, C), 0) == lax.broadcasted_iota(jnp.int32, (C, C), 1)).astype(F32)
    g_bs = [_nn(low.astype(F32), jnp.broadcast_to(la, (C, HD)), HI) for la in las]
    ps = [_nt(k, k, HIGH) for k in ks]
    qks = [_nt(_bf(q), _bf(k)) for q, k in zip(qs, ks)]
    chs = []
    for g_b, p, qk_raw, beta in zip(g_bs, ps, qks, betas):
        g_c = g_b[:, :C]
        gamma = jnp.where(low, jnp.exp(jnp.minimum(g_c - g_c.T, 0.0)), 0.0)
        gl = g_b[C - 1:C, :]
        chs.append(dict(gamma=gamma, eg=jnp.exp(g_b), gl=gl, ekt=jnp.exp(gl - g_b), p=p,
                        m=jnp.where(strict, beta * p * gamma, 0.0), qk_raw=qk_raw))
    xs = _unit_lower_inverses([ch["m"] for ch in chs], C)
    r_ws = [k * (beta * ch["eg"]) for ch, k, beta in zip(chs, ks, betas)]
    uws = [_nn(x, jnp.concatenate([v * beta, r_w], axis=1), HIGH) for x, v, beta, r_w in zip(xs, vs, betas, r_ws)]
    for ch, x, r_w, uw in zip(chs, xs, r_ws, uws):
        ch.update(x=x, r_w=r_w, uw=uw)
    return chs


def _gdn_fwd(qkv, proj, ab_blk, alog, dtb, nh, name, comm=None):
    t = qkv.shape[0]
    nc = t // CHUNK
    C = CHUNK
    hp = min(HP, nh)
    ng = nh // hp

    def body(*refs):
        c, hg = pl.program_id(0), pl.program_id(1)
        step = c * ng + hg
        ins, outs, scratch, comm_begin, comm_end = _comm_hooks(
            comm, refs, 6, 3, step == 0, step == (3 * nc * ng) // 4, step == nc * ng - 1)
        q_ref, k_ref, v_ref, ab_ref, al_ref, dt_ref = ins
        o_ref, x_ref, st_ref = outs
        s_sc, = scratch
        comm_begin()

        @pl.when(c == 0)
        def _():
            for hh in range(hp):
                s_sc[hg * hp + hh] = jnp.zeros((HD, HD), F32)

        sls = [slice(hh * HD, (hh + 1) * HD) for hh in range(hp)]
        qs, ks, vs = [q_ref[:, sl] for sl in sls], [k_ref[:, sl] for sl in sls], [v_ref[:, sl] for sl in sls]
        sts = [s_sc[hg * hp + hh] for hh in range(hp)]
        gates = [_gdn_gates(ab_ref[...], al_ref[...], dt_ref[...], hg * hp + hh, nh) for hh in range(hp)]
        chs = _gdn_chunks(qs, ks, vs, [g[0] for g in gates], [g[1] for g in gates], C)
        stbs = [_bf(st) for st in sts]
        vns = [ch["uw"][:, :HD] - _nt(_bf(ch["uw"][:, HD:]), stb) for ch, stb in zip(chs, stbs)]
        o_st = [_nt(_bf(q * ch["eg"]), stb) for q, ch, stb in zip(qs, chs, stbs)]
        outs_ = [o + _nn(_bf(ch["qk_raw"] * ch["gamma"]), _bf(vn)) for o, ch, vn in zip(o_st, chs, vns)]
        new_sts = [st * jnp.exp(ch["gl"]) + _tn(_bf(vn), _bf(k * ch["ekt"]))
                   for st, ch, vn, k in zip(sts, chs, vns, ks)]
        for hh in range(hp):
            o_ref[:, sls[hh]] = outs_[hh]
            x_ref[0, hh] = chs[hh]["x"]
            st_ref[0, hh] = sts[hh]
            s_sc[hg * hp + hh] = new_sts[hh]
        comm_end()

    blk = lambda off: pl.BlockSpec((C, hp * HD), lambda c, g: (c, off // hp + g))
    vec = pl.BlockSpec((1, HD), lambda c, g: (0, 0))
    cn = comm.n if comm is not None else 0
    return pl.pallas_call(
        body, name=name, grid=(nc, ng),
        in_specs=[blk(0), blk(nh), blk(2 * nh), pl.BlockSpec((C, HD), lambda c, g: (c, ab_blk)), vec, vec]
        + [ANY] * cn,
        out_specs=[blk(0),
                   pl.BlockSpec((1, hp,C, C), lambda c, g: (c, g, 0, 0)),
                   pl.BlockSpec((1, hp,HD, HD), lambda c, g: (c, g, 0, 0))] + [ANY] * cn,
        out_shape=[jax.ShapeDtypeStruct((t, nh * HD), F32),
                   jax.ShapeDtypeStruct((nc, nh, C, C), F32),
                   jax.ShapeDtypeStruct((nc, nh, HD, HD), F32)] + (comm.out_shapes() if cn else []),
        scratch_shapes=[pltpu.VMEM((nh, HD, HD), F32)] + (comm.scratch() if cn else []),
        compiler_params=_cp(("arbitrary", "arbitrary")),
    )(qkv, qkv, qkv, proj, alog, dtb, *(comm.arrays if cn else []))


def _gdn_bwd(qkv, proj, ab_blk, alog, dtb, do, x_sv, st_sv, nh, name, comm=None):
    t = qkv.shape[0]
    nc = t // CHUNK
    C = CHUNK
    hp = min(HP, nh)
    ng = nh // hp

    def one_head(h, hh, dst, q_ref, k_ref, v_ref, ab_ref, al_ref, dt_ref, do_ref, x_ref, st_ref):
        sl = slice(hh * HD, (hh + 1) * HD)
        q, k, v, do_ = q_ref[:, sl], k_ref[:, sl], v_ref[:, sl], do_ref[:, sl]
        la, beta, dla_da = _gdn_gates(ab_ref[...], al_ref[...], dt_ref[...], h, nh)
        low, strict = _tri(C, "lower"), _tri(C, "strict")
        g_b = _nn(low.astype(F32), jnp.broadcast_to(la, (C, HD)), HI)
        yield
        g_c = g_b[:, :C]
        gamma = jnp.where(low, jnp.exp(jnp.minimum(g_c - g_c.T, 0.0)), 0.0)
        eg = jnp.exp(g_b)
        gl = g_b[C - 1:C, :]
        ekt = jnp.exp(gl - g_b)
        egl = jnp.exp(gl)
        p = _nt(k, k, HIGH)
        yield
        x = x_ref[0, hh]
        r_w = k * (beta * eg)
        rhs = jnp.concatenate([v * beta, r_w], axis=1)
        uw = _nn(x, rhs, HIGH)
        yield
        u, w = uw[:, :HD], uw[:, HD:]
        qk_raw = _nt(_bf(q), _bf(k))
        yield
        qk = qk_raw * gamma
        st = st_ref[0, hh]
        stb, dstb = _bf(st), _bf(dst)
        vn = u - _nt(_bf(w), stb)
        yield
        qd, kt = q * eg, k * ekt

        dvn = _tn(_bf(qk), _bf(do_)) + _nt(_bf(kt), dstb)
        yield
        dq2 = jnp.where(low, _nt(_bf(do_), _bf(vn)), 0.0)
        yield
        dqd = _nn(_bf(do_), stb)
        yield
        dkt = _nn(_bf(vn), dstb)
        yield
        dw = -_nn(_bf(dvn), stb)
        yield
        dxx = jnp.concatenate([dvn, dw], axis=1)
        dr = _tn(x, dxx, HIGH)
        yield
        dm = -jnp.where(strict, _nt(dr, uw, HIGH), 0.0)
        yield
        dr_u, dr_w = dr[:, :HD], dr[:, HD:]
        rsum = lambda z: jnp.sum(z, axis=1, keepdims=True)

        dv = dr_u * beta
        dmg = dm * gamma
        dbeta = rsum(dr_u * v) + rsum(dr_w * k) * eg[:, :1] + rsum(dmg * p)
        yield
        dp = dmg * beta
        dq2g = dq2 * gamma
        dk = (dr_w * (beta * eg) + dkt * ekt + _tn(_bf(dq2g), _bf(q))
              + _nn(_bf(dp + dp.T), _bf(k)))
        yield
        dq = dqd * eg + _nn(_bf(dq2g), _bf(k))
        yield
        e = dp * p + dq2g * qk_raw
        t_kt = rsum(dkt * kt)
        dg = rsum(dqd * qd) + rsum(dr_w * r_w) - t_kt + rsum(e) - rsum(e.T)
        yield
        dgl = jnp.sum(t_kt, axis=0, keepdims=True) + jnp.sum(dst * st, keepdims=True) * egl[:, :1]
        rowc = lax.broadcasted_iota(jnp.int32, (C, 1), 0)
        dg = dg + jnp.where(rowc == C - 1, dgl, 0.0)
        dla = _nn(_tri(C, "upper").astype(F32), jnp.broadcast_to(dg, (C, HD)), HI)[:, :1]
        yield
        da = dla * dla_da
        db = dbeta * beta * (1.0 - beta)
        lane = lax.broadcasted_iota(jnp.int32, (C, HD), 1)
        dab = jnp.where(lane == h, da, 0.0) + jnp.where(lane == nh + h, db, 0.0)
        lane1 = lax.broadcasted_iota(jnp.int32, (1, HD), 1)
        d_alog = jnp.where(lane1 == h, jnp.sum(dla * la, axis=0, keepdims=True), 0.0)
        d_dtb = jnp.where(lane1 == h, jnp.sum(da, axis=0, keepdims=True), 0.0)
        new_dst = dst * egl + _tn(_bf(do_), _bf(qd)) - _tn(_bf(dvn), _bf(w))
        return dab, d_alog, d_dtb, new_dst, dq, dk, dv

    def body(*refs):
        c, hg = pl.program_id(0), pl.program_id(1)
        step = c * ng + hg
        ins, outs, scratch, comm_begin, comm_end = _comm_hooks(
            comm, refs, 9, 5, step == 0, step == (3 * nc * ng) // 4, step == nc * ng - 1)
        dq_ref, dk_ref, dv_ref, dab_ref, dpar_ref = outs
        ds_sc, = scratch
        comm_begin()

        @pl.when(c == 0)
        def _():
            for hh in range(hp):
                ds_sc[hg * hp + hh] = jnp.zeros((HD, HD), F32)

        @pl.when(step == 0)
        def _():
            dpar_ref[...] = jnp.zeros_like(dpar_ref)

        @pl.when(hg == 0)
        def _():
            dab_ref[...] = jnp.zeros_like(dab_ref)

        dsts = [ds_sc[hg * hp + hh] for hh in range(hp)]
        res = _interleave([one_head(hg * hp + hh, hh, dsts[hh], *ins) for hh in range(hp)])
        for hh in range(hp):
            sl = slice(hh * HD, (hh + 1) * HD)
            ds_sc[hg * hp + hh] = res[hh][3]
            dq_ref[:, sl], dk_ref[:, sl], dv_ref[:, sl] = res[hh][4], res[hh][5], res[hh][6]
        dab_ref[...] += sum(r[0] for r in res[1:]) + res[0][0]
        dpar_ref[0:1, :] += sum(r[1] for r in res[1:]) + res[0][1]
        dpar_ref[1:2, :] += sum(r[2] for r in res[1:]) + res[0][2]
        comm_end()

    rblk = lambda off: pl.BlockSpec((C, hp * HD), lambda c, g: (nc - 1 - c, off // hp + g))
    oblk = pl.BlockSpec((C, hp * HD), lambda c, g: (nc - 1 - c, g))
    vec = pl.BlockSpec((1, HD), lambda c, g: (0, 0))
    cn = comm.n if comm is not None else 0
    return pl.pallas_call(
        body, name=name, grid=(nc, ng),
        in_specs=[rblk(0), rblk(nh), rblk(2 * nh),
                  pl.BlockSpec((C, HD), lambda c, g: (nc - 1 - c, ab_blk)), vec, vec, oblk,
                  pl.BlockSpec((1, hp,C, C), lambda c, g: (nc - 1 - c, g, 0, 0)),
                  pl.BlockSpec((1, hp,HD, HD), lambda c, g: (nc - 1 - c, g, 0, 0))] + [ANY] * cn,
        out_specs=[oblk, oblk, oblk,
                   pl.BlockSpec((C, HD), lambda c, g: (nc - 1 - c, 0)),
                   pl.BlockSpec((8, HD), lambda c, g: (0, 0))] + [ANY] * cn,
        out_shape=[jax.ShapeDtypeStruct((t, nh * HD), F32)] * 3
        + [jax.ShapeDtypeStruct((t, HD), F32), jax.ShapeDtypeStruct((8, HD), F32)]
        + (comm.out_shapes() if cn else []),
        scratch_shapes=[pltpu.VMEM((nh, HD, HD), F32)] + (comm.scratch() if cn else []),
        compiler_params=_cp(("arbitrary", "arbitrary")),
    )(qkv, qkv, qkv, proj, alog, dtb, do, x_sv, st_sv, *(comm.arrays if cn else []))


def _ada_fwd(c_all, w, b, name):
    nb, d = c_all.shape
    n = w.shape[1]
    tn = _pick(n, 512)

    def body(c_ref, w_ref, b_ref, o_ref):
        cv = c_ref[...]
        o_ref[...] = _nn(cv * _sigmoid(cv), w_ref[...], HI) + b_ref[...]

    return pl.pallas_call(
        body, name=name, grid=(n // tn,),
        in_specs=[pl.BlockSpec((nb, d), lambda j: (0, 0)), pl.BlockSpec((d, tn), lambda j: (0, j)),
                  pl.BlockSpec((1, tn), lambda j: (0, j))],
        out_specs=pl.BlockSpec((nb, tn), lambda j: (0, j)),
        out_shape=jax.ShapeDtypeStruct((nb, n), F32),
        compiler_params=_cp(("parallel",)),
    )(c_all, w, b)


def _ada_wgrad(c_all, dmod, name):
    nb, d = c_all.shape
    n = dmod.shape[1]
    tn = _pick(n, 512)

    def body(c_ref, g_ref, o_ref):
        cv = c_ref[...]
        o_ref[...] = _tn(cv * _sigmoid(cv), g_ref[...], HI)

    return pl.pallas_call(
        body, name=name, grid=(n // tn,),
        in_specs=[pl.BlockSpec((nb, d), lambda j: (0, 0)), pl.BlockSpec((nb, tn), lambda j: (0, j))],
        out_specs=pl.BlockSpec((d, tn), lambda j: (0, j)),
        out_shape=jax.ShapeDtypeStruct((d, n), F32),
        compiler_params=_cp(("parallel",)),
    )(c_all, dmod)


def _adamw(w, m, v, g, name, parts=False):
    lead = w.ndim == 3
    r, cdim = w.shape[-2:]
    cap = max(SUBLANES, ADAM_BLOCK_ELEMS // cdim // SUBLANES * SUBLANES)
    tr = r if r <= cap else _pick_rows(r, cap)
    bc1 = 1.0 - ADAM_B1 ** ADAM_STEP
    bc2 = 1.0 - ADAM_B2 ** ADAM_STEP

    glist = list(g) if isinstance(g, (list, tuple)) else [g]
    bounds = [0]
    for ga in glist:
        bounds.append(bounds[-1] + ga.shape[-2] // tr)

    def body(w_ref, m_ref, v_ref, *rest):
        g_refs, (go_ref, d_ref, mo_ref, vo_ref) = rest[:len(glist)], rest[len(glist):]
        if parts:
            sums = []
            for g_ref in g_refs:
                gv = g_ref[0].astype(F32)
                for s in range(1, N_DEV):
                    gv = gv + g_ref[s].astype(F32)
                sums.append(gv)
            gv = sums[-1]
            for p in range(len(sums) - 2, -1, -1):
                gv = jnp.where(pl.program_id(0) < bounds[p + 1], sums[p], gv)
        else:
            gv = g_refs[0][...]
        wv = w_ref[...]
        mn = ADAM_B1 * m_ref[...] + (1.0 - ADAM_B1) * gv
        vn = ADAM_B2 * v_ref[...] + (1.0 - ADAM_B2) * (gv * gv)
        m_hat = mn / bc1
        v_hat = vn / bc2
        go_ref[...] = gv
        d_ref[...] = -ADAM_LR * (m_hat / (jnp.sqrt(v_hat) + ADAM_EPS) + ADAM_WD * wv)
        mo_ref[...] = mn
        vo_ref[...] = vn

    flat = pl.BlockSpec((tr, cdim), lambda i: (i, 0))
    spec = pl.BlockSpec((None, tr, cdim), lambda i: (0, i, 0)) if lead else flat
    def piece_spec(p):
        lo, n = bounds[p], bounds[p + 1] - bounds[p]
        return pl.BlockSpec((N_DEV, tr, cdim), lambda i: (0, jnp.clip(i - lo, 0, n - 1), 0))

    gspecs = [piece_spec(p) for p in range(len(glist))] if parts else [flat]
    return pl.pallas_call(
        body, name=name, grid=(r // tr,),
        in_specs=[spec, spec, spec] + gspecs,
        out_specs=[spec] * 4,
        out_shape=[jax.ShapeDtypeStruct(w.shape, F32)] * 4,
        compiler_params=_cp(("arbitrary",)),
    )(w, m, v, *glist)


def _pick_rows(r, pref):
    t = pref
    while r % t:
        t -= 8
    assert t > 0
    return t


def _dev_index(x, y, c):
    return 4 * x + 2 * y + c


class _Comm:
    def __init__(self, kind, arrays):
        self.kind, self.n = kind, len(arrays)
        self.arrays = [a[0] if isinstance(a, tuple) else a for a in arrays]
        self.rows = [(a[1], a[2]) if isinstance(a, tuple) else None for a in arrays]

    def out_shapes(self):
        if self.kind == "gather":
            return [jax.ShapeDtypeStruct((N_DEV,) + a.shape, a.dtype) for a in self.arrays]
        return [jax.ShapeDtypeStruct(a.shape if r is None else (N_DEV, r[1]) + a.shape[2:], a.dtype)
                for a, r in zip(self.arrays, self.rows)]

    def scratch(self):
        return [pltpu.SemaphoreType.DMA((self.n, 7)), pltpu.SemaphoreType.DMA((self.n, 7)),
                pltpu.SemaphoreType.DMA((self.n,))]

    def _gather_parts(self, ins, outs, sems):
        send_sems, recv_sems, local_sems = sems
        x, y, c = lax.axis_index("x"), lax.axis_index("y"), lax.axis_index("c")
        me, sibling = (x, y, c), (x, y, 1 - c)
        chips = [(1 - x, y), (x, 1 - y), (1 - x, 1 - y)]

        def copy(a, k, block, to, src=None):
            slot = outs[a].at[_dev_index(*block)]
            return pltpu.make_async_remote_copy(
                src_ref=slot if src is None else src, dst_ref=slot,
                send_sem=send_sems.at[a, k], recv_sem=recv_sems.at[a, k],
                device_id=to, device_id_type=MESH)

        n = self.n
        mine = [pltpu.make_async_copy(ins[a], outs[a].at[_dev_index(*me)], local_sems.at[a]) for a in range(n)]
        first = []
        for a in range(n):
            first.append(copy(a, 0, me, sibling, src=ins[a]))
            first += [copy(a, 1 + j, me, (*chip, c), src=ins[a]) for j, chip in enumerate(chips)]
        landed = [copy(a, 1 + j, (*chip, c), me) for j, chip in enumerate(chips) for a in range(n)]
        passed = [copy(a, 4 + j, (*chip, c), sibling) for j, chip in enumerate(chips) for a in range(n)]
        late = []
        for a in range(n):
            late.append(copy(a, 0, sibling, me))
            late += [copy(a, 4 + j, (*chip, 1 - c), me) for j, chip in enumerate(chips)]
        return mine, first, landed, passed, late

    def _exchange_parts(self, ins, outs, sems):
        send_sems, recv_sems, local_sems = sems
        x, y, c = lax.axis_index("x"), lax.axis_index("y"), lax.axis_index("c")
        my = _dev_index(x, y, c)
        n = self.n

        def block(a, j):
            r = self.rows[a]
            return ins[a].at[j] if r is None else ins[a].at[j, pl.ds(r[0], r[1])]

        mine = [pltpu.make_async_copy(block(a, my), outs[a].at[my], local_sems.at[a]) for a in range(n)]
        sends, recvs = [], []
        for k in range(1, N_DEV):
            px = (1 - x) if (k >> 2) & 1 else x
            py = (1 - y) if (k >> 1) & 1 else y
            pc = (1 - c) if k & 1 else c
            peer = _dev_index(px, py, pc)
            for a in range(n):
                sends.append(pltpu.make_async_remote_copy(
                    src_ref=block(a, peer), dst_ref=outs[a].at[my],
                    send_sem=send_sems.at[a, k - 1], recv_sem=recv_sems.at[a, k - 1],
                    device_id=(px, py, pc), device_id_type=MESH))
                recvs.append(pltpu.make_async_remote_copy(
                    src_ref=block(a, my), dst_ref=outs[a].at[peer],
                    send_sem=send_sems.at[a, k - 1], recv_sem=recv_sems.at[a, k - 1],
                    device_id=(x, y, c), device_id_type=MESH))
        return mine, sends, recvs

    def start(self, ins, outs, sems):
        if self.kind == "gather":
            mine, first, _, _, _ = self._gather_parts(ins, outs, sems)
        else:
            mine, first, _ = self._exchange_parts(ins, outs, sems)
        for cp in mine + first:
            cp.start()

    def mid(self, ins, outs, sems):
        if self.kind == "gather":
            _, _, landed, passed, _ = self._gather_parts(ins, outs, sems)
            for got, fwd in zip(landed, passed):
                got.wait_recv()
                fwd.start()

    def finish(self, ins, outs, sems):
        if self.kind == "gather":
            mine, first, _, passed, late = self._gather_parts(ins, outs, sems)
            for cp in late:
                cp.wait_recv()
            for cp in first + passed:
                cp.wait_send()
        else:
            mine, sends, recvs = self._exchange_parts(ins, outs, sems)
            for cp in sends:
                cp.wait_send()
            for cp in recvs:
                cp.wait_recv()
        for cp in mine:
            cp.wait()

    def run(self, name):
        n = self.n

        def body(*refs):
            ins, outs, sems = refs[:n], refs[n:2 * n], refs[2 * n:]
            self.start(ins, outs, sems)
            self.mid(ins, outs, sems)
            self.finish(ins, outs, sems)

        return pl.pallas_call(
            body, name=name, in_specs=[ANY] * n, out_specs=[ANY] * n,
            out_shape=self.out_shapes(), scratch_shapes=self.scratch(),
        )(*self.arrays)


def _all_gather(arrays, name):
    return _Comm("gather", arrays).run(name)


def _comm_hooks(comm, refs, n_in, n_out, first, middle, last):
    cn = comm.n if comm is not None else 0
    ins, cins = refs[:n_in], refs[n_in:n_in + cn]
    outs, couts = refs[n_in + cn:n_in + cn + n_out], refs[n_in + cn + n_out:n_in + 2 * cn + n_out]
    rest = refs[n_in + 2 * cn + n_out:]
    scratch, csems = (rest[:len(rest) - 3], rest[len(rest) - 3:]) if cn else (rest, ())

    def begin():
        if cn:
            pl.when(first)(lambda: comm.start(cins, couts, csems))
            pl.when(middle)(lambda: comm.mid(cins, couts, csems))

    def end():
        if cn:
            pl.when(last)(lambda: comm.finish(cins, couts, csems))

    return ins, outs, scratch, begin, end


def _local_step(x, tgt, mod, n1, n2, n3, n4, w_in_p, lb_logits, hg_norm, conv_w, alog, dtb, gdn_norm,
                late_w, dist=None):
    t, d = x.shape
    nh = d // 2 // HD
    ab_blk = 8 * nh
    sh_m, sc_m, gt_m, sh_f, sc_f, gt_f = [mod[i:i + 1] for i in range(6)]

    h1, r1 = _prenorm(x, n1, sc_m, sh_m, "prenorm_mix")
    if dist is None:
        proj = _mm(h1, w_in_p, "nn", [F32], "mm_proj")
        o_hg, a_sv, hst_sv = _hgrn2_fwd(proj, lb_logits, nh, "hgrn2_fwd")
        qkv = _gdn_prep(proj, conv_w, 4 * nh, nh, "gdn_prep")
        o_gd, x_sv, gst_sv = _gdn_fwd(qkv, proj, ab_blk, alog, dtb, nh, "gdn_fwd")
        w_out, w_ff1, w_ff2 = late_w
        exch = lambda arrays: None
    else:
        half = d // 2
        proj_top, g_bot, g_out = _mm(h1, w_in_p, "nn", [F32], "mm_proj_top",
                                     comm=_Comm("gather", [late_w[3], late_w[0]]))

        def add_top(acc, extra, outs):
            outs[0][...] = acc + extra[0][...]

        w_bot_p = dist["relayout"](g_bot)
        proj = _mm(h1, w_bot_p, "nn", [F32], "mm_proj_bot", epilogue=add_top, extras=(proj_top,), a_col0=half)
        w_in_p = jnp.concatenate([w_in_p, w_bot_p], axis=0)
        o_hg, a_sv, hst_sv, g_ff2 = _hgrn2_fwd(proj, lb_logits, nh, "hgrn2_fwd",
                                               comm=_Comm("gather", late_w[2:3]))
        qkv = _gdn_prep(proj, conv_w, 4 * nh, nh, "gdn_prep")
        o_gd, x_sv, gst_sv, g_ff1 = _gdn_fwd(qkv, proj, ab_blk, alog, dtb, nh, "gdn_fwd",
                                             comm=_Comm("gather", late_w[1:2]))
        w_out, w_ff1, w_ff2 = dist["assemble"](g_out, g_ff1, g_ff2)
        exch = lambda arrays: _Comm("exchange", arrays)
    om_hg = _headnorm_fwd(o_hg, proj, 3 * nh, hg_norm, "headnorm_hg")
    om_gd = _headnorm_fwd(o_gd, proj, 7 * nh, gdn_norm, "headnorm_gdn")
    om = jnp.concatenate([om_hg, om_gd], axis=1)
    y1 = _mm(om, w_out, "nn", [F32], "mm_out")
    x1, r2, h2, r3 = _postnorm_prenorm(x, y1, n2, gt_m, n3, sc_f, sh_f, "postnorm_mix_prenorm_ffn")

    def relu2(acc, extra, outs):
        rl = jnp.maximum(acc, 0.0)
        outs[0][...] = (rl * rl).astype(BF16)

    act = _mm(h2, w_ff1, "nn", [BF16], "mm_ff1", epilogue=relu2)
    y2 = _mm(act, w_ff2, "nn", [F32], "mm_ff2")
    dout, dy2, loss, dgt_f, dn4 = _final_loss_bwd(x1, y2, n4, gt_f, tgt, "final_loss_bwd")
    dw_ff2 = _mm(act, dy2, "tn", [BF16], "mm_dw_ff2")

    def drelu2(acc, extra, outs):
        outs[0][...] = (acc * (2.0 * jnp.sqrt(extra[0][...].astype(F32)))).astype(BF16)

    recv = {}
    ff2a, ff2b = dist["parts_ff2"](dw_ff2) if dist else (None, None)
    du, *recv["ff2a"] = _listed(_mm(dy2, w_ff2, "nt", [BF16], "mm_da", epilogue=drelu2, extras=(act,),
                                    comm=exch([ff2a])))
    ff1_cols = dict(by_cols=True, tn=dist["n_ff"]) if dist else {}
    dw_ff1, *recv["ff2b"] = _listed(_mm(h2, du, "tn", [BF16], "mm_dw_ff1", comm=exch([ff2b]), **ff1_cols))
    ff1a, ff1b = dist["parts_ff1"](dw_ff1) if dist else (None, None)
    dh2, *recv["ff1a"] = _listed(_mm(du, w_ff1, "nt", [F32], "mm_dh2", comm=exch([ff1a])))
    dx1, dy1, dsh_f, dsc_f, dn3, dgt_m, dn2 = _prenorm_postnorm_bwd(
        dh2, x1, r3, n3, sc_f, dout, y1, r2, n2, gt_m, "prenorm_ffn_postnorm_mix_bwd")

    dw_out = _mm(om, dy1, "tn", [BF16], "mm_dw_out")
    dom = _mm(dy1, w_out, "nt", [F32], "mm_dom")
    do_hg, dg_hg, dhgn = _headnorm_bwd(dom, 0, o_hg, proj, 3 * nh, hg_norm, "headnorm_hg_bwd")
    do_gd, dg_gd, dgdn = _headnorm_bwd(dom, 1, o_gd, proj, 7 * nh, gdn_norm, "headnorm_gdn_bwd")
    p_out = dist["parts_out"](dw_out) if dist else None
    dq_hg, df_hg, di_hg, dl0, *recv["ff1b_out"] = _hgrn2_bwd(proj, lb_logits, do_hg, a_sv, hst_sv, nh,
                                                             "hgrn2_bwd", comm=exch([ff1b, p_out]))
    dq_g, dk_g, dv_g, dab, dpar = _gdn_bwd(qkv, proj, ab_blk, alog, dtb, do_gd, x_sv, gst_sv, nh, "gdn_bwd")
    du_conv, dconv = _gdn_prep_bwd(proj, conv_w, dq_g, dk_g, dv_g, 4 * nh, nh, "gdn_prep_bwd")
    dproj = jnp.concatenate([dq_hg, df_hg, di_hg, dg_hg, du_conv, dg_gd, dab.astype(BF16)], axis=1)
    if dist is None:
        dw_in = _mm(h1, dproj, "tn", [BF16], "mm_dw_in")
        dh1 = _mm(dproj, w_in_p, "nt", [F32], "mm_dh1", tk=1664)
    else:
        q4 = d // 4
        dw_in_a = _mm(h1[:, :q4], dproj, "tn", [BF16], "mm_dw_in_a")
        dw_in_b, in_a = _mm(h1[:, q4:2 * q4], dproj, "tn", [BF16], "mm_dw_in_b",
                            comm=exch([dist["parts_in"](dw_in_a)]))
        dw_in_c, in_b = _mm(h1[:, 2 * q4:], dproj, "tn", [BF16], "mm_dw_in_c",
                            comm=exch([dist["parts_in"](dw_in_b)]))
        dh1, in_c = _mm(dproj, w_in_p, "nt", [F32], "mm_dh1", tk=1664, comm=exch([dist["parts_in"](dw_in_c)]))
        recv["in"] = [in_a, in_b, in_c]
        dw_in = None
    dx, dsh_m, dsc_m, dn1 = _prenorm_bwd(dh1, x, r1, n1, sc_m, dx1, "prenorm_mix_bwd")

    dmod = jnp.concatenate([dsh_m, dsc_m, dgt_m, dsh_f, dsc_f, dgt_f], axis=0)
    grads = dict(dmod=dmod, n1=dn1, n2=dn2, n3=dn3, n4=dn4, w_in=dw_in, lb0=dl0, hg_norm=dhgn, conv=dconv,
                 alog=dpar[0:1], dtb=dpar[1:2], gdn_norm=dgdn, w_out=dw_out, w_ff1=dw_ff1, w_ff2=dw_ff2,
                 recv=recv)
    return loss, dx, grads


def _pack(vals):
    rows = []
    for vv in vals:
        flat = vv.reshape(-1)
        flat = jnp.pad(flat, (0, (-flat.shape[0]) % (SUBLANES * LANES)))
        rows.append(flat.reshape(-1, LANES))
    return jnp.concatenate(rows, axis=0)


def _unpack(packed, shapes):
    out, r = [], 0
    for shp in shapes:
        size = 1
        for s in shp:
            size *= s
        nr = -(-size // (SUBLANES * LANES)) * SUBLANES
        out.append(packed[r:r + nr].reshape(-1)[:size].reshape(shp))
        r += nr
    return out


def _sum_parts(parts, name):
    _, r, cdim = parts.shape

    def body(p_ref, o_ref):
        acc = p_ref[0]
        for s in range(1, N_DEV):
            acc = acc + p_ref[s]
        o_ref[...] = acc

    return pl.pallas_call(
        body, name=name,
        out_shape=jax.ShapeDtypeStruct((r, cdim), F32),
        compiler_params=_cp(),
    )(parts)


def kernel(x, c, w_ada, b_ada, pre_mix_norm, post_mix_norm, pre_ffn_norm, post_ffn_norm, w_in, hg_lb_logits, hg_norm, gdn_conv_w, gdn_a_log, gdn_dt_bias, gdn_norm, w_out, w_ff1, w_ff2, loss_target, m_w_ada, m_b_ada, m_pre_mix_norm, m_post_mix_norm, m_pre_ffn_norm, m_post_ffn_norm, m_w_in, m_hg_lb_logits, m_hg_norm, m_gdn_conv_w, m_gdn_a_log, m_gdn_dt_bias, m_gdn_norm, m_w_out, m_w_ff1, m_w_ff2, v_w_ada, v_b_ada, v_pre_mix_norm, v_post_mix_norm, v_pre_ffn_norm, v_post_ffn_norm, v_w_in, v_hg_lb_logits, v_hg_norm, v_gdn_conv_w, v_gdn_a_log, v_gdn_dt_bias, v_gdn_norm, v_w_out, v_w_ff1, v_w_ff2):
    t, d = x.shape[1], x.shape[2]
    nh = d // 2 // HD
    in_cols = w_in.shape[2] * N_DEV
    main = in_cols - 2 * nh
    me = _dev_index(lax.axis_index("x"), lax.axis_index("y"), lax.axis_index("c"))

    c_all, conv_g = _all_gather([c, gdn_conv_w[0]], "gather_small")
    c_all = c_all.reshape(N_DEV, d)
    conv_full = conv_g.transpose(1, 0, 2).reshape(CONV_K, -1)
    def relayout(g):
        full = g.transpose(1, 0, 2).reshape(g.shape[1], in_cols)
        return jnp.concatenate([full, jnp.zeros((g.shape[1], LANES - 2 * nh), BF16)], axis=1)

    w_in_b16 = w_in[0].astype(BF16)
    w_in_p = relayout(_all_gather([w_in_b16[:d // 2]], "gather_w_in_top")[0])
    late_w = [w_out[0].astype(BF16), w_ff1[0].astype(BF16), w_ff2[0].astype(BF16), w_in_b16[d // 2:]]

    n_in = w_in.shape[2]
    n_ff = w_ff1.shape[2]

    def halves(p):
        r = p.shape[1] // 2
        return (p, 0, r), (p, r, r)

    dist = dict(
        relayout=relayout,
        assemble=lambda g_out, g_ff1, g_ff2: (g_out.reshape(d, d), g_ff1.transpose(1, 0, 2).reshape(d, -1),
                                              g_ff2.reshape(-1, d)),
        n_ff=n_ff,
        parts_ff2=lambda dw: halves(dw.reshape(N_DEV, -1, d)),
        parts_ff1=halves,
        parts_out=lambda dw: dw.reshape(N_DEV, d // N_DEV, d),
        parts_in=lambda dw: dw[:, :in_cols].reshape(dw.shape[0], N_DEV, n_in).transpose(1, 0, 2),
    )

    n_ada = w_ada.shape[2]
    b_loc = lax.dynamic_slice(b_ada, (0, me * n_ada), (1, n_ada))
    mod_part = _ada_fwd(c_all, w_ada[0], b_loc, "ada_fwd")
    mod_all = _all_gather([mod_part], "gather_mod")[0]
    mod = lax.dynamic_slice(mod_all, (0, me, 0), (N_DEV, 1, n_ada)).reshape(6, d)

    pad_lane = lambda vv: jnp.concatenate([vv, jnp.zeros((1, LANES - vv.shape[1]), F32)], axis=1)
    loss, dx, g = _local_step(
        x[0], loss_target[0], mod, pre_mix_norm, post_mix_norm, pre_ffn_norm, post_ffn_norm, w_in_p,
        hg_lb_logits, hg_norm, conv_full, pad_lane(gdn_a_log), pad_lane(gdn_dt_bias), gdn_norm,
        late_w, dist)

    rep_names = ["b_ada", "n1", "n2", "n3", "n4", "lb", "hg_norm", "alog", "dtb", "gdn_norm"]
    rep_w = [b_ada, pre_mix_norm, post_mix_norm, pre_ffn_norm, post_ffn_norm, hg_lb_logits, hg_norm,
             gdn_a_log, gdn_dt_bias, gdn_norm]
    rep_m = [m_b_ada, m_pre_mix_norm, m_post_mix_norm, m_pre_ffn_norm, m_post_ffn_norm, m_hg_lb_logits,
             m_hg_norm, m_gdn_a_log, m_gdn_dt_bias, m_gdn_norm]
    rep_v = [v_b_ada, v_pre_mix_norm, v_post_mix_norm, v_pre_ffn_norm, v_post_ffn_norm, v_hg_lb_logits,
             v_hg_norm, v_gdn_a_log, v_gdn_dt_bias, v_gdn_norm]
    rep_shapes = [a.shape for a in rep_w]
    g_lb = jnp.stack([g["lb0"], -g["lb0"]], axis=0)
    rep_g = [g["dmod"], g["n1"], g["n2"], g["n3"], g["n4"], g_lb, g["hg_norm"],
             g["alog"][:, :nh], g["dtb"][:, :nh], g["gdn_norm"]]
    small = _pack(rep_g + [g["conv"]])
    n_rep_rows = _pack(rep_g).shape[0]
    pad_rows = (-small.shape[0]) % 8
    if pad_rows:
        small = jnp.concatenate([small, jnp.zeros((pad_rows, LANES), F32)], axis=0)
    small_all = _all_gather([small], "gather_small_grads")[0]
    small_sum = _sum_parts(small_all, "sum_small_grads")
    rep_out = _adamw(_pack(rep_w), _pack(rep_m), _pack(rep_v), small_sum[:n_rep_rows], "adamw_small")
    rep_g_o, rep_d_o, rep_m_o, rep_v_o = [dict(zip(rep_names, _unpack(p, rep_shapes))) for p in rep_out]

    conv_sum = small_sum[n_rep_rows:n_rep_rows + CONV_K * conv_full.shape[1] // LANES].reshape(CONV_K, -1)
    n_conv = gdn_conv_w.shape[2]
    conv_loc = lax.dynamic_slice(conv_sum, (0, me * n_conv), (CONV_K, n_conv))
    conv_o = _adamw(gdn_conv_w, m_gdn_conv_w, v_gdn_conv_w, conv_loc, "adamw_conv")

    dmod_all = small_all[:, :6 * d // LANES, :].reshape(N_DEV, 6 * d)
    dmod_loc = lax.dynamic_slice(dmod_all, (0, me * n_ada), (N_DEV, n_ada))
    g_ada = _ada_wgrad(c_all, dmod_loc, "ada_wgrad")
    ada_o = _adamw(w_ada, m_w_ada, v_w_ada, g_ada, "adamw_ada")

    rc = g["recv"]
    r_ff2 = [rc["ff2a"][0], rc["ff2b"][0]]
    r_ff1 = [rc["ff1a"][0], rc["ff1b_out"][0]]
    r_out, r_in = rc["ff1b_out"][1], rc["in"]
    in_o = _adamw(w_in, m_w_in, v_w_in, r_in, "adamw_w_in", parts=True)
    out_o = _adamw(w_out, m_w_out, v_w_out, r_out, "adamw_w_out", parts=True)
    ff1_o = _adamw(w_ff1, m_w_ff1, v_w_ff1, r_ff1, "adamw_w_ff1", parts=True)
    ff2_o = _adamw(w_ff2, m_w_ff2, v_w_ff2, r_ff2, "adamw_w_ff2", parts=True)

    loss_tot = lax.psum(loss[0, 0], ("x", "y", "c"))

    def leaf(kind):
        return [ada_o[kind], rep_out_d[kind]["b_ada"], rep_out_d[kind]["n1"], rep_out_d[kind]["n2"],
                rep_out_d[kind]["n3"], rep_out_d[kind]["n4"], in_o[kind], rep_out_d[kind]["lb"],
                rep_out_d[kind]["hg_norm"], conv_o[kind], rep_out_d[kind]["alog"], rep_out_d[kind]["dtb"],
                rep_out_d[kind]["gdn_norm"], out_o[kind], ff1_o[kind], ff2_o[kind]]

    rep_out_d = [rep_g_o, rep_d_o, rep_m_o, rep_v_o]
    return (loss_tot, dx[None], *leaf(0), *leaf(1), *leaf(2), *leaf(3))
```

```python
import functools

import jax
import jax.numpy as jnp
from jax import lax
from jax.experimental import pallas as pl
from jax.experimental.pallas import tpu as pltpu

F32 = jnp.float32
BF16 = jnp.bfloat16
HI = lax.Precision.HIGHEST
HIGH = lax.Precision.HIGH

EPS = 1e-6
CHUNK = 64
SB = 16
NSB = CHUNK // SB
HP = 8
HD = 128
CONV_K = 4
N_DEV = 8
LANES = 128
SUBLANES = 8
VMEM_LIMIT = 56 * 1024 * 1024
MM_FULL_K = 2048

ADAM_BLOCK_ELEMS = 256 * 1024
ADAM_LR = 0.001
ADAM_B1 = 0.9
ADAM_B2 = 0.999
ADAM_EPS = 1e-08
ADAM_WD = 0.01
ADAM_STEP = 10

ANY = pl.BlockSpec(memory_space=pl.ANY)
MESH = pl.DeviceIdType.MESH


def _cp(sem=None):
    return pltpu.CompilerParams(dimension_semantics=sem, vmem_limit_bytes=VMEM_LIMIT)


def _dot(a, b, dims, precision=None):
    return lax.dot_general(a, b, (dims, ((), ())), precision=precision, preferred_element_type=F32)


def _nn(a, b, precision=None):
    return _dot(a, b, ((1,), (0,)), precision)


def _nt(a, b, precision=None):
    return _dot(a, b, ((1,), (1,)), precision)


def _tn(a, b, precision=None):
    return _dot(a, b, ((0,), (0,)), precision)


def _bf(x):
    return x.astype(BF16)


def _sigmoid(x):
    return 1.0 / (1.0 + jnp.exp(-x))


def _interleave(gens):
    results = [None] * len(gens)
    live = list(range(len(gens)))
    while live:
        for i in list(live):
            try:
                next(gens[i])
            except StopIteration as stop:
                results[i] = stop.value
                live.remove(i)
    return results


def _listed(res):
    return list(res) if isinstance(res, (list, tuple)) else [res]


def _pick(n, pref):
    if n <= pref:
        return n
    t = pref
    while n % t:
        t -= LANES
    assert t > 0, (n, pref)
    return t


def _mm(a, b, mode, out_dtypes, name, epilogue=None, extras=(), tm=1024, tn=2048, tk=1024, comm=None,
        by_cols=False):
    if mode == "nn":
        (m, kd), (_, n) = a.shape, b.shape
    elif mode == "nt":
        (m, kd), (n, _) = a.shape, b.shape
    else:
        (kd, m), (_, n) = a.shape, b.shape
    if kd <= MM_FULL_K:
        tk = kd
    tm, tn, tk = _pick(m, tm), _pick(n, tn), _pick(kd, tk)
    nk = kd // tk
    if mode == "nn":
        a_spec = pl.BlockSpec((tm, tk), lambda i, j, k: (i, k))
        b_spec = pl.BlockSpec((tk, tn), lambda i, j, k: (k, j))
        dims = ((1,), (0,))
    elif mode == "nt":
        a_spec = pl.BlockSpec((tm, tk), lambda i, j, k: (i, k))
        b_spec = pl.BlockSpec((tn, tk), lambda i, j, k: (j, k))
        dims = ((1,), (1,))
    else:
        a_spec = pl.BlockSpec((tk, tm), lambda i, j, k: (k, i))
        b_spec = pl.BlockSpec((tk, tn), lambda i, j, k: (k, j))
        dims = ((0,), (0,))
    o_spec = pl.BlockSpec((tm, tn), lambda i, j, k: (i, j))
    if by_cols:
        assert epilogue is None and not extras
        res_spec = pl.BlockSpec((None, tm, tn), lambda i, j, k: (j, i, 0))
        res_shape = (n // tn, m, tn)
    else:
        res_spec, res_shape = o_spec, (m, n)
    n_extra, n_out = len(extras), len(out_dtypes)

    gm, gn = m // tm, n // tn
    cn = comm.n if comm is not None else 0

    def body(*refs):
        i, j, k = pl.program_id(0), pl.program_id(1), pl.program_id(2)
        at0 = (j == 0) & (k == 0)
        ins, out_refs, scratch, comm_begin, comm_end = _comm_hooks(
            comm, refs, 2 + n_extra, n_out, (i == 0) & at0, (i == gm - 1) & at0,
            (i == gm - 1) & (j == gn - 1) & (k == nk - 1))
        a_ref, b_ref, extra_refs = ins[0], ins[1], ins[2:]
        comm_begin()
        part = _dot(a_ref[...], b_ref[...], dims)
        if nk == 1:
            if epilogue is None:
                out_refs[0][...] = part.astype(out_dtypes[0])
            else:
                epilogue(part, extra_refs, out_refs)
        elif in_place:
            @pl.when(k == 0)
            def _():
                out_refs[0][...] = part

            @pl.when(k > 0)
            def _():
                out_refs[0][...] += part
        else:
            acc, = scratch

            @pl.when(k == 0)
            def _():
                acc[...] = part

            @pl.when(k > 0)
            def _():
                acc[...] += part

            @pl.when(k == nk - 1)
            def _():
                if epilogue is None:
                    out_refs[0][...] = acc[...].astype(out_dtypes[0])
                else:
                    epilogue(acc[...], extra_refs, out_refs)

        comm_end()

    in_place = epilogue is None and n_out == 1 and out_dtypes[0] == F32
    acc_scratch = [] if (nk == 1 or in_place) else [pltpu.VMEM((tm, tn), F32)]
    sem = ("arbitrary",) * 3 if cn else ("parallel", "parallel", "arbitrary")
    outs = pl.pallas_call(
        body, name=name,
        grid=(gm, gn, nk),
        in_specs=[a_spec, b_spec] + [o_spec] * n_extra + [ANY] * cn,
        out_specs=[res_spec] * n_out + [ANY] * cn,
        out_shape=[jax.ShapeDtypeStruct(res_shape, dt) for dt in out_dtypes] + (comm.out_shapes() if cn else []),
        scratch_shapes=acc_scratch + (comm.scratch() if cn else []),
        compiler_params=_cp(sem),
    )(a, b, *extras, *(comm.arrays if cn else []))
    return outs[0] if n_out + cn == 1 else outs


def _row_spec(tb, d):
    return pl.BlockSpec((tb, d), lambda i: (i, 0))


def _vec_spec(d):
    return pl.BlockSpec((1, d), lambda i: (0, 0))


def _prenorm(x, w, sc, sh, name):
    t, d = x.shape
    tb = _pick(t, 256)

    def body(x_ref, w_ref, sc_ref, sh_ref, h_ref, r_ref):
        xv = x_ref[...]
        r = lax.rsqrt(jnp.mean(xv * xv, axis=-1, keepdims=True) + EPS)
        h_ref[...] = ((xv * r * w_ref[...]) * (1.0 + sc_ref[...]) + sh_ref[...]).astype(BF16)
        r_ref[...] = r

    return pl.pallas_call(
        body, name=name, grid=(t // tb,),
        in_specs=[_row_spec(tb, d), _vec_spec(d), _vec_spec(d), _vec_spec(d)],
        out_specs=[_row_spec(tb, d), _row_spec(tb, 1)],
        out_shape=[jax.ShapeDtypeStruct((t, d), BF16), jax.ShapeDtypeStruct((t, 1), F32)],
        compiler_params=_cp(("parallel",)),
    )(x, w, sc, sh)


def _final_loss_bwd(x, y, w, gt, tgt, name):
    t, d = x.shape
    tb = _pick(t, 256)

    def body(x_ref, y_ref, w_ref, gt_ref, tgt_ref, dout_ref, dy_ref, loss_ref, dgt_ref, dw_ref):
        @pl.when(pl.program_id(0) == 0)
        def _():
            loss_ref[...] = jnp.zeros_like(loss_ref)
            dgt_ref[...] = jnp.zeros_like(dgt_ref)
            dw_ref[...] = jnp.zeros_like(dw_ref)

        yv, wv, gtv = y_ref[...], w_ref[...], gt_ref[...]
        r = lax.rsqrt(jnp.mean(yv * yv, axis=-1, keepdims=True) + EPS)
        z = yv * r
        nz = z * wv
        diff = (x_ref[...] + gtv * nz) - tgt_ref[...]
        loss_ref[...] += 0.5 * jnp.sum(jnp.mean(diff * diff, axis=-1, keepdims=True), axis=0, keepdims=True)
        dxv = diff * (1.0 / d)
        dout_ref[...] = dxv
        dgt_ref[...] += jnp.sum(dxv * nz, axis=0, keepdims=True)
        dn = dxv * gtv
        dw_ref[...] += jnp.sum(dn * z, axis=0, keepdims=True)
        dz = dn * wv
        dy_ref[...] = (r * (dz - z * jnp.mean(dz * z, axis=-1, keepdims=True))).astype(BF16)

    return pl.pallas_call(
        body, name=name, grid=(t // tb,),
        in_specs=[_row_spec(tb, d), _row_spec(tb, d), _vec_spec(d), _vec_spec(d), _row_spec(tb, d)],
        out_specs=[_row_spec(tb, d), _row_spec(tb, d), pl.BlockSpec((1, 1), lambda i: (0, 0)),
                   _vec_spec(d), _vec_spec(d)],
        out_shape=[jax.ShapeDtypeStruct((t, d), F32), jax.ShapeDtypeStruct((t, d), BF16),
                   jax.ShapeDtypeStruct((1, 1), F32), jax.ShapeDtypeStruct((1, d), F32),
                   jax.ShapeDtypeStruct((1, d), F32)],
        compiler_params=_cp(("arbitrary",)),
    )(x, y, w, gt, tgt)


def _postnorm_prenorm(x, y, w_post, gt, w_pre, sc, sh, name):
    t, d = x.shape
    tb = _pick(t, 256)

    def body(x_ref, y_ref, wp_ref, gt_ref, wn_ref, sc_ref, sh_ref, x1_ref, r_ref, h_ref, r1_ref):
        yv = y_ref[...]
        r = lax.rsqrt(jnp.mean(yv * yv, axis=-1, keepdims=True) + EPS)
        x1 = x_ref[...] + gt_ref[...] * (yv * r * wp_ref[...])
        r1 = lax.rsqrt(jnp.mean(x1 * x1, axis=-1, keepdims=True) + EPS)
        x1_ref[...] = x1
        r_ref[...] = r
        h_ref[...] = ((x1 * r1 * wn_ref[...]) * (1.0 + sc_ref[...]) + sh_ref[...]).astype(BF16)
        r1_ref[...] = r1

    return pl.pallas_call(
        body, name=name, grid=(t // tb,),
        in_specs=[_row_spec(tb, d), _row_spec(tb, d)] + [_vec_spec(d)] * 5,
        out_specs=[_row_spec(tb, d), _row_spec(tb, 1), _row_spec(tb, d), _row_spec(tb, 1)],
        out_shape=[jax.ShapeDtypeStruct((t, d), F32), jax.ShapeDtypeStruct((t, 1), F32),
                   jax.ShapeDtypeStruct((t, d), BF16), jax.ShapeDtypeStruct((t, 1), F32)],
        compiler_params=_cp(("parallel",)),
    )(x, y, w_post, gt, w_pre, sc, sh)


def _prenorm_postnorm_bwd(dh, x, r_pre, w_pre, sc, dres, y, r_post, w_post, gt, name):
    t, d = x.shape
    tb = _pick(t, 256)

    def body(dh_ref, x_ref, rp_ref, wp_ref, sc_ref, dres_ref, y_ref, rq_ref, wq_ref, gt_ref,
             dx_ref, dy_ref, dsh_ref, dsc_ref, dwp_ref, dgt_ref, dwq_ref):
        @pl.when(pl.program_id(0) == 0)
        def _():
            for ref in (dsh_ref, dsc_ref, dwp_ref, dgt_ref, dwq_ref):
                ref[...] = jnp.zeros_like(ref)

        dhv, rv, wv = dh_ref[...], rp_ref[...], wp_ref[...]
        z = x_ref[...] * rv
        dsh_ref[...] += jnp.sum(dhv, axis=0, keepdims=True)
        dsc_ref[...] += jnp.sum(dhv * (z * wv), axis=0, keepdims=True)
        dzw = dhv * (1.0 + sc_ref[...])
        dwp_ref[...] += jnp.sum(dzw * z, axis=0, keepdims=True)
        dz = dzw * wv
        dxv = dres_ref[...] + rv * (dz - z * jnp.mean(dz * z, axis=-1, keepdims=True))
        dx_ref[...] = dxv

        rq, wq = rq_ref[...], wq_ref[...]
        zq = y_ref[...] * rq
        dgt_ref[...] += jnp.sum(dxv * (zq * wq), axis=0, keepdims=True)
        dn = dxv * gt_ref[...]
        dwq_ref[...] += jnp.sum(dn * zq, axis=0, keepdims=True)
        dzq = dn * wq
        dy_ref[...] = (rq * (dzq - zq * jnp.mean(dzq * zq, axis=-1, keepdims=True))).astype(BF16)

    rs, r1, vs = _row_spec(tb, d), _row_spec(tb, 1), _vec_spec(d)
    return pl.pallas_call(
        body, name=name, grid=(t // tb,),
        in_specs=[rs, rs, r1, vs, vs, rs, rs, r1, vs, vs],
        out_specs=[rs, rs] + [vs] * 5,
        out_shape=[jax.ShapeDtypeStruct((t, d), F32), jax.ShapeDtypeStruct((t, d), BF16)]
        + [jax.ShapeDtypeStruct((1, d), F32)] * 5,
        compiler_params=_cp(("arbitrary",)),
    )(dh, x, r_pre, w_pre, sc, dres, y, r_post, w_post, gt)


def _prenorm_bwd(dh, x, r, w, sc, dres, name):
    t, d = x.shape
    tb = _pick(t, 256)

    def body(dh_ref, x_ref, r_ref, w_ref, sc_ref, dres_ref, dx_ref, dsh_ref, dsc_ref, dw_ref):
        @pl.when(pl.program_id(0) == 0)
        def _():
            dsh_ref[...] = jnp.zeros_like(dsh_ref)
            dsc_ref[...] = jnp.zeros_like(dsc_ref)
            dw_ref[...] = jnp.zeros_like(dw_ref)

        dhv, rv, wv = dh_ref[...], r_ref[...], w_ref[...]
        z = x_ref[...] * rv
        dsh_ref[...] += jnp.sum(dhv, axis=0, keepdims=True)
        dsc_ref[...] += jnp.sum(dhv * (z * wv), axis=0, keepdims=True)
        dzw = dhv * (1.0 + sc_ref[...])
        dw_ref[...] += jnp.sum(dzw * z, axis=0, keepdims=True)
        dz = dzw * wv
        dx_ref[...] = dres_ref[...] + rv * (dz - z * jnp.mean(dz * z, axis=-1, keepdims=True))

    return pl.pallas_call(
        body, name=name, grid=(t // tb,),
        in_specs=[_row_spec(tb, d), _row_spec(tb, d), _row_spec(tb, 1), _vec_spec(d), _vec_spec(d),
                  _row_spec(tb, d)],
        out_specs=[_row_spec(tb, d), _vec_spec(d), _vec_spec(d), _vec_spec(d)],
        out_shape=[jax.ShapeDtypeStruct((t, d), F32)] + [jax.ShapeDtypeStruct((1, d), F32)] * 3,
        compiler_params=_cp(("arbitrary",)),
    )(dh, x, r, w, sc, dres)


def _headnorm_fwd(o, proj, g_blk, nw, name):
    t, wd = o.shape
    nh = wd // HD
    tb = _pick(t, 512)
    gb = g_blk * HD // wd

    def body(o_ref, g_ref, nw_ref, out_ref):
        o3 = o_ref[...].reshape(tb, nh, HD)
        g3 = g_ref[...].reshape(tb, nh, HD)
        rh = lax.rsqrt(jnp.mean(o3 * o3, axis=-1, keepdims=True) + EPS)
        res = (o3 * rh * nw_ref[...].reshape(1, 1, HD)) * (g3 * _sigmoid(g3))
        out_ref[...] = res.reshape(tb, wd).astype(BF16)

    return pl.pallas_call(
        body, name=name, grid=(t // tb,),
        in_specs=[_row_spec(tb, wd), pl.BlockSpec((tb, wd), lambda i: (i, gb)), _vec_spec(HD)],
        out_specs=_row_spec(tb, wd),
        out_shape=jax.ShapeDtypeStruct((t, wd), BF16),
        compiler_params=_cp(("parallel",)),
    )(o, proj, nw)


def _headnorm_bwd(dom, col_blk, o, proj, g_blk, nw, name):
    t, wd = o.shape
    nh = wd // HD
    tb = _pick(t, 512)
    gb = g_blk * HD // wd

    def body(do_ref, o_ref, g_ref, nw_ref, dout_ref, dg_ref, dnw_ref):
        @pl.when(pl.program_id(0) == 0)
        def _():
            dnw_ref[...] = jnp.zeros_like(dnw_ref)

        dn = do_ref[...].reshape(tb, nh, HD)
        o3 = o_ref[...].reshape(tb, nh, HD)
        g3 = g_ref[...].reshape(tb, nh, HD)
        nw3 = nw_ref[...].reshape(1, 1, HD)
        rh = lax.rsqrt(jnp.mean(o3 * o3, axis=-1, keepdims=True) + EPS)
        z = o3 * rh
        sg = _sigmoid(g3)
        sl = g3 * sg
        dnw_ref[...] += jnp.sum(jnp.sum(dn * sl * z, axis=1), axis=0, keepdims=True)
        dg_ref[...] = (dn * (z * nw3) * (sg * (1.0 + g3 * (1.0 - sg)))).reshape(tb, wd).astype(BF16)
        dz = dn * sl * nw3
        dout_ref[...] = (rh * (dz - z * jnp.mean(dz * z, axis=-1, keepdims=True))).reshape(tb, wd)

    return pl.pallas_call(
        body, name=name, grid=(t // tb,),
        in_specs=[pl.BlockSpec((tb, wd), lambda i: (i, col_blk)), _row_spec(tb, wd),
                  pl.BlockSpec((tb, wd), lambda i: (i, gb)), _vec_spec(HD)],
        out_specs=[_row_spec(tb, wd), _row_spec(tb, wd), _vec_spec(HD)],
        out_shape=[jax.ShapeDtypeStruct((t, wd), F32), jax.ShapeDtypeStruct((t, wd), BF16),
                   jax.ShapeDtypeStruct((1, HD), F32)],
        compiler_params=_cp(("arbitrary",)),
    )(dom, o, proj, nw)


def _tri(n, kind):
    r = lax.broadcasted_iota(jnp.int32, (n, n), 0)
    c = lax.broadcasted_iota(jnp.int32, (n, n), 1)
    if kind == "lower":
        return r >= c
    if kind == "strict":
        return r > c
    return r <= c


def _hg_gate(fl, l0, l1):
    mx = jnp.maximum(l0, l1)
    e0, e1 = jnp.exp(l0 - mx), jnp.exp(l1 - mx)
    lb = e0 / (e0 + e1)
    sg = _sigmoid(fl)
    f = lb + (1.0 - lb) * sg
    return lb, sg, f


def _hgrn2_fwd(proj, lb_logits, nh, name, comm=None):
    t = proj.shape[0]
    nc = t // CHUNK
    C = CHUNK
    lg = lb_logits.reshape(2, nh, 1, HD)

    hp = min(HP, nh)
    ng = nh // hp

    def one_head(hh, st, q_ref, f_ref, i_ref, lg_ref, p_sc, r_sc):
        sl = slice(hh * HD, (hh + 1) * HD)
        q, v = q_ref[:, sl], i_ref[:, sl]
        _, _, f = _hg_gate(f_ref[:, sl], lg_ref[0, hh], lg_ref[1, hh])
        k = 1.0 - f
        low = _tri(C, "lower")
        b = _nn(low.astype(F32), jnp.log(f), HI)
        yield
        lane_c = lax.broadcasted_iota(jnp.int32, (SB, C), 1)
        lane_h = lax.broadcasted_iota(jnp.int32, (SB, HD), 1)
        row_h = lax.broadcasted_iota(jnp.int32, (SB, HD), 0)
        ones = jnp.ones((HD, HD), F32)

        for i in range(NSB):
            qi, ki, bi = q[SB * i:SB * (i + 1)], k[SB * i:SB * (i + 1)], b[SB * i:SB * (i + 1)]
            for s in range(SB):
                e = jnp.exp(jnp.minimum(bi - bi[s:s + 1], 0.0))
                p = jnp.where(row_h >= s, qi * ki[s:s + 1] * e, 0.0)
                p_sc[hh, pl.ds((i * SB + s) * SB, SB), :] = p
            yield
        r_sc[hh] = _nn(p_sc[hh], ones, HIGH)
        yield
        a_rows = []
        for i in range(NSB):
            acc = jnp.zeros((SB, HD), F32)
            for s in range(SB):
                acc = jnp.where(lane_h == SB * i + s, r_sc[hh, pl.ds((i * SB + s) * SB, SB), :], acc)
            acc = acc[:, :C]
            if i > 0:
                r = b[SB * i - 1:SB * i]
                bi = b[SB * i:SB * (i + 1)]
                qf = q[SB * i:SB * (i + 1)] * jnp.exp(bi - r)
                kf = k * jnp.exp(jnp.minimum(r - b, 0.0))
                acc = acc + jnp.where(lane_c < SB * i, _nt(qf, kf, HIGH), 0.0)
            a_rows.append(acc)
            yield
        a = jnp.concatenate(a_rows, axis=0)
        bl = b[C - 1:C, :]
        o = _nn(_bf(a), _bf(v)) + _nt(_bf(q * jnp.exp(b)), _bf(st))
        yield
        new_st = st * jnp.exp(bl) + _tn(_bf(v), _bf(k * jnp.exp(bl - b)))
        return o, a, new_st

    def body(*refs):
        c, hg = pl.program_id(0), pl.program_id(1)
        step = c * ng + hg
        ins, outs, scratch, comm_begin, comm_end = _comm_hooks(
            comm, refs, 4, 3, step == 0, step == (3 * nc * ng) // 4, step == nc * ng - 1)
        o_ref, a_ref, st_ref = outs
        s_sc, p_sc, r_sc = scratch
        comm_begin()

        @pl.when(c == 0)
        def _():
            for hh in range(hp):
                s_sc[hg * hp + hh] = jnp.zeros((HD, HD), F32)

        sts = [s_sc[hg * hp + hh] for hh in range(hp)]
        res = _interleave([one_head(hh, sts[hh], *ins, p_sc, r_sc) for hh in range(hp)])
        for hh in range(hp):
            o_ref[:, hh * HD:(hh + 1) * HD] = res[hh][0]
            a_ref[0, hh] = res[hh][1]
            st_ref[0, hh] = sts[hh]
            s_sc[hg * hp + hh] = res[hh][2]
        comm_end()

    blk = lambda off: pl.BlockSpec((C, hp * HD), lambda c, g: (c, off // hp + g))
    cn = comm.n if comm is not None else 0
    return pl.pallas_call(
        body, name=name, grid=(nc, ng),
        in_specs=[blk(0), blk(nh), blk(2 * nh),
                  pl.BlockSpec((2, hp, 1, HD), lambda c, g: (0, g, 0, 0))] + [ANY] * cn,
        out_specs=[blk(0),
                   pl.BlockSpec((1, hp, C, C), lambda c, g: (c, g, 0, 0)),
                   pl.BlockSpec((1, hp, HD, HD), lambda c, g: (c, g, 0, 0))] + [ANY] * cn,
        out_shape=[jax.ShapeDtypeStruct((t, nh * HD), F32),
                   jax.ShapeDtypeStruct((nc, nh, C, C), F32),
                   jax.ShapeDtypeStruct((nc, nh, HD, HD), F32)] + (comm.out_shapes() if cn else []),
        scratch_shapes=[pltpu.VMEM((nh, HD, HD), F32), pltpu.VMEM((hp, C * SB, HD), F32),
                        pltpu.VMEM((hp, C * SB, HD), F32)] + (comm.scratch() if cn else []),
        compiler_params=_cp(("arbitrary", "arbitrary")),
    )(proj, proj, proj, lg, *(comm.arrays if cn else []))


def _hgrn2_bwd(proj, lb_logits, do, a_sv, st_sv, nh, name, comm=None):
    t = proj.shape[0]
    nc = t // CHUNK
    C = CHUNK
    lg = lb_logits.reshape(2, nh, 1, HD)
    hp = min(HP, nh)
    ng = nh // hp

    def one_head(hh, dst, q_ref, f_ref, i_ref, lg_ref, do_ref, a_ref, st_ref, p_sc, r_sc):
        sl = slice(hh * HD, (hh + 1) * HD)
        q, v, do_ = q_ref[:, sl], i_ref[:, sl], do_ref[:, sl]
        lb, sg, f = _hg_gate(f_ref[:, sl], lg_ref[0, hh], lg_ref[1, hh])
        k = 1.0 - f
        low = _tri(C, "lower")
        b = _nn(low.astype(F32), jnp.log(f), HI)
        yield
        bl = b[C - 1:C, :]
        eb, ekb = jnp.exp(b), jnp.exp(bl - b)
        qb, kb = q * eb, k * ekb
        a, st = a_ref[0, hh], st_ref[0, hh]

        da = jnp.where(low, _nt(_bf(do_), _bf(v)), 0.0)
        yield
        dv = _tn(_bf(a), _bf(do_)) + _nt(_bf(kb), _bf(dst))
        yield
        dqb = _nn(_bf(do_), _bf(st))
        dkb = _nn(_bf(v), _bf(dst))
        yield

        row = lax.broadcasted_iota(jnp.int32, (C, HD), 0)
        lane_c = lax.broadcasted_iota(jnp.int32, (SB, C), 1)
        row_h = lax.broadcasted_iota(jnp.int32, (SB, HD), 0)
        ones = jnp.ones((HD, HD), F32)
        sel = (lax.broadcasted_iota(jnp.int32, (C, C * SB), 0)
               == jnp.right_shift(lax.broadcasted_iota(jnp.int32, (C, C * SB), 1), SB.bit_length() - 1)).astype(F32)

        for i in range(NSB):
            doi, vi = do_[SB * i:SB * (i + 1)], v[SB * i:SB * (i + 1)]
            for s in range(SB):
                p_sc[hh, pl.ds((i * SB + s) * SB, SB), :] = doi * vi[s:s + 1]
            yield
        r_sc[hh] = _nn(p_sc[hh], ones, HIGH)
        yield
        dq_rows = []
        dk_off = jnp.zeros((C, HD), F32)
        for i in range(NSB):
            qi, ki, bi = q[SB * i:SB * (i + 1)], k[SB * i:SB * (i + 1)], b[SB * i:SB * (i + 1)]
            acc = jnp.zeros((SB, HD), F32)
            for s in range(SB):
                e = jnp.exp(jnp.minimum(bi - bi[s:s + 1], 0.0))
                g = jnp.where(row_h >= s, r_sc[hh, pl.ds((i * SB + s) * SB, SB), :] * e, 0.0)
                acc = acc + g * ki[s:s + 1]
                p_sc[hh, pl.ds((i * SB + s) * SB, SB), :] = g * qi
            yield
            if i > 0:
                r = b[SB * i - 1:SB * i]
                fq = jnp.exp(bi - r)
                fk = jnp.exp(jnp.minimum(r - b, 0.0))
                dai = jnp.where(lane_c < SB * i, da[SB * i:SB * (i + 1)], 0.0)
                acc = acc + _nn(dai, k * fk, HIGH) * fq
                dk_off = dk_off + _tn(dai, qi * fq, HIGH) * fk
                yield
            dq_rows.append(acc)
        dqi = jnp.concatenate(dq_rows, axis=0)
        dq = dqi + dqb * eb
        dk_inter = dkb * ekb
        dk = _nn(sel, p_sc[hh], HIGH) + dk_off + dk_inter
        yield
        db = q * dq - k * dk
        extra = (jnp.sum(k * dk_inter, axis=0, keepdims=True)
                 + jnp.exp(bl) * jnp.sum(dst * st, axis=0, keepdims=True))
        db = db + jnp.where(row == C - 1, extra, 0.0)
        dlf = _nn(_tri(C, "upper").astype(F32), db, HI)
        yield
        df = dlf / f - dk
        dfl = (df * (1.0 - lb) * sg * (1.0 - sg)).astype(BF16)
        dl = jnp.sum(df * (1.0 - sg), axis=0, keepdims=True) * (lb * (1.0 - lb))
        new_dst = dst * jnp.exp(bl) + _tn(_bf(do_), _bf(qb))
        return dq.astype(BF16), dfl, dv.astype(BF16), dl, new_dst

    def body(*refs):
        c, hg = pl.program_id(0), pl.program_id(1)
        step = c * ng + hg
        ins, outs, scratch, comm_begin, comm_end = _comm_hooks(
            comm, refs, 7, 4, step == 0, step == (3 * nc * ng) // 4, step == nc * ng - 1)
        dq_ref, df_ref, di_ref, dl_ref = outs
        ds_sc, p_sc, r_sc = scratch
        comm_begin()

        @pl.when(c == 0)
        def _():
            for hh in range(hp):
                ds_sc[hg * hp + hh] = jnp.zeros((HD, HD), F32)

        @pl.when(step == 0)
        def _():
            dl_ref[...] = jnp.zeros_like(dl_ref)

        dsts = [ds_sc[hg * hp + hh] for hh in range(hp)]
        res = _interleave([one_head(hh, dsts[hh], *ins, p_sc, r_sc) for hh in range(hp)])
        for hh in range(hp):
            sl = slice(hh * HD, (hh + 1) * HD)
            dq_ref[:, sl], df_ref[:, sl], di_ref[:, sl] = res[hh][0], res[hh][1], res[hh][2]
            dl_ref[pl.ds(hg * hp + hh, 1), :] += res[hh][3]
            ds_sc[hg * hp + hh] = res[hh][4]
        comm_end()

    rblk = lambda off: pl.BlockSpec((C, hp * HD), lambda c, g: (nc - 1 - c, off // hp + g))
    oblk = pl.BlockSpec((C, hp * HD), lambda c, g: (nc - 1 - c, g))
    cn = comm.n if comm is not None else 0
    return pl.pallas_call(
        body, name=name, grid=(nc, ng),
        in_specs=[rblk(0), rblk(nh), rblk(2 * nh),
                  pl.BlockSpec((2, hp, 1, HD), lambda c, g: (0, g, 0, 0)),
                  oblk,
                  pl.BlockSpec((1, hp, C, C), lambda c, g: (nc - 1 - c, g, 0, 0)),
                  pl.BlockSpec((1, hp, HD, HD), lambda c, g: (nc - 1 - c, g, 0, 0))] + [ANY] * cn,
        out_specs=[oblk, oblk, oblk, pl.BlockSpec((nh, HD), lambda c, g: (0, 0))] + [ANY] * cn,
        out_shape=[jax.ShapeDtypeStruct((t, nh * HD), BF16)] * 3 + [jax.ShapeDtypeStruct((nh, HD), F32)]
        + (comm.out_shapes() if cn else []),
        scratch_shapes=[pltpu.VMEM((nh, HD, HD), F32), pltpu.VMEM((hp, C * SB, HD), F32),
                        pltpu.VMEM((hp, C * SB, HD), F32)] + (comm.scratch() if cn else []),
        compiler_params=_cp(("arbitrary", "arbitrary")),
    )(proj, proj, proj, lg, do, a_sv, st_sv, *(comm.arrays if cn else []))


def _shift_rows(u, d, row):
    t = u.shape[0]
    if d == 0:
        return u
    rolled = pltpu.roll(u, d % t, 0)
    if d > 0:
        return jnp.where(row >= d, rolled, 0.0)
    return jnp.where(row < t + d, rolled, 0.0)


def _gdn_prep(proj, conv_w, blk0, nh, name):
    t = proj.shape[0]
    scale = HD ** -0.5

    def body(u_ref, w_ref, o_ref):
        j = pl.program_id(0)
        u, w = u_ref[...], w_ref[...]
        row = lax.broadcasted_iota(jnp.int32, (t, HD), 0)
        y = w[CONV_K - 1:CONV_K, :] * u
        for d in range(1, CONV_K):
            y = y + w[CONV_K - 1 - d:CONV_K - d, :] * _shift_rows(u, d, row)
        a = y * _sigmoid(y)
        n = a * lax.rsqrt(jnp.sum(a * a, axis=-1, keepdims=True) + EPS)
        n = n * jnp.where(j < nh, scale, 1.0)
        o_ref[...] = jnp.where(j < 2 * nh, n, a)

    return pl.pallas_call(
        body, name=name, grid=(3 * nh,),
        in_specs=[pl.BlockSpec((t, HD), lambda j: (0, blk0 + j)), pl.BlockSpec((CONV_K, HD), lambda j: (0, j))],
        out_specs=pl.BlockSpec((t, HD), lambda j: (0, j)),
        out_shape=jax.ShapeDtypeStruct((t, 3 * nh * HD), F32),
        compiler_params=_cp(("parallel",)),
    )(proj, conv_w)


def _gdn_prep_bwd(proj, conv_w, dq, dk, dv, blk0, nh, name):
    t = proj.shape[0]
    scale = HD ** -0.5

    def body(u_ref, w_ref, dq_ref, dk_ref, dv_ref, du_ref, dw_ref):
        j = pl.program_id(0)
        u, w = u_ref[...], w_ref[...]
        dout = jnp.where(j < nh, dq_ref[...], jnp.where(j < 2 * nh, dk_ref[...], dv_ref[...]))
        row = lax.broadcasted_iota(jnp.int32, (t, HD), 0)
        us = [_shift_rows(u, d, row) for d in range(CONV_K)]
        y = w[CONV_K - 1:CONV_K, :] * us[0]
        for d in range(1, CONV_K):
            y = y + w[CONV_K - 1 - d:CONV_K - d, :] * us[d]
        sg = _sigmoid(y)
        a = y * sg
        rs = lax.rsqrt(jnp.sum(a * a, axis=-1, keepdims=True) + EPS)
        n = a * rs
        dn = dout * jnp.where(j < nh, scale, 1.0)
        da_n = rs * (dn - n * jnp.sum(dn * n, axis=-1, keepdims=True))
        da = jnp.where(j < 2 * nh, da_n, dout)
        dy = da * (sg * (1.0 + y * (1.0 - sg)))
        du = w[CONV_K - 1:CONV_K, :] * dy
        for d in range(1, CONV_K):
            du = du + w[CONV_K - 1 - d:CONV_K - d, :] * _shift_rows(dy, -d, row)
        du_ref[...] = du.astype(BF16)
        for d in range(CONV_K):
            dw_ref[CONV_K - 1 - d:CONV_K - d, :] = jnp.sum(dy * us[d], axis=0, keepdims=True)

    return pl.pallas_call(
        body, name=name, grid=(3 * nh,),
        in_specs=[pl.BlockSpec((t, HD), lambda j: (0, blk0 + j)), pl.BlockSpec((CONV_K, HD), lambda j: (0, j))]
        + [pl.BlockSpec((t, HD), functools.partial(lambda p, j: (0, jnp.clip(j - p * nh, 0, nh - 1)), p))
           for p in range(3)],
        out_specs=[pl.BlockSpec((t, HD), lambda j: (0, j)), pl.BlockSpec((CONV_K, HD), lambda j: (0, j))],
        out_shape=[jax.ShapeDtypeStruct((t, 3 * nh * HD), BF16), jax.ShapeDtypeStruct((CONV_K, 3 * nh * HD), F32)],
        compiler_params=_cp(("arbitrary",)),
    )(proj, conv_w, dq, dk, dv)


def _gdn_gates(ab, alog, dtb, h, nh):
    lane = lax.broadcasted_iota(jnp.int32, ab.shape, 1)
    x = ab + dtb
    sp = jnp.maximum(x, 0.0) + jnp.log(1.0 + jnp.exp(-jnp.abs(x)))
    ea = jnp.exp(alog)
    la_all = -ea * sp
    beta_all = _sigmoid(ab)
    pick = lambda val, ln: jnp.sum(jnp.where(lane == ln, val, 0.0), axis=1, keepdims=True)
    la = pick(la_all, h)
    beta = pick(beta_all, nh + h)
    dla_da = pick(-ea * _sigmoid(x), h)
    return la, beta, dla_da


def _unit_lower_inverses(ms, C):
    nb = C // SB
    sh = SB.bit_length() - 1
    rowb = jnp.right_shift(lax.broadcasted_iota(jnp.int32, (C, C), 0), sh)
    colb = jnp.right_shift(lax.broadcasted_iota(jnp.int32, (C, C), 1), sh)
    eye = (lax.broadcasted_iota(jnp.int32, (SB, SB), 0) == lax.broadcasted_iota(jnp.int32, (SB, SB), 1)).astype(F32)
    spread = (jnp.bitwise_and(lax.broadcasted_iota(jnp.int32, (SB, C), 1), SB - 1)
              == lax.broadcasted_iota(jnp.int32, (SB, C), 0)).astype(F32)
    blocks = [[m[SB * i:SB * (i + 1), SB * i:SB * (i + 1)] for i in range(nb)] for m in ms]
    xs = [[eye] * nb for _ in ms]
    for s in range(SB - 1):
        xs = [[x - b[:, s:s + 1] * x[s:s + 1, :] for x, b in zip(xh, bh)] for xh, bh in zip(xs, blocks)]
    ts = [jnp.where(rowb == colb, _nn(jnp.concatenate(xh, axis=0), spread, HIGH), 0.0) for xh in xs]
    lvl = 1
    while (1 << lvl) <= nb:
        off = ((jnp.right_shift(rowb, lvl) == jnp.right_shift(colb, lvl))
               & (jnp.right_shift(rowb, lvl - 1) != jnp.right_shift(colb, lvl - 1)))
        ts = [t - _nn(t, _nn(jnp.where(off, m, 0.0), t, HIGH), HIGH) for t, m in zip(ts, ms)]
        lvl += 1
    return ts


def _gdn_chunks(qs, ks, vs, las, betas, C):
    low, strict = _tri(C, "lower"), _tri(C, "strict")
    eye = (lax.broadcasted_iota(jnp.int32, (C, C), 0) == lax.broadcasted_iota(jnp.int32, (C, C), 1)).astype(F32)
    g_bs = [_nn(low.astype(F32), jnp.broadcast_to(la, (C, HD)), HI) for la in las]
    ps = [_nt(k, k, HIGH) for k in ks]
    qks = [_nt(_bf(q), _bf(k)) for q, k in zip(qs, ks)]
    chs = []
    for g_b, p, qk_raw, beta in zip(g_bs, ps, qks, betas):
        g_c = g_b[:, :C]
        gamma = jnp.where(low, jnp.exp(jnp.minimum(g_c - g_c.T, 0.0)), 0.0)
        gl = g_b[C - 1:C, :]
        chs.append(dict(gamma=gamma, eg=jnp.exp(g_b), gl=gl, ekt=jnp.exp(gl - g_b), p=p,
                        m=jnp.where(strict, beta * p * gamma, 0.0), qk_raw=qk_raw))
    xs = _unit_lower_inverses([ch["m"] for ch in chs], C)
    r_ws = [k * (beta * ch["eg"]) for ch, k, beta in zip(chs, ks, betas)]
    uws = [_nn(x, jnp.concatenate([v * beta, r_w], axis=1), HIGH) for x, v, beta, r_w in zip(xs, vs, betas, r_ws)]
    for ch, x, r_w, uw in zip(chs, xs, r_ws, uws):
        ch.update(x=x, r_w=r_w, uw=uw)
    return chs


def _gdn_fwd(qkv, proj, ab_blk, alog, dtb, nh, name, comm=None):
    t = qkv.shape[0]
    nc = t // CHUNK
    C = CHUNK
    hp = min(HP, nh)
    ng = nh // hp

    def body(*refs):
        c, hg = pl.program_id(0), pl.program_id(1)
        step = c * ng + hg
        ins, outs, scratch, comm_begin, comm_end = _comm_hooks(
            comm, refs, 6, 3, step == 0, step == (3 * nc * ng) // 4, step == nc * ng - 1)
        q_ref, k_ref, v_ref, ab_ref, al_ref, dt_ref = ins
        o_ref, x_ref, st_ref = outs
        s_sc, = scratch
        comm_begin()

        @pl.when(c == 0)
        def _():
            for hh in range(hp):
                s_sc[hg * hp + hh] = jnp.zeros((HD, HD), F32)

        sls = [slice(hh * HD, (hh + 1) * HD) for hh in range(hp)]
        qs, ks, vs = [q_ref[:, sl] for sl in sls], [k_ref[:, sl] for sl in sls], [v_ref[:, sl] for sl in sls]
        sts = [s_sc[hg * hp + hh] for hh in range(hp)]
        gates = [_gdn_gates(ab_ref[...], al_ref[...], dt_ref[...], hg * hp + hh, nh) for hh in range(hp)]
        chs = _gdn_chunks(qs, ks, vs, [g[0] for g in gates], [g[1] for g in gates], C)
        stbs = [_bf(st) for st in sts]
        vns = [ch["uw"][:, :HD] - _nt(_bf(ch["uw"][:, HD:]), stb) for ch, stb in zip(chs, stbs)]
        o_st = [_nt(_bf(q * ch["eg"]), stb) for q, ch, stb in zip(qs, chs, stbs)]
        outs_ = [o + _nn(_bf(ch["qk_raw"] * ch["gamma"]), _bf(vn)) for o, ch, vn in zip(o_st, chs, vns)]
        new_sts = [st * jnp.exp(ch["gl"]) + _tn(_bf(vn), _bf(k * ch["ekt"]))
                   for st, ch, vn, k in zip(sts, chs, vns, ks)]
        for hh in range(hp):
            o_ref[:, sls[hh]] = outs_[hh]
            x_ref[0, hh] = chs[hh]["x"]
            st_ref[0, hh] = sts[hh]
            s_sc[hg * hp + hh] = new_sts[hh]
        comm_end()

    blk = lambda off: pl.BlockSpec((C, hp * HD), lambda c, g: (c, off // hp + g))
    vec = pl.BlockSpec((1, HD), lambda c, g: (0, 0))
    cn = comm.n if comm is not None else 0
    return pl.pallas_call(
        body, name=name, grid=(nc, ng),
        in_specs=[blk(0), blk(nh), blk(2 * nh), pl.BlockSpec((C, HD), lambda c, g: (c, ab_blk)), vec, vec]
        + [ANY] * cn,
        out_specs=[blk(0),
                   pl.BlockSpec((1, hp,C, C), lambda c, g: (c, g, 0, 0)),
                   pl.BlockSpec((1, hp,HD, HD), lambda c, g: (c, g, 0, 0))] + [ANY] * cn,
        out_shape=[jax.ShapeDtypeStruct((t, nh * HD), F32),
                   jax.ShapeDtypeStruct((nc, nh, C, C), F32),
                   jax.ShapeDtypeStruct((nc, nh, HD, HD), F32)] + (comm.out_shapes() if cn else []),
        scratch_shapes=[pltpu.VMEM((nh, HD, HD), F32)] + (comm.scratch() if cn else []),
        compiler_params=_cp(("arbitrary", "arbitrary")),
    )(qkv, qkv, qkv, proj, alog, dtb, *(comm.arrays if cn else []))


def _gdn_bwd(qkv, proj, ab_blk, alog, dtb, do, x_sv, st_sv, nh, name, comm=None):
    t = qkv.shape[0]
    nc = t // CHUNK
    C = CHUNK
    hp = min(HP, nh)
    ng = nh // hp

    def one_head(h, hh, dst, q_ref, k_ref, v_ref, ab_ref, al_ref, dt_ref, do_ref, x_ref, st_ref):
        sl = slice(hh * HD, (hh + 1) * HD)
        q, k, v, do_ = q_ref[:, sl], k_ref[:, sl], v_ref[:, sl], do_ref[:, sl]
        la, beta, dla_da = _gdn_gates(ab_ref[...], al_ref[...], dt_ref[...], h, nh)
        low, strict = _tri(C, "lower"), _tri(C, "strict")
        g_b = _nn(low.astype(F32), jnp.broadcast_to(la, (C, HD)), HI)
        yield
        g_c = g_b[:, :C]
        gamma = jnp.where(low, jnp.exp(jnp.minimum(g_c - g_c.T, 0.0)), 0.0)
        eg = jnp.exp(g_b)
        gl = g_b[C - 1:C, :]
        ekt = jnp.exp(gl - g_b)
        egl = jnp.exp(gl)
        p = _nt(k, k, HIGH)
        yield
        x = x_ref[0, hh]
        r_w = k * (beta * eg)
        rhs = jnp.concatenate([v * beta, r_w], axis=1)
        uw = _nn(x, rhs, HIGH)
        yield
        u, w = uw[:, :HD], uw[:, HD:]
        qk_raw = _nt(_bf(q), _bf(k))
        yield
        qk = qk_raw * gamma
        st = st_ref[0, hh]
        stb, dstb = _bf(st), _bf(dst)
        vn = u - _nt(_bf(w), stb)
        yield
        qd, kt = q * eg, k * ekt

        dvn = _tn(_bf(qk), _bf(do_)) + _nt(_bf(kt), dstb)
        yield
        dq2 = jnp.where(low, _nt(_bf(do_), _bf(vn)), 0.0)
        yield
        dqd = _nn(_bf(do_), stb)
        yield
        dkt = _nn(_bf(vn), dstb)
        yield
        dw = -_nn(_bf(dvn), stb)
        yield
        dxx = jnp.concatenate([dvn, dw], axis=1)
        dr = _tn(x, dxx, HIGH)
        yield
        dm = -jnp.where(strict, _nt(dr, uw, HIGH), 0.0)
        yield
        dr_u, dr_w = dr[:, :HD], dr[:, HD:]
        rsum = lambda z: jnp.sum(z, axis=1, keepdims=True)

        dv = dr_u * beta
        dmg = dm * gamma
        dbeta = rsum(dr_u * v) + rsum(dr_w * k) * eg[:, :1] + rsum(dmg * p)
        yield
        dp = dmg * beta
        dq2g = dq2 * gamma
        dk = (dr_w * (beta * eg) + dkt * ekt + _tn(_bf(dq2g), _bf(q))
              + _nn(_bf(dp + dp.T), _bf(k)))
        yield
        dq = dqd * eg + _nn(_bf(dq2g), _bf(k))
        yield
        e = dp * p + dq2g * qk_raw
        t_kt = rsum(dkt * kt)
        dg = rsum(dqd * qd) + rsum(dr_w * r_w) - t_kt + rsum(e) - rsum(e.T)
        yield
        dgl = jnp.sum(t_kt, axis=0, keepdims=True) + jnp.sum(dst * st, keepdims=True) * egl[:, :1]
        rowc = lax.broadcasted_iota(jnp.int32, (C, 1), 0)
        dg = dg + jnp.where(rowc == C - 1, dgl, 0.0)
        dla = _nn(_tri(C, "upper").astype(F32), jnp.broadcast_to(dg, (C, HD)), HI)[:, :1]
        yield
        da = dla * dla_da
        db = dbeta * beta * (1.0 - beta)
        lane = lax.broadcasted_iota(jnp.int32, (C, HD), 1)
        dab = jnp.where(lane == h, da, 0.0) + jnp.where(lane == nh + h, db, 0.0)
        lane1 = lax.broadcasted_iota(jnp.int32, (1, HD), 1)
        d_alog = jnp.where(lane1 == h, jnp.sum(dla * la, axis=0, keepdims=True), 0.0)
        d_dtb = jnp.where(lane1 == h, jnp.sum(da, axis=0, keepdims=True), 0.0)
        new_dst = dst * egl + _tn(_bf(do_), _bf(qd)) - _tn(_bf(dvn), _bf(w))
        return dab, d_alog, d_dtb, new_dst, dq, dk, dv

    def body(*refs):
        c, hg = pl.program_id(0), pl.program_id(1)
        step = c * ng + hg
        ins, outs, scratch, comm_begin, comm_end = _comm_hooks(
            comm, refs, 9, 5, step == 0, step == (3 * nc * ng) // 4, step == nc * ng - 1)
        dq_ref, dk_ref, dv_ref, dab_ref, dpar_ref = outs
        ds_sc, = scratch
        comm_begin()

        @pl.when(c == 0)
        def _():
            for hh in range(hp):
                ds_sc[hg * hp + hh] = jnp.zeros((HD, HD), F32)

        @pl.when(step == 0)
        def _():
            dpar_ref[...] = jnp.zeros_like(dpar_ref)

        @pl.when(hg == 0)
        def _():
            dab_ref[...] = jnp.zeros_like(dab_ref)

        dsts = [ds_sc[hg * hp + hh] for hh in range(hp)]
        res = _interleave([one_head(hg * hp + hh, hh, dsts[hh], *ins) for hh in range(hp)])
        for hh in range(hp):
            sl = slice(hh * HD, (hh + 1) * HD)
            ds_sc[hg * hp + hh] = res[hh][3]
            dq_ref[:, sl], dk_ref[:, sl], dv_ref[:, sl] = res[hh][4], res[hh][5], res[hh][6]
        dab_ref[...] += sum(r[0] for r in res[1:]) + res[0][0]
        dpar_ref[0:1, :] += sum(r[1] for r in res[1:]) + res[0][1]
        dpar_ref[1:2, :] += sum(r[2] for r in res[1:]) + res[0][2]
        comm_end()

    rblk = lambda off: pl.BlockSpec((C, hp * HD), lambda c, g: (nc - 1 - c, off // hp + g))
    oblk = pl.BlockSpec((C, hp * HD), lambda c, g: (nc - 1 - c, g))
    vec = pl.BlockSpec((1, HD), lambda c, g: (0, 0))
    cn = comm.n if comm is not None else 0
    return pl.pallas_call(
        body, name=name, grid=(nc, ng),
        in_specs=[rblk(0), rblk(nh), rblk(2 * nh),
                  pl.BlockSpec((C, HD), lambda c, g: (nc - 1 - c, ab_blk)), vec, vec, oblk,
                  pl.BlockSpec((1, hp,C, C), lambda c, g: (nc - 1 - c, g, 0, 0)),
                  pl.BlockSpec((1, hp,HD, HD), lambda c, g: (nc - 1 - c, g, 0, 0))] + [ANY] * cn,
        out_specs=[oblk, oblk, oblk,
                   pl.BlockSpec((C, HD), lambda c, g: (nc - 1 - c, 0)),
                   pl.BlockSpec((8, HD), lambda c, g: (0, 0))] + [ANY] * cn,
        out_shape=[jax.ShapeDtypeStruct((t, nh * HD), F32)] * 3
        + [jax.ShapeDtypeStruct((t, HD), F32), jax.ShapeDtypeStruct((8, HD), F32)]
        + (comm.out_shapes() if cn else []),
        scratch_shapes=[pltpu.VMEM((nh, HD, HD), F32)] + (comm.scratch() if cn else []),
        compiler_params=_cp(("arbitrary", "arbitrary")),
    )(qkv, qkv, qkv, proj, alog, dtb, do, x_sv, st_sv, *(comm.arrays if cn else []))


def _ada_fwd(c_all, w, b, name):
    nb, d = c_all.shape
    n = w.shape[1]
    tn = _pick(n, 512)

    def body(c_ref, w_ref, b_ref, o_ref):
        cv = c_ref[...]
        o_ref[...] = _nn(cv * _sigmoid(cv), w_ref[...], HI) + b_ref[...]

    return pl.pallas_call(
        body, name=name, grid=(n // tn,),
        in_specs=[pl.BlockSpec((nb, d), lambda j: (0, 0)), pl.BlockSpec((d, tn), lambda j: (0, j)),
                  pl.BlockSpec((1, tn), lambda j: (0, j))],
        out_specs=pl.BlockSpec((nb, tn), lambda j: (0, j)),
        out_shape=jax.ShapeDtypeStruct((nb, n), F32),
        compiler_params=_cp(("parallel",)),
    )(c_all, w, b)


def _ada_wgrad(c_all, dmod, name):
    nb, d = c_all.shape
    n = dmod.shape[1]
    tn = _pick(n, 512)

    def body(c_ref, g_ref, o_ref):
        cv = c_ref[...]
        o_ref[...] = _tn(cv * _sigmoid(cv), g_ref[...], HI)

    return pl.pallas_call(
        body, name=name, grid=(n // tn,),
        in_specs=[pl.BlockSpec((nb, d), lambda j: (0, 0)), pl.BlockSpec((nb, tn), lambda j: (0, j))],
        out_specs=pl.BlockSpec((d, tn), lambda j: (0, j)),
        out_shape=jax.ShapeDtypeStruct((d, n), F32),
        compiler_params=_cp(("parallel",)),
    )(c_all, dmod)


def _adamw(w, m, v, g, name, parts=False):
    lead = w.ndim == 3
    r, cdim = w.shape[-2:]
    cap = max(SUBLANES, ADAM_BLOCK_ELEMS // cdim // SUBLANES * SUBLANES)
    tr = r if r <= cap else _pick_rows(r, cap)
    bc1 = 1.0 - ADAM_B1 ** ADAM_STEP
    bc2 = 1.0 - ADAM_B2 ** ADAM_STEP

    glist = list(g) if isinstance(g, (list, tuple)) else [g]
    bounds = [0]
    for ga in glist:
        bounds.append(bounds[-1] + ga.shape[-2] // tr)

    def body(w_ref, m_ref, v_ref, *rest):
        g_refs, (go_ref, d_ref, mo_ref, vo_ref) = rest[:len(glist)], rest[len(glist):]
        if parts:
            sums = []
            for g_ref in g_refs:
                gv = g_ref[0].astype(F32)
                for s in range(1, N_DEV):
                    gv = gv + g_ref[s].astype(F32)
                sums.append(gv)
            gv = sums[-1]
            for p in range(len(sums) - 2, -1, -1):
                gv = jnp.where(pl.program_id(0) < bounds[p + 1], sums[p], gv)
        else:
            gv = g_refs[0][...]
        wv = w_ref[...]
        mn = ADAM_B1 * m_ref[...] + (1.0 - ADAM_B1) * gv
        vn = ADAM_B2 * v_ref[...] + (1.0 - ADAM_B2) * (gv * gv)
        m_hat = mn / bc1
        v_hat = vn / bc2
        go_ref[...] = gv
        d_ref[...] = -ADAM_LR * (m_hat / (jnp.sqrt(v_hat) + ADAM_EPS) + ADAM_WD * wv)
        mo_ref[...] = mn
        vo_ref[...] = vn

    flat = pl.BlockSpec((tr, cdim), lambda i: (i, 0))
    spec = pl.BlockSpec((None, tr, cdim), lambda i: (0, i, 0)) if lead else flat
    def piece_spec(p):
        lo, n = bounds[p], bounds[p + 1] - bounds[p]
        return pl.BlockSpec((N_DEV, tr, cdim), lambda i: (0, jnp.clip(i - lo, 0, n - 1), 0))

    gspecs = [piece_spec(p) for p in range(len(glist))] if parts else [flat]
    return pl.pallas_call(
        body, name=name, grid=(r // tr,),
        in_specs=[spec, spec, spec] + gspecs,
        out_specs=[spec] * 4,
        out_shape=[jax.ShapeDtypeStruct(w.shape, F32)] * 4,
        compiler_params=_cp(("arbitrary",)),
    )(w, m, v, *glist)


def _pick_rows(r, pref):
    t = pref
    while r % t:
        t -= 8
    assert t > 0
    return t


def _dev_index(x, y, c):
    return 4 * x + 2 * y + c


class _Comm:
    def __init__(self, kind, arrays):
        self.kind, self.n = kind, len(arrays)
        self.arrays = [a[0] if isinstance(a, tuple) else a for a in arrays]
        self.rows = [(a[1], a[2]) if isinstance(a, tuple) else None for a in arrays]

    def out_shapes(self):
        if self.kind == "gather":
            return [jax.ShapeDtypeStruct((N_DEV,) + a.shape, a.dtype) for a in self.arrays]
        return [jax.ShapeDtypeStruct(a.shape if r is None else (N_DEV, r[1]) + a.shape[2:], a.dtype)
                for a, r in zip(self.arrays, self.rows)]

    def scratch(self):
        return [pltpu.SemaphoreType.DMA((self.n, 7)), pltpu.SemaphoreType.DMA((self.n, 7)),
                pltpu.SemaphoreType.DMA((self.n,))]

    def _gather_parts(self, ins, outs, sems):
        send_sems, recv_sems, local_sems = sems
        x, y, c = lax.axis_index("x"), lax.axis_index("y"), lax.axis_index("c")
        me, sibling = (x, y, c), (x, y, 1 - c)
        chips = [(1 - x, y), (x, 1 - y), (1 - x, 1 - y)]

        def copy(a, k, block, to, src=None):
            slot = outs[a].at[_dev_index(*block)]
            return pltpu.make_async_remote_copy(
                src_ref=slot if src is None else src, dst_ref=slot,
                send_sem=send_sems.at[a, k], recv_sem=recv_sems.at[a, k],
                device_id=to, device_id_type=MESH)

        n = self.n
        mine = [pltpu.make_async_copy(ins[a], outs[a].at[_dev_index(*me)], local_sems.at[a]) for a in range(n)]
        first = []
        for a in range(n):
            first.append(copy(a, 0, me, sibling, src=ins[a]))
            first += [copy(a, 1 + j, me, (*chip, c), src=ins[a]) for j, chip in enumerate(chips)]
        landed = [copy(a, 1 + j, (*chip, c), me) for j, chip in enumerate(chips) for a in range(n)]
        passed = [copy(a, 4 + j, (*chip, c), sibling) for j, chip in enumerate(chips) for a in range(n)]
        late = []
        for a in range(n):
            late.append(copy(a, 0, sibling, me))
            late += [copy(a, 4 + j, (*chip, 1 - c), me) for j, chip in enumerate(chips)]
        return mine, first, landed, passed, late

    def _exchange_parts(self, ins, outs, sems):
        send_sems, recv_sems, local_sems = sems
        x, y, c = lax.axis_index("x"), lax.axis_index("y"), lax.axis_index("c")
        my = _dev_index(x, y, c)
        n = self.n

        def block(a, j):
            r = self.rows[a]
            return ins[a].at[j] if r is None else ins[a].at[j, pl.ds(r[0], r[1])]

        mine = [pltpu.make_async_copy(block(a, my), outs[a].at[my], local_sems.at[a]) for a in range(n)]
        sends, recvs = [], []
        for k in range(1, N_DEV):
            px = (1 - x) if (k >> 2) & 1 else x
            py = (1 - y) if (k >> 1) & 1 else y
            pc = (1 - c) if k & 1 else c
            peer = _dev_index(px, py, pc)
            for a in range(n):
                sends.append(pltpu.make_async_remote_copy(
                    src_ref=block(a, peer), dst_ref=outs[a].at[my],
                    send_sem=send_sems.at[a, k - 1], recv_sem=recv_sems.at[a, k - 1],
                    device_id=(px, py, pc), device_id_type=MESH))
                recvs.append(pltpu.make_async_remote_copy(
                    src_ref=block(a, my), dst_ref=outs[a].at[peer],
                    send_sem=send_sems.at[a, k - 1], recv_sem=recv_sems.at[a, k - 1],
                    device_id=(x, y, c), device_id_type=MESH))
        return mine, sends, recvs

    def start(self, ins, outs, sems):
        if self.kind == "gather":
            mine, first, _, _, _ = self._gather_parts(ins, outs, sems)
        else:
            mine, first, _ = self._exchange_parts(ins, outs, sems)
        for cp in mine + first:
            cp.start()

    def mid(self, ins, outs, sems):
        if self.kind == "gather":
            _, _, landed, passed, _ = self._gather_parts(ins, outs, sems)
            for got, fwd in zip(landed, passed):
                got.wait_recv()
                fwd.start()

    def finish(self, ins, outs, sems):
        if self.kind == "gather":
            mine, first, _, passed, late = self._gather_parts(ins, outs, sems)
            for cp in late:
                cp.wait_recv()
            for cp in first + passed:
                cp.wait_send()
        else:
            mine, sends, recvs = self._exchange_parts(ins, outs, sems)
            for cp in sends:
                cp.wait_send()
            for cp in recvs:
                cp.wait_recv()
        for cp in mine:
            cp.wait()

    def run(self, name):
        n = self.n

        def body(*refs):
            ins, outs, sems = refs[:n], refs[n:2 * n], refs[2 * n:]
            self.start(ins, outs, sems)
            self.mid(ins, outs, sems)
            self.finish(ins, outs, sems)

        return pl.pallas_call(
            body, name=name, in_specs=[ANY] * n, out_specs=[ANY] * n,
            out_shape=self.out_shapes(), scratch_shapes=self.scratch(),
        )(*self.arrays)


def _all_gather(arrays, name):
    return _Comm("gather", arrays).run(name)


def _comm_hooks(comm, refs, n_in, n_out, first, middle, last):
    cn = comm.n if comm is not None else 0
    ins, cins = refs[:n_in], refs[n_in:n_in + cn]
    outs, couts = refs[n_in + cn:n_in + cn + n_out], refs[n_in + cn + n_out:n_in + 2 * cn + n_out]
    rest = refs[n_in + 2 * cn + n_out:]
    scratch, csems = (rest[:len(rest) - 3], rest[len(rest) - 3:]) if cn else (rest, ())

    def begin():
        if cn:
            pl.when(first)(lambda: comm.start(cins, couts, csems))
            pl.when(middle)(lambda: comm.mid(cins, couts, csems))

    def end():
        if cn:
            pl.when(last)(lambda: comm.finish(cins, couts, csems))

    return ins, outs, scratch, begin, end


def _local_step(x, tgt, mod, n1, n2, n3, n4, w_in_p, lb_logits, hg_norm, conv_w, alog, dtb, gdn_norm,
                late_w, dist=None):
    t, d = x.shape
    nh = d // 2 // HD
    ab_blk = 8 * nh
    sh_m, sc_m, gt_m, sh_f, sc_f, gt_f = [mod[i:i + 1] for i in range(6)]

    h1, r1 = _prenorm(x, n1, sc_m, sh_m, "prenorm_mix")
    if dist is None:
        proj = _mm(h1, w_in_p, "nn", [F32], "mm_proj")
        o_hg, a_sv, hst_sv = _hgrn2_fwd(proj, lb_logits, nh, "hgrn2_fwd")
        qkv = _gdn_prep(proj, conv_w, 4 * nh, nh, "gdn_prep")
        o_gd, x_sv, gst_sv = _gdn_fwd(qkv, proj, ab_blk, alog, dtb, nh, "gdn_fwd")
        w_out, w_ff1, w_ff2 = late_w
        exch = lambda arrays: None
    else:
        proj, g_ff2 = _mm(h1, w_in_p, "nn", [F32], "mm_proj", comm=_Comm("gather", late_w[2:]))
        o_hg, a_sv, hst_sv, g_out = _hgrn2_fwd(proj, lb_logits, nh, "hgrn2_fwd",
                                               comm=_Comm("gather", late_w[:1]))
        qkv = _gdn_prep(proj, conv_w, 4 * nh, nh, "gdn_prep")
        o_gd, x_sv, gst_sv, g_ff1 = _gdn_fwd(qkv, proj, ab_blk, alog, dtb, nh, "gdn_fwd",
                                             comm=_Comm("gather", late_w[1:2]))
        w_out, w_ff1, w_ff2 = dist["assemble"](g_out, g_ff1, g_ff2)
        exch = lambda arrays: _Comm("exchange", arrays)
    om_hg = _headnorm_fwd(o_hg, proj, 3 * nh, hg_norm, "headnorm_hg")
    om_gd = _headnorm_fwd(o_gd, proj, 7 * nh, gdn_norm, "headnorm_gdn")
    om = jnp.concatenate([om_hg, om_gd], axis=1)
    y1 = _mm(om, w_out, "nn", [F32], "mm_out")
    x1, r2, h2, r3 = _postnorm_prenorm(x, y1, n2, gt_m, n3, sc_f, sh_f, "postnorm_mix_prenorm_ffn")

    def relu2(acc, extra, outs):
        rl = jnp.maximum(acc, 0.0)
        outs[0][...] = (rl * rl).astype(BF16)

    act = _mm(h2, w_ff1, "nn", [BF16], "mm_ff1", epilogue=relu2)
    y2 = _mm(act, w_ff2, "nn", [F32], "mm_ff2")
    dout, dy2, loss, dgt_f, dn4 = _final_loss_bwd(x1, y2, n4, gt_f, tgt, "final_loss_bwd")
    dw_ff2 = _mm(act, dy2, "tn", [BF16], "mm_dw_ff2")

    def drelu2(acc, extra, outs):
        outs[0][...] = (acc * (2.0 * jnp.sqrt(extra[0][...].astype(F32)))).astype(BF16)

    recv = {}
    ff2a, ff2b = dist["parts_ff2"](dw_ff2) if dist else (None, None)
    du, *recv["ff2a"] = _listed(_mm(dy2, w_ff2, "nt", [BF16], "mm_da", epilogue=drelu2, extras=(act,),
                                    comm=exch([ff2a])))
    ff1_cols = dict(by_cols=True, tn=dist["n_ff"]) if dist else {}
    dw_ff1, *recv["ff2b"] = _listed(_mm(h2, du, "tn", [BF16], "mm_dw_ff1", comm=exch([ff2b]), **ff1_cols))
    ff1a, ff1b = dist["parts_ff1"](dw_ff1) if dist else (None, None)
    dh2, *recv["ff1a"] = _listed(_mm(du, w_ff1, "nt", [F32], "mm_dh2", comm=exch([ff1a])))
    dx1, dy1, dsh_f, dsc_f, dn3, dgt_m, dn2 = _prenorm_postnorm_bwd(
        dh2, x1, r3, n3, sc_f, dout, y1, r2, n2, gt_m, "prenorm_ffn_postnorm_mix_bwd")

    dw_out = _mm(om, dy1, "tn", [BF16], "mm_dw_out")
    dom = _mm(dy1, w_out, "nt", [F32], "mm_dom")
    do_hg, dg_hg, dhgn = _headnorm_bwd(dom, 0, o_hg, proj, 3 * nh, hg_norm, "headnorm_hg_bwd")
    do_gd, dg_gd, dgdn = _headnorm_bwd(dom, 1, o_gd, proj, 7 * nh, gdn_norm, "headnorm_gdn_bwd")
    p_out = dist["parts_out"](dw_out) if dist else None
    dq_hg, df_hg, di_hg, dl0, *recv["ff1b_out"] = _hgrn2_bwd(proj, lb_logits, do_hg, a_sv, hst_sv, nh,
                                                             "hgrn2_bwd", comm=exch([ff1b, p_out]))
    dq_g, dk_g, dv_g, dab, dpar = _gdn_bwd(qkv, proj, ab_blk, alog, dtb, do_gd, x_sv, gst_sv, nh, "gdn_bwd")
    du_conv, dconv = _gdn_prep_bwd(proj, conv_w, dq_g, dk_g, dv_g, 4 * nh, nh, "gdn_prep_bwd")
    dproj = jnp.concatenate([dq_hg, df_hg, di_hg, dg_hg, du_conv, dg_gd, dab.astype(BF16)], axis=1)
    if dist is None:
        dw_in = _mm(h1, dproj, "tn", [BF16], "mm_dw_in")
        dh1 = _mm(dproj, w_in_p, "nt", [F32], "mm_dh1", tk=1664)
    else:
        q4 = d // 4
        dw_in_a = _mm(h1[:, :q4], dproj, "tn", [BF16], "mm_dw_in_a")
        dw_in_b, in_a = _mm(h1[:, q4:2 * q4], dproj, "tn", [BF16], "mm_dw_in_b",
                            comm=exch([dist["parts_in"](dw_in_a)]))
        dw_in_c, in_b = _mm(h1[:, 2 * q4:], dproj, "tn", [BF16], "mm_dw_in_c",
                            comm=exch([dist["parts_in"](dw_in_b)]))
        dh1, in_c = _mm(dproj, w_in_p, "nt", [F32], "mm_dh1", tk=1664, comm=exch([dist["parts_in"](dw_in_c)]))
        recv["in"] = [in_a, in_b, in_c]
        dw_in = None
    dx, dsh_m, dsc_m, dn1 = _prenorm_bwd(dh1, x, r1, n1, sc_m, dx1, "prenorm_mix_bwd")

    dmod = jnp.concatenate([dsh_m, dsc_m, dgt_m, dsh_f, dsc_f, dgt_f], axis=0)
    grads = dict(dmod=dmod, n1=dn1, n2=dn2, n3=dn3, n4=dn4, w_in=dw_in, lb0=dl0, hg_norm=dhgn, conv=dconv,
                 alog=dpar[0:1], dtb=dpar[1:2], gdn_norm=dgdn, w_out=dw_out, w_ff1=dw_ff1, w_ff2=dw_ff2,
                 recv=recv)
    return loss, dx, grads


def _pack(vals):
    rows = []
    for vv in vals:
        flat = vv.reshape(-1)
        flat = jnp.pad(flat, (0, (-flat.shape[0]) % (SUBLANES * LANES)))
        rows.append(flat.reshape(-1, LANES))
    return jnp.concatenate(rows, axis=0)


def _unpack(packed, shapes):
    out, r = [], 0
    for shp in shapes:
        size = 1
        for s in shp:
            size *= s
        nr = -(-size // (SUBLANES * LANES)) * SUBLANES
        out.append(packed[r:r + nr].reshape(-1)[:size].reshape(shp))
        r += nr
    return out


def _sum_parts(parts, name):
    _, r, cdim = parts.shape

    def body(p_ref, o_ref):
        acc = p_ref[0]
        for s in range(1, N_DEV):
            acc = acc + p_ref[s]
        o_ref[...] = acc

    return pl.pallas_call(
        body, name=name,
        out_shape=jax.ShapeDtypeStruct((r, cdim), F32),
        compiler_params=_cp(),
    )(parts)


def kernel(x, c, w_ada, b_ada, pre_mix_norm, post_mix_norm, pre_ffn_norm, post_ffn_norm, w_in, hg_lb_logits, hg_norm, gdn_conv_w, gdn_a_log, gdn_dt_bias, gdn_norm, w_out, w_ff1, w_ff2, loss_target, m_w_ada, m_b_ada, m_pre_mix_norm, m_post_mix_norm, m_pre_ffn_norm, m_post_ffn_norm, m_w_in, m_hg_lb_logits, m_hg_norm, m_gdn_conv_w, m_gdn_a_log, m_gdn_dt_bias, m_gdn_norm, m_w_out, m_w_ff1, m_w_ff2, v_w_ada, v_b_ada, v_pre_mix_norm, v_post_mix_norm, v_pre_ffn_norm, v_post_ffn_norm, v_w_in, v_hg_lb_logits, v_hg_norm, v_gdn_conv_w, v_gdn_a_log, v_gdn_dt_bias, v_gdn_norm, v_w_out, v_w_ff1, v_w_ff2):
    t, d = x.shape[1], x.shape[2]
    nh = d // 2 // HD
    in_cols = w_in.shape[2] * N_DEV
    main = in_cols - 2 * nh
    me = _dev_index(lax.axis_index("x"), lax.axis_index("y"), lax.axis_index("c"))

    c_all, conv_g = _all_gather([c, gdn_conv_w[0]], "gather_small")
    c_all = c_all.reshape(N_DEV, d)
    conv_full = conv_g.transpose(1, 0, 2).reshape(CONV_K, -1)
    w_in_g = _all_gather([w_in[0].astype(BF16)], "gather_w_in")[0]
    w_in_full = w_in_g.transpose(1, 0, 2).reshape(d, in_cols)
    w_in_p = jnp.concatenate([w_in_full, jnp.zeros((d, LANES - 2 * nh), BF16)], axis=1)
    late_w = [w_out[0].astype(BF16), w_ff1[0].astype(BF16), w_ff2[0].astype(BF16)]

    n_in = w_in.shape[2]
    n_ff = w_ff1.shape[2]

    def halves(p):
        r = p.shape[1] // 2
        return (p, 0, r), (p, r, r)

    dist = dict(
        assemble=lambda g_out, g_ff1, g_ff2: (g_out.reshape(d, d), g_ff1.transpose(1, 0, 2).reshape(d, -1),
                                              g_ff2.reshape(-1, d)),
        n_ff=n_ff,
        parts_ff2=lambda dw: halves(dw.reshape(N_DEV, -1, d)),
        parts_ff1=halves,
        parts_out=lambda dw: dw.reshape(N_DEV, d // N_DEV, d),
        parts_in=lambda dw: dw[:, :in_cols].reshape(dw.shape[0], N_DEV, n_in).transpose(1, 0, 2),
    )

    n_ada = w_ada.shape[2]
    b_loc = lax.dynamic_slice(b_ada, (0, me * n_ada), (1, n_ada))
    mod_part = _ada_fwd(c_all, w_ada[0], b_loc, "ada_fwd")
    mod_all = _all_gather([mod_part], "gather_mod")[0]
    mod = lax.dynamic_slice(mod_all, (0, me, 0), (N_DEV, 1, n_ada)).reshape(6, d)

    pad_lane = lambda vv: jnp.concatenate([vv, jnp.zeros((1, LANES - vv.shape[1]), F32)], axis=1)
    loss, dx, g = _local_step(
        x[0], loss_target[0], mod, pre_mix_norm, post_mix_norm, pre_ffn_norm, post_ffn_norm, w_in_p,
        hg_lb_logits, hg_norm, conv_full, pad_lane(gdn_a_log), pad_lane(gdn_dt_bias), gdn_norm,
        late_w, dist)

    rep_names = ["b_ada", "n1", "n2", "n3", "n4", "lb", "hg_norm", "alog", "dtb", "gdn_norm"]
    rep_w = [b_ada, pre_mix_norm, post_mix_norm, pre_ffn_norm, post_ffn_norm, hg_lb_logits, hg_norm,
             gdn_a_log, gdn_dt_bias, gdn_norm]
    rep_m = [m_b_ada, m_pre_mix_norm, m_post_mix_norm, m_pre_ffn_norm, m_post_ffn_norm, m_hg_lb_logits,
             m_hg_norm, m_gdn_a_log, m_gdn_dt_bias, m_gdn_norm]
    rep_v = [v_b_ada, v_pre_mix_norm, v_post_mix_norm, v_pre_ffn_norm, v_post_ffn_norm, v_hg_lb_logits,
             v_hg_norm, v_gdn_a_log, v_gdn_dt_bias, v_gdn_norm]
    rep_shapes = [a.shape for a in rep_w]
    g_lb = jnp.stack([g["lb0"], -g["lb0"]], axis=0)
    rep_g = [g["dmod"], g["n1"], g["n2"], g["n3"], g["n4"], g_lb, g["hg_norm"],
             g["alog"][:, :nh], g["dtb"][:, :nh], g["gdn_norm"]]
    small = _pack(rep_g + [g["conv"]])
    n_rep_rows = _pack(rep_g).shape[0]
    pad_rows = (-small.shape[0]) % 8
    if pad_rows:
        small = jnp.concatenate([small, jnp.zeros((pad_rows, LANES), F32)], axis=0)
    small_all = _all_gather([small], "gather_small_grads")[0]
    small_sum = _sum_parts(small_all, "sum_small_grads")
    rep_out = _adamw(_pack(rep_w), _pack(rep_m), _pack(rep_v), small_sum[:n_rep_rows], "adamw_small")
    rep_g_o, rep_d_o, rep_m_o, rep_v_o = [dict(zip(rep_names, _unpack(p, rep_shapes))) for p in rep_out]

    conv_sum = small_sum[n_rep_rows:n_rep_rows + CONV_K * conv_full.shape[1] // LANES].reshape(CONV_K, -1)
    n_conv = gdn_conv_w.shape[2]
    conv_loc = lax.dynamic_slice(conv_sum, (0, me * n_conv), (CONV_K, n_conv))
    conv_o = _adamw(gdn_conv_w, m_gdn_conv_w, v_gdn_conv_w, conv_loc, "adamw_conv")

    dmod_all = small_all[:, :6 * d // LANES, :].reshape(N_DEV, 6 * d)
    dmod_loc = lax.dynamic_slice(dmod_all, (0, me * n_ada), (N_DEV, n_ada))
    g_ada = _ada_wgrad(c_all, dmod_loc, "ada_wgrad")
    ada_o = _adamw(w_ada, m_w_ada, v_w_ada, g_ada, "adamw_ada")

    rc = g["recv"]
    r_ff2 = [rc["ff2a"][0], rc["ff2b"][0]]
    r_ff1 = [rc["ff1a"][0], rc["ff1b_out"][0]]
    r_out, r_in = rc["ff1b_out"][1], rc["in"]
    in_o = _adamw(w_in, m_w_in, v_w_in, r_in, "adamw_w_in", parts=True)
    out_o = _adamw(w_out, m_w_out, v_w_out, r_out, "adamw_w_out", parts=True)
    ff1_o = _adamw(w_ff1, m_w_ff1, v_w_ff1, r_ff1, "adamw_w_ff1", parts=True)
    ff2_o = _adamw(w_ff2, m_w_ff2, v_w_ff2, r_ff2, "adamw_w_ff2", parts=True)

    loss_tot = lax.psum(loss[0, 0], ("x", "y", "c"))

    def leaf(kind):
        return [ada_o[kind], rep_out_d[kind]["b_ada"], rep_out_d[kind]["n1"], rep_out_d[kind]["n2"],
                rep_out_d[kind]["n3"], rep_out_d[kind]["n4"], in_o[kind], rep_out_d[kind]["lb"],
                rep_out_d[kind]["hg_norm"], conv_o[kind], rep_out_d[kind]["alog"], rep_out_d[kind]["dtb"],
                rep_out_d[kind]["gdn_norm"], out_o[kind], ff1_o[kind], ff2_o[kind]]

    rep_out_d = [rep_g_o, rep_d_o, rep_m_o, rep_v_o]
    return (loss_tot, dx[None], *leaf(0), *leaf(1), *leaf(2), *leaf(3))
```

```python
import functools

import jax
import jax.numpy as jnp
from jax import lax
from jax.experimental import pallas as pl
from jax.experimental.pallas import tpu as pltpu

F32 = jnp.float32
BF16 = jnp.bfloat16
HI = lax.Precision.HIGHEST
HIGH = lax.Precision.HIGH

EPS = 1e-6
CHUNK = 64
SB = 16
NSB = CHUNK // SB
HP = 8
HD = 128
CONV_K = 4
N_DEV = 8
LANES = 128
SUBLANES = 8
VMEM_LIMIT = 56 * 1024 * 1024
MM_FULL_K = 2048

ADAM_BLOCK_ELEMS = 256 * 1024
ADAM_LR = 0.001
ADAM_B1 = 0.9
ADAM_B2 = 0.999
ADAM_EPS = 1e-08
ADAM_WD = 0.01
ADAM_STEP = 10

ANY = pl.BlockSpec(memory_space=pl.ANY)
MESH = pl.DeviceIdType.MESH


def _cp(sem=None):
    return pltpu.CompilerParams(dimension_semantics=sem, vmem_limit_bytes=VMEM_LIMIT)


def _dot(a, b, dims, precision=None):
    return lax.dot_general(a, b, (dims, ((), ())), precision=precision, preferred_element_type=F32)


def _nn(a, b, precision=None):
    return _dot(a, b, ((1,), (0,)), precision)


def _nt(a, b, precision=None):
    return _dot(a, b, ((1,), (1,)), precision)


def _tn(a, b, precision=None):
    return _dot(a, b, ((0,), (0,)), precision)


def _bf(x):
    return x.astype(BF16)


def _sigmoid(x):
    return 1.0 / (1.0 + jnp.exp(-x))


def _interleave(gens):
    results = [None] * len(gens)
    live = list(range(len(gens)))
    while live:
        for i in list(live):
            try:
                next(gens[i])
            except StopIteration as stop:
                results[i] = stop.value
                live.remove(i)
    return results


def _listed(res):
    return list(res) if isinstance(res, (list, tuple)) else [res]


def _pick(n, pref):
    if n <= pref:
        return n
    t = pref
    while n % t:
        t -= LANES
    assert t > 0, (n, pref)
    return t


def _mm(a, b, mode, out_dtypes, name, epilogue=None, extras=(), tm=1024, tn=2048, tk=1024, comm=None,
        by_cols=False):
    if mode == "nn":
        (m, kd), (_, n) = a.shape, b.shape
    elif mode == "nt":
        (m, kd), (n, _) = a.shape, b.shape
    else:
        (kd, m), (_, n) = a.shape, b.shape
    if kd <= MM_FULL_K:
        tk = kd
    tm, tn, tk = _pick(m, tm), _pick(n, tn), _pick(kd, tk)
    nk = kd // tk
    if mode == "nn":
        a_spec = pl.BlockSpec((tm, tk), lambda i, j, k: (i, k))
        b_spec = pl.BlockSpec((tk, tn), lambda i, j, k: (k, j))
        dims = ((1,), (0,))
    elif mode == "nt":
        a_spec = pl.BlockSpec((tm, tk), lambda i, j, k: (i, k))
        b_spec = pl.BlockSpec((tn, tk), lambda i, j, k: (j, k))
        dims = ((1,), (1,))
    else:
        a_spec = pl.BlockSpec((tk, tm), lambda i, j, k: (k, i))
        b_spec = pl.BlockSpec((tk, tn), lambda i, j, k: (k, j))
        dims = ((0,), (0,))
    o_spec = pl.BlockSpec((tm, tn), lambda i, j, k: (i, j))
    if by_cols:
        assert epilogue is None and not extras
        res_spec = pl.BlockSpec((None, tm, tn), lambda i, j, k: (j, i, 0))
        res_shape = (n // tn, m, tn)
    else:
        res_spec, res_shape = o_spec, (m, n)
    n_extra, n_out = len(extras), len(out_dtypes)

    gm, gn = m // tm, n // tn
    cn = comm.n if comm is not None else 0

    def body(*refs):
        i, j, k = pl.program_id(0), pl.program_id(1), pl.program_id(2)
        at0 = (j == 0) & (k == 0)
        ins, out_refs, scratch, comm_begin, comm_end = _comm_hooks(
            comm, refs, 2 + n_extra, n_out, (i == 0) & at0, (i == gm - 1) & at0,
            (i == gm - 1) & (j == gn - 1) & (k == nk - 1))
        a_ref, b_ref, extra_refs = ins[0], ins[1], ins[2:]
        comm_begin()
        part = _dot(a_ref[...], b_ref[...], dims)
        if nk == 1:
            if epilogue is None:
                out_refs[0][...] = part.astype(out_dtypes[0])
            else:
                epilogue(part, extra_refs, out_refs)
        else:
            acc, = scratch

            @pl.when(k == 0)
            def _():
                acc[...] = jnp.zeros_like(acc)

            acc[...] += part

            @pl.when(k == nk - 1)
            def _():
                if epilogue is None:
                    out_refs[0][...] = acc[...].astype(out_dtypes[0])
                else:
                    epilogue(acc[...], extra_refs, out_refs)

        comm_end()

    acc_scratch = [] if nk == 1 else [pltpu.VMEM((tm, tn), F32)]
    sem = ("arbitrary",) * 3 if cn else ("parallel", "parallel", "arbitrary")
    outs = pl.pallas_call(
        body, name=name,
        grid=(gm, gn, nk),
        in_specs=[a_spec, b_spec] + [o_spec] * n_extra + [ANY] * cn,
        out_specs=[res_spec] * n_out + [ANY] * cn,
        out_shape=[jax.ShapeDtypeStruct(res_shape, dt) for dt in out_dtypes] + (comm.out_shapes() if cn else []),
        scratch_shapes=acc_scratch + (comm.scratch() if cn else []),
        compiler_params=_cp(sem),
    )(a, b, *extras, *(comm.arrays if cn else []))
    return outs[0] if n_out + cn == 1 else outs


def _row_spec(tb, d):
    return pl.BlockSpec((tb, d), lambda i: (i, 0))


def _vec_spec(d):
    return pl.BlockSpec((1, d), lambda i: (0, 0))


def _prenorm(x, w, sc, sh, name):
    t, d = x.shape
    tb = _pick(t, 256)

    def body(x_ref, w_ref, sc_ref, sh_ref, h_ref, r_ref):
        xv = x_ref[...]
        r = lax.rsqrt(jnp.mean(xv * xv, axis=-1, keepdims=True) + EPS)
        h_ref[...] = ((xv * r * w_ref[...]) * (1.0 + sc_ref[...]) + sh_ref[...]).astype(BF16)
        r_ref[...] = r

    return pl.pallas_call(
        body, name=name, grid=(t // tb,),
        in_specs=[_row_spec(tb, d), _vec_spec(d), _vec_spec(d), _vec_spec(d)],
        out_specs=[_row_spec(tb, d), _row_spec(tb, 1)],
        out_shape=[jax.ShapeDtypeStruct((t, d), BF16), jax.ShapeDtypeStruct((t, 1), F32)],
        compiler_params=_cp(("parallel",)),
    )(x, w, sc, sh)


def _final_loss_bwd(x, y, w, gt, tgt, name):
    t, d = x.shape
    tb = _pick(t, 256)

    def body(x_ref, y_ref, w_ref, gt_ref, tgt_ref, dout_ref, dy_ref, loss_ref, dgt_ref, dw_ref):
        @pl.when(pl.program_id(0) == 0)
        def _():
            loss_ref[...] = jnp.zeros_like(loss_ref)
            dgt_ref[...] = jnp.zeros_like(dgt_ref)
            dw_ref[...] = jnp.zeros_like(dw_ref)

        yv, wv, gtv = y_ref[...], w_ref[...], gt_ref[...]
        r = lax.rsqrt(jnp.mean(yv * yv, axis=-1, keepdims=True) + EPS)
        z = yv * r
        nz = z * wv
        diff = (x_ref[...] + gtv * nz) - tgt_ref[...]
        loss_ref[...] += 0.5 * jnp.sum(jnp.mean(diff * diff, axis=-1, keepdims=True), axis=0, keepdims=True)
        dxv = diff * (1.0 / d)
        dout_ref[...] = dxv
        dgt_ref[...] += jnp.sum(dxv * nz, axis=0, keepdims=True)
        dn = dxv * gtv
        dw_ref[...] += jnp.sum(dn * z, axis=0, keepdims=True)
        dz = dn * wv
        dy_ref[...] = (r * (dz - z * jnp.mean(dz * z, axis=-1, keepdims=True))).astype(BF16)

    return pl.pallas_call(
        body, name=name, grid=(t // tb,),
        in_specs=[_row_spec(tb, d), _row_spec(tb, d), _vec_spec(d), _vec_spec(d), _row_spec(tb, d)],
        out_specs=[_row_spec(tb, d), _row_spec(tb, d), pl.BlockSpec((1, 1), lambda i: (0, 0)),
                   _vec_spec(d), _vec_spec(d)],
        out_shape=[jax.ShapeDtypeStruct((t, d), F32), jax.ShapeDtypeStruct((t, d), BF16),
                   jax.ShapeDtypeStruct((1, 1), F32), jax.ShapeDtypeStruct((1, d), F32),
                   jax.ShapeDtypeStruct((1, d), F32)],
        compiler_params=_cp(("arbitrary",)),
    )(x, y, w, gt, tgt)


def _postnorm_prenorm(x, y, w_post, gt, w_pre, sc, sh, name):
    t, d = x.shape
    tb = _pick(t, 256)

    def body(x_ref, y_ref, wp_ref, gt_ref, wn_ref, sc_ref, sh_ref, x1_ref, r_ref, h_ref, r1_ref):
        yv = y_ref[...]
        r = lax.rsqrt(jnp.mean(yv * yv, axis=-1, keepdims=True) + EPS)
        x1 = x_ref[...] + gt_ref[...] * (yv * r * wp_ref[...])
        r1 = lax.rsqrt(jnp.mean(x1 * x1, axis=-1, keepdims=True) + EPS)
        x1_ref[...] = x1
        r_ref[...] = r
        h_ref[...] = ((x1 * r1 * wn_ref[...]) * (1.0 + sc_ref[...]) + sh_ref[...]).astype(BF16)
        r1_ref[...] = r1

    return pl.pallas_call(
        body, name=name, grid=(t // tb,),
        in_specs=[_row_spec(tb, d), _row_spec(tb, d)] + [_vec_spec(d)] * 5,
        out_specs=[_row_spec(tb, d), _row_spec(tb, 1), _row_spec(tb, d), _row_spec(tb, 1)],
        out_shape=[jax.ShapeDtypeStruct((t, d), F32), jax.ShapeDtypeStruct((t, 1), F32),
                   jax.ShapeDtypeStruct((t, d), BF16), jax.ShapeDtypeStruct((t, 1), F32)],
        compiler_params=_cp(("parallel",)),
    )(x, y, w_post, gt, w_pre, sc, sh)


def _prenorm_postnorm_bwd(dh, x, r_pre, w_pre, sc, dres, y, r_post, w_post, gt, name):
    t, d = x.shape
    tb = _pick(t, 256)

    def body(dh_ref, x_ref, rp_ref, wp_ref, sc_ref, dres_ref, y_ref, rq_ref, wq_ref, gt_ref,
             dx_ref, dy_ref, dsh_ref, dsc_ref, dwp_ref, dgt_ref, dwq_ref):
        @pl.when(pl.program_id(0) == 0)
        def _():
            for ref in (dsh_ref, dsc_ref, dwp_ref, dgt_ref, dwq_ref):
                ref[...] = jnp.zeros_like(ref)

        dhv, rv, wv = dh_ref[...], rp_ref[...], wp_ref[...]
        z = x_ref[...] * rv
        dsh_ref[...] += jnp.sum(dhv, axis=0, keepdims=True)
        dsc_ref[...] += jnp.sum(dhv * (z * wv), axis=0, keepdims=True)
        dzw = dhv * (1.0 + sc_ref[...])
        dwp_ref[...] += jnp.sum(dzw * z, axis=0, keepdims=True)
        dz = dzw * wv
        dxv = dres_ref[...] + rv * (dz - z * jnp.mean(dz * z, axis=-1, keepdims=True))
        dx_ref[...] = dxv

        rq, wq = rq_ref[...], wq_ref[...]
        zq = y_ref[...] * rq
        dgt_ref[...] += jnp.sum(dxv * (zq * wq), axis=0, keepdims=True)
        dn = dxv * gt_ref[...]
        dwq_ref[...] += jnp.sum(dn * zq, axis=0, keepdims=True)
        dzq = dn * wq
        dy_ref[...] = (rq * (dzq - zq * jnp.mean(dzq * zq, axis=-1, keepdims=True))).astype(BF16)

    rs, r1, vs = _row_spec(tb, d), _row_spec(tb, 1), _vec_spec(d)
    return pl.pallas_call(
        body, name=name, grid=(t // tb,),
        in_specs=[rs, rs, r1, vs, vs, rs, rs, r1, vs, vs],
        out_specs=[rs, rs] + [vs] * 5,
        out_shape=[jax.ShapeDtypeStruct((t, d), F32), jax.ShapeDtypeStruct((t, d), BF16)]
        + [jax.ShapeDtypeStruct((1, d), F32)] * 5,
        compiler_params=_cp(("arbitrary",)),
    )(dh, x, r_pre, w_pre, sc, dres, y, r_post, w_post, gt)


def _prenorm_bwd(dh, x, r, w, sc, dres, name):
    t, d = x.shape
    tb = _pick(t, 256)

    def body(dh_ref, x_ref, r_ref, w_ref, sc_ref, dres_ref, dx_ref, dsh_ref, dsc_ref, dw_ref):
        @pl.when(pl.program_id(0) == 0)
        def _():
            dsh_ref[...] = jnp.zeros_like(dsh_ref)
            dsc_ref[...] = jnp.zeros_like(dsc_ref)
            dw_ref[...] = jnp.zeros_like(dw_ref)

        dhv, rv, wv = dh_ref[...], r_ref[...], w_ref[...]
        z = x_ref[...] * rv
        dsh_ref[...] += jnp.sum(dhv, axis=0, keepdims=True)
        dsc_ref[...] += jnp.sum(dhv * (z * wv), axis=0, keepdims=True)
        dzw = dhv * (1.0 + sc_ref[...])
        dw_ref[...] += jnp.sum(dzw * z, axis=0, keepdims=True)
        dz = dzw * wv
        dx_ref[...] = dres_ref[...] + rv * (dz - z * jnp.mean(dz * z, axis=-1, keepdims=True))

    return pl.pallas_call(
        body, name=name, grid=(t // tb,),
        in_specs=[_row_spec(tb, d), _row_spec(tb, d), _row_spec(tb, 1), _vec_spec(d), _vec_spec(d),
                  _row_spec(tb, d)],
        out_specs=[_row_spec(tb, d), _vec_spec(d), _vec_spec(d), _vec_spec(d)],
        out_shape=[jax.ShapeDtypeStruct((t, d), F32)] + [jax.ShapeDtypeStruct((1, d), F32)] * 3,
        compiler_params=_cp(("arbitrary",)),
    )(dh, x, r, w, sc, dres)


def _headnorm_fwd(o, proj, g_blk, nw, name):
    t, wd = o.shape
    nh = wd // HD
    tb = _pick(t, 512)
    gb = g_blk * HD // wd

    def body(o_ref, g_ref, nw_ref, out_ref):
        o3 = o_ref[...].reshape(tb, nh, HD)
        g3 = g_ref[...].reshape(tb, nh, HD)
        rh = lax.rsqrt(jnp.mean(o3 * o3, axis=-1, keepdims=True) + EPS)
        res = (o3 * rh * nw_ref[...].reshape(1, 1, HD)) * (g3 * _sigmoid(g3))
        out_ref[...] = res.reshape(tb, wd).astype(BF16)

    return pl.pallas_call(
        body, name=name, grid=(t // tb,),
        in_specs=[_row_spec(tb, wd), pl.BlockSpec((tb, wd), lambda i: (i, gb)), _vec_spec(HD)],
        out_specs=_row_spec(tb, wd),
        out_shape=jax.ShapeDtypeStruct((t, wd), BF16),
        compiler_params=_cp(("parallel",)),
    )(o, proj, nw)


def _headnorm_bwd(dom, col_blk, o, proj, g_blk, nw, name):
    t, wd = o.shape
    nh = wd // HD
    tb = _pick(t, 512)
    gb = g_blk * HD // wd

    def body(do_ref, o_ref, g_ref, nw_ref, dout_ref, dg_ref, dnw_ref):
        @pl.when(pl.program_id(0) == 0)
        def _():
            dnw_ref[...] = jnp.zeros_like(dnw_ref)

        dn = do_ref[...].reshape(tb, nh, HD)
        o3 = o_ref[...].reshape(tb, nh, HD)
        g3 = g_ref[...].reshape(tb, nh, HD)
        nw3 = nw_ref[...].reshape(1, 1, HD)
        rh = lax.rsqrt(jnp.mean(o3 * o3, axis=-1, keepdims=True) + EPS)
        z = o3 * rh
        sg = _sigmoid(g3)
        sl = g3 * sg
        dnw_ref[...] += jnp.sum(jnp.sum(dn * sl * z, axis=1), axis=0, keepdims=True)
        dg_ref[...] = (dn * (z * nw3) * (sg * (1.0 + g3 * (1.0 - sg)))).reshape(tb, wd).astype(BF16)
        dz = dn * sl * nw3
        dout_ref[...] = (rh * (dz - z * jnp.mean(dz * z, axis=-1, keepdims=True))).reshape(tb, wd)

    return pl.pallas_call(
        body, name=name, grid=(t // tb,),
        in_specs=[pl.BlockSpec((tb, wd), lambda i: (i, col_blk)), _row_spec(tb, wd),
                  pl.BlockSpec((tb, wd), lambda i: (i, gb)), _vec_spec(HD)],
        out_specs=[_row_spec(tb, wd), _row_spec(tb, wd), _vec_spec(HD)],
        out_shape=[jax.ShapeDtypeStruct((t, wd), F32), jax.ShapeDtypeStruct((t, wd), BF16),
                   jax.ShapeDtypeStruct((1, HD), F32)],
        compiler_params=_cp(("arbitrary",)),
    )(dom, o, proj, nw)


def _tri(n, kind):
    r = lax.broadcasted_iota(jnp.int32, (n, n), 0)
    c = lax.broadcasted_iota(jnp.int32, (n, n), 1)
    if kind == "lower":
        return r >= c
    if kind == "strict":
        return r > c
    return r <= c


def _hg_gate(fl, l0, l1):
    mx = jnp.maximum(l0, l1)
    e0, e1 = jnp.exp(l0 - mx), jnp.exp(l1 - mx)
    lb = e0 / (e0 + e1)
    sg = _sigmoid(fl)
    f = lb + (1.0 - lb) * sg
    return lb, sg, f


def _hgrn2_fwd(proj, lb_logits, nh, name, comm=None):
    t = proj.shape[0]
    nc = t // CHUNK
    C = CHUNK
    lg = lb_logits.reshape(2, nh, 1, HD)

    hp = min(HP, nh)
    ng = nh // hp

    def one_head(hh, st, q_ref, f_ref, i_ref, lg_ref, p_sc, r_sc):
        sl = slice(hh * HD, (hh + 1) * HD)
        q, v = q_ref[:, sl], i_ref[:, sl]
        _, _, f = _hg_gate(f_ref[:, sl], lg_ref[0, hh], lg_ref[1, hh])
        k = 1.0 - f
        low = _tri(C, "lower")
        b = _nn(low.astype(F32), jnp.log(f), HI)
        yield
        lane_c = lax.broadcasted_iota(jnp.int32, (SB, C), 1)
        lane_h = lax.broadcasted_iota(jnp.int32, (SB, HD), 1)
        row_h = lax.broadcasted_iota(jnp.int32, (SB, HD), 0)
        ones = jnp.ones((HD, HD), F32)

        for i in range(NSB):
            qi, ki, bi = q[SB * i:SB * (i + 1)], k[SB * i:SB * (i + 1)], b[SB * i:SB * (i + 1)]
            for s in range(SB):
                e = jnp.exp(jnp.minimum(bi - bi[s:s + 1], 0.0))
                p = jnp.where(row_h >= s, qi * ki[s:s + 1] * e, 0.0)
                p_sc[hh, pl.ds((i * SB + s) * SB, SB), :] = p
            yield
        r_sc[hh] = _nn(p_sc[hh], ones, HIGH)
        yield
        a_rows = []
        for i in range(NSB):
            acc = jnp.zeros((SB, HD), F32)
            for s in range(SB):
                acc = jnp.where(lane_h == SB * i + s, r_sc[hh, pl.ds((i * SB + s) * SB, SB), :], acc)
            acc = acc[:, :C]
            if i > 0:
                r = b[SB * i - 1:SB * i]
                bi = b[SB * i:SB * (i + 1)]
                qf = q[SB * i:SB * (i + 1)] * jnp.exp(bi - r)
                kf = k * jnp.exp(jnp.minimum(r - b, 0.0))
                acc = acc + jnp.where(lane_c < SB * i, _nt(qf, kf, HIGH), 0.0)
            a_rows.append(acc)
            yield
        a = jnp.concatenate(a_rows, axis=0)
        bl = b[C - 1:C, :]
        o = _nn(_bf(a), _bf(v)) + _nt(_bf(q * jnp.exp(b)), _bf(st))
        yield
        new_st = st * jnp.exp(bl) + _tn(_bf(v), _bf(k * jnp.exp(bl - b)))
        return o, a, new_st

    def body(*refs):
        c, hg = pl.program_id(0), pl.program_id(1)
        step = c * ng + hg
        ins, outs, scratch, comm_begin, comm_end = _comm_hooks(
            comm, refs, 4, 3, step == 0, step == (3 * nc * ng) // 4, step == nc * ng - 1)
        o_ref, a_ref, st_ref = outs
        s_sc, p_sc, r_sc = scratch
        comm_begin()

        @pl.when(c == 0)
        def _():
            for hh in range(hp):
                s_sc[hg * hp + hh] = jnp.zeros((HD, HD), F32)

        sts = [s_sc[hg * hp + hh] for hh in range(hp)]
        res = _interleave([one_head(hh, sts[hh], *ins, p_sc, r_sc) for hh in range(hp)])
        for hh in range(hp):
            o_ref[:, hh * HD:(hh + 1) * HD] = res[hh][0]
            a_ref[0, hh] = res[hh][1]
            st_ref[0, hh] = sts[hh]
            s_sc[hg * hp + hh] = res[hh][2]
        comm_end()

    blk = lambda off: pl.BlockSpec((C, hp * HD), lambda c, g: (c, off // hp + g))
    cn = comm.n if comm is not None else 0
    return pl.pallas_call(
        body, name=name, grid=(nc, ng),
        in_specs=[blk(0), blk(nh), blk(2 * nh),
                  pl.BlockSpec((2, hp, 1, HD), lambda c, g: (0, g, 0, 0))] + [ANY] * cn,
        out_specs=[blk(0),
                   pl.BlockSpec((1, hp, C, C), lambda c, g: (c, g, 0, 0)),
                   pl.BlockSpec((1, hp, HD, HD), lambda c, g: (c, g, 0, 0))] + [ANY] * cn,
        out_shape=[jax.ShapeDtypeStruct((t, nh * HD), F32),
                   jax.ShapeDtypeStruct((nc, nh, C, C), F32),
                   jax.ShapeDtypeStruct((nc, nh, HD, HD), F32)] + (comm.out_shapes() if cn else []),
        scratch_shapes=[pltpu.VMEM((nh, HD, HD), F32), pltpu.VMEM((hp, C * SB, HD), F32),
                        pltpu.VMEM((hp, C * SB, HD), F32)] + (comm.scratch() if cn else []),
        compiler_params=_cp(("arbitrary", "arbitrary")),
    )(proj, proj, proj, lg, *(comm.arrays if cn else []))


def _hgrn2_bwd(proj, lb_logits, do, a_sv, st_sv, nh, name, comm=None):
    t = proj.shape[0]
    nc = t // CHUNK
    C = CHUNK
    lg = lb_logits.reshape(2, nh, 1, HD)
    hp = min(HP, nh)
    ng = nh // hp

    def one_head(hh, dst, q_ref, f_ref, i_ref, lg_ref, do_ref, a_ref, st_ref, p_sc, r_sc):
        sl = slice(hh * HD, (hh + 1) * HD)
        q, v, do_ = q_ref[:, sl], i_ref[:, sl], do_ref[:, sl]
        lb, sg, f = _hg_gate(f_ref[:, sl], lg_ref[0, hh], lg_ref[1, hh])
        k = 1.0 - f
        low = _tri(C, "lower")
        b = _nn(low.astype(F32), jnp.log(f), HI)
        yield
        bl = b[C - 1:C, :]
        eb, ekb = jnp.exp(b), jnp.exp(bl - b)
        qb, kb = q * eb, k * ekb
        a, st = a_ref[0, hh], st_ref[0, hh]

        da = jnp.where(low, _nt(_bf(do_), _bf(v)), 0.0)
        yield
        dv = _tn(_bf(a), _bf(do_)) + _nt(_bf(kb), _bf(dst))
        yield
        dqb = _nn(_bf(do_), _bf(st))
        dkb = _nn(_bf(v), _bf(dst))
        yield

        row = lax.broadcasted_iota(jnp.int32, (C, HD), 0)
        lane_c = lax.broadcasted_iota(jnp.int32, (SB, C), 1)
        row_h = lax.broadcasted_iota(jnp.int32, (SB, HD), 0)
        ones = jnp.ones((HD, HD), F32)
        sel = (lax.broadcasted_iota(jnp.int32, (C, C * SB), 0)
               == jnp.right_shift(lax.broadcasted_iota(jnp.int32, (C, C * SB), 1), SB.bit_length() - 1)).astype(F32)

        for i in range(NSB):
            doi, vi = do_[SB * i:SB * (i + 1)], v[SB * i:SB * (i + 1)]
            for s in range(SB):
                p_sc[hh, pl.ds((i * SB + s) * SB, SB), :] = doi * vi[s:s + 1]
            yield
        r_sc[hh] = _nn(p_sc[hh], ones, HIGH)
        yield
        dq_rows = []
        dk_off = jnp.zeros((C, HD), F32)
        for i in range(NSB):
            qi, ki, bi = q[SB * i:SB * (i + 1)], k[SB * i:SB * (i + 1)], b[SB * i:SB * (i + 1)]
            acc = jnp.zeros((SB, HD), F32)
            for s in range(SB):
                e = jnp.exp(jnp.minimum(bi - bi[s:s + 1], 0.0))
                g = jnp.where(row_h >= s, r_sc[hh, pl.ds((i * SB + s) * SB, SB), :] * e, 0.0)
                acc = acc + g * ki[s:s + 1]
                p_sc[hh, pl.ds((i * SB + s) * SB, SB), :] = g * qi
            yield
            if i > 0:
                r = b[SB * i - 1:SB * i]
                fq = jnp.exp(bi - r)
                fk = jnp.exp(jnp.minimum(r - b, 0.0))
                dai = jnp.where(lane_c < SB * i, da[SB * i:SB * (i + 1)], 0.0)
                acc = acc + _nn(dai, k * fk, HIGH) * fq
                dk_off = dk_off + _tn(dai, qi * fq, HIGH) * fk
                yield
            dq_rows.append(acc)
        dqi = jnp.concatenate(dq_rows, axis=0)
        dq = dqi + dqb * eb
        dk_inter = dkb * ekb
        dk = _nn(sel, p_sc[hh], HIGH) + dk_off + dk_inter
        yield
        db = q * dq - k * dk
        extra = (jnp.sum(k * dk_inter, axis=0, keepdims=True)
                 + jnp.exp(bl) * jnp.sum(dst * st, axis=0, keepdims=True))
        db = db + jnp.where(row == C - 1, extra, 0.0)
        dlf = _nn(_tri(C, "upper").astype(F32), db, HI)
        yield
        df = dlf / f - dk
        dfl = (df * (1.0 - lb) * sg * (1.0 - sg)).astype(BF16)
        dl = jnp.sum(df * (1.0 - sg), axis=0, keepdims=True) * (lb * (1.0 - lb))
        new_dst = dst * jnp.exp(bl) + _tn(_bf(do_), _bf(qb))
        return dq.astype(BF16), dfl, dv.astype(BF16), dl, new_dst

    def body(*refs):
        c, hg = pl.program_id(0), pl.program_id(1)
        step = c * ng + hg
        ins, outs, scratch, comm_begin, comm_end = _comm_hooks(
            comm, refs, 7, 4, step == 0, step == (3 * nc * ng) // 4, step == nc * ng - 1)
        dq_ref, df_ref, di_ref, dl_ref = outs
        ds_sc, p_sc, r_sc = scratch
        comm_begin()

        @pl.when(c == 0)
        def _():
            for hh in range(hp):
                ds_sc[hg * hp + hh] = jnp.zeros((HD, HD), F32)

        @pl.when(step == 0)
        def _():
            dl_ref[...] = jnp.zeros_like(dl_ref)

        dsts = [ds_sc[hg * hp + hh] for hh in range(hp)]
        res = _interleave([one_head(hh, dsts[hh], *ins, p_sc, r_sc) for hh in range(hp)])
        for hh in range(hp):
            sl = slice(hh * HD, (hh + 1) * HD)
            dq_ref[:, sl], df_ref[:, sl], di_ref[:, sl] = res[hh][0], res[hh][1], res[hh][2]
            dl_ref[pl.ds(hg * hp + hh, 1), :] += res[hh][3]
            ds_sc[hg * hp + hh] = res[hh][4]
        comm_end()

    rblk = lambda off: pl.BlockSpec((C, hp * HD), lambda c, g: (nc - 1 - c, off // hp + g))
    oblk = pl.BlockSpec((C, hp * HD), lambda c, g: (nc - 1 - c, g))
    cn = comm.n if comm is not None else 0
    return pl.pallas_call(
        body, name=name, grid=(nc, ng),
        in_specs=[rblk(0), rblk(nh), rblk(2 * nh),
                  pl.BlockSpec((2, hp, 1, HD), lambda c, g: (0, g, 0, 0)),
                  oblk,
                  pl.BlockSpec((1, hp, C, C), lambda c, g: (nc - 1 - c, g, 0, 0)),
                  pl.BlockSpec((1, hp, HD, HD), lambda c, g: (nc - 1 - c, g, 0, 0))] + [ANY] * cn,
        out_specs=[oblk, oblk, oblk, pl.BlockSpec((nh, HD), lambda c, g: (0, 0))] + [ANY] * cn,
        out_shape=[jax.ShapeDtypeStruct((t, nh * HD), BF16)] * 3 + [jax.ShapeDtypeStruct((nh, HD), F32)]
        + (comm.out_shapes() if cn else []),
        scratch_shapes=[pltpu.VMEM((nh, HD, HD), F32), pltpu.VMEM((hp, C * SB, HD), F32),
                        pltpu.VMEM((hp, C * SB, HD), F32)] + (comm.scratch() if cn else []),
        compiler_params=_cp(("arbitrary", "arbitrary")),
    )(proj, proj, proj, lg, do, a_sv, st_sv, *(comm.arrays if cn else []))


def _shift_rows(u, d, row):
    t = u.shape[0]
    if d == 0:
        return u
    rolled = pltpu.roll(u, d % t, 0)
    if d > 0:
        return jnp.where(row >= d, rolled, 0.0)
    return jnp.where(row < t + d, rolled, 0.0)


def _gdn_prep(proj, conv_w, blk0, nh, name):
    t = proj.shape[0]
    scale = HD ** -0.5

    def body(u_ref, w_ref, o_ref):
        j = pl.program_id(0)
        u, w = u_ref[...], w_ref[...]
        row = lax.broadcasted_iota(jnp.int32, (t, HD), 0)
        y = w[CONV_K - 1:CONV_K, :] * u
        for d in range(1, CONV_K):
            y = y + w[CONV_K - 1 - d:CONV_K - d, :] * _shift_rows(u, d, row)
        a = y * _sigmoid(y)
        n = a * lax.rsqrt(jnp.sum(a * a, axis=-1, keepdims=True) + EPS)
        n = n * jnp.where(j < nh, scale, 1.0)
        o_ref[...] = jnp.where(j < 2 * nh, n, a)

    return pl.pallas_call(
        body, name=name, grid=(3 * nh,),
        in_specs=[pl.BlockSpec((t, HD), lambda j: (0, blk0 + j)), pl.BlockSpec((CONV_K, HD), lambda j: (0, j))],
        out_specs=pl.BlockSpec((t, HD), lambda j: (0, j)),
        out_shape=jax.ShapeDtypeStruct((t, 3 * nh * HD), F32),
        compiler_params=_cp(("parallel",)),
    )(proj, conv_w)


def _gdn_prep_bwd(proj, conv_w, dq, dk, dv, blk0, nh, name):
    t = proj.shape[0]
    scale = HD ** -0.5

    def body(u_ref, w_ref, dq_ref, dk_ref, dv_ref, du_ref, dw_ref):
        j = pl.program_id(0)
        u, w = u_ref[...], w_ref[...]
        dout = jnp.where(j < nh, dq_ref[...], jnp.where(j < 2 * nh, dk_ref[...], dv_ref[...]))
        row = lax.broadcasted_iota(jnp.int32, (t, HD), 0)
        us = [_shift_rows(u, d, row) for d in range(CONV_K)]
        y = w[CONV_K - 1:CONV_K, :] * us[0]
        for d in range(1, CONV_K):
            y = y + w[CONV_K - 1 - d:CONV_K - d, :] * us[d]
        sg = _sigmoid(y)
        a = y * sg
        rs = lax.rsqrt(jnp.sum(a * a, axis=-1, keepdims=True) + EPS)
        n = a * rs
        dn = dout * jnp.where(j < nh, scale, 1.0)
        da_n = rs * (dn - n * jnp.sum(dn * n, axis=-1, keepdims=True))
        da = jnp.where(j < 2 * nh, da_n, dout)
        dy = da * (sg * (1.0 + y * (1.0 - sg)))
        du = w[CONV_K - 1:CONV_K, :] * dy
        for d in range(1, CONV_K):
            du = du + w[CONV_K - 1 - d:CONV_K - d, :] * _shift_rows(dy, -d, row)
        du_ref[...] = du.astype(BF16)
        for d in range(CONV_K):
            dw_ref[CONV_K - 1 - d:CONV_K - d, :] = jnp.sum(dy * us[d], axis=0, keepdims=True)

    return pl.pallas_call(
        body, name=name, grid=(3 * nh,),
        in_specs=[pl.BlockSpec((t, HD), lambda j: (0, blk0 + j)), pl.BlockSpec((CONV_K, HD), lambda j: (0, j))]
        + [pl.BlockSpec((t, HD), functools.partial(lambda p, j: (0, jnp.clip(j - p * nh, 0, nh - 1)), p))
           for p in range(3)],
        out_specs=[pl.BlockSpec((t, HD), lambda j: (0, j)), pl.BlockSpec((CONV_K, HD), lambda j: (0, j))],
        out_shape=[jax.ShapeDtypeStruct((t, 3 * nh * HD), BF16), jax.ShapeDtypeStruct((CONV_K, 3 * nh * HD), F32)],
        compiler_params=_cp(("arbitrary",)),
    )(proj, conv_w, dq, dk, dv)


def _gdn_gates(ab, alog, dtb, h, nh):
    lane = lax.broadcasted_iota(jnp.int32, ab.shape, 1)
    x = ab + dtb
    sp = jnp.maximum(x, 0.0) + jnp.log(1.0 + jnp.exp(-jnp.abs(x)))
    ea = jnp.exp(alog)
    la_all = -ea * sp
    beta_all = _sigmoid(ab)
    pick = lambda val, ln: jnp.sum(jnp.where(lane == ln, val, 0.0), axis=1, keepdims=True)
    la = pick(la_all, h)
    beta = pick(beta_all, nh + h)
    dla_da = pick(-ea * _sigmoid(x), h)
    return la, beta, dla_da


def _unit_lower_inverses(ms, C):
    nb = C // SB
    sh = SB.bit_length() - 1
    rowb = jnp.right_shift(lax.broadcasted_iota(jnp.int32, (C, C), 0), sh)
    colb = jnp.right_shift(lax.broadcasted_iota(jnp.int32, (C, C), 1), sh)
    eye = (lax.broadcasted_iota(jnp.int32, (SB, SB), 0) == lax.broadcasted_iota(jnp.int32, (SB, SB), 1)).astype(F32)
    spread = (jnp.bitwise_and(lax.broadcasted_iota(jnp.int32, (SB, C), 1), SB - 1)
              == lax.broadcasted_iota(jnp.int32, (SB, C), 0)).astype(F32)
    blocks = [[m[SB * i:SB * (i + 1), SB * i:SB * (i + 1)] for i in range(nb)] for m in ms]
    xs = [[eye] * nb for _ in ms]
    for s in range(SB - 1):
        xs = [[x - b[:, s:s + 1] * x[s:s + 1, :] for x, b in zip(xh, bh)] for xh, bh in zip(xs, blocks)]
    ts = [jnp.where(rowb == colb, _nn(jnp.concatenate(xh, axis=0), spread, HIGH), 0.0) for xh in xs]
    lvl = 1
    while (1 << lvl) <= nb:
        off = ((jnp.right_shift(rowb, lvl) == jnp.right_shift(colb, lvl))
               & (jnp.right_shift(rowb, lvl - 1) != jnp.right_shift(colb, lvl - 1)))
        ts = [t - _nn(t, _nn(jnp.where(off, m, 0.0), t, HIGH), HIGH) for t, m in zip(ts, ms)]
        lvl += 1
    return ts


def _gdn_chunks(qs, ks, vs, las, betas, C):
    low, strict = _tri(C, "lower"), _tri(C, "strict")
    eye = (lax.broadcasted_iota(jnp.int32, (C, C), 0) == lax.broadcasted_iota(jnp.int32, (C, C), 1)).astype(F32)
    g_bs = [_nn(low.astype(F32), jnp.broadcast_to(la, (C, HD)), HI) for la in las]
    ps = [_nt(k, k, HIGH) for k in ks]
    qks = [_nt(_bf(q), _bf(k)) for q, k in zip(qs, ks)]
    chs = []
    for g_b, p, qk_raw, beta in zip(g_bs, ps, qks, betas):
        g_c = g_b[:, :C]
        gamma = jnp.where(low, jnp.exp(jnp.minimum(g_c - g_c.T, 0.0)), 0.0)
        gl = g_b[C - 1:C, :]
        chs.append(dict(gamma=gamma, eg=jnp.exp(g_b), gl=gl, ekt=jnp.exp(gl - g_b), p=p,
                        m=jnp.where(strict, beta * p * gamma, 0.0), qk_raw=qk_raw))
    xs = _unit_lower_inverses([ch["m"] for ch in chs], C)
    r_ws = [k * (beta * ch["eg"]) for ch, k, beta in zip(chs, ks, betas)]
    uws = [_nn(x, jnp.concatenate([v * beta, r_w], axis=1), HIGH) for x, v, beta, r_w in zip(xs, vs, betas, r_ws)]
    for ch, x, r_w, uw in zip(chs, xs, r_ws, uws):
        ch.update(x=x, r_w=r_w, uw=uw)
    return chs


def _gdn_fwd(qkv, proj, ab_blk, alog, dtb, nh, name, comm=None):
    t = qkv.shape[0]
    nc = t // CHUNK
    C = CHUNK
    hp = min(HP, nh)
    ng = nh // hp

    def body(*refs):
        c, hg = pl.program_id(0), pl.program_id(1)
        step = c * ng + hg
        ins, outs, scratch, comm_begin, comm_end = _comm_hooks(
            comm, refs, 6, 3, step == 0, step == (3 * nc * ng) // 4, step == nc * ng - 1)
        q_ref, k_ref, v_ref, ab_ref, al_ref, dt_ref = ins
        o_ref, x_ref, st_ref = outs
        s_sc, = scratch
        comm_begin()

        @pl.when(c == 0)
        def _():
            for hh in range(hp):
                s_sc[hg * hp + hh] = jnp.zeros((HD, HD), F32)

        sls = [slice(hh * HD, (hh + 1) * HD) for hh in range(hp)]
        qs, ks, vs = [q_ref[:, sl] for sl in sls], [k_ref[:, sl] for sl in sls], [v_ref[:, sl] for sl in sls]
        sts = [s_sc[hg * hp + hh] for hh in range(hp)]
        gates = [_gdn_gates(ab_ref[...], al_ref[...], dt_ref[...], hg * hp + hh, nh) for hh in range(hp)]
        chs = _gdn_chunks(qs, ks, vs, [g[0] for g in gates], [g[1] for g in gates], C)
        stbs = [_bf(st) for st in sts]
        vns = [ch["uw"][:, :HD] - _nt(_bf(ch["uw"][:, HD:]), stb) for ch, stb in zip(chs, stbs)]
        o_st = [_nt(_bf(q * ch["eg"]), stb) for q, ch, stb in zip(qs, chs, stbs)]
        outs_ = [o + _nn(_bf(ch["qk_raw"] * ch["gamma"]), _bf(vn)) for o, ch, vn in zip(o_st, chs, vns)]
        new_sts = [st * jnp.exp(ch["gl"]) + _tn(_bf(vn), _bf(k * ch["ekt"]))
                   for st, ch, vn, k in zip(sts, chs, vns, ks)]
        for hh in range(hp):
            o_ref[:, sls[hh]] = outs_[hh]
            x_ref[0, hh] = chs[hh]["x"]
            st_ref[0, hh] = sts[hh]
            s_sc[hg * hp + hh] = new_sts[hh]
        comm_end()

    blk = lambda off: pl.BlockSpec((C, hp * HD), lambda c, g: (c, off // hp + g))
    vec = pl.BlockSpec((1, HD), lambda c, g: (0, 0))
    cn = comm.n if comm is not None else 0
    return pl.pallas_call(
        body, name=name, grid=(nc, ng),
        in_specs=[blk(0), blk(nh), blk(2 * nh), pl.BlockSpec((C, HD), lambda c, g: (c, ab_blk)), vec, vec]
        + [ANY] * cn,
        out_specs=[blk(0),
                   pl.BlockSpec((1, hp,C, C), lambda c, g: (c, g, 0, 0)),
                   pl.BlockSpec((1, hp,HD, HD), lambda c, g: (c, g, 0, 0))] + [ANY] * cn,
        out_shape=[jax.ShapeDtypeStruct((t, nh * HD), F32),
                   jax.ShapeDtypeStruct((nc, nh, C, C), F32),
                   jax.ShapeDtypeStruct((nc, nh, HD, HD), F32)] + (comm.out_shapes() if cn else []),
        scratch_shapes=[pltpu.VMEM((nh, HD, HD), F32)] + (comm.scratch() if cn else []),
        compiler_params=_cp(("arbitrary", "arbitrary")),
    )(qkv, qkv, qkv, proj, alog, dtb, *(comm.arrays if cn else []))


def _gdn_bwd(qkv, proj, ab_blk, alog, dtb, do, x_sv, st_sv, nh, name, comm=None):
    t = qkv.shape[0]
    nc = t // CHUNK
    C = CHUNK
    hp = min(HP, nh)
    ng = nh // hp

    def one_head(h, hh, dst, q_ref, k_ref, v_ref, ab_ref, al_ref, dt_ref, do_ref, x_ref, st_ref):
        sl = slice(hh * HD, (hh + 1) * HD)
        q, k, v, do_ = q_ref[:, sl], k_ref[:, sl], v_ref[:, sl], do_ref[:, sl]
        la, beta, dla_da = _gdn_gates(ab_ref[...], al_ref[...], dt_ref[...], h, nh)
        low, strict = _tri(C, "lower"), _tri(C, "strict")
        g_b = _nn(low.astype(F32), jnp.broadcast_to(la, (C, HD)), HI)
        yield
        g_c = g_b[:, :C]
        gamma = jnp.where(low, jnp.exp(jnp.minimum(g_c - g_c.T, 0.0)), 0.0)
        eg = jnp.exp(g_b)
        gl = g_b[C - 1:C, :]
        ekt = jnp.exp(gl - g_b)
        egl = jnp.exp(gl)
        p = _nt(k, k, HIGH)
        yield
        x = x_ref[0, hh]
        r_w = k * (beta * eg)
        rhs = jnp.concatenate([v * beta, r_w], axis=1)
        uw = _nn(x, rhs, HIGH)
        yield
        u, w = uw[:, :HD], uw[:, HD:]
        qk_raw = _nt(_bf(q), _bf(k))
        yield
        qk = qk_raw * gamma
        st = st_ref[0, hh]
        stb, dstb = _bf(st), _bf(dst)
        vn = u - _nt(_bf(w), stb)
        yield
        qd, kt = q * eg, k * ekt

        dvn = _tn(_bf(qk), _bf(do_)) + _nt(_bf(kt), dstb)
        yield
        dq2 = jnp.where(low, _nt(_bf(do_), _bf(vn)), 0.0)
        yield
        dqd = _nn(_bf(do_), stb)
        yield
        dkt = _nn(_bf(vn), dstb)
        yield
        dw = -_nn(_bf(dvn), stb)
        yield
        dxx = jnp.concatenate([dvn, dw], axis=1)
        dr = _tn(x, dxx, HIGH)
        yield
        dm = -jnp.where(strict, _nt(dr, uw, HIGH), 0.0)
        yield
        dr_u, dr_w = dr[:, :HD], dr[:, HD:]
        rsum = lambda z: jnp.sum(z, axis=1, keepdims=True)

        dv = dr_u * beta
        dmg = dm * gamma
        dbeta = rsum(dr_u * v) + rsum(dr_w * k) * eg[:, :1] + rsum(dmg * p)
        yield
        dp = dmg * beta
        dq2g = dq2 * gamma
        dk = (dr_w * (beta * eg) + dkt * ekt + _tn(_bf(dq2g), _bf(q))
              + _nn(_bf(dp + dp.T), _bf(k)))
        yield
        dq = dqd * eg + _nn(_bf(dq2g), _bf(k))
        yield
        e = dp * p + dq2g * qk_raw
        t_kt = rsum(dkt * kt)
        dg = rsum(dqd * qd) + rsum(dr_w * r_w) - t_kt + rsum(e) - rsum(e.T)
        yield
        dgl = jnp.sum(t_kt, axis=0, keepdims=True) + jnp.sum(dst * st, keepdims=True) * egl[:, :1]
        rowc = lax.broadcasted_iota(jnp.int32, (C, 1), 0)
        dg = dg + jnp.where(rowc == C - 1, dgl, 0.0)
        dla = _nn(_tri(C, "upper").astype(F32), jnp.broadcast_to(dg, (C, HD)), HI)[:, :1]
        yield
        da = dla * dla_da
        db = dbeta * beta * (1.0 - beta)
        lane = lax.broadcasted_iota(jnp.int32, (C, HD), 1)
        dab = jnp.where(lane == h, da, 0.0) + jnp.where(lane == nh + h, db, 0.0)
        lane1 = lax.broadcasted_iota(jnp.int32, (1, HD), 1)
        d_alog = jnp.where(lane1 == h, jnp.sum(dla * la, axis=0, keepdims=True), 0.0)
        d_dtb = jnp.where(lane1 == h, jnp.sum(da, axis=0, keepdims=True), 0.0)
        new_dst = dst * egl + _tn(_bf(do_), _bf(qd)) - _tn(_bf(dvn), _bf(w))
        return dab, d_alog, d_dtb, new_dst, dq, dk, dv

    def body(*refs):
        c, hg = pl.program_id(0), pl.program_id(1)
        step = c * ng + hg
        ins, outs, scratch, comm_begin, comm_end = _comm_hooks(
            comm, refs, 9, 5, step == 0, step == (3 * nc * ng) // 4, step == nc * ng - 1)
        dq_ref, dk_ref, dv_ref, dab_ref, dpar_ref = outs
        ds_sc, = scratch
        comm_begin()

        @pl.when(c == 0)
        def _():
            for hh in range(hp):
                ds_sc[hg * hp + hh] = jnp.zeros((HD, HD), F32)

        @pl.when(step == 0)
        def _():
            dpar_ref[...] = jnp.zeros_like(dpar_ref)

        @pl.when(hg == 0)
        def _():
            dab_ref[...] = jnp.zeros_like(dab_ref)

        dsts = [ds_sc[hg * hp + hh] for hh in range(hp)]
        res = _interleave([one_head(hg * hp + hh, hh, dsts[hh], *ins) for hh in range(hp)])
        for hh in range(hp):
            sl = slice(hh * HD, (hh + 1) * HD)
            ds_sc[hg * hp + hh] = res[hh][3]
            dq_ref[:, sl], dk_ref[:, sl], dv_ref[:, sl] = res[hh][4], res[hh][5], res[hh][6]
        dab_ref[...] += sum(r[0] for r in res[1:]) + res[0][0]
        dpar_ref[0:1, :] += sum(r[1] for r in res[1:]) + res[0][1]
        dpar_ref[1:2, :] += sum(r[2] for r in res[1:]) + res[0][2]
        comm_end()

    rblk = lambda off: pl.BlockSpec((C, hp * HD), lambda c, g: (nc - 1 - c, off // hp + g))
    oblk = pl.BlockSpec((C, hp * HD), lambda c, g: (nc - 1 - c, g))
    vec = pl.BlockSpec((1, HD), lambda c, g: (0, 0))
    cn = comm.n if comm is not None else 0
    return pl.pallas_call(
        body, name=name, grid=(nc, ng),
        in_specs=[rblk(0), rblk(nh), rblk(2 * nh),
                  pl.BlockSpec((C, HD), lambda c, g: (nc - 1 - c, ab_blk)), vec, vec, oblk,
                  pl.BlockSpec((1, hp,C, C), lambda c, g: (nc - 1 - c, g, 0, 0)),
                  pl.BlockSpec((1, hp,HD, HD), lambda c, g: (nc - 1 - c, g, 0, 0))] + [ANY] * cn,
        out_specs=[oblk, oblk, oblk,
                   pl.BlockSpec((C, HD), lambda c, g: (nc - 1 - c, 0)),
                   pl.BlockSpec((8, HD), lambda c, g: (0, 0))] + [ANY] * cn,
        out_shape=[jax.ShapeDtypeStruct((t, nh * HD), F32)] * 3
        + [jax.ShapeDtypeStruct((t, HD), F32), jax.ShapeDtypeStruct((8, HD), F32)]
        + (comm.out_shapes() if cn else []),
        scratch_shapes=[pltpu.VMEM((nh, HD, HD), F32)] + (comm.scratch() if cn else []),
        compiler_params=_cp(("arbitrary", "arbitrary")),
    )(qkv, qkv, qkv, proj, alog, dtb, do, x_sv, st_sv, *(comm.arrays if cn else []))


def _ada_fwd(c_all, w, b, name):
    nb, d = c_all.shape
    n = w.shape[1]
    tn = _pick(n, 512)

    def body(c_ref, w_ref, b_ref, o_ref):
        cv = c_ref[...]
        o_ref[...] = _nn(cv * _sigmoid(cv), w_ref[...], HI) + b_ref[...]

    return pl.pallas_call(
        body, name=name, grid=(n // tn,),
        in_specs=[pl.BlockSpec((nb, d), lambda j: (0, 0)), pl.BlockSpec((d, tn), lambda j: (0, j)),
                  pl.BlockSpec((1, tn), lambda j: (0, j))],
        out_specs=pl.BlockSpec((nb, tn), lambda j: (0, j)),
        out_shape=jax.ShapeDtypeStruct((nb, n), F32),
        compiler_params=_cp(("parallel",)),
    )(c_all, w, b)


def _ada_wgrad(c_all, dmod, name):
    nb, d = c_all.shape
    n = dmod.shape[1]
    tn = _pick(n, 512)

    def body(c_ref, g_ref, o_ref):
        cv = c_ref[...]
        o_ref[...] = _tn(cv * _sigmoid(cv), g_ref[...], HI)

    return pl.pallas_call(
        body, name=name, grid=(n // tn,),
        in_specs=[pl.BlockSpec((nb, d), lambda j: (0, 0)), pl.BlockSpec((nb, tn), lambda j: (0, j))],
        out_specs=pl.BlockSpec((d, tn), lambda j: (0, j)),
        out_shape=jax.ShapeDtypeStruct((d, n), F32),
        compiler_params=_cp(("parallel",)),
    )(c_all, dmod)


def _adamw(w, m, v, g, name, parts=False):
    lead = w.ndim == 3
    r, cdim = w.shape[-2:]
    cap = max(SUBLANES, ADAM_BLOCK_ELEMS // cdim // SUBLANES * SUBLANES)
    tr = r if r <= cap else _pick_rows(r, cap)
    bc1 = 1.0 - ADAM_B1 ** ADAM_STEP
    bc2 = 1.0 - ADAM_B2 ** ADAM_STEP

    glist = list(g) if isinstance(g, (list, tuple)) else [g]
    bounds = [0]
    for ga in glist:
        bounds.append(bounds[-1] + ga.shape[-2] // tr)

    def body(w_ref, m_ref, v_ref, *rest):
        g_refs, (go_ref, d_ref, mo_ref, vo_ref) = rest[:len(glist)], rest[len(glist):]
        if parts:
            sums = []
            for g_ref in g_refs:
                gv = g_ref[0].astype(F32)
                for s in range(1, N_DEV):
                    gv = gv + g_ref[s].astype(F32)
                sums.append(gv)
            gv = sums[-1]
            for p in range(len(sums) - 2, -1, -1):
                gv = jnp.where(pl.program_id(0) < bounds[p + 1], sums[p], gv)
        else:
            gv = g_refs[0][...]
        wv = w_ref[...]
        mn = ADAM_B1 * m_ref[...] + (1.0 - ADAM_B1) * gv
        vn = ADAM_B2 * v_ref[...] + (1.0 - ADAM_B2) * (gv * gv)
        m_hat = mn / bc1
        v_hat = vn / bc2
        go_ref[...] = gv
        d_ref[...] = -ADAM_LR * (m_hat / (jnp.sqrt(v_hat) + ADAM_EPS) + ADAM_WD * wv)
        mo_ref[...] = mn
        vo_ref[...] = vn

    flat = pl.BlockSpec((tr, cdim), lambda i: (i, 0))
    spec = pl.BlockSpec((None, tr, cdim), lambda i: (0, i, 0)) if lead else flat
    def piece_spec(p):
        lo, n = bounds[p], bounds[p + 1] - bounds[p]
        return pl.BlockSpec((N_DEV, tr, cdim), lambda i: (0, jnp.clip(i - lo, 0, n - 1), 0))

    gspecs = [piece_spec(p) for p in range(len(glist))] if parts else [flat]
    return pl.pallas_call(
        body, name=name, grid=(r // tr,),
        in_specs=[spec, spec, spec] + gspecs,
        out_specs=[spec] * 4,
        out_shape=[jax.ShapeDtypeStruct(w.shape, F32)] * 4,
        compiler_params=_cp(("arbitrary",)),
    )(w, m, v, *glist)


def _pick_rows(r, pref):
    t = pref
    while r % t:
        t -= 8
    assert t > 0
    return t


def _dev_index(x, y, c):
    return 4 * x + 2 * y + c


class _Comm:
    def __init__(self, kind, arrays):
        self.kind, self.n = kind, len(arrays)
        self.arrays = [a[0] if isinstance(a, tuple) else a for a in arrays]
        self.rows = [(a[1], a[2]) if isinstance(a, tuple) else None for a in arrays]

    def out_shapes(self):
        if self.kind == "gather":
            return [jax.ShapeDtypeStruct((N_DEV,) + a.shape, a.dtype) for a in self.arrays]
        return [jax.ShapeDtypeStruct(a.shape if r is None else (N_DEV, r[1]) + a.shape[2:], a.dtype)
                for a, r in zip(self.arrays, self.rows)]

    def scratch(self):
        return [pltpu.SemaphoreType.DMA((self.n, 7)), pltpu.SemaphoreType.DMA((self.n, 7)),
                pltpu.SemaphoreType.DMA((self.n,))]

    def _gather_parts(self, ins, outs, sems):
        send_sems, recv_sems, local_sems = sems
        x, y, c = lax.axis_index("x"), lax.axis_index("y"), lax.axis_index("c")
        me, sibling = (x, y, c), (x, y, 1 - c)
        chips = [(1 - x, y), (x, 1 - y), (1 - x, 1 - y)]

        def copy(a, k, block, to, src=None):
            slot = outs[a].at[_dev_index(*block)]
            return pltpu.make_async_remote_copy(
                src_ref=slot if src is None else src, dst_ref=slot,
                send_sem=send_sems.at[a, k], recv_sem=recv_sems.at[a, k],
                device_id=to, device_id_type=MESH)

        n = self.n
        mine = [pltpu.make_async_copy(ins[a], outs[a].at[_dev_index(*me)], local_sems.at[a]) for a in range(n)]
        first = []
        for a in range(n):
            first.append(copy(a, 0, me, sibling, src=ins[a]))
            first += [copy(a, 1 + j, me, (*chip, c), src=ins[a]) for j, chip in enumerate(chips)]
        landed = [copy(a, 1 + j, (*chip, c), me) for j, chip in enumerate(chips) for a in range(n)]
        passed = [copy(a, 4 + j, (*chip, c), sibling) for j, chip in enumerate(chips) for a in range(n)]
        late = []
        for a in range(n):
            late.append(copy(a, 0, sibling, me))
            late += [copy(a, 4 + j, (*chip, 1 - c), me) for j, chip in enumerate(chips)]
        return mine, first, landed, passed, late

    def _exchange_parts(self, ins, outs, sems):
        send_sems, recv_sems, local_sems = sems
        x, y, c = lax.axis_index("x"), lax.axis_index("y"), lax.axis_index("c")
        my = _dev_index(x, y, c)
        n = self.n

        def block(a, j):
            r = self.rows[a]
            return ins[a].at[j] if r is None else ins[a].at[j, pl.ds(r[0], r[1])]

        mine = [pltpu.make_async_copy(block(a, my), outs[a].at[my], local_sems.at[a]) for a in range(n)]
        sends, recvs = [], []
        for k in range(1, N_DEV):
            px = (1 - x) if (k >> 2) & 1 else x
            py = (1 - y) if (k >> 1) & 1 else y
            pc = (1 - c) if k & 1 else c
            peer = _dev_index(px, py, pc)
            for a in range(n):
                sends.append(pltpu.make_async_remote_copy(
                    src_ref=block(a, peer), dst_ref=outs[a].at[my],
                    send_sem=send_sems.at[a, k - 1], recv_sem=recv_sems.at[a, k - 1],
                    device_id=(px, py, pc), device_id_type=MESH))
                recvs.append(pltpu.make_async_remote_copy(
                    src_ref=block(a, my), dst_ref=outs[a].at[peer],
                    send_sem=send_sems.at[a, k - 1], recv_sem=recv_sems.at[a, k - 1],
                    device_id=(x, y, c), device_id_type=MESH))
        return mine, sends, recvs

    def start(self, ins, outs, sems):
        if self.kind == "gather":
            mine, first, _, _, _ = self._gather_parts(ins, outs, sems)
        else:
            mine, first, _ = self._exchange_parts(ins, outs, sems)
        for cp in mine + first:
            cp.start()

    def mid(self, ins, outs, sems):
        if self.kind == "gather":
            _, _, landed, passed, _ = self._gather_parts(ins, outs, sems)
            for got, fwd in zip(landed, passed):
                got.wait_recv()
                fwd.start()

    def finish(self, ins, outs, sems):
        if self.kind == "gather":
            mine, first, _, passed, late = self._gather_parts(ins, outs, sems)
            for cp in late:
                cp.wait_recv()
            for cp in first + passed:
                cp.wait_send()
        else:
            mine, sends, recvs = self._exchange_parts(ins, outs, sems)
            for cp in sends:
                cp.wait_send()
            for cp in recvs:
                cp.wait_recv()
        for cp in mine:
            cp.wait()

    def run(self, name):
        n = self.n

        def body(*refs):
            ins, outs, sems = refs[:n], refs[n:2 * n], refs[2 * n:]
            self.start(ins, outs, sems)
            self.mid(ins, outs, sems)
            self.finish(ins, outs, sems)

        return pl.pallas_call(
            body, name=name, in_specs=[ANY] * n, out_specs=[ANY] * n,
            out_shape=self.out_shapes(), scratch_shapes=self.scratch(),
        )(*self.arrays)


def _all_gather(arrays, name):
    return _Comm("gather", arrays).run(name)


def _comm_hooks(comm, refs, n_in, n_out, first, middle, last):
    cn = comm.n if comm is not None else 0
    ins, cins = refs[:n_in], refs[n_in:n_in + cn]
    outs, couts = refs[n_in + cn:n_in + cn + n_out], refs[n_in + cn + n_out:n_in + 2 * cn + n_out]
    rest = refs[n_in + 2 * cn + n_out:]
    scratch, csems = (rest[:len(rest) - 3], rest[len(rest) - 3:]) if cn else (rest, ())

    def begin():
        if cn:
            pl.when(first)(lambda: comm.start(cins, couts, csems))
            pl.when(middle)(lambda: comm.mid(cins, couts, csems))

    def end():
        if cn:
            pl.when(last)(lambda: comm.finish(cins, couts, csems))

    return ins, outs, scratch, begin, end


def _local_step(x, tgt, mod, n1, n2, n3, n4, w_in_p, lb_logits, hg_norm, conv_w, alog, dtb, gdn_norm,
                late_w, dist=None):
    t, d = x.shape
    nh = d // 2 // HD
    ab_blk = 8 * nh
    sh_m, sc_m, gt_m, sh_f, sc_f, gt_f = [mod[i:i + 1] for i in range(6)]

    h1, r1 = _prenorm(x, n1, sc_m, sh_m, "prenorm_mix")
    if dist is None:
        proj = _mm(h1, w_in_p, "nn", [F32], "mm_proj")
        o_hg, a_sv, hst_sv = _hgrn2_fwd(proj, lb_logits, nh, "hgrn2_fwd")
        qkv = _gdn_prep(proj, conv_w, 4 * nh, nh, "gdn_prep")
        o_gd, x_sv, gst_sv = _gdn_fwd(qkv, proj, ab_blk, alog, dtb, nh, "gdn_fwd")
        w_out, w_ff1, w_ff2 = late_w
        exch = lambda arrays: None
    else:
        proj, g_ff2 = _mm(h1, w_in_p, "nn", [F32], "mm_proj", comm=_Comm("gather", late_w[2:]))
        o_hg, a_sv, hst_sv, g_out = _hgrn2_fwd(proj, lb_logits, nh, "hgrn2_fwd",
                                               comm=_Comm("gather", late_w[:1]))
        qkv = _gdn_prep(proj, conv_w, 4 * nh, nh, "gdn_prep")
        o_gd, x_sv, gst_sv, g_ff1 = _gdn_fwd(qkv, proj, ab_blk, alog, dtb, nh, "gdn_fwd",
                                             comm=_Comm("gather", late_w[1:2]))
        w_out, w_ff1, w_ff2 = dist["assemble"](g_out, g_ff1, g_ff2)
        exch = lambda arrays: _Comm("exchange", arrays)
    om_hg = _headnorm_fwd(o_hg, proj, 3 * nh, hg_norm, "headnorm_hg")
    om_gd = _headnorm_fwd(o_gd, proj, 7 * nh, gdn_norm, "headnorm_gdn")
    om = jnp.concatenate([om_hg, om_gd], axis=1)
    y1 = _mm(om, w_out, "nn", [F32], "mm_out")
    x1, r2, h2, r3 = _postnorm_prenorm(x, y1, n2, gt_m, n3, sc_f, sh_f, "postnorm_mix_prenorm_ffn")

    def relu2(acc, extra, outs):
        rl = jnp.maximum(acc, 0.0)
        outs[0][...] = (rl * rl).astype(BF16)

    act = _mm(h2, w_ff1, "nn", [BF16], "mm_ff1", epilogue=relu2)
    y2 = _mm(act, w_ff2, "nn", [F32], "mm_ff2")
    dout, dy2, loss, dgt_f, dn4 = _final_loss_bwd(x1, y2, n4, gt_f, tgt, "final_loss_bwd")
    whole_t = dict(tk=t, tn=1024)
    dw_ff2 = _mm(act, dy2, "tn", [BF16], "mm_dw_ff2", **whole_t)

    def drelu2(acc, extra, outs):
        outs[0][...] = (acc * (2.0 * jnp.sqrt(extra[0][...].astype(F32)))).astype(BF16)

    recv = {}
    ff2a, ff2b = dist["parts_ff2"](dw_ff2) if dist else (None, None)
    du, *recv["ff2a"] = _listed(_mm(dy2, w_ff2, "nt", [BF16], "mm_da", epilogue=drelu2, extras=(act,),
                                    comm=exch([ff2a])))
    ff1_cols = dict(by_cols=True, tk=t, tn=dist["n_ff"]) if dist else whole_t
    dw_ff1, *recv["ff2b"] = _listed(_mm(h2, du, "tn", [BF16], "mm_dw_ff1", comm=exch([ff2b]), **ff1_cols))
    ff1a, ff1b = dist["parts_ff1"](dw_ff1) if dist else (None, None)
    dh2, *recv["ff1a"] = _listed(_mm(du, w_ff1, "nt", [F32], "mm_dh2", comm=exch([ff1a])))
    dx1, dy1, dsh_f, dsc_f, dn3, dgt_m, dn2 = _prenorm_postnorm_bwd(
        dh2, x1, r3, n3, sc_f, dout, y1, r2, n2, gt_m, "prenorm_ffn_postnorm_mix_bwd")

    dw_out = _mm(om, dy1, "tn", [BF16], "mm_dw_out", **whole_t)
    dom = _mm(dy1, w_out, "nt", [F32], "mm_dom")
    do_hg, dg_hg, dhgn = _headnorm_bwd(dom, 0, o_hg, proj, 3 * nh, hg_norm, "headnorm_hg_bwd")
    do_gd, dg_gd, dgdn = _headnorm_bwd(dom, 1, o_gd, proj, 7 * nh, gdn_norm, "headnorm_gdn_bwd")
    p_out = dist["parts_out"](dw_out) if dist else None
    dq_hg, df_hg, di_hg, dl0, *recv["ff1b_out"] = _hgrn2_bwd(proj, lb_logits, do_hg, a_sv, hst_sv, nh,
                                                             "hgrn2_bwd", comm=exch([ff1b, p_out]))
    dq_g, dk_g, dv_g, dab, dpar = _gdn_bwd(qkv, proj, ab_blk, alog, dtb, do_gd, x_sv, gst_sv, nh, "gdn_bwd")
    du_conv, dconv = _gdn_prep_bwd(proj, conv_w, dq_g, dk_g, dv_g, 4 * nh, nh, "gdn_prep_bwd")
    dproj = jnp.concatenate([dq_hg, df_hg, di_hg, dg_hg, du_conv, dg_gd, dab.astype(BF16)], axis=1)
    if dist is None:
        dw_in = _mm(h1, dproj, "tn", [BF16], "mm_dw_in")
        dh1 = _mm(dproj, w_in_p, "nt", [F32], "mm_dh1", tk=1664)
    else:
        q4 = d // 4
        dw_in_a = _mm(h1[:, :q4], dproj, "tn", [BF16], "mm_dw_in_a")
        dw_in_b, in_a = _mm(h1[:, q4:2 * q4], dproj, "tn", [BF16], "mm_dw_in_b",
                            comm=exch([dist["parts_in"](dw_in_a)]))
        dw_in_c, in_b = _mm(h1[:, 2 * q4:], dproj, "tn", [BF16], "mm_dw_in_c",
                            comm=exch([dist["parts_in"](dw_in_b)]))
        dh1, in_c = _mm(dproj, w_in_p, "nt", [F32], "mm_dh1", tk=1664, comm=exch([dist["parts_in"](dw_in_c)]))
        recv["in"] = [in_a, in_b, in_c]
        dw_in = None
    dx, dsh_m, dsc_m, dn1 = _prenorm_bwd(dh1, x, r1, n1, sc_m, dx1, "prenorm_mix_bwd")

    dmod = jnp.concatenate([dsh_m, dsc_m, dgt_m, dsh_f, dsc_f, dgt_f], axis=0)
    grads = dict(dmod=dmod, n1=dn1, n2=dn2, n3=dn3, n4=dn4, w_in=dw_in, lb0=dl0, hg_norm=dhgn, conv=dconv,
                 alog=dpar[0:1], dtb=dpar[1:2], gdn_norm=dgdn, w_out=dw_out, w_ff1=dw_ff1, w_ff2=dw_ff2,
                 recv=recv)
    return loss, dx, grads


def _pack(vals):
    rows = []
    for vv in vals:
        flat = vv.reshape(-1)
        flat = jnp.pad(flat, (0, (-flat.shape[0]) % (SUBLANES * LANES)))
        rows.append(flat.reshape(-1, LANES))
    return jnp.concatenate(rows, axis=0)


def _unpack(packed, shapes):
    out, r = [], 0
    for shp in shapes:
        size = 1
        for s in shp:
            size *= s
        nr = -(-size // (SUBLANES * LANES)) * SUBLANES
        out.append(packed[r:r + nr].reshape(-1)[:size].reshape(shp))
        r += nr
    return out


def _sum_parts(parts, name):
    _, r, cdim = parts.shape

    def body(p_ref, o_ref):
        acc = p_ref[0]
        for s in range(1, N_DEV):
            acc = acc + p_ref[s]
        o_ref[...] = acc

    return pl.pallas_call(
        body, name=name,
        out_shape=jax.ShapeDtypeStruct((r, cdim), F32),
        compiler_params=_cp(),
    )(parts)


def kernel(x, c, w_ada, b_ada, pre_mix_norm, post_mix_norm, pre_ffn_norm, post_ffn_norm, w_in, hg_lb_logits, hg_norm, gdn_conv_w, gdn_a_log, gdn_dt_bias, gdn_norm, w_out, w_ff1, w_ff2, loss_target, m_w_ada, m_b_ada, m_pre_mix_norm, m_post_mix_norm, m_pre_ffn_norm, m_post_ffn_norm, m_w_in, m_hg_lb_logits, m_hg_norm, m_gdn_conv_w, m_gdn_a_log, m_gdn_dt_bias, m_gdn_norm, m_w_out, m_w_ff1, m_w_ff2, v_w_ada, v_b_ada, v_pre_mix_norm, v_post_mix_norm, v_pre_ffn_norm, v_post_ffn_norm, v_w_in, v_hg_lb_logits, v_hg_norm, v_gdn_conv_w, v_gdn_a_log, v_gdn_dt_bias, v_gdn_norm, v_w_out, v_w_ff1, v_w_ff2):
    t, d = x.shape[1], x.shape[2]
    nh = d // 2 // HD
    in_cols = w_in.shape[2] * N_DEV
    main = in_cols - 2 * nh
    me = _dev_index(lax.axis_index("x"), lax.axis_index("y"), lax.axis_index("c"))

    c_all, conv_g = _all_gather([c, gdn_conv_w[0]], "gather_small")
    c_all = c_all.reshape(N_DEV, d)
    conv_full = conv_g.transpose(1, 0, 2).reshape(CONV_K, -1)
    w_in_g = _all_gather([w_in[0].astype(BF16)], "gather_w_in")[0]
    w_in_full = w_in_g.transpose(1, 0, 2).reshape(d, in_cols)
    w_in_p = jnp.concatenate([w_in_full, jnp.zeros((d, LANES - 2 * nh), BF16)], axis=1)
    late_w = [w_out[0].astype(BF16), w_ff1[0].astype(BF16), w_ff2[0].astype(BF16)]

    n_in = w_in.shape[2]
    n_ff = w_ff1.shape[2]

    def halves(p):
        r = p.shape[1] // 2
        return (p, 0, r), (p, r, r)

    dist = dict(
        assemble=lambda g_out, g_ff1, g_ff2: (g_out.reshape(d, d), g_ff1.transpose(1, 0, 2).reshape(d, -1),
                                              g_ff2.reshape(-1, d)),
        n_ff=n_ff,
        parts_ff2=lambda dw: halves(dw.reshape(N_DEV, -1, d)),
        parts_ff1=halves,
        parts_out=lambda dw: dw.reshape(N_DEV, d // N_DEV, d),
        parts_in=lambda dw: dw[:, :in_cols].reshape(dw.shape[0], N_DEV, n_in).transpose(1, 0, 2),
    )

    n_ada = w_ada.shape[2]
    b_loc = lax.dynamic_slice(b_ada, (0, me * n_ada), (1, n_ada))
    mod_part = _ada_fwd(c_all, w_ada[0], b_loc, "ada_fwd")
    mod_all = _all_gather([mod_part], "gather_mod")[0]
    mod = lax.dynamic_slice(mod_all, (0, me, 0), (N_DEV, 1, n_ada)).reshape(6, d)

    pad_lane = lambda vv: jnp.concatenate([vv, jnp.zeros((1, LANES - vv.shape[1]), F32)], axis=1)
    loss, dx, g = _local_step(
        x[0], loss_target[0], mod, pre_mix_norm, post_mix_norm, pre_ffn_norm, post_ffn_norm, w_in_p,
        hg_lb_logits, hg_norm, conv_full, pad_lane(gdn_a_log), pad_lane(gdn_dt_bias), gdn_norm,
        late_w, dist)

    rep_names = ["b_ada", "n1", "n2", "n3", "n4", "lb", "hg_norm", "alog", "dtb", "gdn_norm"]
    rep_w = [b_ada, pre_mix_norm, post_mix_norm, pre_ffn_norm, post_ffn_norm, hg_lb_logits, hg_norm,
             gdn_a_log, gdn_dt_bias, gdn_norm]
    rep_m = [m_b_ada, m_pre_mix_norm, m_post_mix_norm, m_pre_ffn_norm, m_post_ffn_norm, m_hg_lb_logits,
             m_hg_norm, m_gdn_a_log, m_gdn_dt_bias, m_gdn_norm]
    rep_v = [v_b_ada, v_pre_mix_norm, v_post_mix_norm, v_pre_ffn_norm, v_post_ffn_norm, v_hg_lb_logits,
             v_hg_norm, v_gdn_a_log, v_gdn_dt_bias, v_gdn_norm]
    rep_shapes = [a.shape for a in rep_w]
    g_lb = jnp.stack([g["lb0"], -g["lb0"]], axis=0)
    rep_g = [g["dmod"], g["n1"], g["n2"], g["n3"], g["n4"], g_lb, g["hg_norm"],
             g["alog"][:, :nh], g["dtb"][:, :nh], g["gdn_norm"]]
    small = _pack(rep_g + [g["conv"]])
    n_rep_rows = _pack(rep_g).shape[0]
    pad_rows = (-small.shape[0]) % 8
    if pad_rows:
        small = jnp.concatenate([small, jnp.zeros((pad_rows, LANES), F32)], axis=0)
    small_all = _all_gather([small], "gather_small_grads")[0]
    small_sum = _sum_parts(small_all, "sum_small_grads")
    rep_out = _adamw(_pack(rep_w), _pack(rep_m), _pack(rep_v), small_sum[:n_rep_rows], "adamw_small")
    rep_g_o, rep_d_o, rep_m_o, rep_v_o = [dict(zip(rep_names, _unpack(p, rep_shapes))) for p in rep_out]

    conv_sum = small_sum[n_rep_rows:n_rep_rows + CONV_K * conv_full.shape[1] // LANES].reshape(CONV_K, -1)
    n_conv = gdn_conv_w.shape[2]
    conv_loc = lax.dynamic_slice(conv_sum, (0, me * n_conv), (CONV_K, n_conv))
    conv_o = _adamw(gdn_conv_w, m_gdn_conv_w, v_gdn_conv_w, conv_loc, "adamw_conv")

    dmod_all = small_all[:, :6 * d // LANES, :].reshape(N_DEV, 6 * d)
    dmod_loc = lax.dynamic_slice(dmod_all, (0, me * n_ada), (N_DEV, n_ada))
    g_ada = _ada_wgrad(c_all, dmod_loc, "ada_wgrad")
    ada_o = _adamw(w_ada, m_w_ada, v_w_ada, g_ada, "adamw_ada")

    rc = g["recv"]
    r_ff2 = [rc["ff2a"][0], rc["ff2b"][0]]
    r_ff1 = [rc["ff1a"][0], rc["ff1b_out"][0]]
    r_out, r_in = rc["ff1b_out"][1], rc["in"]
    in_o = _adamw(w_in, m_w_in, v_w_in, r_in, "adamw_w_in", parts=True)
    out_o = _adamw(w_out, m_w_out, v_w_out, r_out, "adamw_w_out", parts=True)
    ff1_o = _adamw(w_ff1, m_w_ff1, v_w_ff1, r_ff1, "adamw_w_ff1", parts=True)
    ff2_o = _adamw(w_ff2, m_w_ff2, v_w_ff2, r_ff2, "adamw_w_ff2", parts=True)

    loss_tot = lax.psum(loss[0, 0], ("x", "y", "c"))

    def leaf(kind):
        return [ada_o[kind], rep_out_d[kind]["b_ada"], rep_out_d[kind]["n1"], rep_out_d[kind]["n2"],
                rep_out_d[kind]["n3"], rep_out_d[kind]["n4"], in_o[kind], rep_out_d[kind]["lb"],
                rep_out_d[kind]["hg_norm"], conv_o[kind], rep_out_d[kind]["alog"], rep_out_d[kind]["dtb"],
                rep_out_d[kind]["gdn_norm"], out_o[kind], ff1_o[kind], ff2_o[kind]]

    rep_out_d = [rep_g_o, rep_d_o, rep_m_o, rep_v_o]
    return (loss_tot, dx[None], *leaf(0), *leaf(1), *leaf(2), *leaf(3))
```

```python
import functools

import jax
import jax.numpy as jnp
from jax import lax
from jax.experimental import pallas as pl
from jax.experimental.pallas import tpu as pltpu

F32 = jnp.float32
BF16 = jnp.bfloat16
HI = lax.Precision.HIGHEST
HIGH = lax.Precision.HIGH

EPS = 1e-6
CHUNK = 64
SB = 16
NSB = CHUNK // SB
HP = 8
HD = 128
CONV_K = 4
N_DEV = 8
LANES = 128
SUBLANES = 8
VMEM_LIMIT = 56 * 1024 * 1024
MM_FULL_K = 2048

ADAM_BLOCK_ELEMS = 256 * 1024
ADAM_LR = 0.001
ADAM_B1 = 0.9
ADAM_B2 = 0.999
ADAM_EPS = 1e-08
ADAM_WD = 0.01
ADAM_STEP = 10

ANY = pl.BlockSpec(memory_space=pl.ANY)
MESH = pl.DeviceIdType.MESH


def _cp(sem=None):
    return pltpu.CompilerParams(dimension_semantics=sem, vmem_limit_bytes=VMEM_LIMIT)


def _dot(a, b, dims, precision=None):
    return lax.dot_general(a, b, (dims, ((), ())), precision=precision, preferred_element_type=F32)


def _nn(a, b, precision=None):
    return _dot(a, b, ((1,), (0,)), precision)


def _nt(a, b, precision=None):
    return _dot(a, b, ((1,), (1,)), precision)


def _tn(a, b, precision=None):
    return _dot(a, b, ((0,), (0,)), precision)


def _bf(x):
    return x.astype(BF16)


def _sigmoid(x):
    return 1.0 / (1.0 + jnp.exp(-x))


def _interleave(gens):
    results = [None] * len(gens)
    live = list(range(len(gens)))
    while live:
        for i in list(live):
            try:
                next(gens[i])
            except StopIteration as stop:
                results[i] = stop.value
                live.remove(i)
    return results


def _listed(res):
    return list(res) if isinstance(res, (list, tuple)) else [res]


def _pick(n, pref):
    if n <= pref:
        return n
    t = pref
    while n % t:
        t -= LANES
    assert t > 0, (n, pref)
    return t


def _mm(a, b, mode, out_dtypes, name, epilogue=None, extras=(), tm=1024, tn=2048, tk=1024, comm=None,
        by_cols=False):
    if mode == "nn":
        (m, kd), (_, n) = a.shape, b.shape
    elif mode == "nt":
        (m, kd), (n, _) = a.shape, b.shape
    else:
        (kd, m), (_, n) = a.shape, b.shape
    if kd <= MM_FULL_K:
        tk = kd
    tm, tn, tk = _pick(m, tm), _pick(n, tn), _pick(kd, tk)
    nk = kd // tk
    if mode == "nn":
        a_spec = pl.BlockSpec((tm, tk), lambda i, j, k: (i, k))
        b_spec = pl.BlockSpec((tk, tn), lambda i, j, k: (k, j))
        dims = ((1,), (0,))
    elif mode == "nt":
        a_spec = pl.BlockSpec((tm, tk), lambda i, j, k: (i, k))
        b_spec = pl.BlockSpec((tn, tk), lambda i, j, k: (j, k))
        dims = ((1,), (1,))
    else:
        a_spec = pl.BlockSpec((tk, tm), lambda i, j, k: (k, i))
        b_spec = pl.BlockSpec((tk, tn), lambda i, j, k: (k, j))
        dims = ((0,), (0,))
    o_spec = pl.BlockSpec((tm, tn), lambda i, j, k: (i, j))
    if by_cols:
        assert epilogue is None and not extras
        res_spec = pl.BlockSpec((None, tm, tn), lambda i, j, k: (j, i, 0))
        res_shape = (n // tn, m, tn)
    else:
        res_spec, res_shape = o_spec, (m, n)
    n_extra, n_out = len(extras), len(out_dtypes)

    gm, gn = m // tm, n // tn
    cn = comm.n if comm is not None else 0

    def body(*refs):
        i, j, k = pl.program_id(0), pl.program_id(1), pl.program_id(2)
        at0 = (j == 0) & (k == 0)
        ins, out_refs, scratch, comm_begin, comm_end = _comm_hooks(
            comm, refs, 2 + n_extra, n_out, (i == 0) & at0, (i == gm - 1) & at0,
            (i == gm - 1) & (j == gn - 1) & (k == nk - 1))
        a_ref, b_ref, extra_refs = ins[0], ins[1], ins[2:]
        comm_begin()
        if nk == 1:
            part = _dot(a_ref[...], b_ref[...], dims)
            if epilogue is None:
                out_refs[0][...] = part.astype(out_dtypes[0])
            else:
                epilogue(part, extra_refs, out_refs)
        else:
            acc, = scratch

            @pl.when(k == 0)
            def _():
                acc[...] = jnp.zeros_like(acc)

            acc[...] += _dot(a_ref[...], b_ref[...], dims)

            @pl.when(k == nk - 1)
            def _():
                if epilogue is None:
                    out_refs[0][...] = acc[...].astype(out_dtypes[0])
                else:
                    epilogue(acc[...], extra_refs, out_refs)

        comm_end()

    acc_scratch = [] if nk == 1 else [pltpu.VMEM((tm, tn), F32)]
    sem = ("arbitrary",) * 3 if cn else ("parallel", "parallel", "arbitrary")
    outs = pl.pallas_call(
        body, name=name,
        grid=(gm, gn, nk),
        in_specs=[a_spec, b_spec] + [o_spec] * n_extra + [ANY] * cn,
        out_specs=[res_spec] * n_out + [ANY] * cn,
        out_shape=[jax.ShapeDtypeStruct(res_shape, dt) for dt in out_dtypes] + (comm.out_shapes() if cn else []),
        scratch_shapes=acc_scratch + (comm.scratch() if cn else []),
        compiler_params=_cp(sem),
    )(a, b, *extras, *(comm.arrays if cn else []))
    return outs[0] if n_out + cn == 1 else outs


def _row_spec(tb, d):
    return pl.BlockSpec((tb, d), lambda i: (i, 0))


def _vec_spec(d):
    return pl.BlockSpec((1, d), lambda i: (0, 0))


def _prenorm(x, w, sc, sh, name):
    t, d = x.shape
    tb = _pick(t, 256)

    def body(x_ref, w_ref, sc_ref, sh_ref, h_ref, r_ref):
        xv = x_ref[...]
        r = lax.rsqrt(jnp.mean(xv * xv, axis=-1, keepdims=True) + EPS)
        h_ref[...] = ((xv * r * w_ref[...]) * (1.0 + sc_ref[...]) + sh_ref[...]).astype(BF16)
        r_ref[...] = r

    return pl.pallas_call(
        body, name=name, grid=(t // tb,),
        in_specs=[_row_spec(tb, d), _vec_spec(d), _vec_spec(d), _vec_spec(d)],
        out_specs=[_row_spec(tb, d), _row_spec(tb, 1)],
        out_shape=[jax.ShapeDtypeStruct((t, d), BF16), jax.ShapeDtypeStruct((t, 1), F32)],
        compiler_params=_cp(("parallel",)),
    )(x, w, sc, sh)


def _final_loss_bwd(x, y, w, gt, tgt, name):
    t, d = x.shape
    tb = _pick(t, 256)

    def body(x_ref, y_ref, w_ref, gt_ref, tgt_ref, dout_ref, dy_ref, loss_ref, dgt_ref, dw_ref):
        @pl.when(pl.program_id(0) == 0)
        def _():
            loss_ref[...] = jnp.zeros_like(loss_ref)
            dgt_ref[...] = jnp.zeros_like(dgt_ref)
            dw_ref[...] = jnp.zeros_like(dw_ref)

        yv, wv, gtv = y_ref[...], w_ref[...], gt_ref[...]
        r = lax.rsqrt(jnp.mean(yv * yv, axis=-1, keepdims=True) + EPS)
        z = yv * r
        nz = z * wv
        diff = (x_ref[...] + gtv * nz) - tgt_ref[...]
        loss_ref[...] += 0.5 * jnp.sum(jnp.mean(diff * diff, axis=-1, keepdims=True), axis=0, keepdims=True)
        dxv = diff * (1.0 / d)
        dout_ref[...] = dxv
        dgt_ref[...] += jnp.sum(dxv * nz, axis=0, keepdims=True)
        dn = dxv * gtv
        dw_ref[...] += jnp.sum(dn * z, axis=0, keepdims=True)
        dz = dn * wv
        dy_ref[...] = (r * (dz - z * jnp.mean(dz * z, axis=-1, keepdims=True))).astype(BF16)

    return pl.pallas_call(
        body, name=name, grid=(t // tb,),
        in_specs=[_row_spec(tb, d), _row_spec(tb, d), _vec_spec(d), _vec_spec(d), _row_spec(tb, d)],
        out_specs=[_row_spec(tb, d), _row_spec(tb, d), pl.BlockSpec((1, 1), lambda i: (0, 0)),
                   _vec_spec(d), _vec_spec(d)],
        out_shape=[jax.ShapeDtypeStruct((t, d), F32), jax.ShapeDtypeStruct((t, d), BF16),
                   jax.ShapeDtypeStruct((1, 1), F32), jax.ShapeDtypeStruct((1, d), F32),
                   jax.ShapeDtypeStruct((1, d), F32)],
        compiler_params=_cp(("arbitrary",)),
    )(x, y, w, gt, tgt)


def _postnorm_prenorm(x, y, w_post, gt, w_pre, sc, sh, name):
    t, d = x.shape
    tb = _pick(t, 256)

    def body(x_ref, y_ref, wp_ref, gt_ref, wn_ref, sc_ref, sh_ref, x1_ref, r_ref, h_ref, r1_ref):
        yv = y_ref[...]
        r = lax.rsqrt(jnp.mean(yv * yv, axis=-1, keepdims=True) + EPS)
        x1 = x_ref[...] + gt_ref[...] * (yv * r * wp_ref[...])
        r1 = lax.rsqrt(jnp.mean(x1 * x1, axis=-1, keepdims=True) + EPS)
        x1_ref[...] = x1
        r_ref[...] = r
        h_ref[...] = ((x1 * r1 * wn_ref[...]) * (1.0 + sc_ref[...]) + sh_ref[...]).astype(BF16)
        r1_ref[...] = r1

    return pl.pallas_call(
        body, name=name, grid=(t // tb,),
        in_specs=[_row_spec(tb, d), _row_spec(tb, d)] + [_vec_spec(d)] * 5,
        out_specs=[_row_spec(tb, d), _row_spec(tb, 1), _row_spec(tb, d), _row_spec(tb, 1)],
        out_shape=[jax.ShapeDtypeStruct((t, d), F32), jax.ShapeDtypeStruct((t, 1), F32),
                   jax.ShapeDtypeStruct((t, d), BF16), jax.ShapeDtypeStruct((t, 1), F32)],
        compiler_params=_cp(("parallel",)),
    )(x, y, w_post, gt, w_pre, sc, sh)


def _prenorm_postnorm_bwd(dh, x, r_pre, w_pre, sc, dres, y, r_post, w_post, gt, name):
    t, d = x.shape
    tb = _pick(t, 256)

    def body(dh_ref, x_ref, rp_ref, wp_ref, sc_ref, dres_ref, y_ref, rq_ref, wq_ref, gt_ref,
             dx_ref, dy_ref, dsh_ref, dsc_ref, dwp_ref, dgt_ref, dwq_ref):
        @pl.when(pl.program_id(0) == 0)
        def _():
            for ref in (dsh_ref, dsc_ref, dwp_ref, dgt_ref, dwq_ref):
                ref[...] = jnp.zeros_like(ref)

        dhv, rv, wv = dh_ref[...], rp_ref[...], wp_ref[...]
        z = x_ref[...] * rv
        dsh_ref[...] += jnp.sum(dhv, axis=0, keepdims=True)
        dsc_ref[...] += jnp.sum(dhv * (z * wv), axis=0, keepdims=True)
        dzw = dhv * (1.0 + sc_ref[...])
        dwp_ref[...] += jnp.sum(dzw * z, axis=0, keepdims=True)
        dz = dzw * wv
        dxv = dres_ref[...] + rv * (dz - z * jnp.mean(dz * z, axis=-1, keepdims=True))
        dx_ref[...] = dxv

        rq, wq = rq_ref[...], wq_ref[...]
        zq = y_ref[...] * rq
        dgt_ref[...] += jnp.sum(dxv * (zq * wq), axis=0, keepdims=True)
        dn = dxv * gt_ref[...]
        dwq_ref[...] += jnp.sum(dn * zq, axis=0, keepdims=True)
        dzq = dn * wq
        dy_ref[...] = (rq * (dzq - zq * jnp.mean(dzq * zq, axis=-1, keepdims=True))).astype(BF16)

    rs, r1, vs = _row_spec(tb, d), _row_spec(tb, 1), _vec_spec(d)
    return pl.pallas_call(
        body, name=name, grid=(t // tb,),
        in_specs=[rs, rs, r1, vs, vs, rs, rs, r1, vs, vs],
        out_specs=[rs, rs] + [vs] * 5,
        out_shape=[jax.ShapeDtypeStruct((t, d), F32), jax.ShapeDtypeStruct((t, d), BF16)]
        + [jax.ShapeDtypeStruct((1, d), F32)] * 5,
        compiler_params=_cp(("arbitrary",)),
    )(dh, x, r_pre, w_pre, sc, dres, y, r_post, w_post, gt)


def _prenorm_bwd(dh, x, r, w, sc, dres, name):
    t, d = x.shape
    tb = _pick(t, 256)

    def body(dh_ref, x_ref, r_ref, w_ref, sc_ref, dres_ref, dx_ref, dsh_ref, dsc_ref, dw_ref):
        @pl.when(pl.program_id(0) == 0)
        def _():
            dsh_ref[...] = jnp.zeros_like(dsh_ref)
            dsc_ref[...] = jnp.zeros_like(dsc_ref)
            dw_ref[...] = jnp.zeros_like(dw_ref)

        dhv, rv, wv = dh_ref[...], r_ref[...], w_ref[...]
        z = x_ref[...] * rv
        dsh_ref[...] += jnp.sum(dhv, axis=0, keepdims=True)
        dsc_ref[...] += jnp.sum(dhv * (z * wv), axis=0, keepdims=True)
        dzw = dhv * (1.0 + sc_ref[...])
        dw_ref[...] += jnp.sum(dzw * z, axis=0, keepdims=True)
        dz = dzw * wv
        dx_ref[...] = dres_ref[...] + rv * (dz - z * jnp.mean(dz * z, axis=-1, keepdims=True))

    return pl.pallas_call(
        body, name=name, grid=(t // tb,),
        in_specs=[_row_spec(tb, d), _row_spec(tb, d), _row_spec(tb, 1), _vec_spec(d), _vec_spec(d),
                  _row_spec(tb, d)],
        out_specs=[_row_spec(tb, d), _vec_spec(d), _vec_spec(d), _vec_spec(d)],
        out_shape=[jax.ShapeDtypeStruct((t, d), F32)] + [jax.ShapeDtypeStruct((1, d), F32)] * 3,
        compiler_params=_cp(("arbitrary",)),
    )(dh, x, r, w, sc, dres)


def _headnorm_fwd(o, proj, g_blk, nw, name):
    t, wd = o.shape
    nh = wd // HD
    tb = _pick(t, 512)
    gb = g_blk * HD // wd

    def body(o_ref, g_ref, nw_ref, out_ref):
        o3 = o_ref[...].reshape(tb, nh, HD)
        g3 = g_ref[...].reshape(tb, nh, HD)
        rh = lax.rsqrt(jnp.mean(o3 * o3, axis=-1, keepdims=True) + EPS)
        res = (o3 * rh * nw_ref[...].reshape(1, 1, HD)) * (g3 * _sigmoid(g3))
        out_ref[...] = res.reshape(tb, wd).astype(BF16)

    return pl.pallas_call(
        body, name=name, grid=(t // tb,),
        in_specs=[_row_spec(tb, wd), pl.BlockSpec((tb, wd), lambda i: (i, gb)), _vec_spec(HD)],
        out_specs=_row_spec(tb, wd),
        out_shape=jax.ShapeDtypeStruct((t, wd), BF16),
        compiler_params=_cp(("parallel",)),
    )(o, proj, nw)


def _headnorm_bwd(dom, col_blk, o, proj, g_blk, nw, name):
    t, wd = o.shape
    nh = wd // HD
    tb = _pick(t, 512)
    gb = g_blk * HD // wd

    def body(do_ref, o_ref, g_ref, nw_ref, dout_ref, dg_ref, dnw_ref):
        @pl.when(pl.program_id(0) == 0)
        def _():
            dnw_ref[...] = jnp.zeros_like(dnw_ref)

        dn = do_ref[...].reshape(tb, nh, HD)
        o3 = o_ref[...].reshape(tb, nh, HD)
        g3 = g_ref[...].reshape(tb, nh, HD)
        nw3 = nw_ref[...].reshape(1, 1, HD)
        rh = lax.rsqrt(jnp.mean(o3 * o3, axis=-1, keepdims=True) + EPS)
        z = o3 * rh
        sg = _sigmoid(g3)
        sl = g3 * sg
        dnw_ref[...] += jnp.sum(jnp.sum(dn * sl * z, axis=1), axis=0, keepdims=True)
        dg_ref[...] = (dn * (z * nw3) * (sg * (1.0 + g3 * (1.0 - sg)))).reshape(tb, wd).astype(BF16)
        dz = dn * sl * nw3
        dout_ref[...] = (rh * (dz - z * jnp.mean(dz * z, axis=-1, keepdims=True))).reshape(tb, wd)

    return pl.pallas_call(
        body, name=name, grid=(t // tb,),
        in_specs=[pl.BlockSpec((tb, wd), lambda i: (i, col_blk)), _row_spec(tb, wd),
                  pl.BlockSpec((tb, wd), lambda i: (i, gb)), _vec_spec(HD)],
        out_specs=[_row_spec(tb, wd), _row_spec(tb, wd), _vec_spec(HD)],
        out_shape=[jax.ShapeDtypeStruct((t, wd), F32), jax.ShapeDtypeStruct((t, wd), BF16),
                   jax.ShapeDtypeStruct((1, HD), F32)],
        compiler_params=_cp(("arbitrary",)),
    )(dom, o, proj, nw)


def _tri(n, kind):
    r = lax.broadcasted_iota(jnp.int32, (n, n), 0)
    c = lax.broadcasted_iota(jnp.int32, (n, n), 1)
    if kind == "lower":
        return r >= c
    if kind == "strict":
        return r > c
    return r <= c


def _hg_gate(fl, l0, l1):
    mx = jnp.maximum(l0, l1)
    e0, e1 = jnp.exp(l0 - mx), jnp.exp(l1 - mx)
    lb = e0 / (e0 + e1)
    sg = _sigmoid(fl)
    f = lb + (1.0 - lb) * sg
    return lb, sg, f


def _hgrn2_fwd(proj, lb_logits, nh, name, comm=None):
    t = proj.shape[0]
    nc = t // CHUNK
    C = CHUNK
    lg = lb_logits.reshape(2, nh, 1, HD)

    hp = min(HP, nh)
    ng = nh // hp

    def one_head(hh, st, q_ref, f_ref, i_ref, lg_ref, p_sc, r_sc):
        sl = slice(hh * HD, (hh + 1) * HD)
        q, v = q_ref[:, sl], i_ref[:, sl]
        _, _, f = _hg_gate(f_ref[:, sl], lg_ref[0, hh], lg_ref[1, hh])
        k = 1.0 - f
        low = _tri(C, "lower")
        b = _nn(low.astype(F32), jnp.log(f), HI)
        yield
        lane_c = lax.broadcasted_iota(jnp.int32, (SB, C), 1)
        lane_h = lax.broadcasted_iota(jnp.int32, (SB, HD), 1)
        row_h = lax.broadcasted_iota(jnp.int32, (SB, HD), 0)
        ones = jnp.ones((HD, HD), F32)

        for i in range(NSB):
            qi, ki, bi = q[SB * i:SB * (i + 1)], k[SB * i:SB * (i + 1)], b[SB * i:SB * (i + 1)]
            for s in range(SB):
                e = jnp.exp(jnp.minimum(bi - bi[s:s + 1], 0.0))
                p = jnp.where(row_h >= s, qi * ki[s:s + 1] * e, 0.0)
                p_sc[hh, pl.ds((i * SB + s) * SB, SB), :] = p
            yield
        r_sc[hh] = _nn(p_sc[hh], ones, HIGH)
        yield
        a_rows = []
        for i in range(NSB):
            acc = jnp.zeros((SB, HD), F32)
            for s in range(SB):
                acc = jnp.where(lane_h == SB * i + s, r_sc[hh, pl.ds((i * SB + s) * SB, SB), :], acc)
            acc = acc[:, :C]
            if i > 0:
                r = b[SB * i - 1:SB * i]
                bi = b[SB * i:SB * (i + 1)]
                qf = q[SB * i:SB * (i + 1)] * jnp.exp(bi - r)
                kf = k * jnp.exp(jnp.minimum(r - b, 0.0))
                acc = acc + jnp.where(lane_c < SB * i, _nt(qf, kf, HIGH), 0.0)
            a_rows.append(acc)
            yield
        a = jnp.concatenate(a_rows, axis=0)
        bl = b[C - 1:C, :]
        o = _nn(_bf(a), _bf(v)) + _nt(_bf(q * jnp.exp(b)), _bf(st))
        yield
        new_st = st * jnp.exp(bl) + _tn(_bf(v), _bf(k * jnp.exp(bl - b)))
        return o, a, new_st

    def body(*refs):
        c, hg = pl.program_id(0), pl.program_id(1)
        step = c * ng + hg
        ins, outs, scratch, comm_begin, comm_end = _comm_hooks(
            comm, refs, 4, 3, step == 0, step == (3 * nc * ng) // 4, step == nc * ng - 1)
        o_ref, a_ref, st_ref = outs
        s_sc, p_sc, r_sc = scratch
        comm_begin()

        @pl.when(c == 0)
        def _():
            for hh in range(hp):
                s_sc[hg * hp + hh] = jnp.zeros((HD, HD), F32)

        sts = [s_sc[hg * hp + hh] for hh in range(hp)]
        res = _interleave([one_head(hh, sts[hh], *ins, p_sc, r_sc) for hh in range(hp)])
        for hh in range(hp):
            o_ref[:, hh * HD:(hh + 1) * HD] = res[hh][0]
            a_ref[0, hh] = res[hh][1]
            st_ref[0, hh] = sts[hh]
            s_sc[hg * hp + hh] = res[hh][2]
        comm_end()

    blk = lambda off: pl.BlockSpec((C, hp * HD), lambda c, g: (c, off // hp + g))
    cn = comm.n if comm is not None else 0
    return pl.pallas_call(
        body, name=name, grid=(nc, ng),
        in_specs=[blk(0), blk(nh), blk(2 * nh),
                  pl.BlockSpec((2, hp, 1, HD), lambda c, g: (0, g, 0, 0))] + [ANY] * cn,
        out_specs=[blk(0),
                   pl.BlockSpec((1, hp, C, C), lambda c, g: (c, g, 0, 0)),
                   pl.BlockSpec((1, hp, HD, HD), lambda c, g: (c, g, 0, 0))] + [ANY] * cn,
        out_shape=[jax.ShapeDtypeStruct((t, nh * HD), F32),
                   jax.ShapeDtypeStruct((nc, nh, C, C), F32),
                   jax.ShapeDtypeStruct((nc, nh, HD, HD), F32)] + (comm.out_shapes() if cn else []),
        scratch_shapes=[pltpu.VMEM((nh, HD, HD), F32), pltpu.VMEM((hp, C * SB, HD), F32),
                        pltpu.VMEM((hp, C * SB, HD), F32)] + (comm.scratch() if cn else []),
        compiler_params=_cp(("arbitrary", "arbitrary")),
    )(proj, proj, proj, lg, *(comm.arrays if cn else []))


def _hgrn2_bwd(proj, lb_logits, do, a_sv, st_sv, nh, name, comm=None):
    t = proj.shape[0]
    nc = t // CHUNK
    C = CHUNK
    lg = lb_logits.reshape(2, nh, 1, HD)
    hp = min(HP, nh)
    ng = nh // hp

    def one_head(hh, dst, q_ref, f_ref, i_ref, lg_ref, do_ref, a_ref, st_ref, p_sc, r_sc):
        sl = slice(hh * HD, (hh + 1) * HD)
        q, v, do_ = q_ref[:, sl], i_ref[:, sl], do_ref[:, sl]
        lb, sg, f = _hg_gate(f_ref[:, sl], lg_ref[0, hh], lg_ref[1, hh])
        k = 1.0 - f
        low = _tri(C, "lower")
        b = _nn(low.astype(F32), jnp.log(f), HI)
        yield
        bl = b[C - 1:C, :]
        eb, ekb = jnp.exp(b), jnp.exp(bl - b)
        qb, kb = q * eb, k * ekb
        a, st = a_ref[0, hh], st_ref[0, hh]

        da = jnp.where(low, _nt(_bf(do_), _bf(v)), 0.0)
        yield
        dv = _tn(_bf(a), _bf(do_)) + _nt(_bf(kb), _bf(dst))
        yield
        dqb = _nn(_bf(do_), _bf(st))
        dkb = _nn(_bf(v), _bf(dst))
        yield

        row = lax.broadcasted_iota(jnp.int32, (C, HD), 0)
        lane_c = lax.broadcasted_iota(jnp.int32, (SB, C), 1)
        row_h = lax.broadcasted_iota(jnp.int32, (SB, HD), 0)
        ones = jnp.ones((HD, HD), F32)
        sel = (lax.broadcasted_iota(jnp.int32, (C, C * SB), 0)
               == jnp.right_shift(lax.broadcasted_iota(jnp.int32, (C, C * SB), 1), SB.bit_length() - 1)).astype(F32)

        for i in range(NSB):
            doi, vi = do_[SB * i:SB * (i + 1)], v[SB * i:SB * (i + 1)]
            for s in range(SB):
                p_sc[hh, pl.ds((i * SB + s) * SB, SB), :] = doi * vi[s:s + 1]
            yield
        r_sc[hh] = _nn(p_sc[hh], ones, HIGH)
        yield
        dq_rows = []
        dk_off = jnp.zeros((C, HD), F32)
        for i in range(NSB):
            qi, ki, bi = q[SB * i:SB * (i + 1)], k[SB * i:SB * (i + 1)], b[SB * i:SB * (i + 1)]
            acc = jnp.zeros((SB, HD), F32)
            for s in range(SB):
                e = jnp.exp(jnp.minimum(bi - bi[s:s + 1], 0.0))
                g = jnp.where(row_h >= s, r_sc[hh, pl.ds((i * SB + s) * SB, SB), :] * e, 0.0)
                acc = acc + g * ki[s:s + 1]
                p_sc[hh, pl.ds((i * SB + s) * SB, SB), :] = g * qi
            yield
            if i > 0:
                r = b[SB * i - 1:SB * i]
                fq = jnp.exp(bi - r)
                fk = jnp.exp(jnp.minimum(r - b, 0.0))
                dai = jnp.where(lane_c < SB * i, da[SB * i:SB * (i + 1)], 0.0)
                acc = acc + _nn(dai, k * fk, HIGH) * fq
                dk_off = dk_off + _tn(dai, qi * fq, HIGH) * fk
                yield
            dq_rows.append(acc)
        dqi = jnp.concatenate(dq_rows, axis=0)
        dq = dqi + dqb * eb
        dk_inter = dkb * ekb
        dk = _nn(sel, p_sc[hh], HIGH) + dk_off + dk_inter
        yield
        db = q * dq - k * dk
        extra = (jnp.sum(k * dk_inter, axis=0, keepdims=True)
                 + jnp.exp(bl) * jnp.sum(dst * st, axis=0, keepdims=True))
        db = db + jnp.where(row == C - 1, extra, 0.0)
        dlf = _nn(_tri(C, "upper").astype(F32), db, HI)
        yield
        df = dlf / f - dk
        dfl = (df * (1.0 - lb) * sg * (1.0 - sg)).astype(BF16)
        dl = jnp.sum(df * (1.0 - sg), axis=0, keepdims=True) * (lb * (1.0 - lb))
        new_dst = dst * jnp.exp(bl) + _tn(_bf(do_), _bf(qb))
        return dq.astype(BF16), dfl, dv.astype(BF16), dl, new_dst

    def body(*refs):
        c, hg = pl.program_id(0), pl.program_id(1)
        step = c * ng + hg
        ins, outs, scratch, comm_begin, comm_end = _comm_hooks(
            comm, refs, 7, 4, step == 0, step == (3 * nc * ng) // 4, step == nc * ng - 1)
        dq_ref, df_ref, di_ref, dl_ref = outs
        ds_sc, p_sc, r_sc = scratch
        comm_begin()

        @pl.when(c == 0)
        def _():
            for hh in range(hp):
                ds_sc[hg * hp + hh] = jnp.zeros((HD, HD), F32)

        @pl.when(step == 0)
        def _():
            dl_ref[...] = jnp.zeros_like(dl_ref)

        dsts = [ds_sc[hg * hp + hh] for hh in range(hp)]
        res = _interleave([one_head(hh, dsts[hh], *ins, p_sc, r_sc) for hh in range(hp)])
        for hh in range(hp):
            sl = slice(hh * HD, (hh + 1) * HD)
            dq_ref[:, sl], df_ref[:, sl], di_ref[:, sl] = res[hh][0], res[hh][1], res[hh][2]
            dl_ref[pl.ds(hg * hp + hh, 1), :] += res[hh][3]
            ds_sc[hg * hp + hh] = res[hh][4]
        comm_end()

    rblk = lambda off: pl.BlockSpec((C, hp * HD), lambda c, g: (nc - 1 - c, off // hp + g))
    oblk = pl.BlockSpec((C, hp * HD), lambda c, g: (nc - 1 - c, g))
    cn = comm.n if comm is not None else 0
    return pl.pallas_call(
        body, name=name, grid=(nc, ng),
        in_specs=[rblk(0), rblk(nh), rblk(2 * nh),
                  pl.BlockSpec((2, hp, 1, HD), lambda c, g: (0, g, 0, 0)),
                  oblk,
                  pl.BlockSpec((1, hp, C, C), lambda c, g: (nc - 1 - c, g, 0, 0)),
                  pl.BlockSpec((1, hp, HD, HD), lambda c, g: (nc - 1 - c, g, 0, 0))] + [ANY] * cn,
        out_specs=[oblk, oblk, oblk, pl.BlockSpec((nh, HD), lambda c, g: (0, 0))] + [ANY] * cn,
        out_shape=[jax.ShapeDtypeStruct((t, nh * HD), BF16)] * 3 + [jax.ShapeDtypeStruct((nh, HD), F32)]
        + (comm.out_shapes() if cn else []),
        scratch_shapes=[pltpu.VMEM((nh, HD, HD), F32), pltpu.VMEM((hp, C * SB, HD), F32),
                        pltpu.VMEM((hp, C * SB, HD), F32)] + (comm.scratch() if cn else []),
        compiler_params=_cp(("arbitrary", "arbitrary")),
    )(proj, proj, proj, lg, do, a_sv, st_sv, *(comm.arrays if cn else []))


def _shift_rows(u, d, row):
    t = u.shape[0]
    if d == 0:
        return u
    rolled = pltpu.roll(u, d % t, 0)
    if d > 0:
        return jnp.where(row >= d, rolled, 0.0)
    return jnp.where(row < t + d, rolled, 0.0)


def _gdn_prep(proj, conv_w, blk0, nh, name):
    t = proj.shape[0]
    scale = HD ** -0.5

    def body(u_ref, w_ref, o_ref):
        j = pl.program_id(0)
        u, w = u_ref[...], w_ref[...]
        row = lax.broadcasted_iota(jnp.int32, (t, HD), 0)
        y = w[CONV_K - 1:CONV_K, :] * u
        for d in range(1, CONV_K):
            y = y + w[CONV_K - 1 - d:CONV_K - d, :] * _shift_rows(u, d, row)
        a = y * _sigmoid(y)
        n = a * lax.rsqrt(jnp.sum(a * a, axis=-1, keepdims=True) + EPS)
        n = n * jnp.where(j < nh, scale, 1.0)
        o_ref[...] = jnp.where(j < 2 * nh, n, a)

    return pl.pallas_call(
        body, name=name, grid=(3 * nh,),
        in_specs=[pl.BlockSpec((t, HD), lambda j: (0, blk0 + j)), pl.BlockSpec((CONV_K, HD), lambda j: (0, j))],
        out_specs=pl.BlockSpec((t, HD), lambda j: (0, j)),
        out_shape=jax.ShapeDtypeStruct((t, 3 * nh * HD), F32),
        compiler_params=_cp(("parallel",)),
    )(proj, conv_w)


def _gdn_prep_bwd(proj, conv_w, dq, dk, dv, blk0, nh, name):
    t = proj.shape[0]
    scale = HD ** -0.5

    def body(u_ref, w_ref, dq_ref, dk_ref, dv_ref, du_ref, dw_ref):
        j = pl.program_id(0)
        u, w = u_ref[...], w_ref[...]
        dout = jnp.where(j < nh, dq_ref[...], jnp.where(j < 2 * nh, dk_ref[...], dv_ref[...]))
        row = lax.broadcasted_iota(jnp.int32, (t, HD), 0)
        us = [_shift_rows(u, d, row) for d in range(CONV_K)]
        y = w[CONV_K - 1:CONV_K, :] * us[0]
        for d in range(1, CONV_K):
            y = y + w[CONV_K - 1 - d:CONV_K - d, :] * us[d]
        sg = _sigmoid(y)
        a = y * sg
        rs = lax.rsqrt(jnp.sum(a * a, axis=-1, keepdims=True) + EPS)
        n = a * rs
        dn = dout * jnp.where(j < nh, scale, 1.0)
        da_n = rs * (dn - n * jnp.sum(dn * n, axis=-1, keepdims=True))
        da = jnp.where(j < 2 * nh, da_n, dout)
        dy = da * (sg * (1.0 + y * (1.0 - sg)))
        du = w[CONV_K - 1:CONV_K, :] * dy
        for d in range(1, CONV_K):
            du = du + w[CONV_K - 1 - d:CONV_K - d, :] * _shift_rows(dy, -d, row)
        du_ref[...] = du.astype(BF16)
        for d in range(CONV_K):
            dw_ref[CONV_K - 1 - d:CONV_K - d, :] = jnp.sum(dy * us[d], axis=0, keepdims=True)

    return pl.pallas_call(
        body, name=name, grid=(3 * nh,),
        in_specs=[pl.BlockSpec((t, HD), lambda j: (0, blk0 + j)), pl.BlockSpec((CONV_K, HD), lambda j: (0, j))]
        + [pl.BlockSpec((t, HD), functools.partial(lambda p, j: (0, jnp.clip(j - p * nh, 0, nh - 1)), p))
           for p in range(3)],
        out_specs=[pl.BlockSpec((t, HD), lambda j: (0, j)), pl.BlockSpec((CONV_K, HD), lambda j: (0, j))],
        out_shape=[jax.ShapeDtypeStruct((t, 3 * nh * HD), BF16), jax.ShapeDtypeStruct((CONV_K, 3 * nh * HD), F32)],
        compiler_params=_cp(("arbitrary",)),
    )(proj, conv_w, dq, dk, dv)


def _gdn_gates(ab, alog, dtb, h, nh):
    lane = lax.broadcasted_iota(jnp.int32, ab.shape, 1)
    x = ab + dtb
    sp = jnp.maximum(x, 0.0) + jnp.log(1.0 + jnp.exp(-jnp.abs(x)))
    ea = jnp.exp(alog)
    la_all = -ea * sp
    beta_all = _sigmoid(ab)
    pick = lambda val, ln: jnp.sum(jnp.where(lane == ln, val, 0.0), axis=1, keepdims=True)
    la = pick(la_all, h)
    beta = pick(beta_all, nh + h)
    dla_da = pick(-ea * _sigmoid(x), h)
    return la, beta, dla_da


def _unit_lower_inverses(ms, C):
    nb = C // SB
    sh = SB.bit_length() - 1
    rowb = jnp.right_shift(lax.broadcasted_iota(jnp.int32, (C, C), 0), sh)
    colb = jnp.right_shift(lax.broadcasted_iota(jnp.int32, (C, C), 1), sh)
    eye = (lax.broadcasted_iota(jnp.int32, (SB, SB), 0) == lax.broadcasted_iota(jnp.int32, (SB, SB), 1)).astype(F32)
    spread = (jnp.bitwise_and(lax.broadcasted_iota(jnp.int32, (SB, C), 1), SB - 1)
              == lax.broadcasted_iota(jnp.int32, (SB, C), 0)).astype(F32)
    blocks = [[m[SB * i:SB * (i + 1), SB * i:SB * (i + 1)] for i in range(nb)] for m in ms]
    xs = [[eye] * nb for _ in ms]
    for s in range(SB - 1):
        xs = [[x - b[:, s:s + 1] * x[s:s + 1, :] for x, b in zip(xh, bh)] for xh, bh in zip(xs, blocks)]
    ts = [jnp.where(rowb == colb, _nn(jnp.concatenate(xh, axis=0), spread, HIGH), 0.0) for xh in xs]
    lvl = 1
    while (1 << lvl) <= nb:
        off = ((jnp.right_shift(rowb, lvl) == jnp.right_shift(colb, lvl))
               & (jnp.right_shift(rowb, lvl - 1) != jnp.right_shift(colb, lvl - 1)))
        ts = [t - _nn(t, _nn(jnp.where(off, m, 0.0), t, HIGH), HIGH) for t, m in zip(ts, ms)]
        lvl += 1
    return ts


def _gdn_chunks(qs, ks, vs, las, betas, C):
    low, strict = _tri(C, "lower"), _tri(C, "strict")
    eye = (lax.broadcasted_iota(jnp.int32, (C, C), 0) == lax.broadcasted_iota(jnp.int32, (C, C), 1)).astype(F32)
    g_bs = [_nn(low.astype(F32), jnp.broadcast_to(la, (C, HD)), HI) for la in las]
    ps = [_nt(k, k, HIGH) for k in ks]
    qks = [_nt(_bf(q), _bf(k)) for q, k in zip(qs, ks)]
    chs = []
    for g_b, p, qk_raw, beta in zip(g_bs, ps, qks, betas):
        g_c = g_b[:, :C]
        gamma = jnp.where(low, jnp.exp(jnp.minimum(g_c - g_c.T, 0.0)), 0.0)
        gl = g_b[C - 1:C, :]
        chs.append(dict(gamma=gamma, eg=jnp.exp(g_b), gl=gl, ekt=jnp.exp(gl - g_b), p=p,
                        m=jnp.where(strict, beta * p * gamma, 0.0), qk_raw=qk_raw))
    xs = _unit_lower_inverses([ch["m"] for ch in chs], C)
    r_ws = [k * (beta * ch["eg"]) for ch, k, beta in zip(chs, ks, betas)]
    uws = [_nn(x, jnp.concatenate([v * beta, r_w], axis=1), HIGH) for x, v, beta, r_w in zip(xs, vs, betas, r_ws)]
    for ch, x, r_w, uw in zip(chs, xs, r_ws, uws):
        ch.update(x=x, r_w=r_w, uw=uw)
    return chs


def _gdn_fwd(qkv, proj, ab_blk, alog, dtb, nh, name, comm=None):
    t = qkv.shape[0]
    nc = t // CHUNK
    C = CHUNK
    hp = min(HP, nh)
    ng = nh // hp

    def body(*refs):
        c, hg = pl.program_id(0), pl.program_id(1)
        step = c * ng + hg
        ins, outs, scratch, comm_begin, comm_end = _comm_hooks(
            comm, refs, 6, 3, step == 0, step == (3 * nc * ng) // 4, step == nc * ng - 1)
        q_ref, k_ref, v_ref, ab_ref, al_ref, dt_ref = ins
        o_ref, x_ref, st_ref = outs
        s_sc, = scratch
        comm_begin()

        @pl.when(c == 0)
        def _():
            for hh in range(hp):
                s_sc[hg * hp + hh] = jnp.zeros((HD, HD), F32)

        sls = [slice(hh * HD, (hh + 1) * HD) for hh in range(hp)]
        qs, ks, vs = [q_ref[:, sl] for sl in sls], [k_ref[:, sl] for sl in sls], [v_ref[:, sl] for sl in sls]
        sts = [s_sc[hg * hp + hh] for hh in range(hp)]
        gates = [_gdn_gates(ab_ref[...], al_ref[...], dt_ref[...], hg * hp + hh, nh) for hh in range(hp)]
        chs = _gdn_chunks(qs, ks, vs, [g[0] for g in gates], [g[1] for g in gates], C)
        stbs = [_bf(st) for st in sts]
        vns = [ch["uw"][:, :HD] - _nt(_bf(ch["uw"][:, HD:]), stb) for ch, stb in zip(chs, stbs)]
        o_st = [_nt(_bf(q * ch["eg"]), stb) for q, ch, stb in zip(qs, chs, stbs)]
        outs_ = [o + _nn(_bf(ch["qk_raw"] * ch["gamma"]), _bf(vn)) for o, ch, vn in zip(o_st, chs, vns)]
        new_sts = [st * jnp.exp(ch["gl"]) + _tn(_bf(vn), _bf(k * ch["ekt"]))
                   for st, ch, vn, k in zip(sts, chs, vns, ks)]
        for hh in range(hp):
            o_ref[:, sls[hh]] = outs_[hh]
            x_ref[0, hh] = chs[hh]["x"]
            st_ref[0, hh] = sts[hh]
            s_sc[hg * hp + hh] = new_sts[hh]
        comm_end()

    blk = lambda off: pl.BlockSpec((C, hp * HD), lambda c, g: (c, off // hp + g))
    vec = pl.BlockSpec((1, HD), lambda c, g: (0, 0))
    cn = comm.n if comm is not None else 0
    return pl.pallas_call(
        body, name=name, grid=(nc, ng),
        in_specs=[blk(0), blk(nh), blk(2 * nh), pl.BlockSpec((C, HD), lambda c, g: (c, ab_blk)), vec, vec]
        + [ANY] * cn,
        out_specs=[blk(0),
                   pl.BlockSpec((1, hp,C, C), lambda c, g: (c, g, 0, 0)),
                   pl.BlockSpec((1, hp,HD, HD), lambda c, g: (c, g, 0, 0))] + [ANY] * cn,
        out_shape=[jax.ShapeDtypeStruct((t, nh * HD), F32),
                   jax.ShapeDtypeStruct((nc, nh, C, C), F32),
                   jax.ShapeDtypeStruct((nc, nh, HD, HD), F32)] + (comm.out_shapes() if cn else []),
        scratch_shapes=[pltpu.VMEM((nh, HD, HD), F32)] + (comm.scratch() if cn else []),
        compiler_params=_cp(("arbitrary", "arbitrary")),
    )(qkv, qkv, qkv, proj, alog, dtb, *(comm.arrays if cn else []))


def _gdn_bwd(qkv, proj, ab_blk, alog, dtb, do, x_sv, st_sv, nh, name, comm=None):
    t = qkv.shape[0]
    nc = t // CHUNK
    C = CHUNK
    hp = min(HP, nh)
    ng = nh // hp

    def one_head(h, hh, dst, q_ref, k_ref, v_ref, ab_ref, al_ref, dt_ref, do_ref, x_ref, st_ref):
        sl = slice(hh * HD, (hh + 1) * HD)
        q, k, v, do_ = q_ref[:, sl], k_ref[:, sl], v_ref[:, sl], do_ref[:, sl]
        la, beta, dla_da = _gdn_gates(ab_ref[...], al_ref[...], dt_ref[...], h, nh)
        low, strict = _tri(C, "lower"), _tri(C, "strict")
        g_b = _nn(low.astype(F32), jnp.broadcast_to(la, (C, HD)), HI)
        yield
        g_c = g_b[:, :C]
        gamma = jnp.where(low, jnp.exp(jnp.minimum(g_c - g_c.T, 0.0)), 0.0)
        eg = jnp.exp(g_b)
        gl = g_b[C - 1:C, :]
        ekt = jnp.exp(gl - g_b)
        egl = jnp.exp(gl)
        p = _nt(k, k, HIGH)
        yield
        x = x_ref[0, hh]
        r_w = k * (beta * eg)
        rhs = jnp.concatenate([v * beta, r_w], axis=1)
        uw = _nn(x, rhs, HIGH)
        yield
        u, w = uw[:, :HD], uw[:, HD:]
        qk_raw = _nt(_bf(q), _bf(k))
        yield
        qk = qk_raw * gamma
        st = st_ref[0, hh]
        stb, dstb = _bf(st), _bf(dst)
        vn = u - _nt(_bf(w), stb)
        yield
        qd, kt = q * eg, k * ekt

        dvn = _tn(_bf(qk), _bf(do_)) + _nt(_bf(kt), dstb)
        yield
        dq2 = jnp.where(low, _nt(_bf(do_), _bf(vn)), 0.0)
        yield
        dqd = _nn(_bf(do_), stb)
        yield
        dkt = _nn(_bf(vn), dstb)
        yield
        dw = -_nn(_bf(dvn), stb)
        yield
        dxx = jnp.concatenate([dvn, dw], axis=1)
        dr = _tn(x, dxx, HIGH)
        yield
        dm = -jnp.where(strict, _nt(dr, uw, HIGH), 0.0)
        yield
        dr_u, dr_w = dr[:, :HD], dr[:, HD:]
        rsum = lambda z: jnp.sum(z, axis=1, keepdims=True)

        dv = dr_u * beta
        dmg = dm * gamma
        dbeta = rsum(dr_u * v) + rsum(dr_w * k) * eg[:, :1] + rsum(dmg * p)
        yield
        dp = dmg * beta
        dq2g = dq2 * gamma
        dk = (dr_w * (beta * eg) + dkt * ekt + _tn(_bf(dq2g), _bf(q))
              + _nn(_bf(dp + dp.T), _bf(k)))
        yield
        dq = dqd * eg + _nn(_bf(dq2g), _bf(k))
        yield
        e = dp * p + dq2g * qk_raw
        t_kt = rsum(dkt * kt)
        dg = rsum(dqd * qd) + rsum(dr_w * r_w) - t_kt + rsum(e) - rsum(e.T)
        yield
        dgl = jnp.sum(t_kt, axis=0, keepdims=True) + jnp.sum(dst * st, keepdims=True) * egl[:, :1]
        rowc = lax.broadcasted_iota(jnp.int32, (C, 1), 0)
        dg = dg + jnp.where(rowc == C - 1, dgl, 0.0)
        dla = _nn(_tri(C, "upper").astype(F32), jnp.broadcast_to(dg, (C, HD)), HI)[:, :1]
        yield
        da = dla * dla_da
        db = dbeta * beta * (1.0 - beta)
        lane = lax.broadcasted_iota(jnp.int32, (C, HD), 1)
        dab = jnp.where(lane == h, da, 0.0) + jnp.where(lane == nh + h, db, 0.0)
        lane1 = lax.broadcasted_iota(jnp.int32, (1, HD), 1)
        d_alog = jnp.where(lane1 == h, jnp.sum(dla * la, axis=0, keepdims=True), 0.0)
        d_dtb = jnp.where(lane1 == h, jnp.sum(da, axis=0, keepdims=True), 0.0)
        new_dst = dst * egl + _tn(_bf(do_), _bf(qd)) - _tn(_bf(dvn), _bf(w))
        return dab, d_alog, d_dtb, new_dst, dq, dk, dv

    def body(*refs):
        c, hg = pl.program_id(0), pl.program_id(1)
        step = c * ng + hg
        ins, outs, scratch, comm_begin, comm_end = _comm_hooks(
            comm, refs, 9, 5, step == 0, step == (3 * nc * ng) // 4, step == nc * ng - 1)
        dq_ref, dk_ref, dv_ref, dab_ref, dpar_ref = outs
        ds_sc, = scratch
        comm_begin()

        @pl.when(c == 0)
        def _():
            for hh in range(hp):
                ds_sc[hg * hp + hh] = jnp.zeros((HD, HD), F32)

        @pl.when(step == 0)
        def _():
            dpar_ref[...] = jnp.zeros_like(dpar_ref)

        @pl.when(hg == 0)
        def _():
            dab_ref[...] = jnp.zeros_like(dab_ref)

        dsts = [ds_sc[hg * hp + hh] for hh in range(hp)]
        res = _interleave([one_head(hg * hp + hh, hh, dsts[hh], *ins) for hh in range(hp)])
        for hh in range(hp):
            sl = slice(hh * HD, (hh + 1) * HD)
            ds_sc[hg * hp + hh] = res[hh][3]
            dq_ref[:, sl], dk_ref[:, sl], dv_ref[:, sl] = res[hh][4], res[hh][5], res[hh][6]
        dab_ref[...] += sum(r[0] for r in res[1:]) + res[0][0]
        dpar_ref[0:1, :] += sum(r[1] for r in res[1:]) + res[0][1]
        dpar_ref[1:2, :] += sum(r[2] for r in res[1:]) + res[0][2]
        comm_end()

    rblk = lambda off: pl.BlockSpec((C, hp * HD), lambda c, g: (nc - 1 - c, off // hp + g))
    oblk = pl.BlockSpec((C, hp * HD), lambda c, g: (nc - 1 - c, g))
    vec = pl.BlockSpec((1, HD), lambda c, g: (0, 0))
    cn = comm.n if comm is not None else 0
    return pl.pallas_call(
        body, name=name, grid=(nc, ng),
        in_specs=[rblk(0), rblk(nh), rblk(2 * nh),
                  pl.BlockSpec((C, HD), lambda c, g: (nc - 1 - c, ab_blk)), vec, vec, oblk,
                  pl.BlockSpec((1, hp,C, C), lambda c, g: (nc - 1 - c, g, 0, 0)),
                  pl.BlockSpec((1, hp,HD, HD), lambda c, g: (nc - 1 - c, g, 0, 0))] + [ANY] * cn,
        out_specs=[oblk, oblk, oblk,
                   pl.BlockSpec((C, HD), lambda c, g: (nc - 1 - c, 0)),
                   pl.BlockSpec((8, HD), lambda c, g: (0, 0))] + [ANY] * cn,
        out_shape=[jax.ShapeDtypeStruct((t, nh * HD), F32)] * 3
        + [jax.ShapeDtypeStruct((t, HD), F32), jax.ShapeDtypeStruct((8, HD), F32)]
        + (comm.out_shapes() if cn else []),
        scratch_shapes=[pltpu.VMEM((nh, HD, HD), F32)] + (comm.scratch() if cn else []),
        compiler_params=_cp(("arbitrary", "arbitrary")),
    )(qkv, qkv, qkv, proj, alog, dtb, do, x_sv, st_sv, *(comm.arrays if cn else []))


def _ada_fwd(c_all, w, b, name):
    nb, d = c_all.shape
    n = w.shape[1]
    tn = _pick(n, 512)

    def body(c_ref, w_ref, b_ref, o_ref):
        cv = c_ref[...]
        o_ref[...] = _nn(cv * _sigmoid(cv), w_ref[...], HI) + b_ref[...]

    return pl.pallas_call(
        body, name=name, grid=(n // tn,),
        in_specs=[pl.BlockSpec((nb, d), lambda j: (0, 0)), pl.BlockSpec((d, tn), lambda j: (0, j)),
                  pl.BlockSpec((1, tn), lambda j: (0, j))],
        out_specs=pl.BlockSpec((nb, tn), lambda j: (0, j)),
        out_shape=jax.ShapeDtypeStruct((nb, n), F32),
        compiler_params=_cp(("parallel",)),
    )(c_all, w, b)


def _ada_wgrad(c_all, dmod, name):
    nb, d = c_all.shape
    n = dmod.shape[1]
    tn = _pick(n, 512)

    def body(c_ref, g_ref, o_ref):
        cv = c_ref[...]
        o_ref[...] = _tn(cv * _sigmoid(cv), g_ref[...], HI)

    return pl.pallas_call(
        body, name=name, grid=(n // tn,),
        in_specs=[pl.BlockSpec((nb, d), lambda j: (0, 0)), pl.BlockSpec((nb, tn), lambda j: (0, j))],
        out_specs=pl.BlockSpec((d, tn), lambda j: (0, j)),
        out_shape=jax.ShapeDtypeStruct((d, n), F32),
        compiler_params=_cp(("parallel",)),
    )(c_all, dmod)


def _adamw(w, m, v, g, name, parts=False):
    lead = w.ndim == 3
    r, cdim = w.shape[-2:]
    cap = max(SUBLANES, ADAM_BLOCK_ELEMS // cdim // SUBLANES * SUBLANES)
    tr = r if r <= cap else _pick_rows(r, cap)
    bc1 = 1.0 - ADAM_B1 ** ADAM_STEP
    bc2 = 1.0 - ADAM_B2 ** ADAM_STEP

    glist = list(g) if isinstance(g, (list, tuple)) else [g]
    bounds = [0]
    for ga in glist:
        bounds.append(bounds[-1] + ga.shape[-2] // tr)

    def body(w_ref, m_ref, v_ref, *rest):
        g_refs, (go_ref, d_ref, mo_ref, vo_ref) = rest[:len(glist)], rest[len(glist):]
        if parts:
            sums = []
            for g_ref in g_refs:
                gv = g_ref[0].astype(F32)
                for s in range(1, N_DEV):
                    gv = gv + g_ref[s].astype(F32)
                sums.append(gv)
            gv = sums[-1]
            for p in range(len(sums) - 2, -1, -1):
                gv = jnp.where(pl.program_id(0) < bounds[p + 1], sums[p], gv)
        else:
            gv = g_refs[0][...]
        wv = w_ref[...]
        mn = ADAM_B1 * m_ref[...] + (1.0 - ADAM_B1) * gv
        vn = ADAM_B2 * v_ref[...] + (1.0 - ADAM_B2) * (gv * gv)
        m_hat = mn / bc1
        v_hat = vn / bc2
        go_ref[...] = gv
        d_ref[...] = -ADAM_LR * (m_hat / (jnp.sqrt(v_hat) + ADAM_EPS) + ADAM_WD * wv)
        mo_ref[...] = mn
        vo_ref[...] = vn

    flat = pl.BlockSpec((tr, cdim), lambda i: (i, 0))
    spec = pl.BlockSpec((None, tr, cdim), lambda i: (0, i, 0)) if lead else flat
    def piece_spec(p):
        lo, n = bounds[p], bounds[p + 1] - bounds[p]
        return pl.BlockSpec((N_DEV, tr, cdim), lambda i: (0, jnp.clip(i - lo, 0, n - 1), 0))

    gspecs = [piece_spec(p) for p in range(len(glist))] if parts else [flat]
    return pl.pallas_call(
        body, name=name, grid=(r // tr,),
        in_specs=[spec, spec, spec] + gspecs,
        out_specs=[spec] * 4,
        out_shape=[jax.ShapeDtypeStruct(w.shape, F32)] * 4,
        compiler_params=_cp(("arbitrary",)),
    )(w, m, v, *glist)


def _pick_rows(r, pref):
    t = pref
    while r % t:
        t -= 8
    assert t > 0
    return t


def _dev_index(x, y, c):
    return 4 * x + 2 * y + c


class _Comm:
    def __init__(self, kind, arrays):
        self.kind, self.n = kind, len(arrays)
        self.arrays = [a[0] if isinstance(a, tuple) else a for a in arrays]
        self.rows = [(a[1], a[2]) if isinstance(a, tuple) else None for a in arrays]

    def out_shapes(self):
        if self.kind == "gather":
            return [jax.ShapeDtypeStruct((N_DEV,) + a.shape, a.dtype) for a in self.arrays]
        return [jax.ShapeDtypeStruct(a.shape if r is None else (N_DEV, r[1]) + a.shape[2:], a.dtype)
                for a, r in zip(self.arrays, self.rows)]

    def scratch(self):
        return [pltpu.SemaphoreType.DMA((self.n, 7)), pltpu.SemaphoreType.DMA((self.n, 7)),
                pltpu.SemaphoreType.DMA((self.n,))]

    def _gather_parts(self, ins, outs, sems):
        send_sems, recv_sems, local_sems = sems
        x, y, c = lax.axis_index("x"), lax.axis_index("y"), lax.axis_index("c")
        me, sibling = (x, y, c), (x, y, 1 - c)
        chips = [(1 - x, y), (x, 1 - y), (1 - x, 1 - y)]

        def copy(a, k, block, to, src=None):
            slot = outs[a].at[_dev_index(*block)]
            return pltpu.make_async_remote_copy(
                src_ref=slot if src is None else src, dst_ref=slot,
                send_sem=send_sems.at[a, k], recv_sem=recv_sems.at[a, k],
                device_id=to, device_id_type=MESH)

        n = self.n
        mine = [pltpu.make_async_copy(ins[a], outs[a].at[_dev_index(*me)], local_sems.at[a]) for a in range(n)]
        first = []
        for a in range(n):
            first.append(copy(a, 0, me, sibling, src=ins[a]))
            first += [copy(a, 1 + j, me, (*chip, c), src=ins[a]) for j, chip in enumerate(chips)]
        landed = [copy(a, 1 + j, (*chip, c), me) for j, chip in enumerate(chips) for a in range(n)]
        passed = [copy(a, 4 + j, (*chip, c), sibling) for j, chip in enumerate(chips) for a in range(n)]
        late = []
        for a in range(n):
            late.append(copy(a, 0, sibling, me))
            late += [copy(a, 4 + j, (*chip, 1 - c), me) for j, chip in enumerate(chips)]
        return mine, first, landed, passed, late

    def _exchange_parts(self, ins, outs, sems):
        send_sems, recv_sems, local_sems = sems
        x, y, c = lax.axis_index("x"), lax.axis_index("y"), lax.axis_index("c")
        my = _dev_index(x, y, c)
        n = self.n

        def block(a, j):
            r = self.rows[a]
            return ins[a].at[j] if r is None else ins[a].at[j, pl.ds(r[0], r[1])]

        mine = [pltpu.make_async_copy(block(a, my), outs[a].at[my], local_sems.at[a]) for a in range(n)]
        sends, recvs = [], []
        for k in range(1, N_DEV):
            px = (1 - x) if (k >> 2) & 1 else x
            py = (1 - y) if (k >> 1) & 1 else y
            pc = (1 - c) if k & 1 else c
            peer = _dev_index(px, py, pc)
            for a in range(n):
                sends.append(pltpu.make_async_remote_copy(
                    src_ref=block(a, peer), dst_ref=outs[a].at[my],
                    send_sem=send_sems.at[a, k - 1], recv_sem=recv_sems.at[a, k - 1],
                    device_id=(px, py, pc), device_id_type=MESH))
                recvs.append(pltpu.make_async_remote_copy(
                    src_ref=block(a, my), dst_ref=outs[a].at[peer],
                    send_sem=send_sems.at[a, k - 1], recv_sem=recv_sems.at[a, k - 1],
                    device_id=(x, y, c), device_id_type=MESH))
        return mine, sends, recvs

    def start(self, ins, outs, sems):
        if self.kind == "gather":
            mine, first, _, _, _ = self._gather_parts(ins, outs, sems)
        else:
            mine, first, _ = self._exchange_parts(ins, outs, sems)
        for cp in mine + first:
            cp.start()

    def mid(self, ins, outs, sems):
        if self.kind == "gather":
            _, _, landed, passed, _ = self._gather_parts(ins, outs, sems)
            for got, fwd in zip(landed, passed):
                got.wait_recv()
                fwd.start()

    def finish(self, ins, outs, sems):
        if self.kind == "gather":
            mine, first, _, passed, late = self._gather_parts(ins, outs, sems)
            for cp in late:
                cp.wait_recv()
            for cp in first + passed:
                cp.wait_send()
        else:
            mine, sends, recvs = self._exchange_parts(ins, outs, sems)
            for cp in sends:
                cp.wait_send()
            for cp in recvs:
                cp.wait_recv()
        for cp in mine:
            cp.wait()

    def run(self, name):
        n = self.n

        def body(*refs):
            ins, outs, sems = refs[:n], refs[n:2 * n], refs[2 * n:]
            self.start(ins, outs, sems)
            self.mid(ins, outs, sems)
            self.finish(ins, outs, sems)

        return pl.pallas_call(
            body, name=name, in_specs=[ANY] * n, out_specs=[ANY] * n,
            out_shape=self.out_shapes(), scratch_shapes=self.scratch(),
        )(*self.arrays)


def _all_gather(arrays, name):
    return _Comm("gather", arrays).run(name)


def _comm_hooks(comm, refs, n_in, n_out, first, middle, last):
    cn = comm.n if comm is not None else 0
    ins, cins = refs[:n_in], refs[n_in:n_in + cn]
    outs, couts = refs[n_in + cn:n_in + cn + n_out], refs[n_in + cn + n_out:n_in + 2 * cn + n_out]
    rest = refs[n_in + 2 * cn + n_out:]
    scratch, csems = (rest[:len(rest) - 3], rest[len(rest) - 3:]) if cn else (rest, ())

    def begin():
        if cn:
            pl.when(first)(lambda: comm.start(cins, couts, csems))
            pl.when(middle)(lambda: comm.mid(cins, couts, csems))

    def end():
        if cn:
            pl.when(last)(lambda: comm.finish(cins, couts, csems))

    return ins, outs, scratch, begin, end


def _local_step(x, tgt, mod, n1, n2, n3, n4, w_in_p, lb_logits, hg_norm, conv_w, alog, dtb, gdn_norm,
                late_w, dist=None):
    t, d = x.shape
    nh = d // 2 // HD
    ab_blk = 8 * nh
    sh_m, sc_m, gt_m, sh_f, sc_f, gt_f = [mod[i:i + 1] for i in range(6)]

    h1, r1 = _prenorm(x, n1, sc_m, sh_m, "prenorm_mix")
    if dist is None:
        proj = _mm(h1, w_in_p, "nn", [F32], "mm_proj")
        o_hg, a_sv, hst_sv = _hgrn2_fwd(proj, lb_logits, nh, "hgrn2_fwd")
        qkv = _gdn_prep(proj, conv_w, 4 * nh, nh, "gdn_prep")
        o_gd, x_sv, gst_sv = _gdn_fwd(qkv, proj, ab_blk, alog, dtb, nh, "gdn_fwd")
        w_out, w_ff1, w_ff2 = late_w
        exch = lambda arrays: None
    else:
        proj, g_ff2 = _mm(h1, w_in_p, "nn", [F32], "mm_proj", comm=_Comm("gather", late_w[2:]))
        o_hg, a_sv, hst_sv, g_out = _hgrn2_fwd(proj, lb_logits, nh, "hgrn2_fwd",
                                               comm=_Comm("gather", late_w[:1]))
        qkv = _gdn_prep(proj, conv_w, 4 * nh, nh, "gdn_prep")
        o_gd, x_sv, gst_sv, g_ff1 = _gdn_fwd(qkv, proj, ab_blk, alog, dtb, nh, "gdn_fwd",
                                             comm=_Comm("gather", late_w[1:2]))
        w_out, w_ff1, w_ff2 = dist["assemble"](g_out, g_ff1, g_ff2)
        exch = lambda arrays: _Comm("exchange", arrays)
    om_hg = _headnorm_fwd(o_hg, proj, 3 * nh, hg_norm, "headnorm_hg")
    om_gd = _headnorm_fwd(o_gd, proj, 7 * nh, gdn_norm, "headnorm_gdn")
    om = jnp.concatenate([om_hg, om_gd], axis=1)
    y1 = _mm(om, w_out, "nn", [F32], "mm_out")
    x1, r2, h2, r3 = _postnorm_prenorm(x, y1, n2, gt_m, n3, sc_f, sh_f, "postnorm_mix_prenorm_ffn")

    def relu2(acc, extra, outs):
        rl = jnp.maximum(acc, 0.0)
        outs[0][...] = (rl * rl).astype(BF16)

    act = _mm(h2, w_ff1, "nn", [BF16], "mm_ff1", epilogue=relu2)
    y2 = _mm(act, w_ff2, "nn", [F32], "mm_ff2")
    dout, dy2, loss, dgt_f, dn4 = _final_loss_bwd(x1, y2, n4, gt_f, tgt, "final_loss_bwd")
    whole_t = dict(tk=t, tn=1024)
    dw_ff2 = _mm(act, dy2, "tn", [BF16], "mm_dw_ff2", **whole_t)

    def drelu2(acc, extra, outs):
        outs[0][...] = (acc * (2.0 * jnp.sqrt(extra[0][...].astype(F32)))).astype(BF16)

    recv = {}
    ff2a, ff2b = dist["parts_ff2"](dw_ff2) if dist else (None, None)
    du, *recv["ff2a"] = _listed(_mm(dy2, w_ff2, "nt", [BF16], "mm_da", epilogue=drelu2, extras=(act,),
                                    comm=exch([ff2a])))
    ff1_cols = dict(by_cols=True, tk=t, tn=dist["n_ff"]) if dist else whole_t
    dw_ff1, *recv["ff2b"] = _listed(_mm(h2, du, "tn", [BF16], "mm_dw_ff1", comm=exch([ff2b]), **ff1_cols))
    ff1a, ff1b = dist["parts_ff1"](dw_ff1) if dist else (None, None)
    dh2, *recv["ff1a"] = _listed(_mm(du, w_ff1, "nt", [F32], "mm_dh2", comm=exch([ff1a])))
    dx1, dy1, dsh_f, dsc_f, dn3, dgt_m, dn2 = _prenorm_postnorm_bwd(
        dh2, x1, r3, n3, sc_f, dout, y1, r2, n2, gt_m, "prenorm_ffn_postnorm_mix_bwd")

    dw_out = _mm(om, dy1, "tn", [BF16], "mm_dw_out", **whole_t)
    dom = _mm(dy1, w_out, "nt", [F32], "mm_dom")
    do_hg, dg_hg, dhgn = _headnorm_bwd(dom, 0, o_hg, proj, 3 * nh, hg_norm, "headnorm_hg_bwd")
    do_gd, dg_gd, dgdn = _headnorm_bwd(dom, 1, o_gd, proj, 7 * nh, gdn_norm, "headnorm_gdn_bwd")
    p_out = dist["parts_out"](dw_out) if dist else None
    dq_hg, df_hg, di_hg, dl0, *recv["ff1b_out"] = _hgrn2_bwd(proj, lb_logits, do_hg, a_sv, hst_sv, nh,
                                                             "hgrn2_bwd", comm=exch([ff1b, p_out]))
    dq_g, dk_g, dv_g, dab, dpar = _gdn_bwd(qkv, proj, ab_blk, alog, dtb, do_gd, x_sv, gst_sv, nh, "gdn_bwd")
    du_conv, dconv = _gdn_prep_bwd(proj, conv_w, dq_g, dk_g, dv_g, 4 * nh, nh, "gdn_prep_bwd")
    dproj = jnp.concatenate([dq_hg, df_hg, di_hg, dg_hg, du_conv, dg_gd, dab.astype(BF16)], axis=1)
    if dist is None:
        dw_in = _mm(h1, dproj, "tn", [BF16], "mm_dw_in")
        dh1 = _mm(dproj, w_in_p, "nt", [F32], "mm_dh1", tk=1664)
    else:
        q4 = d // 4
        dw_in_a = _mm(h1[:, :q4], dproj, "tn", [BF16], "mm_dw_in_a")
        dw_in_b, in_a = _mm(h1[:, q4:2 * q4], dproj, "tn", [BF16], "mm_dw_in_b",
                            comm=exch([dist["parts_in"](dw_in_a)]))
        dw_in_c, in_b = _mm(h1[:, 2 * q4:], dproj, "tn", [BF16], "mm_dw_in_c",
                            comm=exch([dist["parts_in"](dw_in_b)]))
        dh1, in_c = _mm(dproj, w_in_p, "nt", [F32], "mm_dh1", tk=1664, comm=exch([dist["parts_in"](dw_in_c)]))
        recv["in"] = [in_a, in_b, in_c]
        dw_in = None
    dx, dsh_m, dsc_m, dn1 = _prenorm_bwd(dh1, x, r1, n1, sc_m, dx1, "prenorm_mix_bwd")

    dmod = jnp.concatenate([dsh_m, dsc_m, dgt_m, dsh_f, dsc_f, dgt_f], axis=0)
    grads = dict(dmod=dmod, n1=dn1, n2=dn2, n3=dn3, n4=dn4, w_in=dw_in, lb0=dl0, hg_norm=dhgn, conv=dconv,
                 alog=dpar[0:1], dtb=dpar[1:2], gdn_norm=dgdn, w_out=dw_out, w_ff1=dw_ff1, w_ff2=dw_ff2,
                 recv=recv)
    return loss, dx, grads


def _pack(vals):
    rows = []
    for vv in vals:
        flat = vv.reshape(-1)
        flat = jnp.pad(flat, (0, (-flat.shape[0]) % (SUBLANES * LANES)))
        rows.append(flat.reshape(-1, LANES))
    return jnp.concatenate(rows, axis=0)


def _unpack(packed, shapes):
    out, r = [], 0
    for shp in shapes:
        size = 1
        for s in shp:
            size *= s
        nr = -(-size // (SUBLANES * LANES)) * SUBLANES
        out.append(packed[r:r + nr].reshape(-1)[:size].reshape(shp))
        r += nr
    return out


def _sum_parts(parts, name):
    _, r, cdim = parts.shape

    def body(p_ref, o_ref):
        acc = p_ref[0]
        for s in range(1, N_DEV):
            acc = acc + p_ref[s]
        o_ref[...] = acc

    return pl.pallas_call(
        body, name=name,
        out_shape=jax.ShapeDtypeStruct((r, cdim), F32),
        compiler_params=_cp(),
    )(parts)


def kernel(x, c, w_ada, b_ada, pre_mix_norm, post_mix_norm, pre_ffn_norm, post_ffn_norm, w_in, hg_lb_logits, hg_norm, gdn_conv_w, gdn_a_log, gdn_dt_bias, gdn_norm, w_out, w_ff1, w_ff2, loss_target, m_w_ada, m_b_ada, m_pre_mix_norm, m_post_mix_norm, m_pre_ffn_norm, m_post_ffn_norm, m_w_in, m_hg_lb_logits, m_hg_norm, m_gdn_conv_w, m_gdn_a_log, m_gdn_dt_bias, m_gdn_norm, m_w_out, m_w_ff1, m_w_ff2, v_w_ada, v_b_ada, v_pre_mix_norm, v_post_mix_norm, v_pre_ffn_norm, v_post_ffn_norm, v_w_in, v_hg_lb_logits, v_hg_norm, v_gdn_conv_w, v_gdn_a_log, v_gdn_dt_bias, v_gdn_norm, v_w_out, v_w_ff1, v_w_ff2):
    t, d = x.shape[1], x.shape[2]
    nh = d // 2 // HD
    in_cols = w_in.shape[2] * N_DEV
    main = in_cols - 2 * nh
    me = _dev_index(lax.axis_index("x"), lax.axis_index("y"), lax.axis_index("c"))

    c_all, conv_g = _all_gather([c, gdn_conv_w[0]], "gather_small")
    c_all = c_all.reshape(N_DEV, d)
    conv_full = conv_g.transpose(1, 0, 2).reshape(CONV_K, -1)
    w_in_g = _all_gather([w_in[0].astype(BF16)], "gather_w_in")[0]
    w_in_full = w_in_g.transpose(1, 0, 2).reshape(d, in_cols)
    w_in_p = jnp.concatenate([w_in_full, jnp.zeros((d, LANES - 2 * nh), BF16)], axis=1)
    late_w = [w_out[0].astype(BF16), w_ff1[0].astype(BF16), w_ff2[0].astype(BF16)]

    n_in = w_in.shape[2]
    n_ff = w_ff1.shape[2]

    def halves(p):
        r = p.shape[1] // 2
        return (p, 0, r), (p, r, r)

    dist = dict(
        assemble=lambda g_out, g_ff1, g_ff2: (g_out.reshape(d, d), g_ff1.transpose(1, 0, 2).reshape(d, -1),
                                              g_ff2.reshape(-1, d)),
        n_ff=n_ff,
        parts_ff2=lambda dw: halves(dw.reshape(N_DEV, -1, d)),
        parts_ff1=halves,
        parts_out=lambda dw: dw.reshape(N_DEV, d // N_DEV, d),
        parts_in=lambda dw: dw[:, :in_cols].reshape(dw.shape[0], N_DEV, n_in).transpose(1, 0, 2),
    )

    n_ada = w_ada.shape[2]
    b_loc = lax.dynamic_slice(b_ada, (0, me * n_ada), (1, n_ada))
    mod_part = _ada_fwd(c_all, w_ada[0], b_loc, "ada_fwd")
    mod_all = _all_gather([mod_part], "gather_mod")[0]
    mod = lax.dynamic_slice(mod_all, (0, me, 0), (N_DEV, 1, n_ada)).reshape(6, d)

    pad_lane = lambda vv: jnp.concatenate([vv, jnp.zeros((1, LANES - vv.shape[1]), F32)], axis=1)
    loss, dx, g = _local_step(
        x[0], loss_target[0], mod, pre_mix_norm, post_mix_norm, pre_ffn_norm, post_ffn_norm, w_in_p,
        hg_lb_logits, hg_norm, conv_full, pad_lane(gdn_a_log), pad_lane(gdn_dt_bias), gdn_norm,
        late_w, dist)

    rep_names = ["b_ada", "n1", "n2", "n3", "n4", "lb", "hg_norm", "alog", "dtb", "gdn_norm"]
    rep_w = [b_ada, pre_mix_norm, post_mix_norm, pre_ffn_norm, post_ffn_norm, hg_lb_logits, hg_norm,
             gdn_a_log, gdn_dt_bias, gdn_norm]
    rep_m = [m_b_ada, m_pre_mix_norm, m_post_mix_norm, m_pre_ffn_norm, m_post_ffn_norm, m_hg_lb_logits,
             m_hg_norm, m_gdn_a_log, m_gdn_dt_bias, m_gdn_norm]
    rep_v = [v_b_ada, v_pre_mix_norm, v_post_mix_norm, v_pre_ffn_norm, v_post_ffn_norm, v_hg_lb_logits,
             v_hg_norm, v_gdn_a_log, v_gdn_dt_bias, v_gdn_norm]
    rep_shapes = [a.shape for a in rep_w]
    g_lb = jnp.stack([g["lb0"], -g["lb0"]], axis=0)
    rep_g = [g["dmod"], g["n1"], g["n2"], g["n3"], g["n4"], g_lb, g["hg_norm"],
             g["alog"][:, :nh], g["dtb"][:, :nh], g["gdn_norm"]]
    small = _pack(rep_g + [g["conv"]])
    n_rep_rows = _pack(rep_g).shape[0]
    pad_rows = (-small.shape[0]) % 8
    if pad_rows:
        small = jnp.concatenate([small, jnp.zeros((pad_rows, LANES), F32)], axis=0)
    small_all = _all_gather([small], "gather_small_grads")[0]
    small_sum = _sum_parts(small_all, "sum_small_grads")
    rep_out = _adamw(_pack(rep_w), _pack(rep_m), _pack(rep_v), small_sum[:n_rep_rows], "adamw_small")
    rep_g_o, rep_d_o, rep_m_o, rep_v_o = [dict(zip(rep_names, _unpack(p, rep_shapes))) for p in rep_out]

    conv_sum = small_sum[n_rep_rows:n_rep_rows + CONV_K * conv_full.shape[1] // LANES].reshape(CONV_K, -1)
    n_conv = gdn_conv_w.shape[2]
    conv_loc = lax.dynamic_slice(conv_sum, (0, me * n_conv), (CONV_K, n_conv))
    conv_o = _adamw(gdn_conv_w, m_gdn_conv_w, v_gdn_conv_w, conv_loc, "adamw_conv")

    dmod_all = small_all[:, :6 * d // LANES, :].reshape(N_DEV, 6 * d)
    dmod_loc = lax.dynamic_slice(dmod_all, (0, me * n_ada), (N_DEV, n_ada))
    g_ada = _ada_wgrad(c_all, dmod_loc, "ada_wgrad")
    ada_o = _adamw(w_ada, m_w_ada, v_w_ada, g_ada, "adamw_ada")

    rc = g["recv"]
    r_ff2 = [rc["ff2a"][0], rc["ff2b"][0]]
    r_ff1 = [rc["ff1a"][0], rc["ff1b_out"][0]]
    r_out, r_in = rc["ff1b_out"][1], rc["in"]
    in_o = _adamw(w_in, m_w_in, v_w_in, r_in, "adamw_w_in", parts=True)
    out_o = _adamw(w_out, m_w_out, v_w_out, r_out, "adamw_w_out", parts=True)
    ff1_o = _adamw(w_ff1, m_w_ff1, v_w_ff1, r_ff1, "adamw_w_ff1", parts=True)
    ff2_o = _adamw(w_ff2, m_w_ff2, v_w_ff2, r_ff2, "adamw_w_ff2", parts=True)

    loss_tot = lax.psum(loss[0, 0], ("x", "y", "c"))

    def leaf(kind):
        return [ada_o[kind], rep_out_d[kind]["b_ada"], rep_out_d[kind]["n1"], rep_out_d[kind]["n2"],
                rep_out_d[kind]["n3"], rep_out_d[kind]["n4"], in_o[kind], rep_out_d[kind]["lb"],
                rep_out_d[kind]["hg_norm"], conv_o[kind], rep_out_d[kind]["alog"], rep_out_d[kind]["dtb"],
                rep_out_d[kind]["gdn_norm"], out_o[kind], ff1_o[kind], ff2_o[kind]]

    rep_out_d = [rep_g_o, rep_d_o, rep_m_o, rep_v_o]
    return (loss_tot, dx[None], *leaf(0), *leaf(1), *leaf(2), *leaf(3))
```

```python
import functools

import jax
import jax.numpy as jnp
from jax import lax
from jax.experimental import pallas as pl
from jax.experimental.pallas import tpu as pltpu

F32 = jnp.float32
BF16 = jnp.bfloat16
HI = lax.Precision.HIGHEST
HIGH = lax.Precision.HIGH

EPS = 1e-6
CHUNK = 64
SB = 16
NSB = CHUNK // SB
HP = 8
HD = 128
CONV_K = 4
N_DEV = 8
LANES = 128
SUBLANES = 8
VMEM_LIMIT = 56 * 1024 * 1024
MM_FULL_K = 2048

ADAM_BLOCK_ELEMS = 256 * 1024
ADAM_LR = 0.001
ADAM_B1 = 0.9
ADAM_B2 = 0.999
ADAM_EPS = 1e-08
ADAM_WD = 0.01
ADAM_STEP = 10

ANY = pl.BlockSpec(memory_space=pl.ANY)
MESH = pl.DeviceIdType.MESH


def _cp(sem=None):
    return pltpu.CompilerParams(dimension_semantics=sem, vmem_limit_bytes=VMEM_LIMIT)


def _dot(a, b, dims, precision=None):
    return lax.dot_general(a, b, (dims, ((), ())), precision=precision, preferred_element_type=F32)


def _nn(a, b, precision=None):
    return _dot(a, b, ((1,), (0,)), precision)


def _nt(a, b, precision=None):
    return _dot(a, b, ((1,), (1,)), precision)


def _tn(a, b, precision=None):
    return _dot(a, b, ((0,), (0,)), precision)


def _bf(x):
    return x.astype(BF16)


def _sigmoid(x):
    return 1.0 / (1.0 + jnp.exp(-x))


def _interleave(gens):
    results = [None] * len(gens)
    live = list(range(len(gens)))
    while live:
        for i in list(live):
            try:
                next(gens[i])
            except StopIteration as stop:
                results[i] = stop.value
                live.remove(i)
    return results


def _listed(res):
    return list(res) if isinstance(res, (list, tuple)) else [res]


def _pick(n, pref):
    if n <= pref:
        return n
    t = pref
    while n % t:
        t -= LANES
    assert t > 0, (n, pref)
    return t


def _mm(a, b, mode, out_dtypes, name, epilogue=None, extras=(), tm=1024, tn=2048, tk=1024, comm=None,
        by_cols=False):
    if mode == "nn":
        (m, kd), (_, n) = a.shape, b.shape
    elif mode == "nt":
        (m, kd), (n, _) = a.shape, b.shape
    else:
        (kd, m), (_, n) = a.shape, b.shape
    if kd <= MM_FULL_K:
        tk = kd
    tm, tn, tk = _pick(m, tm), _pick(n, tn), _pick(kd, tk)
    nk = kd // tk
    if mode == "nn":
        a_spec = pl.BlockSpec((tm, tk), lambda i, j, k: (i, k))
        b_spec = pl.BlockSpec((tk, tn), lambda i, j, k: (k, j))
        dims = ((1,), (0,))
    elif mode == "nt":
        a_spec = pl.BlockSpec((tm, tk), lambda i, j, k: (i, k))
        b_spec = pl.BlockSpec((tn, tk), lambda i, j, k: (j, k))
        dims = ((1,), (1,))
    else:
        a_spec = pl.BlockSpec((tk, tm), lambda i, j, k: (k, i))
        b_spec = pl.BlockSpec((tk, tn), lambda i, j, k: (k, j))
        dims = ((0,), (0,))
    o_spec = pl.BlockSpec((tm, tn), lambda i, j, k: (i, j))
    if by_cols:
        assert epilogue is None and not extras
        res_spec = pl.BlockSpec((None, tm, tn), lambda i, j, k: (j, i, 0))
        res_shape = (n // tn, m, tn)
    else:
        res_spec, res_shape = o_spec, (m, n)
    n_extra, n_out = len(extras), len(out_dtypes)

    gm, gn = m // tm, n // tn
    cn = comm.n if comm is not None else 0

    def body(*refs):
        i, j, k = pl.program_id(0), pl.program_id(1), pl.program_id(2)
        at0 = (j == 0) & (k == 0)
        ins, out_refs, scratch, comm_begin, comm_end = _comm_hooks(
            comm, refs, 2 + n_extra, n_out, (i == 0) & at0, (i == gm - 1) & at0,
            (i == gm - 1) & (j == gn - 1) & (k == nk - 1))
        a_ref, b_ref, extra_refs = ins[0], ins[1], ins[2:]
        comm_begin()
        if nk == 1:
            part = _dot(a_ref[...], b_ref[...], dims)
            if epilogue is None:
                out_refs[0][...] = part.astype(out_dtypes[0])
            else:
                epilogue(part, extra_refs, out_refs)
        else:
            acc, = scratch

            @pl.when(k == 0)
            def _():
                acc[...] = jnp.zeros_like(acc)

            acc[...] += _dot(a_ref[...], b_ref[...], dims)

            @pl.when(k == nk - 1)
            def _():
                if epilogue is None:
                    out_refs[0][...] = acc[...].astype(out_dtypes[0])
                else:
                    epilogue(acc[...], extra_refs, out_refs)

        comm_end()

    acc_scratch = [] if nk == 1 else [pltpu.VMEM((tm, tn), F32)]
    sem = ("arbitrary",) * 3 if cn else ("parallel", "parallel", "arbitrary")
    outs = pl.pallas_call(
        body, name=name,
        grid=(gm, gn, nk),
        in_specs=[a_spec, b_spec] + [o_spec] * n_extra + [ANY] * cn,
        out_specs=[res_spec] * n_out + [ANY] * cn,
        out_shape=[jax.ShapeDtypeStruct(res_shape, dt) for dt in out_dtypes] + (comm.out_shapes() if cn else []),
        scratch_shapes=acc_scratch + (comm.scratch() if cn else []),
        compiler_params=_cp(sem),
    )(a, b, *extras, *(comm.arrays if cn else []))
    return outs[0] if n_out + cn == 1 else outs


def _row_spec(tb, d):
    return pl.BlockSpec((tb, d), lambda i: (i, 0))


def _vec_spec(d):
    return pl.BlockSpec((1, d), lambda i: (0, 0))


def _prenorm(x, w, sc, sh, name):
    t, d = x.shape
    tb = _pick(t, 256)

    def body(x_ref, w_ref, sc_ref, sh_ref, h_ref, r_ref):
        xv = x_ref[...]
        r = lax.rsqrt(jnp.mean(xv * xv, axis=-1, keepdims=True) + EPS)
        h_ref[...] = ((xv * r * w_ref[...]) * (1.0 + sc_ref[...]) + sh_ref[...]).astype(BF16)
        r_ref[...] = r

    return pl.pallas_call(
        body, name=name, grid=(t // tb,),
        in_specs=[_row_spec(tb, d), _vec_spec(d), _vec_spec(d), _vec_spec(d)],
        out_specs=[_row_spec(tb, d), _row_spec(tb, 1)],
        out_shape=[jax.ShapeDtypeStruct((t, d), BF16), jax.ShapeDtypeStruct((t, 1), F32)],
        compiler_params=_cp(("parallel",)),
    )(x, w, sc, sh)


def _final_loss_bwd(x, y, w, gt, tgt, name):
    t, d = x.shape
    tb = _pick(t, 256)

    def body(x_ref, y_ref, w_ref, gt_ref, tgt_ref, dout_ref, dy_ref, loss_ref, dgt_ref, dw_ref):
        @pl.when(pl.program_id(0) == 0)
        def _():
            loss_ref[...] = jnp.zeros_like(loss_ref)
            dgt_ref[...] = jnp.zeros_like(dgt_ref)
            dw_ref[...] = jnp.zeros_like(dw_ref)

        yv, wv, gtv = y_ref[...], w_ref[...], gt_ref[...]
        r = lax.rsqrt(jnp.mean(yv * yv, axis=-1, keepdims=True) + EPS)
        z = yv * r
        nz = z * wv
        diff = (x_ref[...] + gtv * nz) - tgt_ref[...]
        loss_ref[...] += 0.5 * jnp.sum(jnp.mean(diff * diff, axis=-1, keepdims=True), axis=0, keepdims=True)
        dxv = diff * (1.0 / d)
        dout_ref[...] = dxv
        dgt_ref[...] += jnp.sum(dxv * nz, axis=0, keepdims=True)
        dn = dxv * gtv
        dw_ref[...] += jnp.sum(dn * z, axis=0, keepdims=True)
        dz = dn * wv
        dy_ref[...] = (r * (dz - z * jnp.mean(dz * z, axis=-1, keepdims=True))).astype(BF16)

    return pl.pallas_call(
        body, name=name, grid=(t // tb,),
        in_specs=[_row_spec(tb, d), _row_spec(tb, d), _vec_spec(d), _vec_spec(d), _row_spec(tb, d)],
        out_specs=[_row_spec(tb, d), _row_spec(tb, d), pl.BlockSpec((1, 1), lambda i: (0, 0)),
                   _vec_spec(d), _vec_spec(d)],
        out_shape=[jax.ShapeDtypeStruct((t, d), F32), jax.ShapeDtypeStruct((t, d), BF16),
                   jax.ShapeDtypeStruct((1, 1), F32), jax.ShapeDtypeStruct((1, d), F32),
                   jax.ShapeDtypeStruct((1, d), F32)],
        compiler_params=_cp(("arbitrary",)),
    )(x, y, w, gt, tgt)


def _postnorm_prenorm(x, y, w_post, gt, w_pre, sc, sh, name):
    t, d = x.shape
    tb = _pick(t, 256)

    def body(x_ref, y_ref, wp_ref, gt_ref, wn_ref, sc_ref, sh_ref, x1_ref, r_ref, h_ref, r1_ref):
        yv = y_ref[...]
        r = lax.rsqrt(jnp.mean(yv * yv, axis=-1, keepdims=True) + EPS)
        x1 = x_ref[...] + gt_ref[...] * (yv * r * wp_ref[...])
        r1 = lax.rsqrt(jnp.mean(x1 * x1, axis=-1, keepdims=True) + EPS)
        x1_ref[...] = x1
        r_ref[...] = r
        h_ref[...] = ((x1 * r1 * wn_ref[...]) * (1.0 + sc_ref[...]) + sh_ref[...]).astype(BF16)
        r1_ref[...] = r1

    return pl.pallas_call(
        body, name=name, grid=(t // tb,),
        in_specs=[_row_spec(tb, d), _row_spec(tb, d)] + [_vec_spec(d)] * 5,
        out_specs=[_row_spec(tb, d), _row_spec(tb, 1), _row_spec(tb, d), _row_spec(tb, 1)],
        out_shape=[jax.ShapeDtypeStruct((t, d), F32), jax.ShapeDtypeStruct((t, 1), F32),
                   jax.ShapeDtypeStruct((t, d), BF16), jax.ShapeDtypeStruct((t, 1), F32)],
        compiler_params=_cp(("parallel",)),
    )(x, y, w_post, gt, w_pre, sc, sh)


def _prenorm_postnorm_bwd(dh, x, r_pre, w_pre, sc, dres, y, r_post, w_post, gt, name):
    t, d = x.shape
    tb = _pick(t, 256)

    def body(dh_ref, x_ref, rp_ref, wp_ref, sc_ref, dres_ref, y_ref, rq_ref, wq_ref, gt_ref,
             dx_ref, dy_ref, dsh_ref, dsc_ref, dwp_ref, dgt_ref, dwq_ref):
        @pl.when(pl.program_id(0) == 0)
        def _():
            for ref in (dsh_ref, dsc_ref, dwp_ref, dgt_ref, dwq_ref):
                ref[...] = jnp.zeros_like(ref)

        dhv, rv, wv = dh_ref[...], rp_ref[...], wp_ref[...]
        z = x_ref[...] * rv
        dsh_ref[...] += jnp.sum(dhv, axis=0, keepdims=True)
        dsc_ref[...] += jnp.sum(dhv * (z * wv), axis=0, keepdims=True)
        dzw = dhv * (1.0 + sc_ref[...])
        dwp_ref[...] += jnp.sum(dzw * z, axis=0, keepdims=True)
        dz = dzw * wv
        dxv = dres_ref[...] + rv * (dz - z * jnp.mean(dz * z, axis=-1, keepdims=True))
        dx_ref[...] = dxv

        rq, wq = rq_ref[...], wq_ref[...]
        zq = y_ref[...] * rq
        dgt_ref[...] += jnp.sum(dxv * (zq * wq), axis=0, keepdims=True)
        dn = dxv * gt_ref[...]
        dwq_ref[...] += jnp.sum(dn * zq, axis=0, keepdims=True)
        dzq = dn * wq
        dy_ref[...] = (rq * (dzq - zq * jnp.mean(dzq * zq, axis=-1, keepdims=True))).astype(BF16)

    rs, r1, vs = _row_spec(tb, d), _row_spec(tb, 1), _vec_spec(d)
    return pl.pallas_call(
        body, name=name, grid=(t // tb,),
        in_specs=[rs, rs, r1, vs, vs, rs, rs, r1, vs, vs],
        out_specs=[rs, rs] + [vs] * 5,
        out_shape=[jax.ShapeDtypeStruct((t, d), F32), jax.ShapeDtypeStruct((t, d), BF16)]
        + [jax.ShapeDtypeStruct((1, d), F32)] * 5,
        compiler_params=_cp(("arbitrary",)),
    )(dh, x, r_pre, w_pre, sc, dres, y, r_post, w_post, gt)


def _prenorm_bwd(dh, x, r, w, sc, dres, name):
    t, d = x.shape
    tb = _pick(t, 256)

    def body(dh_ref, x_ref, r_ref, w_ref, sc_ref, dres_ref, dx_ref, dsh_ref, dsc_ref, dw_ref):
        @pl.when(pl.program_id(0) == 0)
        def _():
            dsh_ref[...] = jnp.zeros_like(dsh_ref)
            dsc_ref[...] = jnp.zeros_like(dsc_ref)
            dw_ref[...] = jnp.zeros_like(dw_ref)

        dhv, rv, wv = dh_ref[...], r_ref[...], w_ref[...]
        z = x_ref[...] * rv
        dsh_ref[...] += jnp.sum(dhv, axis=0, keepdims=True)
        dsc_ref[...] += jnp.sum(dhv * (z * wv), axis=0, keepdims=True)
        dzw = dhv * (1.0 + sc_ref[...])
        dw_ref[...] += jnp.sum(dzw * z, axis=0, keepdims=True)
        dz = dzw * wv
        dx_ref[...] = dres_ref[...] + rv * (dz - z * jnp.mean(dz * z, axis=-1, keepdims=True))

    return pl.pallas_call(
        body, name=name, grid=(t // tb,),
        in_specs=[_row_spec(tb, d), _row_spec(tb, d), _row_spec(tb, 1), _vec_spec(d), _vec_spec(d),
                  _row_spec(tb, d)],
        out_specs=[_row_spec(tb, d), _vec_spec(d), _vec_spec(d), _vec_spec(d)],
        out_shape=[jax.ShapeDtypeStruct((t, d), F32)] + [jax.ShapeDtypeStruct((1, d), F32)] * 3,
        compiler_params=_cp(("arbitrary",)),
    )(dh, x, r, w, sc, dres)


def _headnorm_fwd(o, proj, g_blk, nw, name):
    t, wd = o.shape
    nh = wd // HD
    tb = _pick(t, 512)
    gb = g_blk * HD // wd

    def body(o_ref, g_ref, nw_ref, out_ref):
        o3 = o_ref[...].reshape(tb, nh, HD)
        g3 = g_ref[...].reshape(tb, nh, HD)
        rh = lax.rsqrt(jnp.mean(o3 * o3, axis=-1, keepdims=True) + EPS)
        res = (o3 * rh * nw_ref[...].reshape(1, 1, HD)) * (g3 * _sigmoid(g3))
        out_ref[...] = res.reshape(tb, wd).astype(BF16)

    return pl.pallas_call(
        body, name=name, grid=(t // tb,),
        in_specs=[_row_spec(tb, wd), pl.BlockSpec((tb, wd), lambda i: (i, gb)), _vec_spec(HD)],
        out_specs=_row_spec(tb, wd),
        out_shape=jax.ShapeDtypeStruct((t, wd), BF16),
        compiler_params=_cp(("parallel",)),
    )(o, proj, nw)


def _headnorm_bwd(dom, col_blk, o, proj, g_blk, nw, name):
    t, wd = o.shape
    nh = wd // HD
    tb = _pick(t, 512)
    gb = g_blk * HD // wd

    def body(do_ref, o_ref, g_ref, nw_ref, dout_ref, dg_ref, dnw_ref):
        @pl.when(pl.program_id(0) == 0)
        def _():
            dnw_ref[...] = jnp.zeros_like(dnw_ref)

        dn = do_ref[...].reshape(tb, nh, HD)
        o3 = o_ref[...].reshape(tb, nh, HD)
        g3 = g_ref[...].reshape(tb, nh, HD)
        nw3 = nw_ref[...].reshape(1, 1, HD)
        rh = lax.rsqrt(jnp.mean(o3 * o3, axis=-1, keepdims=True) + EPS)
        z = o3 * rh
        sg = _sigmoid(g3)
        sl = g3 * sg
        dnw_ref[...] += jnp.sum(jnp.sum(dn * sl * z, axis=1), axis=0, keepdims=True)
        dg_ref[...] = (dn * (z * nw3) * (sg * (1.0 + g3 * (1.0 - sg)))).reshape(tb, wd).astype(BF16)
        dz = dn * sl * nw3
        dout_ref[...] = (rh * (dz - z * jnp.mean(dz * z, axis=-1, keepdims=True))).reshape(tb, wd)

    return pl.pallas_call(
        body, name=name, grid=(t // tb,),
        in_specs=[pl.BlockSpec((tb, wd), lambda i: (i, col_blk)), _row_spec(tb, wd),
                  pl.BlockSpec((tb, wd), lambda i: (i, gb)), _vec_spec(HD)],
        out_specs=[_row_spec(tb, wd), _row_spec(tb, wd), _vec_spec(HD)],
        out_shape=[jax.ShapeDtypeStruct((t, wd), F32), jax.ShapeDtypeStruct((t, wd), BF16),
                   jax.ShapeDtypeStruct((1, HD), F32)],
        compiler_params=_cp(("arbitrary",)),
    )(dom, o, proj, nw)


def _tri(n, kind):
    r = lax.broadcasted_iota(jnp.int32, (n, n), 0)
    c = lax.broadcasted_iota(jnp.int32, (n, n), 1)
    if kind == "lower":
        return r >= c
    if kind == "strict":
        return r > c
    return r <= c


def _hg_gate(fl, l0, l1):
    mx = jnp.maximum(l0, l1)
    e0, e1 = jnp.exp(l0 - mx), jnp.exp(l1 - mx)
    lb = e0 / (e0 + e1)
    sg = _sigmoid(fl)
    f = lb + (1.0 - lb) * sg
    return lb, sg, f


def _hgrn2_fwd(proj, lb_logits, nh, name, comm=None):
    t = proj.shape[0]
    nc = t // CHUNK
    C = CHUNK
    lg = lb_logits.reshape(2, nh, 1, HD)

    hp = min(HP, nh)
    ng = nh // hp

    def one_head(hh, st, q_ref, f_ref, i_ref, lg_ref, p_sc, r_sc):
        sl = slice(hh * HD, (hh + 1) * HD)
        q, v = q_ref[:, sl], i_ref[:, sl]
        _, _, f = _hg_gate(f_ref[:, sl], lg_ref[0, hh], lg_ref[1, hh])
        k = 1.0 - f
        low = _tri(C, "lower")
        b = _nn(low.astype(F32), jnp.log(f), HI)
        yield
        lane_c = lax.broadcasted_iota(jnp.int32, (SB, C), 1)
        lane_h = lax.broadcasted_iota(jnp.int32, (SB, HD), 1)
        row_h = lax.broadcasted_iota(jnp.int32, (SB, HD), 0)
        ones = jnp.ones((HD, HD), F32)

        for i in range(NSB):
            qi, ki, bi = q[SB * i:SB * (i + 1)], k[SB * i:SB * (i + 1)], b[SB * i:SB * (i + 1)]
            for s in range(SB):
                e = jnp.exp(jnp.minimum(bi - bi[s:s + 1], 0.0))
                p = jnp.where(row_h >= s, qi * ki[s:s + 1] * e, 0.0)
                p_sc[hh, pl.ds((i * SB + s) * SB, SB), :] = p
            yield
        r_sc[hh] = _nn(p_sc[hh], ones, HIGH)
        yield
        a_rows = []
        for i in range(NSB):
            acc = jnp.zeros((SB, HD), F32)
            for s in range(SB):
                acc = jnp.where(lane_h == SB * i + s, r_sc[hh, pl.ds((i * SB + s) * SB, SB), :], acc)
            acc = acc[:, :C]
            if i > 0:
                r = b[SB * i - 1:SB * i]
                bi = b[SB * i:SB * (i + 1)]
                qf = q[SB * i:SB * (i + 1)] * jnp.exp(bi - r)
                kf = k * jnp.exp(jnp.minimum(r - b, 0.0))
                acc = acc + jnp.where(lane_c < SB * i, _nt(qf, kf, HIGH), 0.0)
            a_rows.append(acc)
            yield
        a = jnp.concatenate(a_rows, axis=0)
        bl = b[C - 1:C, :]
        o = _nn(_bf(a), _bf(v)) + _nt(_bf(q * jnp.exp(b)), _bf(st))
        yield
        new_st = st * jnp.exp(bl) + _tn(_bf(v), _bf(k * jnp.exp(bl - b)))
        return o, a, new_st

    def body(*refs):
        c, hg = pl.program_id(0), pl.program_id(1)
        step = c * ng + hg
        ins, outs, scratch, comm_begin, comm_end = _comm_hooks(
            comm, refs, 4, 3, step == 0, step == (3 * nc * ng) // 4, step == nc * ng - 1)
        o_ref, a_ref, st_ref = outs
        s_sc, p_sc, r_sc = scratch
        comm_begin()

        @pl.when(c == 0)
        def _():
            for hh in range(hp):
                s_sc[hg * hp + hh] = jnp.zeros((HD, HD), F32)

        sts = [s_sc[hg * hp + hh] for hh in range(hp)]
        res = _interleave([one_head(hh, sts[hh], *ins, p_sc, r_sc) for hh in range(hp)])
        for hh in range(hp):
            o_ref[:, hh * HD:(hh + 1) * HD] = res[hh][0]
            a_ref[0, hh] = res[hh][1]
            st_ref[0, hh] = sts[hh]
            s_sc[hg * hp + hh] = res[hh][2]
        comm_end()

    blk = lambda off: pl.BlockSpec((C, hp * HD), lambda c, g: (c, off // hp + g))
    cn = comm.n if comm is not None else 0
    return pl.pallas_call(
        body, name=name, grid=(nc, ng),
        in_specs=[blk(0), blk(nh), blk(2 * nh),
                  pl.BlockSpec((2, hp, 1, HD), lambda c, g: (0, g, 0, 0))] + [ANY] * cn,
        out_specs=[blk(0),
                   pl.BlockSpec((1, hp, C, C), lambda c, g: (c, g, 0, 0)),
                   pl.BlockSpec((1, hp, HD, HD), lambda c, g: (c, g, 0, 0))] + [ANY] * cn,
        out_shape=[jax.ShapeDtypeStruct((t, nh * HD), F32),
                   jax.ShapeDtypeStruct((nc, nh, C, C), F32),
                   jax.ShapeDtypeStruct((nc, nh, HD, HD), F32)] + (comm.out_shapes() if cn else []),
        scratch_shapes=[pltpu.VMEM((nh, HD, HD), F32), pltpu.VMEM((hp, C * SB, HD), F32),
                        pltpu.VMEM((hp, C * SB, HD), F32)] + (comm.scratch() if cn else []),
        compiler_params=_cp(("arbitrary", "arbitrary")),
    )(proj, proj, proj, lg, *(comm.arrays if cn else []))


def _hgrn2_bwd(proj, lb_logits, do, a_sv, st_sv, nh, name, comm=None):
    t = proj.shape[0]
    nc = t // CHUNK
    C = CHUNK
    lg = lb_logits.reshape(2, nh, 1, HD)
    hp = min(HP, nh)
    ng = nh // hp

    def one_head(hh, dst, q_ref, f_ref, i_ref, lg_ref, do_ref, a_ref, st_ref, p_sc, r_sc):
        sl = slice(hh * HD, (hh + 1) * HD)
        q, v, do_ = q_ref[:, sl], i_ref[:, sl], do_ref[:, sl]
        lb, sg, f = _hg_gate(f_ref[:, sl], lg_ref[0, hh], lg_ref[1, hh])
        k = 1.0 - f
        low = _tri(C, "lower")
        b = _nn(low.astype(F32), jnp.log(f), HI)
        yield
        bl = b[C - 1:C, :]
        eb, ekb = jnp.exp(b), jnp.exp(bl - b)
        qb, kb = q * eb, k * ekb
        a, st = a_ref[0, hh], st_ref[0, hh]

        da = jnp.where(low, _nt(_bf(do_), _bf(v)), 0.0)
        yield
        dv = _tn(_bf(a), _bf(do_)) + _nt(_bf(kb), _bf(dst))
        yield
        dqb = _nn(_bf(do_), _bf(st))
        dkb = _nn(_bf(v), _bf(dst))
        yield

        row = lax.broadcasted_iota(jnp.int32, (C, HD), 0)
        lane_c = lax.broadcasted_iota(jnp.int32, (SB, C), 1)
        row_h = lax.broadcasted_iota(jnp.int32, (SB, HD), 0)
        ones = jnp.ones((HD, HD), F32)
        sel = (lax.broadcasted_iota(jnp.int32, (C, C * SB), 0)
               == jnp.right_shift(lax.broadcasted_iota(jnp.int32, (C, C * SB), 1), SB.bit_length() - 1)).astype(F32)

        for i in range(NSB):
            doi, vi = do_[SB * i:SB * (i + 1)], v[SB * i:SB * (i + 1)]
            for s in range(SB):
                p_sc[hh, pl.ds((i * SB + s) * SB, SB), :] = doi * vi[s:s + 1]
            yield
        r_sc[hh] = _nn(p_sc[hh], ones, HIGH)
        yield
        dq_rows = []
        dk_off = jnp.zeros((C, HD), F32)
        for i in range(NSB):
            qi, ki, bi = q[SB * i:SB * (i + 1)], k[SB * i:SB * (i + 1)], b[SB * i:SB * (i + 1)]
            acc = jnp.zeros((SB, HD), F32)
            for s in range(SB):
                e = jnp.exp(jnp.minimum(bi - bi[s:s + 1], 0.0))
                g = jnp.where(row_h >= s, r_sc[hh, pl.ds((i * SB + s) * SB, SB), :] * e, 0.0)
                acc = acc + g * ki[s:s + 1]
                p_sc[hh, pl.ds((i * SB + s) * SB, SB), :] = g * qi
            yield
            if i > 0:
                r = b[SB * i - 1:SB * i]
                fq = jnp.exp(bi - r)
                fk = jnp.exp(jnp.minimum(r - b, 0.0))
                dai = jnp.where(lane_c < SB * i, da[SB * i:SB * (i + 1)], 0.0)
                acc = acc + _nn(dai, k * fk, HIGH) * fq
                dk_off = dk_off + _tn(dai, qi * fq, HIGH) * fk
                yield
            dq_rows.append(acc)
        dqi = jnp.concatenate(dq_rows, axis=0)
        dq = dqi + dqb * eb
        dk_inter = dkb * ekb
        dk = _nn(sel, p_sc[hh], HIGH) + dk_off + dk_inter
        yield
        db = q * dq - k * dk
        extra = (jnp.sum(k * dk_inter, axis=0, keepdims=True)
                 + jnp.exp(bl) * jnp.sum(dst * st, axis=0, keepdims=True))
        db = db + jnp.where(row == C - 1, extra, 0.0)
        dlf = _nn(_tri(C, "upper").astype(F32), db, HI)
        yield
        df = dlf / f - dk
        dfl = (df * (1.0 - lb) * sg * (1.0 - sg)).astype(BF16)
        dl = jnp.sum(df * (1.0 - sg), axis=0, keepdims=True) * (lb * (1.0 - lb))
        new_dst = dst * jnp.exp(bl) + _tn(_bf(do_), _bf(qb))
        return dq.astype(BF16), dfl, dv.astype(BF16), dl, new_dst

    def body(*refs):
        c, hg = pl.program_id(0), pl.program_id(1)
        step = c * ng + hg
        ins, outs, scratch, comm_begin, comm_end = _comm_hooks(
            comm, refs, 7, 4, step == 0, step == (3 * nc * ng) // 4, step == nc * ng - 1)
        dq_ref, df_ref, di_ref, dl_ref = outs
        ds_sc, p_sc, r_sc = scratch
        comm_begin()

        @pl.when(c == 0)
        def _():
            for hh in range(hp):
                ds_sc[hg * hp + hh] = jnp.zeros((HD, HD), F32)

        @pl.when(step == 0)
        def _():
            dl_ref[...] = jnp.zeros_like(dl_ref)

        dsts = [ds_sc[hg * hp + hh] for hh in range(hp)]
        res = _interleave([one_head(hh, dsts[hh], *ins, p_sc, r_sc) for hh in range(hp)])
        for hh in range(hp):
            sl = slice(hh * HD, (hh + 1) * HD)
            dq_ref[:, sl], df_ref[:, sl], di_ref[:, sl] = res[hh][0], res[hh][1], res[hh][2]
            dl_ref[pl.ds(hg * hp + hh, 1), :] += res[hh][3]
            ds_sc[hg * hp + hh] = res[hh][4]
        comm_end()

    rblk = lambda off: pl.BlockSpec((C, hp * HD), lambda c, g: (nc - 1 - c, off // hp + g))
    oblk = pl.BlockSpec((C, hp * HD), lambda c, g: (nc - 1 - c, g))
    cn = comm.n if comm is not None else 0
    return pl.pallas_call(
        body, name=name, grid=(nc, ng),
        in_specs=[rblk(0), rblk(nh), rblk(2 * nh),
                  pl.BlockSpec((2, hp, 1, HD), lambda c, g: (0, g, 0, 0)),
                  oblk,
                  pl.BlockSpec((1, hp, C, C), lambda c, g: (nc - 1 - c, g, 0, 0)),
                  pl.BlockSpec((1, hp, HD, HD), lambda c, g: (nc - 1 - c, g, 0, 0))] + [ANY] * cn,
        out_specs=[oblk, oblk, oblk, pl.BlockSpec((nh, HD), lambda c, g: (0, 0))] + [ANY] * cn,
        out_shape=[jax.ShapeDtypeStruct((t, nh * HD), BF16)] * 3 + [jax.ShapeDtypeStruct((nh, HD), F32)]
        + (comm.out_shapes() if cn else []),
        scratch_shapes=[pltpu.VMEM((nh, HD, HD), F32), pltpu.VMEM((hp, C * SB, HD), F32),
                        pltpu.VMEM((hp, C * SB, HD), F32)] + (comm.scratch() if cn else []),
        compiler_params=_cp(("arbitrary", "arbitrary")),
    )(proj, proj, proj, lg, do, a_sv, st_sv, *(comm.arrays if cn else []))


def _shift_rows(u, d, row):
    t = u.shape[0]
    if d == 0:
        return u
    rolled = pltpu.roll(u, d % t, 0)
    if d > 0:
        return jnp.where(row >= d, rolled, 0.0)
    return jnp.where(row < t + d, rolled, 0.0)


def _gdn_prep(proj, conv_w, blk0, nh, name):
    t = proj.shape[0]
    scale = HD ** -0.5

    def body(u_ref, w_ref, o_ref):
        j = pl.program_id(0)
        u, w = u_ref[...], w_ref[...]
        row = lax.broadcasted_iota(jnp.int32, (t, HD), 0)
        y = w[CONV_K - 1:CONV_K, :] * u
        for d in range(1, CONV_K):
            y = y + w[CONV_K - 1 - d:CONV_K - d, :] * _shift_rows(u, d, row)
        a = y * _sigmoid(y)
        n = a * lax.rsqrt(jnp.sum(a * a, axis=-1, keepdims=True) + EPS)
        n = n * jnp.where(j < nh, scale, 1.0)
        o_ref[...] = jnp.where(j < 2 * nh, n, a)

    return pl.pallas_call(
        body, name=name, grid=(3 * nh,),
        in_specs=[pl.BlockSpec((t, HD), lambda j: (0, blk0 + j)), pl.BlockSpec((CONV_K, HD), lambda j: (0, j))],
        out_specs=pl.BlockSpec((t, HD), lambda j: (0, j)),
        out_shape=jax.ShapeDtypeStruct((t, 3 * nh * HD), F32),
        compiler_params=_cp(("parallel",)),
    )(proj, conv_w)


def _gdn_prep_bwd(proj, conv_w, dq, dk, dv, blk0, nh, name):
    t = proj.shape[0]
    scale = HD ** -0.5

    def body(u_ref, w_ref, dq_ref, dk_ref, dv_ref, du_ref, dw_ref):
        j = pl.program_id(0)
        u, w = u_ref[...], w_ref[...]
        dout = jnp.where(j < nh, dq_ref[...], jnp.where(j < 2 * nh, dk_ref[...], dv_ref[...]))
        row = lax.broadcasted_iota(jnp.int32, (t, HD), 0)
        us = [_shift_rows(u, d, row) for d in range(CONV_K)]
        y = w[CONV_K - 1:CONV_K, :] * us[0]
        for d in range(1, CONV_K):
            y = y + w[CONV_K - 1 - d:CONV_K - d, :] * us[d]
        sg = _sigmoid(y)
        a = y * sg
        rs = lax.rsqrt(jnp.sum(a * a, axis=-1, keepdims=True) + EPS)
        n = a * rs
        dn = dout * jnp.where(j < nh, scale, 1.0)
        da_n = rs * (dn - n * jnp.sum(dn * n, axis=-1, keepdims=True))
        da = jnp.where(j < 2 * nh, da_n, dout)
        dy = da * (sg * (1.0 + y * (1.0 - sg)))
        du = w[CONV_K - 1:CONV_K, :] * dy
        for d in range(1, CONV_K):
            du = du + w[CONV_K - 1 - d:CONV_K - d, :] * _shift_rows(dy, -d, row)
        du_ref[...] = du.astype(BF16)
        for d in range(CONV_K):
            dw_ref[CONV_K - 1 - d:CONV_K - d, :] = jnp.sum(dy * us[d], axis=0, keepdims=True)

    return pl.pallas_call(
        body, name=name, grid=(3 * nh,),
        in_specs=[pl.BlockSpec((t, HD), lambda j: (0, blk0 + j)), pl.BlockSpec((CONV_K, HD), lambda j: (0, j))]
        + [pl.BlockSpec((t, HD), functools.partial(lambda p, j: (0, jnp.clip(j - p * nh, 0, nh - 1)), p))
           for p in range(3)],
        out_specs=[pl.BlockSpec((t, HD), lambda j: (0, j)), pl.BlockSpec((CONV_K, HD), lambda j: (0, j))],
        out_shape=[jax.ShapeDtypeStruct((t, 3 * nh * HD), BF16), jax.ShapeDtypeStruct((CONV_K, 3 * nh * HD), F32)],
        compiler_params=_cp(("arbitrary",)),
    )(proj, conv_w, dq, dk, dv)


def _gdn_gates(ab, alog, dtb, h, nh):
    lane = lax.broadcasted_iota(jnp.int32, ab.shape, 1)
    x = ab + dtb
    sp = jnp.maximum(x, 0.0) + jnp.log(1.0 + jnp.exp(-jnp.abs(x)))
    ea = jnp.exp(alog)
    la_all = -ea * sp
    beta_all = _sigmoid(ab)
    pick = lambda val, ln: jnp.sum(jnp.where(lane == ln, val, 0.0), axis=1, keepdims=True)
    la = pick(la_all, h)
    beta = pick(beta_all, nh + h)
    dla_da = pick(-ea * _sigmoid(x), h)
    return la, beta, dla_da


def _unit_lower_inverses(ms, C):
    nb = C // SB
    sh = SB.bit_length() - 1
    rowb = jnp.right_shift(lax.broadcasted_iota(jnp.int32, (C, C), 0), sh)
    colb = jnp.right_shift(lax.broadcasted_iota(jnp.int32, (C, C), 1), sh)
    eye = (lax.broadcasted_iota(jnp.int32, (SB, SB), 0) == lax.broadcasted_iota(jnp.int32, (SB, SB), 1)).astype(F32)
    spread = (jnp.bitwise_and(lax.broadcasted_iota(jnp.int32, (SB, C), 1), SB - 1)
              == lax.broadcasted_iota(jnp.int32, (SB, C), 0)).astype(F32)
    blocks = [[m[SB * i:SB * (i + 1), SB * i:SB * (i + 1)] for i in range(nb)] for m in ms]
    xs = [[eye] * nb for _ in ms]
    for s in range(SB - 1):
        xs = [[x - b[:, s:s + 1] * x[s:s + 1, :] for x, b in zip(xh, bh)] for xh, bh in zip(xs, blocks)]
    ts = [jnp.where(rowb == colb, _nn(jnp.concatenate(xh, axis=0), spread, HIGH), 0.0) for xh in xs]
    lvl = 1
    while (1 << lvl) <= nb:
        off = ((jnp.right_shift(rowb, lvl) == jnp.right_shift(colb, lvl))
               & (jnp.right_shift(rowb, lvl - 1) != jnp.right_shift(colb, lvl - 1)))
        ts = [t - _nn(t, _nn(jnp.where(off, m, 0.0), t, HIGH), HIGH) for t, m in zip(ts, ms)]
        lvl += 1
    return ts


def _gdn_chunks(qs, ks, vs, las, betas, C):
    low, strict = _tri(C, "lower"), _tri(C, "strict")
    eye = (lax.broadcasted_iota(jnp.int32, (C, C), 0) == lax.broadcasted_iota(jnp.int32, (C, C), 1)).astype(F32)
    g_bs = [_nn(low.astype(F32), jnp.broadcast_to(la, (C, HD)), HI) for la in las]
    ps = [_nt(k, k, HIGH) for k in ks]
    qks = [_nt(_bf(q), _bf(k)) for q, k in zip(qs, ks)]
    chs = []
    for g_b, p, qk_raw, beta in zip(g_bs, ps, qks, betas):
        g_c = g_b[:, :C]
        gamma = jnp.where(low, jnp.exp(jnp.minimum(g_c - g_c.T, 0.0)), 0.0)
        gl = g_b[C - 1:C, :]
        chs.append(dict(gamma=gamma, eg=jnp.exp(g_b), gl=gl, ekt=jnp.exp(gl - g_b), p=p,
                        m=jnp.where(strict, beta * p * gamma, 0.0), qk_raw=qk_raw))
    xs = _unit_lower_inverses([ch["m"] for ch in chs], C)
    r_ws = [k * (beta * ch["eg"]) for ch, k, beta in zip(chs, ks, betas)]
    uws = [_nn(x, jnp.concatenate([v * beta, r_w], axis=1), HIGH) for x, v, beta, r_w in zip(xs, vs, betas, r_ws)]
    for ch, x, r_w, uw in zip(chs, xs, r_ws, uws):
        ch.update(x=x, r_w=r_w, uw=uw)
    return chs


def _gdn_fwd(qkv, proj, ab_blk, alog, dtb, nh, name, comm=None):
    t = qkv.shape[0]
    nc = t // CHUNK
    C = CHUNK
    hp = min(HP, nh)
    ng = nh // hp

    def body(*refs):
        c, hg = pl.program_id(0), pl.program_id(1)
        step = c * ng + hg
        ins, outs, scratch, comm_begin, comm_end = _comm_hooks(
            comm, refs, 6, 3, step == 0, step == (3 * nc * ng) // 4, step == nc * ng - 1)
        q_ref, k_ref, v_ref, ab_ref, al_ref, dt_ref = ins
        o_ref, x_ref, st_ref = outs
        s_sc, = scratch
        comm_begin()

        @pl.when(c == 0)
        def _():
            for hh in range(hp):
                s_sc[hg * hp + hh] = jnp.zeros((HD, HD), F32)

        sls = [slice(hh * HD, (hh + 1) * HD) for hh in range(hp)]
        qs, ks, vs = [q_ref[:, sl] for sl in sls], [k_ref[:, sl] for sl in sls], [v_ref[:, sl] for sl in sls]
        sts = [s_sc[hg * hp + hh] for hh in range(hp)]
        gates = [_gdn_gates(ab_ref[...], al_ref[...], dt_ref[...], hg * hp + hh, nh) for hh in range(hp)]
        chs = _gdn_chunks(qs, ks, vs, [g[0] for g in gates], [g[1] for g in gates], C)
        stbs = [_bf(st) for st in sts]
        vns = [ch["uw"][:, :HD] - _nt(_bf(ch["uw"][:, HD:]), stb) for ch, stb in zip(chs, stbs)]
        o_st = [_nt(_bf(q * ch["eg"]), stb) for q, ch, stb in zip(qs, chs, stbs)]
        outs_ = [o + _nn(_bf(ch["qk_raw"] * ch["gamma"]), _bf(vn)) for o, ch, vn in zip(o_st, chs, vns)]
        new_sts = [st * jnp.exp(ch["gl"]) + _tn(_bf(vn), _bf(k * ch["ekt"]))
                   for st, ch, vn, k in zip(sts, chs, vns, ks)]
        for hh in range(hp):
            o_ref[:, sls[hh]] = outs_[hh]
            x_ref[0, hh] = chs[hh]["x"]
            st_ref[0, hh] = sts[hh]
            s_sc[hg * hp + hh] = new_sts[hh]
        comm_end()

    blk = lambda off: pl.BlockSpec((C, hp * HD), lambda c, g: (c, off // hp + g))
    vec = pl.BlockSpec((1, HD), lambda c, g: (0, 0))
    cn = comm.n if comm is not None else 0
    return pl.pallas_call(
        body, name=name, grid=(nc, ng),
        in_specs=[blk(0), blk(nh), blk(2 * nh), pl.BlockSpec((C, HD), lambda c, g: (c, ab_blk)), vec, vec]
        + [ANY] * cn,
        out_specs=[blk(0),
                   pl.BlockSpec((1, hp,C, C), lambda c, g: (c, g, 0, 0)),
                   pl.BlockSpec((1, hp,HD, HD), lambda c, g: (c, g, 0, 0))] + [ANY] * cn,
        out_shape=[jax.ShapeDtypeStruct((t, nh * HD), F32),
                   jax.ShapeDtypeStruct((nc, nh, C, C), F32),
                   jax.ShapeDtypeStruct((nc, nh, HD, HD), F32)] + (comm.out_shapes() if cn else []),
        scratch_shapes=[pltpu.VMEM((nh, HD, HD), F32)] + (comm.scratch() if cn else []),
        compiler_params=_cp(("arbitrary", "arbitrary")),
    )(qkv, qkv, qkv, proj, alog, dtb, *(comm.arrays if cn else []))


def _gdn_bwd(qkv, proj, ab_blk, alog, dtb, do, x_sv, st_sv, nh, name, comm=None):
    t = qkv.shape[0]
    nc = t // CHUNK
    C = CHUNK
    hp = min(HP, nh)
    ng = nh // hp

    def one_head(h, hh, dst, q_ref, k_ref, v_ref, ab_ref, al_ref, dt_ref, do_ref, x_ref, st_ref):
        sl = slice(hh * HD, (hh + 1) * HD)
        q, k, v, do_ = q_ref[:, sl], k_ref[:, sl], v_ref[:, sl], do_ref[:, sl]
        la, beta, dla_da = _gdn_gates(ab_ref[...], al_ref[...], dt_ref[...], h, nh)
        low, strict = _tri(C, "lower"), _tri(C, "strict")
        g_b = _nn(low.astype(F32), jnp.broadcast_to(la, (C, HD)), HI)
        yield
        g_c = g_b[:, :C]
        gamma = jnp.where(low, jnp.exp(jnp.minimum(g_c - g_c.T, 0.0)), 0.0)
        eg = jnp.exp(g_b)
        gl = g_b[C - 1:C, :]
        ekt = jnp.exp(gl - g_b)
        egl = jnp.exp(gl)
        p = _nt(k, k, HIGH)
        yield
        x = x_ref[0, hh]
        r_w = k * (beta * eg)
        rhs = jnp.concatenate([v * beta, r_w], axis=1)
        uw = _nn(x, rhs, HIGH)
        yield
        u, w = uw[:, :HD], uw[:, HD:]
        qk_raw = _nt(_bf(q), _bf(k))
        yield
        qk = qk_raw * gamma
        st = st_ref[0, hh]
        stb, dstb = _bf(st), _bf(dst)
        vn = u - _nt(_bf(w), stb)
        yield
        qd, kt = q * eg, k * ekt

        dvn = _tn(_bf(qk), _bf(do_)) + _nt(_bf(kt), dstb)
        yield
        dq2 = jnp.where(low, _nt(_bf(do_), _bf(vn)), 0.0)
        yield
        dqd = _nn(_bf(do_), stb)
        yield
        dkt = _nn(_bf(vn), dstb)
        yield
        dw = -_nn(_bf(dvn), stb)
        yield
        dxx = jnp.concatenate([dvn, dw], axis=1)
        dr = _tn(x, dxx, HIGH)
        yield
        dm = -jnp.where(strict, _nt(dr, uw, HIGH), 0.0)
        yield
        dr_u, dr_w = dr[:, :HD], dr[:, HD:]
        rsum = lambda z: jnp.sum(z, axis=1, keepdims=True)

        dv = dr_u * beta
        dmg = dm * gamma
        dbeta = rsum(dr_u * v) + rsum(dr_w * k) * eg[:, :1] + rsum(dmg * p)
        yield
        dp = dmg * beta
        dq2g = dq2 * gamma
        dk = (dr_w * (beta * eg) + dkt * ekt + _tn(_bf(dq2g), _bf(q))
              + _nn(_bf(dp + dp.T), _bf(k)))
        yield
        dq = dqd * eg + _nn(_bf(dq2g), _bf(k))
        yield
        e = dp * p + dq2g * qk_raw
        t_kt = rsum(dkt * kt)
        dg = rsum(dqd * qd) + rsum(dr_w * r_w) - t_kt + rsum(e) - rsum(e.T)
        yield
        dgl = jnp.sum(t_kt, axis=0, keepdims=True) + jnp.sum(dst * st, keepdims=True) * egl[:, :1]
        rowc = lax.broadcasted_iota(jnp.int32, (C, 1), 0)
        dg = dg + jnp.where(rowc == C - 1, dgl, 0.0)
        dla = _nn(_tri(C, "upper").astype(F32), jnp.broadcast_to(dg, (C, HD)), HI)[:, :1]
        yield
        da = dla * dla_da
        db = dbeta * beta * (1.0 - beta)
        lane = lax.broadcasted_iota(jnp.int32, (C, HD), 1)
        dab = jnp.where(lane == h, da, 0.0) + jnp.where(lane == nh + h, db, 0.0)
        lane1 = lax.broadcasted_iota(jnp.int32, (1, HD), 1)
        d_alog = jnp.where(lane1 == h, jnp.sum(dla * la, axis=0, keepdims=True), 0.0)
        d_dtb = jnp.where(lane1 == h, jnp.sum(da, axis=0, keepdims=True), 0.0)
        new_dst = dst * egl + _tn(_bf(do_), _bf(qd)) - _tn(_bf(dvn), _bf(w))
        return dab, d_alog, d_dtb, new_dst, dq, dk, dv

    def body(*refs):
        c, hg = pl.program_id(0), pl.program_id(1)
        step = c * ng + hg
        ins, outs, scratch, comm_begin, comm_end = _comm_hooks(
            comm, refs, 9, 5, step == 0, step == (3 * nc * ng) // 4, step == nc * ng - 1)
        dq_ref, dk_ref, dv_ref, dab_ref, dpar_ref = outs
        ds_sc, = scratch
        comm_begin()

        @pl.when(c == 0)
        def _():
            for hh in range(hp):
                ds_sc[hg * hp + hh] = jnp.zeros((HD, HD), F32)

        @pl.when(step == 0)
        def _():
            dpar_ref[...] = jnp.zeros_like(dpar_ref)

        @pl.when(hg == 0)
        def _():
            dab_ref[...] = jnp.zeros_like(dab_ref)

        dsts = [ds_sc[hg * hp + hh] for hh in range(hp)]
        res = _interleave([one_head(hg * hp + hh, hh, dsts[hh], *ins) for hh in range(hp)])
        for hh in range(hp):
            sl = slice(hh * HD, (hh + 1) * HD)
            ds_sc[hg * hp + hh] = res[hh][3]
            dq_ref[:, sl], dk_ref[:, sl], dv_ref[:, sl] = res[hh][4], res[hh][5], res[hh][6]
        dab_ref[...] += sum(r[0] for r in res[1:]) + res[0][0]
        dpar_ref[0:1, :] += sum(r[1] for r in res[1:]) + res[0][1]
        dpar_ref[1:2, :] += sum(r[2] for r in res[1:]) + res[0][2]
        comm_end()

    rblk = lambda off: pl.BlockSpec((C, hp * HD), lambda c, g: (nc - 1 - c, off // hp + g))
    oblk = pl.BlockSpec((C, hp * HD), lambda c, g: (nc - 1 - c, g))
    vec = pl.BlockSpec((1, HD), lambda c, g: (0, 0))
    cn = comm.n if comm is not None else 0
    return pl.pallas_call(
        body, name=name, grid=(nc, ng),
        in_specs=[rblk(0), rblk(nh), rblk(2 * nh),
                  pl.BlockSpec((C, HD), lambda c, g: (nc - 1 - c, ab_blk)), vec, vec, oblk,
                  pl.BlockSpec((1, hp,C, C), lambda c, g: (nc - 1 - c, g, 0, 0)),
                  pl.BlockSpec((1, hp,HD, HD), lambda c, g: (nc - 1 - c, g, 0, 0))] + [ANY] * cn,
        out_specs=[oblk, oblk, oblk,
                   pl.BlockSpec((C, HD), lambda c, g: (nc - 1 - c, 0)),
                   pl.BlockSpec((8, HD), lambda c, g: (0, 0))] + [ANY] * cn,
        out_shape=[jax.ShapeDtypeStruct((t, nh * HD), F32)] * 3
        + [jax.ShapeDtypeStruct((t, HD), F32), jax.ShapeDtypeStruct((8, HD), F32)]
        + (comm.out_shapes() if cn else []),
        scratch_shapes=[pltpu.VMEM((nh, HD, HD), F32)] + (comm.scratch() if cn else []),
        compiler_params=_cp(("arbitrary", "arbitrary")),
    )(qkv, qkv, qkv, proj, alog, dtb, do, x_sv, st_sv, *(comm.arrays if cn else []))


def _ada_fwd(c_all, w, b, name):
    nb, d = c_all.shape
    n = w.shape[1]
    tn = _pick(n, 512)

    def body(c_ref, w_ref, b_ref, o_ref):
        cv = c_ref[...]
        o_ref[...] = _nn(cv * _sigmoid(cv), w_ref[...], HI) + b_ref[...]

    return pl.pallas_call(
        body, name=name, grid=(n // tn,),
        in_specs=[pl.BlockSpec((nb, d), lambda j: (0, 0)), pl.BlockSpec((d, tn), lambda j: (0, j)),
                  pl.BlockSpec((1, tn), lambda j: (0, j))],
        out_specs=pl.BlockSpec((nb, tn), lambda j: (0, j)),
        out_shape=jax.ShapeDtypeStruct((nb, n), F32),
        compiler_params=_cp(("parallel",)),
    )(c_all, w, b)


def _ada_wgrad(c_all, dmod, name):
    nb, d = c_all.shape
    n = dmod.shape[1]
    tn = _pick(n, 512)

    def body(c_ref, g_ref, o_ref):
        cv = c_ref[...]
        o_ref[...] = _tn(cv * _sigmoid(cv), g_ref[...], HI)

    return pl.pallas_call(
        body, name=name, grid=(n // tn,),
        in_specs=[pl.BlockSpec((nb, d), lambda j: (0, 0)), pl.BlockSpec((nb, tn), lambda j: (0, j))],
        out_specs=pl.BlockSpec((d, tn), lambda j: (0, j)),
        out_shape=jax.ShapeDtypeStruct((d, n), F32),
        compiler_params=_cp(("parallel",)),
    )(c_all, dmod)


def _adamw(w, m, v, g, name, parts=False):
    lead = w.ndim == 3
    r, cdim = w.shape[-2:]
    cap = max(SUBLANES, ADAM_BLOCK_ELEMS // cdim // SUBLANES * SUBLANES)
    tr = r if r <= cap else _pick_rows(r, cap)
    bc1 = 1.0 - ADAM_B1 ** ADAM_STEP
    bc2 = 1.0 - ADAM_B2 ** ADAM_STEP

    glist = list(g) if isinstance(g, (list, tuple)) else [g]
    bounds = [0]
    for ga in glist:
        bounds.append(bounds[-1] + ga.shape[-2] // tr)

    def body(w_ref, m_ref, v_ref, *rest):
        g_refs, (go_ref, d_ref, mo_ref, vo_ref) = rest[:len(glist)], rest[len(glist):]
        if parts:
            sums = []
            for g_ref in g_refs:
                gv = g_ref[0].astype(F32)
                for s in range(1, N_DEV):
                    gv = gv + g_ref[s].astype(F32)
                sums.append(gv)
            gv = sums[-1]
            for p in range(len(sums) - 2, -1, -1):
                gv = jnp.where(pl.program_id(0) < bounds[p + 1], sums[p], gv)
        else:
            gv = g_refs[0][...]
        wv = w_ref[...]
        mn = ADAM_B1 * m_ref[...] + (1.0 - ADAM_B1) * gv
        vn = ADAM_B2 * v_ref[...] + (1.0 - ADAM_B2) * (gv * gv)
        m_hat = mn / bc1
        v_hat = vn / bc2
        go_ref[...] = gv
        d_ref[...] = -ADAM_LR * (m_hat / (jnp.sqrt(v_hat) + ADAM_EPS) + ADAM_WD * wv)
        mo_ref[...] = mn
        vo_ref[...] = vn

    flat = pl.BlockSpec((tr, cdim), lambda i: (i, 0))
    spec = pl.BlockSpec((None, tr, cdim), lambda i: (0, i, 0)) if lead else flat
    def piece_spec(p):
        lo, n = bounds[p], bounds[p + 1] - bounds[p]
        return pl.BlockSpec((N_DEV, tr, cdim), lambda i: (0, jnp.clip(i - lo, 0, n - 1), 0))

    gspecs = [piece_spec(p) for p in range(len(glist))] if parts else [flat]
    return pl.pallas_call(
        body, name=name, grid=(r // tr,),
        in_specs=[spec, spec, spec] + gspecs,
        out_specs=[spec] * 4,
        out_shape=[jax.ShapeDtypeStruct(w.shape, F32)] * 4,
        compiler_params=_cp(("arbitrary",)),
    )(w, m, v, *glist)


def _pick_rows(r, pref):
    t = pref
    while r % t:
        t -= 8
    assert t > 0
    return t


def _dev_index(x, y, c):
    return 4 * x + 2 * y + c


class _Comm:
    def __init__(self, kind, arrays):
        self.kind, self.n = kind, len(arrays)
        self.arrays = [a[0] if isinstance(a, tuple) else a for a in arrays]
        self.rows = [(a[1], a[2]) if isinstance(a, tuple) else None for a in arrays]

    def out_shapes(self):
        if self.kind == "gather":
            return [jax.ShapeDtypeStruct((N_DEV,) + a.shape, a.dtype) for a in self.arrays]
        return [jax.ShapeDtypeStruct(a.shape if r is None else (N_DEV, r[1]) + a.shape[2:], a.dtype)
                for a, r in zip(self.arrays, self.rows)]

    def scratch(self):
        return [pltpu.SemaphoreType.DMA((self.n, 7)), pltpu.SemaphoreType.DMA((self.n, 7)),
                pltpu.SemaphoreType.DMA((self.n,))]

    def _gather_parts(self, ins, outs, sems):
        send_sems, recv_sems, local_sems = sems
        x, y, c = lax.axis_index("x"), lax.axis_index("y"), lax.axis_index("c")
        me, sibling = (x, y, c), (x, y, 1 - c)
        chips = [(1 - x, y), (x, 1 - y), (1 - x, 1 - y)]

        def copy(a, k, block, to, src=None):
            slot = outs[a].at[_dev_index(*block)]
            return pltpu.make_async_remote_copy(
                src_ref=slot if src is None else src, dst_ref=slot,
                send_sem=send_sems.at[a, k], recv_sem=recv_sems.at[a, k],
                device_id=to, device_id_type=MESH)

        n = self.n
        mine = [pltpu.make_async_copy(ins[a], outs[a].at[_dev_index(*me)], local_sems.at[a]) for a in range(n)]
        first = []
        for a in range(n):
            first.append(copy(a, 0, me, sibling, src=ins[a]))
            first += [copy(a, 1 + j, me, (*chip, c), src=ins[a]) for j, chip in enumerate(chips)]
        landed = [copy(a, 1 + j, (*chip, c), me) for j, chip in enumerate(chips) for a in range(n)]
        passed = [copy(a, 4 + j, (*chip, c), sibling) for j, chip in enumerate(chips) for a in range(n)]
        late = []
        for a in range(n):
            late.append(copy(a, 0, sibling, me))
            late += [copy(a, 4 + j, (*chip, 1 - c), me) for j, chip in enumerate(chips)]
        return mine, first, landed, passed, late

    def _exchange_parts(self, ins, outs, sems):
        send_sems, recv_sems, local_sems = sems
        x, y, c = lax.axis_index("x"), lax.axis_index("y"), lax.axis_index("c")
        my = _dev_index(x, y, c)
        n = self.n

        def block(a, j):
            r = self.rows[a]
            return ins[a].at[j] if r is None else ins[a].at[j, pl.ds(r[0], r[1])]

        mine = [pltpu.make_async_copy(block(a, my), outs[a].at[my], local_sems.at[a]) for a in range(n)]
        sends, recvs = [], []
        for k in range(1, N_DEV):
            px = (1 - x) if (k >> 2) & 1 else x
            py = (1 - y) if (k >> 1) & 1 else y
            pc = (1 - c) if k & 1 else c
            peer = _dev_index(px, py, pc)
            for a in range(n):
                sends.append(pltpu.make_async_remote_copy(
                    src_ref=block(a, peer), dst_ref=outs[a].at[my],
                    send_sem=send_sems.at[a, k - 1], recv_sem=recv_sems.at[a, k - 1],
                    device_id=(px, py, pc), device_id_type=MESH))
                recvs.append(pltpu.make_async_remote_copy(
                    src_ref=block(a, my), dst_ref=outs[a].at[peer],
                    send_sem=send_sems.at[a, k - 1], recv_sem=recv_sems.at[a, k - 1],
                    device_id=(x, y, c), device_id_type=MESH))
        return mine, sends, recvs

    def start(self, ins, outs, sems):
        if self.kind == "gather":
            mine, first, _, _, _ = self._gather_parts(ins, outs, sems)
        else:
            mine, first, _ = self._exchange_parts(ins, outs, sems)
        for cp in mine + first:
            cp.start()

    def mid(self, ins, outs, sems):
        if self.kind == "gather":
            _, _, landed, passed, _ = self._gather_parts(ins, outs, sems)
            for got, fwd in zip(landed, passed):
                got.wait_recv()
                fwd.start()

    def finish(self, ins, outs, sems):
        if self.kind == "gather":
            mine, first, _, passed, late = self._gather_parts(ins, outs, sems)
            for cp in late:
                cp.wait_recv()
            for cp in first + passed:
                cp.wait_send()
        else:
            mine, sends, recvs = self._exchange_parts(ins, outs, sems)
            for cp in sends:
                cp.wait_send()
            for cp in recvs:
                cp.wait_recv()
        for cp in mine:
            cp.wait()

    def run(self, name):
        n = self.n

        def body(*refs):
            ins, outs, sems = refs[:n], refs[n:2 * n], refs[2 * n:]
            self.start(ins, outs, sems)
            self.mid(ins, outs, sems)
            self.finish(ins, outs, sems)

        return pl.pallas_call(
            body, name=name, in_specs=[ANY] * n, out_specs=[ANY] * n,
            out_shape=self.out_shapes(), scratch_shapes=self.scratch(),
        )(*self.arrays)


def _all_gather(arrays, name):
    return _Comm("gather", arrays).run(name)


def _comm_hooks(comm, refs, n_in, n_out, first, middle, last):
    cn = comm.n if comm is not None else 0
    ins, cins = refs[:n_in], refs[n_in:n_in + cn]
    outs, couts = refs[n_in + cn:n_in + cn + n_out], refs[n_in + cn + n_out:n_in + 2 * cn + n_out]
    rest = refs[n_in + 2 * cn + n_out:]
    scratch, csems = (rest[:len(rest) - 3], rest[len(rest) - 3:]) if cn else (rest, ())

    def begin():
        if cn:
            pl.when(first)(lambda: comm.start(cins, couts, csems))
            pl.when(middle)(lambda: comm.mid(cins, couts, csems))

    def end():
        if cn:
            pl.when(last)(lambda: comm.finish(cins, couts, csems))

    return ins, outs, scratch, begin, end


def _local_step(x, tgt, mod, n1, n2, n3, n4, w_in_p, lb_logits, hg_norm, conv_w, alog, dtb, gdn_norm,
                late_w, dist=None):
    t, d = x.shape
    nh = d // 2 // HD
    ab_blk = 8 * nh
    sh_m, sc_m, gt_m, sh_f, sc_f, gt_f = [mod[i:i + 1] for i in range(6)]

    h1, r1 = _prenorm(x, n1, sc_m, sh_m, "prenorm_mix")
    if dist is None:
        proj = _mm(h1, w_in_p, "nn", [F32], "mm_proj")
        o_hg, a_sv, hst_sv = _hgrn2_fwd(proj, lb_logits, nh, "hgrn2_fwd")
        qkv = _gdn_prep(proj, conv_w, 4 * nh, nh, "gdn_prep")
        o_gd, x_sv, gst_sv = _gdn_fwd(qkv, proj, ab_blk, alog, dtb, nh, "gdn_fwd")
        w_out, w_ff1, w_ff2 = late_w
        w_ff1_t, w_ff2_t = w_ff1.T, w_ff2.T
        exch = lambda arrays: None
    else:
        proj, g_ff2 = _mm(h1, w_in_p, "nn", [F32], "mm_proj", comm=_Comm("gather", late_w[2:]))
        o_hg, a_sv, hst_sv, g_out = _hgrn2_fwd(proj, lb_logits, nh, "hgrn2_fwd",
                                               comm=_Comm("gather", late_w[:1]))
        qkv = _gdn_prep(proj, conv_w, 4 * nh, nh, "gdn_prep")
        o_gd, x_sv, gst_sv, g_ff1 = _gdn_fwd(qkv, proj, ab_blk, alog, dtb, nh, "gdn_fwd",
                                             comm=_Comm("gather", late_w[1:2]))
        w_out, w_ff1, w_ff2, w_ff1_t, w_ff2_t = dist["assemble"](g_out, g_ff1, g_ff2)
        exch = lambda arrays: _Comm("exchange", arrays)
    w_in_pt = w_in_p.T
    om_hg = _headnorm_fwd(o_hg, proj, 3 * nh, hg_norm, "headnorm_hg")
    om_gd = _headnorm_fwd(o_gd, proj, 7 * nh, gdn_norm, "headnorm_gdn")
    om = jnp.concatenate([om_hg, om_gd], axis=1)
    y1 = _mm(om, w_out, "nn", [F32], "mm_out")
    x1, r2, h2, r3 = _postnorm_prenorm(x, y1, n2, gt_m, n3, sc_f, sh_f, "postnorm_mix_prenorm_ffn")

    def relu2(acc, extra, outs):
        rl = jnp.maximum(acc, 0.0)
        outs[0][...] = (rl * rl).astype(BF16)

    act = _mm(h2, w_ff1, "nn", [BF16], "mm_ff1", epilogue=relu2)
    y2 = _mm(act, w_ff2, "nn", [F32], "mm_ff2")
    dout, dy2, loss, dgt_f, dn4 = _final_loss_bwd(x1, y2, n4, gt_f, tgt, "final_loss_bwd")
    whole_t = dict(tk=t, tn=1024)
    dw_ff2 = _mm(act, dy2, "tn", [BF16], "mm_dw_ff2", **whole_t)

    def drelu2(acc, extra, outs):
        outs[0][...] = (acc * (2.0 * jnp.sqrt(extra[0][...].astype(F32)))).astype(BF16)

    recv = {}
    ff2a, ff2b = dist["parts_ff2"](dw_ff2) if dist else (None, None)
    du, *recv["ff2a"] = _listed(_mm(dy2, w_ff2_t, "nn", [BF16], "mm_da", epilogue=drelu2, extras=(act,),
                                    comm=exch([ff2a])))
    ff1_cols = dict(by_cols=True, tk=t, tn=dist["n_ff"]) if dist else whole_t
    dw_ff1, *recv["ff2b"] = _listed(_mm(h2, du, "tn", [BF16], "mm_dw_ff1", comm=exch([ff2b]), **ff1_cols))
    ff1a, ff1b = dist["parts_ff1"](dw_ff1) if dist else (None, None)
    dh2, *recv["ff1a"] = _listed(_mm(du, w_ff1_t, "nn", [F32], "mm_dh2", comm=exch([ff1a])))
    dx1, dy1, dsh_f, dsc_f, dn3, dgt_m, dn2 = _prenorm_postnorm_bwd(
        dh2, x1, r3, n3, sc_f, dout, y1, r2, n2, gt_m, "prenorm_ffn_postnorm_mix_bwd")

    dw_out = _mm(om, dy1, "tn", [BF16], "mm_dw_out", **whole_t)
    dom = _mm(dy1, w_out, "nt", [F32], "mm_dom")
    do_hg, dg_hg, dhgn = _headnorm_bwd(dom, 0, o_hg, proj, 3 * nh, hg_norm, "headnorm_hg_bwd")
    do_gd, dg_gd, dgdn = _headnorm_bwd(dom, 1, o_gd, proj, 7 * nh, gdn_norm, "headnorm_gdn_bwd")
    p_out = dist["parts_out"](dw_out) if dist else None
    dq_hg, df_hg, di_hg, dl0, *recv["ff1b_out"] = _hgrn2_bwd(proj, lb_logits, do_hg, a_sv, hst_sv, nh,
                                                             "hgrn2_bwd", comm=exch([ff1b, p_out]))
    dq_g, dk_g, dv_g, dab, dpar = _gdn_bwd(qkv, proj, ab_blk, alog, dtb, do_gd, x_sv, gst_sv, nh, "gdn_bwd")
    du_conv, dconv = _gdn_prep_bwd(proj, conv_w, dq_g, dk_g, dv_g, 4 * nh, nh, "gdn_prep_bwd")
    dproj = jnp.concatenate([dq_hg, df_hg, di_hg, dg_hg, du_conv, dg_gd, dab.astype(BF16)], axis=1)
    if dist is None:
        dw_in = _mm(h1, dproj, "tn", [BF16], "mm_dw_in")
        dh1 = _mm(dproj, w_in_pt, "nn", [F32], "mm_dh1", tk=1664)
    else:
        q4 = d // 4
        dw_in_a = _mm(h1[:, :q4], dproj, "tn", [BF16], "mm_dw_in_a")
        dw_in_b, in_a = _mm(h1[:, q4:2 * q4], dproj, "tn", [BF16], "mm_dw_in_b",
                            comm=exch([dist["parts_in"](dw_in_a)]))
        dw_in_c, in_b = _mm(h1[:, 2 * q4:], dproj, "tn", [BF16], "mm_dw_in_c",
                            comm=exch([dist["parts_in"](dw_in_b)]))
        dh1, in_c = _mm(dproj, w_in_pt, "nn", [F32], "mm_dh1", tk=1664, comm=exch([dist["parts_in"](dw_in_c)]))
        recv["in"] = [in_a, in_b, in_c]
        dw_in = None
    dx, dsh_m, dsc_m, dn1 = _prenorm_bwd(dh1, x, r1, n1, sc_m, dx1, "prenorm_mix_bwd")

    dmod = jnp.concatenate([dsh_m, dsc_m, dgt_m, dsh_f, dsc_f, dgt_f], axis=0)
    grads = dict(dmod=dmod, n1=dn1, n2=dn2, n3=dn3, n4=dn4, w_in=dw_in, lb0=dl0, hg_norm=dhgn, conv=dconv,
                 alog=dpar[0:1], dtb=dpar[1:2], gdn_norm=dgdn, w_out=dw_out, w_ff1=dw_ff1, w_ff2=dw_ff2,
                 recv=recv)
    return loss, dx, grads


def _pack(vals):
    rows = []
    for vv in vals:
        flat = vv.reshape(-1)
        flat = jnp.pad(flat, (0, (-flat.shape[0]) % (SUBLANES * LANES)))
        rows.append(flat.reshape(-1, LANES))
    return jnp.concatenate(rows, axis=0)


def _unpack(packed, shapes):
    out, r = [], 0
    for shp in shapes:
        size = 1
        for s in shp:
            size *= s
        nr = -(-size // (SUBLANES * LANES)) * SUBLANES
        out.append(packed[r:r + nr].reshape(-1)[:size].reshape(shp))
        r += nr
    return out


def _sum_parts(parts, name):
    _, r, cdim = parts.shape

    def body(p_ref, o_ref):
        acc = p_ref[0]
        for s in range(1, N_DEV):
            acc = acc + p_ref[s]
        o_ref[...] = acc

    return pl.pallas_call(
        body, name=name,
        out_shape=jax.ShapeDtypeStruct((r, cdim), F32),
        compiler_params=_cp(),
    )(parts)


def kernel(x, c, w_ada, b_ada, pre_mix_norm, post_mix_norm, pre_ffn_norm, post_ffn_norm, w_in, hg_lb_logits, hg_norm, gdn_conv_w, gdn_a_log, gdn_dt_bias, gdn_norm, w_out, w_ff1, w_ff2, loss_target, m_w_ada, m_b_ada, m_pre_mix_norm, m_post_mix_norm, m_pre_ffn_norm, m_post_ffn_norm, m_w_in, m_hg_lb_logits, m_hg_norm, m_gdn_conv_w, m_gdn_a_log, m_gdn_dt_bias, m_gdn_norm, m_w_out, m_w_ff1, m_w_ff2, v_w_ada, v_b_ada, v_pre_mix_norm, v_post_mix_norm, v_pre_ffn_norm, v_post_ffn_norm, v_w_in, v_hg_lb_logits, v_hg_norm, v_gdn_conv_w, v_gdn_a_log, v_gdn_dt_bias, v_gdn_norm, v_w_out, v_w_ff1, v_w_ff2):
    t, d = x.shape[1], x.shape[2]
    nh = d // 2 // HD
    in_cols = w_in.shape[2] * N_DEV
    main = in_cols - 2 * nh
    me = _dev_index(lax.axis_index("x"), lax.axis_index("y"), lax.axis_index("c"))

    c_all, conv_g = _all_gather([c, gdn_conv_w[0]], "gather_small")
    c_all = c_all.reshape(N_DEV, d)
    conv_full = conv_g.transpose(1, 0, 2).reshape(CONV_K, -1)
    w_in_g = _all_gather([w_in[0].astype(BF16)], "gather_w_in")[0]
    w_in_full = w_in_g.transpose(1, 0, 2).reshape(d, in_cols)
    w_in_p = jnp.concatenate([w_in_full, jnp.zeros((d, LANES - 2 * nh), BF16)], axis=1)
    late_w = [w_out[0].astype(BF16), w_ff1[0].astype(BF16), w_ff2[0].astype(BF16)]

    n_in = w_in.shape[2]
    n_ff = w_ff1.shape[2]

    def halves(p):
        r = p.shape[1] // 2
        return (p, 0, r), (p, r, r)

    dist = dict(
        assemble=lambda g_out, g_ff1, g_ff2: (g_out.reshape(d, d), g_ff1.transpose(1, 0, 2).reshape(d, -1),
                                              g_ff2.reshape(-1, d), g_ff1.transpose(0, 2, 1).reshape(-1, d),
                                              g_ff2.reshape(-1, d).T),
        n_ff=n_ff,
        parts_ff2=lambda dw: halves(dw.reshape(N_DEV, -1, d)),
        parts_ff1=halves,
        parts_out=lambda dw: dw.reshape(N_DEV, d // N_DEV, d),
        parts_in=lambda dw: dw[:, :in_cols].reshape(dw.shape[0], N_DEV, n_in).transpose(1, 0, 2),
    )

    n_ada = w_ada.shape[2]
    b_loc = lax.dynamic_slice(b_ada, (0, me * n_ada), (1, n_ada))
    mod_part = _ada_fwd(c_all, w_ada[0], b_loc, "ada_fwd")
    mod_all = _all_gather([mod_part], "gather_mod")[0]
    mod = lax.dynamic_slice(mod_all, (0, me, 0), (N_DEV, 1, n_ada)).reshape(6, d)

    pad_lane = lambda vv: jnp.concatenate([vv, jnp.zeros((1, LANES - vv.shape[1]), F32)], axis=1)
    loss, dx, g = _local_step(
        x[0], loss_target[0], mod, pre_mix_norm, post_mix_norm, pre_ffn_norm, post_ffn_norm, w_in_p,
        hg_lb_logits, hg_norm, conv_full, pad_lane(gdn_a_log), pad_lane(gdn_dt_bias), gdn_norm,
        late_w, dist)

    rep_names = ["b_ada", "n1", "n2", "n3", "n4", "lb", "hg_norm", "alog", "dtb", "gdn_norm"]
    rep_w = [b_ada, pre_mix_norm, post_mix_norm, pre_ffn_norm, post_ffn_norm, hg_lb_logits, hg_norm,
             gdn_a_log, gdn_dt_bias, gdn_norm]
    rep_m = [m_b_ada, m_pre_mix_norm, m_post_mix_norm, m_pre_ffn_norm, m_post_ffn_norm, m_hg_lb_logits,
             m_hg_norm, m_gdn_a_log, m_gdn_dt_bias, m_gdn_norm]
    rep_v = [v_b_ada, v_pre_mix_norm, v_post_mix_norm, v_pre_ffn_norm, v_post_ffn_norm, v_hg_lb_logits,
             v_hg_norm, v_gdn_a_log, v_gdn_dt_bias, v_gdn_norm]
    rep_shapes = [a.shape for a in rep_w]
    g_lb = jnp.stack([g["lb0"], -g["lb0"]], axis=0)
    rep_g = [g["dmod"], g["n1"], g["n2"], g["n3"], g["n4"], g_lb, g["hg_norm"],
             g["alog"][:, :nh], g["dtb"][:, :nh], g["gdn_norm"]]
    small = _pack(rep_g + [g["conv"]])
    n_rep_rows = _pack(rep_g).shape[0]
    pad_rows = (-small.shape[0]) % 8
    if pad_rows:
        small = jnp.concatenate([small, jnp.zeros((pad_rows, LANES), F32)], axis=0)
    small_all = _all_gather([small], "gather_small_grads")[0]
    small_sum = _sum_parts(small_all, "sum_small_grads")
    rep_out = _adamw(_pack(rep_w), _pack(rep_m), _pack(rep_v), small_sum[:n_rep_rows], "adamw_small")
    rep_g_o, rep_d_o, rep_m_o, rep_v_o = [dict(zip(rep_names, _unpack(p, rep_shapes))) for p in rep_out]

    conv_sum = small_sum[n_rep_rows:n_rep_rows + CONV_K * conv_full.shape[1] // LANES].reshape(CONV_K, -1)
    n_conv = gdn_conv_w.shape[2]
    conv_loc = lax.dynamic_slice(conv_sum, (0, me * n_conv), (CONV_K, n_conv))
    conv_o = _adamw(gdn_conv_w, m_gdn_conv_w, v_gdn_conv_w, conv_loc, "adamw_conv")

    dmod_all = small_all[:, :6 * d // LANES, :].reshape(N_DEV, 6 * d)
    dmod_loc = lax.dynamic_slice(dmod_all, (0, me * n_ada), (N_DEV, n_ada))
    g_ada = _ada_wgrad(c_all, dmod_loc, "ada_wgrad")
    ada_o = _adamw(w_ada, m_w_ada, v_w_ada, g_ada, "adamw_ada")

    rc = g["recv"]
    r_ff2 = [rc["ff2a"][0], rc["ff2b"][0]]
    r_ff1 = [rc["ff1a"][0], rc["ff1b_out"][0]]
    r_out, r_in = rc["ff1b_out"][1], rc["in"]
    in_o = _adamw(w_in, m_w_in, v_w_in, r_in, "adamw_w_in", parts=True)
    out_o = _adamw(w_out, m_w_out, v_w_out, r_out, "adamw_w_out", parts=True)
    ff1_o = _adamw(w_ff1, m_w_ff1, v_w_ff1, r_ff1, "adamw_w_ff1", parts=True)
    ff2_o = _adamw(w_ff2, m_w_ff2, v_w_ff2, r_ff2, "adamw_w_ff2", parts=True)

    loss_tot = lax.psum(loss[0, 0], ("x", "y", "c"))

    def leaf(kind):
        return [ada_o[kind], rep_out_d[kind]["b_ada"], rep_out_d[kind]["n1"], rep_out_d[kind]["n2"],
                rep_out_d[kind]["n3"], rep_out_d[kind]["n4"], in_o[kind], rep_out_d[kind]["lb"],
                rep_out_d[kind]["hg_norm"], conv_o[kind], rep_out_d[kind]["alog"], rep_out_d[kind]["dtb"],
                rep_out_d[kind]["gdn_norm"], out_o[kind], ff1_o[kind], ff2_o[kind]]

    rep_out_d = [rep_g_o, rep_d_o, rep_m_o, rep_v_o]
    return (loss_tot, dx[None], *leaf(0), *leaf(1), *leaf(2), *leaf(3))
```

```python
import functools

import jax
import jax.numpy as jnp
from jax import lax
from jax.experimental import pallas as pl
from jax.experimental.pallas import tpu as pltpu

F32 = jnp.float32
BF16 = jnp.bfloat16
HI = lax.Precision.HIGHEST
HIGH = lax.Precision.HIGH

EPS = 1e-6
CHUNK = 64
SB = 16
NSB = CHUNK // SB
HP = 8
HD = 128
CONV_K = 4
N_DEV = 8
LANES = 128
SUBLANES = 8
VMEM_LIMIT = 56 * 1024 * 1024
MM_FULL_K = 2048

ADAM_BLOCK_ELEMS = 256 * 1024
ADAM_LR = 0.001
ADAM_B1 = 0.9
ADAM_B2 = 0.999
ADAM_EPS = 1e-08
ADAM_WD = 0.01
ADAM_STEP = 10

ANY = pl.BlockSpec(memory_space=pl.ANY)
MESH = pl.DeviceIdType.MESH


def _cp(sem=None):
    return pltpu.CompilerParams(dimension_semantics=sem, vmem_limit_bytes=VMEM_LIMIT)


def _dot(a, b, dims, precision=None):
    return lax.dot_general(a, b, (dims, ((), ())), precision=precision, preferred_element_type=F32)


def _nn(a, b, precision=None):
    return _dot(a, b, ((1,), (0,)), precision)


def _nt(a, b, precision=None):
    return _dot(a, b, ((1,), (1,)), precision)


def _tn(a, b, precision=None):
    return _dot(a, b, ((0,), (0,)), precision)


def _bf(x):
    return x.astype(BF16)


def _sigmoid(x):
    return 1.0 / (1.0 + jnp.exp(-x))


def _interleave(gens):
    results = [None] * len(gens)
    live = list(range(len(gens)))
    while live:
        for i in list(live):
            try:
                next(gens[i])
            except StopIteration as stop:
                results[i] = stop.value
                live.remove(i)
    return results


def _listed(res):
    return list(res) if isinstance(res, (list, tuple)) else [res]


def _pick(n, pref):
    if n <= pref:
        return n
    t = pref
    while n % t:
        t -= LANES
    assert t > 0, (n, pref)
    return t


def _mm(a, b, mode, out_dtypes, name, epilogue=None, extras=(), tm=1024, tn=2048, tk=1024, comm=None,
        by_cols=False):
    if mode == "nn":
        (m, kd), (_, n) = a.shape, b.shape
    elif mode == "nt":
        (m, kd), (n, _) = a.shape, b.shape
    else:
        (kd, m), (_, n) = a.shape, b.shape
    if kd <= MM_FULL_K:
        tk = kd
    tm, tn, tk = _pick(m, tm), _pick(n, tn), _pick(kd, tk)
    nk = kd // tk
    if mode == "nn":
        a_spec = pl.BlockSpec((tm, tk), lambda i, j, k: (i, k))
        b_spec = pl.BlockSpec((tk, tn), lambda i, j, k: (k, j))
        dims = ((1,), (0,))
    elif mode == "nt":
        a_spec = pl.BlockSpec((tm, tk), lambda i, j, k: (i, k))
        b_spec = pl.BlockSpec((tn, tk), lambda i, j, k: (j, k))
        dims = ((1,), (1,))
    else:
        a_spec = pl.BlockSpec((tk, tm), lambda i, j, k: (k, i))
        b_spec = pl.BlockSpec((tk, tn), lambda i, j, k: (k, j))
        dims = ((0,), (0,))
    o_spec = pl.BlockSpec((tm, tn), lambda i, j, k: (i, j))
    if by_cols:
        assert epilogue is None and not extras
        res_spec = pl.BlockSpec((None, tm, tn), lambda i, j, k: (j, i, 0))
        res_shape = (n // tn, m, tn)
    else:
        res_spec, res_shape = o_spec, (m, n)
    n_extra, n_out = len(extras), len(out_dtypes)

    gm, gn = m // tm, n // tn
    cn = comm.n if comm is not None else 0

    def body(*refs):
        i, j, k = pl.program_id(0), pl.program_id(1), pl.program_id(2)
        at0 = (j == 0) & (k == 0)
        ins, out_refs, scratch, comm_begin, comm_end = _comm_hooks(
            comm, refs, 2 + n_extra, n_out, (i == 0) & at0, (i == gm - 1) & at0,
            (i == gm - 1) & (j == gn - 1) & (k == nk - 1))
        a_ref, b_ref, extra_refs = ins[0], ins[1], ins[2:]
        comm_begin()
        if nk == 1:
            part = _dot(a_ref[...], b_ref[...], dims)
            if epilogue is None:
                out_refs[0][...] = part.astype(out_dtypes[0])
            else:
                epilogue(part, extra_refs, out_refs)
        else:
            acc, = scratch

            @pl.when(k == 0)
            def _():
                acc[...] = jnp.zeros_like(acc)

            acc[...] += _dot(a_ref[...], b_ref[...], dims)

            @pl.when(k == nk - 1)
            def _():
                if epilogue is None:
                    out_refs[0][...] = acc[...].astype(out_dtypes[0])
                else:
                    epilogue(acc[...], extra_refs, out_refs)

        comm_end()

    acc_scratch = [] if nk == 1 else [pltpu.VMEM((tm, tn), F32)]
    sem = ("arbitrary",) * 3 if cn else ("parallel", "parallel", "arbitrary")
    outs = pl.pallas_call(
        body, name=name,
        grid=(gm, gn, nk),
        in_specs=[a_spec, b_spec] + [o_spec] * n_extra + [ANY] * cn,
        out_specs=[res_spec] * n_out + [ANY] * cn,
        out_shape=[jax.ShapeDtypeStruct(res_shape, dt) for dt in out_dtypes] + (comm.out_shapes() if cn else []),
        scratch_shapes=acc_scratch + (comm.scratch() if cn else []),
        compiler_params=_cp(sem),
    )(a, b, *extras, *(comm.arrays if cn else []))
    return outs[0] if n_out + cn == 1 else outs


def _row_spec(tb, d):
    return pl.BlockSpec((tb, d), lambda i: (i, 0))


def _vec_spec(d):
    return pl.BlockSpec((1, d), lambda i: (0, 0))


def _prenorm(x, w, sc, sh, name):
    t, d = x.shape
    tb = _pick(t, 256)

    def body(x_ref, w_ref, sc_ref, sh_ref, h_ref, r_ref):
        xv = x_ref[...]
        r = lax.rsqrt(jnp.mean(xv * xv, axis=-1, keepdims=True) + EPS)
        h_ref[...] = ((xv * r * w_ref[...]) * (1.0 + sc_ref[...]) + sh_ref[...]).astype(BF16)
        r_ref[...] = r

    return pl.pallas_call(
        body, name=name, grid=(t // tb,),
        in_specs=[_row_spec(tb, d), _vec_spec(d), _vec_spec(d), _vec_spec(d)],
        out_specs=[_row_spec(tb, d), _row_spec(tb, 1)],
        out_shape=[jax.ShapeDtypeStruct((t, d), BF16), jax.ShapeDtypeStruct((t, 1), F32)],
        compiler_params=_cp(("parallel",)),
    )(x, w, sc, sh)


def _final_loss_bwd(x, y, w, gt, tgt, name):
    t, d = x.shape
    tb = _pick(t, 256)

    def body(x_ref, y_ref, w_ref, gt_ref, tgt_ref, dout_ref, dy_ref, loss_ref, dgt_ref, dw_ref):
        @pl.when(pl.program_id(0) == 0)
        def _():
            loss_ref[...] = jnp.zeros_like(loss_ref)
            dgt_ref[...] = jnp.zeros_like(dgt_ref)
            dw_ref[...] = jnp.zeros_like(dw_ref)

        yv, wv, gtv = y_ref[...], w_ref[...], gt_ref[...]
        r = lax.rsqrt(jnp.mean(yv * yv, axis=-1, keepdims=True) + EPS)
        z = yv * r
        nz = z * wv
        diff = (x_ref[...] + gtv * nz) - tgt_ref[...]
        loss_ref[...] += 0.5 * jnp.sum(jnp.mean(diff * diff, axis=-1, keepdims=True), axis=0, keepdims=True)
        dxv = diff * (1.0 / d)
        dout_ref[...] = dxv
        dgt_ref[...] += jnp.sum(dxv * nz, axis=0, keepdims=True)
        dn = dxv * gtv
        dw_ref[...] += jnp.sum(dn * z, axis=0, keepdims=True)
        dz = dn * wv
        dy_ref[...] = (r * (dz - z * jnp.mean(dz * z, axis=-1, keepdims=True))).astype(BF16)

    return pl.pallas_call(
        body, name=name, grid=(t // tb,),
        in_specs=[_row_spec(tb, d), _row_spec(tb, d), _vec_spec(d), _vec_spec(d), _row_spec(tb, d)],
        out_specs=[_row_spec(tb, d), _row_spec(tb, d), pl.BlockSpec((1, 1), lambda i: (0, 0)),
                   _vec_spec(d), _vec_spec(d)],
        out_shape=[jax.ShapeDtypeStruct((t, d), F32), jax.ShapeDtypeStruct((t, d), BF16),
                   jax.ShapeDtypeStruct((1, 1), F32), jax.ShapeDtypeStruct((1, d), F32),
                   jax.ShapeDtypeStruct((1, d), F32)],
        compiler_params=_cp(("arbitrary",)),
    )(x, y, w, gt, tgt)


def _postnorm_prenorm(x, y, w_post, gt, w_pre, sc, sh, name):
    t, d = x.shape
    tb = _pick(t, 256)

    def body(x_ref, y_ref, wp_ref, gt_ref, wn_ref, sc_ref, sh_ref, x1_ref, r_ref, h_ref, r1_ref):
        yv = y_ref[...]
        r = lax.rsqrt(jnp.mean(yv * yv, axis=-1, keepdims=True) + EPS)
        x1 = x_ref[...] + gt_ref[...] * (yv * r * wp_ref[...])
        r1 = lax.rsqrt(jnp.mean(x1 * x1, axis=-1, keepdims=True) + EPS)
        x1_ref[...] = x1
        r_ref[...] = r
        h_ref[...] = ((x1 * r1 * wn_ref[...]) * (1.0 + sc_ref[...]) + sh_ref[...]).astype(BF16)
        r1_ref[...] = r1

    return pl.pallas_call(
        body, name=name, grid=(t // tb,),
        in_specs=[_row_spec(tb, d), _row_spec(tb, d)] + [_vec_spec(d)] * 5,
        out_specs=[_row_spec(tb, d), _row_spec(tb, 1), _row_spec(tb, d), _row_spec(tb, 1)],
        out_shape=[jax.ShapeDtypeStruct((t, d), F32), jax.ShapeDtypeStruct((t, 1), F32),
                   jax.ShapeDtypeStruct((t, d), BF16), jax.ShapeDtypeStruct((t, 1), F32)],
        compiler_params=_cp(("parallel",)),
    )(x, y, w_post, gt, w_pre, sc, sh)


def _prenorm_postnorm_bwd(dh, x, r_pre, w_pre, sc, dres, y, r_post, w_post, gt, name):
    t, d = x.shape
    tb = _pick(t, 256)

    def body(dh_ref, x_ref, rp_ref, wp_ref, sc_ref, dres_ref, y_ref, rq_ref, wq_ref, gt_ref,
             dx_ref, dy_ref, dsh_ref, dsc_ref, dwp_ref, dgt_ref, dwq_ref):
        @pl.when(pl.program_id(0) == 0)
        def _():
            for ref in (dsh_ref, dsc_ref, dwp_ref, dgt_ref, dwq_ref):
                ref[...] = jnp.zeros_like(ref)

        dhv, rv, wv = dh_ref[...], rp_ref[...], wp_ref[...]
        z = x_ref[...] * rv
        dsh_ref[...] += jnp.sum(dhv, axis=0, keepdims=True)
        dsc_ref[...] += jnp.sum(dhv * (z * wv), axis=0, keepdims=True)
        dzw = dhv * (1.0 + sc_ref[...])
        dwp_ref[...] += jnp.sum(dzw * z, axis=0, keepdims=True)
        dz = dzw * wv
        dxv = dres_ref[...] + rv * (dz - z * jnp.mean(dz * z, axis=-1, keepdims=True))
        dx_ref[...] = dxv

        rq, wq = rq_ref[...], wq_ref[...]
        zq = y_ref[...] * rq
        dgt_ref[...] += jnp.sum(dxv * (zq * wq), axis=0, keepdims=True)
        dn = dxv * gt_ref[...]
        dwq_ref[...] += jnp.sum(dn * zq, axis=0, keepdims=True)
        dzq = dn * wq
        dy_ref[...] = (rq * (dzq - zq * jnp.mean(dzq * zq, axis=-1, keepdims=True))).astype(BF16)

    rs, r1, vs = _row_spec(tb, d), _row_spec(tb, 1), _vec_spec(d)
    return pl.pallas_call(
        body, name=name, grid=(t // tb,),
        in_specs=[rs, rs, r1, vs, vs, rs, rs, r1, vs, vs],
        out_specs=[rs, rs] + [vs] * 5,
        out_shape=[jax.ShapeDtypeStruct((t, d), F32), jax.ShapeDtypeStruct((t, d), BF16)]
        + [jax.ShapeDtypeStruct((1, d), F32)] * 5,
        compiler_params=_cp(("arbitrary",)),
    )(dh, x, r_pre, w_pre, sc, dres, y, r_post, w_post, gt)


def _prenorm_bwd(dh, x, r, w, sc, dres, name):
    t, d = x.shape
    tb = _pick(t, 256)

    def body(dh_ref, x_ref, r_ref, w_ref, sc_ref, dres_ref, dx_ref, dsh_ref, dsc_ref, dw_ref):
        @pl.when(pl.program_id(0) == 0)
        def _():
            dsh_ref[...] = jnp.zeros_like(dsh_ref)
            dsc_ref[...] = jnp.zeros_like(dsc_ref)
            dw_ref[...] = jnp.zeros_like(dw_ref)

        dhv, rv, wv = dh_ref[...], r_ref[...], w_ref[...]
        z = x_ref[...] * rv
        dsh_ref[...] += jnp.sum(dhv, axis=0, keepdims=True)
        dsc_ref[...] += jnp.sum(dhv * (z * wv), axis=0, keepdims=True)
        dzw = dhv * (1.0 + sc_ref[...])
        dw_ref[...] += jnp.sum(dzw * z, axis=0, keepdims=True)
        dz = dzw * wv
        dx_ref[...] = dres_ref[...] + rv * (dz - z * jnp.mean(dz * z, axis=-1, keepdims=True))

    return pl.pallas_call(
        body, name=name, grid=(t // tb,),
        in_specs=[_row_spec(tb, d), _row_spec(tb, d), _row_spec(tb, 1), _vec_spec(d), _vec_spec(d),
                  _row_spec(tb, d)],
        out_specs=[_row_spec(tb, d), _vec_spec(d), _vec_spec(d), _vec_spec(d)],
        out_shape=[jax.ShapeDtypeStruct((t, d), F32)] + [jax.ShapeDtypeStruct((1, d), F32)] * 3,
        compiler_params=_cp(("arbitrary",)),
    )(dh, x, r, w, sc, dres)


def _headnorm_fwd(o, proj, g_blk, nw, name):
    t, wd = o.shape
    nh = wd // HD
    tb = _pick(t, 512)
    gb = g_blk * HD // wd

    def body(o_ref, g_ref, nw_ref, out_ref):
        o3 = o_ref[...].reshape(tb, nh, HD)
        g3 = g_ref[...].reshape(tb, nh, HD)
        rh = lax.rsqrt(jnp.mean(o3 * o3, axis=-1, keepdims=True) + EPS)
        res = (o3 * rh * nw_ref[...].reshape(1, 1, HD)) * (g3 * _sigmoid(g3))
        out_ref[...] = res.reshape(tb, wd).astype(BF16)

    return pl.pallas_call(
        body, name=name, grid=(t // tb,),
        in_specs=[_row_spec(tb, wd), pl.BlockSpec((tb, wd), lambda i: (i, gb)), _vec_spec(HD)],
        out_specs=_row_spec(tb, wd),
        out_shape=jax.ShapeDtypeStruct((t, wd), BF16),
        compiler_params=_cp(("parallel",)),
    )(o, proj, nw)


def _headnorm_bwd(dom, col_blk, o, proj, g_blk, nw, name):
    t, wd = o.shape
    nh = wd // HD
    tb = _pick(t, 512)
    gb = g_blk * HD // wd

    def body(do_ref, o_ref, g_ref, nw_ref, dout_ref, dg_ref, dnw_ref):
        @pl.when(pl.program_id(0) == 0)
        def _():
            dnw_ref[...] = jnp.zeros_like(dnw_ref)

        dn = do_ref[...].reshape(tb, nh, HD)
        o3 = o_ref[...].reshape(tb, nh, HD)
        g3 = g_ref[...].reshape(tb, nh, HD)
        nw3 = nw_ref[...].reshape(1, 1, HD)
        rh = lax.rsqrt(jnp.mean(o3 * o3, axis=-1, keepdims=True) + EPS)
        z = o3 * rh
        sg = _sigmoid(g3)
        sl = g3 * sg
        dnw_ref[...] += jnp.sum(jnp.sum(dn * sl * z, axis=1), axis=0, keepdims=True)
        dg_ref[...] = (dn * (z * nw3) * (sg * (1.0 + g3 * (1.0 - sg)))).reshape(tb, wd).astype(BF16)
        dz = dn * sl * nw3
        dout_ref[...] = (rh * (dz - z * jnp.mean(dz * z, axis=-1, keepdims=True))).reshape(tb, wd)

    return pl.pallas_call(
        body, name=name, grid=(t // tb,),
        in_specs=[pl.BlockSpec((tb, wd), lambda i: (i, col_blk)), _row_spec(tb, wd),
                  pl.BlockSpec((tb, wd), lambda i: (i, gb)), _vec_spec(HD)],
        out_specs=[_row_spec(tb, wd), _row_spec(tb, wd), _vec_spec(HD)],
        out_shape=[jax.ShapeDtypeStruct((t, wd), F32), jax.ShapeDtypeStruct((t, wd), BF16),
                   jax.ShapeDtypeStruct((1, HD), F32)],
        compiler_params=_cp(("arbitrary",)),
    )(dom, o, proj, nw)


def _tri(n, kind):
    r = lax.broadcasted_iota(jnp.int32, (n, n), 0)
    c = lax.broadcasted_iota(jnp.int32, (n, n), 1)
    if kind == "lower":
        return r >= c
    if kind == "strict":
        return r > c
    return r <= c


def _hg_gate(fl, l0, l1):
    mx = jnp.maximum(l0, l1)
    e0, e1 = jnp.exp(l0 - mx), jnp.exp(l1 - mx)
    lb = e0 / (e0 + e1)
    sg = _sigmoid(fl)
    f = lb + (1.0 - lb) * sg
    return lb, sg, f


def _hgrn2_fwd(proj, lb_logits, nh, name, comm=None):
    t = proj.shape[0]
    nc = t // CHUNK
    C = CHUNK
    lg = lb_logits.reshape(2, nh, 1, HD)

    hp = min(HP, nh)
    ng = nh // hp

    def one_head(hh, st, q_ref, f_ref, i_ref, lg_ref, p_sc, r_sc):
        sl = slice(hh * HD, (hh + 1) * HD)
        q, v = q_ref[:, sl], i_ref[:, sl]
        _, _, f = _hg_gate(f_ref[:, sl], lg_ref[0, hh], lg_ref[1, hh])
        k = 1.0 - f
        low = _tri(C, "lower")
        b = _nn(low.astype(F32), jnp.log(f), HI)
        yield
        lane_c = lax.broadcasted_iota(jnp.int32, (SB, C), 1)
        lane_h = lax.broadcasted_iota(jnp.int32, (SB, HD), 1)
        row_h = lax.broadcasted_iota(jnp.int32, (SB, HD), 0)
        ones = jnp.ones((HD, HD), F32)

        for i in range(NSB):
            qi, ki, bi = q[SB * i:SB * (i + 1)], k[SB * i:SB * (i + 1)], b[SB * i:SB * (i + 1)]
            for s in range(SB):
                e = jnp.exp(jnp.minimum(bi - bi[s:s + 1], 0.0))
                p = jnp.where(row_h >= s, qi * ki[s:s + 1] * e, 0.0)
                p_sc[hh, pl.ds((i * SB + s) * SB, SB), :] = p
            yield
        r_sc[hh] = _nn(p_sc[hh], ones, HIGH)
        yield
        a_rows = []
        for i in range(NSB):
            acc = jnp.zeros((SB, HD), F32)
            for s in range(SB):
                acc = jnp.where(lane_h == SB * i + s, r_sc[hh, pl.ds((i * SB + s) * SB, SB), :], acc)
            acc = acc[:, :C]
            if i > 0:
                r = b[SB * i - 1:SB * i]
                bi = b[SB * i:SB * (i + 1)]
                qf = q[SB * i:SB * (i + 1)] * jnp.exp(bi - r)
                kf = k * jnp.exp(jnp.minimum(r - b, 0.0))
                acc = acc + jnp.where(lane_c < SB * i, _nt(qf, kf, HIGH), 0.0)
            a_rows.append(acc)
            yield
        a = jnp.concatenate(a_rows, axis=0)
        bl = b[C - 1:C, :]
        o = _nn(_bf(a), _bf(v)) + _nt(_bf(q * jnp.exp(b)), _bf(st))
        yield
        new_st = st * jnp.exp(bl) + _tn(_bf(v), _bf(k * jnp.exp(bl - b)))
        return o, a, new_st

    def body(*refs):
        c, hg = pl.program_id(0), pl.program_id(1)
        step = c * ng + hg
        ins, outs, scratch, comm_begin, comm_end = _comm_hooks(
            comm, refs, 4, 3, step == 0, step == (3 * nc * ng) // 4, step == nc * ng - 1)
        o_ref, a_ref, st_ref = outs
        s_sc, p_sc, r_sc = scratch
        comm_begin()

        @pl.when(c == 0)
        def _():
            for hh in range(hp):
                s_sc[hg * hp + hh] = jnp.zeros((HD, HD), F32)

        sts = [s_sc[hg * hp + hh] for hh in range(hp)]
        res = _interleave([one_head(hh, sts[hh], *ins, p_sc, r_sc) for hh in range(hp)])
        for hh in range(hp):
            o_ref[:, hh * HD:(hh + 1) * HD] = res[hh][0]
            a_ref[0, hh] = res[hh][1]
            st_ref[0, hh] = sts[hh]
            s_sc[hg * hp + hh] = res[hh][2]
        comm_end()

    blk = lambda off: pl.BlockSpec((C, hp * HD), lambda c, g: (c, off // hp + g))
    cn = comm.n if comm is not None else 0
    return pl.pallas_call(
        body, name=name, grid=(nc, ng),
        in_specs=[blk(0), blk(nh), blk(2 * nh),
                  pl.BlockSpec((2, hp, 1, HD), lambda c, g: (0, g, 0, 0))] + [ANY] * cn,
        out_specs=[blk(0),
                   pl.BlockSpec((1, hp, C, C), lambda c, g: (c, g, 0, 0)),
                   pl.BlockSpec((1, hp, HD, HD), lambda c, g: (c, g, 0, 0))] + [ANY] * cn,
        out_shape=[jax.ShapeDtypeStruct((t, nh * HD), F32),
                   jax.ShapeDtypeStruct((nc, nh, C, C), F32),
                   jax.ShapeDtypeStruct((nc, nh, HD, HD), F32)] + (comm.out_shapes() if cn else []),
        scratch_shapes=[pltpu.VMEM((nh, HD, HD), F32), pltpu.VMEM((hp, C * SB, HD), F32),
                        pltpu.VMEM((hp, C * SB, HD), F32)] + (comm.scratch() if cn else []),
        compiler_params=_cp(("arbitrary", "arbitrary")),
    )(proj, proj, proj, lg, *(comm.arrays if cn else []))


def _hgrn2_bwd(proj, lb_logits, do, a_sv, st_sv, nh, name, comm=None):
    t = proj.shape[0]
    nc = t // CHUNK
    C = CHUNK
    lg = lb_logits.reshape(2, nh, 1, HD)
    hp = min(HP, nh)
    ng = nh // hp

    def one_head(hh, dst, q_ref, f_ref, i_ref, lg_ref, do_ref, a_ref, st_ref, p_sc, r_sc):
        sl = slice(hh * HD, (hh + 1) * HD)
        q, v, do_ = q_ref[:, sl], i_ref[:, sl], do_ref[:, sl]
        lb, sg, f = _hg_gate(f_ref[:, sl], lg_ref[0, hh], lg_ref[1, hh])
        k = 1.0 - f
        low = _tri(C, "lower")
        b = _nn(low.astype(F32), jnp.log(f), HI)
        yield
        bl = b[C - 1:C, :]
        eb, ekb = jnp.exp(b), jnp.exp(bl - b)
        qb, kb = q * eb, k * ekb
        a, st = a_ref[0, hh], st_ref[0, hh]

        da = jnp.where(low, _nt(_bf(do_), _bf(v)), 0.0)
        yield
        dv = _tn(_bf(a), _bf(do_)) + _nt(_bf(kb), _bf(dst))
        yield
        dqb = _nn(_bf(do_), _bf(st))
        dkb = _nn(_bf(v), _bf(dst))
        yield

        row = lax.broadcasted_iota(jnp.int32, (C, HD), 0)
        lane_c = lax.broadcasted_iota(jnp.int32, (SB, C), 1)
        row_h = lax.broadcasted_iota(jnp.int32, (SB, HD), 0)
        ones = jnp.ones((HD, HD), F32)
        sel = (lax.broadcasted_iota(jnp.int32, (C, C * SB), 0)
               == jnp.right_shift(lax.broadcasted_iota(jnp.int32, (C, C * SB), 1), SB.bit_length() - 1)).astype(F32)

        for i in range(NSB):
            doi, vi = do_[SB * i:SB * (i + 1)], v[SB * i:SB * (i + 1)]
            for s in range(SB):
                p_sc[hh, pl.ds((i * SB + s) * SB, SB), :] = doi * vi[s:s + 1]
            yield
        r_sc[hh] = _nn(p_sc[hh], ones, HIGH)
        yield
        dq_rows = []
        dk_off = jnp.zeros((C, HD), F32)
        for i in range(NSB):
            qi, ki, bi = q[SB * i:SB * (i + 1)], k[SB * i:SB * (i + 1)], b[SB * i:SB * (i + 1)]
            acc = jnp.zeros((SB, HD), F32)
            for s in range(SB):
                e = jnp.exp(jnp.minimum(bi - bi[s:s + 1], 0.0))
                g = jnp.where(row_h >= s, r_sc[hh, pl.ds((i * SB + s) * SB, SB), :] * e, 0.0)
                acc = acc + g * ki[s:s + 1]
                p_sc[hh, pl.ds((i * SB + s) * SB, SB), :] = g * qi
            yield
            if i > 0:
                r = b[SB * i - 1:SB * i]
                fq = jnp.exp(bi - r)
                fk = jnp.exp(jnp.minimum(r - b, 0.0))
                dai = jnp.where(lane_c < SB * i, da[SB * i:SB * (i + 1)], 0.0)
                acc = acc + _nn(dai, k * fk, HIGH) * fq
                dk_off = dk_off + _tn(dai, qi * fq, HIGH) * fk
                yield
            dq_rows.append(acc)
        dqi = jnp.concatenate(dq_rows, axis=0)
        dq = dqi + dqb * eb
        dk_inter = dkb * ekb
        dk = _nn(sel, p_sc[hh], HIGH) + dk_off + dk_inter
        yield
        db = q * dq - k * dk
        extra = (jnp.sum(k * dk_inter, axis=0, keepdims=True)
                 + jnp.exp(bl) * jnp.sum(dst * st, axis=0, keepdims=True))
        db = db + jnp.where(row == C - 1, extra, 0.0)
        dlf = _nn(_tri(C, "upper").astype(F32), db, HI)
        yield
        df = dlf / f - dk
        dfl = (df * (1.0 - lb) * sg * (1.0 - sg)).astype(BF16)
        dl = jnp.sum(df * (1.0 - sg), axis=0, keepdims=True) * (lb * (1.0 - lb))
        new_dst = dst * jnp.exp(bl) + _tn(_bf(do_), _bf(qb))
        return dq.astype(BF16), dfl, dv.astype(BF16), dl, new_dst

    def body(*refs):
        c, hg = pl.program_id(0), pl.program_id(1)
        step = c * ng + hg
        ins, outs, scratch, comm_begin, comm_end = _comm_hooks(
            comm, refs, 7, 4, step == 0, step == (3 * nc * ng) // 4, step == nc * ng - 1)
        dq_ref, df_ref, di_ref, dl_ref = outs
        ds_sc, p_sc, r_sc = scratch
        comm_begin()

        @pl.when(c == 0)
        def _():
            for hh in range(hp):
                ds_sc[hg * hp + hh] = jnp.zeros((HD, HD), F32)

        @pl.when(step == 0)
        def _():
            dl_ref[...] = jnp.zeros_like(dl_ref)

        dsts = [ds_sc[hg * hp + hh] for hh in range(hp)]
        res = _interleave([one_head(hh, dsts[hh], *ins, p_sc, r_sc) for hh in range(hp)])
        for hh in range(hp):
            sl = slice(hh * HD, (hh + 1) * HD)
            dq_ref[:, sl], df_ref[:, sl], di_ref[:, sl] = res[hh][0], res[hh][1], res[hh][2]
            dl_ref[pl.ds(hg * hp + hh, 1), :] += res[hh][3]
            ds_sc[hg * hp + hh] = res[hh][4]
        comm_end()

    rblk = lambda off: pl.BlockSpec((C, hp * HD), lambda c, g: (nc - 1 - c, off // hp + g))
    oblk = pl.BlockSpec((C, hp * HD), lambda c, g: (nc - 1 - c, g))
    cn = comm.n if comm is not None else 0
    return pl.pallas_call(
        body, name=name, grid=(nc, ng),
        in_specs=[rblk(0), rblk(nh), rblk(2 * nh),
                  pl.BlockSpec((2, hp, 1, HD), lambda c, g: (0, g, 0, 0)),
                  oblk,
                  pl.BlockSpec((1, hp, C, C), lambda c, g: (nc - 1 - c, g, 0, 0)),
                  pl.BlockSpec((1, hp, HD, HD), lambda c, g: (nc - 1 - c, g, 0, 0))] + [ANY] * cn,
        out_specs=[oblk, oblk, oblk, pl.BlockSpec((nh, HD), lambda c, g: (0, 0))] + [ANY] * cn,
        out_shape=[jax.ShapeDtypeStruct((t, nh * HD), BF16)] * 3 + [jax.ShapeDtypeStruct((nh, HD), F32)]
        + (comm.out_shapes() if cn else []),
        scratch_shapes=[pltpu.VMEM((nh, HD, HD), F32), pltpu.VMEM((hp, C * SB, HD), F32),
                        pltpu.VMEM((hp, C * SB, HD), F32)] + (comm.scratch() if cn else []),
        compiler_params=_cp(("arbitrary", "arbitrary")),
    )(proj, proj, proj, lg, do, a_sv, st_sv, *(comm.arrays if cn else []))


def _shift_rows(u, d, row):
    t = u.shape[0]
    if d == 0:
        return u
    rolled = pltpu.roll(u, d % t, 0)
    if d > 0:
        return jnp.where(row >= d, rolled, 0.0)
    return jnp.where(row < t + d, rolled, 0.0)


def _gdn_prep(proj, conv_w, blk0, nh, name):
    t = proj.shape[0]
    scale = HD ** -0.5

    def body(u_ref, w_ref, o_ref):
        j = pl.program_id(0)
        u, w = u_ref[...], w_ref[...]
        row = lax.broadcasted_iota(jnp.int32, (t, HD), 0)
        y = w[CONV_K - 1:CONV_K, :] * u
        for d in range(1, CONV_K):
            y = y + w[CONV_K - 1 - d:CONV_K - d, :] * _shift_rows(u, d, row)
        a = y * _sigmoid(y)
        n = a * lax.rsqrt(jnp.sum(a * a, axis=-1, keepdims=True) + EPS)
        n = n * jnp.where(j < nh, scale, 1.0)
        o_ref[...] = jnp.where(j < 2 * nh, n, a)

    return pl.pallas_call(
        body, name=name, grid=(3 * nh,),
        in_specs=[pl.BlockSpec((t, HD), lambda j: (0, blk0 + j)), pl.BlockSpec((CONV_K, HD), lambda j: (0, j))],
        out_specs=pl.BlockSpec((t, HD), lambda j: (0, j)),
        out_shape=jax.ShapeDtypeStruct((t, 3 * nh * HD), F32),
        compiler_params=_cp(("parallel",)),
    )(proj, conv_w)


def _gdn_prep_bwd(proj, conv_w, dq, dk, dv, blk0, nh, name):
    t = proj.shape[0]
    scale = HD ** -0.5

    def body(u_ref, w_ref, dq_ref, dk_ref, dv_ref, du_ref, dw_ref):
        j = pl.program_id(0)
        u, w = u_ref[...], w_ref[...]
        dout = jnp.where(j < nh, dq_ref[...], jnp.where(j < 2 * nh, dk_ref[...], dv_ref[...]))
        row = lax.broadcasted_iota(jnp.int32, (t, HD), 0)
        us = [_shift_rows(u, d, row) for d in range(CONV_K)]
        y = w[CONV_K - 1:CONV_K, :] * us[0]
        for d in range(1, CONV_K):
            y = y + w[CONV_K - 1 - d:CONV_K - d, :] * us[d]
        sg = _sigmoid(y)
        a = y * sg
        rs = lax.rsqrt(jnp.sum(a * a, axis=-1, keepdims=True) + EPS)
        n = a * rs
        dn = dout * jnp.where(j < nh, scale, 1.0)
        da_n = rs * (dn - n * jnp.sum(dn * n, axis=-1, keepdims=True))
        da = jnp.where(j < 2 * nh, da_n, dout)
        dy = da * (sg * (1.0 + y * (1.0 - sg)))
        du = w[CONV_K - 1:CONV_K, :] * dy
        for d in range(1, CONV_K):
            du = du + w[CONV_K - 1 - d:CONV_K - d, :] * _shift_rows(dy, -d, row)
        du_ref[...] = du.astype(BF16)
        for d in range(CONV_K):
            dw_ref[CONV_K - 1 - d:CONV_K - d, :] = jnp.sum(dy * us[d], axis=0, keepdims=True)

    return pl.pallas_call(
        body, name=name, grid=(3 * nh,),
        in_specs=[pl.BlockSpec((t, HD), lambda j: (0, blk0 + j)), pl.BlockSpec((CONV_K, HD), lambda j: (0, j))]
        + [pl.BlockSpec((t, HD), functools.partial(lambda p, j: (0, jnp.clip(j - p * nh, 0, nh - 1)), p))
           for p in range(3)],
        out_specs=[pl.BlockSpec((t, HD), lambda j: (0, j)), pl.BlockSpec((CONV_K, HD), lambda j: (0, j))],
        out_shape=[jax.ShapeDtypeStruct((t, 3 * nh * HD), BF16), jax.ShapeDtypeStruct((CONV_K, 3 * nh * HD), F32)],
        compiler_params=_cp(("arbitrary",)),
    )(proj, conv_w, dq, dk, dv)


def _gdn_gates(ab, alog, dtb, h, nh):
    lane = lax.broadcasted_iota(jnp.int32, ab.shape, 1)
    x = ab + dtb
    sp = jnp.maximum(x, 0.0) + jnp.log(1.0 + jnp.exp(-jnp.abs(x)))
    ea = jnp.exp(alog)
    la_all = -ea * sp
    beta_all = _sigmoid(ab)
    pick = lambda val, ln: jnp.sum(jnp.where(lane == ln, val, 0.0), axis=1, keepdims=True)
    la = pick(la_all, h)
    beta = pick(beta_all, nh + h)
    dla_da = pick(-ea * _sigmoid(x), h)
    return la, beta, dla_da


def _unit_lower_inverses(ms, C):
    nb = C // SB
    sh = SB.bit_length() - 1
    rowb = jnp.right_shift(lax.broadcasted_iota(jnp.int32, (C, C), 0), sh)
    colb = jnp.right_shift(lax.broadcasted_iota(jnp.int32, (C, C), 1), sh)
    eye = (lax.broadcasted_iota(jnp.int32, (SB, SB), 0) == lax.broadcasted_iota(jnp.int32, (SB, SB), 1)).astype(F32)
    spread = (jnp.bitwise_and(lax.broadcasted_iota(jnp.int32, (SB, C), 1), SB - 1)
              == lax.broadcasted_iota(jnp.int32, (SB, C), 0)).astype(F32)
    blocks = [[m[SB * i:SB * (i + 1), SB * i:SB * (i + 1)] for i in range(nb)] for m in ms]
    xs = [[eye] * nb for _ in ms]
    for s in range(SB - 1):
        xs = [[x - b[:, s:s + 1] * x[s:s + 1, :] for x, b in zip(xh, bh)] for xh, bh in zip(xs, blocks)]
    ts = [jnp.where(rowb == colb, _nn(jnp.concatenate(xh, axis=0), spread, HIGH), 0.0) for xh in xs]
    lvl = 1
    while (1 << lvl) <= nb:
        off = ((jnp.right_shift(rowb, lvl) == jnp.right_shift(colb, lvl))
               & (jnp.right_shift(rowb, lvl - 1) != jnp.right_shift(colb, lvl - 1)))
        ts = [t - _nn(t, _nn(jnp.where(off, m, 0.0), t, HIGH), HIGH) for t, m in zip(ts, ms)]
        lvl += 1
    return ts


def _gdn_chunks(qs, ks, vs, las, betas, C):
    low, strict = _tri(C, "lower"), _tri(C, "strict")
    eye = (lax.broadcasted_iota(jnp.int32, (C, C), 0) == lax.broadcasted_iota(jnp.int32, (C, C), 1)).astype(F32)
    g_bs = [_nn(low.astype(F32), jnp.broadcast_to(la, (C, HD)), HI) for la in las]
    ps = [_nt(k, k, HIGH) for k in ks]
    qks = [_nt(_bf(q), _bf(k)) for q, k in zip(qs, ks)]
    chs = []
    for g_b, p, qk_raw, beta in zip(g_bs, ps, qks, betas):
        g_c = g_b[:, :C]
        gamma = jnp.where(low, jnp.exp(jnp.minimum(g_c - g_c.T, 0.0)), 0.0)
        gl = g_b[C - 1:C, :]
        chs.append(dict(gamma=gamma, eg=jnp.exp(g_b), gl=gl, ekt=jnp.exp(gl - g_b), p=p,
                        m=jnp.where(strict, beta * p * gamma, 0.0), qk_raw=qk_raw))
    xs = _unit_lower_inverses([ch["m"] for ch in chs], C)
    r_ws = [k * (beta * ch["eg"]) for ch, k, beta in zip(chs, ks, betas)]
    uws = [_nn(x, jnp.concatenate([v * beta, r_w], axis=1), HIGH) for x, v, beta, r_w in zip(xs, vs, betas, r_ws)]
    for ch, x, r_w, uw in zip(chs, xs, r_ws, uws):
        ch.update(x=x, r_w=r_w, uw=uw)
    return chs


def _gdn_fwd(qkv, proj, ab_blk, alog, dtb, nh, name, comm=None):
    t = qkv.shape[0]
    nc = t // CHUNK
    C = CHUNK
    hp = min(HP, nh)
    ng = nh // hp

    def body(*refs):
        c, hg = pl.program_id(0), pl.program_id(1)
        step = c * ng + hg
        ins, outs, scratch, comm_begin, comm_end = _comm_hooks(
            comm, refs, 6, 3, step == 0, step == (3 * nc * ng) // 4, step == nc * ng - 1)
        q_ref, k_ref, v_ref, ab_ref, al_ref, dt_ref = ins
        o_ref, x_ref, st_ref = outs
        s_sc, = scratch
        comm_begin()

        @pl.when(c == 0)
        def _():
            for hh in range(hp):
                s_sc[hg * hp + hh] = jnp.zeros((HD, HD), F32)

        sls = [slice(hh * HD, (hh + 1) * HD) for hh in range(hp)]
        qs, ks, vs = [q_ref[:, sl] for sl in sls], [k_ref[:, sl] for sl in sls], [v_ref[:, sl] for sl in sls]
        sts = [s_sc[hg * hp + hh] for hh in range(hp)]
        gates = [_gdn_gates(ab_ref[...], al_ref[...], dt_ref[...], hg * hp + hh, nh) for hh in range(hp)]
        chs = _gdn_chunks(qs, ks, vs, [g[0] for g in gates], [g[1] for g in gates], C)
        stbs = [_bf(st) for st in sts]
        vns = [ch["uw"][:, :HD] - _nt(_bf(ch["uw"][:, HD:]), stb) for ch, stb in zip(chs, stbs)]
        o_st = [_nt(_bf(q * ch["eg"]), stb) for q, ch, stb in zip(qs, chs, stbs)]
        outs_ = [o + _nn(_bf(ch["qk_raw"] * ch["gamma"]), _bf(vn)) for o, ch, vn in zip(o_st, chs, vns)]
        new_sts = [st * jnp.exp(ch["gl"]) + _tn(_bf(vn), _bf(k * ch["ekt"]))
                   for st, ch, vn, k in zip(sts, chs, vns, ks)]
        for hh in range(hp):
            o_ref[:, sls[hh]] = outs_[hh]
            x_ref[0, hh] = chs[hh]["x"]
            st_ref[0, hh] = sts[hh]
            s_sc[hg * hp + hh] = new_sts[hh]
        comm_end()

    blk = lambda off: pl.BlockSpec((C, hp * HD), lambda c, g: (c, off // hp + g))
    vec = pl.BlockSpec((1, HD), lambda c, g: (0, 0))
    cn = comm.n if comm is not None else 0
    return pl.pallas_call(
        body, name=name, grid=(nc, ng),
        in_specs=[blk(0), blk(nh), blk(2 * nh), pl.BlockSpec((C, HD), lambda c, g: (c, ab_blk)), vec, vec]
        + [ANY] * cn,
        out_specs=[blk(0),
                   pl.BlockSpec((1, hp,C, C), lambda c, g: (c, g, 0, 0)),
                   pl.BlockSpec((1, hp,HD, HD), lambda c, g: (c, g, 0, 0))] + [ANY] * cn,
        out_shape=[jax.ShapeDtypeStruct((t, nh * HD), F32),
                   jax.ShapeDtypeStruct((nc, nh, C, C), F32),
                   jax.ShapeDtypeStruct((nc, nh, HD, HD), F32)] + (comm.out_shapes() if cn else []),
        scratch_shapes=[pltpu.VMEM((nh, HD, HD), F32)] + (comm.scratch() if cn else []),
        compiler_params=_cp(("arbitrary", "arbitrary")),
    )(qkv, qkv, qkv, proj, alog, dtb, *(comm.arrays if cn else []))


def _gdn_bwd(qkv, proj, ab_blk, alog, dtb, do, x_sv, st_sv, nh, name, comm=None):
    t = qkv.shape[0]
    nc = t // CHUNK
    C = CHUNK
    hp = min(HP, nh)
    ng = nh // hp

    def one_head(h, hh, dst, q_ref, k_ref, v_ref, ab_ref, al_ref, dt_ref, do_ref, x_ref, st_ref):
        sl = slice(hh * HD, (hh + 1) * HD)
        q, k, v, do_ = q_ref[:, sl], k_ref[:, sl], v_ref[:, sl], do_ref[:, sl]
        la, beta, dla_da = _gdn_gates(ab_ref[...], al_ref[...], dt_ref[...], h, nh)
        low, strict = _tri(C, "lower"), _tri(C, "strict")
        g_b = _nn(low.astype(F32), jnp.broadcast_to(la, (C, HD)), HI)
        yield
        g_c = g_b[:, :C]
        gamma = jnp.where(low, jnp.exp(jnp.minimum(g_c - g_c.T, 0.0)), 0.0)
        eg = jnp.exp(g_b)
        gl = g_b[C - 1:C, :]
        ekt = jnp.exp(gl - g_b)
        egl = jnp.exp(gl)
        p = _nt(k, k, HIGH)
        yield
        x = x_ref[0, hh]
        r_w = k * (beta * eg)
        rhs = jnp.concatenate([v * beta, r_w], axis=1)
        uw = _nn(x, rhs, HIGH)
        yield
        u, w = uw[:, :HD], uw[:, HD:]
        qk_raw = _nt(_bf(q), _bf(k))
        yield
        qk = qk_raw * gamma
        st = st_ref[0, hh]
        stb, dstb = _bf(st), _bf(dst)
        vn = u - _nt(_bf(w), stb)
        yield
        qd, kt = q * eg, k * ekt

        dvn = _tn(_bf(qk), _bf(do_)) + _nt(_bf(kt), dstb)
        yield
        dq2 = jnp.where(low, _nt(_bf(do_), _bf(vn)), 0.0)
        yield
        dqd = _nn(_bf(do_), stb)
        yield
        dkt = _nn(_bf(vn), dstb)
        yield
        dw = -_nn(_bf(dvn), stb)
        yield
        dxx = jnp.concatenate([dvn, dw], axis=1)
        dr = _tn(x, dxx, HIGH)
        yield
        dm = -jnp.where(strict, _nt(dr, uw, HIGH), 0.0)
        yield
        dr_u, dr_w = dr[:, :HD], dr[:, HD:]
        rsum = lambda z: jnp.sum(z, axis=1, keepdims=True)

        dv = dr_u * beta
        dmg = dm * gamma
        dbeta = rsum(dr_u * v) + rsum(dr_w * k) * eg[:, :1] + rsum(dmg * p)
        yield
        dp = dmg * beta
        dq2g = dq2 * gamma
        dk = (dr_w * (beta * eg) + dkt * ekt + _tn(_bf(dq2g), _bf(q))
              + _nn(_bf(dp + dp.T), _bf(k)))
        yield
        dq = dqd * eg + _nn(_bf(dq2g), _bf(k))
        yield
        e = dp * p + dq2g * qk_raw
        t_kt = rsum(dkt * kt)
        dg = rsum(dqd * qd) + rsum(dr_w * r_w) - t_kt + rsum(e) - rsum(e.T)
        yield
        dgl = jnp.sum(t_kt, axis=0, keepdims=True) + jnp.sum(dst * st, keepdims=True) * egl[:, :1]
        rowc = lax.broadcasted_iota(jnp.int32, (C, 1), 0)
        dg = dg + jnp.where(rowc == C - 1, dgl, 0.0)
        dla = _nn(_tri(C, "upper").astype(F32), jnp.broadcast_to(dg, (C, HD)), HI)[:, :1]
        yield
        da = dla * dla_da
        db = dbeta * beta * (1.0 - beta)
        lane = lax.broadcasted_iota(jnp.int32, (C, HD), 1)
        dab = jnp.where(lane == h, da, 0.0) + jnp.where(lane == nh + h, db, 0.0)
        lane1 = lax.broadcasted_iota(jnp.int32, (1, HD), 1)
        d_alog = jnp.where(lane1 == h, jnp.sum(dla * la, axis=0, keepdims=True), 0.0)
        d_dtb = jnp.where(lane1 == h, jnp.sum(da, axis=0, keepdims=True), 0.0)
        new_dst = dst * egl + _tn(_bf(do_), _bf(qd)) - _tn(_bf(dvn), _bf(w))
        return dab, d_alog, d_dtb, new_dst, dq, dk, dv

    def body(*refs):
        c, hg = pl.program_id(0), pl.program_id(1)
        step = c * ng + hg
        ins, outs, scratch, comm_begin, comm_end = _comm_hooks(
            comm, refs, 9, 5, step == 0, step == (3 * nc * ng) // 4, step == nc * ng - 1)
        dq_ref, dk_ref, dv_ref, dab_ref, dpar_ref = outs
        ds_sc, = scratch
        comm_begin()

        @pl.when(c == 0)
        def _():
            for hh in range(hp):
                ds_sc[hg * hp + hh] = jnp.zeros((HD, HD), F32)

        @pl.when(step == 0)
        def _():
            dpar_ref[...] = jnp.zeros_like(dpar_ref)

        @pl.when(hg == 0)
        def _():
            dab_ref[...] = jnp.zeros_like(dab_ref)

        dsts = [ds_sc[hg * hp + hh] for hh in range(hp)]
        res = _interleave([one_head(hg * hp + hh, hh, dsts[hh], *ins) for hh in range(hp)])
        for hh in range(hp):
            sl = slice(hh * HD, (hh + 1) * HD)
            ds_sc[hg * hp + hh] = res[hh][3]
            dq_ref[:, sl], dk_ref[:, sl], dv_ref[:, sl] = res[hh][4], res[hh][5], res[hh][6]
        dab_ref[...] += sum(r[0] for r in res[1:]) + res[0][0]
        dpar_ref[0:1, :] += sum(r[1] for r in res[1:]) + res[0][1]
        dpar_ref[1:2, :] += sum(r[2] for r in res[1:]) + res[0][2]
        comm_end()

    rblk = lambda off: pl.BlockSpec((C, hp * HD), lambda c, g: (nc - 1 - c, off // hp + g))
    oblk = pl.BlockSpec((C, hp * HD), lambda c, g: (nc - 1 - c, g))
    vec = pl.BlockSpec((1, HD), lambda c, g: (0, 0))
    cn = comm.n if comm is not None else 0
    return pl.pallas_call(
        body, name=name, grid=(nc, ng),
        in_specs=[rblk(0), rblk(nh), rblk(2 * nh),
                  pl.BlockSpec((C, HD), lambda c, g: (nc - 1 - c, ab_blk)), vec, vec, oblk,
                  pl.BlockSpec((1, hp,C, C), lambda c, g: (nc - 1 - c, g, 0, 0)),
                  pl.BlockSpec((1, hp,HD, HD), lambda c, g: (nc - 1 - c, g, 0, 0))] + [ANY] * cn,
        out_specs=[oblk, oblk, oblk,
                   pl.BlockSpec((C, HD), lambda c, g: (nc - 1 - c, 0)),
                   pl.BlockSpec((8, HD), lambda c, g: (0, 0))] + [ANY] * cn,
        out_shape=[jax.ShapeDtypeStruct((t, nh * HD), F32)] * 3
        + [jax.ShapeDtypeStruct((t, HD), F32), jax.ShapeDtypeStruct((8, HD), F32)]
        + (comm.out_shapes() if cn else []),
        scratch_shapes=[pltpu.VMEM((nh, HD, HD), F32)] + (comm.scratch() if cn else []),
        compiler_params=_cp(("arbitrary", "arbitrary")),
    )(qkv, qkv, qkv, proj, alog, dtb, do, x_sv, st_sv, *(comm.arrays if cn else []))


def _ada_fwd(c_all, w, b, name):
    nb, d = c_all.shape
    n = w.shape[1]
    tn = _pick(n, 512)

    def body(c_ref, w_ref, b_ref, o_ref):
        cv = c_ref[...]
        o_ref[...] = _nn(cv * _sigmoid(cv), w_ref[...], HI) + b_ref[...]

    return pl.pallas_call(
        body, name=name, grid=(n // tn,),
        in_specs=[pl.BlockSpec((nb, d), lambda j: (0, 0)), pl.BlockSpec((d, tn), lambda j: (0, j)),
                  pl.BlockSpec((1, tn), lambda j: (0, j))],
        out_specs=pl.BlockSpec((nb, tn), lambda j: (0, j)),
        out_shape=jax.ShapeDtypeStruct((nb, n), F32),
        compiler_params=_cp(("parallel",)),
    )(c_all, w, b)


def _ada_wgrad(c_all, dmod, name):
    nb, d = c_all.shape
    n = dmod.shape[1]
    tn = _pick(n, 512)

    def body(c_ref, g_ref, o_ref):
        cv = c_ref[...]
        o_ref[...] = _tn(cv * _sigmoid(cv), g_ref[...], HI)

    return pl.pallas_call(
        body, name=name, grid=(n // tn,),
        in_specs=[pl.BlockSpec((nb, d), lambda j: (0, 0)), pl.BlockSpec((nb, tn), lambda j: (0, j))],
        out_specs=pl.BlockSpec((d, tn), lambda j: (0, j)),
        out_shape=jax.ShapeDtypeStruct((d, n), F32),
        compiler_params=_cp(("parallel",)),
    )(c_all, dmod)


def _adamw(w, m, v, g, name, parts=False):
    lead = w.ndim == 3
    r, cdim = w.shape[-2:]
    cap = max(SUBLANES, ADAM_BLOCK_ELEMS // cdim // SUBLANES * SUBLANES)
    tr = r if r <= cap else _pick_rows(r, cap)
    bc1 = 1.0 - ADAM_B1 ** ADAM_STEP
    bc2 = 1.0 - ADAM_B2 ** ADAM_STEP

    glist = list(g) if isinstance(g, (list, tuple)) else [g]
    bounds = [0]
    for ga in glist:
        bounds.append(bounds[-1] + ga.shape[-2] // tr)

    def body(w_ref, m_ref, v_ref, *rest):
        g_refs, (go_ref, d_ref, mo_ref, vo_ref) = rest[:len(glist)], rest[len(glist):]
        if parts:
            sums = []
            for g_ref in g_refs:
                gv = g_ref[0].astype(F32)
                for s in range(1, N_DEV):
                    gv = gv + g_ref[s].astype(F32)
                sums.append(gv)
            gv = sums[-1]
            for p in range(len(sums) - 2, -1, -1):
                gv = jnp.where(pl.program_id(0) < bounds[p + 1], sums[p], gv)
        else:
            gv = g_refs[0][...]
        wv = w_ref[...]
        mn = ADAM_B1 * m_ref[...] + (1.0 - ADAM_B1) * gv
        vn = ADAM_B2 * v_ref[...] + (1.0 - ADAM_B2) * (gv * gv)
        m_hat = mn / bc1
        v_hat = vn / bc2
        go_ref[...] = gv
        d_ref[...] = -ADAM_LR * (m_hat / (jnp.sqrt(v_hat) + ADAM_EPS) + ADAM_WD * wv)
        mo_ref[...] = mn
        vo_ref[...] = vn

    flat = pl.BlockSpec((tr, cdim), lambda i: (i, 0))
    spec = pl.BlockSpec((None, tr, cdim), lambda i: (0, i, 0)) if lead else flat
    def piece_spec(p):
        lo, n = bounds[p], bounds[p + 1] - bounds[p]
        return pl.BlockSpec((N_DEV, tr, cdim), lambda i: (0, jnp.clip(i - lo, 0, n - 1), 0))

    gspecs = [piece_spec(p) for p in range(len(glist))] if parts else [flat]
    return pl.pallas_call(
        body, name=name, grid=(r // tr,),
        in_specs=[spec, spec, spec] + gspecs,
        out_specs=[spec] * 4,
        out_shape=[jax.ShapeDtypeStruct(w.shape, F32)] * 4,
        compiler_params=_cp(("arbitrary",)),
    )(w, m, v, *glist)


def _pick_rows(r, pref):
    t = pref
    while r % t:
        t -= 8
    assert t > 0
    return t


def _dev_index(x, y, c):
    return 4 * x + 2 * y + c


class _Comm:
    def __init__(self, kind, arrays):
        self.kind, self.n = kind, len(arrays)
        self.arrays = [a[0] if isinstance(a, tuple) else a for a in arrays]
        self.rows = [(a[1], a[2]) if isinstance(a, tuple) else None for a in arrays]

    def out_shapes(self):
        if self.kind == "gather":
            return [jax.ShapeDtypeStruct((N_DEV,) + a.shape, a.dtype) for a in self.arrays]
        return [jax.ShapeDtypeStruct(a.shape if r is None else (N_DEV, r[1]) + a.shape[2:], a.dtype)
                for a, r in zip(self.arrays, self.rows)]

    def scratch(self):
        return [pltpu.SemaphoreType.DMA((self.n, 7)), pltpu.SemaphoreType.DMA((self.n, 7)),
                pltpu.SemaphoreType.DMA((self.n,))]

    def _gather_parts(self, ins, outs, sems):
        send_sems, recv_sems, local_sems = sems
        x, y, c = lax.axis_index("x"), lax.axis_index("y"), lax.axis_index("c")
        me, sibling = (x, y, c), (x, y, 1 - c)
        chips = [(1 - x, y), (x, 1 - y), (1 - x, 1 - y)]

        def copy(a, k, block, to, src=None):
            slot = outs[a].at[_dev_index(*block)]
            return pltpu.make_async_remote_copy(
                src_ref=slot if src is None else src, dst_ref=slot,
                send_sem=send_sems.at[a, k], recv_sem=recv_sems.at[a, k],
                device_id=to, device_id_type=MESH)

        n = self.n
        mine = [pltpu.make_async_copy(ins[a], outs[a].at[_dev_index(*me)], local_sems.at[a]) for a in range(n)]
        first = []
        for a in range(n):
            first.append(copy(a, 0, me, sibling, src=ins[a]))
            first += [copy(a, 1 + j, me, (*chip, c), src=ins[a]) for j, chip in enumerate(chips)]
        landed = [copy(a, 1 + j, (*chip, c), me) for j, chip in enumerate(chips) for a in range(n)]
        passed = [copy(a, 4 + j, (*chip, c), sibling) for j, chip in enumerate(chips) for a in range(n)]
        late = []
        for a in range(n):
            late.append(copy(a, 0, sibling, me))
            late += [copy(a, 4 + j, (*chip, 1 - c), me) for j, chip in enumerate(chips)]
        return mine, first, landed, passed, late

    def _exchange_parts(self, ins, outs, sems):
        send_sems, recv_sems, local_sems = sems
        x, y, c = lax.axis_index("x"), lax.axis_index("y"), lax.axis_index("c")
        my = _dev_index(x, y, c)
        n = self.n

        def block(a, j):
            r = self.rows[a]
            return ins[a].at[j] if r is None else ins[a].at[j, pl.ds(r[0], r[1])]

        mine = [pltpu.make_async_copy(block(a, my), outs[a].at[my], local_sems.at[a]) for a in range(n)]
        sends, recvs = [], []
        for k in range(1, N_DEV):
            px = (1 - x) if (k >> 2) & 1 else x
            py = (1 - y) if (k >> 1) & 1 else y
            pc = (1 - c) if k & 1 else c
            peer = _dev_index(px, py, pc)
            for a in range(n):
                sends.append(pltpu.make_async_remote_copy(
                    src_ref=block(a, peer), dst_ref=outs[a].at[my],
                    send_sem=send_sems.at[a, k - 1], recv_sem=recv_sems.at[a, k - 1],
                    device_id=(px, py, pc), device_id_type=MESH))
                recvs.append(pltpu.make_async_remote_copy(
                    src_ref=block(a, my), dst_ref=outs[a].at[peer],
                    send_sem=send_sems.at[a, k - 1], recv_sem=recv_sems.at[a, k - 1],
                    device_id=(x, y, c), device_id_type=MESH))
        return mine, sends, recvs

    def start(self, ins, outs, sems):
        if self.kind == "gather":
            mine, first, _, _, _ = self._gather_parts(ins, outs, sems)
        else:
            mine, first, _ = self._exchange_parts(ins, outs, sems)
        for cp in mine + first:
            cp.start()

    def mid(self, ins, outs, sems):
        if self.kind == "gather":
            _, _, landed, passed, _ = self._gather_parts(ins, outs, sems)
            for got, fwd in zip(landed, passed):
                got.wait_recv()
                fwd.start()

    def finish(self, ins, outs, sems):
        if self.kind == "gather":
            mine, first, _, passed, late = self._gather_parts(ins, outs, sems)
            for cp in late:
                cp.wait_recv()
            for cp in first + passed:
                cp.wait_send()
        else:
            mine, sends, recvs = self._exchange_parts(ins, outs, sems)
            for cp in sends:
                cp.wait_send()
            for cp in recvs:
                cp.wait_recv()
        for cp in mine:
            cp.wait()

    def run(self, name):
        n = self.n

        def body(*refs):
            ins, outs, sems = refs[:n], refs[n:2 * n], refs[2 * n:]
            self.start(ins, outs, sems)
            self.mid(ins, outs, sems)
            self.finish(ins, outs, sems)

        return pl.pallas_call(
            body, name=name, in_specs=[ANY] * n, out_specs=[ANY] * n,
            out_shape=self.out_shapes(), scratch_shapes=self.scratch(),
        )(*self.arrays)


def _all_gather(arrays, name):
    return _Comm("gather", arrays).run(name)


def _comm_hooks(comm, refs, n_in, n_out, first, middle, last):
    cn = comm.n if comm is not None else 0
    ins, cins = refs[:n_in], refs[n_in:n_in + cn]
    outs, couts = refs[n_in + cn:n_in + cn + n_out], refs[n_in + cn + n_out:n_in + 2 * cn + n_out]
    rest = refs[n_in + 2 * cn + n_out:]
    scratch, csems = (rest[:len(rest) - 3], rest[len(rest) - 3:]) if cn else (rest, ())

    def begin():
        if cn:
            pl.when(first)(lambda: comm.start(cins, couts, csems))
            pl.when(middle)(lambda: comm.mid(cins, couts, csems))

    def end():
        if cn:
            pl.when(last)(lambda: comm.finish(cins, couts, csems))

    return ins, outs, scratch, begin, end


def _local_step(x, tgt, mod, n1, n2, n3, n4, w_in_p, lb_logits, hg_norm, conv_w, alog, dtb, gdn_norm,
                late_w, dist=None):
    t, d = x.shape
    nh = d // 2 // HD
    ab_blk = 8 * nh
    sh_m, sc_m, gt_m, sh_f, sc_f, gt_f = [mod[i:i + 1] for i in range(6)]

    h1, r1 = _prenorm(x, n1, sc_m, sh_m, "prenorm_mix")
    if dist is None:
        proj = _mm(h1, w_in_p, "nn", [F32], "mm_proj")
        o_hg, a_sv, hst_sv = _hgrn2_fwd(proj, lb_logits, nh, "hgrn2_fwd")
        qkv = _gdn_prep(proj, conv_w, 4 * nh, nh, "gdn_prep")
        o_gd, x_sv, gst_sv = _gdn_fwd(qkv, proj, ab_blk, alog, dtb, nh, "gdn_fwd")
        w_out, w_ff1, w_ff2 = late_w
        exch = lambda arrays: None
    else:
        proj, g_out = _mm(h1, w_in_p, "nn", [F32], "mm_proj", comm=_Comm("gather", late_w[:1]))
        o_hg, a_sv, hst_sv, g_ff2 = _hgrn2_fwd(proj, lb_logits, nh, "hgrn2_fwd",
                                               comm=_Comm("gather", late_w[2:]))
        qkv = _gdn_prep(proj, conv_w, 4 * nh, nh, "gdn_prep")
        o_gd, x_sv, gst_sv, g_ff1 = _gdn_fwd(qkv, proj, ab_blk, alog, dtb, nh, "gdn_fwd",
                                             comm=_Comm("gather", late_w[1:2]))
        w_out, w_ff1, w_ff2 = dist["assemble"](g_out, g_ff1, g_ff2)
        exch = lambda arrays: _Comm("exchange", arrays)
    om_hg = _headnorm_fwd(o_hg, proj, 3 * nh, hg_norm, "headnorm_hg")
    om_gd = _headnorm_fwd(o_gd, proj, 7 * nh, gdn_norm, "headnorm_gdn")
    om = jnp.concatenate([om_hg, om_gd], axis=1)
    y1 = _mm(om, w_out, "nn", [F32], "mm_out")
    x1, r2, h2, r3 = _postnorm_prenorm(x, y1, n2, gt_m, n3, sc_f, sh_f, "postnorm_mix_prenorm_ffn")

    def relu2(acc, extra, outs):
        rl = jnp.maximum(acc, 0.0)
        outs[0][...] = (rl * rl).astype(BF16)

    act = _mm(h2, w_ff1, "nn", [BF16], "mm_ff1", epilogue=relu2)
    y2 = _mm(act, w_ff2, "nn", [F32], "mm_ff2")
    dout, dy2, loss, dgt_f, dn4 = _final_loss_bwd(x1, y2, n4, gt_f, tgt, "final_loss_bwd")
    whole_t = dict(tk=t, tn=1024)
    dw_ff2 = _mm(act, dy2, "tn", [BF16], "mm_dw_ff2", **whole_t)

    def drelu2(acc, extra, outs):
        outs[0][...] = (acc * (2.0 * jnp.sqrt(extra[0][...].astype(F32)))).astype(BF16)

    recv = {}
    ff2a, ff2b = dist["parts_ff2"](dw_ff2) if dist else (None, None)
    du, *recv["ff2a"] = _listed(_mm(dy2, w_ff2, "nt", [BF16], "mm_da", epilogue=drelu2, extras=(act,),
                                    comm=exch([ff2a])))
    ff1_cols = dict(by_cols=True, tk=t, tn=dist["n_ff"]) if dist else whole_t
    dw_ff1, *recv["ff2b"] = _listed(_mm(h2, du, "tn", [BF16], "mm_dw_ff1", comm=exch([ff2b]), **ff1_cols))
    ff1a, ff1b = dist["parts_ff1"](dw_ff1) if dist else (None, None)
    dh2, *recv["ff1a"] = _listed(_mm(du, w_ff1, "nt", [F32], "mm_dh2", comm=exch([ff1a])))
    dx1, dy1, dsh_f, dsc_f, dn3, dgt_m, dn2 = _prenorm_postnorm_bwd(
        dh2, x1, r3, n3, sc_f, dout, y1, r2, n2, gt_m, "prenorm_ffn_postnorm_mix_bwd")

    dw_out = _mm(om, dy1, "tn", [BF16], "mm_dw_out", **whole_t)
    dom = _mm(dy1, w_out, "nt", [F32], "mm_dom")
    do_hg, dg_hg, dhgn = _headnorm_bwd(dom, 0, o_hg, proj, 3 * nh, hg_norm, "headnorm_hg_bwd")
    do_gd, dg_gd, dgdn = _headnorm_bwd(dom, 1, o_gd, proj, 7 * nh, gdn_norm, "headnorm_gdn_bwd")
    p_out = dist["parts_out"](dw_out) if dist else None
    dq_hg, df_hg, di_hg, dl0, *recv["ff1b_out"] = _hgrn2_bwd(proj, lb_logits, do_hg, a_sv, hst_sv, nh,
                                                             "hgrn2_bwd", comm=exch([ff1b, p_out]))
    dq_g, dk_g, dv_g, dab, dpar = _gdn_bwd(qkv, proj, ab_blk, alog, dtb, do_gd, x_sv, gst_sv, nh, "gdn_bwd")
    du_conv, dconv = _gdn_prep_bwd(proj, conv_w, dq_g, dk_g, dv_g, 4 * nh, nh, "gdn_prep_bwd")
    dproj = jnp.concatenate([dq_hg, df_hg, di_hg, dg_hg, du_conv, dg_gd, dab.astype(BF16)], axis=1)
    if dist is None:
        dw_in = _mm(h1, dproj, "tn", [BF16], "mm_dw_in")
        dh1 = _mm(dproj, w_in_p, "nt", [F32], "mm_dh1", tk=1664)
    else:
        q4 = d // 4
        dw_in_a = _mm(h1[:, :q4], dproj, "tn", [BF16], "mm_dw_in_a")
        dw_in_b, in_a = _mm(h1[:, q4:2 * q4], dproj, "tn", [BF16], "mm_dw_in_b",
                            comm=exch([dist["parts_in"](dw_in_a)]))
        dw_in_c, in_b = _mm(h1[:, 2 * q4:], dproj, "tn", [BF16], "mm_dw_in_c",
                            comm=exch([dist["parts_in"](dw_in_b)]))
        dh1, in_c = _mm(dproj, w_in_p, "nt", [F32], "mm_dh1", tk=1664, comm=exch([dist["parts_in"](dw_in_c)]))
        recv["in"] = [in_a, in_b, in_c]
        dw_in = None
    dx, dsh_m, dsc_m, dn1 = _prenorm_bwd(dh1, x, r1, n1, sc_m, dx1, "prenorm_mix_bwd")

    dmod = jnp.concatenate([dsh_m, dsc_m, dgt_m, dsh_f, dsc_f, dgt_f], axis=0)
    grads = dict(dmod=dmod, n1=dn1, n2=dn2, n3=dn3, n4=dn4, w_in=dw_in, lb0=dl0, hg_norm=dhgn, conv=dconv,
                 alog=dpar[0:1], dtb=dpar[1:2], gdn_norm=dgdn, w_out=dw_out, w_ff1=dw_ff1, w_ff2=dw_ff2,
                 recv=recv)
    return loss, dx, grads


def _pack(vals):
    rows = []
    for vv in vals:
        flat = vv.reshape(-1)
        flat = jnp.pad(flat, (0, (-flat.shape[0]) % (SUBLANES * LANES)))
        rows.append(flat.reshape(-1, LANES))
    return jnp.concatenate(rows, axis=0)


def _unpack(packed, shapes):
    out, r = [], 0
    for shp in shapes:
        size = 1
        for s in shp:
            size *= s
        nr = -(-size // (SUBLANES * LANES)) * SUBLANES
        out.append(packed[r:r + nr].reshape(-1)[:size].reshape(shp))
        r += nr
    return out


def _sum_parts(parts, name):
    _, r, cdim = parts.shape

    def body(p_ref, o_ref):
        acc = p_ref[0]
        for s in range(1, N_DEV):
            acc = acc + p_ref[s]
        o_ref[...] = acc

    return pl.pallas_call(
        body, name=name,
        out_shape=jax.ShapeDtypeStruct((r, cdim), F32),
        compiler_params=_cp(),
    )(parts)


def kernel(x, c, w_ada, b_ada, pre_mix_norm, post_mix_norm, pre_ffn_norm, post_ffn_norm, w_in, hg_lb_logits, hg_norm, gdn_conv_w, gdn_a_log, gdn_dt_bias, gdn_norm, w_out, w_ff1, w_ff2, loss_target, m_w_ada, m_b_ada, m_pre_mix_norm, m_post_mix_norm, m_pre_ffn_norm, m_post_ffn_norm, m_w_in, m_hg_lb_logits, m_hg_norm, m_gdn_conv_w, m_gdn_a_log, m_gdn_dt_bias, m_gdn_norm, m_w_out, m_w_ff1, m_w_ff2, v_w_ada, v_b_ada, v_pre_mix_norm, v_post_mix_norm, v_pre_ffn_norm, v_post_ffn_norm, v_w_in, v_hg_lb_logits, v_hg_norm, v_gdn_conv_w, v_gdn_a_log, v_gdn_dt_bias, v_gdn_norm, v_w_out, v_w_ff1, v_w_ff2):
    t, d = x.shape[1], x.shape[2]
    nh = d // 2 // HD
    in_cols = w_in.shape[2] * N_DEV
    main = in_cols - 2 * nh
    me = _dev_index(lax.axis_index("x"), lax.axis_index("y"), lax.axis_index("c"))

    c_all, conv_g = _all_gather([c, gdn_conv_w[0]], "gather_small")
    c_all = c_all.reshape(N_DEV, d)
    conv_full = conv_g.transpose(1, 0, 2).reshape(CONV_K, -1)
    w_in_g = _all_gather([w_in[0].astype(BF16)], "gather_w_in")[0]
    w_in_full = w_in_g.transpose(1, 0, 2).reshape(d, in_cols)
    w_in_p = jnp.concatenate([w_in_full, jnp.zeros((d, LANES - 2 * nh), BF16)], axis=1)
    late_w = [w_out[0].astype(BF16), w_ff1[0].astype(BF16), w_ff2[0].astype(BF16)]

    n_in = w_in.shape[2]
    n_ff = w_ff1.shape[2]

    def halves(p):
        r = p.shape[1] // 2
        return (p, 0, r), (p, r, r)

    dist = dict(
        assemble=lambda g_out, g_ff1, g_ff2: (g_out.reshape(d, d), g_ff1.transpose(1, 0, 2).reshape(d, -1),
                                              g_ff2.reshape(-1, d)),
        n_ff=n_ff,
        parts_ff2=lambda dw: halves(dw.reshape(N_DEV, -1, d)),
        parts_ff1=halves,
        parts_out=lambda dw: dw.reshape(N_DEV, d // N_DEV, d),
        parts_in=lambda dw: dw[:, :in_cols].reshape(dw.shape[0], N_DEV, n_in).transpose(1, 0, 2),
    )

    n_ada = w_ada.shape[2]
    b_loc = lax.dynamic_slice(b_ada, (0, me * n_ada), (1, n_ada))
    mod_part = _ada_fwd(c_all, w_ada[0], b_loc, "ada_fwd")
    mod_all = _all_gather([mod_part], "gather_mod")[0]
    mod = lax.dynamic_slice(mod_all, (0, me, 0), (N_DEV, 1, n_ada)).reshape(6, d)

    pad_lane = lambda vv: jnp.concatenate([vv, jnp.zeros((1, LANES - vv.shape[1]), F32)], axis=1)
    loss, dx, g = _local_step(
        x[0], loss_target[0], mod, pre_mix_norm, post_mix_norm, pre_ffn_norm, post_ffn_norm, w_in_p,
        hg_lb_logits, hg_norm, conv_full, pad_lane(gdn_a_log), pad_lane(gdn_dt_bias), gdn_norm,
        late_w, dist)

    rep_names = ["b_ada", "n1", "n2", "n3", "n4", "lb", "hg_norm", "alog", "dtb", "gdn_norm"]
    rep_w = [b_ada, pre_mix_norm, post_mix_norm, pre_ffn_norm, post_ffn_norm, hg_lb_logits, hg_norm,
             gdn_a_log, gdn_dt_bias, gdn_norm]
    rep_m = [m_b_ada, m_pre_mix_norm, m_post_mix_norm, m_pre_ffn_norm, m_post_ffn_norm, m_hg_lb_logits,
             m_hg_norm, m_gdn_a_log, m_gdn_dt_bias, m_gdn_norm]
    rep_v = [v_b_ada, v_pre_mix_norm, v_post_mix_norm, v_pre_ffn_norm, v_post_ffn_norm, v_hg_lb_logits,
             v_hg_norm, v_gdn_a_log, v_gdn_dt_bias, v_gdn_norm]
    rep_shapes = [a.shape for a in rep_w]
    g_lb = jnp.stack([g["lb0"], -g["lb0"]], axis=0)
    rep_g = [g["dmod"], g["n1"], g["n2"], g["n3"], g["n4"], g_lb, g["hg_norm"],
             g["alog"][:, :nh], g["dtb"][:, :nh], g["gdn_norm"]]
    small = _pack(rep_g + [g["conv"]])
    n_rep_rows = _pack(rep_g).shape[0]
    pad_rows = (-small.shape[0]) % 8
    if pad_rows:
        small = jnp.concatenate([small, jnp.zeros((pad_rows, LANES), F32)], axis=0)
    small_all = _all_gather([small], "gather_small_grads")[0]
    small_sum = _sum_parts(small_all, "sum_small_grads")
    rep_out = _adamw(_pack(rep_w), _pack(rep_m), _pack(rep_v), small_sum[:n_rep_rows], "adamw_small")
    rep_g_o, rep_d_o, rep_m_o, rep_v_o = [dict(zip(rep_names, _unpack(p, rep_shapes))) for p in rep_out]

    conv_sum = small_sum[n_rep_rows:n_rep_rows + CONV_K * conv_full.shape[1] // LANES].reshape(CONV_K, -1)
    n_conv = gdn_conv_w.shape[2]
    conv_loc = lax.dynamic_slice(conv_sum, (0, me * n_conv), (CONV_K, n_conv))
    conv_o = _adamw(gdn_conv_w, m_gdn_conv_w, v_gdn_conv_w, conv_loc, "adamw_conv")

    dmod_all = small_all[:, :6 * d // LANES, :].reshape(N_DEV, 6 * d)
    dmod_loc = lax.dynamic_slice(dmod_all, (0, me * n_ada), (N_DEV, n_ada))
    g_ada = _ada_wgrad(c_all, dmod_loc, "ada_wgrad")
    ada_o = _adamw(w_ada, m_w_ada, v_w_ada, g_ada, "adamw_ada")

    rc = g["recv"]
    r_ff2 = [rc["ff2a"][0], rc["ff2b"][0]]
    r_ff1 = [rc["ff1a"][0], rc["ff1b_out"][0]]
    r_out, r_in = rc["ff1b_out"][1], rc["in"]
    in_o = _adamw(w_in, m_w_in, v_w_in, r_in, "adamw_w_in", parts=True)
    out_o = _adamw(w_out, m_w_out, v_w_out, r_out, "adamw_w_out", parts=True)
    ff1_o = _adamw(w_ff1, m_w_ff1, v_w_ff1, r_ff1, "adamw_w_ff1", parts=True)
    ff2_o = _adamw(w_ff2, m_w_ff2, v_w_ff2, r_ff2, "adamw_w_ff2", parts=True)

    loss_tot = lax.psum(loss[0, 0], ("x", "y", "c"))

    def leaf(kind):
        return [ada_o[kind], rep_out_d[kind]["b_ada"], rep_out_d[kind]["n1"], rep_out_d[kind]["n2"],
                rep_out_d[kind]["n3"], rep_out_d[kind]["n4"], in_o[kind], rep_out_d[kind]["lb"],
                rep_out_d[kind]["hg_norm"], conv_o[kind], rep_out_d[kind]["alog"], rep_out_d[kind]["dtb"],
                rep_out_d[kind]["gdn_norm"], out_o[kind], ff1_o[kind], ff2_o[kind]]

    rep_out_d = [rep_g_o, rep_d_o, rep_m_o, rep_v_o]
    return (loss_tot, dx[None], *leaf(0), *leaf(1), *leaf(2), *leaf(3))
```

```python
import functools

import jax
import jax.numpy as jnp
from jax import lax
from jax.experimental import pallas as pl
from jax.experimental.pallas import tpu as pltpu

F32 = jnp.float32
BF16 = jnp.bfloat16
HI = lax.Precision.HIGHEST
HIGH = lax.Precision.HIGH

EPS = 1e-6
CHUNK = 64
SB = 16
NSB = CHUNK // SB
HP = 8
HD = 128
CONV_K = 4
N_DEV = 8
LANES = 128
SUBLANES = 8
VMEM_LIMIT = 56 * 1024 * 1024
MM_FULL_K = 2048

ADAM_BLOCK_ELEMS = 256 * 1024
ADAM_LR = 0.001
ADAM_B1 = 0.9
ADAM_B2 = 0.999
ADAM_EPS = 1e-08
ADAM_WD = 0.01
ADAM_STEP = 10

ANY = pl.BlockSpec(memory_space=pl.ANY)
MESH = pl.DeviceIdType.MESH


def _cp(sem=None):
    return pltpu.CompilerParams(dimension_semantics=sem, vmem_limit_bytes=VMEM_LIMIT)


def _dot(a, b, dims, precision=None):
    return lax.dot_general(a, b, (dims, ((), ())), precision=precision, preferred_element_type=F32)


def _nn(a, b, precision=None):
    return _dot(a, b, ((1,), (0,)), precision)


def _nt(a, b, precision=None):
    return _dot(a, b, ((1,), (1,)), precision)


def _tn(a, b, precision=None):
    return _dot(a, b, ((0,), (0,)), precision)


def _bf(x):
    return x.astype(BF16)


def _sigmoid(x):
    return 1.0 / (1.0 + jnp.exp(-x))


def _interleave(gens):
    results = [None] * len(gens)
    live = list(range(len(gens)))
    while live:
        for i in list(live):
            try:
                next(gens[i])
            except StopIteration as stop:
                results[i] = stop.value
                live.remove(i)
    return results


def _listed(res):
    return list(res) if isinstance(res, (list, tuple)) else [res]


def _pick(n, pref):
    if n <= pref:
        return n
    t = pref
    while n % t:
        t -= LANES
    assert t > 0, (n, pref)
    return t


def _mm(a, b, mode, out_dtypes, name, epilogue=None, extras=(), tm=1024, tn=2048, tk=1024, comm=None,
        by_cols=False):
    if mode == "nn":
        (m, kd), (_, n) = a.shape, b.shape
    elif mode == "nt":
        (m, kd), (n, _) = a.shape, b.shape
    else:
        (kd, m), (_, n) = a.shape, b.shape
    if kd <= MM_FULL_K:
        tk = kd
    tm, tn, tk = _pick(m, tm), _pick(n, tn), _pick(kd, tk)
    nk = kd // tk
    if mode == "nn":
        a_spec = pl.BlockSpec((tm, tk), lambda i, j, k: (i, k))
        b_spec = pl.BlockSpec((tk, tn), lambda i, j, k: (k, j))
        dims = ((1,), (0,))
    elif mode == "nt":
        a_spec = pl.BlockSpec((tm, tk), lambda i, j, k: (i, k))
        b_spec = pl.BlockSpec((tn, tk), lambda i, j, k: (j, k))
        dims = ((1,), (1,))
    else:
        a_spec = pl.BlockSpec((tk, tm), lambda i, j, k: (k, i))
        b_spec = pl.BlockSpec((tk, tn), lambda i, j, k: (k, j))
        dims = ((0,), (0,))
    o_spec = pl.BlockSpec((tm, tn), lambda i, j, k: (i, j))
    if by_cols:
        assert epilogue is None and not extras
        res_spec = pl.BlockSpec((None, tm, tn), lambda i, j, k: (j, i, 0))
        res_shape = (n // tn, m, tn)
    else:
        res_spec, res_shape = o_spec, (m, n)
    n_extra, n_out = len(extras), len(out_dtypes)

    gm, gn = m // tm, n // tn
    cn = comm.n if comm is not None else 0

    def body(*refs):
        i, j, k = pl.program_id(0), pl.program_id(1), pl.program_id(2)
        at0 = (j == 0) & (k == 0)
        ins, out_refs, scratch, comm_begin, comm_end = _comm_hooks(
            comm, refs, 2 + n_extra, n_out, (i == 0) & at0, (i == gm - 1) & at0,
            (i == gm - 1) & (j == gn - 1) & (k == nk - 1))
        a_ref, b_ref, extra_refs = ins[0], ins[1], ins[2:]
        comm_begin()
        if nk == 1:
            part = _dot(a_ref[...], b_ref[...], dims)
            if epilogue is None:
                out_refs[0][...] = part.astype(out_dtypes[0])
            else:
                epilogue(part, extra_refs, out_refs)
        else:
            acc, = scratch

            @pl.when(k == 0)
            def _():
                acc[...] = jnp.zeros_like(acc)

            acc[...] += _dot(a_ref[...], b_ref[...], dims)

            @pl.when(k == nk - 1)
            def _():
                if epilogue is None:
                    out_refs[0][...] = acc[...].astype(out_dtypes[0])
                else:
                    epilogue(acc[...], extra_refs, out_refs)

        comm_end()

    acc_scratch = [] if nk == 1 else [pltpu.VMEM((tm, tn), F32)]
    sem = ("arbitrary",) * 3 if cn else ("parallel", "parallel", "arbitrary")
    outs = pl.pallas_call(
        body, name=name,
        grid=(gm, gn, nk),
        in_specs=[a_spec, b_spec] + [o_spec] * n_extra + [ANY] * cn,
        out_specs=[res_spec] * n_out + [ANY] * cn,
        out_shape=[jax.ShapeDtypeStruct(res_shape, dt) for dt in out_dtypes] + (comm.out_shapes() if cn else []),
        scratch_shapes=acc_scratch + (comm.scratch() if cn else []),
        compiler_params=_cp(sem),
    )(a, b, *extras, *(comm.arrays if cn else []))
    return outs[0] if n_out + cn == 1 else outs


def _row_spec(tb, d):
    return pl.BlockSpec((tb, d), lambda i: (i, 0))


def _vec_spec(d):
    return pl.BlockSpec((1, d), lambda i: (0, 0))


def _prenorm(x, w, sc, sh, name):
    t, d = x.shape
    tb = _pick(t, 256)

    def body(x_ref, w_ref, sc_ref, sh_ref, h_ref, r_ref):
        xv = x_ref[...]
        r = lax.rsqrt(jnp.mean(xv * xv, axis=-1, keepdims=True) + EPS)
        h_ref[...] = ((xv * r * w_ref[...]) * (1.0 + sc_ref[...]) + sh_ref[...]).astype(BF16)
        r_ref[...] = r

    return pl.pallas_call(
        body, name=name, grid=(t // tb,),
        in_specs=[_row_spec(tb, d), _vec_spec(d), _vec_spec(d), _vec_spec(d)],
        out_specs=[_row_spec(tb, d), _row_spec(tb, 1)],
        out_shape=[jax.ShapeDtypeStruct((t, d), BF16), jax.ShapeDtypeStruct((t, 1), F32)],
        compiler_params=_cp(("parallel",)),
    )(x, w, sc, sh)


def _final_loss_bwd(x, y, w, gt, tgt, name):
    t, d = x.shape
    tb = _pick(t, 256)

    def body(x_ref, y_ref, w_ref, gt_ref, tgt_ref, dout_ref, dy_ref, loss_ref, dgt_ref, dw_ref):
        @pl.when(pl.program_id(0) == 0)
        def _():
            loss_ref[...] = jnp.zeros_like(loss_ref)
            dgt_ref[...] = jnp.zeros_like(dgt_ref)
            dw_ref[...] = jnp.zeros_like(dw_ref)

        yv, wv, gtv = y_ref[...], w_ref[...], gt_ref[...]
        r = lax.rsqrt(jnp.mean(yv * yv, axis=-1, keepdims=True) + EPS)
        z = yv * r
        nz = z * wv
        diff = (x_ref[...] + gtv * nz) - tgt_ref[...]
        loss_ref[...] += 0.5 * jnp.sum(jnp.mean(diff * diff, axis=-1, keepdims=True), axis=0, keepdims=True)
        dxv = diff * (1.0 / d)
        dout_ref[...] = dxv
        dgt_ref[...] += jnp.sum(dxv * nz, axis=0, keepdims=True)
        dn = dxv * gtv
        dw_ref[...] += jnp.sum(dn * z, axis=0, keepdims=True)
        dz = dn * wv
        dy_ref[...] = (r * (dz - z * jnp.mean(dz * z, axis=-1, keepdims=True))).astype(BF16)

    return pl.pallas_call(
        body, name=name, grid=(t // tb,),
        in_specs=[_row_spec(tb, d), _row_spec(tb, d), _vec_spec(d), _vec_spec(d), _row_spec(tb, d)],
        out_specs=[_row_spec(tb, d), _row_spec(tb, d), pl.BlockSpec((1, 1), lambda i: (0, 0)),
                   _vec_spec(d), _vec_spec(d)],
        out_shape=[jax.ShapeDtypeStruct((t, d), F32), jax.ShapeDtypeStruct((t, d), BF16),
                   jax.ShapeDtypeStruct((1, 1), F32), jax.ShapeDtypeStruct((1, d), F32),
                   jax.ShapeDtypeStruct((1, d), F32)],
        compiler_params=_cp(("arbitrary",)),
    )(x, y, w, gt, tgt)


def _postnorm_prenorm(x, y, w_post, gt, w_pre, sc, sh, name):
    t, d = x.shape
    tb = _pick(t, 256)

    def body(x_ref, y_ref, wp_ref, gt_ref, wn_ref, sc_ref, sh_ref, x1_ref, r_ref, h_ref, r1_ref):
        yv = y_ref[...]
        r = lax.rsqrt(jnp.mean(yv * yv, axis=-1, keepdims=True) + EPS)
        x1 = x_ref[...] + gt_ref[...] * (yv * r * wp_ref[...])
        r1 = lax.rsqrt(jnp.mean(x1 * x1, axis=-1, keepdims=True) + EPS)
        x1_ref[...] = x1
        r_ref[...] = r
        h_ref[...] = ((x1 * r1 * wn_ref[...]) * (1.0 + sc_ref[...]) + sh_ref[...]).astype(BF16)
        r1_ref[...] = r1

    return pl.pallas_call(
        body, name=name, grid=(t // tb,),
        in_specs=[_row_spec(tb, d), _row_spec(tb, d)] + [_vec_spec(d)] * 5,
        out_specs=[_row_spec(tb, d), _row_spec(tb, 1), _row_spec(tb, d), _row_spec(tb, 1)],
        out_shape=[jax.ShapeDtypeStruct((t, d), F32), jax.ShapeDtypeStruct((t, 1), F32),
                   jax.ShapeDtypeStruct((t, d), BF16), jax.ShapeDtypeStruct((t, 1), F32)],
        compiler_params=_cp(("parallel",)),
    )(x, y, w_post, gt, w_pre, sc, sh)


def _prenorm_postnorm_bwd(dh, x, r_pre, w_pre, sc, dres, y, r_post, w_post, gt, name):
    t, d = x.shape
    tb = _pick(t, 256)

    def body(dh_ref, x_ref, rp_ref, wp_ref, sc_ref, dres_ref, y_ref, rq_ref, wq_ref, gt_ref,
             dx_ref, dy_ref, dsh_ref, dsc_ref, dwp_ref, dgt_ref, dwq_ref):
        @pl.when(pl.program_id(0) == 0)
        def _():
            for ref in (dsh_ref, dsc_ref, dwp_ref, dgt_ref, dwq_ref):
                ref[...] = jnp.zeros_like(ref)

        dhv, rv, wv = dh_ref[...], rp_ref[...], wp_ref[...]
        z = x_ref[...] * rv
        dsh_ref[...] += jnp.sum(dhv, axis=0, keepdims=True)
        dsc_ref[...] += jnp.sum(dhv * (z * wv), axis=0, keepdims=True)
        dzw = dhv * (1.0 + sc_ref[...])
        dwp_ref[...] += jnp.sum(dzw * z, axis=0, keepdims=True)
        dz = dzw * wv
        dxv = dres_ref[...] + rv * (dz - z * jnp.mean(dz * z, axis=-1, keepdims=True))
        dx_ref[...] = dxv

        rq, wq = rq_ref[...], wq_ref[...]
        zq = y_ref[...] * rq
        dgt_ref[...] += jnp.sum(dxv * (zq * wq), axis=0, keepdims=True)
        dn = dxv * gt_ref[...]
        dwq_ref[...] += jnp.sum(dn * zq, axis=0, keepdims=True)
        dzq = dn * wq
        dy_ref[...] = (rq * (dzq - zq * jnp.mean(dzq * zq, axis=-1, keepdims=True))).astype(BF16)

    rs, r1, vs = _row_spec(tb, d), _row_spec(tb, 1), _vec_spec(d)
    return pl.pallas_call(
        body, name=name, grid=(t // tb,),
        in_specs=[rs, rs, r1, vs, vs, rs, rs, r1, vs, vs],
        out_specs=[rs, rs] + [vs] * 5,
        out_shape=[jax.ShapeDtypeStruct((t, d), F32), jax.ShapeDtypeStruct((t, d), BF16)]
        + [jax.ShapeDtypeStruct((1, d), F32)] * 5,
        compiler_params=_cp(("arbitrary",)),
    )(dh, x, r_pre, w_pre, sc, dres, y, r_post, w_post, gt)


def _prenorm_bwd(dh, x, r, w, sc, dres, name):
    t, d = x.shape
    tb = _pick(t, 256)

    def body(dh_ref, x_ref, r_ref, w_ref, sc_ref, dres_ref, dx_ref, dsh_ref, dsc_ref, dw_ref):
        @pl.when(pl.program_id(0) == 0)
        def _():
            dsh_ref[...] = jnp.zeros_like(dsh_ref)
            dsc_ref[...] = jnp.zeros_like(dsc_ref)
            dw_ref[...] = jnp.zeros_like(dw_ref)

        dhv, rv, wv = dh_ref[...], r_ref[...], w_ref[...]
        z = x_ref[...] * rv
        dsh_ref[...] += jnp.sum(dhv, axis=0, keepdims=True)
        dsc_ref[...] += jnp.sum(dhv * (z * wv), axis=0, keepdims=True)
        dzw = dhv * (1.0 + sc_ref[...])
        dw_ref[...] += jnp.sum(dzw * z, axis=0, keepdims=True)
        dz = dzw * wv
        dx_ref[...] = dres_ref[...] + rv * (dz - z * jnp.mean(dz * z, axis=-1, keepdims=True))

    return pl.pallas_call(
        body, name=name, grid=(t // tb,),
        in_specs=[_row_spec(tb, d), _row_spec(tb, d), _row_spec(tb, 1), _vec_spec(d), _vec_spec(d),
                  _row_spec(tb, d)],
        out_specs=[_row_spec(tb, d), _vec_spec(d), _vec_spec(d), _vec_spec(d)],
        out_shape=[jax.ShapeDtypeStruct((t, d), F32)] + [jax.ShapeDtypeStruct((1, d), F32)] * 3,
        compiler_params=_cp(("arbitrary",)),
    )(dh, x, r, w, sc, dres)


def _headnorm_fwd(o, proj, g_blk, nw, name):
    t, wd = o.shape
    nh = wd // HD
    tb = _pick(t, 512)
    gb = g_blk * HD // wd

    def body(o_ref, g_ref, nw_ref, out_ref):
        o3 = o_ref[...].reshape(tb, nh, HD)
        g3 = g_ref[...].reshape(tb, nh, HD)
        rh = lax.rsqrt(jnp.mean(o3 * o3, axis=-1, keepdims=True) + EPS)
        res = (o3 * rh * nw_ref[...].reshape(1, 1, HD)) * (g3 * _sigmoid(g3))
        out_ref[...] = res.reshape(tb, wd).astype(BF16)

    return pl.pallas_call(
        body, name=name, grid=(t // tb,),
        in_specs=[_row_spec(tb, wd), pl.BlockSpec((tb, wd), lambda i: (i, gb)), _vec_spec(HD)],
        out_specs=_row_spec(tb, wd),
        out_shape=jax.ShapeDtypeStruct((t, wd), BF16),
        compiler_params=_cp(("parallel",)),
    )(o, proj, nw)


def _headnorm_bwd(dom, col_blk, o, proj, g_blk, nw, name):
    t, wd = o.shape
    nh = wd // HD
    tb = _pick(t, 512)
    gb = g_blk * HD // wd

    def body(do_ref, o_ref, g_ref, nw_ref, dout_ref, dg_ref, dnw_ref):
        @pl.when(pl.program_id(0) == 0)
        def _():
            dnw_ref[...] = jnp.zeros_like(dnw_ref)

        dn = do_ref[...].reshape(tb, nh, HD)
        o3 = o_ref[...].reshape(tb, nh, HD)
        g3 = g_ref[...].reshape(tb, nh, HD)
        nw3 = nw_ref[...].reshape(1, 1, HD)
        rh = lax.rsqrt(jnp.mean(o3 * o3, axis=-1, keepdims=True) + EPS)
        z = o3 * rh
        sg = _sigmoid(g3)
        sl = g3 * sg
        dnw_ref[...] += jnp.sum(jnp.sum(dn * sl * z, axis=1), axis=0, keepdims=True)
        dg_ref[...] = (dn * (z * nw3) * (sg * (1.0 + g3 * (1.0 - sg)))).reshape(tb, wd).astype(BF16)
        dz = dn * sl * nw3
        dout_ref[...] = (rh * (dz - z * jnp.mean(dz * z, axis=-1, keepdims=True))).reshape(tb, wd)

    return pl.pallas_call(
        body, name=name, grid=(t // tb,),
        in_specs=[pl.BlockSpec((tb, wd), lambda i: (i, col_blk)), _row_spec(tb, wd),
                  pl.BlockSpec((tb, wd), lambda i: (i, gb)), _vec_spec(HD)],
        out_specs=[_row_spec(tb, wd), _row_spec(tb, wd), _vec_spec(HD)],
        out_shape=[jax.ShapeDtypeStruct((t, wd), F32), jax.ShapeDtypeStruct((t, wd), BF16),
                   jax.ShapeDtypeStruct((1, HD), F32)],
        compiler_params=_cp(("arbitrary",)),
    )(dom, o, proj, nw)


def _tri(n, kind):
    r = lax.broadcasted_iota(jnp.int32, (n, n), 0)
    c = lax.broadcasted_iota(jnp.int32, (n, n), 1)
    if kind == "lower":
        return r >= c
    if kind == "strict":
        return r > c
    return r <= c


def _hg_gate(fl, l0, l1):
    mx = jnp.maximum(l0, l1)
    e0, e1 = jnp.exp(l0 - mx), jnp.exp(l1 - mx)
    lb = e0 / (e0 + e1)
    sg = _sigmoid(fl)
    f = lb + (1.0 - lb) * sg
    return lb, sg, f


def _hgrn2_fwd(proj, lb_logits, nh, name, comm=None):
    t = proj.shape[0]
    nc = t // CHUNK
    C = CHUNK
    lg = lb_logits.reshape(2, nh, 1, HD)

    hp = min(HP, nh)
    ng = nh // hp

    def one_head(hh, st, q_ref, f_ref, i_ref, lg_ref, p_sc, r_sc):
        sl = slice(hh * HD, (hh + 1) * HD)
        q, v = q_ref[:, sl], i_ref[:, sl]
        _, _, f = _hg_gate(f_ref[:, sl], lg_ref[0, hh], lg_ref[1, hh])
        k = 1.0 - f
        low = _tri(C, "lower")
        b = _nn(low.astype(F32), jnp.log(f), HI)
        yield
        lane_c = lax.broadcasted_iota(jnp.int32, (SB, C), 1)
        lane_h = lax.broadcasted_iota(jnp.int32, (SB, HD), 1)
        row_h = lax.broadcasted_iota(jnp.int32, (SB, HD), 0)
        ones = jnp.ones((HD, HD), F32)

        for i in range(NSB):
            qi, ki, bi = q[SB * i:SB * (i + 1)], k[SB * i:SB * (i + 1)], b[SB * i:SB * (i + 1)]
            for s in range(SB):
                e = jnp.exp(jnp.minimum(bi - bi[s:s + 1], 0.0))
                p = jnp.where(row_h >= s, qi * ki[s:s + 1] * e, 0.0)
                p_sc[hh, pl.ds((i * SB + s) * SB, SB), :] = p
            yield
        r_sc[hh] = _nn(p_sc[hh], ones, HIGH)
        yield
        a_rows = []
        for i in range(NSB):
            acc = jnp.zeros((SB, HD), F32)
            for s in range(SB):
                acc = jnp.where(lane_h == SB * i + s, r_sc[hh, pl.ds((i * SB + s) * SB, SB), :], acc)
            acc = acc[:, :C]
            if i > 0:
                r = b[SB * i - 1:SB * i]
                bi = b[SB * i:SB * (i + 1)]
                qf = q[SB * i:SB * (i + 1)] * jnp.exp(bi - r)
                kf = k * jnp.exp(jnp.minimum(r - b, 0.0))
                acc = acc + jnp.where(lane_c < SB * i, _nt(qf, kf, HIGH), 0.0)
            a_rows.append(acc)
            yield
        a = jnp.concatenate(a_rows, axis=0)
        bl = b[C - 1:C, :]
        o = _nn(_bf(a), _bf(v)) + _nt(_bf(q * jnp.exp(b)), _bf(st))
        yield
        new_st = st * jnp.exp(bl) + _tn(_bf(v), _bf(k * jnp.exp(bl - b)))
        return o, a, new_st

    def body(*refs):
        c, hg = pl.program_id(0), pl.program_id(1)
        step = c * ng + hg
        ins, outs, scratch, comm_begin, comm_end = _comm_hooks(
            comm, refs, 4, 3, step == 0, step == (3 * nc * ng) // 4, step == nc * ng - 1)
        o_ref, a_ref, st_ref = outs
        s_sc, p_sc, r_sc = scratch
        comm_begin()

        @pl.when(c == 0)
        def _():
            for hh in range(hp):
                s_sc[hg * hp + hh] = jnp.zeros((HD, HD), F32)

        sts = [s_sc[hg * hp + hh] for hh in range(hp)]
        res = _interleave([one_head(hh, sts[hh], *ins, p_sc, r_sc) for hh in range(hp)])
        for hh in range(hp):
            o_ref[:, hh * HD:(hh + 1) * HD] = res[hh][0]
            a_ref[0, hh] = res[hh][1]
            st_ref[0, hh] = sts[hh]
            s_sc[hg * hp + hh] = res[hh][2]
        comm_end()

    blk = lambda off: pl.BlockSpec((C, hp * HD), lambda c, g: (c, off // hp + g))
    cn = comm.n if comm is not None else 0
    return pl.pallas_call(
        body, name=name, grid=(nc, ng),
        in_specs=[blk(0), blk(nh), blk(2 * nh),
                  pl.BlockSpec((2, hp, 1, HD), lambda c, g: (0, g, 0, 0))] + [ANY] * cn,
        out_specs=[blk(0),
                   pl.BlockSpec((1, hp, C, C), lambda c, g: (c, g, 0, 0)),
                   pl.BlockSpec((1, hp, HD, HD), lambda c, g: (c, g, 0, 0))] + [ANY] * cn,
        out_shape=[jax.ShapeDtypeStruct((t, nh * HD), F32),
                   jax.ShapeDtypeStruct((nc, nh, C, C), F32),
                   jax.ShapeDtypeStruct((nc, nh, HD, HD), F32)] + (comm.out_shapes() if cn else []),
        scratch_shapes=[pltpu.VMEM((nh, HD, HD), F32), pltpu.VMEM((hp, C * SB, HD), F32),
                        pltpu.VMEM((hp, C * SB, HD), F32)] + (comm.scratch() if cn else []),
        compiler_params=_cp(("arbitrary", "arbitrary")),
    )(proj, proj, proj, lg, *(comm.arrays if cn else []))


def _hgrn2_bwd(proj, lb_logits, do, a_sv, st_sv, nh, name, comm=None):
    t = proj.shape[0]
    nc = t // CHUNK
    C = CHUNK
    lg = lb_logits.reshape(2, nh, 1, HD)
    hp = min(HP, nh)
    ng = nh // hp

    def one_head(hh, dst, q_ref, f_ref, i_ref, lg_ref, do_ref, a_ref, st_ref, p_sc, r_sc):
        sl = slice(hh * HD, (hh + 1) * HD)
        q, v, do_ = q_ref[:, sl], i_ref[:, sl], do_ref[:, sl]
        lb, sg, f = _hg_gate(f_ref[:, sl], lg_ref[0, hh], lg_ref[1, hh])
        k = 1.0 - f
        low = _tri(C, "lower")
        b = _nn(low.astype(F32), jnp.log(f), HI)
        yield
        bl = b[C - 1:C, :]
        eb, ekb = jnp.exp(b), jnp.exp(bl - b)
        qb, kb = q * eb, k * ekb
        a, st = a_ref[0, hh], st_ref[0, hh]

        da = jnp.where(low, _nt(_bf(do_), _bf(v)), 0.0)
        yield
        dv = _tn(_bf(a), _bf(do_)) + _nt(_bf(kb), _bf(dst))
        yield
        dqb = _nn(_bf(do_), _bf(st))
        dkb = _nn(_bf(v), _bf(dst))
        yield

        row = lax.broadcasted_iota(jnp.int32, (C, HD), 0)
        lane_c = lax.broadcasted_iota(jnp.int32, (SB, C), 1)
        row_h = lax.broadcasted_iota(jnp.int32, (SB, HD), 0)
        ones = jnp.ones((HD, HD), F32)
        sel = (lax.broadcasted_iota(jnp.int32, (C, C * SB), 0)
               == jnp.right_shift(lax.broadcasted_iota(jnp.int32, (C, C * SB), 1), SB.bit_length() - 1)).astype(F32)

        for i in range(NSB):
            doi, vi = do_[SB * i:SB * (i + 1)], v[SB * i:SB * (i + 1)]
            for s in range(SB):
                p_sc[hh, pl.ds((i * SB + s) * SB, SB), :] = doi * vi[s:s + 1]
            yield
        r_sc[hh] = _nn(p_sc[hh], ones, HIGH)
        yield
        dq_rows = []
        dk_off = jnp.zeros((C, HD), F32)
        for i in range(NSB):
            qi, ki, bi = q[SB * i:SB * (i + 1)], k[SB * i:SB * (i + 1)], b[SB * i:SB * (i + 1)]
            acc = jnp.zeros((SB, HD), F32)
            for s in range(SB):
                e = jnp.exp(jnp.minimum(bi - bi[s:s + 1], 0.0))
                g = jnp.where(row_h >= s, r_sc[hh, pl.ds((i * SB + s) * SB, SB), :] * e, 0.0)
                acc = acc + g * ki[s:s + 1]
                p_sc[hh, pl.ds((i * SB + s) * SB, SB), :] = g * qi
            yield
            if i > 0:
                r = b[SB * i - 1:SB * i]
                fq = jnp.exp(bi - r)
                fk = jnp.exp(jnp.minimum(r - b, 0.0))
                dai = jnp.where(lane_c < SB * i, da[SB * i:SB * (i + 1)], 0.0)
                acc = acc + _nn(dai, k * fk, HIGH) * fq
                dk_off = dk_off + _tn(dai, qi * fq, HIGH) * fk
                yield
            dq_rows.append(acc)
        dqi = jnp.concatenate(dq_rows, axis=0)
        dq = dqi + dqb * eb
        dk_inter = dkb * ekb
        dk = _nn(sel, p_sc[hh], HIGH) + dk_off + dk_inter
        yield
        db = q * dq - k * dk
        extra = (jnp.sum(k * dk_inter, axis=0, keepdims=True)
                 + jnp.exp(bl) * jnp.sum(dst * st, axis=0, keepdims=True))
        db = db + jnp.where(row == C - 1, extra, 0.0)
        dlf = _nn(_tri(C, "upper").astype(F32), db, HI)
        yield
        df = dlf / f - dk
        dfl = (df * (1.0 - lb) * sg * (1.0 - sg)).astype(BF16)
        dl = jnp.sum(df * (1.0 - sg), axis=0, keepdims=True) * (lb * (1.0 - lb))
        new_dst = dst * jnp.exp(bl) + _tn(_bf(do_), _bf(qb))
        return dq.astype(BF16), dfl, dv.astype(BF16), dl, new_dst

    def body(*refs):
        c, hg = pl.program_id(0), pl.program_id(1)
        step = c * ng + hg
        ins, outs, scratch, comm_begin, comm_end = _comm_hooks(
            comm, refs, 7, 4, step == 0, step == (3 * nc * ng) // 4, step == nc * ng - 1)
        dq_ref, df_ref, di_ref, dl_ref = outs
        ds_sc, p_sc, r_sc = scratch
        comm_begin()

        @pl.when(c == 0)
        def _():
            for hh in range(hp):
                ds_sc[hg * hp + hh] = jnp.zeros((HD, HD), F32)

        @pl.when(step == 0)
        def _():
            dl_ref[...] = jnp.zeros_like(dl_ref)

        dsts = [ds_sc[hg * hp + hh] for hh in range(hp)]
        res = _interleave([one_head(hh, dsts[hh], *ins, p_sc, r_sc) for hh in range(hp)])
        for hh in range(hp):
            sl = slice(hh * HD, (hh + 1) * HD)
            dq_ref[:, sl], df_ref[:, sl], di_ref[:, sl] = res[hh][0], res[hh][1], res[hh][2]
            dl_ref[pl.ds(hg * hp + hh, 1), :] += res[hh][3]
            ds_sc[hg * hp + hh] = res[hh][4]
        comm_end()

    rblk = lambda off: pl.BlockSpec((C, hp * HD), lambda c, g: (nc - 1 - c, off // hp + g))
    oblk = pl.BlockSpec((C, hp * HD), lambda c, g: (nc - 1 - c, g))
    cn = comm.n if comm is not None else 0
    return pl.pallas_call(
        body, name=name, grid=(nc, ng),
        in_specs=[rblk(0), rblk(nh), rblk(2 * nh),
                  pl.BlockSpec((2, hp, 1, HD), lambda c, g: (0, g, 0, 0)),
                  oblk,
                  pl.BlockSpec((1, hp, C, C), lambda c, g: (nc - 1 - c, g, 0, 0)),
                  pl.BlockSpec((1, hp, HD, HD), lambda c, g: (nc - 1 - c, g, 0, 0))] + [ANY] * cn,
        out_specs=[oblk, oblk, oblk, pl.BlockSpec((nh, HD), lambda c, g: (0, 0))] + [ANY] * cn,
        out_shape=[jax.ShapeDtypeStruct((t, nh * HD), BF16)] * 3 + [jax.ShapeDtypeStruct((nh, HD), F32)]
        + (comm.out_shapes() if cn else []),
        scratch_shapes=[pltpu.VMEM((nh, HD, HD), F32), pltpu.VMEM((hp, C * SB, HD), F32),
                        pltpu.VMEM((hp, C * SB, HD), F32)] + (comm.scratch() if cn else []),
        compiler_params=_cp(("arbitrary", "arbitrary")),
    )(proj, proj, proj, lg, do, a_sv, st_sv, *(comm.arrays if cn else []))


def _shift_rows(u, d, row):
    t = u.shape[0]
    if d == 0:
        return u
    rolled = pltpu.roll(u, d % t, 0)
    if d > 0:
        return jnp.where(row >= d, rolled, 0.0)
    return jnp.where(row < t + d, rolled, 0.0)


def _gdn_prep(proj, conv_w, blk0, nh, name):
    t = proj.shape[0]
    scale = HD ** -0.5

    def body(u_ref, w_ref, o_ref):
        j = pl.program_id(0)
        u, w = u_ref[...], w_ref[...]
        row = lax.broadcasted_iota(jnp.int32, (t, HD), 0)
        y = w[CONV_K - 1:CONV_K, :] * u
        for d in range(1, CONV_K):
            y = y + w[CONV_K - 1 - d:CONV_K - d, :] * _shift_rows(u, d, row)
        a = y * _sigmoid(y)
        n = a * lax.rsqrt(jnp.sum(a * a, axis=-1, keepdims=True) + EPS)
        n = n * jnp.where(j < nh, scale, 1.0)
        o_ref[...] = jnp.where(j < 2 * nh, n, a)

    return pl.pallas_call(
        body, name=name, grid=(3 * nh,),
        in_specs=[pl.BlockSpec((t, HD), lambda j: (0, blk0 + j)), pl.BlockSpec((CONV_K, HD), lambda j: (0, j))],
        out_specs=pl.BlockSpec((t, HD), lambda j: (0, j)),
        out_shape=jax.ShapeDtypeStruct((t, 3 * nh * HD), F32),
        compiler_params=_cp(("parallel",)),
    )(proj, conv_w)


def _gdn_prep_bwd(proj, conv_w, dq, dk, dv, blk0, nh, name):
    t = proj.shape[0]
    scale = HD ** -0.5

    def body(u_ref, w_ref, dq_ref, dk_ref, dv_ref, du_ref, dw_ref):
        j = pl.program_id(0)
        u, w = u_ref[...], w_ref[...]
        dout = jnp.where(j < nh, dq_ref[...], jnp.where(j < 2 * nh, dk_ref[...], dv_ref[...]))
        row = lax.broadcasted_iota(jnp.int32, (t, HD), 0)
        us = [_shift_rows(u, d, row) for d in range(CONV_K)]
        y = w[CONV_K - 1:CONV_K, :] * us[0]
        for d in range(1, CONV_K):
            y = y + w[CONV_K - 1 - d:CONV_K - d, :] * us[d]
        sg = _sigmoid(y)
        a = y * sg
        rs = lax.rsqrt(jnp.sum(a * a, axis=-1, keepdims=True) + EPS)
        n = a * rs
        dn = dout * jnp.where(j < nh, scale, 1.0)
        da_n = rs * (dn - n * jnp.sum(dn * n, axis=-1, keepdims=True))
        da = jnp.where(j < 2 * nh, da_n, dout)
        dy = da * (sg * (1.0 + y * (1.0 - sg)))
        du = w[CONV_K - 1:CONV_K, :] * dy
        for d in range(1, CONV_K):
            du = du + w[CONV_K - 1 - d:CONV_K - d, :] * _shift_rows(dy, -d, row)
        du_ref[...] = du.astype(BF16)
        for d in range(CONV_K):
            dw_ref[CONV_K - 1 - d:CONV_K - d, :] = jnp.sum(dy * us[d], axis=0, keepdims=True)

    return pl.pallas_call(
        body, name=name, grid=(3 * nh,),
        in_specs=[pl.BlockSpec((t, HD), lambda j: (0, blk0 + j)), pl.BlockSpec((CONV_K, HD), lambda j: (0, j))]
        + [pl.BlockSpec((t, HD), functools.partial(lambda p, j: (0, jnp.clip(j - p * nh, 0, nh - 1)), p))
           for p in range(3)],
        out_specs=[pl.BlockSpec((t, HD), lambda j: (0, j)), pl.BlockSpec((CONV_K, HD), lambda j: (0, j))],
        out_shape=[jax.ShapeDtypeStruct((t, 3 * nh * HD), BF16), jax.ShapeDtypeStruct((CONV_K, 3 * nh * HD), F32)],
        compiler_params=_cp(("arbitrary",)),
    )(proj, conv_w, dq, dk, dv)


def _gdn_gates(ab, alog, dtb, h, nh):
    lane = lax.broadcasted_iota(jnp.int32, ab.shape, 1)
    x = ab + dtb
    sp = jnp.maximum(x, 0.0) + jnp.log(1.0 + jnp.exp(-jnp.abs(x)))
    ea = jnp.exp(alog)
    la_all = -ea * sp
    beta_all = _sigmoid(ab)
    pick = lambda val, ln: jnp.sum(jnp.where(lane == ln, val, 0.0), axis=1, keepdims=True)
    la = pick(la_all, h)
    beta = pick(beta_all, nh + h)
    dla_da = pick(-ea * _sigmoid(x), h)
    return la, beta, dla_da


def _unit_lower_inverses(ms, C):
    nb = C // SB
    sh = SB.bit_length() - 1
    rowb = jnp.right_shift(lax.broadcasted_iota(jnp.int32, (C, C), 0), sh)
    colb = jnp.right_shift(lax.broadcasted_iota(jnp.int32, (C, C), 1), sh)
    eye = (lax.broadcasted_iota(jnp.int32, (SB, SB), 0) == lax.broadcasted_iota(jnp.int32, (SB, SB), 1)).astype(F32)
    spread = (jnp.bitwise_and(lax.broadcasted_iota(jnp.int32, (SB, C), 1), SB - 1)
              == lax.broadcasted_iota(jnp.int32, (SB, C), 0)).astype(F32)
    blocks = [[m[SB * i:SB * (i + 1), SB * i:SB * (i + 1)] for i in range(nb)] for m in ms]
    xs = [[eye] * nb for _ in ms]
    for s in range(SB - 1):
        xs = [[x - b[:, s:s + 1] * x[s:s + 1, :] for x, b in zip(xh, bh)] for xh, bh in zip(xs, blocks)]
    ts = [jnp.where(rowb == colb, _nn(jnp.concatenate(xh, axis=0), spread, HIGH), 0.0) for xh in xs]
    lvl = 1
    while (1 << lvl) <= nb:
        off = ((jnp.right_shift(rowb, lvl) == jnp.right_shift(colb, lvl))
               & (jnp.right_shift(rowb, lvl - 1) != jnp.right_shift(colb, lvl - 1)))
        ts = [t - _nn(t, _nn(jnp.where(off, m, 0.0), t, HIGH), HIGH) for t, m in zip(ts, ms)]
        lvl += 1
    return ts


def _gdn_chunks(qs, ks, vs, las, betas, C):
    low, strict = _tri(C, "lower"), _tri(C, "strict")
    eye = (lax.broadcasted_iota(jnp.int32, (C, C), 0) == lax.broadcasted_iota(jnp.int32, (C, C), 1)).astype(F32)
    g_bs = [_nn(low.astype(F32), jnp.broadcast_to(la, (C, HD)), HI) for la in las]
    ps = [_nt(k, k, HIGH) for k in ks]
    qks = [_nt(_bf(q), _bf(k)) for q, k in zip(qs, ks)]
    chs = []
    for g_b, p, qk_raw, beta in zip(g_bs, ps, qks, betas):
        g_c = g_b[:, :C]
        gamma = jnp.where(low, jnp.exp(jnp.minimum(g_c - g_c.T, 0.0)), 0.0)
        gl = g_b[C - 1:C, :]
        chs.append(dict(gamma=gamma, eg=jnp.exp(g_b), gl=gl, ekt=jnp.exp(gl - g_b), p=p,
                        m=jnp.where(strict, beta * p * gamma, 0.0), qk_raw=qk_raw))
    xs = _unit_lower_inverses([ch["m"] for ch in chs], C)
    r_ws = [k * (beta * ch["eg"]) for ch, k, beta in zip(chs, ks, betas)]
    uws = [_nn(x, jnp.concatenate([v * beta, r_w], axis=1), HIGH) for x, v, beta, r_w in zip(xs, vs, betas, r_ws)]
    for ch, x, r_w, uw in zip(chs, xs, r_ws, uws):
        ch.update(x=x, r_w=r_w, uw=uw)
    return chs


def _gdn_fwd(qkv, proj, ab_blk, alog, dtb, nh, name, comm=None):
    t = qkv.shape[0]
    nc = t // CHUNK
    C = CHUNK
    hp = min(HP, nh)
    ng = nh // hp

    def body(*refs):
        c, hg = pl.program_id(0), pl.program_id(1)
        step = c * ng + hg
        ins, outs, scratch, comm_begin, comm_end = _comm_hooks(
            comm, refs, 6, 3, step == 0, step == (3 * nc * ng) // 4, step == nc * ng - 1)
        q_ref, k_ref, v_ref, ab_ref, al_ref, dt_ref = ins
        o_ref, x_ref, st_ref = outs
        s_sc, = scratch
        comm_begin()

        @pl.when(c == 0)
        def _():
            for hh in range(hp):
                s_sc[hg * hp + hh] = jnp.zeros((HD, HD), F32)

        sls = [slice(hh * HD, (hh + 1) * HD) for hh in range(hp)]
        qs, ks, vs = [q_ref[:, sl] for sl in sls], [k_ref[:, sl] for sl in sls], [v_ref[:, sl] for sl in sls]
        sts = [s_sc[hg * hp + hh] for hh in range(hp)]
        gates = [_gdn_gates(ab_ref[...], al_ref[...], dt_ref[...], hg * hp + hh, nh) for hh in range(hp)]
        chs = _gdn_chunks(qs, ks, vs, [g[0] for g in gates], [g[1] for g in gates], C)
        stbs = [_bf(st) for st in sts]
        vns = [ch["uw"][:, :HD] - _nt(_bf(ch["uw"][:, HD:]), stb) for ch, stb in zip(chs, stbs)]
        o_st = [_nt(_bf(q * ch["eg"]), stb) for q, ch, stb in zip(qs, chs, stbs)]
        outs_ = [o + _nn(_bf(ch["qk_raw"] * ch["gamma"]), _bf(vn)) for o, ch, vn in zip(o_st, chs, vns)]
        new_sts = [st * jnp.exp(ch["gl"]) + _tn(_bf(vn), _bf(k * ch["ekt"]))
                   for st, ch, vn, k in zip(sts, chs, vns, ks)]
        for hh in range(hp):
            o_ref[:, sls[hh]] = outs_[hh]
            x_ref[0, hh] = chs[hh]["x"]
            st_ref[0, hh] = sts[hh]
            s_sc[hg * hp + hh] = new_sts[hh]
        comm_end()

    blk = lambda off: pl.BlockSpec((C, hp * HD), lambda c, g: (c, off // hp + g))
    vec = pl.BlockSpec((1, HD), lambda c, g: (0, 0))
    cn = comm.n if comm is not None else 0
    return pl.pallas_call(
        body, name=name, grid=(nc, ng),
        in_specs=[blk(0), blk(nh), blk(2 * nh), pl.BlockSpec((C, HD), lambda c, g: (c, ab_blk)), vec, vec]
        + [ANY] * cn,
        out_specs=[blk(0),
                   pl.BlockSpec((1, hp,C, C), lambda c, g: (c, g, 0, 0)),
                   pl.BlockSpec((1, hp,HD, HD), lambda c, g: (c, g, 0, 0))] + [ANY] * cn,
        out_shape=[jax.ShapeDtypeStruct((t, nh * HD), F32),
                   jax.ShapeDtypeStruct((nc, nh, C, C), F32),
                   jax.ShapeDtypeStruct((nc, nh, HD, HD), F32)] + (comm.out_shapes() if cn else []),
        scratch_shapes=[pltpu.VMEM((nh, HD, HD), F32)] + (comm.scratch() if cn else []),
        compiler_params=_cp(("arbitrary", "arbitrary")),
    )(qkv, qkv, qkv, proj, alog, dtb, *(comm.arrays if cn else []))


def _gdn_bwd(qkv, proj, ab_blk, alog, dtb, do, x_sv, st_sv, nh, name, comm=None):
    t = qkv.shape[0]
    nc = t // CHUNK
    C = CHUNK
    hp = min(HP, nh)
    ng = nh // hp

    def one_head(h, hh, dst, q_ref, k_ref, v_ref, ab_ref, al_ref, dt_ref, do_ref, x_ref, st_ref):
        sl = slice(hh * HD, (hh + 1) * HD)
        q, k, v, do_ = q_ref[:, sl], k_ref[:, sl], v_ref[:, sl], do_ref[:, sl]
        la, beta, dla_da = _gdn_gates(ab_ref[...], al_ref[...], dt_ref[...], h, nh)
        low, strict = _tri(C, "lower"), _tri(C, "strict")
        g_b = _nn(low.astype(F32), jnp.broadcast_to(la, (C, HD)), HI)
        yield
        g_c = g_b[:, :C]
        gamma = jnp.where(low, jnp.exp(jnp.minimum(g_c - g_c.T, 0.0)), 0.0)
        eg = jnp.exp(g_b)
        gl = g_b[C - 1:C, :]
        ekt = jnp.exp(gl - g_b)
        egl = jnp.exp(gl)
        p = _nt(k, k, HIGH)
        yield
        x = x_ref[0, hh]
        r_w = k * (beta * eg)
        rhs = jnp.concatenate([v * beta, r_w], axis=1)
        uw = _nn(x, rhs, HIGH)
        yield
        u, w = uw[:, :HD], uw[:, HD:]
        qk_raw = _nt(_bf(q), _bf(k))
        yield
        qk = qk_raw * gamma
        st = st_ref[0, hh]
        stb, dstb = _bf(st), _bf(dst)
        vn = u - _nt(_bf(w), stb)
        yield
        qd, kt = q * eg, k * ekt

        dvn = _tn(_bf(qk), _bf(do_)) + _nt(_bf(kt), dstb)
        yield
        dq2 = jnp.where(low, _nt(_bf(do_), _bf(vn)), 0.0)
        yield
        dqd = _nn(_bf(do_), stb)
        yield
        dkt = _nn(_bf(vn), dstb)
        yield
        dw = -_nn(_bf(dvn), stb)
        yield
        dxx = jnp.concatenate([dvn, dw], axis=1)
        dr = _tn(x, dxx, HIGH)
        yield
        dm = -jnp.where(strict, _nt(dr, uw, HIGH), 0.0)
        yield
        dr_u, dr_w = dr[:, :HD], dr[:, HD:]
        rsum = lambda z: jnp.sum(z, axis=1, keepdims=True)

        dv = dr_u * beta
        dmg = dm * gamma
        dbeta = rsum(dr_u * v) + rsum(dr_w * k) * eg[:, :1] + rsum(dmg * p)
        yield
        dp = dmg * beta
        dq2g = dq2 * gamma
        dk = (dr_w * (beta * eg) + dkt * ekt + _tn(_bf(dq2g), _bf(q))
              + _nn(_bf(dp + dp.T), _bf(k)))
        yield
        dq = dqd * eg + _nn(_bf(dq2g), _bf(k))
        yield
        e = dp * p + dq2g * qk_raw
        t_kt = rsum(dkt * kt)
        dg = rsum(dqd * qd) + rsum(dr_w * r_w) - t_kt + rsum(e) - rsum(e.T)
        yield
        dgl = jnp.sum(t_kt, axis=0, keepdims=True) + jnp.sum(dst * st, keepdims=True) * egl[:, :1]
        rowc = lax.broadcasted_iota(jnp.int32, (C, 1), 0)
        dg = dg + jnp.where(rowc == C - 1, dgl, 0.0)
        dla = _nn(_tri(C, "upper").astype(F32), jnp.broadcast_to(dg, (C, HD)), HI)[:, :1]
        yield
        da = dla * dla_da
        db = dbeta * beta * (1.0 - beta)
        lane = lax.broadcasted_iota(jnp.int32, (C, HD), 1)
        dab = jnp.where(lane == h, da, 0.0) + jnp.where(lane == nh + h, db, 0.0)
        lane1 = lax.broadcasted_iota(jnp.int32, (1, HD), 1)
        d_alog = jnp.where(lane1 == h, jnp.sum(dla * la, axis=0, keepdims=True), 0.0)
        d_dtb = jnp.where(lane1 == h, jnp.sum(da, axis=0, keepdims=True), 0.0)
        new_dst = dst * egl + _tn(_bf(do_), _bf(qd)) - _tn(_bf(dvn), _bf(w))
        return dab, d_alog, d_dtb, new_dst, dq, dk, dv

    def body(*refs):
        c, hg = pl.program_id(0), pl.program_id(1)
        step = c * ng + hg
        ins, outs, scratch, comm_begin, comm_end = _comm_hooks(
            comm, refs, 9, 5, step == 0, step == (3 * nc * ng) // 4, step == nc * ng - 1)
        dq_ref, dk_ref, dv_ref, dab_ref, dpar_ref = outs
        ds_sc, = scratch
        comm_begin()

        @pl.when(c == 0)
        def _():
            for hh in range(hp):
                ds_sc[hg * hp + hh] = jnp.zeros((HD, HD), F32)

        @pl.when(step == 0)
        def _():
            dpar_ref[...] = jnp.zeros_like(dpar_ref)

        @pl.when(hg == 0)
        def _():
            dab_ref[...] = jnp.zeros_like(dab_ref)

        dsts = [ds_sc[hg * hp + hh] for hh in range(hp)]
        res = _interleave([one_head(hg * hp + hh, hh, dsts[hh], *ins) for hh in range(hp)])
        for hh in range(hp):
            sl = slice(hh * HD, (hh + 1) * HD)
            ds_sc[hg * hp + hh] = res[hh][3]
            dq_ref[:, sl], dk_ref[:, sl], dv_ref[:, sl] = res[hh][4], res[hh][5], res[hh][6]
        dab_ref[...] += sum(r[0] for r in res[1:]) + res[0][0]
        dpar_ref[0:1, :] += sum(r[1] for r in res[1:]) + res[0][1]
        dpar_ref[1:2, :] += sum(r[2] for r in res[1:]) + res[0][2]
        comm_end()

    rblk = lambda off: pl.BlockSpec((C, hp * HD), lambda c, g: (nc - 1 - c, off // hp + g))
    oblk = pl.BlockSpec((C, hp * HD), lambda c, g: (nc - 1 - c, g))
    vec = pl.BlockSpec((1, HD), lambda c, g: (0, 0))
    cn = comm.n if comm is not None else 0
    return pl.pallas_call(
        body, name=name, grid=(nc, ng),
        in_specs=[rblk(0), rblk(nh), rblk(2 * nh),
                  pl.BlockSpec((C, HD), lambda c, g: (nc - 1 - c, ab_blk)), vec, vec, oblk,
                  pl.BlockSpec((1, hp,C, C), lambda c, g: (nc - 1 - c, g, 0, 0)),
                  pl.BlockSpec((1, hp,HD, HD), lambda c, g: (nc - 1 - c, g, 0, 0))] + [ANY] * cn,
        out_specs=[oblk, oblk, oblk,
                   pl.BlockSpec((C, HD), lambda c, g: (nc - 1 - c, 0)),
                   pl.BlockSpec((8, HD), lambda c, g: (0, 0))] + [ANY] * cn,
        out_shape=[jax.ShapeDtypeStruct((t, nh * HD), F32)] * 3
        + [jax.ShapeDtypeStruct((t, HD), F32), jax.ShapeDtypeStruct((8, HD), F32)]
        + (comm.out_shapes() if cn else []),
        scratch_shapes=[pltpu.VMEM((nh, HD, HD), F32)] + (comm.scratch() if cn else []),
        compiler_params=_cp(("arbitrary", "arbitrary")),
    )(qkv, qkv, qkv, proj, alog, dtb, do, x_sv, st_sv, *(comm.arrays if cn else []))


def _ada_fwd(c_all, w, b, name):
    nb, d = c_all.shape
    n = w.shape[1]
    tn = _pick(n, 512)

    def body(c_ref, w_ref, b_ref, o_ref):
        cv = c_ref[...]
        o_ref[...] = _nn(cv * _sigmoid(cv), w_ref[...], HI) + b_ref[...]

    return pl.pallas_call(
        body, name=name, grid=(n // tn,),
        in_specs=[pl.BlockSpec((nb, d), lambda j: (0, 0)), pl.BlockSpec((d, tn), lambda j: (0, j)),
                  pl.BlockSpec((1, tn), lambda j: (0, j))],
        out_specs=pl.BlockSpec((nb, tn), lambda j: (0, j)),
        out_shape=jax.ShapeDtypeStruct((nb, n), F32),
        compiler_params=_cp(("parallel",)),
    )(c_all, w, b)


def _ada_wgrad(c_all, dmod, name):
    nb, d = c_all.shape
    n = dmod.shape[1]
    tn = _pick(n, 512)

    def body(c_ref, g_ref, o_ref):
        cv = c_ref[...]
        o_ref[...] = _tn(cv * _sigmoid(cv), g_ref[...], HI)

    return pl.pallas_call(
        body, name=name, grid=(n // tn,),
        in_specs=[pl.BlockSpec((nb, d), lambda j: (0, 0)), pl.BlockSpec((nb, tn), lambda j: (0, j))],
        out_specs=pl.BlockSpec((d, tn), lambda j: (0, j)),
        out_shape=jax.ShapeDtypeStruct((d, n), F32),
        compiler_params=_cp(("parallel",)),
    )(c_all, dmod)


def _adamw(w, m, v, g, name, parts=False):
    lead = w.ndim == 3
    r, cdim = w.shape[-2:]
    cap = max(SUBLANES, ADAM_BLOCK_ELEMS // cdim // SUBLANES * SUBLANES)
    tr = r if r <= cap else _pick_rows(r, cap)
    bc1 = 1.0 - ADAM_B1 ** ADAM_STEP
    bc2 = 1.0 - ADAM_B2 ** ADAM_STEP

    glist = list(g) if isinstance(g, (list, tuple)) else [g]
    bounds = [0]
    for ga in glist:
        bounds.append(bounds[-1] + ga.shape[-2] // tr)

    def body(w_ref, m_ref, v_ref, *rest):
        g_refs, (go_ref, d_ref, mo_ref, vo_ref) = rest[:len(glist)], rest[len(glist):]
        if parts:
            sums = []
            for g_ref in g_refs:
                gv = g_ref[0].astype(F32)
                for s in range(1, N_DEV):
                    gv = gv + g_ref[s].astype(F32)
                sums.append(gv)
            gv = sums[-1]
            for p in range(len(sums) - 2, -1, -1):
                gv = jnp.where(pl.program_id(0) < bounds[p + 1], sums[p], gv)
        else:
            gv = g_refs[0][...]
        wv = w_ref[...]
        mn = ADAM_B1 * m_ref[...] + (1.0 - ADAM_B1) * gv
        vn = ADAM_B2 * v_ref[...] + (1.0 - ADAM_B2) * (gv * gv)
        m_hat = mn / bc1
        v_hat = vn / bc2
        go_ref[...] = gv
        d_ref[...] = -ADAM_LR * (m_hat / (jnp.sqrt(v_hat) + ADAM_EPS) + ADAM_WD * wv)
        mo_ref[...] = mn
        vo_ref[...] = vn

    flat = pl.BlockSpec((tr, cdim), lambda i: (i, 0))
    spec = pl.BlockSpec((None, tr, cdim), lambda i: (0, i, 0)) if lead else flat
    def piece_spec(p):
        lo, n = bounds[p], bounds[p + 1] - bounds[p]
        return pl.BlockSpec((N_DEV, tr, cdim), lambda i: (0, jnp.clip(i - lo, 0, n - 1), 0))

    gspecs = [piece_spec(p) for p in range(len(glist))] if parts else [flat]
    return pl.pallas_call(
        body, name=name, grid=(r // tr,),
        in_specs=[spec, spec, spec] + gspecs,
        out_specs=[spec] * 4,
        out_shape=[jax.ShapeDtypeStruct(w.shape, F32)] * 4,
        compiler_params=_cp(("arbitrary",)),
    )(w, m, v, *glist)


def _pick_rows(r, pref):
    t = pref
    while r % t:
        t -= 8
    assert t > 0
    return t


def _dev_index(x, y, c):
    return 4 * x + 2 * y + c


class _Comm:
    def __init__(self, kind, arrays):
        self.kind, self.n = kind, len(arrays)
        self.arrays = [a[0] if isinstance(a, tuple) else a for a in arrays]
        self.rows = [(a[1], a[2]) if isinstance(a, tuple) else None for a in arrays]

    def out_shapes(self):
        if self.kind == "gather":
            return [jax.ShapeDtypeStruct((N_DEV,) + a.shape, a.dtype) for a in self.arrays]
        return [jax.ShapeDtypeStruct(a.shape if r is None else (N_DEV, r[1]) + a.shape[2:], a.dtype)
                for a, r in zip(self.arrays, self.rows)]

    def scratch(self):
        return [pltpu.SemaphoreType.DMA((self.n, 7)), pltpu.SemaphoreType.DMA((self.n, 7)),
                pltpu.SemaphoreType.DMA((self.n,))]

    def _gather_parts(self, ins, outs, sems):
        send_sems, recv_sems, local_sems = sems
        x, y, c = lax.axis_index("x"), lax.axis_index("y"), lax.axis_index("c")
        me, sibling = (x, y, c), (x, y, 1 - c)
        chips = [(1 - x, y), (x, 1 - y), (1 - x, 1 - y)]

        def copy(a, k, block, to, src=None):
            slot = outs[a].at[_dev_index(*block)]
            return pltpu.make_async_remote_copy(
                src_ref=slot if src is None else src, dst_ref=slot,
                send_sem=send_sems.at[a, k], recv_sem=recv_sems.at[a, k],
                device_id=to, device_id_type=MESH)

        n = self.n
        mine = [pltpu.make_async_copy(ins[a], outs[a].at[_dev_index(*me)], local_sems.at[a]) for a in range(n)]
        first = []
        for a in range(n):
            first.append(copy(a, 0, me, sibling, src=ins[a]))
            first += [copy(a, 1 + j, me, (*chip, c), src=ins[a]) for j, chip in enumerate(chips)]
        landed = [copy(a, 1 + j, (*chip, c), me) for j, chip in enumerate(chips) for a in range(n)]
        passed = [copy(a, 4 + j, (*chip, c), sibling) for j, chip in enumerate(chips) for a in range(n)]
        late = []
        for a in range(n):
            late.append(copy(a, 0, sibling, me))
            late += [copy(a, 4 + j, (*chip, 1 - c), me) for j, chip in enumerate(chips)]
        return mine, first, landed, passed, late

    def _exchange_parts(self, ins, outs, sems):
        send_sems, recv_sems, local_sems = sems
        x, y, c = lax.axis_index("x"), lax.axis_index("y"), lax.axis_index("c")
        my = _dev_index(x, y, c)
        n = self.n

        def block(a, j):
            r = self.rows[a]
            return ins[a].at[j] if r is None else ins[a].at[j, pl.ds(r[0], r[1])]

        mine = [pltpu.make_async_copy(block(a, my), outs[a].at[my], local_sems.at[a]) for a in range(n)]
        sends, recvs = [], []
        for k in range(1, N_DEV):
            px = (1 - x) if (k >> 2) & 1 else x
            py = (1 - y) if (k >> 1) & 1 else y
            pc = (1 - c) if k & 1 else c
            peer = _dev_index(px, py, pc)
            for a in range(n):
                sends.append(pltpu.make_async_remote_copy(
                    src_ref=block(a, peer), dst_ref=outs[a].at[my],
                    send_sem=send_sems.at[a, k - 1], recv_sem=recv_sems.at[a, k - 1],
                    device_id=(px, py, pc), device_id_type=MESH))
                recvs.append(pltpu.make_async_remote_copy(
                    src_ref=block(a, my), dst_ref=outs[a].at[peer],
                    send_sem=send_sems.at[a, k - 1], recv_sem=recv_sems.at[a, k - 1],
                    device_id=(x, y, c), device_id_type=MESH))
        return mine, sends, recvs

    def start(self, ins, outs, sems):
        if self.kind == "gather":
            mine, first, _, _, _ = self._gather_parts(ins, outs, sems)
        else:
            mine, first, _ = self._exchange_parts(ins, outs, sems)
        for cp in mine + first:
            cp.start()

    def mid(self, ins, outs, sems):
        if self.kind == "gather":
            _, _, landed, passed, _ = self._gather_parts(ins, outs, sems)
            for got, fwd in zip(landed, passed):
                got.wait_recv()
                fwd.start()

    def finish(self, ins, outs, sems):
        if self.kind == "gather":
            mine, first, _, passed, late = self._gather_parts(ins, outs, sems)
            for cp in late:
                cp.wait_recv()
            for cp in first + passed:
                cp.wait_send()
        else:
            mine, sends, recvs = self._exchange_parts(ins, outs, sems)
            for cp in sends:
                cp.wait_send()
            for cp in recvs:
                cp.wait_recv()
        for cp in mine:
            cp.wait()

    def run(self, name):
        n = self.n

        def body(*refs):
            ins, outs, sems = refs[:n], refs[n:2 * n], refs[2 * n:]
            self.start(ins, outs, sems)
            self.mid(ins, outs, sems)
            self.finish(ins, outs, sems)

        return pl.pallas_call(
            body, name=name, in_specs=[ANY] * n, out_specs=[ANY] * n,
            out_shape=self.out_shapes(), scratch_shapes=self.scratch(),
        )(*self.arrays)


def _all_gather(arrays, name):
    return _Comm("gather", arrays).run(name)


def _comm_hooks(comm, refs, n_in, n_out, first, middle, last):
    cn = comm.n if comm is not None else 0
    ins, cins = refs[:n_in], refs[n_in:n_in + cn]
    outs, couts = refs[n_in + cn:n_in + cn + n_out], refs[n_in + cn + n_out:n_in + 2 * cn + n_out]
    rest = refs[n_in + 2 * cn + n_out:]
    scratch, csems = (rest[:len(rest) - 3], rest[len(rest) - 3:]) if cn else (rest, ())

    def begin():
        if cn:
            pl.when(first)(lambda: comm.start(cins, couts, csems))
            pl.when(middle)(lambda: comm.mid(cins, couts, csems))

    def end():
        if cn:
            pl.when(last)(lambda: comm.finish(cins, couts, csems))

    return ins, outs, scratch, begin, end


def _local_step(x, tgt, mod, n1, n2, n3, n4, w_in_p, lb_logits, hg_norm, conv_w, alog, dtb, gdn_norm,
                late_w, dist=None):
    t, d = x.shape
    nh = d // 2 // HD
    ab_blk = 8 * nh
    sh_m, sc_m, gt_m, sh_f, sc_f, gt_f = [mod[i:i + 1] for i in range(6)]

    h1, r1 = _prenorm(x, n1, sc_m, sh_m, "prenorm_mix")
    if dist is None:
        proj = _mm(h1, w_in_p, "nn", [F32], "mm_proj")
        o_hg, a_sv, hst_sv = _hgrn2_fwd(proj, lb_logits, nh, "hgrn2_fwd")
        qkv = _gdn_prep(proj, conv_w, 4 * nh, nh, "gdn_prep")
        o_gd, x_sv, gst_sv = _gdn_fwd(qkv, proj, ab_blk, alog, dtb, nh, "gdn_fwd")
        w_out, w_ff1, w_ff2 = late_w
        exch = lambda arrays: None
    else:
        proj, g_out = _mm(h1, w_in_p, "nn", [F32], "mm_proj", comm=_Comm("gather", late_w[:1]))
        o_hg, a_sv, hst_sv, g_ff2 = _hgrn2_fwd(proj, lb_logits, nh, "hgrn2_fwd",
                                               comm=_Comm("gather", late_w[2:]))
        qkv = _gdn_prep(proj, conv_w, 4 * nh, nh, "gdn_prep")
        o_gd, x_sv, gst_sv, g_ff1 = _gdn_fwd(qkv, proj, ab_blk, alog, dtb, nh, "gdn_fwd",
                                             comm=_Comm("gather", late_w[1:2]))
        w_out, w_ff1, w_ff2 = dist["assemble"](g_out, g_ff1, g_ff2)
        exch = lambda arrays: _Comm("exchange", arrays)
    om_hg = _headnorm_fwd(o_hg, proj, 3 * nh, hg_norm, "headnorm_hg")
    om_gd = _headnorm_fwd(o_gd, proj, 7 * nh, gdn_norm, "headnorm_gdn")
    om = jnp.concatenate([om_hg, om_gd], axis=1)
    y1 = _mm(om, w_out, "nn", [F32], "mm_out")
    x1, r2, h2, r3 = _postnorm_prenorm(x, y1, n2, gt_m, n3, sc_f, sh_f, "postnorm_mix_prenorm_ffn")

    def relu2(acc, extra, outs):
        rl = jnp.maximum(acc, 0.0)
        outs[0][...] = (rl * rl).astype(BF16)

    act = _mm(h2, w_ff1, "nn", [BF16], "mm_ff1", epilogue=relu2)
    y2 = _mm(act, w_ff2, "nn", [F32], "mm_ff2")
    dout, dy2, loss, dgt_f, dn4 = _final_loss_bwd(x1, y2, n4, gt_f, tgt, "final_loss_bwd")
    whole_t = dict(tk=t, tn=1024)
    dw_ff2 = _mm(act, dy2, "tn", [BF16], "mm_dw_ff2", **whole_t)

    def drelu2(acc, extra, outs):
        outs[0][...] = (acc * (2.0 * jnp.sqrt(extra[0][...].astype(F32)))).astype(BF16)

    recv = {}
    ff2a, ff2b = dist["parts_ff2"](dw_ff2) if dist else (None, None)
    du, *recv["ff2a"] = _listed(_mm(dy2, w_ff2, "nt", [BF16], "mm_da", epilogue=drelu2, extras=(act,),
                                    comm=exch([ff2a])))
    ff1_cols = dict(by_cols=True, tk=t, tn=dist["n_ff"]) if dist else whole_t
    dw_ff1, *recv["ff2b"] = _listed(_mm(h2, du, "tn", [BF16], "mm_dw_ff1", comm=exch([ff2b]), **ff1_cols))
    ff1a, ff1b = dist["parts_ff1"](dw_ff1) if dist else (None, None)
    dh2, *recv["ff1a"] = _listed(_mm(du, w_ff1, "nt", [F32], "mm_dh2", comm=exch([ff1a])))
    dx1, dy1, dsh_f, dsc_f, dn3, dgt_m, dn2 = _prenorm_postnorm_bwd(
        dh2, x1, r3, n3, sc_f, dout, y1, r2, n2, gt_m, "prenorm_ffn_postnorm_mix_bwd")

    dw_out = _mm(om, dy1, "tn", [BF16], "mm_dw_out", **whole_t)
    dom = _mm(dy1, w_out, "nt", [F32], "mm_dom")
    do_hg, dg_hg, dhgn = _headnorm_bwd(dom, 0, o_hg, proj, 3 * nh, hg_norm, "headnorm_hg_bwd")
    do_gd, dg_gd, dgdn = _headnorm_bwd(dom, 1, o_gd, proj, 7 * nh, gdn_norm, "headnorm_gdn_bwd")
    p_out = dist["parts_out"](dw_out) if dist else None
    dq_hg, df_hg, di_hg, dl0, *recv["ff1b_out"] = _hgrn2_bwd(proj, lb_logits, do_hg, a_sv, hst_sv, nh,
                                                             "hgrn2_bwd", comm=exch([ff1b, p_out]))
    dq_g, dk_g, dv_g, dab, dpar = _gdn_bwd(qkv, proj, ab_blk, alog, dtb, do_gd, x_sv, gst_sv, nh, "gdn_bwd")
    du_conv, dconv = _gdn_prep_bwd(proj, conv_w, dq_g, dk_g, dv_g, 4 * nh, nh, "gdn_prep_bwd")
    dproj = jnp.concatenate([dq_hg, df_hg, di_hg, dg_hg, du_conv, dg_gd, dab.astype(BF16)], axis=1)
    if dist is None:
        dw_in = _mm(h1, dproj, "tn", [BF16], "mm_dw_in")
        dh1 = _mm(dproj, w_in_p, "nt", [F32], "mm_dh1", tk=1664)
    else:
        q4 = d // 4
        dw_in_a = _mm(h1[:, :q4], dproj, "tn", [BF16], "mm_dw_in_a")
        dw_in_b, in_a = _mm(h1[:, q4:2 * q4], dproj, "tn", [BF16], "mm_dw_in_b",
                            comm=exch([dist["parts_in"](dw_in_a)]))
        dw_in_c, in_b = _mm(h1[:, 2 * q4:], dproj, "tn", [BF16], "mm_dw_in_c",
                            comm=exch([dist["parts_in"](dw_in_b)]))
        dh1, in_c = _mm(dproj, w_in_p, "nt", [F32], "mm_dh1", tk=1664, comm=exch([dist["parts_in"](dw_in_c)]))
        recv["in"] = [in_a, in_b, in_c]
        dw_in = None
    dx, dsh_m, dsc_m, dn1 = _prenorm_bwd(dh1, x, r1, n1, sc_m, dx1, "prenorm_mix_bwd")

    dmod = jnp.concatenate([dsh_m, dsc_m, dgt_m, dsh_f, dsc_f, dgt_f], axis=0)
    grads = dict(dmod=dmod, n1=dn1, n2=dn2, n3=dn3, n4=dn4, w_in=dw_in, lb0=dl0, hg_norm=dhgn, conv=dconv,
                 alog=dpar[0:1], dtb=dpar[1:2], gdn_norm=dgdn, w_out=dw_out, w_ff1=dw_ff1, w_ff2=dw_ff2,
                 recv=recv)
    return loss, dx, grads


def _pack(vals):
    rows = []
    for vv in vals:
        flat = vv.reshape(-1)
        flat = jnp.pad(flat, (0, (-flat.shape[0]) % (SUBLANES * LANES)))
        rows.append(flat.reshape(-1, LANES))
    return jnp.concatenate(rows, axis=0)


def _unpack(packed, shapes):
    out, r = [], 0
    for shp in shapes:
        size = 1
        for s in shp:
            size *= s
        nr = -(-size // (SUBLANES * LANES)) * SUBLANES
        out.append(packed[r:r + nr].reshape(-1)[:size].reshape(shp))
        r += nr
    return out


def _sum_parts(parts, name):
    _, r, cdim = parts.shape

    def body(p_ref, o_ref):
        acc = p_ref[0]
        for s in range(1, N_DEV):
            acc = acc + p_ref[s]
        o_ref[...] = acc

    return pl.pallas_call(
        body, name=name,
        out_shape=jax.ShapeDtypeStruct((r, cdim), F32),
        compiler_params=_cp(),
    )(parts)


def kernel(x, c, w_ada, b_ada, pre_mix_norm, post_mix_norm, pre_ffn_norm, post_ffn_norm, w_in, hg_lb_logits, hg_norm, gdn_conv_w, gdn_a_log, gdn_dt_bias, gdn_norm, w_out, w_ff1, w_ff2, loss_target, m_w_ada, m_b_ada, m_pre_mix_norm, m_post_mix_norm, m_pre_ffn_norm, m_post_ffn_norm, m_w_in, m_hg_lb_logits, m_hg_norm, m_gdn_conv_w, m_gdn_a_log, m_gdn_dt_bias, m_gdn_norm, m_w_out, m_w_ff1, m_w_ff2, v_w_ada, v_b_ada, v_pre_mix_norm, v_post_mix_norm, v_pre_ffn_norm, v_post_ffn_norm, v_w_in, v_hg_lb_logits, v_hg_norm, v_gdn_conv_w, v_gdn_a_log, v_gdn_dt_bias, v_gdn_norm, v_w_out, v_w_ff1, v_w_ff2):
    t, d = x.shape[1], x.shape[2]
    nh = d // 2 // HD
    in_cols = w_in.shape[2] * N_DEV
    main = in_cols - 2 * nh
    me = _dev_index(lax.axis_index("x"), lax.axis_index("y"), lax.axis_index("c"))

    c_all, conv_g = _all_gather([c, gdn_conv_w[0]], "gather_small")
    c_all = c_all.reshape(N_DEV, d)
    conv_full = conv_g.transpose(1, 0, 2).reshape(CONV_K, -1)
    w_in_b16 = w_in[0].astype(BF16)
    w_in_g = jnp.concatenate(_all_gather([w_in_b16[:d // 2], w_in_b16[d // 2:]], "gather_w_in"), axis=1)
    w_in_full = w_in_g.transpose(1, 0, 2).reshape(d, in_cols)
    w_in_p = jnp.concatenate([w_in_full, jnp.zeros((d, LANES - 2 * nh), BF16)], axis=1)
    late_w = [w_out[0].astype(BF16), w_ff1[0].astype(BF16), w_ff2[0].astype(BF16)]

    n_in = w_in.shape[2]
    n_ff = w_ff1.shape[2]

    def halves(p):
        r = p.shape[1] // 2
        return (p, 0, r), (p, r, r)

    dist = dict(
        assemble=lambda g_out, g_ff1, g_ff2: (g_out.reshape(d, d), g_ff1.transpose(1, 0, 2).reshape(d, -1),
                                              g_ff2.reshape(-1, d)),
        n_ff=n_ff,
        parts_ff2=lambda dw: halves(dw.reshape(N_DEV, -1, d)),
        parts_ff1=halves,
        parts_out=lambda dw: dw.reshape(N_DEV, d // N_DEV, d),
        parts_in=lambda dw: dw[:, :in_cols].reshape(dw.shape[0], N_DEV, n_in).transpose(1, 0, 2),
    )

    n_ada = w_ada.shape[2]
    b_loc = lax.dynamic_slice(b_ada, (0, me * n_ada), (1, n_ada))
    mod_part = _ada_fwd(c_all, w_ada[0], b_loc, "ada_fwd")
    mod_all = _all_gather([mod_part], "gather_mod")[0]
    mod = lax.dynamic_slice(mod_all, (0, me, 0), (N_DEV, 1, n_ada)).reshape(6, d)

    pad_lane = lambda vv: jnp.concatenate([vv, jnp.zeros((1, LANES - vv.shape[1]), F32)], axis=1)
    loss, dx, g = _local_step(
        x[0], loss_target[0], mod, pre_mix_norm, post_mix_norm, pre_ffn_norm, post_ffn_norm, w_in_p,
        hg_lb_logits, hg_norm, conv_full, pad_lane(gdn_a_log), pad_lane(gdn_dt_bias), gdn_norm,
        late_w, dist)

    rep_names = ["b_ada", "n1", "n2", "n3", "n4", "lb", "hg_norm", "alog", "dtb", "gdn_norm"]
    rep_w = [b_ada, pre_mix_norm, post_mix_norm, pre_ffn_norm, post_ffn_norm, hg_lb_logits, hg_norm,
             gdn_a_log, gdn_dt_bias, gdn_norm]
    rep_m = [m_b_ada, m_pre_mix_norm, m_post_mix_norm, m_pre_ffn_norm, m_post_ffn_norm, m_hg_lb_logits,
             m_hg_norm, m_gdn_a_log, m_gdn_dt_bias, m_gdn_norm]
    rep_v = [v_b_ada, v_pre_mix_norm, v_post_mix_norm, v_pre_ffn_norm, v_post_ffn_norm, v_hg_lb_logits,
             v_hg_norm, v_gdn_a_log, v_gdn_dt_bias, v_gdn_norm]
    rep_shapes = [a.shape for a in rep_w]
    g_lb = jnp.stack([g["lb0"], -g["lb0"]], axis=0)
    rep_g = [g["dmod"], g["n1"], g["n2"], g["n3"], g["n4"], g_lb, g["hg_norm"],
             g["alog"][:, :nh], g["dtb"][:, :nh], g["gdn_norm"]]
    small = _pack(rep_g + [g["conv"]])
    n_rep_rows = _pack(rep_g).shape[0]
    pad_rows = (-small.shape[0]) % 8
    if pad_rows:
        small = jnp.concatenate([small, jnp.zeros((pad_rows, LANES), F32)], axis=0)
    small_all = _all_gather([small], "gather_small_grads")[0]
    small_sum = _sum_parts(small_all, "sum_small_grads")
    rep_out = _adamw(_pack(rep_w), _pack(rep_m), _pack(rep_v), small_sum[:n_rep_rows], "adamw_small")
    rep_g_o, rep_d_o, rep_m_o, rep_v_o = [dict(zip(rep_names, _unpack(p, rep_shapes))) for p in rep_out]

    conv_sum = small_sum[n_rep_rows:n_rep_rows + CONV_K * conv_full.shape[1] // LANES].reshape(CONV_K, -1)
    n_conv = gdn_conv_w.shape[2]
    conv_loc = lax.dynamic_slice(conv_sum, (0, me * n_conv), (CONV_K, n_conv))
    conv_o = _adamw(gdn_conv_w, m_gdn_conv_w, v_gdn_conv_w, conv_loc, "adamw_conv")

    dmod_all = small_all[:, :6 * d // LANES, :].reshape(N_DEV, 6 * d)
    dmod_loc = lax.dynamic_slice(dmod_all, (0, me * n_ada), (N_DEV, n_ada))
    g_ada = _ada_wgrad(c_all, dmod_loc, "ada_wgrad")
    ada_o = _adamw(w_ada, m_w_ada, v_w_ada, g_ada, "adamw_ada")

    rc = g["recv"]
    r_ff2 = [rc["ff2a"][0], rc["ff2b"][0]]
    r_ff1 = [rc["ff1a"][0], rc["ff1b_out"][0]]
    r_out, r_in = rc["ff1b_out"][1], rc["in"]
    in_o = _adamw(w_in, m_w_in, v_w_in, r_in, "adamw_w_in", parts=True)
    out_o = _adamw(w_out, m_w_out, v_w_out, r_out, "adamw_w_out", parts=True)
    ff1_o = _adamw(w_ff1, m_w_ff1, v_w_ff1, r_ff1, "adamw_w_ff1", parts=True)
    ff2_o = _adamw(w_ff2, m_w_ff2, v_w_ff2, r_ff2, "adamw_w_ff2", parts=True)

    loss_tot = lax.psum(loss[0, 0], ("x", "y", "c"))

    def leaf(kind):
        return [ada_o[kind], rep_out_d[kind]["b_ada"], rep_out_d[kind]["n1"], rep_out_d[kind]["n2"],
                rep_out_d[kind]["n3"], rep_out_d[kind]["n4"], in_o[kind], rep_out_d[kind]["lb"],
                rep_out_d[kind]["hg_norm"], conv_o[kind], rep_out_d[kind]["alog"], rep_out_d[kind]["dtb"],
                rep_out_d[kind]["gdn_norm"], out_o[kind], ff1_o[kind], ff2_o[kind]]

    rep_out_d = [rep_g_o, rep_d_o, rep_m_o, rep_v_o]
    return (loss_tot, dx[None], *leaf(0), *leaf(1), *leaf(2), *leaf(3))
```
